```python
import jax, jax.numpy as jnp
from jax import lax
import numpy as np

D_MODEL = 1024
BATCH = 4
SEQ = 4096
DEPTH = 2

GRID_W = 64
CTX_LEN = 256
EPS = 1e-6
NEG_INF = -1e30

HEAD_DIM = 64
N_Q_HEADS = (D_MODEL // 2) // HEAD_DIM
N_KV_HEADS = N_Q_HEADS // 4
Q_GROUP = N_Q_HEADS // N_KV_HEADS
WINDOW = 128
BLOCK = 128
ROPE_BASE = 10000.0
ATT_WIDTH = N_Q_HEADS * HEAD_DIM
KV_WIDTH = N_KV_HEADS * HEAD_DIM
GM_WIDTH = D_MODEL // 2
GM_GROUPS = 8
GM_GROUP_DIM = GM_WIDTH // GM_GROUPS
CHUNK = 128
Q_END = ATT_WIDTH
KV_END = Q_END + 2 * KV_WIDTH
IN_WIDTH = KV_END + 2 * GM_WIDTH
MIX_WIDTH = ATT_WIDTH + GM_WIDTH
POOL_SIZES = (2, 4, 8, 16)
POOL_GROUPS = len(POOL_SIZES)
POOL_GROUP_DIM = D_MODEL // POOL_GROUPS
D_FF_DENSE = 256 * ((8 * D_MODEL // 3 + 255) // 256)
N_EXPERTS = 8
TOP_K = 2
D_FF_EXPERT = 7 * D_MODEL // 2
N_EVEN = (DEPTH + 1) // 2
N_ODD = DEPTH // 2

kernel_name = "hybrid_window_gmlp_pool_moe_dit"


def _rms_normalize(x):
    xf = x.astype(jnp.float32)
    y = xf * lax.rsqrt(jnp.mean(xf * xf, axis=-1, keepdims=True) + EPS)
    return y.astype(x.dtype)


def _modulate(x, shift, scale):
    return _rms_normalize(x) * (1 + scale) + shift


def _axial_rope_tables(n_tokens):
    rows = n_tokens // GRID_W
    row_pos = jnp.repeat(jnp.arange(rows, dtype=jnp.float32), GRID_W)
    col_pos = jnp.tile(jnp.arange(GRID_W, dtype=jnp.float32), rows)
    axis_dim = HEAD_DIM // 2
    inv_freq = ROPE_BASE ** (-jnp.arange(0, axis_dim, 2, dtype=jnp.float32) / axis_dim)
    ang = jnp.stack([row_pos[:, None] * inv_freq, col_pos[:, None] * inv_freq], axis=1)
    return jnp.cos(ang), jnp.sin(ang)


def _apply_axial_rope(t, cos, sin):
    shp = t.shape
    tr = t.reshape(shp[:-1] + (2, 2, HEAD_DIM // 4))
    t1, t2 = tr[..., 0, :], tr[..., 1, :]
    cs = cos[:, None].astype(t.dtype)
    sn = sin[:, None].astype(t.dtype)
    out = jnp.stack([t1 * cs - t2 * sn, t2 * cs + t1 * sn], axis=-2)
    return out.reshape(shp)


def _softmax_with_sink(scores, sink_l):
    m = jnp.maximum(jnp.max(scores, axis=-1, keepdims=True), sink_l)
    p = jnp.exp(scores - m)
    return p / (jnp.sum(p, axis=-1, keepdims=True) + jnp.exp(sink_l - m))


def _window_attention_with_context(q, k, v, kc, vc, sink):
    B, S = q.shape[0], q.shape[1]
    nb = S // BLOCK
    qb = q.reshape(B, nb, BLOCK, N_KV_HEADS, Q_GROUP, HEAD_DIM) * (HEAD_DIM ** -0.5)

    def band(t):
        tp = jnp.pad(t, ((0, 0), (BLOCK, BLOCK), (0, 0), (0, 0)))
        tp = tp.reshape(B, nb + 2, BLOCK, N_KV_HEADS, HEAD_DIM)
        return jnp.concatenate([tp[:, :-2], tp[:, 1:-1], tp[:, 2:]], axis=2)

    kb, vb = band(k), band(v)
    qpos = jnp.arange(nb)[:, None, None] * BLOCK + jnp.arange(BLOCK)[None, :, None]
    kpos = (jnp.arange(nb)[:, None, None] - 1) * BLOCK + jnp.arange(3 * BLOCK)[None, None, :]
    mask = (jnp.abs(kpos - qpos) <= WINDOW) & (kpos >= 0) & (kpos < S)
    s_loc = jnp.einsum('bnqhgd,bnkhd->bnhgqk', qb, kb).astype(jnp.float32)
    s_loc = jnp.where(mask[None, :, None, None], s_loc, NEG_INF)
    s_ctx = jnp.einsum('bnqhgd,bchd->bnhgqc', qb, kc).astype(jnp.float32)
    scores = jnp.concatenate([s_loc, s_ctx], axis=-1)
    sink_l = sink.astype(jnp.float32).reshape(1, 1, N_KV_HEADS, Q_GROUP, 1, 1)
    p = _softmax_with_sink(scores, sink_l).astype(v.dtype)
    out = (jnp.einsum('bnhgqk,bnkhd->bnqhgd', p[..., :3 * BLOCK], vb)
           + jnp.einsum('bnhgqc,bchd->bnqhgd', p[..., 3 * BLOCK:], vc))
    return out.reshape(B, S, ATT_WIDTH)


def _context_attention(qc, kc, vc, sink):
    B, L = qc.shape[0], qc.shape[1]
    qg = qc.reshape(B, L, N_KV_HEADS, Q_GROUP, HEAD_DIM) * (HEAD_DIM ** -0.5)
    s = jnp.einsum('blhgd,bchd->bhglc', qg, kc).astype(jnp.float32)
    sink_l = sink.astype(jnp.float32).reshape(1, N_KV_HEADS, Q_GROUP, 1, 1)
    p = _softmax_with_sink(s, sink_l).astype(vc.dtype)
    return jnp.einsum('bhglc,bchd->blhgd', p, vc).reshape(B, L, ATT_WIDTH)


def _spatial_gating(u, v, gain, w_s, b_s):
    B, S = u.shape[0], u.shape[1]
    v = _rms_normalize(v) * gain
    vg = v.reshape(B, S // CHUNK, CHUNK, GM_GROUPS, GM_GROUP_DIM)
    mixed = jnp.einsum('gts,bnsgd->bntgd', w_s, vg) + b_s.T[:, :, None]
    return u * mixed.reshape(B, S, GM_WIDTH)


def _attn_gmlp_mixer(hx, hc, cos, sin, w_in, sink, gm_gain, w_s, b_s, w_out, with_ctx_out):
    B, S = hx.shape[0], hx.shape[1]
    proj = hx @ w_in
    q = proj[..., :Q_END].reshape(B, S, N_Q_HEADS, HEAD_DIM)
    k = proj[..., Q_END:Q_END + KV_WIDTH].reshape(B, S, N_KV_HEADS, HEAD_DIM)
    v = proj[..., Q_END + KV_WIDTH:KV_END].reshape(B, S, N_KV_HEADS, HEAD_DIM)
    u_gm, v_gm = jnp.split(jax.nn.gelu(proj[..., KV_END:]), 2, axis=-1)
    q = _apply_axial_rope(q, cos, sin)
    k = _apply_axial_rope(k, cos, sin)
    L = hc.shape[1]
    kv_c = hc @ w_in[:, Q_END:KV_END]
    kc = kv_c[..., :KV_WIDTH].reshape(B, L, N_KV_HEADS, HEAD_DIM)
    vc = kv_c[..., KV_WIDTH:].reshape(B, L, N_KV_HEADS, HEAD_DIM)
    att = _window_attention_with_context(q, k, v, kc, vc, sink)
    gm = _spatial_gating(u_gm, v_gm, gm_gain, w_s, b_s)
    out_x = jnp.concatenate([att, gm], axis=-1) @ w_out
    if with_ctx_out:
        qc = (hc @ w_in[:, :Q_END]).reshape(B, L, N_Q_HEADS, HEAD_DIM)
        uc, vgc = jnp.split(jax.nn.gelu(hc @ w_in[:, KV_END:]), 2, axis=-1)
        att_c = _context_attention(qc, kc, vc, sink)
        gm_c = _spatial_gating(uc, vgc, gm_gain, w_s, b_s)
        return out_x, jnp.concatenate([att_c, gm_c], axis=-1) @ w_out
    return out_x, None


def _multiscale_pool(h, w_pool, scale):
    B, S, D = h.shape
    hg = h.reshape(B, S, POOL_GROUPS, POOL_GROUP_DIM)
    t = jnp.arange(S)
    outs = []
    for gi, w in enumerate(POOL_SIZES):
        seg = hg[:, :, gi].astype(jnp.float32)
        cs = jnp.pad(jnp.cumsum(seg, axis=1), ((0, 0), (1, 0), (0, 0)))
        lo = jnp.maximum(t - w // 2, 0)
        hi = jnp.minimum(t + (w - 1 - w // 2), S - 1) + 1
        mean = (cs[:, hi] - cs[:, lo]) / (hi - lo).astype(jnp.float32)[None, :, None]
        outs.append((mean - seg).astype(h.dtype))
    diff = jnp.stack(outs, axis=2)
    y = jnp.einsum('bsgc,gcd->bsgd', diff, w_pool).reshape(B, S, D)
    return y * scale


def _swiglu(h, w_gate, w_up, w_down):
    return (jax.nn.silu(h @ w_gate) * (h @ w_up)) @ w_down


def _moe_swiglu(h, w_router, w_gate, w_up, w_down):
    B, S, D = h.shape
    tok = h.reshape(-1, D)
    n = tok.shape[0]
    logits = (tok @ w_router).astype(jnp.float32)
    top_val, top_idx = lax.top_k(logits, TOP_K)
    weights = jax.nn.softmax(top_val, axis=-1)
    flat_e = top_idx.reshape(-1)
    order = jnp.argsort(flat_e)
    token_of = order // TOP_K
    xs = tok[token_of]
    group_sizes = jnp.bincount(flat_e, length=N_EXPERTS).astype(jnp.int32)
    g = lax.ragged_dot(xs, w_gate, group_sizes)
    u = lax.ragged_dot(xs, w_up, group_sizes)
    y = lax.ragged_dot(jax.nn.silu(g) * u, w_down, group_sizes)
    y = y * weights.reshape(-1)[order][:, None].astype(y.dtype)
    out = jax.ops.segment_sum(y, token_of, num_segments=n)
    return out.reshape(B, S, D)


def setup_inputs(seed: int = 0) -> dict:
    key = jax.random.key(seed)
    ks = jax.random.split(key, 24)
    D = D_MODEL
    f32 = jnp.float32
    nrm = lambda k, shp, s: jax.random.normal(k, shp, f32) * s
    return {
        "x": nrm(ks[0], (BATCH, SEQ, D), 1.0),
        "c": nrm(ks[1], (BATCH, D), 1.0),
        "ctx": nrm(ks[2], (BATCH, CTX_LEN, D), 1.0),
        "c_ctx": nrm(ks[3], (D,), 1.0),
        "w_ada": nrm(ks[4], (DEPTH, D, 6 * D), D ** -0.5),
        "b_ada": nrm(ks[5], (DEPTH, 6 * D), 0.02),
        "w_in": nrm(ks[6], (N_EVEN, D, IN_WIDTH), D ** -0.5),
        "attn_sink": nrm(ks[7], (N_EVEN, N_Q_HEADS), 1.0),
        "gm_gain": 1.0 + nrm(ks[8], (N_EVEN, GM_WIDTH), 0.1),
        "gm_w_s": nrm(ks[9], (N_EVEN, GM_GROUPS, CHUNK, CHUNK), CHUNK ** -0.5),
        "gm_b_s": 1.0 + nrm(ks[10], (N_EVEN, GM_GROUPS, CHUNK), 0.1),
        "w_out": nrm(ks[11], (N_EVEN, MIX_WIDTH, D), MIX_WIDTH ** -0.5),
        "ffn_w_gate": nrm(ks[12], (N_EVEN, D, D_FF_DENSE), D ** -0.5),
        "ffn_w_up": nrm(ks[13], (N_EVEN, D, D_FF_DENSE), D ** -0.5),
        "ffn_w_down": nrm(ks[14], (N_EVEN, D_FF_DENSE, D), D_FF_DENSE ** -0.5),
        "pool_w": nrm(ks[15], (N_ODD, POOL_GROUPS, POOL_GROUP_DIM, POOL_GROUP_DIM), POOL_GROUP_DIM ** -0.5),
        "pool_scale": 1.0 + nrm(ks[16], (N_ODD, D), 0.1),
        "router_w": nrm(ks[17], (N_ODD, D, N_EXPERTS), D ** -0.5),
        "moe_w_gate": nrm(ks[18], (N_ODD, N_EXPERTS, D, D_FF_EXPERT), D ** -0.5),
        "moe_w_up": nrm(ks[19], (N_ODD, N_EXPERTS, D, D_FF_EXPERT), D ** -0.5),
        "moe_w_down": nrm(ks[20], (N_ODD, N_EXPERTS, D_FF_EXPERT, D), D_FF_EXPERT ** -0.5),
        "final_gain": 1.0 + nrm(ks[21], (D,), 0.1),
    }


def reference(x, c, ctx, c_ctx, w_ada, b_ada, w_in, attn_sink, gm_gain, gm_w_s, gm_b_s, w_out,
              ffn_w_gate, ffn_w_up, ffn_w_down, pool_w, pool_scale, router_w,
              moe_w_gate, moe_w_up, moe_w_down, final_gain):
    S = x.shape[1]
    cos, sin = _axial_rope_tables(S)
    s_c = jax.nn.silu(c)
    s_cc = jax.nn.silu(c_ctx)
    for layer in range(DEPTH):
        even = layer % 2 == 0
        i = layer // 2
        advance_ctx = any(j % 2 == 0 for j in range(layer + 1, DEPTH))
        mx = jnp.split((s_c @ w_ada[layer] + b_ada[layer])[:, None, :], 6, axis=-1)
        if even or advance_ctx:
            mc = jnp.split(s_cc @ w_ada[layer] + b_ada[layer], 6)
        if even:
            hx = _modulate(x, mx[0], mx[1])
            hc = _modulate(ctx, mc[0], mc[1])
            yx, yc = _attn_gmlp_mixer(hx, hc, cos, sin, w_in[i], attn_sink[i], gm_gain[i],
                                      gm_w_s[i], gm_b_s[i], w_out[i], advance_ctx)
            x = x + mx[2] * yx
            x = x + mx[5] * _swiglu(_modulate(x, mx[3], mx[4]), ffn_w_gate[i], ffn_w_up[i], ffn_w_down[i])
            if advance_ctx:
                ctx = ctx + mc[2] * yc
                ctx = ctx + mc[5] * _swiglu(_modulate(ctx, mc[3], mc[4]),
                                            ffn_w_gate[i], ffn_w_up[i], ffn_w_down[i])
        else:
            x = x + mx[2] * _multiscale_pool(_modulate(x, mx[0], mx[1]), pool_w[i], pool_scale[i])
            x = x + mx[5] * _moe_swiglu(_modulate(x, mx[3], mx[4]), router_w[i],
                                        moe_w_gate[i], moe_w_up[i], moe_w_down[i])
            if advance_ctx:
                ctx = ctx + mc[2] * _multiscale_pool(_modulate(ctx, mc[0], mc[1]), pool_w[i], pool_scale[i])
                ctx = ctx + mc[5] * _moe_swiglu(_modulate(ctx, mc[3], mc[4]), router_w[i],
                                                moe_w_gate[i], moe_w_up[i], moe_w_down[i])
    return _rms_normalize(x) * final_gain
```

```python
import functools

import numpy as np
import jax
import jax.numpy as jnp
from jax import lax
from jax.experimental import pallas as pl
from jax.experimental.pallas import tpu as pltpu

F32 = jnp.float32
BF16 = jnp.bfloat16

D = 1024
GRID_W = 64
EPS = 1e-6
NEG_INF = -1e30
HEAD_DIM = 64
N_Q_HEADS = 8
BLOCK = 128
ATT_W = 512
KV_W = 128
GM_W = 512
IN_W = 1792
POOL_SIZES = (2, 4, 8, 16)
POOL_GD = 256
POOL_HALO = 16
N_EXPERTS = 8
ROPE_BASE = 10000.0
LANES = 128
SQRT_2_OVER_PI = 0.7978845608028654

TM_IN = 512
TM_FFN = 512
TM_POOL = 512
TM_MOE = 512
TF_MOE = 512
TC_COMB = 256
VMEM_LIMIT = 56 * 1024 * 1024


def _cparams(sem, vmem=None):
    return pltpu.CompilerParams(dimension_semantics=sem, vmem_limit_bytes=vmem)


def _modulate(xf, shift, scale):
    ms = jnp.mean(xf * xf, axis=-1, keepdims=True)
    return xf * lax.rsqrt(ms + EPS) * (1.0 + scale) + shift


def _sigmoid(z):
    return 1.0 / (1.0 + jnp.exp(-z))


def _ada_kernel(c_ref, w_ref, b_ref, o_ref):
    c = c_ref[...]
    s = c * _sigmoid(c)
    o_ref[0] = jnp.dot(s, w_ref[0], precision=lax.Precision.HIGHEST,
                       preferred_element_type=F32) + b_ref[0]


def _ada_mod(cvec, w_ada, b_ada):
    depth, _, n6 = w_ada.shape
    tn = 1536
    out = pl.pallas_call(
        _ada_kernel,
        grid=(depth, n6 // tn),
        in_specs=[
            pl.BlockSpec((8, D), lambda l, j: (0, 0)),
            pl.BlockSpec((1, D, tn), lambda l, j: (l, 0, j)),
            pl.BlockSpec((1, 1, tn), lambda l, j: (l, 0, j)),
        ],
        out_specs=pl.BlockSpec((1, 8, tn), lambda l, j: (l, 0, j)),
        out_shape=jax.ShapeDtypeStruct((depth, 8, n6), F32),
        compiler_params=_cparams(("arbitrary", "arbitrary")),
        name="ada_mod",
    )(cvec, w_ada, b_ada.reshape(depth, 1, n6))
    return out.reshape(depth, 8, 6, D)


def _rope(t, cs, sn, first_half):
    fwd = pltpu.roll(t, LANES - 16, axis=1)
    bwd = pltpu.roll(t, 16, axis=1)
    return t * cs + jnp.where(first_half, fwd, bwd) * sn


def _inproj_kernel(x_ref, mod_ref, w_ref, gain_ref, cos_ref, sin_ref,
                   q_ref, k_ref, ksw_ref, v_ref, vsw_ref, u_ref, vg_ref):
    mod = mod_ref[0, 0]
    h = _modulate(x_ref[0], mod[0:1], mod[1:2]).astype(BF16)
    proj = jnp.dot(h, w_ref[...], preferred_element_type=F32)
    cs = cos_ref[...]
    sn = sin_ref[...]
    lane = lax.broadcasted_iota(jnp.int32, cs.shape, 1)
    first_half = (lane & 16) == 0
    for cix in range(ATT_W // LANES):
        t = proj[:, cix * LANES:(cix + 1) * LANES]
        q_ref[0, :, cix * LANES:(cix + 1) * LANES] = (
            _rope(t, cs, sn, first_half) * (HEAD_DIM ** -0.5)).astype(BF16)
    kr = _rope(proj[:, ATT_W:ATT_W + KV_W], cs, sn, first_half)
    k_ref[0] = kr.astype(BF16)
    ksw_ref[0] = pltpu.roll(kr, HEAD_DIM, axis=1).astype(BF16)
    vv = proj[:, ATT_W + KV_W:ATT_W + 2 * KV_W]
    v_ref[0] = vv.astype(BF16)
    vsw_ref[0] = pltpu.roll(vv, HEAD_DIM, axis=1).astype(BF16)
    z = proj[:, ATT_W + 2 * KV_W:]
    g = z * (0.5 * (1.0 + jnp.tanh(SQRT_2_OVER_PI * (z + 0.044715 * (z * z * z)))))
    u_ref[0] = g[:, :GM_W].astype(BF16)
    vg = g[:, GM_W:]
    ms = jnp.mean(vg * vg, axis=-1, keepdims=True)
    vg_ref[0] = (vg * lax.rsqrt(ms + EPS) * gain_ref[...]).astype(BF16)


def _in_proj(x, mod, w_in_bf, gm_gain, cos_t, sin_t):
    b, s, _ = x.shape
    tm = TM_IN
    row = lambda w: pl.BlockSpec((1, tm, w), lambda bi, i: (bi, i, 0))
    outs = pl.pallas_call(
        _inproj_kernel,
        grid=(b, s // tm),
        in_specs=[
            row(D),
            pl.BlockSpec((1, 1, 6, D), lambda bi, i: (0, bi, 0, 0)),
            pl.BlockSpec((D, IN_W), lambda bi, i: (0, 0)),
            pl.BlockSpec((1, GM_W), lambda bi, i: (0, 0)),
            pl.BlockSpec((tm, LANES), lambda bi, i: (i, 0)),
            pl.BlockSpec((tm, LANES), lambda bi, i: (i, 0)),
        ],
        out_specs=[row(ATT_W), row(KV_W), row(KV_W), row(KV_W), row(KV_W), row(GM_W), row(GM_W)],
        out_shape=[jax.ShapeDtypeStruct((b, s, w), BF16)
                   for w in (ATT_W, KV_W, KV_W, KV_W, KV_W, GM_W, GM_W)],
        compiler_params=_cparams(("arbitrary", "arbitrary"), VMEM_LIMIT),
        name="in_proj",
    )(x, mod, w_in_bf, gm_gain.reshape(1, GM_W), cos_t, sin_t)
    return outs


def _ctx_kernel(c_ref, mod_ref, w_ref, k_ref, ksw_ref, v_ref, vsw_ref):
    mod = mod_ref[0, 0]
    h = _modulate(c_ref[0], mod[0:1], mod[1:2]).astype(BF16)
    kv = jnp.dot(h, w_ref[...], preferred_element_type=F32)
    kk = kv[:, :KV_W]
    vv = kv[:, KV_W:]
    k_ref[0] = kk.astype(BF16)
    ksw_ref[0] = pltpu.roll(kk, HEAD_DIM, axis=1).astype(BF16)
    v_ref[0] = vv.astype(BF16)
    vsw_ref[0] = pltpu.roll(vv, HEAD_DIM, axis=1).astype(BF16)


def _ctx_kv(ctx, mod, w_in_bf):
    b, l, _ = ctx.shape
    spec = pl.BlockSpec((1, l, KV_W), lambda bi: (bi, 0, 0))
    return pl.pallas_call(
        _ctx_kernel,
        grid=(b,),
        in_specs=[
            pl.BlockSpec((1, l, D), lambda bi: (bi, 0, 0)),
            pl.BlockSpec((1, 1, 6, D), lambda bi: (0, b, 0, 0)),
            pl.BlockSpec((D, 2 * KV_W), lambda bi: (0, ATT_W // (2 * KV_W))),
        ],
        out_specs=[spec] * 4,
        out_shape=[jax.ShapeDtypeStruct((b, l, KV_W), BF16)] * 4,
        compiler_params=_cparams(("arbitrary",)),
        name="ctx_kv",
    )(ctx, mod, w_in_bf)


def _attn_kernel(sink_ref, q_ref, kp_ref, kc_ref, kn_ref, ksp_ref, ksc_ref, ksn_ref,
                 vp_ref, vc_ref, vn_ref, vsp_ref, vsc_ref, vsn_ref,
                 kx_ref, ksx_ref, vx_ref, vsx_ref,
                 u_ref, vg_ref, ws_ref, bs_ref, wout_ref, x_ref, mod_ref, o_ref):
    n = pl.program_id(1)
    nb = pl.num_programs(1)
    ctx_len = kx_ref.shape[1]
    nk = 3 * BLOCK + ctx_len

    lane = lax.broadcasted_iota(jnp.int32, (1, LANES), 1)
    low = lane < HEAD_DIM

    k0 = jnp.concatenate([kp_ref[0], kc_ref[0], kn_ref[0], kx_ref[0]], axis=0)
    k1 = jnp.concatenate([ksp_ref[0], ksc_ref[0], ksn_ref[0], ksx_ref[0]], axis=0)
    v0 = jnp.concatenate([vp_ref[0], vc_ref[0], vn_ref[0], vx_ref[0]], axis=0)
    v1 = jnp.concatenate([vsp_ref[0], vsc_ref[0], vsn_ref[0], vsx_ref[0]], axis=0)
    zero = jnp.zeros((), BF16)
    k_var = ((jnp.where(low, k0, zero), jnp.where(low, zero, k1)),
             (jnp.where(low, k1, zero), jnp.where(low, zero, k0)))
    v_var = ((jnp.where(low, v0, zero), jnp.where(low, zero, v1)),
             (jnp.where(low, v1, zero), jnp.where(low, zero, v0)))

    row = lax.broadcasted_iota(jnp.int32, (BLOCK, nk), 0)
    col = lax.broadcasted_iota(jnp.int32, (BLOCK, nk), 1)
    valid = ((col >= BLOCK) & (col < 2 * BLOCK)) | (col >= 3 * BLOCK)
    valid = valid | ((col < BLOCK) & (col >= row) & (n > 0))
    valid = valid | ((col >= 2 * BLOCK) & (col < 3 * BLOCK) & (col - 2 * BLOCK <= row) & (n < nb - 1))
    valid2 = jnp.concatenate([valid, valid], axis=0)
    top = lax.broadcasted_iota(jnp.int32, (2 * BLOCK, 1), 0) < BLOCK

    q = q_ref[0]
    att_pairs = []
    for kvh in range(2):
        qst = jnp.concatenate([q[:, (2 * kvh) * LANES:(2 * kvh + 1) * LANES],
                               q[:, (2 * kvh + 1) * LANES:(2 * kvh + 2) * LANES]], axis=0)
        acc = None
        for half in range(2):
            s = lax.dot_general(qst, k_var[kvh][half], (((1,), (1,)), ((), ())),
                                preferred_element_type=F32)
            s = jnp.where(valid2, s, NEG_INF)
            sk = jnp.where(top, sink_ref[4 * kvh + half], sink_ref[4 * kvh + 2 + half])
            m = jnp.maximum(jnp.max(s, axis=-1, keepdims=True), sk)
            p = jnp.exp(s - m)
            den = jnp.sum(p, axis=-1, keepdims=True) + jnp.exp(sk - m)
            o = jnp.dot(p.astype(BF16), v_var[kvh][half], preferred_element_type=F32)
            o = o / den
            acc = o if acc is None else acc + o
        att_pairs += [acc[:BLOCK], acc[BLOCK:]]

    u = u_ref[0]
    vg = vg_ref[0]
    bs = bs_ref[...]
    gm_pairs = []
    for j in range(GM_W // LANES):
        vp = vg[:, j * LANES:(j + 1) * LANES]
        mixed = (jnp.dot(ws_ref[2 * j], jnp.where(low, vp, zero), preferred_element_type=F32)
                 + jnp.dot(ws_ref[2 * j + 1], jnp.where(low, zero, vp), preferred_element_type=F32)
                 + jnp.where(low, bs[:, 2 * j:2 * j + 1], bs[:, 2 * j + 1:2 * j + 2]))
        gm_pairs.append(u[:, j * LANES:(j + 1) * LANES].astype(F32) * mixed)

    mix = jnp.concatenate(att_pairs + gm_pairs, axis=1).astype(BF16)
    y = jnp.dot(mix, wout_ref[...], preferred_element_type=F32)
    mod = mod_ref[0, 0]
    o_ref[0] = x_ref[0] + mod[2:3] * y


def _attn_mixer(x, mod, sink, q, k, ksw, v, vsw, kx, ksx, vx, vsx, u, vg, ws_bf, bs_t, wout_bf):
    b, s, _ = x.shape
    nb = s // BLOCK
    l = kx.shape[1]
    cur = lambda w: pl.BlockSpec((1, BLOCK, w), lambda bi, n: (bi, n, 0))
    prv = lambda w: pl.BlockSpec((1, BLOCK, w), lambda bi, n: (bi, jnp.maximum(n - 1, 0), 0))
    nxt = lambda w: pl.BlockSpec((1, BLOCK, w), lambda bi, n: (bi, jnp.minimum(n + 1, nb - 1), 0))
    cx = pl.BlockSpec((1, l, KV_W), lambda bi, n: (bi, 0, 0))
    return pl.pallas_call(
        _attn_kernel,
        grid=(b, nb),
        in_specs=[
            pl.BlockSpec(memory_space=pltpu.SMEM),
            cur(ATT_W),
            prv(KV_W), cur(KV_W), nxt(KV_W), prv(KV_W), cur(KV_W), nxt(KV_W),
            prv(KV_W), cur(KV_W), nxt(KV_W), prv(KV_W), cur(KV_W), nxt(KV_W),
            cx, cx, cx, cx,
            cur(GM_W), cur(GM_W),
            pl.BlockSpec((8, BLOCK, BLOCK), lambda bi, n: (0, 0, 0)),
            pl.BlockSpec((BLOCK, 8), lambda bi, n: (0, 0)),
            pl.BlockSpec((D, D), lambda bi, n: (0, 0)),
            cur(D),
            pl.BlockSpec((1, 1, 6, D), lambda bi, n: (0, bi, 0, 0)),
        ],
        out_specs=cur(D),
        out_shape=jax.ShapeDtypeStruct((b, s, D), F32),
        compiler_params=_cparams(("arbitrary", "arbitrary"), VMEM_LIMIT),
        name="attn_gmlp_out",
    )(sink, q, k, k, k, ksw, ksw, ksw, v, v, v, vsw, vsw, vsw, kx, ksx, vx, vsx,
      u, vg, ws_bf, bs_t, wout_bf, x, mod)


def _ffn_kernel(x_ref, mod_ref, wg_ref, wu_ref, wd_ref, o_ref, h_scr, acc_scr):
    j = pl.program_id(1)
    mod = mod_ref[0, 0]

    @pl.when(j == 0)
    def _():
        h_scr[...] = _modulate(x_ref[...], mod[3:4], mod[4:5]).astype(BF16)
        acc_scr[...] = jnp.zeros_like(acc_scr)

    h = h_scr[...]
    g = jnp.dot(h, wg_ref[...], preferred_element_type=F32)
    up = jnp.dot(h, wu_ref[...], preferred_element_type=F32)
    a = (g * _sigmoid(g) * up).astype(BF16)
    acc_scr[...] += jnp.dot(a, wd_ref[...], preferred_element_type=F32)

    @pl.when(j == pl.num_programs(1) - 1)
    def _():
        o_ref[...] = x_ref[...] + mod[5:6] * acc_scr[...]


def _dense_ffn(x2d, mod, wg, wu, wd, seq):
    n = x2d.shape[0]
    f = wg.shape[1]
    tm = TM_FFN
    tf = f // 2
    per_b = seq // tm
    return pl.pallas_call(
        _ffn_kernel,
        grid=(n // tm, f // tf),
        in_specs=[
            pl.BlockSpec((tm, D), lambda i, j: (i, 0)),
            pl.BlockSpec((1, 1, 6, D), lambda i, j: (0, i // per_b, 0, 0)),
            pl.BlockSpec((D, tf), lambda i, j: (0, j)),
            pl.BlockSpec((D, tf), lambda i, j: (0, j)),
            pl.BlockSpec((tf, D), lambda i, j: (j, 0)),
        ],
        out_specs=pl.BlockSpec((tm, D), lambda i, j: (i, 0)),
        out_shape=jax.ShapeDtypeStruct((n, D), F32),
        scratch_shapes=[pltpu.VMEM((tm, D), BF16), pltpu.VMEM((tm, D), F32)],
        compiler_params=_cparams(("arbitrary", "arbitrary"), VMEM_LIMIT),
        name="dense_ffn",
    )(x2d, mod, wg, wu, wd)


def _pool_route_kernel(x_ref, xp_ref, xn_ref, mod_ref, band_ref, pw_ref, psc_ref, wr_hi_ref, wr_lo_ref,
                       tri_ref, x3_ref, h2_ref, route_ref, cnt_ref, hext, carry):
    bi = pl.program_id(0)
    i = pl.program_id(1)
    ni = pl.num_programs(1)
    tm = x_ref.shape[1]
    seq = tm * ni
    mod = mod_ref[0, 0]

    @pl.when((bi == 0) & (i == 0))
    def _():
        carry[...] = jnp.zeros_like(carry)

    xf = x_ref[0]
    hp = _modulate(xp_ref[0], mod[0:1], mod[1:2])
    hn = _modulate(xn_ref[0], mod[0:1], mod[1:2])
    hext[0:POOL_HALO] = jnp.where(i > 0, hp, 0.0).astype(BF16)
    h_main = _modulate(xf, mod[0:1], mod[1:2])
    hext[POOL_HALO:POOL_HALO + tm] = h_main.astype(BF16)
    hext[POOL_HALO + tm:] = jnp.where(i < ni - 1, hn, 0.0).astype(BF16)

    t_local = lax.broadcasted_iota(jnp.int32, (BLOCK, 1), 0)
    ys = []
    for gi, w in enumerate(POOL_SIZES):
        lo_off = -(w // 2)
        hi_off = w - 1 - w // 2
        cols = slice(gi * POOL_GD, (gi + 1) * POOL_GD)
        outs = []
        for sb in range(tm // BLOCK):
            r0 = sb * BLOCK
            win = jnp.dot(band_ref[gi], hext[r0:r0 + BLOCK + 2 * POOL_HALO, cols],
                          preferred_element_type=F32)
            t = i * tm + r0 + t_local
            cnt = (jnp.minimum(t + hi_off, seq - 1) - jnp.maximum(t + lo_off, 0) + 1).astype(F32)
            diff = win / cnt - h_main[r0:r0 + BLOCK, cols]
            outs.append(diff.astype(BF16))
        dg = jnp.concatenate(outs, axis=0)
        ys.append(jnp.dot(dg, pw_ref[gi], preferred_element_type=F32))
    y = jnp.concatenate(ys, axis=1) * psc_ref[...]
    x3 = xf + mod[2:3] * y
    x3_ref[0] = x3

    h2 = _modulate(x3, mod[3:4], mod[4:5])
    h2_ref[0] = h2
    h_hi = h2.astype(BF16)
    h_lo = (h2 - h_hi.astype(F32)).astype(BF16)
    logits = (jnp.dot(h_hi, wr_hi_ref[...], preferred_element_type=F32)
              + jnp.dot(h_hi, wr_lo_ref[...], preferred_element_type=F32)
              + jnp.dot(h_lo, wr_hi_ref[...], preferred_element_type=F32))
    lane = lax.broadcasted_iota(jnp.int32, (tm, LANES), 1)
    lane_f = lane.astype(F32)
    neg = -jnp.inf
    lg = jnp.where(lane < N_EXPERTS, logits, neg)
    m1 = jnp.max(lg, axis=-1, keepdims=True)
    i1 = jnp.min(jnp.where(lg == m1, lane_f, float(LANES)), axis=-1, keepdims=True)
    oh1 = lane_f == i1
    lg2 = jnp.where(oh1, neg, lg)
    m2 = jnp.max(lg2, axis=-1, keepdims=True)
    i2 = jnp.min(jnp.where(lg2 == m2, lane_f, float(LANES)), axis=-1, keepdims=True)
    oh2 = lane_f == i2
    e = jnp.exp(m2 - m1)
    w1 = 1.0 / (1.0 + e)
    w2 = e / (1.0 + e)
    oh = jnp.where(oh1 | oh2, 1.0, 0.0)
    before = jnp.dot(tri_ref[...], oh.astype(BF16), preferred_element_type=F32) + carry[...]
    r1 = jnp.sum(jnp.where(oh1, before, 0.0), axis=-1, keepdims=True)
    r2 = jnp.sum(jnp.where(oh2, before, 0.0), axis=-1, keepdims=True)
    carry[...] = carry[...] + jnp.sum(oh, axis=0, keepdims=True)
    cnt_ref[...] = carry[...]
    info = jnp.where(lane == 0, i1, jnp.where(lane == 1, i2, jnp.where(lane == 2, w1, jnp.where(
        lane == 3, w2, jnp.where(lane == 4, r1, jnp.where(lane == 5, r2, 0.0))))))
    route_ref[...] = info.T[0:8, :]


def _pool_route(x, mod, band, pw_bf, pool_scale, wr_hi, wr_lo, tri):
    b, s, _ = x.shape
    tm = TM_POOL
    ni = s // tm
    hb = tm // POOL_HALO
    row = pl.BlockSpec((1, tm, D), lambda bi, i: (bi, i, 0))
    const2 = lambda shp: pl.BlockSpec(shp, lambda bi, i: (0,) * len(shp))
    return pl.pallas_call(
        _pool_route_kernel,
        grid=(b, ni),
        in_specs=[
            row,
            pl.BlockSpec((1, POOL_HALO, D), lambda bi, i: (bi, jnp.maximum(i * hb - 1, 0), 0)),
            pl.BlockSpec((1, POOL_HALO, D), lambda bi, i: (bi, jnp.minimum((i + 1) * hb, s // POOL_HALO - 1), 0)),
            pl.BlockSpec((1, 1, 6, D), lambda bi, i: (1, bi, 0, 0)),
            const2(band.shape), const2(pw_bf.shape), const2((1, D)),
            const2(wr_hi.shape), const2(wr_lo.shape), const2(tri.shape),
        ],
        out_specs=[row, row,
                   pl.BlockSpec((8, tm), lambda bi, i: (0, bi * ni + i)),
                   pl.BlockSpec((1, LANES), lambda bi, i: (0, 0))],
        out_shape=[jax.ShapeDtypeStruct((b, s, D), F32), jax.ShapeDtypeStruct((b, s, D), F32),
                   jax.ShapeDtypeStruct((8, b * s), F32), jax.ShapeDtypeStruct((1, LANES), F32)],
        scratch_shapes=[pltpu.VMEM((tm + 2 * POOL_HALO, D), BF16), pltpu.VMEM((1, LANES), F32)],
        compiler_params=_cparams(("arbitrary", "arbitrary"), VMEM_LIMIT),
        name="pool_route",
    )(x, x, x, mod, band, pw_bf, pool_scale.reshape(1, D), wr_hi, wr_lo, tri)


def _row_copy(src_hbm, row, dst, r, sem):
    return pltpu.make_async_copy(src_hbm.at[pl.ds(row, 1)], dst.at[pl.ds(r, 1)], sem)


def _gather_kernel(nused_ref, idx_ref, h_hbm, o_ref, buf, sem):
    i = pl.program_id(0)
    rows = buf.shape[0]

    @pl.when(i < nused_ref[0])
    def _():
        def issue(r, c):
            _row_copy(h_hbm, idx_ref[0, 0, r], buf, r, sem).start()
            return c
        lax.fori_loop(0, rows, issue, 0)

        def drain(r, c):
            _row_copy(h_hbm, 0, buf, r, sem).wait()
            return c
        lax.fori_loop(0, rows, drain, 0)
        o_ref[...] = buf[...].astype(BF16)

    @pl.when(i >= nused_ref[0])
    def _():
        o_ref[...] = jnp.zeros_like(o_ref)


def _gather_rows(h2d, src, n_used, n_tiles):
    tm = TM_MOE
    return pl.pallas_call(
        _gather_kernel,
        grid_spec=pltpu.PrefetchScalarGridSpec(
            num_scalar_prefetch=1,
            grid=(n_tiles,),
            in_specs=[
                pl.BlockSpec((1, 1, tm), lambda i, nu: (i, 0, 0), memory_space=pltpu.SMEM),
                pl.BlockSpec(memory_space=pl.ANY),
            ],
            out_specs=pl.BlockSpec((tm, D), lambda i, nu: (i, 0)),
            scratch_shapes=[pltpu.VMEM((tm, D), F32), pltpu.SemaphoreType.DMA],
        ),
        out_shape=jax.ShapeDtypeStruct((n_tiles * tm, D), BF16),
        compiler_params=_cparams(("arbitrary",)),
        name="moe_gather",
    )(n_used, src.reshape(n_tiles, 1, tm), h2d)


def _moe_kernel(te_ref, nused_ref, x_ref, wg_ref, wu_ref, wd_ref, o_ref, acc_scr):
    i = pl.program_id(0)
    j = pl.program_id(1)
    nj = pl.num_programs(1)
    used = i < nused_ref[0]

    @pl.when(used & (j == 0))
    def _():
        acc_scr[...] = jnp.zeros_like(acc_scr)

    @pl.when(used)
    def _():
        xb = x_ref[...]
        g = jnp.dot(xb, wg_ref[0], preferred_element_type=F32)
        up = jnp.dot(xb, wu_ref[0], preferred_element_type=F32)
        a = (g * _sigmoid(g) * up).astype(BF16)
        acc_scr[...] += jnp.dot(a, wd_ref[0], preferred_element_type=F32)

    @pl.when(used & (j == nj - 1))
    def _():
        o_ref[...] = acc_scr[...]

    @pl.when(jnp.logical_not(used) & (j == nj - 1))
    def _():
        o_ref[...] = jnp.zeros_like(o_ref)


def _moe_experts(xs, tile_expert, n_used, wg, wu, wd):
    p = xs.shape[0]
    f = wg.shape[2]
    tm, tf = TM_MOE, TF_MOE
    nj = f // tf

    def jj(i, j, nu):
        return jnp.where(i < nu[0], j, nj - 1)

    return pl.pallas_call(
        _moe_kernel,
        grid_spec=pltpu.PrefetchScalarGridSpec(
            num_scalar_prefetch=2,
            grid=(p // tm, nj),
            in_specs=[
                pl.BlockSpec((tm, D), lambda i, j, te, nu: (i, 0)),
                pl.BlockSpec((1, D, tf), lambda i, j, te, nu: (te[i], 0, jj(i, j, nu))),
                pl.BlockSpec((1, D, tf), lambda i, j, te, nu: (te[i], 0, jj(i, j, nu))),
                pl.BlockSpec((1, tf, D), lambda i, j, te, nu: (te[i], jj(i, j, nu), 0)),
            ],
            out_specs=pl.BlockSpec((tm, D), lambda i, j, te, nu: (i, 0)),
            scratch_shapes=[pltpu.VMEM((tm, D), F32)],
        ),
        out_shape=jax.ShapeDtypeStruct((p, D), F32),
        compiler_params=_cparams(("arbitrary", "arbitrary"), VMEM_LIMIT),
        name="moe_experts",
    )(tile_expert, n_used, xs, wg, wu, wd)


def _combine_kernel(p1_ref, p2_ref, y_hbm, x_ref, w_ref, mod_ref, gain_ref, o_ref, b1, b2, sem):
    rows = b1.shape[0]

    def issue(r, c):
        _row_copy(y_hbm, p1_ref[0, 0, r], b1, r, sem).start()
        _row_copy(y_hbm, p2_ref[0, 0, r], b2, r, sem).start()
        return c
    lax.fori_loop(0, rows, issue, 0)

    def drain(r, c):
        _row_copy(y_hbm, 0, b1, r, sem).wait()
        _row_copy(y_hbm, 0, b2, r, sem).wait()
        return c
    lax.fori_loop(0, rows, drain, 0)

    w = w_ref[...]
    moe = w[:, 0:1] * b1[...] + w[:, 1:2] * b2[...]
    mod = mod_ref[0, 0]
    x4 = x_ref[...] + mod[5:6] * moe
    ms = jnp.mean(x4 * x4, axis=-1, keepdims=True)
    o_ref[...] = x4 * lax.rsqrt(ms + EPS) * gain_ref[...]


def _combine(y, x3_2d, pos1, pos2, wts, mod, final_gain, seq):
    n = x3_2d.shape[0]
    tc = TC_COMB
    nt = n // tc
    per_b = seq // tc
    idx = pl.BlockSpec((1, 1, tc), lambda i: (i, 0, 0), memory_space=pltpu.SMEM)
    return pl.pallas_call(
        _combine_kernel,
        grid=(nt,),
        in_specs=[
            idx, idx,
            pl.BlockSpec(memory_space=pl.ANY),
            pl.BlockSpec((tc, D), lambda i: (i, 0)),
            pl.BlockSpec((tc, 2), lambda i: (i, 0)),
            pl.BlockSpec((1, 1, 6, D), lambda i: (1, i // per_b, 0, 0)),
            pl.BlockSpec((1, D), lambda i: (0, 0)),
        ],
        out_specs=pl.BlockSpec((tc, D), lambda i: (i, 0)),
        out_shape=jax.ShapeDtypeStruct((n, D), F32),
        scratch_shapes=[pltpu.VMEM((tc, D), F32), pltpu.VMEM((tc, D), F32), pltpu.SemaphoreType.DMA],
        compiler_params=_cparams(("arbitrary",)),
        name="moe_combine",
    )(pos1.reshape(nt, 1, tc), pos2.reshape(nt, 1, tc), y, x3_2d, wts, mod, final_gain.reshape(1, D))


def _rope_tables(seq):
    rows = seq // GRID_W
    row_pos = jnp.repeat(jnp.arange(rows, dtype=F32), GRID_W)
    col_pos = jnp.tile(jnp.arange(GRID_W, dtype=F32), rows)
    axis_dim = HEAD_DIM // 2
    inv_freq = ROPE_BASE ** (-jnp.arange(0, axis_dim, 2, dtype=F32) / axis_dim)
    ar = row_pos[:, None] * inv_freq
    ac = col_pos[:, None] * inv_freq
    cos64 = jnp.concatenate([jnp.cos(ar), jnp.cos(ar), jnp.cos(ac), jnp.cos(ac)], axis=1)
    sin64 = jnp.concatenate([-jnp.sin(ar), jnp.sin(ar), -jnp.sin(ac), jnp.sin(ac)], axis=1)
    return jnp.tile(cos64, (1, 2)), jnp.tile(sin64, (1, 2))


def _band_matrices():
    r = np.arange(BLOCK)[:, None]
    c = np.arange(BLOCK + 2 * POOL_HALO)[None, :] - POOL_HALO
    mats = []
    for w in POOL_SIZES:
        lo = -(w // 2)
        hi = w - 1 - w // 2
        mats.append(((c >= r + lo) & (c <= r + hi)).astype(np.float32))
    return jnp.asarray(np.stack(mats), dtype=BF16)


def kernel(x, c, ctx, c_ctx, w_ada, b_ada, w_in, attn_sink, gm_gain, gm_w_s, gm_b_s, w_out,
           ffn_w_gate, ffn_w_up, ffn_w_down, pool_w, pool_scale, router_w,
           moe_w_gate, moe_w_up, moe_w_down, final_gain):
    b, s, _ = x.shape
    n = b * s
    assert w_ada.shape[0] == 2 and w_in.shape[0] == 1 and pool_w.shape[0] == 1
    assert s % TM_IN == 0 and s % TM_POOL == 0 and s % TM_FFN == 0 and b <= 4

    cvec = jnp.concatenate([c, c_ctx[None, :], jnp.zeros((8 - b - 1, D), F32)], axis=0)
    mod = _ada_mod(cvec, w_ada, b_ada)

    cos_t, sin_t = _rope_tables(s)
    w_in_bf = w_in[0].astype(BF16)
    q, k, ksw, v, vsw, u, vg = _in_proj(x, mod, w_in_bf, gm_gain[0], cos_t, sin_t)
    kx, ksx, vx, vsx = _ctx_kv(ctx, mod, w_in_bf)
    x1 = _attn_mixer(x, mod, attn_sink[0], q, k, ksw, v, vsw, kx, ksx, vx, vsx, u, vg,
                     gm_w_s[0].astype(BF16), gm_b_s[0].T, w_out[0].astype(BF16))
    x2 = _dense_ffn(x1.reshape(n, D), mod, ffn_w_gate[0].astype(BF16), ffn_w_up[0].astype(BF16),
                    ffn_w_down[0].astype(BF16), s)

    wr = jnp.pad(router_w[0], ((0, 0), (0, LANES - N_EXPERTS)))
    wr_hi = wr.astype(BF16)
    wr_lo = (wr - wr_hi.astype(F32)).astype(BF16)
    tri = jnp.asarray(np.tril(np.ones((TM_POOL, TM_POOL), np.float32), -1), dtype=BF16)
    x3, h2, route, counts = _pool_route(x2.reshape(b, s, D), mod, _band_matrices(), pool_w[0].astype(BF16),
                                        pool_scale[0], wr_hi, wr_lo, tri)

    tm = TM_MOE
    n_tiles = (2 * n) // tm + N_EXPERTS
    cnt = counts[0, :N_EXPERTS].astype(jnp.int32)
    tiles_e = (cnt + tm - 1) // tm
    tile_end = jnp.cumsum(tiles_e)
    off = (tile_end - tiles_e) * tm
    n_used = tile_end[-1]
    tix = jnp.arange(n_tiles, dtype=jnp.int32)
    te = jnp.minimum(jnp.searchsorted(tile_end, tix, side="right"), N_EXPERTS - 1).astype(jnp.int32)
    te_last = te[jnp.maximum(n_used - 1, 0)]
    tile_expert = jnp.where(tix < n_used, te, te_last)
    e1 = route[0].astype(jnp.int32)
    e2 = route[1].astype(jnp.int32)
    pos1 = off[e1] + route[4].astype(jnp.int32)
    pos2 = off[e2] + route[5].astype(jnp.int32)
    tok = jnp.arange(n, dtype=jnp.int32)
    src = jnp.zeros((n_tiles * tm,), jnp.int32).at[jnp.concatenate([pos1, pos2])].set(
        jnp.concatenate([tok, tok]), unique_indices=True)
    n_used_arr = n_used.reshape(1).astype(jnp.int32)

    xs = _gather_rows(h2.reshape(n, D), src, n_used_arr, n_tiles)
    y = _moe_experts(xs, tile_expert, n_used_arr, moe_w_gate[0].astype(BF16), moe_w_up[0].astype(BF16),
                     moe_w_down[0].astype(BF16))
    out = _combine(y, x3.reshape(n, D), pos1, pos2, route[2:4].T, mod, final_gain, s)
    return out.reshape(b, s, D)
```

```python
import functools

import numpy as np
import jax
import jax.numpy as jnp
from jax import lax
from jax.experimental import pallas as pl
from jax.experimental.pallas import tpu as pltpu

F32 = jnp.float32
BF16 = jnp.bfloat16

D = 1024
GRID_W = 64
EPS = 1e-6
NEG_INF = -1e30
HEAD_DIM = 64
N_Q_HEADS = 8
BLOCK = 128
ATT_W = 512
KV_W = 128
GM_W = 512
IN_W = 1792
POOL_SIZES = (2, 4, 8, 16)
POOL_GD = 256
POOL_HALO = 16
N_EXPERTS = 8
ROPE_BASE = 10000.0
LANES = 128
SQRT_2_OVER_PI = 0.7978845608028654

TM_IN = 512
TM_FFN = 512
TM_POOL = 512
TM_MOE = 512
MOE_NJ = 2
TC_COMB = 512
VMEM_LIMIT = 56 * 1024 * 1024


def _cparams(sem, vmem=None):
    return pltpu.CompilerParams(dimension_semantics=sem, vmem_limit_bytes=vmem)


def _modulate(xf, shift, scale):
    ms = jnp.mean(xf * xf, axis=-1, keepdims=True)
    return xf * lax.rsqrt(ms + EPS) * (1.0 + scale) + shift


def _sigmoid(z):
    return 1.0 / (1.0 + jnp.exp(-z))


def _ada_kernel(c_ref, w_ref, b_ref, o_ref):
    c = c_ref[...]
    s = c * _sigmoid(c)
    o_ref[0] = jnp.dot(s, w_ref[0], precision=lax.Precision.HIGHEST,
                       preferred_element_type=F32) + b_ref[0]


def _ada_mod(cvec, w_ada, b_ada):
    depth, _, n6 = w_ada.shape
    tn = 1536
    out = pl.pallas_call(
        _ada_kernel,
        grid=(depth, n6 // tn),
        in_specs=[
            pl.BlockSpec((8, D), lambda l, j: (0, 0)),
            pl.BlockSpec((1, D, tn), lambda l, j: (l, 0, j)),
            pl.BlockSpec((1, 1, tn), lambda l, j: (l, 0, j)),
        ],
        out_specs=pl.BlockSpec((1, 8, tn), lambda l, j: (l, 0, j)),
        out_shape=jax.ShapeDtypeStruct((depth, 8, n6), F32),
        compiler_params=_cparams(("arbitrary", "arbitrary")),
        name="ada_mod",
    )(cvec, w_ada, b_ada.reshape(depth, 1, n6))
    return out.reshape(depth, 8, 6, D)


def _rope(t, cs, sn, first_half):
    fwd = pltpu.roll(t, LANES - 16, axis=1)
    bwd = pltpu.roll(t, 16, axis=1)
    return t * cs + jnp.where(first_half, fwd, bwd) * sn


def _inproj_kernel(x_ref, mod_ref, w_ref, gain_ref, cos_ref, sin_ref,
                   q_ref, k_ref, ksw_ref, v_ref, vsw_ref, u_ref, vg_ref):
    mod = mod_ref[0, 0]
    h = _modulate(x_ref[0], mod[0:1], mod[1:2]).astype(BF16)
    proj = jnp.dot(h, w_ref[...], preferred_element_type=F32)
    cs = cos_ref[...]
    sn = sin_ref[...]
    lane = lax.broadcasted_iota(jnp.int32, cs.shape, 1)
    first_half = (lane & 16) == 0
    for cix in range(ATT_W // LANES):
        t = proj[:, cix * LANES:(cix + 1) * LANES]
        q_ref[0, :, cix * LANES:(cix + 1) * LANES] = (
            _rope(t, cs, sn, first_half) * (HEAD_DIM ** -0.5)).astype(BF16)
    kr = _rope(proj[:, ATT_W:ATT_W + KV_W], cs, sn, first_half)
    k_ref[0] = kr.astype(BF16)
    ksw_ref[0] = pltpu.roll(kr, HEAD_DIM, axis=1).astype(BF16)
    vv = proj[:, ATT_W + KV_W:ATT_W + 2 * KV_W]
    v_ref[0] = vv.astype(BF16)
    vsw_ref[0] = pltpu.roll(vv, HEAD_DIM, axis=1).astype(BF16)
    z = proj[:, ATT_W + 2 * KV_W:]
    g = z * (0.5 * (1.0 + jnp.tanh(SQRT_2_OVER_PI * (z + 0.044715 * (z * z * z)))))
    u_ref[0] = g[:, :GM_W].astype(BF16)
    vg = g[:, GM_W:]
    ms = jnp.mean(vg * vg, axis=-1, keepdims=True)
    vg_ref[0] = (vg * lax.rsqrt(ms + EPS) * gain_ref[...]).astype(BF16)


def _in_proj(x, mod, w_in_bf, gm_gain, cos_t, sin_t):
    b, s, _ = x.shape
    tm = TM_IN
    row = lambda w: pl.BlockSpec((1, tm, w), lambda bi, i: (bi, i, 0))
    outs = pl.pallas_call(
        _inproj_kernel,
        grid=(b, s // tm),
        in_specs=[
            row(D),
            pl.BlockSpec((1, 1, 6, D), lambda bi, i: (0, bi, 0, 0)),
            pl.BlockSpec((D, IN_W), lambda bi, i: (0, 0)),
            pl.BlockSpec((1, GM_W), lambda bi, i: (0, 0)),
            pl.BlockSpec((tm, LANES), lambda bi, i: (i, 0)),
            pl.BlockSpec((tm, LANES), lambda bi, i: (i, 0)),
        ],
        out_specs=[row(ATT_W), row(KV_W), row(KV_W), row(KV_W), row(KV_W), row(GM_W), row(GM_W)],
        out_shape=[jax.ShapeDtypeStruct((b, s, w), BF16)
                   for w in (ATT_W, KV_W, KV_W, KV_W, KV_W, GM_W, GM_W)],
        compiler_params=_cparams(("arbitrary", "arbitrary"), VMEM_LIMIT),
        name="in_proj",
    )(x, mod, w_in_bf, gm_gain.reshape(1, GM_W), cos_t, sin_t)
    return outs


def _ctx_kernel(c_ref, mod_ref, w_ref, k_ref, ksw_ref, v_ref, vsw_ref):
    mod = mod_ref[0, 0]
    h = _modulate(c_ref[0], mod[0:1], mod[1:2]).astype(BF16)
    kv = jnp.dot(h, w_ref[...], preferred_element_type=F32)
    kk = kv[:, :KV_W]
    vv = kv[:, KV_W:]
    k_ref[0] = kk.astype(BF16)
    ksw_ref[0] = pltpu.roll(kk, HEAD_DIM, axis=1).astype(BF16)
    v_ref[0] = vv.astype(BF16)
    vsw_ref[0] = pltpu.roll(vv, HEAD_DIM, axis=1).astype(BF16)


def _ctx_kv(ctx, mod, w_in_bf):
    b, l, _ = ctx.shape
    spec = pl.BlockSpec((1, l, KV_W), lambda bi: (bi, 0, 0))
    return pl.pallas_call(
        _ctx_kernel,
        grid=(b,),
        in_specs=[
            pl.BlockSpec((1, l, D), lambda bi: (bi, 0, 0)),
            pl.BlockSpec((1, 1, 6, D), lambda bi: (0, b, 0, 0)),
            pl.BlockSpec((D, 2 * KV_W), lambda bi: (0, ATT_W // (2 * KV_W))),
        ],
        out_specs=[spec] * 4,
        out_shape=[jax.ShapeDtypeStruct((b, l, KV_W), BF16)] * 4,
        compiler_params=_cparams(("arbitrary",)),
        name="ctx_kv",
    )(ctx, mod, w_in_bf)


def _attn_kernel(sink_ref, q_ref, kp_ref, kc_ref, kn_ref, ksp_ref, ksc_ref, ksn_ref,
                 vp_ref, vc_ref, vn_ref, vsp_ref, vsc_ref, vsn_ref,
                 kx_ref, ksx_ref, vx_ref, vsx_ref,
                 u_ref, vg_ref, ws_ref, bs_ref, wout_ref, x_ref, mod_ref, o_ref):
    n = pl.program_id(1)
    nb = pl.num_programs(1)
    ctx_len = kx_ref.shape[1]
    nk = 3 * BLOCK + ctx_len

    lane = lax.broadcasted_iota(jnp.int32, (1, LANES), 1)
    low = lane < HEAD_DIM

    k0 = jnp.concatenate([kp_ref[0], kc_ref[0], kn_ref[0], kx_ref[0]], axis=0)
    k1 = jnp.concatenate([ksp_ref[0], ksc_ref[0], ksn_ref[0], ksx_ref[0]], axis=0)
    v0 = jnp.concatenate([vp_ref[0], vc_ref[0], vn_ref[0], vx_ref[0]], axis=0)
    v1 = jnp.concatenate([vsp_ref[0], vsc_ref[0], vsn_ref[0], vsx_ref[0]], axis=0)
    zero = jnp.zeros((), BF16)
    k_var = ((jnp.where(low, k0, zero), jnp.where(low, zero, k1)),
             (jnp.where(low, k1, zero), jnp.where(low, zero, k0)))
    v_var = ((jnp.where(low, v0, zero), jnp.where(low, zero, v1)),
             (jnp.where(low, v1, zero), jnp.where(low, zero, v0)))

    row = lax.broadcasted_iota(jnp.int32, (BLOCK, nk), 0)
    col = lax.broadcasted_iota(jnp.int32, (BLOCK, nk), 1)
    valid = ((col >= BLOCK) & (col < 2 * BLOCK)) | (col >= 3 * BLOCK)
    valid = valid | ((col < BLOCK) & (col >= row) & (n > 0))
    valid = valid | ((col >= 2 * BLOCK) & (col < 3 * BLOCK) & (col - 2 * BLOCK <= row) & (n < nb - 1))
    valid2 = jnp.concatenate([valid, valid], axis=0)
    top = lax.broadcasted_iota(jnp.int32, (2 * BLOCK, 1), 0) < BLOCK

    q = q_ref[0]
    att_pairs = []
    for kvh in range(2):
        qst = jnp.concatenate([q[:, (2 * kvh) * LANES:(2 * kvh + 1) * LANES],
                               q[:, (2 * kvh + 1) * LANES:(2 * kvh + 2) * LANES]], axis=0)
        acc = None
        for half in range(2):
            s = lax.dot_general(qst, k_var[kvh][half], (((1,), (1,)), ((), ())),
                                preferred_element_type=F32)
            s = jnp.where(valid2, s, NEG_INF)
            sk = jnp.where(top, sink_ref[4 * kvh + half], sink_ref[4 * kvh + 2 + half])
            m = jnp.maximum(jnp.max(s, axis=-1, keepdims=True), sk)
            p = jnp.exp(s - m)
            den = jnp.sum(p, axis=-1, keepdims=True) + jnp.exp(sk - m)
            o = jnp.dot(p.astype(BF16), v_var[kvh][half], preferred_element_type=F32)
            o = o / den
            acc = o if acc is None else acc + o
        att_pairs += [acc[:BLOCK], acc[BLOCK:]]

    u = u_ref[0]
    vg = vg_ref[0]
    bs = bs_ref[...]
    gm_pairs = []
    for j in range(GM_W // LANES):
        vp = vg[:, j * LANES:(j + 1) * LANES]
        mixed = (jnp.dot(ws_ref[2 * j], jnp.where(low, vp, zero), preferred_element_type=F32)
                 + jnp.dot(ws_ref[2 * j + 1], jnp.where(low, zero, vp), preferred_element_type=F32)
                 + jnp.where(low, bs[:, 2 * j:2 * j + 1], bs[:, 2 * j + 1:2 * j + 2]))
        gm_pairs.append(u[:, j * LANES:(j + 1) * LANES].astype(F32) * mixed)

    mix = jnp.concatenate(att_pairs + gm_pairs, axis=1).astype(BF16)
    y = jnp.dot(mix, wout_ref[...], preferred_element_type=F32)
    mod = mod_ref[0, 0]
    o_ref[0] = x_ref[0] + mod[2:3] * y


def _attn_mixer(x, mod, sink, q, k, ksw, v, vsw, kx, ksx, vx, vsx, u, vg, ws_bf, bs_t, wout_bf):
    b, s, _ = x.shape
    nb = s // BLOCK
    l = kx.shape[1]
    cur = lambda w: pl.BlockSpec((1, BLOCK, w), lambda bi, n: (bi, n, 0))
    prv = lambda w: pl.BlockSpec((1, BLOCK, w), lambda bi, n: (bi, jnp.maximum(n - 1, 0), 0))
    nxt = lambda w: pl.BlockSpec((1, BLOCK, w), lambda bi, n: (bi, jnp.minimum(n + 1, nb - 1), 0))
    cx = pl.BlockSpec((1, l, KV_W), lambda bi, n: (bi, 0, 0))
    return pl.pallas_call(
        _attn_kernel,
        grid=(b, nb),
        in_specs=[
            pl.BlockSpec(memory_space=pltpu.SMEM),
            cur(ATT_W),
            prv(KV_W), cur(KV_W), nxt(KV_W), prv(KV_W), cur(KV_W), nxt(KV_W),
            prv(KV_W), cur(KV_W), nxt(KV_W), prv(KV_W), cur(KV_W), nxt(KV_W),
            cx, cx, cx, cx,
            cur(GM_W), cur(GM_W),
            pl.BlockSpec((8, BLOCK, BLOCK), lambda bi, n: (0, 0, 0)),
            pl.BlockSpec((BLOCK, 8), lambda bi, n: (0, 0)),
            pl.BlockSpec((D, D), lambda bi, n: (0, 0)),
            cur(D),
            pl.BlockSpec((1, 1, 6, D), lambda bi, n: (0, bi, 0, 0)),
        ],
        out_specs=cur(D),
        out_shape=jax.ShapeDtypeStruct((b, s, D), F32),
        compiler_params=_cparams(("arbitrary", "arbitrary"), VMEM_LIMIT),
        name="attn_gmlp_out",
    )(sink, q, k, k, k, ksw, ksw, ksw, v, v, v, vsw, vsw, vsw, kx, ksx, vx, vsx,
      u, vg, ws_bf, bs_t, wout_bf, x, mod)


def _ffn_kernel(x_ref, mod_ref, wg_ref, wu_ref, wd_ref, o_ref):
    mod = mod_ref[0, 0]
    xf = x_ref[...]
    h = _modulate(xf, mod[3:4], mod[4:5]).astype(BF16)
    g = jnp.dot(h, wg_ref[...], preferred_element_type=F32)
    up = jnp.dot(h, wu_ref[...], preferred_element_type=F32)
    a = (g * _sigmoid(g) * up).astype(BF16)
    o_ref[...] = xf + mod[5:6] * jnp.dot(a, wd_ref[...], preferred_element_type=F32)


def _dense_ffn(x2d, mod, wg, wu, wd, seq):
    n = x2d.shape[0]
    f = wg.shape[1]
    tm = TM_FFN
    per_b = seq // tm
    resident = lambda shp: pl.BlockSpec(shp, lambda i: (0, 0), pipeline_mode=pl.Buffered(1))
    return pl.pallas_call(
        _ffn_kernel,
        grid=(n // tm,),
        in_specs=[
            pl.BlockSpec((tm, D), lambda i: (i, 0)),
            pl.BlockSpec((1, 1, 6, D), lambda i: (0, i // per_b, 0, 0)),
            resident((D, f)), resident((D, f)), resident((f, D)),
        ],
        out_specs=pl.BlockSpec((tm, D), lambda i: (i, 0)),
        out_shape=jax.ShapeDtypeStruct((n, D), F32),
        compiler_params=_cparams(("arbitrary",), VMEM_LIMIT),
        name="dense_ffn",
    )(x2d, mod, wg, wu, wd)


def _pool_route_kernel(x_ref, xp_ref, xn_ref, mod_ref, band_ref, pw_ref, psc_ref, wr_hi_ref, wr_lo_ref,
                       tri_ref, x3_ref, h2_ref, route_ref, cnt_ref, hext, carry):
    bi = pl.program_id(0)
    i = pl.program_id(1)
    ni = pl.num_programs(1)
    tm = x_ref.shape[1]
    seq = tm * ni
    mod = mod_ref[0, 0]

    @pl.when((bi == 0) & (i == 0))
    def _():
        carry[...] = jnp.zeros_like(carry)

    xf = x_ref[0]
    hp = _modulate(xp_ref[0], mod[0:1], mod[1:2])
    hn = _modulate(xn_ref[0], mod[0:1], mod[1:2])
    hext[0:POOL_HALO] = jnp.where(i > 0, hp, 0.0).astype(BF16)
    h_main = _modulate(xf, mod[0:1], mod[1:2])
    hext[POOL_HALO:POOL_HALO + tm] = h_main.astype(BF16)
    hext[POOL_HALO + tm:] = jnp.where(i < ni - 1, hn, 0.0).astype(BF16)

    t_local = lax.broadcasted_iota(jnp.int32, (BLOCK, 1), 0)
    ys = []
    for gi, w in enumerate(POOL_SIZES):
        lo_off = -(w // 2)
        hi_off = w - 1 - w // 2
        cols = slice(gi * POOL_GD, (gi + 1) * POOL_GD)
        outs = []
        for sb in range(tm // BLOCK):
            r0 = sb * BLOCK
            win = jnp.dot(band_ref[gi], hext[r0:r0 + BLOCK + 2 * POOL_HALO, cols],
                          preferred_element_type=F32)
            t = i * tm + r0 + t_local
            cnt = (jnp.minimum(t + hi_off, seq - 1) - jnp.maximum(t + lo_off, 0) + 1).astype(F32)
            diff = win / cnt - h_main[r0:r0 + BLOCK, cols]
            outs.append(diff.astype(BF16))
        dg = jnp.concatenate(outs, axis=0)
        ys.append(jnp.dot(dg, pw_ref[gi], preferred_element_type=F32))
    y = jnp.concatenate(ys, axis=1) * psc_ref[...]
    x3 = xf + mod[2:3] * y
    x3_ref[0] = x3

    h2 = _modulate(x3, mod[3:4], mod[4:5])
    h2_ref[0] = h2
    h_hi = h2.astype(BF16)
    h_lo = (h2 - h_hi.astype(F32)).astype(BF16)
    logits = (jnp.dot(h_hi, wr_hi_ref[...], preferred_element_type=F32)
              + jnp.dot(h_hi, wr_lo_ref[...], preferred_element_type=F32)
              + jnp.dot(h_lo, wr_hi_ref[...], preferred_element_type=F32))
    lane = lax.broadcasted_iota(jnp.int32, (tm, LANES), 1)
    lane_f = lane.astype(F32)
    neg = -jnp.inf
    lg = jnp.where(lane < N_EXPERTS, logits, neg)
    m1 = jnp.max(lg, axis=-1, keepdims=True)
    i1 = jnp.min(jnp.where(lg == m1, lane_f, float(LANES)), axis=-1, keepdims=True)
    oh1 = lane_f == i1
    lg2 = jnp.where(oh1, neg, lg)
    m2 = jnp.max(lg2, axis=-1, keepdims=True)
    i2 = jnp.min(jnp.where(lg2 == m2, lane_f, float(LANES)), axis=-1, keepdims=True)
    oh2 = lane_f == i2
    e = jnp.exp(m2 - m1)
    w1 = 1.0 / (1.0 + e)
    w2 = e / (1.0 + e)
    oh = jnp.where(oh1 | oh2, 1.0, 0.0)
    before = jnp.dot(tri_ref[...], oh.astype(BF16), preferred_element_type=F32) + carry[...]
    r1 = jnp.sum(jnp.where(oh1, before, 0.0), axis=-1, keepdims=True)
    r2 = jnp.sum(jnp.where(oh2, before, 0.0), axis=-1, keepdims=True)
    carry[...] = carry[...] + jnp.sum(oh, axis=0, keepdims=True)
    cnt_ref[...] = carry[...]
    info = jnp.where(lane == 0, i1, jnp.where(lane == 1, i2, jnp.where(lane == 2, w1, jnp.where(
        lane == 3, w2, jnp.where(lane == 4, r1, jnp.where(lane == 5, r2, 0.0))))))
    route_ref[...] = info.T[0:8, :]


def _pool_route(x, mod, band, pw_bf, pool_scale, wr_hi, wr_lo, tri):
    b, s, _ = x.shape
    tm = TM_POOL
    ni = s // tm
    hb = tm // POOL_HALO
    row = pl.BlockSpec((1, tm, D), lambda bi, i: (bi, i, 0))
    const2 = lambda shp: pl.BlockSpec(shp, lambda bi, i: (0,) * len(shp))
    return pl.pallas_call(
        _pool_route_kernel,
        grid=(b, ni),
        in_specs=[
            row,
            pl.BlockSpec((1, POOL_HALO, D), lambda bi, i: (bi, jnp.maximum(i * hb - 1, 0), 0)),
            pl.BlockSpec((1, POOL_HALO, D), lambda bi, i: (bi, jnp.minimum((i + 1) * hb, s // POOL_HALO - 1), 0)),
            pl.BlockSpec((1, 1, 6, D), lambda bi, i: (1, bi, 0, 0)),
            const2(band.shape), const2(pw_bf.shape), const2((1, D)),
            const2(wr_hi.shape), const2(wr_lo.shape), const2(tri.shape),
        ],
        out_specs=[row, row,
                   pl.BlockSpec((8, tm), lambda bi, i: (0, bi * ni + i)),
                   pl.BlockSpec((1, LANES), lambda bi, i: (0, 0))],
        out_shape=[jax.ShapeDtypeStruct((b, s, D), F32), jax.ShapeDtypeStruct((b, s, D), F32),
                   jax.ShapeDtypeStruct((8, b * s), F32), jax.ShapeDtypeStruct((1, LANES), F32)],
        scratch_shapes=[pltpu.VMEM((tm + 2 * POOL_HALO, D), BF16), pltpu.VMEM((1, LANES), F32)],
        compiler_params=_cparams(("arbitrary", "arbitrary"), VMEM_LIMIT),
        name="pool_route",
    )(x, x, x, mod, band, pw_bf, pool_scale.reshape(1, D), wr_hi, wr_lo, tri)


def _moe_kernel(te_ref, nused_ref, fnext_ref, fprev_ref, f0_ref, h_hbm, wg_ref, wu_ref, wd_ref,
                y_hbm, xbuf, xb, acc, stage, gsem, ssem):
    i = pl.program_id(0)
    j = pl.program_id(1)
    nt = pl.num_programs(0)
    nj = pl.num_programs(1)
    used_tiles = nused_ref[0]
    tm = xb.shape[0]
    rows_per_step = tm // MOE_NJ
    tok_mask = h_hbm.shape[0] - 1

    def gather_row(fref, r, slot):
        tok = fref[0, 0, r] & tok_mask
        return pltpu.make_async_copy(h_hbm.at[pl.ds(tok, 1)], xbuf.at[slot, pl.ds(r, 1)], gsem.at[slot])

    def scatter_row(fref, r, slot):
        return pltpu.make_async_copy(stage.at[slot, pl.ds(r, 1)], y_hbm.at[pl.ds(fref[0, 0, r], 1)],
                                     ssem.at[slot])

    def gather_all(slot):
        return pltpu.make_async_copy(h_hbm.at[pl.ds(0, tm)], xbuf.at[slot], gsem.at[slot])

    def scatter_all(slot):
        return pltpu.make_async_copy(stage.at[slot], y_hbm.at[pl.ds(0, tm)], ssem.at[slot])

    cur = i % 2
    used = i < used_tiles

    @pl.when((i == 0) & (j == 0))
    def _():
        stage[...] = jnp.zeros_like(stage)
        spare = y_hbm.shape[0] - 2 * tm
        fills = [pltpu.make_async_copy(stage.at[sl], y_hbm.at[pl.ds(spare + sl * tm, tm)], ssem.at[sl])
                 for sl in range(2)]
        for cp in fills:
            cp.start()
        for cp in fills:
            cp.wait()

    @pl.when((i == 0) & (j == 0) & used)
    def _():
        def prime(r, c):
            gather_row(f0_ref, r, 0).start()
            return c
        lax.fori_loop(0, tm, prime, 0)

    @pl.when(used & (j == 0))
    def _():
        gather_all(cur).wait()
        xb[...] = xbuf[cur].astype(BF16)

    @pl.when((j == nj - 1) & (i >= 2) & (i - 2 < used_tiles))
    def _():
        scatter_all(cur).wait()

    def expert_ffn():
        xv = xb[...]
        g = jnp.dot(xv, wg_ref[0], preferred_element_type=F32)
        up = jnp.dot(xv, wu_ref[0], preferred_element_type=F32)
        a = (g * _sigmoid(g) * up).astype(BF16)
        return jnp.dot(a, wd_ref[0], preferred_element_type=F32)

    has_next = i + 1 < used_tiles
    has_prev = (i >= 1) & (i - 1 < used_tiles)
    steady = (i >= 2) & has_next

    for jv in range(MOE_NJ):
        @pl.when(steady & (j == jv))
        def _(jv=jv):
            for r in range(rows_per_step):
                gather_row(fnext_ref, jv * rows_per_step + r, 1 - cur).start()
                scatter_row(fprev_ref, jv * rows_per_step + r, 1 - cur).start()
            part = expert_ffn()
            if MOE_NJ == 1:
                stage[cur] = part
            elif jv == 0:
                acc[...] = part
            elif jv == MOE_NJ - 1:
                stage[cur] = acc[...] + part
            else:
                acc[...] += part

    @pl.when(jnp.logical_not(steady))
    def _():
        @pl.when(used)
        def _():
            part = expert_ffn()

            @pl.when((j == 0) & (nj > 1))
            def _():
                acc[...] = part

            @pl.when((j > 0) & (j < nj - 1))
            def _():
                acc[...] += part

            @pl.when((j == nj - 1) & (nj > 1))
            def _():
                stage[cur] = acc[...] + part

            @pl.when(nj == 1)
            def _():
                stage[cur] = part

        base = j * rows_per_step

        @pl.when(has_next)
        def _():
            def issue(r, c):
                gather_row(fnext_ref, base + r, 1 - cur).start()
                return c
            lax.fori_loop(0, rows_per_step, issue, 0)

        @pl.when(has_prev)
        def _():
            def issue(r, c):
                scatter_row(fprev_ref, base + r, 1 - cur).start()
                return c
            lax.fori_loop(0, rows_per_step, issue, 0)

    @pl.when((i == nt - 1) & (j == nj - 1) & (nt - 2 < used_tiles))
    def _():
        scatter_all(1 - cur).wait()


def _moe_experts(h2d, fmap, tile_expert, n_used, wg, wu, wd, n_tiles, y_rows):
    f = wg.shape[2]
    tm = TM_MOE
    nj = MOE_NJ
    tf = f // nj

    def jj(i, j, nu):
        return jnp.where(i < nu[0], j, nj - 1)

    fblk = lambda imap: pl.BlockSpec((1, 1, tm), imap, memory_space=pltpu.SMEM)
    return pl.pallas_call(
        _moe_kernel,
        grid_spec=pltpu.PrefetchScalarGridSpec(
            num_scalar_prefetch=2,
            grid=(n_tiles, nj),
            in_specs=[
                fblk(lambda i, j, te, nu: (jnp.minimum(i + 1, n_tiles - 1), 0, 0)),
                fblk(lambda i, j, te, nu: (jnp.maximum(i - 1, 0), 0, 0)),
                fblk(lambda i, j, te, nu: (0, 0, 0)),
                pl.BlockSpec(memory_space=pl.ANY),
                pl.BlockSpec((1, D, tf), lambda i, j, te, nu: (te[i], 0, jj(i, j, nu))),
                pl.BlockSpec((1, D, tf), lambda i, j, te, nu: (te[i], 0, jj(i, j, nu))),
                pl.BlockSpec((1, tf, D), lambda i, j, te, nu: (te[i], jj(i, j, nu), 0)),
            ],
            out_specs=pl.BlockSpec(memory_space=pl.ANY),
            scratch_shapes=[pltpu.VMEM((2, tm, D), F32), pltpu.VMEM((tm, D), BF16), pltpu.VMEM((tm, D), F32),
                            pltpu.VMEM((2, tm, D), F32), pltpu.SemaphoreType.DMA((2,)),
                            pltpu.SemaphoreType.DMA((2,))],
        ),
        out_shape=jax.ShapeDtypeStruct((y_rows, D), F32),
        compiler_params=_cparams(("arbitrary", "arbitrary"), VMEM_LIMIT),
        name="moe_experts",
    )(tile_expert, n_used, fmap, fmap, fmap, h2d, wg, wu, wd)


def _combine_kernel(y1_ref, y2_ref, x_ref, w_ref, mod_ref, gain_ref, o_ref):
    w = w_ref[...]
    moe = w[:, 0:1] * y1_ref[...] + w[:, 1:2] * y2_ref[...]
    mod = mod_ref[0, 0]
    x4 = x_ref[...] + mod[5:6] * moe
    ms = jnp.mean(x4 * x4, axis=-1, keepdims=True)
    o_ref[...] = x4 * lax.rsqrt(ms + EPS) * gain_ref[...]


def _combine(y, x3_2d, wts, mod, final_gain, seq):
    n = x3_2d.shape[0]
    tc = TC_COMB
    nt = n // tc
    per_b = seq // tc
    return pl.pallas_call(
        _combine_kernel,
        grid=(nt,),
        in_specs=[
            pl.BlockSpec((tc, D), lambda i: (i, 0)),
            pl.BlockSpec((tc, D), lambda i: (i + nt, 0)),
            pl.BlockSpec((tc, D), lambda i: (i, 0)),
            pl.BlockSpec((tc, 2), lambda i: (i, 0)),
            pl.BlockSpec((1, 1, 6, D), lambda i: (1, i // per_b, 0, 0)),
            pl.BlockSpec((1, D), lambda i: (0, 0)),
        ],
        out_specs=pl.BlockSpec((tc, D), lambda i: (i, 0)),
        out_shape=jax.ShapeDtypeStruct((n, D), F32),
        compiler_params=_cparams(("arbitrary",)),
        name="moe_combine",
    )(y, y, x3_2d, wts, mod, final_gain.reshape(1, D))


def _rope_tables(seq):
    rows = seq // GRID_W
    row_pos = jnp.repeat(jnp.arange(rows, dtype=F32), GRID_W)
    col_pos = jnp.tile(jnp.arange(GRID_W, dtype=F32), rows)
    axis_dim = HEAD_DIM // 2
    inv_freq = ROPE_BASE ** (-jnp.arange(0, axis_dim, 2, dtype=F32) / axis_dim)
    ar = row_pos[:, None] * inv_freq
    ac = col_pos[:, None] * inv_freq
    cos64 = jnp.concatenate([jnp.cos(ar), jnp.cos(ar), jnp.cos(ac), jnp.cos(ac)], axis=1)
    sin64 = jnp.concatenate([-jnp.sin(ar), jnp.sin(ar), -jnp.sin(ac), jnp.sin(ac)], axis=1)
    return jnp.tile(cos64, (1, 2)), jnp.tile(sin64, (1, 2))


def _band_matrices():
    r = np.arange(BLOCK)[:, None]
    c = np.arange(BLOCK + 2 * POOL_HALO)[None, :] - POOL_HALO
    mats = []
    for w in POOL_SIZES:
        lo = -(w // 2)
        hi = w - 1 - w // 2
        mats.append(((c >= r + lo) & (c <= r + hi)).astype(np.float32))
    return jnp.asarray(np.stack(mats), dtype=BF16)


def kernel(x, c, ctx, c_ctx, w_ada, b_ada, w_in, attn_sink, gm_gain, gm_w_s, gm_b_s, w_out,
           ffn_w_gate, ffn_w_up, ffn_w_down, pool_w, pool_scale, router_w,
           moe_w_gate, moe_w_up, moe_w_down, final_gain):
    b, s, _ = x.shape
    n = b * s
    assert w_ada.shape[0] == 2 and w_in.shape[0] == 1 and pool_w.shape[0] == 1
    assert s % TM_IN == 0 and s % TM_POOL == 0 and s % TM_FFN == 0 and b <= 4
    assert n & (n - 1) == 0 and TM_MOE % MOE_NJ == 0

    cvec = jnp.concatenate([c, c_ctx[None, :], jnp.zeros((8 - b - 1, D), F32)], axis=0)
    mod = _ada_mod(cvec, w_ada, b_ada)

    cos_t, sin_t = _rope_tables(s)
    w_in_bf = w_in[0].astype(BF16)
    q, k, ksw, v, vsw, u, vg = _in_proj(x, mod, w_in_bf, gm_gain[0], cos_t, sin_t)
    kx, ksx, vx, vsx = _ctx_kv(ctx, mod, w_in_bf)
    x1 = _attn_mixer(x, mod, attn_sink[0], q, k, ksw, v, vsw, kx, ksx, vx, vsx, u, vg,
                     gm_w_s[0].astype(BF16), gm_b_s[0].T, w_out[0].astype(BF16))
    x2 = _dense_ffn(x1.reshape(n, D), mod, ffn_w_gate[0].astype(BF16), ffn_w_up[0].astype(BF16),
                    ffn_w_down[0].astype(BF16), s)

    wr = jnp.pad(router_w[0], ((0, 0), (0, LANES - N_EXPERTS)))
    wr_hi = wr.astype(BF16)
    wr_lo = (wr - wr_hi.astype(F32)).astype(BF16)
    tri = jnp.asarray(np.tril(np.ones((TM_POOL, TM_POOL), np.float32), -1), dtype=BF16)
    x3, h2, route, counts = _pool_route(x2.reshape(b, s, D), mod, _band_matrices(), pool_w[0].astype(BF16),
                                        pool_scale[0], wr_hi, wr_lo, tri)

    tm = TM_MOE
    n_tiles = (2 * n) // tm + N_EXPERTS
    cnt = counts[0, :N_EXPERTS].astype(jnp.int32)
    tiles_e = (cnt + tm - 1) // tm
    tile_end = jnp.cumsum(tiles_e)
    off = (tile_end - tiles_e) * tm
    n_used = tile_end[-1]
    tix = jnp.arange(n_tiles, dtype=jnp.int32)
    te = jnp.minimum(jnp.searchsorted(tile_end, tix, side="right"), N_EXPERTS - 1).astype(jnp.int32)
    te_last = te[jnp.maximum(n_used - 1, 0)]
    tile_expert = jnp.where(tix < n_used, te, te_last)
    e1 = route[0].astype(jnp.int32)
    e2 = route[1].astype(jnp.int32)
    pos1 = off[e1] + route[4].astype(jnp.int32)
    pos2 = off[e2] + route[5].astype(jnp.int32)
    slot = jnp.arange(n_tiles * tm, dtype=jnp.int32)
    tok = jnp.arange(n, dtype=jnp.int32)
    fmap = (2 * n + slot % (2 * tm)).at[jnp.concatenate([pos1, pos2])].set(
        jnp.concatenate([tok, n + tok]), unique_indices=True)
    n_used_arr = n_used.reshape(1).astype(jnp.int32)

    y = _moe_experts(h2.reshape(n, D), fmap.reshape(n_tiles, 1, tm), tile_expert, n_used_arr,
                     moe_w_gate[0].astype(BF16), moe_w_up[0].astype(BF16), moe_w_down[0].astype(BF16),
                     n_tiles, 2 * n + 2 * tm)
    out = _combine(y, x3.reshape(n, D), route[2:4].T, mod, final_gain, s)
    return out.reshape(b, s, D)
```

```python
import functools

import numpy as np
import jax
import jax.numpy as jnp
from jax import lax
from jax.experimental import pallas as pl
from jax.experimental.pallas import tpu as pltpu

F32 = jnp.float32
BF16 = jnp.bfloat16

D = 1024
GRID_W = 64
EPS = 1e-6
NEG_INF = -1e30
HEAD_DIM = 64
N_Q_HEADS = 8
BLOCK = 128
ATT_W = 512
KV_W = 128
GM_W = 512
IN_W = 1792
POOL_SIZES = (2, 4, 8, 16)
POOL_GD = 256
POOL_HALO = 16
N_EXPERTS = 8
ROPE_BASE = 10000.0
LANES = 128
SQRT_2_OVER_PI = 0.7978845608028654

TM_IN = 512
TM_FFN = 512
TM_POOL = 512
TM_MOE = 512
MOE_NJ = 2
TC_COMB = 512
Q_BLOCKS = 4
VMEM_LIMIT = 56 * 1024 * 1024


def _cparams(sem, vmem=None):
    return pltpu.CompilerParams(dimension_semantics=sem, vmem_limit_bytes=vmem)


def _modulate(xf, shift, scale):
    ms = jnp.mean(xf * xf, axis=-1, keepdims=True)
    return xf * lax.rsqrt(ms + EPS) * (1.0 + scale) + shift


def _sigmoid(z):
    return 1.0 / (1.0 + jnp.exp(-z))


def _ada_kernel(c_ref, w_ref, b_ref, o_ref):
    c = c_ref[...]
    s = c * _sigmoid(c)
    o_ref[0] = jnp.dot(s, w_ref[0], precision=lax.Precision.HIGHEST,
                       preferred_element_type=F32) + b_ref[0]


def _ada_mod(cvec, w_ada, b_ada):
    depth, _, n6 = w_ada.shape
    tn = 1536
    out = pl.pallas_call(
        _ada_kernel,
        grid=(depth, n6 // tn),
        in_specs=[
            pl.BlockSpec((8, D), lambda l, j: (0, 0)),
            pl.BlockSpec((1, D, tn), lambda l, j: (l, 0, j)),
            pl.BlockSpec((1, 1, tn), lambda l, j: (l, 0, j)),
        ],
        out_specs=pl.BlockSpec((1, 8, tn), lambda l, j: (l, 0, j)),
        out_shape=jax.ShapeDtypeStruct((depth, 8, n6), F32),
        compiler_params=_cparams(("arbitrary", "arbitrary")),
        name="ada_mod",
    )(cvec, w_ada, b_ada.reshape(depth, 1, n6))
    return out.reshape(depth, 8, 6, D)


def _rope(t, cs, sn, first_half):
    fwd = pltpu.roll(t, LANES - 16, axis=1)
    bwd = pltpu.roll(t, 16, axis=1)
    return t * cs + jnp.where(first_half, fwd, bwd) * sn


def _inproj_kernel(x_ref, mod_ref, w_ref, gain_ref, cos_ref, sin_ref,
                   q_ref, k_ref, ksw_ref, v_ref, vsw_ref, u_ref, vg_ref):
    mod = mod_ref[0, 0]
    h = _modulate(x_ref[0], mod[0:1], mod[1:2]).astype(BF16)
    proj = jnp.dot(h, w_ref[...], preferred_element_type=F32)
    cs = cos_ref[...]
    sn = sin_ref[...]
    lane = lax.broadcasted_iota(jnp.int32, cs.shape, 1)
    first_half = (lane & 16) == 0
    for cix in range(ATT_W // LANES):
        t = proj[:, cix * LANES:(cix + 1) * LANES]
        q_ref[0, :, cix * LANES:(cix + 1) * LANES] = (
            _rope(t, cs, sn, first_half) * (HEAD_DIM ** -0.5)).astype(BF16)
    kr = _rope(proj[:, ATT_W:ATT_W + KV_W], cs, sn, first_half)
    k_ref[0] = kr.astype(BF16)
    ksw_ref[0] = pltpu.roll(kr, HEAD_DIM, axis=1).astype(BF16)
    vv = proj[:, ATT_W + KV_W:ATT_W + 2 * KV_W]
    v_ref[0] = vv.astype(BF16)
    vsw_ref[0] = pltpu.roll(vv, HEAD_DIM, axis=1).astype(BF16)
    z = proj[:, ATT_W + 2 * KV_W:]
    g = z * (0.5 * (1.0 + jnp.tanh(SQRT_2_OVER_PI * (z + 0.044715 * (z * z * z)))))
    u_ref[0] = g[:, :GM_W].astype(BF16)
    vg = g[:, GM_W:]
    ms = jnp.mean(vg * vg, axis=-1, keepdims=True)
    vg_ref[0] = (vg * lax.rsqrt(ms + EPS) * gain_ref[...]).astype(BF16)


def _in_proj(x, mod, w_in_bf, gm_gain, cos_t, sin_t):
    b, s, _ = x.shape
    tm = TM_IN
    row = lambda w: pl.BlockSpec((1, tm, w), lambda bi, i: (bi, i, 0))
    outs = pl.pallas_call(
        _inproj_kernel,
        grid=(b, s // tm),
        in_specs=[
            row(D),
            pl.BlockSpec((1, 1, 6, D), lambda bi, i: (0, bi, 0, 0)),
            pl.BlockSpec((D, IN_W), lambda bi, i: (0, 0)),
            pl.BlockSpec((1, GM_W), lambda bi, i: (0, 0)),
            pl.BlockSpec((tm, LANES), lambda bi, i: (i, 0)),
            pl.BlockSpec((tm, LANES), lambda bi, i: (i, 0)),
        ],
        out_specs=[row(ATT_W), row(KV_W), row(KV_W), row(KV_W), row(KV_W), row(GM_W), row(GM_W)],
        out_shape=[jax.ShapeDtypeStruct((b, s, w), BF16)
                   for w in (ATT_W, KV_W, KV_W, KV_W, KV_W, GM_W, GM_W)],
        compiler_params=_cparams(("arbitrary", "arbitrary"), VMEM_LIMIT),
        name="in_proj",
    )(x, mod, w_in_bf, gm_gain.reshape(1, GM_W), cos_t, sin_t)
    return outs


def _ctx_kernel(c_ref, mod_ref, w_ref, k_ref, ksw_ref, v_ref, vsw_ref):
    mod = mod_ref[0, 0]
    h = _modulate(c_ref[0], mod[0:1], mod[1:2]).astype(BF16)
    kv = jnp.dot(h, w_ref[...], preferred_element_type=F32)
    kk = kv[:, :KV_W]
    vv = kv[:, KV_W:]
    k_ref[0] = kk.astype(BF16)
    ksw_ref[0] = pltpu.roll(kk, HEAD_DIM, axis=1).astype(BF16)
    v_ref[0] = vv.astype(BF16)
    vsw_ref[0] = pltpu.roll(vv, HEAD_DIM, axis=1).astype(BF16)


def _ctx_kv(ctx, mod, w_in_bf):
    b, l, _ = ctx.shape
    spec = pl.BlockSpec((1, l, KV_W), lambda bi: (bi, 0, 0))
    return pl.pallas_call(
        _ctx_kernel,
        grid=(b,),
        in_specs=[
            pl.BlockSpec((1, l, D), lambda bi: (bi, 0, 0)),
            pl.BlockSpec((1, 1, 6, D), lambda bi: (0, b, 0, 0)),
            pl.BlockSpec((D, 2 * KV_W), lambda bi: (0, ATT_W // (2 * KV_W))),
        ],
        out_specs=[spec] * 4,
        out_shape=[jax.ShapeDtypeStruct((b, l, KV_W), BF16)] * 4,
        compiler_params=_cparams(("arbitrary",)),
        name="ctx_kv",
    )(ctx, mod, w_in_bf)


def _attn_kernel(sink_ref, q_ref, kp_ref, kc_ref, kn_ref, ksp_ref, ksc_ref, ksn_ref,
                 vp_ref, vc_ref, vn_ref, vsp_ref, vsc_ref, vsn_ref,
                 kx_ref, ksx_ref, vx_ref, vsx_ref,
                 u_ref, vg_ref, wcat_ref, bs_ref, wout_ref, x_ref, mod_ref, o_ref):
    n = pl.program_id(1)
    nblk = pl.num_programs(1) * Q_BLOCKS
    lane = lax.broadcasted_iota(jnp.int32, (1, LANES), 1)
    low = lane < HEAD_DIM
    zero = jnp.zeros((), BF16)

    def variants(a0, a1):
        return ((jnp.where(low, a0, zero), jnp.where(low, zero, a1)),
                (jnp.where(low, a1, zero), jnp.where(low, zero, a0)))

    cat = lambda refs: jnp.concatenate([r[0] for r in refs], axis=0)
    kb_var = variants(cat((kp_ref, kc_ref, kn_ref)), cat((ksp_ref, ksc_ref, ksn_ref)))
    vb_var = variants(cat((vp_ref, vc_ref, vn_ref)), cat((vsp_ref, vsc_ref, vsn_ref)))
    kx_var = variants(kx_ref[0], ksx_ref[0])
    vx_var = variants(vx_ref[0], vsx_ref[0])

    row = lax.broadcasted_iota(jnp.int32, (2 * BLOCK, BLOCK), 0) & (BLOCK - 1)
    col = lax.broadcasted_iota(jnp.int32, (2 * BLOCK, BLOCK), 1)
    top = lax.broadcasted_iota(jnp.int32, (2 * BLOCK, 1), 0) < BLOCK
    nt_dims = (((1,), (1,)), ((), ()))

    q = q_ref[0]
    att_blocks = [[None] * 4 for _ in range(Q_BLOCKS)]
    for kvh in range(2):
        qst = jnp.concatenate(
            [q[qb * BLOCK:(qb + 1) * BLOCK, pr * LANES:(pr + 1) * LANES]
             for qb in range(Q_BLOCKS) for pr in (2 * kvh, 2 * kvh + 1)], axis=0)
        accs = [None] * Q_BLOCKS
        for half in range(2):
            sk = jnp.where(top, sink_ref[4 * kvh + half], sink_ref[4 * kvh + 2 + half])
            s_ctx = lax.dot_general(qst, kx_var[kvh][half], nt_dims, preferred_element_type=F32)
            for qb in range(Q_BLOCKS):
                g = n * Q_BLOCKS + qb
                qrows = qst[qb * 2 * BLOCK:(qb + 1) * 2 * BLOCK]
                sb = lax.dot_general(qrows, kb_var[kvh][half][qb * BLOCK:(qb + 3) * BLOCK], nt_dims,
                                     preferred_element_type=F32)
                s0 = jnp.where((col >= row) & (g > 0), sb[:, :BLOCK], NEG_INF)
                s1 = sb[:, BLOCK:2 * BLOCK]
                s2 = jnp.where((col <= row) & (g < nblk - 1), sb[:, 2 * BLOCK:], NEG_INF)
                sc = s_ctx[qb * 2 * BLOCK:(qb + 1) * 2 * BLOCK]
                m = jnp.maximum(jnp.maximum(s0, s1), s2)
                m = jnp.maximum(jnp.max(m, axis=-1, keepdims=True), jnp.max(sc, axis=-1, keepdims=True))
                m = jnp.maximum(m, sk)
                p0, p1, p2, pc = (jnp.exp(t - m) for t in (s0, s1, s2, sc))
                den = (jnp.sum(p0 + p1 + p2, axis=-1, keepdims=True) + jnp.sum(pc, axis=-1, keepdims=True)
                       + jnp.exp(sk - m))
                pb = jnp.concatenate([p0, p1, p2], axis=1).astype(BF16)
                o = (jnp.dot(pb, vb_var[kvh][half][qb * BLOCK:(qb + 3) * BLOCK], preferred_element_type=F32)
                     + jnp.dot(pc.astype(BF16), vx_var[kvh][half], preferred_element_type=F32))
                o = o / den
                accs[qb] = o if accs[qb] is None else accs[qb] + o
        for qb in range(Q_BLOCKS):
            att_blocks[qb][2 * kvh] = accs[qb][:BLOCK]
            att_blocks[qb][2 * kvh + 1] = accs[qb][BLOCK:]

    u = u_ref[0]
    vg = vg_ref[0]
    bs = bs_ref[...]
    gm_blocks = [[None] * 4 for _ in range(Q_BLOCKS)]
    for j in range(GM_W // LANES):
        chunks = [vg[c * BLOCK:(c + 1) * BLOCK, j * LANES:(j + 1) * LANES] for c in range(Q_BLOCKS)]
        rhs = jnp.concatenate(
            [jnp.concatenate([jnp.where(low, v, zero) for v in chunks], axis=1),
             jnp.concatenate([jnp.where(low, zero, v) for v in chunks], axis=1)], axis=0)
        mixed = jnp.dot(wcat_ref[j], rhs, preferred_element_type=F32)
        bias = jnp.where(low, bs[:, 2 * j:2 * j + 1], bs[:, 2 * j + 1:2 * j + 2])
        for c in range(Q_BLOCKS):
            gm_blocks[c][j] = (u[c * BLOCK:(c + 1) * BLOCK, j * LANES:(j + 1) * LANES].astype(F32)
                               * (mixed[:, c * LANES:(c + 1) * LANES] + bias))

    mix = jnp.concatenate([jnp.concatenate(att_blocks[c] + gm_blocks[c], axis=1) for c in range(Q_BLOCKS)],
                          axis=0).astype(BF16)
    y = jnp.dot(mix, wout_ref[...], preferred_element_type=F32)
    mod = mod_ref[0, 0]
    o_ref[0] = x_ref[0] + mod[2:3] * y


def _attn_mixer(x, mod, sink, q, k, ksw, v, vsw, kx, ksx, vx, vsx, u, vg, wcat_bf, bs_t, wout_bf):
    b, s, _ = x.shape
    tq = Q_BLOCKS * BLOCK
    nb = s // BLOCK
    l = kx.shape[1]
    cur = lambda w: pl.BlockSpec((1, tq, w), lambda bi, n: (bi, n, 0))
    prv = lambda w: pl.BlockSpec((1, BLOCK, w), lambda bi, n: (bi, jnp.maximum(n * Q_BLOCKS - 1, 0), 0))
    nxt = lambda w: pl.BlockSpec((1, BLOCK, w), lambda bi, n: (bi, jnp.minimum((n + 1) * Q_BLOCKS, nb - 1), 0))
    cx = pl.BlockSpec((1, l, KV_W), lambda bi, n: (bi, 0, 0))
    return pl.pallas_call(
        _attn_kernel,
        grid=(b, s // tq),
        in_specs=[
            pl.BlockSpec(memory_space=pltpu.SMEM),
            cur(ATT_W),
            prv(KV_W), cur(KV_W), nxt(KV_W), prv(KV_W), cur(KV_W), nxt(KV_W),
            prv(KV_W), cur(KV_W), nxt(KV_W), prv(KV_W), cur(KV_W), nxt(KV_W),
            cx, cx, cx, cx,
            cur(GM_W), cur(GM_W),
            pl.BlockSpec((4, BLOCK, 2 * BLOCK), lambda bi, n: (0, 0, 0)),
            pl.BlockSpec((BLOCK, 8), lambda bi, n: (0, 0)),
            pl.BlockSpec((D, D), lambda bi, n: (0, 0)),
            cur(D),
            pl.BlockSpec((1, 1, 6, D), lambda bi, n: (0, bi, 0, 0)),
        ],
        out_specs=cur(D),
        out_shape=jax.ShapeDtypeStruct((b, s, D), F32),
        compiler_params=_cparams(("arbitrary", "arbitrary"), VMEM_LIMIT),
        name="attn_gmlp_out",
    )(sink, q, k, k, k, ksw, ksw, ksw, v, v, v, vsw, vsw, vsw, kx, ksx, vx, vsx,
      u, vg, wcat_bf, bs_t, wout_bf, x, mod)


def _ffn_kernel(x_ref, mod_ref, wg_ref, wu_ref, wd_ref, o_ref):
    mod = mod_ref[0, 0]
    xf = x_ref[...]
    h = _modulate(xf, mod[3:4], mod[4:5]).astype(BF16)
    g = jnp.dot(h, wg_ref[...], preferred_element_type=F32)
    up = jnp.dot(h, wu_ref[...], preferred_element_type=F32)
    a = (g * _sigmoid(g) * up).astype(BF16)
    o_ref[...] = xf + mod[5:6] * jnp.dot(a, wd_ref[...], preferred_element_type=F32)


def _dense_ffn(x2d, mod, wg, wu, wd, seq):
    n = x2d.shape[0]
    f = wg.shape[1]
    tm = TM_FFN
    per_b = seq // tm
    resident = lambda shp: pl.BlockSpec(shp, lambda i: (0, 0), pipeline_mode=pl.Buffered(1))
    return pl.pallas_call(
        _ffn_kernel,
        grid=(n // tm,),
        in_specs=[
            pl.BlockSpec((tm, D), lambda i: (i, 0)),
            pl.BlockSpec((1, 1, 6, D), lambda i: (0, i // per_b, 0, 0)),
            resident((D, f)), resident((D, f)), resident((f, D)),
        ],
        out_specs=pl.BlockSpec((tm, D), lambda i: (i, 0)),
        out_shape=jax.ShapeDtypeStruct((n, D), F32),
        compiler_params=_cparams(("arbitrary",), VMEM_LIMIT),
        name="dense_ffn",
    )(x2d, mod, wg, wu, wd)


def _pool_route_kernel(x_ref, xp_ref, xn_ref, mod_ref, band_ref, pw_ref, psc_ref, wr_hi_ref, wr_lo_ref,
                       tri_ref, x3_ref, h2_ref, route_ref, cnt_ref, hext, carry):
    bi = pl.program_id(0)
    i = pl.program_id(1)
    ni = pl.num_programs(1)
    tm = x_ref.shape[1]
    seq = tm * ni
    mod = mod_ref[0, 0]

    @pl.when((bi == 0) & (i == 0))
    def _():
        carry[...] = jnp.zeros_like(carry)

    xf = x_ref[0]
    hp = _modulate(xp_ref[0], mod[0:1], mod[1:2])
    hn = _modulate(xn_ref[0], mod[0:1], mod[1:2])
    hext[0:POOL_HALO] = jnp.where(i > 0, hp, 0.0).astype(BF16)
    h_main = _modulate(xf, mod[0:1], mod[1:2])
    hext[POOL_HALO:POOL_HALO + tm] = h_main.astype(BF16)
    hext[POOL_HALO + tm:] = jnp.where(i < ni - 1, hn, 0.0).astype(BF16)

    t_local = lax.broadcasted_iota(jnp.int32, (BLOCK, 1), 0)
    ys = []
    for gi, w in enumerate(POOL_SIZES):
        lo_off = -(w // 2)
        hi_off = w - 1 - w // 2
        cols = slice(gi * POOL_GD, (gi + 1) * POOL_GD)
        outs = []
        for sb in range(tm // BLOCK):
            r0 = sb * BLOCK
            win = jnp.dot(band_ref[gi], hext[r0:r0 + BLOCK + 2 * POOL_HALO, cols],
                          preferred_element_type=F32)
            t = i * tm + r0 + t_local
            cnt = (jnp.minimum(t + hi_off, seq - 1) - jnp.maximum(t + lo_off, 0) + 1).astype(F32)
            diff = win / cnt - h_main[r0:r0 + BLOCK, cols]
            outs.append(diff.astype(BF16))
        dg = jnp.concatenate(outs, axis=0)
        ys.append(jnp.dot(dg, pw_ref[gi], preferred_element_type=F32))
    y = jnp.concatenate(ys, axis=1) * psc_ref[...]
    x3 = xf + mod[2:3] * y
    x3_ref[0] = x3

    h2 = _modulate(x3, mod[3:4], mod[4:5])
    h2_ref[0] = h2
    h_hi = h2.astype(BF16)
    h_lo = (h2 - h_hi.astype(F32)).astype(BF16)
    logits = (jnp.dot(h_hi, wr_hi_ref[...], preferred_element_type=F32)
              + jnp.dot(h_hi, wr_lo_ref[...], preferred_element_type=F32)
              + jnp.dot(h_lo, wr_hi_ref[...], preferred_element_type=F32))
    lane = lax.broadcasted_iota(jnp.int32, (tm, LANES), 1)
    lane_f = lane.astype(F32)
    neg = -jnp.inf
    lg = jnp.where(lane < N_EXPERTS, logits, neg)
    m1 = jnp.max(lg, axis=-1, keepdims=True)
    i1 = jnp.min(jnp.where(lg == m1, lane_f, float(LANES)), axis=-1, keepdims=True)
    oh1 = lane_f == i1
    lg2 = jnp.where(oh1, neg, lg)
    m2 = jnp.max(lg2, axis=-1, keepdims=True)
    i2 = jnp.min(jnp.where(lg2 == m2, lane_f, float(LANES)), axis=-1, keepdims=True)
    oh2 = lane_f == i2
    e = jnp.exp(m2 - m1)
    w1 = 1.0 / (1.0 + e)
    w2 = e / (1.0 + e)
    oh = jnp.where(oh1 | oh2, 1.0, 0.0)
    before = jnp.dot(tri_ref[...], oh.astype(BF16), preferred_element_type=F32) + carry[...]
    r1 = jnp.sum(jnp.where(oh1, before, 0.0), axis=-1, keepdims=True)
    r2 = jnp.sum(jnp.where(oh2, before, 0.0), axis=-1, keepdims=True)
    carry[...] = carry[...] + jnp.sum(oh, axis=0, keepdims=True)
    cnt_ref[...] = carry[...]
    info = jnp.where(lane == 0, i1, jnp.where(lane == 1, i2, jnp.where(lane == 2, w1, jnp.where(
        lane == 3, w2, jnp.where(lane == 4, r1, jnp.where(lane == 5, r2, 0.0))))))
    route_ref[...] = info.T[0:8, :]


def _pool_route(x, mod, band, pw_bf, pool_scale, wr_hi, wr_lo, tri):
    b, s, _ = x.shape
    tm = TM_POOL
    ni = s // tm
    hb = tm // POOL_HALO
    row = pl.BlockSpec((1, tm, D), lambda bi, i: (bi, i, 0))
    const2 = lambda shp: pl.BlockSpec(shp, lambda bi, i: (0,) * len(shp))
    return pl.pallas_call(
        _pool_route_kernel,
        grid=(b, ni),
        in_specs=[
            row,
            pl.BlockSpec((1, POOL_HALO, D), lambda bi, i: (bi, jnp.maximum(i * hb - 1, 0), 0)),
            pl.BlockSpec((1, POOL_HALO, D), lambda bi, i: (bi, jnp.minimum((i + 1) * hb, s // POOL_HALO - 1), 0)),
            pl.BlockSpec((1, 1, 6, D), lambda bi, i: (1, bi, 0, 0)),
            const2(band.shape), const2(pw_bf.shape), const2((1, D)),
            const2(wr_hi.shape), const2(wr_lo.shape), const2(tri.shape),
        ],
        out_specs=[row, row,
                   pl.BlockSpec((8, tm), lambda bi, i: (0, bi * ni + i)),
                   pl.BlockSpec((1, LANES), lambda bi, i: (0, 0))],
        out_shape=[jax.ShapeDtypeStruct((b, s, D), F32), jax.ShapeDtypeStruct((b, s, D), F32),
                   jax.ShapeDtypeStruct((8, b * s), F32), jax.ShapeDtypeStruct((1, LANES), F32)],
        scratch_shapes=[pltpu.VMEM((tm + 2 * POOL_HALO, D), BF16), pltpu.VMEM((1, LANES), F32)],
        compiler_params=_cparams(("arbitrary", "arbitrary"), VMEM_LIMIT),
        name="pool_route",
    )(x, x, x, mod, band, pw_bf, pool_scale.reshape(1, D), wr_hi, wr_lo, tri)


def _slotmap_kernel(pos_ref, lo_ref, hi_ref, o_ref):
    n_pairs = pos_ref.shape[0]
    spare_mask = 2 * TM_MOE - 1
    for e in range(lo_ref.shape[0]):
        def fill(p, c):
            o_ref[p] = n_pairs + (p & spare_mask)
            return c
        lax.fori_loop(lo_ref[e], hi_ref[e], fill, 0)

    def place(f, c):
        o_ref[pos_ref[f]] = f
        return c
    lax.fori_loop(0, n_pairs, place, 0, unroll=8)


def _slot_map(pos_flat, lo, hi, n_slots):
    smem = pl.BlockSpec(memory_space=pltpu.SMEM)
    return pl.pallas_call(
        _slotmap_kernel,
        in_specs=[smem, smem, smem],
        out_specs=smem,
        out_shape=jax.ShapeDtypeStruct((n_slots,), jnp.int32),
        name="moe_slot_map",
    )(pos_flat, lo, hi)


def _moe_kernel(te_ref, nused_ref, fnext_ref, fprev_ref, f0_ref, h_hbm, wg_ref, wu_ref, wd_ref,
                y_hbm, xbuf, xb, acc, stage, gsem, ssem):
    i = pl.program_id(0)
    j = pl.program_id(1)
    nt = pl.num_programs(0)
    nj = pl.num_programs(1)
    used_tiles = nused_ref[0]
    tm = xb.shape[0]
    rows_per_step = tm // MOE_NJ
    tok_mask = h_hbm.shape[0] - 1

    def gather_row(fref, r, slot):
        tok = fref[0, 0, r] & tok_mask
        return pltpu.make_async_copy(h_hbm.at[pl.ds(tok, 1)], xbuf.at[slot, pl.ds(r, 1)], gsem.at[slot])

    def scatter_row(fref, r, slot):
        return pltpu.make_async_copy(stage.at[slot, pl.ds(r, 1)], y_hbm.at[pl.ds(fref[0, 0, r], 1)],
                                     ssem.at[slot])

    def gather_all(slot):
        return pltpu.make_async_copy(h_hbm.at[pl.ds(0, tm)], xbuf.at[slot], gsem.at[slot])

    def scatter_all(slot):
        return pltpu.make_async_copy(stage.at[slot], y_hbm.at[pl.ds(0, tm)], ssem.at[slot])

    cur = i % 2
    used = i < used_tiles

    @pl.when((i == 0) & (j == 0))
    def _():
        stage[...] = jnp.zeros_like(stage)
        spare = y_hbm.shape[0] - 2 * tm
        fills = [pltpu.make_async_copy(stage.at[sl], y_hbm.at[pl.ds(spare + sl * tm, tm)], ssem.at[sl])
                 for sl in range(2)]
        for cp in fills:
            cp.start()
        for cp in fills:
            cp.wait()

    @pl.when((i == 0) & (j == 0) & used)
    def _():
        def prime(r, c):
            gather_row(f0_ref, r, 0).start()
            return c
        lax.fori_loop(0, tm, prime, 0)

    @pl.when(used & (j == 0))
    def _():
        gather_all(cur).wait()
        xb[...] = xbuf[cur].astype(BF16)

    @pl.when((j == nj - 1) & (i >= 2) & (i - 2 < used_tiles))
    def _():
        scatter_all(cur).wait()

    def expert_ffn():
        xv = xb[...]
        g = jnp.dot(xv, wg_ref[0], preferred_element_type=F32)
        up = jnp.dot(xv, wu_ref[0], preferred_element_type=F32)
        a = (g * _sigmoid(g) * up).astype(BF16)
        return jnp.dot(a, wd_ref[0], preferred_element_type=F32)

    has_next = i + 1 < used_tiles
    has_prev = (i >= 1) & (i - 1 < used_tiles)
    steady = (i >= 2) & has_next

    for jv in range(MOE_NJ):
        @pl.when(steady & (j == jv))
        def _(jv=jv):
            for r in range(rows_per_step):
                gather_row(fnext_ref, jv * rows_per_step + r, 1 - cur).start(priority=1)
                scatter_row(fprev_ref, jv * rows_per_step + r, 1 - cur).start(priority=1)
            part = expert_ffn()
            if MOE_NJ == 1:
                stage[cur] = part
            elif jv == 0:
                acc[...] = part
            elif jv == MOE_NJ - 1:
                stage[cur] = acc[...] + part
            else:
                acc[...] += part

    @pl.when(jnp.logical_not(steady))
    def _():
        @pl.when(used)
        def _():
            part = expert_ffn()

            @pl.when((j == 0) & (nj > 1))
            def _():
                acc[...] = part

            @pl.when((j > 0) & (j < nj - 1))
            def _():
                acc[...] += part

            @pl.when((j == nj - 1) & (nj > 1))
            def _():
                stage[cur] = acc[...] + part

            @pl.when(nj == 1)
            def _():
                stage[cur] = part

        base = j * rows_per_step

        @pl.when(has_next)
        def _():
            def issue(r, c):
                gather_row(fnext_ref, base + r, 1 - cur).start()
                return c
            lax.fori_loop(0, rows_per_step, issue, 0)

        @pl.when(has_prev)
        def _():
            def issue(r, c):
                scatter_row(fprev_ref, base + r, 1 - cur).start()
                return c
            lax.fori_loop(0, rows_per_step, issue, 0)

    @pl.when((i == nt - 1) & (j == nj - 1) & (nt - 2 < used_tiles))
    def _():
        scatter_all(1 - cur).wait()


def _moe_experts(h2d, fmap, tile_expert, n_used, wg, wu, wd, n_tiles, y_rows):
    f = wg.shape[2]
    tm = TM_MOE
    nj = MOE_NJ
    tf = f // nj

    def jj(i, j, nu):
        return jnp.where(i < nu[0], j, nj - 1)

    fblk = lambda imap: pl.BlockSpec((1, 1, tm), imap, memory_space=pltpu.SMEM)
    return pl.pallas_call(
        _moe_kernel,
        grid_spec=pltpu.PrefetchScalarGridSpec(
            num_scalar_prefetch=2,
            grid=(n_tiles, nj),
            in_specs=[
                fblk(lambda i, j, te, nu: (jnp.minimum(i + 1, n_tiles - 1), 0, 0)),
                fblk(lambda i, j, te, nu: (jnp.maximum(i - 1, 0), 0, 0)),
                fblk(lambda i, j, te, nu: (0, 0, 0)),
                pl.BlockSpec(memory_space=pl.ANY),
                pl.BlockSpec((1, D, tf), lambda i, j, te, nu: (te[i], 0, jj(i, j, nu))),
                pl.BlockSpec((1, D, tf), lambda i, j, te, nu: (te[i], 0, jj(i, j, nu))),
                pl.BlockSpec((1, tf, D), lambda i, j, te, nu: (te[i], jj(i, j, nu), 0)),
            ],
            out_specs=pl.BlockSpec(memory_space=pl.ANY),
            scratch_shapes=[pltpu.VMEM((2, tm, D), F32), pltpu.VMEM((tm, D), BF16), pltpu.VMEM((tm, D), F32),
                            pltpu.VMEM((2, tm, D), F32), pltpu.SemaphoreType.DMA((2,)),
                            pltpu.SemaphoreType.DMA((2,))],
        ),
        out_shape=jax.ShapeDtypeStruct((y_rows, D), F32),
        compiler_params=_cparams(("arbitrary", "arbitrary"), VMEM_LIMIT),
        name="moe_experts",
    )(tile_expert, n_used, fmap, fmap, fmap, h2d, wg, wu, wd)


def _combine_kernel(y1_ref, y2_ref, x_ref, w_ref, mod_ref, gain_ref, o_ref):
    w = w_ref[...]
    moe = w[:, 0:1] * y1_ref[...] + w[:, 1:2] * y2_ref[...]
    mod = mod_ref[0, 0]
    x4 = x_ref[...] + mod[5:6] * moe
    ms = jnp.mean(x4 * x4, axis=-1, keepdims=True)
    o_ref[...] = x4 * lax.rsqrt(ms + EPS) * gain_ref[...]


def _combine(y, x3_2d, wts, mod, final_gain, seq):
    n = x3_2d.shape[0]
    tc = TC_COMB
    nt = n // tc
    per_b = seq // tc
    return pl.pallas_call(
        _combine_kernel,
        grid=(nt,),
        in_specs=[
            pl.BlockSpec((tc, D), lambda i: (i, 0)),
            pl.BlockSpec((tc, D), lambda i: (i + nt, 0)),
            pl.BlockSpec((tc, D), lambda i: (i, 0)),
            pl.BlockSpec((tc, 2), lambda i: (i, 0)),
            pl.BlockSpec((1, 1, 6, D), lambda i: (1, i // per_b, 0, 0)),
            pl.BlockSpec((1, D), lambda i: (0, 0)),
        ],
        out_specs=pl.BlockSpec((tc, D), lambda i: (i, 0)),
        out_shape=jax.ShapeDtypeStruct((n, D), F32),
        compiler_params=_cparams(("arbitrary",)),
        name="moe_combine",
    )(y, y, x3_2d, wts, mod, final_gain.reshape(1, D))


def _rope_tables(seq):
    rows = seq // GRID_W
    row_pos = jnp.repeat(jnp.arange(rows, dtype=F32), GRID_W)
    col_pos = jnp.tile(jnp.arange(GRID_W, dtype=F32), rows)
    axis_dim = HEAD_DIM // 2
    inv_freq = ROPE_BASE ** (-jnp.arange(0, axis_dim, 2, dtype=F32) / axis_dim)
    ar = row_pos[:, None] * inv_freq
    ac = col_pos[:, None] * inv_freq
    cos64 = jnp.concatenate([jnp.cos(ar), jnp.cos(ar), jnp.cos(ac), jnp.cos(ac)], axis=1)
    sin64 = jnp.concatenate([-jnp.sin(ar), jnp.sin(ar), -jnp.sin(ac), jnp.sin(ac)], axis=1)
    return jnp.tile(cos64, (1, 2)), jnp.tile(sin64, (1, 2))


def _band_matrices():
    r = np.arange(BLOCK)[:, None]
    c = np.arange(BLOCK + 2 * POOL_HALO)[None, :] - POOL_HALO
    mats = []
    for w in POOL_SIZES:
        lo = -(w // 2)
        hi = w - 1 - w // 2
        mats.append(((c >= r + lo) & (c <= r + hi)).astype(np.float32))
    return jnp.asarray(np.stack(mats), dtype=BF16)


def kernel(x, c, ctx, c_ctx, w_ada, b_ada, w_in, attn_sink, gm_gain, gm_w_s, gm_b_s, w_out,
           ffn_w_gate, ffn_w_up, ffn_w_down, pool_w, pool_scale, router_w,
           moe_w_gate, moe_w_up, moe_w_down, final_gain):
    b, s, _ = x.shape
    n = b * s
    assert w_ada.shape[0] == 2 and w_in.shape[0] == 1 and pool_w.shape[0] == 1
    assert s % TM_IN == 0 and s % TM_POOL == 0 and s % TM_FFN == 0 and b <= 4
    assert n & (n - 1) == 0 and TM_MOE % MOE_NJ == 0

    cvec = jnp.concatenate([c, c_ctx[None, :], jnp.zeros((8 - b - 1, D), F32)], axis=0)
    mod = _ada_mod(cvec, w_ada, b_ada)

    cos_t, sin_t = _rope_tables(s)
    w_in_bf = w_in[0].astype(BF16)
    q, k, ksw, v, vsw, u, vg = _in_proj(x, mod, w_in_bf, gm_gain[0], cos_t, sin_t)
    kx, ksx, vx, vsx = _ctx_kv(ctx, mod, w_in_bf)
    wcat = gm_w_s[0].reshape(4, 2, BLOCK, BLOCK).transpose(0, 2, 1, 3).reshape(4, BLOCK, 2 * BLOCK).astype(BF16)
    x1 = _attn_mixer(x, mod, attn_sink[0], q, k, ksw, v, vsw, kx, ksx, vx, vsx, u, vg,
                     wcat, gm_b_s[0].T, w_out[0].astype(BF16))
    x2 = _dense_ffn(x1.reshape(n, D), mod, ffn_w_gate[0].astype(BF16), ffn_w_up[0].astype(BF16),
                    ffn_w_down[0].astype(BF16), s)

    wr = jnp.pad(router_w[0], ((0, 0), (0, LANES - N_EXPERTS)))
    wr_hi = wr.astype(BF16)
    wr_lo = (wr - wr_hi.astype(F32)).astype(BF16)
    tri = jnp.asarray(np.tril(np.ones((TM_POOL, TM_POOL), np.float32), -1), dtype=BF16)
    x3, h2, route, counts = _pool_route(x2.reshape(b, s, D), mod, _band_matrices(), pool_w[0].astype(BF16),
                                        pool_scale[0], wr_hi, wr_lo, tri)

    tm = TM_MOE
    n_tiles = (2 * n) // tm + N_EXPERTS
    cnt = counts[0, :N_EXPERTS].astype(jnp.int32)
    tiles_e = (cnt + tm - 1) // tm
    tile_end = jnp.cumsum(tiles_e)
    off = (tile_end - tiles_e) * tm
    n_used = tile_end[-1]
    tix = jnp.arange(n_tiles, dtype=jnp.int32)
    te = jnp.minimum(jnp.searchsorted(tile_end, tix, side="right"), N_EXPERTS - 1).astype(jnp.int32)
    te_last = te[jnp.maximum(n_used - 1, 0)]
    tile_expert = jnp.where(tix < n_used, te, te_last)
    e1 = route[0].astype(jnp.int32)
    e2 = route[1].astype(jnp.int32)
    pos1 = off[e1] + route[4].astype(jnp.int32)
    pos2 = off[e2] + route[5].astype(jnp.int32)
    n_slots = n_tiles * tm
    pad_lo = jnp.concatenate([off + cnt, (n_used * tm).reshape(1)]).astype(jnp.int32)
    pad_hi = jnp.concatenate([off + tiles_e * tm, jnp.full((1,), n_slots, jnp.int32)]).astype(jnp.int32)
    fmap = _slot_map(jnp.concatenate([pos1, pos2]), pad_lo, pad_hi, n_slots)
    n_used_arr = n_used.reshape(1).astype(jnp.int32)

    y = _moe_experts(h2.reshape(n, D), fmap.reshape(n_tiles, 1, tm), tile_expert, n_used_arr,
                     moe_w_gate[0].astype(BF16), moe_w_up[0].astype(BF16), moe_w_down[0].astype(BF16),
                     n_tiles, 2 * n + 2 * tm)
    out = _combine(y, x3.reshape(n, D), route[2:4].T, mod, final_gain, s)
    return out.reshape(b, s, D)
```

```python
import functools

import numpy as np
import jax
import jax.numpy as jnp
from jax import lax
from jax.experimental import pallas as pl
from jax.experimental.pallas import tpu as pltpu

F32 = jnp.float32
BF16 = jnp.bfloat16

D = 1024
GRID_W = 64
EPS = 1e-6
NEG_INF = -1e30
HEAD_DIM = 64
N_Q_HEADS = 8
BLOCK = 128
ATT_W = 512
KV_W = 128
GM_W = 512
IN_W = 1792
POOL_SIZES = (2, 4, 8, 16)
POOL_GD = 256
POOL_HALO = 16
N_EXPERTS = 8
ROPE_BASE = 10000.0
LANES = 128
SLAB = D // LANES
SQRT_2_OVER_PI = 0.7978845608028654

TM_IN = 512
TM_FFN = 512
TM_POOL = 512
TM_MOE = 512
MOE_NJ = 2
TC_COMB = 512
Q_BLOCKS = 4
VMEM_LIMIT = 56 * 1024 * 1024


def _cparams(sem, vmem=None):
    return pltpu.CompilerParams(dimension_semantics=sem, vmem_limit_bytes=vmem)


def _modulate(xf, shift, scale):
    ms = jnp.mean(xf * xf, axis=-1, keepdims=True)
    return xf * lax.rsqrt(ms + EPS) * (1.0 + scale) + shift


def _sigmoid(z):
    return 1.0 / (1.0 + jnp.exp(-z))


def _rows_to_slabs(val, slab_ref):
    rows = val.shape[0]
    for cix in range(SLAB):
        slab_ref[pl.ds(cix, rows, stride=SLAB), :] = val[:, cix * LANES:(cix + 1) * LANES]


def _slabs_to_rows(slab_ref, rows):
    return jnp.concatenate([slab_ref[pl.ds(cix, rows, stride=SLAB), :] for cix in range(SLAB)], axis=1)


def _ada_kernel(c_ref, w_ref, b_ref, o_ref):
    c = c_ref[...]
    s = c * _sigmoid(c)
    o_ref[0] = jnp.dot(s, w_ref[0], precision=lax.Precision.HIGHEST,
                       preferred_element_type=F32) + b_ref[0]


def _ada_mod(cvec, w_ada, b_ada):
    depth, _, n6 = w_ada.shape
    tn = 1536
    out = pl.pallas_call(
        _ada_kernel,
        grid=(depth, n6 // tn),
        in_specs=[
            pl.BlockSpec((8, D), lambda l, j: (0, 0)),
            pl.BlockSpec((1, D, tn), lambda l, j: (l, 0, j)),
            pl.BlockSpec((1, 1, tn), lambda l, j: (l, 0, j)),
        ],
        out_specs=pl.BlockSpec((1, 8, tn), lambda l, j: (l, 0, j)),
        out_shape=jax.ShapeDtypeStruct((depth, 8, n6), F32),
        compiler_params=_cparams(("arbitrary", "arbitrary")),
        name="ada_mod",
    )(cvec, w_ada, b_ada.reshape(depth, 1, n6))
    return out.reshape(depth, 8, 6, D)


def _rope(t, cs, sn, first_half):
    fwd = pltpu.roll(t, LANES - 16, axis=1)
    bwd = pltpu.roll(t, 16, axis=1)
    return t * cs + jnp.where(first_half, fwd, bwd) * sn


def _inproj_kernel(x_ref, mod_ref, w_ref, gain_ref, cos_ref, sin_ref,
                   q_ref, k_ref, ksw_ref, v_ref, vsw_ref, u_ref, vg_ref):
    mod = mod_ref[0, 0]
    h = _modulate(x_ref[0], mod[0:1], mod[1:2]).astype(BF16)
    proj = jnp.dot(h, w_ref[...], preferred_element_type=F32)
    cs = cos_ref[...]
    sn = sin_ref[...]
    lane = lax.broadcasted_iota(jnp.int32, cs.shape, 1)
    first_half = (lane & 16) == 0
    for cix in range(ATT_W // LANES):
        t = proj[:, cix * LANES:(cix + 1) * LANES]
        q_ref[0, :, cix * LANES:(cix + 1) * LANES] = (
            _rope(t, cs, sn, first_half) * (HEAD_DIM ** -0.5)).astype(BF16)
    kr = _rope(proj[:, ATT_W:ATT_W + KV_W], cs, sn, first_half)
    k_ref[0] = kr.astype(BF16)
    ksw_ref[0] = pltpu.roll(kr, HEAD_DIM, axis=1).astype(BF16)
    vv = proj[:, ATT_W + KV_W:ATT_W + 2 * KV_W]
    v_ref[0] = vv.astype(BF16)
    vsw_ref[0] = pltpu.roll(vv, HEAD_DIM, axis=1).astype(BF16)
    z = proj[:, ATT_W + 2 * KV_W:]
    g = z * (0.5 * (1.0 + jnp.tanh(SQRT_2_OVER_PI * (z + 0.044715 * (z * z * z)))))
    u_ref[0] = g[:, :GM_W].astype(BF16)
    vg = g[:, GM_W:]
    ms = jnp.mean(vg * vg, axis=-1, keepdims=True)
    vg_ref[0] = (vg * lax.rsqrt(ms + EPS) * gain_ref[...]).astype(BF16)


def _in_proj(x, mod, w_in_bf, gm_gain, cos_t, sin_t):
    b, s, _ = x.shape
    tm = TM_IN
    row = lambda w: pl.BlockSpec((1, tm, w), lambda bi, i: (bi, i, 0))
    outs = pl.pallas_call(
        _inproj_kernel,
        grid=(b, s // tm),
        in_specs=[
            row(D),
            pl.BlockSpec((1, 1, 6, D), lambda bi, i: (0, bi, 0, 0)),
            pl.BlockSpec((D, IN_W), lambda bi, i: (0, 0)),
            pl.BlockSpec((1, GM_W), lambda bi, i: (0, 0)),
            pl.BlockSpec((tm, LANES), lambda bi, i: (i, 0)),
            pl.BlockSpec((tm, LANES), lambda bi, i: (i, 0)),
        ],
        out_specs=[row(ATT_W), row(KV_W), row(KV_W), row(KV_W), row(KV_W), row(GM_W), row(GM_W)],
        out_shape=[jax.ShapeDtypeStruct((b, s, w), BF16)
                   for w in (ATT_W, KV_W, KV_W, KV_W, KV_W, GM_W, GM_W)],
        compiler_params=_cparams(("arbitrary", "arbitrary"), VMEM_LIMIT),
        name="in_proj",
    )(x, mod, w_in_bf, gm_gain.reshape(1, GM_W), cos_t, sin_t)
    return outs


def _ctx_kernel(c_ref, mod_ref, w_ref, k_ref, ksw_ref, v_ref, vsw_ref):
    mod = mod_ref[0, 0]
    h = _modulate(c_ref[0], mod[0:1], mod[1:2]).astype(BF16)
    kv = jnp.dot(h, w_ref[...], preferred_element_type=F32)
    kk = kv[:, :KV_W]
    vv = kv[:, KV_W:]
    k_ref[0] = kk.astype(BF16)
    ksw_ref[0] = pltpu.roll(kk, HEAD_DIM, axis=1).astype(BF16)
    v_ref[0] = vv.astype(BF16)
    vsw_ref[0] = pltpu.roll(vv, HEAD_DIM, axis=1).astype(BF16)


def _ctx_kv(ctx, mod, w_in_bf):
    b, l, _ = ctx.shape
    spec = pl.BlockSpec((1, l, KV_W), lambda bi: (bi, 0, 0))
    return pl.pallas_call(
        _ctx_kernel,
        grid=(b,),
        in_specs=[
            pl.BlockSpec((1, l, D), lambda bi: (bi, 0, 0)),
            pl.BlockSpec((1, 1, 6, D), lambda bi: (0, b, 0, 0)),
            pl.BlockSpec((D, 2 * KV_W), lambda bi: (0, ATT_W // (2 * KV_W))),
        ],
        out_specs=[spec] * 4,
        out_shape=[jax.ShapeDtypeStruct((b, l, KV_W), BF16)] * 4,
        compiler_params=_cparams(("arbitrary",)),
        name="ctx_kv",
    )(ctx, mod, w_in_bf)


def _attn_kernel(sink_ref, q_ref, kp_ref, kc_ref, kn_ref, ksp_ref, ksc_ref, ksn_ref,
                 vp_ref, vc_ref, vn_ref, vsp_ref, vsc_ref, vsn_ref,
                 kx_ref, ksx_ref, vx_ref, vsx_ref,
                 u_ref, vg_ref, wcat_ref, bs_ref, wout_ref, x_ref, mod_ref, o_ref):
    n = pl.program_id(1)
    nblk = pl.num_programs(1) * Q_BLOCKS
    lane = lax.broadcasted_iota(jnp.int32, (1, LANES), 1)
    low = lane < HEAD_DIM
    zero = jnp.zeros((), BF16)

    def variants(a0, a1):
        return ((jnp.where(low, a0, zero), jnp.where(low, zero, a1)),
                (jnp.where(low, a1, zero), jnp.where(low, zero, a0)))

    cat = lambda refs: jnp.concatenate([r[0] for r in refs], axis=0)
    kb_var = variants(cat((kp_ref, kc_ref, kn_ref)), cat((ksp_ref, ksc_ref, ksn_ref)))
    vb_var = variants(cat((vp_ref, vc_ref, vn_ref)), cat((vsp_ref, vsc_ref, vsn_ref)))
    kx_var = variants(kx_ref[0], ksx_ref[0])
    vx_var = variants(vx_ref[0], vsx_ref[0])

    row = lax.broadcasted_iota(jnp.int32, (2 * BLOCK, BLOCK), 0) & (BLOCK - 1)
    col = lax.broadcasted_iota(jnp.int32, (2 * BLOCK, BLOCK), 1)
    top = lax.broadcasted_iota(jnp.int32, (2 * BLOCK, 1), 0) < BLOCK
    nt_dims = (((1,), (1,)), ((), ()))

    q = q_ref[0]
    att_blocks = [[None] * 4 for _ in range(Q_BLOCKS)]
    for kvh in range(2):
        qst = jnp.concatenate(
            [q[qb * BLOCK:(qb + 1) * BLOCK, pr * LANES:(pr + 1) * LANES]
             for qb in range(Q_BLOCKS) for pr in (2 * kvh, 2 * kvh + 1)], axis=0)
        accs = [None] * Q_BLOCKS
        for half in range(2):
            sk = jnp.where(top, sink_ref[4 * kvh + half], sink_ref[4 * kvh + 2 + half])
            s_ctx = lax.dot_general(qst, kx_var[kvh][half], nt_dims, preferred_element_type=F32)
            for qb in range(Q_BLOCKS):
                g = n * Q_BLOCKS + qb
                qrows = qst[qb * 2 * BLOCK:(qb + 1) * 2 * BLOCK]
                sb = lax.dot_general(qrows, kb_var[kvh][half][qb * BLOCK:(qb + 3) * BLOCK], nt_dims,
                                     preferred_element_type=F32)
                s0 = jnp.where((col >= row) & (g > 0), sb[:, :BLOCK], NEG_INF)
                s1 = sb[:, BLOCK:2 * BLOCK]
                s2 = jnp.where((col <= row) & (g < nblk - 1), sb[:, 2 * BLOCK:], NEG_INF)
                sc = s_ctx[qb * 2 * BLOCK:(qb + 1) * 2 * BLOCK]
                m = jnp.maximum(jnp.maximum(s0, s1), s2)
                m = jnp.maximum(jnp.max(m, axis=-1, keepdims=True), jnp.max(sc, axis=-1, keepdims=True))
                m = jnp.maximum(m, sk)
                p0, p1, p2, pc = (jnp.exp(t - m) for t in (s0, s1, s2, sc))
                den = (jnp.sum(p0 + p1 + p2, axis=-1, keepdims=True) + jnp.sum(pc, axis=-1, keepdims=True)
                       + jnp.exp(sk - m))
                pb = jnp.concatenate([p0, p1, p2], axis=1).astype(BF16)
                o = (jnp.dot(pb, vb_var[kvh][half][qb * BLOCK:(qb + 3) * BLOCK], preferred_element_type=F32)
                     + jnp.dot(pc.astype(BF16), vx_var[kvh][half], preferred_element_type=F32))
                o = o / den
                accs[qb] = o if accs[qb] is None else accs[qb] + o
        for qb in range(Q_BLOCKS):
            att_blocks[qb][2 * kvh] = accs[qb][:BLOCK]
            att_blocks[qb][2 * kvh + 1] = accs[qb][BLOCK:]

    u = u_ref[0]
    vg = vg_ref[0]
    bs = bs_ref[...]
    gm_blocks = [[None] * 4 for _ in range(Q_BLOCKS)]
    for j in range(GM_W // LANES):
        chunks = [vg[c * BLOCK:(c + 1) * BLOCK, j * LANES:(j + 1) * LANES] for c in range(Q_BLOCKS)]
        rhs = jnp.concatenate(
            [jnp.concatenate([jnp.where(low, v, zero) for v in chunks], axis=1),
             jnp.concatenate([jnp.where(low, zero, v) for v in chunks], axis=1)], axis=0)
        mixed = jnp.dot(wcat_ref[j], rhs, preferred_element_type=F32)
        bias = jnp.where(low, bs[:, 2 * j:2 * j + 1], bs[:, 2 * j + 1:2 * j + 2])
        for c in range(Q_BLOCKS):
            gm_blocks[c][j] = (u[c * BLOCK:(c + 1) * BLOCK, j * LANES:(j + 1) * LANES].astype(F32)
                               * (mixed[:, c * LANES:(c + 1) * LANES] + bias))

    mix = jnp.concatenate([jnp.concatenate(att_blocks[c] + gm_blocks[c], axis=1) for c in range(Q_BLOCKS)],
                          axis=0).astype(BF16)
    y = jnp.dot(mix, wout_ref[...], preferred_element_type=F32)
    mod = mod_ref[0, 0]
    o_ref[0] = x_ref[0] + mod[2:3] * y


def _attn_mixer(x, mod, sink, q, k, ksw, v, vsw, kx, ksx, vx, vsx, u, vg, wcat_bf, bs_t, wout_bf):
    b, s, _ = x.shape
    tq = Q_BLOCKS * BLOCK
    nb = s // BLOCK
    l = kx.shape[1]
    cur = lambda w: pl.BlockSpec((1, tq, w), lambda bi, n: (bi, n, 0))
    prv = lambda w: pl.BlockSpec((1, BLOCK, w), lambda bi, n: (bi, jnp.maximum(n * Q_BLOCKS - 1, 0), 0))
    nxt = lambda w: pl.BlockSpec((1, BLOCK, w), lambda bi, n: (bi, jnp.minimum((n + 1) * Q_BLOCKS, nb - 1), 0))
    cx = pl.BlockSpec((1, l, KV_W), lambda bi, n: (bi, 0, 0))
    return pl.pallas_call(
        _attn_kernel,
        grid=(b, s // tq),
        in_specs=[
            pl.BlockSpec(memory_space=pltpu.SMEM),
            cur(ATT_W),
            prv(KV_W), cur(KV_W), nxt(KV_W), prv(KV_W), cur(KV_W), nxt(KV_W),
            prv(KV_W), cur(KV_W), nxt(KV_W), prv(KV_W), cur(KV_W), nxt(KV_W),
            cx, cx, cx, cx,
            cur(GM_W), cur(GM_W),
            pl.BlockSpec((4, BLOCK, 2 * BLOCK), lambda bi, n: (0, 0, 0)),
            pl.BlockSpec((BLOCK, 8), lambda bi, n: (0, 0)),
            pl.BlockSpec((D, D), lambda bi, n: (0, 0)),
            cur(D),
            pl.BlockSpec((1, 1, 6, D), lambda bi, n: (0, bi, 0, 0)),
        ],
        out_specs=cur(D),
        out_shape=jax.ShapeDtypeStruct((b, s, D), F32),
        compiler_params=_cparams(("arbitrary", "arbitrary"), VMEM_LIMIT),
        name="attn_gmlp_out",
    )(sink, q, k, k, k, ksw, ksw, ksw, v, v, v, vsw, vsw, vsw, kx, ksx, vx, vsx,
      u, vg, wcat_bf, bs_t, wout_bf, x, mod)


def _ffn_kernel(x_ref, mod_ref, wg_ref, wu_ref, wd_ref, o_ref):
    mod = mod_ref[0, 0]
    xf = x_ref[...]
    h = _modulate(xf, mod[3:4], mod[4:5]).astype(BF16)
    g = jnp.dot(h, wg_ref[...], preferred_element_type=F32)
    up = jnp.dot(h, wu_ref[...], preferred_element_type=F32)
    a = (g * _sigmoid(g) * up).astype(BF16)
    o_ref[...] = xf + mod[5:6] * jnp.dot(a, wd_ref[...], preferred_element_type=F32)


def _dense_ffn(x2d, mod, wg, wu, wd, seq):
    n = x2d.shape[0]
    f = wg.shape[1]
    tm = TM_FFN
    per_b = seq // tm
    resident = lambda shp: pl.BlockSpec(shp, lambda i: (0, 0), pipeline_mode=pl.Buffered(1))
    return pl.pallas_call(
        _ffn_kernel,
        grid=(n // tm,),
        in_specs=[
            pl.BlockSpec((tm, D), lambda i: (i, 0)),
            pl.BlockSpec((1, 1, 6, D), lambda i: (0, i // per_b, 0, 0)),
            resident((D, f)), resident((D, f)), resident((f, D)),
        ],
        out_specs=pl.BlockSpec((tm, D), lambda i: (i, 0)),
        out_shape=jax.ShapeDtypeStruct((n, D), F32),
        compiler_params=_cparams(("arbitrary",), VMEM_LIMIT),
        name="dense_ffn",
    )(x2d, mod, wg, wu, wd)


def _pool_route_kernel(x_ref, xp_ref, xn_ref, mod_ref, band_ref, pw_ref, psc_ref, wr_hi_ref, wr_lo_ref,
                       tri_ref, x3_ref, h2_ref, route_ref, cnt_ref, hext, carry):
    bi = pl.program_id(0)
    i = pl.program_id(1)
    ni = pl.num_programs(1)
    tm = x_ref.shape[1]
    seq = tm * ni
    mod = mod_ref[0, 0]

    @pl.when((bi == 0) & (i == 0))
    def _():
        carry[...] = jnp.zeros_like(carry)

    xf = x_ref[0]
    hp = _modulate(xp_ref[0], mod[0:1], mod[1:2])
    hn = _modulate(xn_ref[0], mod[0:1], mod[1:2])
    hext[0:POOL_HALO] = jnp.where(i > 0, hp, 0.0).astype(BF16)
    h_main = _modulate(xf, mod[0:1], mod[1:2])
    hext[POOL_HALO:POOL_HALO + tm] = h_main.astype(BF16)
    hext[POOL_HALO + tm:] = jnp.where(i < ni - 1, hn, 0.0).astype(BF16)

    t_local = lax.broadcasted_iota(jnp.int32, (BLOCK, 1), 0)
    ys = []
    for gi, w in enumerate(POOL_SIZES):
        lo_off = -(w // 2)
        hi_off = w - 1 - w // 2
        cols = slice(gi * POOL_GD, (gi + 1) * POOL_GD)
        outs = []
        for sb in range(tm // BLOCK):
            r0 = sb * BLOCK
            win = jnp.dot(band_ref[gi], hext[r0:r0 + BLOCK + 2 * POOL_HALO, cols],
                          preferred_element_type=F32)
            t = i * tm + r0 + t_local
            cnt = (jnp.minimum(t + hi_off, seq - 1) - jnp.maximum(t + lo_off, 0) + 1).astype(F32)
            diff = win / cnt - h_main[r0:r0 + BLOCK, cols]
            outs.append(diff.astype(BF16))
        dg = jnp.concatenate(outs, axis=0)
        ys.append(jnp.dot(dg, pw_ref[gi], preferred_element_type=F32))
    y = jnp.concatenate(ys, axis=1) * psc_ref[...]
    x3 = xf + mod[2:3] * y
    x3_ref[0] = x3

    h2 = _modulate(x3, mod[3:4], mod[4:5])
    _rows_to_slabs(h2, h2_ref)
    h_hi = h2.astype(BF16)
    h_lo = (h2 - h_hi.astype(F32)).astype(BF16)
    logits = (jnp.dot(h_hi, wr_hi_ref[...], preferred_element_type=F32)
              + jnp.dot(h_hi, wr_lo_ref[...], preferred_element_type=F32)
              + jnp.dot(h_lo, wr_hi_ref[...], preferred_element_type=F32))
    lane = lax.broadcasted_iota(jnp.int32, (tm, LANES), 1)
    lane_f = lane.astype(F32)
    neg = -jnp.inf
    lg = jnp.where(lane < N_EXPERTS, logits, neg)
    m1 = jnp.max(lg, axis=-1, keepdims=True)
    i1 = jnp.min(jnp.where(lg == m1, lane_f, float(LANES)), axis=-1, keepdims=True)
    oh1 = lane_f == i1
    lg2 = jnp.where(oh1, neg, lg)
    m2 = jnp.max(lg2, axis=-1, keepdims=True)
    i2 = jnp.min(jnp.where(lg2 == m2, lane_f, float(LANES)), axis=-1, keepdims=True)
    oh2 = lane_f == i2
    e = jnp.exp(m2 - m1)
    w1 = 1.0 / (1.0 + e)
    w2 = e / (1.0 + e)
    oh = jnp.where(oh1 | oh2, 1.0, 0.0)
    before = jnp.dot(tri_ref[...], oh.astype(BF16), preferred_element_type=F32) + carry[...]
    r1 = jnp.sum(jnp.where(oh1, before, 0.0), axis=-1, keepdims=True)
    r2 = jnp.sum(jnp.where(oh2, before, 0.0), axis=-1, keepdims=True)
    carry[...] = carry[...] + jnp.sum(oh, axis=0, keepdims=True)
    cnt_ref[...] = carry[...]
    info = jnp.where(lane == 0, i1, jnp.where(lane == 1, i2, jnp.where(lane == 2, w1, jnp.where(
        lane == 3, w2, jnp.where(lane == 4, r1, jnp.where(lane == 5, r2, 0.0))))))
    route_ref[...] = info.T[0:8, :]


def _pool_route(x, mod, band, pw_bf, pool_scale, wr_hi, wr_lo, tri):
    b, s, _ = x.shape
    tm = TM_POOL
    ni = s // tm
    hb = tm // POOL_HALO
    row = pl.BlockSpec((1, tm, D), lambda bi, i: (bi, i, 0))
    const2 = lambda shp: pl.BlockSpec(shp, lambda bi, i: (0,) * len(shp))
    return pl.pallas_call(
        _pool_route_kernel,
        grid=(b, ni),
        in_specs=[
            row,
            pl.BlockSpec((1, POOL_HALO, D), lambda bi, i: (bi, jnp.maximum(i * hb - 1, 0), 0)),
            pl.BlockSpec((1, POOL_HALO, D), lambda bi, i: (bi, jnp.minimum((i + 1) * hb, s // POOL_HALO - 1), 0)),
            pl.BlockSpec((1, 1, 6, D), lambda bi, i: (1, bi, 0, 0)),
            const2(band.shape), const2(pw_bf.shape), const2((1, D)),
            const2(wr_hi.shape), const2(wr_lo.shape), const2(tri.shape),
        ],
        out_specs=[row,
                   pl.BlockSpec((tm * SLAB, LANES), lambda bi, i: (bi * ni + i, 0)),
                   pl.BlockSpec((8, tm), lambda bi, i: (0, bi * ni + i)),
                   pl.BlockSpec((1, LANES), lambda bi, i: (0, 0))],
        out_shape=[jax.ShapeDtypeStruct((b, s, D), F32), jax.ShapeDtypeStruct((b * s * SLAB, LANES), F32),
                   jax.ShapeDtypeStruct((8, b * s), F32), jax.ShapeDtypeStruct((1, LANES), F32)],
        scratch_shapes=[pltpu.VMEM((tm + 2 * POOL_HALO, D), BF16), pltpu.VMEM((1, LANES), F32)],
        compiler_params=_cparams(("arbitrary", "arbitrary"), VMEM_LIMIT),
        name="pool_route",
    )(x, x, x, mod, band, pw_bf, pool_scale.reshape(1, D), wr_hi, wr_lo, tri)


def _slotmap_kernel(pos_ref, lo_ref, hi_ref, o_ref):
    n_pairs = pos_ref.shape[0]
    spare_mask = 2 * TM_MOE - 1
    for e in range(lo_ref.shape[0]):
        def fill(p, c):
            o_ref[p] = n_pairs + (p & spare_mask)
            return c
        lax.fori_loop(lo_ref[e], hi_ref[e], fill, 0)

    def place(f, c):
        o_ref[pos_ref[f]] = f
        return c
    lax.fori_loop(0, n_pairs, place, 0, unroll=8)


def _slot_map(pos_flat, lo, hi, n_slots):
    smem = pl.BlockSpec(memory_space=pltpu.SMEM)
    return pl.pallas_call(
        _slotmap_kernel,
        in_specs=[smem, smem, smem],
        out_specs=smem,
        out_shape=jax.ShapeDtypeStruct((n_slots,), jnp.int32),
        name="moe_slot_map",
    )(pos_flat, lo, hi)


def _moe_kernel(te_ref, nused_ref, fnext_ref, fprev_ref, f0_ref, h_hbm, wg_ref, wu_ref, wd_ref,
                y_hbm, xbuf, xb, acc, stage, gsem, ssem):
    i = pl.program_id(0)
    j = pl.program_id(1)
    nt = pl.num_programs(0)
    nj = pl.num_programs(1)
    used_tiles = nused_ref[0]
    tm = xb.shape[0]
    rows_per_step = tm // MOE_NJ
    tok_mask = h_hbm.shape[0] // SLAB - 1
    tile_rows = tm * SLAB

    def slab(ix):
        return pl.ds(pl.multiple_of(ix * SLAB, SLAB), SLAB)

    def gather_row(fref, r, slot):
        tok = fref[0, 0, r] & tok_mask
        return pltpu.make_async_copy(h_hbm.at[slab(tok)], xbuf.at[slot, slab(r)], gsem.at[slot])

    def scatter_row(fref, r, slot):
        return pltpu.make_async_copy(stage.at[slot, slab(r)], y_hbm.at[slab(fref[0, 0, r])], ssem.at[slot])

    def gather_all(slot):
        return pltpu.make_async_copy(h_hbm.at[pl.ds(0, tile_rows)], xbuf.at[slot], gsem.at[slot])

    def scatter_all(slot):
        return pltpu.make_async_copy(stage.at[slot], y_hbm.at[pl.ds(0, tile_rows)], ssem.at[slot])

    cur = i % 2
    used = i < used_tiles

    @pl.when((i == 0) & (j == 0))
    def _():
        stage[...] = jnp.zeros_like(stage)
        spare = y_hbm.shape[0] - 2 * tile_rows
        fills = [pltpu.make_async_copy(stage.at[sl], y_hbm.at[pl.ds(spare + sl * tile_rows, tile_rows)],
                                       ssem.at[sl])
                 for sl in range(2)]
        for cp in fills:
            cp.start()
        for cp in fills:
            cp.wait()

    @pl.when((i == 0) & (j == 0) & used)
    def _():
        def prime(r, c):
            gather_row(f0_ref, r, 0).start()
            return c
        lax.fori_loop(0, tm, prime, 0)

    @pl.when(used & (j == 0))
    def _():
        gather_all(cur).wait()
        xb[...] = _slabs_to_rows(xbuf.at[cur], tm).astype(BF16)

    @pl.when((j == nj - 1) & (i >= 2) & (i - 2 < used_tiles))
    def _():
        scatter_all(cur).wait()

    def expert_ffn():
        xv = xb[...]
        g = jnp.dot(xv, wg_ref[0], preferred_element_type=F32)
        up = jnp.dot(xv, wu_ref[0], preferred_element_type=F32)
        a = (g * _sigmoid(g) * up).astype(BF16)
        return jnp.dot(a, wd_ref[0], preferred_element_type=F32)

    has_next = i + 1 < used_tiles
    has_prev = (i >= 1) & (i - 1 < used_tiles)
    steady = (i >= 2) & has_next

    for jv in range(MOE_NJ):
        @pl.when(steady & (j == jv))
        def _(jv=jv):
            for r in range(rows_per_step):
                gather_row(fnext_ref, jv * rows_per_step + r, 1 - cur).start()
                scatter_row(fprev_ref, jv * rows_per_step + r, 1 - cur).start()
            part = expert_ffn()
            if MOE_NJ == 1:
                _rows_to_slabs(part, stage.at[cur])
            elif jv == 0:
                acc[...] = part
            elif jv == MOE_NJ - 1:
                _rows_to_slabs(acc[...] + part, stage.at[cur])
            else:
                acc[...] += part

    @pl.when(jnp.logical_not(steady))
    def _():
        @pl.when(used)
        def _():
            part = expert_ffn()

            @pl.when((j == 0) & (nj > 1))
            def _():
                acc[...] = part

            @pl.when((j > 0) & (j < nj - 1))
            def _():
                acc[...] += part

            @pl.when((j == nj - 1) & (nj > 1))
            def _():
                _rows_to_slabs(acc[...] + part, stage.at[cur])

            @pl.when(nj == 1)
            def _():
                _rows_to_slabs(part, stage.at[cur])

        base = j * rows_per_step

        @pl.when(has_next)
        def _():
            def issue(r, c):
                gather_row(fnext_ref, base + r, 1 - cur).start()
                return c
            lax.fori_loop(0, rows_per_step, issue, 0)

        @pl.when(has_prev)
        def _():
            def issue(r, c):
                scatter_row(fprev_ref, base + r, 1 - cur).start()
                return c
            lax.fori_loop(0, rows_per_step, issue, 0)

    @pl.when((i == nt - 1) & (j == nj - 1) & (nt - 2 < used_tiles))
    def _():
        scatter_all(1 - cur).wait()


def _moe_experts(h2d, fmap, tile_expert, n_used, wg, wu, wd, n_tiles, y_rows):
    f = wg.shape[2]
    tm = TM_MOE
    nj = MOE_NJ
    tf = f // nj

    def jj(i, j, nu):
        return jnp.where(i < nu[0], j, nj - 1)

    fblk = lambda imap: pl.BlockSpec((1, 1, tm), imap, memory_space=pltpu.SMEM)
    return pl.pallas_call(
        _moe_kernel,
        grid_spec=pltpu.PrefetchScalarGridSpec(
            num_scalar_prefetch=2,
            grid=(n_tiles, nj),
            in_specs=[
                fblk(lambda i, j, te, nu: (jnp.minimum(i + 1, n_tiles - 1), 0, 0)),
                fblk(lambda i, j, te, nu: (jnp.maximum(i - 1, 0), 0, 0)),
                fblk(lambda i, j, te, nu: (0, 0, 0)),
                pl.BlockSpec(memory_space=pl.ANY),
                pl.BlockSpec((1, D, tf), lambda i, j, te, nu: (te[i], 0, jj(i, j, nu))),
                pl.BlockSpec((1, D, tf), lambda i, j, te, nu: (te[i], 0, jj(i, j, nu))),
                pl.BlockSpec((1, tf, D), lambda i, j, te, nu: (te[i], jj(i, j, nu), 0)),
            ],
            out_specs=pl.BlockSpec(memory_space=pl.ANY),
            scratch_shapes=[pltpu.VMEM((2, tm * SLAB, LANES), F32), pltpu.VMEM((tm, D), BF16),
                            pltpu.VMEM((tm, D), F32), pltpu.VMEM((2, tm * SLAB, LANES), F32),
                            pltpu.SemaphoreType.DMA((2,)), pltpu.SemaphoreType.DMA((2,))],
        ),
        out_shape=jax.ShapeDtypeStruct((y_rows * SLAB, LANES), F32),
        compiler_params=_cparams(("arbitrary", "arbitrary"), VMEM_LIMIT),
        name="moe_experts",
    )(tile_expert, n_used, fmap, fmap, fmap, h2d, wg, wu, wd)


def _combine_kernel(y1_ref, y2_ref, x_ref, w_ref, mod_ref, gain_ref, o_ref):
    w = w_ref[...]
    rows = x_ref.shape[0]
    moe = w[:, 0:1] * _slabs_to_rows(y1_ref, rows) + w[:, 1:2] * _slabs_to_rows(y2_ref, rows)
    mod = mod_ref[0, 0]
    x4 = x_ref[...] + mod[5:6] * moe
    ms = jnp.mean(x4 * x4, axis=-1, keepdims=True)
    o_ref[...] = x4 * lax.rsqrt(ms + EPS) * gain_ref[...]


def _combine(y, x3_2d, wts, mod, final_gain, seq):
    n = x3_2d.shape[0]
    tc = TC_COMB
    nt = n // tc
    per_b = seq // tc
    return pl.pallas_call(
        _combine_kernel,
        grid=(nt,),
        in_specs=[
            pl.BlockSpec((tc * SLAB, LANES), lambda i: (i, 0)),
            pl.BlockSpec((tc * SLAB, LANES), lambda i: (i + nt, 0)),
            pl.BlockSpec((tc, D), lambda i: (i, 0)),
            pl.BlockSpec((tc, 2), lambda i: (i, 0)),
            pl.BlockSpec((1, 1, 6, D), lambda i: (1, i // per_b, 0, 0)),
            pl.BlockSpec((1, D), lambda i: (0, 0)),
        ],
        out_specs=pl.BlockSpec((tc, D), lambda i: (i, 0)),
        out_shape=jax.ShapeDtypeStruct((n, D), F32),
        compiler_params=_cparams(("arbitrary",)),
        name="moe_combine",
    )(y, y, x3_2d, wts, mod, final_gain.reshape(1, D))


def _rope_tables(seq):
    rows = seq // GRID_W
    row_pos = jnp.repeat(jnp.arange(rows, dtype=F32), GRID_W)
    col_pos = jnp.tile(jnp.arange(GRID_W, dtype=F32), rows)
    axis_dim = HEAD_DIM // 2
    inv_freq = ROPE_BASE ** (-jnp.arange(0, axis_dim, 2, dtype=F32) / axis_dim)
    ar = row_pos[:, None] * inv_freq
    ac = col_pos[:, None] * inv_freq
    cos64 = jnp.concatenate([jnp.cos(ar), jnp.cos(ar), jnp.cos(ac), jnp.cos(ac)], axis=1)
    sin64 = jnp.concatenate([-jnp.sin(ar), jnp.sin(ar), -jnp.sin(ac), jnp.sin(ac)], axis=1)
    return jnp.tile(cos64, (1, 2)), jnp.tile(sin64, (1, 2))


def _band_matrices():
    r = np.arange(BLOCK)[:, None]
    c = np.arange(BLOCK + 2 * POOL_HALO)[None, :] - POOL_HALO
    mats = []
    for w in POOL_SIZES:
        lo = -(w // 2)
        hi = w - 1 - w // 2
        mats.append(((c >= r + lo) & (c <= r + hi)).astype(np.float32))
    return jnp.asarray(np.stack(mats), dtype=BF16)


def kernel(x, c, ctx, c_ctx, w_ada, b_ada, w_in, attn_sink, gm_gain, gm_w_s, gm_b_s, w_out,
           ffn_w_gate, ffn_w_up, ffn_w_down, pool_w, pool_scale, router_w,
           moe_w_gate, moe_w_up, moe_w_down, final_gain):
    b, s, _ = x.shape
    n = b * s
    assert w_ada.shape[0] == 2 and w_in.shape[0] == 1 and pool_w.shape[0] == 1
    assert s % TM_IN == 0 and s % TM_POOL == 0 and s % TM_FFN == 0 and b <= 4
    assert n & (n - 1) == 0 and TM_MOE % MOE_NJ == 0

    cvec = jnp.concatenate([c, c_ctx[None, :], jnp.zeros((8 - b - 1, D), F32)], axis=0)
    mod = _ada_mod(cvec, w_ada, b_ada)

    cos_t, sin_t = _rope_tables(s)
    w_in_bf = w_in[0].astype(BF16)
    q, k, ksw, v, vsw, u, vg = _in_proj(x, mod, w_in_bf, gm_gain[0], cos_t, sin_t)
    kx, ksx, vx, vsx = _ctx_kv(ctx, mod, w_in_bf)
    wcat = gm_w_s[0].reshape(4, 2, BLOCK, BLOCK).transpose(0, 2, 1, 3).reshape(4, BLOCK, 2 * BLOCK).astype(BF16)
    x1 = _attn_mixer(x, mod, attn_sink[0], q, k, ksw, v, vsw, kx, ksx, vx, vsx, u, vg,
                     wcat, gm_b_s[0].T, w_out[0].astype(BF16))
    x2 = _dense_ffn(x1.reshape(n, D), mod, ffn_w_gate[0].astype(BF16), ffn_w_up[0].astype(BF16),
                    ffn_w_down[0].astype(BF16), s)

    wr = jnp.pad(router_w[0], ((0, 0), (0, LANES - N_EXPERTS)))
    wr_hi = wr.astype(BF16)
    wr_lo = (wr - wr_hi.astype(F32)).astype(BF16)
    tri = jnp.asarray(np.tril(np.ones((TM_POOL, TM_POOL), np.float32), -1), dtype=BF16)
    x3, h2, route, counts = _pool_route(x2.reshape(b, s, D), mod, _band_matrices(), pool_w[0].astype(BF16),
                                        pool_scale[0], wr_hi, wr_lo, tri)

    tm = TM_MOE
    n_tiles = (2 * n) // tm + N_EXPERTS
    cnt = counts[0, :N_EXPERTS].astype(jnp.int32)
    tiles_e = (cnt + tm - 1) // tm
    tile_end = jnp.cumsum(tiles_e)
    off = (tile_end - tiles_e) * tm
    n_used = tile_end[-1]
    tix = jnp.arange(n_tiles, dtype=jnp.int32)
    te = jnp.minimum(jnp.searchsorted(tile_end, tix, side="right"), N_EXPERTS - 1).astype(jnp.int32)
    te_last = te[jnp.maximum(n_used - 1, 0)]
    tile_expert = jnp.where(tix < n_used, te, te_last)
    e1 = route[0].astype(jnp.int32)
    e2 = route[1].astype(jnp.int32)
    pos1 = off[e1] + route[4].astype(jnp.int32)
    pos2 = off[e2] + route[5].astype(jnp.int32)
    n_slots = n_tiles * tm
    pad_lo = jnp.concatenate([off + cnt, (n_used * tm).reshape(1)]).astype(jnp.int32)
    pad_hi = jnp.concatenate([off + tiles_e * tm, jnp.full((1,), n_slots, jnp.int32)]).astype(jnp.int32)
    fmap = _slot_map(jnp.concatenate([pos1, pos2]), pad_lo, pad_hi, n_slots)
    n_used_arr = n_used.reshape(1).astype(jnp.int32)

    y = _moe_experts(h2, fmap.reshape(n_tiles, 1, tm), tile_expert, n_used_arr,
                     moe_w_gate[0].astype(BF16), moe_w_up[0].astype(BF16), moe_w_down[0].astype(BF16),
                     n_tiles, 2 * n + 2 * tm)
    out = _combine(y, x3.reshape(n, D), route[2:4].T, mod, final_gain, s)
    return out.reshape(b, s, D)
```

```python
import functools

import numpy as np
import jax
import jax.numpy as jnp
from jax import lax
from jax.experimental import pallas as pl
from jax.experimental.pallas import tpu as pltpu

F32 = jnp.float32
BF16 = jnp.bfloat16

D = 1024
GRID_W = 64
EPS = 1e-6
NEG_INF = -1e30
HEAD_DIM = 64
N_Q_HEADS = 8
BLOCK = 128
ATT_W = 512
KV_W = 128
GM_W = 512
IN_W = 1792
POOL_SIZES = (2, 4, 8, 16)
POOL_GD = 256
POOL_HALO = 16
N_EXPERTS = 8
ROPE_BASE = 10000.0
LANES = 128
SLAB = D // LANES
SQRT_2_OVER_PI = 0.7978845608028654

TM_IN = 512
TM_FFN = 512
TM_POOL = 512
TM_MOE = 512
MOE_SPLIT = 2
W_CHUNK = 256
W_RING = 4
MOE_VMEM_LIMIT = 60 * 1024 * 1024
TC_COMB = 512
Q_BLOCKS = 4
VMEM_LIMIT = 56 * 1024 * 1024


def _cparams(sem, vmem=None):
    return pltpu.CompilerParams(dimension_semantics=sem, vmem_limit_bytes=vmem)


def _modulate(xf, shift, scale):
    ms = jnp.mean(xf * xf, axis=-1, keepdims=True)
    return xf * lax.rsqrt(ms + EPS) * (1.0 + scale) + shift


def _sigmoid(z):
    return 1.0 / (1.0 + jnp.exp(-z))


def _rows_to_slabs(val, slab_ref):
    rows = val.shape[0]
    for cix in range(SLAB):
        slab_ref[pl.ds(cix, rows, stride=SLAB), :] = val[:, cix * LANES:(cix + 1) * LANES]


def _slabs_to_rows(slab_ref, rows):
    return jnp.concatenate([slab_ref[pl.ds(cix, rows, stride=SLAB), :] for cix in range(SLAB)], axis=1)


def _ada_kernel(c_ref, w_ref, b_ref, o_ref):
    c = c_ref[...]
    s = c * _sigmoid(c)
    o_ref[0] = jnp.dot(s, w_ref[0], precision=lax.Precision.HIGHEST,
                       preferred_element_type=F32) + b_ref[0]


def _ada_mod(cvec, w_ada, b_ada):
    depth, _, n6 = w_ada.shape
    tn = 1536
    out = pl.pallas_call(
        _ada_kernel,
        grid=(depth, n6 // tn),
        in_specs=[
            pl.BlockSpec((8, D), lambda l, j: (0, 0)),
            pl.BlockSpec((1, D, tn), lambda l, j: (l, 0, j)),
            pl.BlockSpec((1, 1, tn), lambda l, j: (l, 0, j)),
        ],
        out_specs=pl.BlockSpec((1, 8, tn), lambda l, j: (l, 0, j)),
        out_shape=jax.ShapeDtypeStruct((depth, 8, n6), F32),
        compiler_params=_cparams(("arbitrary", "arbitrary")),
        name="ada_mod",
    )(cvec, w_ada, b_ada.reshape(depth, 1, n6))
    return out.reshape(depth, 8, 6, D)


def _rope(t, cs, sn, first_half):
    fwd = pltpu.roll(t, LANES - 16, axis=1)
    bwd = pltpu.roll(t, 16, axis=1)
    return t * cs + jnp.where(first_half, fwd, bwd) * sn


def _inproj_kernel(x_ref, mod_ref, w_ref, gain_ref, cos_ref, sin_ref,
                   q_ref, k_ref, ksw_ref, v_ref, vsw_ref, u_ref, vg_ref):
    mod = mod_ref[0, 0]
    h = _modulate(x_ref[0], mod[0:1], mod[1:2]).astype(BF16)
    proj = jnp.dot(h, w_ref[...], preferred_element_type=F32)
    cs = cos_ref[...]
    sn = sin_ref[...]
    lane = lax.broadcasted_iota(jnp.int32, cs.shape, 1)
    first_half = (lane & 16) == 0
    for cix in range(ATT_W // LANES):
        t = proj[:, cix * LANES:(cix + 1) * LANES]
        q_ref[0, :, cix * LANES:(cix + 1) * LANES] = (
            _rope(t, cs, sn, first_half) * (HEAD_DIM ** -0.5)).astype(BF16)
    kr = _rope(proj[:, ATT_W:ATT_W + KV_W], cs, sn, first_half)
    k_ref[0] = kr.astype(BF16)
    ksw_ref[0] = pltpu.roll(kr, HEAD_DIM, axis=1).astype(BF16)
    vv = proj[:, ATT_W + KV_W:ATT_W + 2 * KV_W]
    v_ref[0] = vv.astype(BF16)
    vsw_ref[0] = pltpu.roll(vv, HEAD_DIM, axis=1).astype(BF16)
    z = proj[:, ATT_W + 2 * KV_W:]
    g = z * (0.5 * (1.0 + jnp.tanh(SQRT_2_OVER_PI * (z + 0.044715 * (z * z * z)))))
    u_ref[0] = g[:, :GM_W].astype(BF16)
    vg = g[:, GM_W:]
    ms = jnp.mean(vg * vg, axis=-1, keepdims=True)
    vg_ref[0] = (vg * lax.rsqrt(ms + EPS) * gain_ref[...]).astype(BF16)


def _in_proj(x, mod, w_in_bf, gm_gain, cos_t, sin_t):
    b, s, _ = x.shape
    tm = TM_IN
    row = lambda w: pl.BlockSpec((1, tm, w), lambda bi, i: (bi, i, 0))
    outs = pl.pallas_call(
        _inproj_kernel,
        grid=(b, s // tm),
        in_specs=[
            row(D),
            pl.BlockSpec((1, 1, 6, D), lambda bi, i: (0, bi, 0, 0)),
            pl.BlockSpec((D, IN_W), lambda bi, i: (0, 0)),
            pl.BlockSpec((1, GM_W), lambda bi, i: (0, 0)),
            pl.BlockSpec((tm, LANES), lambda bi, i: (i, 0)),
            pl.BlockSpec((tm, LANES), lambda bi, i: (i, 0)),
        ],
        out_specs=[row(ATT_W), row(KV_W), row(KV_W), row(KV_W), row(KV_W), row(GM_W), row(GM_W)],
        out_shape=[jax.ShapeDtypeStruct((b, s, w), BF16)
                   for w in (ATT_W, KV_W, KV_W, KV_W, KV_W, GM_W, GM_W)],
        compiler_params=_cparams(("arbitrary", "arbitrary"), VMEM_LIMIT),
        name="in_proj",
    )(x, mod, w_in_bf, gm_gain.reshape(1, GM_W), cos_t, sin_t)
    return outs


def _ctx_kernel(c_ref, mod_ref, w_ref, k_ref, ksw_ref, v_ref, vsw_ref):
    mod = mod_ref[0, 0]
    h = _modulate(c_ref[0], mod[0:1], mod[1:2]).astype(BF16)
    kv = jnp.dot(h, w_ref[...], preferred_element_type=F32)
    kk = kv[:, :KV_W]
    vv = kv[:, KV_W:]
    k_ref[0] = kk.astype(BF16)
    ksw_ref[0] = pltpu.roll(kk, HEAD_DIM, axis=1).astype(BF16)
    v_ref[0] = vv.astype(BF16)
    vsw_ref[0] = pltpu.roll(vv, HEAD_DIM, axis=1).astype(BF16)


def _ctx_kv(ctx, mod, w_in_bf):
    b, l, _ = ctx.shape
    spec = pl.BlockSpec((1, l, KV_W), lambda bi: (bi, 0, 0))
    return pl.pallas_call(
        _ctx_kernel,
        grid=(b,),
        in_specs=[
            pl.BlockSpec((1, l, D), lambda bi: (bi, 0, 0)),
            pl.BlockSpec((1, 1, 6, D), lambda bi: (0, b, 0, 0)),
            pl.BlockSpec((D, 2 * KV_W), lambda bi: (0, ATT_W // (2 * KV_W))),
        ],
        out_specs=[spec] * 4,
        out_shape=[jax.ShapeDtypeStruct((b, l, KV_W), BF16)] * 4,
        compiler_params=_cparams(("arbitrary",)),
        name="ctx_kv",
    )(ctx, mod, w_in_bf)


def _attn_kernel(sink_ref, q_ref, kp_ref, kc_ref, kn_ref, ksp_ref, ksc_ref, ksn_ref,
                 vp_ref, vc_ref, vn_ref, vsp_ref, vsc_ref, vsn_ref,
                 kx_ref, ksx_ref, vx_ref, vsx_ref,
                 u_ref, vg_ref, wcat_ref, bs_ref, wout_ref, x_ref, mod_ref, o_ref):
    n = pl.program_id(1)
    nblk = pl.num_programs(1) * Q_BLOCKS
    lane = lax.broadcasted_iota(jnp.int32, (1, LANES), 1)
    low = lane < HEAD_DIM
    zero = jnp.zeros((), BF16)

    def variants(a0, a1):
        return ((jnp.where(low, a0, zero), jnp.where(low, zero, a1)),
                (jnp.where(low, a1, zero), jnp.where(low, zero, a0)))

    cat = lambda refs: jnp.concatenate([r[0] for r in refs], axis=0)
    kb_var = variants(cat((kp_ref, kc_ref, kn_ref)), cat((ksp_ref, ksc_ref, ksn_ref)))
    vb_var = variants(cat((vp_ref, vc_ref, vn_ref)), cat((vsp_ref, vsc_ref, vsn_ref)))
    kx_var = variants(kx_ref[0], ksx_ref[0])
    vx_var = variants(vx_ref[0], vsx_ref[0])

    row = lax.broadcasted_iota(jnp.int32, (2 * BLOCK, BLOCK), 0) & (BLOCK - 1)
    col = lax.broadcasted_iota(jnp.int32, (2 * BLOCK, BLOCK), 1)
    top = lax.broadcasted_iota(jnp.int32, (2 * BLOCK, 1), 0) < BLOCK
    nt_dims = (((1,), (1,)), ((), ()))

    q = q_ref[0]
    att_blocks = [[None] * 4 for _ in range(Q_BLOCKS)]
    for kvh in range(2):
        qst = jnp.concatenate(
            [q[qb * BLOCK:(qb + 1) * BLOCK, pr * LANES:(pr + 1) * LANES]
             for qb in range(Q_BLOCKS) for pr in (2 * kvh, 2 * kvh + 1)], axis=0)
        accs = [None] * Q_BLOCKS
        for half in range(2):
            sk = jnp.where(top, sink_ref[4 * kvh + half], sink_ref[4 * kvh + 2 + half])
            s_ctx = lax.dot_general(qst, kx_var[kvh][half], nt_dims, preferred_element_type=F32)
            for qb in range(Q_BLOCKS):
                g = n * Q_BLOCKS + qb
                qrows = qst[qb * 2 * BLOCK:(qb + 1) * 2 * BLOCK]
                sb = lax.dot_general(qrows, kb_var[kvh][half][qb * BLOCK:(qb + 3) * BLOCK], nt_dims,
                                     preferred_element_type=F32)
                s0 = jnp.where((col >= row) & (g > 0), sb[:, :BLOCK], NEG_INF)
                s1 = sb[:, BLOCK:2 * BLOCK]
                s2 = jnp.where((col <= row) & (g < nblk - 1), sb[:, 2 * BLOCK:], NEG_INF)
                sc = s_ctx[qb * 2 * BLOCK:(qb + 1) * 2 * BLOCK]
                m = jnp.maximum(jnp.maximum(s0, s1), s2)
                m = jnp.maximum(jnp.max(m, axis=-1, keepdims=True), jnp.max(sc, axis=-1, keepdims=True))
                m = jnp.maximum(m, sk)
                p0, p1, p2, pc = (jnp.exp(t - m) for t in (s0, s1, s2, sc))
                den = (jnp.sum(p0 + p1 + p2, axis=-1, keepdims=True) + jnp.sum(pc, axis=-1, keepdims=True)
                       + jnp.exp(sk - m))
                pb = jnp.concatenate([p0, p1, p2], axis=1).astype(BF16)
                o = (jnp.dot(pb, vb_var[kvh][half][qb * BLOCK:(qb + 3) * BLOCK], preferred_element_type=F32)
                     + jnp.dot(pc.astype(BF16), vx_var[kvh][half], preferred_element_type=F32))
                o = o / den
                accs[qb] = o if accs[qb] is None else accs[qb] + o
        for qb in range(Q_BLOCKS):
            att_blocks[qb][2 * kvh] = accs[qb][:BLOCK]
            att_blocks[qb][2 * kvh + 1] = accs[qb][BLOCK:]

    u = u_ref[0]
    vg = vg_ref[0]
    bs = bs_ref[...]
    gm_blocks = [[None] * 4 for _ in range(Q_BLOCKS)]
    for j in range(GM_W // LANES):
        chunks = [vg[c * BLOCK:(c + 1) * BLOCK, j * LANES:(j + 1) * LANES] for c in range(Q_BLOCKS)]
        rhs = jnp.concatenate(
            [jnp.concatenate([jnp.where(low, v, zero) for v in chunks], axis=1),
             jnp.concatenate([jnp.where(low, zero, v) for v in chunks], axis=1)], axis=0)
        mixed = jnp.dot(wcat_ref[j], rhs, preferred_element_type=F32)
        bias = jnp.where(low, bs[:, 2 * j:2 * j + 1], bs[:, 2 * j + 1:2 * j + 2])
        for c in range(Q_BLOCKS):
            gm_blocks[c][j] = (u[c * BLOCK:(c + 1) * BLOCK, j * LANES:(j + 1) * LANES].astype(F32)
                               * (mixed[:, c * LANES:(c + 1) * LANES] + bias))

    mix = jnp.concatenate([jnp.concatenate(att_blocks[c] + gm_blocks[c], axis=1) for c in range(Q_BLOCKS)],
                          axis=0).astype(BF16)
    y = jnp.dot(mix, wout_ref[...], preferred_element_type=F32)
    mod = mod_ref[0, 0]
    o_ref[0] = x_ref[0] + mod[2:3] * y


def _attn_mixer(x, mod, sink, q, k, ksw, v, vsw, kx, ksx, vx, vsx, u, vg, wcat_bf, bs_t, wout_bf):
    b, s, _ = x.shape
    tq = Q_BLOCKS * BLOCK
    nb = s // BLOCK
    l = kx.shape[1]
    cur = lambda w: pl.BlockSpec((1, tq, w), lambda bi, n: (bi, n, 0))
    prv = lambda w: pl.BlockSpec((1, BLOCK, w), lambda bi, n: (bi, jnp.maximum(n * Q_BLOCKS - 1, 0), 0))
    nxt = lambda w: pl.BlockSpec((1, BLOCK, w), lambda bi, n: (bi, jnp.minimum((n + 1) * Q_BLOCKS, nb - 1), 0))
    cx = pl.BlockSpec((1, l, KV_W), lambda bi, n: (bi, 0, 0))
    return pl.pallas_call(
        _attn_kernel,
        grid=(b, s // tq),
        in_specs=[
            pl.BlockSpec(memory_space=pltpu.SMEM),
            cur(ATT_W),
            prv(KV_W), cur(KV_W), nxt(KV_W), prv(KV_W), cur(KV_W), nxt(KV_W),
            prv(KV_W), cur(KV_W), nxt(KV_W), prv(KV_W), cur(KV_W), nxt(KV_W),
            cx, cx, cx, cx,
            cur(GM_W), cur(GM_W),
            pl.BlockSpec((4, BLOCK, 2 * BLOCK), lambda bi, n: (0, 0, 0)),
            pl.BlockSpec((BLOCK, 8), lambda bi, n: (0, 0)),
            pl.BlockSpec((D, D), lambda bi, n: (0, 0)),
            cur(D),
            pl.BlockSpec((1, 1, 6, D), lambda bi, n: (0, bi, 0, 0)),
        ],
        out_specs=cur(D),
        out_shape=jax.ShapeDtypeStruct((b, s, D), F32),
        compiler_params=_cparams(("arbitrary", "arbitrary"), VMEM_LIMIT),
        name="attn_gmlp_out",
    )(sink, q, k, k, k, ksw, ksw, ksw, v, v, v, vsw, vsw, vsw, kx, ksx, vx, vsx,
      u, vg, wcat_bf, bs_t, wout_bf, x, mod)


def _ffn_kernel(x_ref, mod_ref, wg_ref, wu_ref, wd_ref, o_ref):
    mod = mod_ref[0, 0]
    xf = x_ref[...]
    h = _modulate(xf, mod[3:4], mod[4:5]).astype(BF16)
    g = jnp.dot(h, wg_ref[...], preferred_element_type=F32)
    up = jnp.dot(h, wu_ref[...], preferred_element_type=F32)
    a = (g * _sigmoid(g) * up).astype(BF16)
    o_ref[...] = xf + mod[5:6] * jnp.dot(a, wd_ref[...], preferred_element_type=F32)


def _dense_ffn(x2d, mod, wg, wu, wd, seq):
    n = x2d.shape[0]
    f = wg.shape[1]
    tm = TM_FFN
    per_b = seq // tm
    resident = lambda shp: pl.BlockSpec(shp, lambda i: (0, 0), pipeline_mode=pl.Buffered(1))
    return pl.pallas_call(
        _ffn_kernel,
        grid=(n // tm,),
        in_specs=[
            pl.BlockSpec((tm, D), lambda i: (i, 0)),
            pl.BlockSpec((1, 1, 6, D), lambda i: (0, i // per_b, 0, 0)),
            resident((D, f)), resident((D, f)), resident((f, D)),
        ],
        out_specs=pl.BlockSpec((tm, D), lambda i: (i, 0)),
        out_shape=jax.ShapeDtypeStruct((n, D), F32),
        compiler_params=_cparams(("arbitrary",), VMEM_LIMIT),
        name="dense_ffn",
    )(x2d, mod, wg, wu, wd)


def _pool_route_kernel(x_ref, xp_ref, xn_ref, mod_ref, band_ref, pw_ref, psc_ref, wr_hi_ref, wr_lo_ref,
                       tri_ref, x3_ref, h2_ref, route_ref, cnt_ref, hext, carry):
    bi = pl.program_id(0)
    i = pl.program_id(1)
    ni = pl.num_programs(1)
    tm = x_ref.shape[1]
    seq = tm * ni
    mod = mod_ref[0, 0]

    @pl.when((bi == 0) & (i == 0))
    def _():
        carry[...] = jnp.zeros_like(carry)

    xf = x_ref[0]
    hp = _modulate(xp_ref[0], mod[0:1], mod[1:2])
    hn = _modulate(xn_ref[0], mod[0:1], mod[1:2])
    hext[0:POOL_HALO] = jnp.where(i > 0, hp, 0.0).astype(BF16)
    h_main = _modulate(xf, mod[0:1], mod[1:2])
    hext[POOL_HALO:POOL_HALO + tm] = h_main.astype(BF16)
    hext[POOL_HALO + tm:] = jnp.where(i < ni - 1, hn, 0.0).astype(BF16)

    t_local = lax.broadcasted_iota(jnp.int32, (BLOCK, 1), 0)
    ys = []
    for gi, w in enumerate(POOL_SIZES):
        lo_off = -(w // 2)
        hi_off = w - 1 - w // 2
        cols = slice(gi * POOL_GD, (gi + 1) * POOL_GD)
        outs = []
        for sb in range(tm // BLOCK):
            r0 = sb * BLOCK
            win = jnp.dot(band_ref[gi], hext[r0:r0 + BLOCK + 2 * POOL_HALO, cols],
                          preferred_element_type=F32)
            t = i * tm + r0 + t_local
            cnt = (jnp.minimum(t + hi_off, seq - 1) - jnp.maximum(t + lo_off, 0) + 1).astype(F32)
            diff = win / cnt - h_main[r0:r0 + BLOCK, cols]
            outs.append(diff.astype(BF16))
        dg = jnp.concatenate(outs, axis=0)
        ys.append(jnp.dot(dg, pw_ref[gi], preferred_element_type=F32))
    y = jnp.concatenate(ys, axis=1) * psc_ref[...]
    x3 = xf + mod[2:3] * y
    x3_ref[0] = x3

    h2 = _modulate(x3, mod[3:4], mod[4:5])
    _rows_to_slabs(h2, h2_ref)
    h_hi = h2.astype(BF16)
    h_lo = (h2 - h_hi.astype(F32)).astype(BF16)
    logits = (jnp.dot(h_hi, wr_hi_ref[...], preferred_element_type=F32)
              + jnp.dot(h_hi, wr_lo_ref[...], preferred_element_type=F32)
              + jnp.dot(h_lo, wr_hi_ref[...], preferred_element_type=F32))
    lane = lax.broadcasted_iota(jnp.int32, (tm, LANES), 1)
    lane_f = lane.astype(F32)
    neg = -jnp.inf
    lg = jnp.where(lane < N_EXPERTS, logits, neg)
    m1 = jnp.max(lg, axis=-1, keepdims=True)
    i1 = jnp.min(jnp.where(lg == m1, lane_f, float(LANES)), axis=-1, keepdims=True)
    oh1 = lane_f == i1
    lg2 = jnp.where(oh1, neg, lg)
    m2 = jnp.max(lg2, axis=-1, keepdims=True)
    i2 = jnp.min(jnp.where(lg2 == m2, lane_f, float(LANES)), axis=-1, keepdims=True)
    oh2 = lane_f == i2
    e = jnp.exp(m2 - m1)
    w1 = 1.0 / (1.0 + e)
    w2 = e / (1.0 + e)
    oh = jnp.where(oh1 | oh2, 1.0, 0.0)
    before = jnp.dot(tri_ref[...], oh.astype(BF16), preferred_element_type=F32) + carry[...]
    r1 = jnp.sum(jnp.where(oh1, before, 0.0), axis=-1, keepdims=True)
    r2 = jnp.sum(jnp.where(oh2, before, 0.0), axis=-1, keepdims=True)
    carry[...] = carry[...] + jnp.sum(oh, axis=0, keepdims=True)
    cnt_ref[...] = carry[...]
    info = jnp.where(lane == 0, i1, jnp.where(lane == 1, i2, jnp.where(lane == 2, w1, jnp.where(
        lane == 3, w2, jnp.where(lane == 4, r1, jnp.where(lane == 5, r2, 0.0))))))
    route_ref[...] = info.T[0:8, :]


def _pool_route(x, mod, band, pw_bf, pool_scale, wr_hi, wr_lo, tri):
    b, s, _ = x.shape
    tm = TM_POOL
    ni = s // tm
    hb = tm // POOL_HALO
    row = pl.BlockSpec((1, tm, D), lambda bi, i: (bi, i, 0))
    const2 = lambda shp: pl.BlockSpec(shp, lambda bi, i: (0,) * len(shp))
    return pl.pallas_call(
        _pool_route_kernel,
        grid=(b, ni),
        in_specs=[
            row,
            pl.BlockSpec((1, POOL_HALO, D), lambda bi, i: (bi, jnp.maximum(i * hb - 1, 0), 0)),
            pl.BlockSpec((1, POOL_HALO, D), lambda bi, i: (bi, jnp.minimum((i + 1) * hb, s // POOL_HALO - 1), 0)),
            pl.BlockSpec((1, 1, 6, D), lambda bi, i: (1, bi, 0, 0)),
            const2(band.shape), const2(pw_bf.shape), const2((1, D)),
            const2(wr_hi.shape), const2(wr_lo.shape), const2(tri.shape),
        ],
        out_specs=[row,
                   pl.BlockSpec((tm * SLAB, LANES), lambda bi, i: (bi * ni + i, 0)),
                   pl.BlockSpec((8, tm), lambda bi, i: (0, bi * ni + i)),
                   pl.BlockSpec((1, LANES), lambda bi, i: (0, 0))],
        out_shape=[jax.ShapeDtypeStruct((b, s, D), F32), jax.ShapeDtypeStruct((b * s * SLAB, LANES), F32),
                   jax.ShapeDtypeStruct((8, b * s), F32), jax.ShapeDtypeStruct((1, LANES), F32)],
        scratch_shapes=[pltpu.VMEM((tm + 2 * POOL_HALO, D), BF16), pltpu.VMEM((1, LANES), F32)],
        compiler_params=_cparams(("arbitrary", "arbitrary"), VMEM_LIMIT),
        name="pool_route",
    )(x, x, x, mod, band, pw_bf, pool_scale.reshape(1, D), wr_hi, wr_lo, tri)


def _slotmap_kernel(pos_ref, lo_ref, hi_ref, o_ref):
    n_pairs = pos_ref.shape[0]
    spare_mask = 2 * TM_MOE - 1
    for e in range(lo_ref.shape[0]):
        def fill(p, c):
            o_ref[p] = n_pairs + (p & spare_mask)
            return c
        lax.fori_loop(lo_ref[e], hi_ref[e], fill, 0)

    def place(f, c):
        o_ref[pos_ref[f]] = f
        return c
    lax.fori_loop(0, n_pairs, place, 0, unroll=8)


def _slot_map(pos_flat, lo, hi, n_slots):
    smem = pl.BlockSpec(memory_space=pltpu.SMEM)
    return pl.pallas_call(
        _slotmap_kernel,
        in_specs=[smem, smem, smem],
        out_specs=smem,
        out_shape=jax.ShapeDtypeStruct((n_slots,), jnp.int32),
        name="moe_slot_map",
    )(pos_flat, lo, hi)


def _moe_kernel(te_ref, nused_ref, fnext_ref, fprev_ref, f0_ref, h_hbm, wg_hbm, wu_hbm, wd_hbm,
                y_hbm, wg_res, wu_res, wd_res, stg_col, stg_row, xbuf, xb, stage, gsem, ssem, wsem):
    i = pl.program_id(0)
    nt = pl.num_programs(0)
    used_tiles = nused_ref[0]
    tm = xb.shape[0]
    f_dim = wg_res.shape[1]
    tok_mask = h_hbm.shape[0] // SLAB - 1
    tile_rows = tm * SLAB

    def slab(ix):
        return pl.ds(pl.multiple_of(ix * SLAB, SLAB), SLAB)

    def gather_row(fref, r, slot):
        tok = fref[0, 0, r] & tok_mask
        return pltpu.make_async_copy(h_hbm.at[slab(tok)], xbuf.at[slot, slab(r)], gsem.at[slot])

    def scatter_row(fref, r, slot):
        return pltpu.make_async_copy(stage.at[slot, slab(r)], y_hbm.at[slab(fref[0, 0, r])], ssem.at[slot])

    def gather_all(slot):
        return pltpu.make_async_copy(h_hbm.at[pl.ds(0, tile_rows)], xbuf.at[slot], gsem.at[slot])

    def scatter_all(slot):
        return pltpu.make_async_copy(stage.at[slot], y_hbm.at[pl.ds(0, tile_rows)], ssem.at[slot])

    cur = i % 2
    used = i < used_tiles

    @pl.when(i == 0)
    def _():
        stage[...] = jnp.zeros_like(stage)
        spare = y_hbm.shape[0] - 2 * tile_rows
        fills = [pltpu.make_async_copy(stage.at[sl], y_hbm.at[pl.ds(spare + sl * tile_rows, tile_rows)],
                                       ssem.at[sl])
                 for sl in range(2)]
        for cp in fills:
            cp.start()
        for cp in fills:
            cp.wait()

    @pl.when((i == 0) & used)
    def _():
        def prime(r, c):
            gather_row(f0_ref, r, 0).start()
            return c
        lax.fori_loop(0, tm, prime, 0)

    expert = te_ref[i]
    new_expert = used & ((i == 0) | (expert != te_ref[jnp.maximum(i - 1, 0)]))

    @pl.when(new_expert)
    def _():
        n_c = f_dim // W_CHUNK
        plan = [(w, res, False, c) for w, res in ((wg_hbm, wg_res), (wu_hbm, wu_res)) for c in range(n_c)]
        plan += [(wd_hbm, wd_res, True, c) for c in range(n_c)]

        def chunk_copy(k):
            w_hbm, _, by_rows, c = plan[k]
            cs = pl.ds(c * W_CHUNK, W_CHUNK)
            buf = k % W_RING
            if by_rows:
                return pltpu.make_async_copy(w_hbm.at[expert, cs, :], stg_row.at[buf], wsem.at[W_RING + buf])
            return pltpu.make_async_copy(w_hbm.at[expert, :, cs], stg_col.at[buf], wsem.at[buf])

        for k in range(W_RING):
            chunk_copy(k).start()
        for k in range(len(plan)):
            _, res, by_rows, c = plan[k]
            chunk_copy(k).wait()
            cs = slice(c * W_CHUNK, (c + 1) * W_CHUNK)
            if by_rows:
                res[cs, :] = stg_row[k % W_RING].astype(BF16)
            else:
                res[:, cs] = stg_col[k % W_RING].astype(BF16)
            if k + W_RING < len(plan):
                chunk_copy(k + W_RING).start()

    @pl.when(used)
    def _():
        gather_all(cur).wait()
        xb[...] = _slabs_to_rows(xbuf.at[cur], tm).astype(BF16)

    @pl.when((i >= 2) & (i - 2 < used_tiles))
    def _():
        scatter_all(cur).wait()

    def expert_ffn():
        xv = xb[...]
        out = None
        for hf in range(MOE_SPLIT):
            cs = slice(hf * (f_dim // MOE_SPLIT), (hf + 1) * (f_dim // MOE_SPLIT))
            g = jnp.dot(xv, wg_res[:, cs], preferred_element_type=F32)
            up = jnp.dot(xv, wu_res[:, cs], preferred_element_type=F32)
            a = (g * _sigmoid(g) * up).astype(BF16)
            part = jnp.dot(a, wd_res[cs, :], preferred_element_type=F32)
            out = part if out is None else out + part
        return out

    has_next = i + 1 < used_tiles
    has_prev = (i >= 1) & (i - 1 < used_tiles)
    steady = (i >= 2) & has_next

    @pl.when(steady)
    def _():
        for r in range(tm):
            gather_row(fnext_ref, r, 1 - cur).start()
            scatter_row(fprev_ref, r, 1 - cur).start()
        _rows_to_slabs(expert_ffn(), stage.at[cur])

    @pl.when(jnp.logical_not(steady))
    def _():
        @pl.when(used)
        def _():
            _rows_to_slabs(expert_ffn(), stage.at[cur])

        @pl.when(has_next)
        def _():
            def issue(r, c):
                gather_row(fnext_ref, r, 1 - cur).start()
                return c
            lax.fori_loop(0, tm, issue, 0)

        @pl.when(has_prev)
        def _():
            def issue(r, c):
                scatter_row(fprev_ref, r, 1 - cur).start()
                return c
            lax.fori_loop(0, tm, issue, 0)

    @pl.when((i == nt - 1) & (nt - 2 < used_tiles))
    def _():
        scatter_all(1 - cur).wait()


def _moe_experts(h_slabs, fmap, tile_expert, n_used, wg, wu, wd, n_tiles, y_rows):
    f = wg.shape[2]
    tm = TM_MOE
    fblk = lambda imap: pl.BlockSpec((1, 1, tm), imap, memory_space=pltpu.SMEM)
    hbm = pl.BlockSpec(memory_space=pl.ANY)
    return pl.pallas_call(
        _moe_kernel,
        grid_spec=pltpu.PrefetchScalarGridSpec(
            num_scalar_prefetch=2,
            grid=(n_tiles,),
            in_specs=[
                fblk(lambda i, te, nu: (jnp.minimum(i + 1, n_tiles - 1), 0, 0)),
                fblk(lambda i, te, nu: (jnp.maximum(i - 1, 0), 0, 0)),
                fblk(lambda i, te, nu: (0, 0, 0)),
                hbm, hbm, hbm, hbm,
            ],
            out_specs=hbm,
            scratch_shapes=[pltpu.VMEM((D, f), BF16), pltpu.VMEM((D, f), BF16), pltpu.VMEM((f, D), BF16),
                            pltpu.VMEM((W_RING, D, W_CHUNK), F32), pltpu.VMEM((W_RING, W_CHUNK, D), F32),
                            pltpu.VMEM((2, tm * SLAB, LANES), F32), pltpu.VMEM((tm, D), BF16),
                            pltpu.VMEM((2, tm * SLAB, LANES), F32),
                            pltpu.SemaphoreType.DMA((2,)), pltpu.SemaphoreType.DMA((2,)),
                            pltpu.SemaphoreType.DMA((2 * W_RING,))],
        ),
        out_shape=jax.ShapeDtypeStruct((y_rows * SLAB, LANES), F32),
        compiler_params=_cparams(("arbitrary",), MOE_VMEM_LIMIT),
        name="moe_experts",
    )(tile_expert, n_used, fmap, fmap, fmap, h_slabs, wg, wu, wd)


def _combine_kernel(y1_ref, y2_ref, x_ref, w_ref, mod_ref, gain_ref, o_ref):
    w = w_ref[...]
    rows = x_ref.shape[0]
    moe = w[:, 0:1] * _slabs_to_rows(y1_ref, rows) + w[:, 1:2] * _slabs_to_rows(y2_ref, rows)
    mod = mod_ref[0, 0]
    x4 = x_ref[...] + mod[5:6] * moe
    ms = jnp.mean(x4 * x4, axis=-1, keepdims=True)
    o_ref[...] = x4 * lax.rsqrt(ms + EPS) * gain_ref[...]


def _combine(y, x3_2d, wts, mod, final_gain, seq):
    n = x3_2d.shape[0]
    tc = TC_COMB
    nt = n // tc
    per_b = seq // tc
    return pl.pallas_call(
        _combine_kernel,
        grid=(nt,),
        in_specs=[
            pl.BlockSpec((tc * SLAB, LANES), lambda i: (i, 0)),
            pl.BlockSpec((tc * SLAB, LANES), lambda i: (i + nt, 0)),
            pl.BlockSpec((tc, D), lambda i: (i, 0)),
            pl.BlockSpec((tc, 2), lambda i: (i, 0)),
            pl.BlockSpec((1, 1, 6, D), lambda i: (1, i // per_b, 0, 0)),
            pl.BlockSpec((1, D), lambda i: (0, 0)),
        ],
        out_specs=pl.BlockSpec((tc, D), lambda i: (i, 0)),
        out_shape=jax.ShapeDtypeStruct((n, D), F32),
        compiler_params=_cparams(("arbitrary",)),
        name="moe_combine",
    )(y, y, x3_2d, wts, mod, final_gain.reshape(1, D))


def _rope_tables(seq):
    rows = seq // GRID_W
    row_pos = jnp.repeat(jnp.arange(rows, dtype=F32), GRID_W)
    col_pos = jnp.tile(jnp.arange(GRID_W, dtype=F32), rows)
    axis_dim = HEAD_DIM // 2
    inv_freq = ROPE_BASE ** (-jnp.arange(0, axis_dim, 2, dtype=F32) / axis_dim)
    ar = row_pos[:, None] * inv_freq
    ac = col_pos[:, None] * inv_freq
    cos64 = jnp.concatenate([jnp.cos(ar), jnp.cos(ar), jnp.cos(ac), jnp.cos(ac)], axis=1)
    sin64 = jnp.concatenate([-jnp.sin(ar), jnp.sin(ar), -jnp.sin(ac), jnp.sin(ac)], axis=1)
    return jnp.tile(cos64, (1, 2)), jnp.tile(sin64, (1, 2))


def _band_matrices():
    r = np.arange(BLOCK)[:, None]
    c = np.arange(BLOCK + 2 * POOL_HALO)[None, :] - POOL_HALO
    mats = []
    for w in POOL_SIZES:
        lo = -(w // 2)
        hi = w - 1 - w // 2
        mats.append(((c >= r + lo) & (c <= r + hi)).astype(np.float32))
    return jnp.asarray(np.stack(mats), dtype=BF16)


def kernel(x, c, ctx, c_ctx, w_ada, b_ada, w_in, attn_sink, gm_gain, gm_w_s, gm_b_s, w_out,
           ffn_w_gate, ffn_w_up, ffn_w_down, pool_w, pool_scale, router_w,
           moe_w_gate, moe_w_up, moe_w_down, final_gain):
    b, s, _ = x.shape
    n = b * s
    assert w_ada.shape[0] == 2 and w_in.shape[0] == 1 and pool_w.shape[0] == 1
    assert s % TM_IN == 0 and s % TM_POOL == 0 and s % TM_FFN == 0 and b <= 4
    assert n & (n - 1) == 0

    cvec = jnp.concatenate([c, c_ctx[None, :], jnp.zeros((8 - b - 1, D), F32)], axis=0)
    mod = _ada_mod(cvec, w_ada, b_ada)

    cos_t, sin_t = _rope_tables(s)
    w_in_bf = w_in[0].astype(BF16)
    q, k, ksw, v, vsw, u, vg = _in_proj(x, mod, w_in_bf, gm_gain[0], cos_t, sin_t)
    kx, ksx, vx, vsx = _ctx_kv(ctx, mod, w_in_bf)
    wcat = gm_w_s[0].reshape(4, 2, BLOCK, BLOCK).transpose(0, 2, 1, 3).reshape(4, BLOCK, 2 * BLOCK).astype(BF16)
    x1 = _attn_mixer(x, mod, attn_sink[0], q, k, ksw, v, vsw, kx, ksx, vx, vsx, u, vg,
                     wcat, gm_b_s[0].T, w_out[0].astype(BF16))
    x2 = _dense_ffn(x1.reshape(n, D), mod, ffn_w_gate[0].astype(BF16), ffn_w_up[0].astype(BF16),
                    ffn_w_down[0].astype(BF16), s)

    wr = jnp.pad(router_w[0], ((0, 0), (0, LANES - N_EXPERTS)))
    wr_hi = wr.astype(BF16)
    wr_lo = (wr - wr_hi.astype(F32)).astype(BF16)
    tri = jnp.asarray(np.tril(np.ones((TM_POOL, TM_POOL), np.float32), -1), dtype=BF16)
    x3, h2, route, counts = _pool_route(x2.reshape(b, s, D), mod, _band_matrices(), pool_w[0].astype(BF16),
                                        pool_scale[0], wr_hi, wr_lo, tri)

    tm = TM_MOE
    n_tiles = (2 * n) // tm + N_EXPERTS
    cnt = counts[0, :N_EXPERTS].astype(jnp.int32)
    tiles_e = (cnt + tm - 1) // tm
    tile_end = jnp.cumsum(tiles_e)
    off = (tile_end - tiles_e) * tm
    n_used = tile_end[-1]
    tix = jnp.arange(n_tiles, dtype=jnp.int32)
    te = jnp.minimum(jnp.sum(tix[:, None] >= tile_end[None, :], axis=1), N_EXPERTS - 1).astype(jnp.int32)
    te_last = te[jnp.maximum(n_used - 1, 0)]
    tile_expert = jnp.where(tix < n_used, te, te_last)
    e1 = route[0].astype(jnp.int32)
    e2 = route[1].astype(jnp.int32)
    pos1 = off[e1] + route[4].astype(jnp.int32)
    pos2 = off[e2] + route[5].astype(jnp.int32)
    n_slots = n_tiles * tm
    pad_lo = jnp.concatenate([off + cnt, (n_used * tm).reshape(1)]).astype(jnp.int32)
    pad_hi = jnp.concatenate([off + tiles_e * tm, jnp.full((1,), n_slots, jnp.int32)]).astype(jnp.int32)
    fmap = _slot_map(jnp.concatenate([pos1, pos2]), pad_lo, pad_hi, n_slots)
    n_used_arr = n_used.reshape(1).astype(jnp.int32)

    y = _moe_experts(h2, fmap.reshape(n_tiles, 1, tm), tile_expert, n_used_arr,
                     moe_w_gate[0], moe_w_up[0], moe_w_down[0], n_tiles, 2 * n + 2 * tm)
    out = _combine(y, x3.reshape(n, D), route[2:4].T, mod, final_gain, s)
    return out.reshape(b, s, D)
```

```python
import functools

import numpy as np
import jax
import jax.numpy as jnp
from jax import lax
from jax.experimental import pallas as pl
from jax.experimental.pallas import tpu as pltpu

F32 = jnp.float32
BF16 = jnp.bfloat16

D = 1024
GRID_W = 64
EPS = 1e-6
NEG_INF = -1e30
HEAD_DIM = 64
N_Q_HEADS = 8
BLOCK = 128
ATT_W = 512
KV_W = 128
GM_W = 512
IN_W = 1792
POOL_SIZES = (2, 4, 8, 16)
POOL_GD = 256
POOL_HALO = 16
N_EXPERTS = 8
ROPE_BASE = 10000.0
LANES = 128
SLAB = D // LANES
SQRT_2_OVER_PI = 0.7978845608028654

TM_IN = 512
TM_FFN = 512
TM_POOL = 512
TM_MOE = 512
MOE_SPLIT = 2
W_CHUNK = 256
W_RING = 4
MOE_VMEM_LIMIT = 60 * 1024 * 1024
TC_COMB = 512
Q_BLOCKS = 4
VMEM_LIMIT = 56 * 1024 * 1024


def _cparams(sem, vmem=None):
    return pltpu.CompilerParams(dimension_semantics=sem, vmem_limit_bytes=vmem)


def _modulate(xf, shift, scale):
    ms = jnp.mean(xf * xf, axis=-1, keepdims=True)
    return xf * lax.rsqrt(ms + EPS) * (1.0 + scale) + shift


def _sigmoid(z):
    return 1.0 / (1.0 + jnp.exp(-z))


def _rows_to_slabs(val, slab_ref):
    rows = val.shape[0]
    for cix in range(SLAB):
        slab_ref[pl.ds(cix, rows, stride=SLAB), :] = val[:, cix * LANES:(cix + 1) * LANES]


def _slabs_to_rows(slab_ref, rows):
    return jnp.concatenate([slab_ref[pl.ds(cix, rows, stride=SLAB), :] for cix in range(SLAB)], axis=1)


def _ada_kernel(c_ref, w_ref, b_ref, o_ref):
    c = c_ref[...]
    s = c * _sigmoid(c)
    o_ref[0] = jnp.dot(s, w_ref[0], precision=lax.Precision.HIGHEST,
                       preferred_element_type=F32) + b_ref[0]


def _ada_mod(cvec, w_ada, b_ada):
    depth, _, n6 = w_ada.shape
    tn = 1536
    out = pl.pallas_call(
        _ada_kernel,
        grid=(depth, n6 // tn),
        in_specs=[
            pl.BlockSpec((8, D), lambda l, j: (0, 0)),
            pl.BlockSpec((1, D, tn), lambda l, j: (l, 0, j)),
            pl.BlockSpec((1, 1, tn), lambda l, j: (l, 0, j)),
        ],
        out_specs=pl.BlockSpec((1, 8, tn), lambda l, j: (l, 0, j)),
        out_shape=jax.ShapeDtypeStruct((depth, 8, n6), F32),
        compiler_params=_cparams(("arbitrary", "arbitrary")),
        name="ada_mod",
    )(cvec, w_ada, b_ada.reshape(depth, 1, n6))
    return out.reshape(depth, 8, 6, D)


def _rope(t, cs, sn, first_half):
    fwd = pltpu.roll(t, LANES - 16, axis=1)
    bwd = pltpu.roll(t, 16, axis=1)
    return t * cs + jnp.where(first_half, fwd, bwd) * sn


def _inproj_kernel(x_ref, mod_ref, w_ref, gain_ref, cos_ref, sin_ref,
                   q_ref, k_ref, ksw_ref, v_ref, vsw_ref, u_ref, vg_ref):
    mod = mod_ref[0, 0]
    h = _modulate(x_ref[0], mod[0:1], mod[1:2]).astype(BF16)
    proj = jnp.dot(h, w_ref[...], preferred_element_type=F32)
    cs = cos_ref[...]
    sn = sin_ref[...]
    lane = lax.broadcasted_iota(jnp.int32, cs.shape, 1)
    first_half = (lane & 16) == 0
    for cix in range(ATT_W // LANES):
        t = proj[:, cix * LANES:(cix + 1) * LANES]
        q_ref[0, :, cix * LANES:(cix + 1) * LANES] = (
            _rope(t, cs, sn, first_half) * (HEAD_DIM ** -0.5)).astype(BF16)
    kr = _rope(proj[:, ATT_W:ATT_W + KV_W], cs, sn, first_half)
    k_ref[0] = kr.astype(BF16)
    ksw_ref[0] = pltpu.roll(kr, HEAD_DIM, axis=1).astype(BF16)
    vv = proj[:, ATT_W + KV_W:ATT_W + 2 * KV_W]
    v_ref[0] = vv.astype(BF16)
    vsw_ref[0] = pltpu.roll(vv, HEAD_DIM, axis=1).astype(BF16)
    z = proj[:, ATT_W + 2 * KV_W:]
    g = z * (0.5 * (1.0 + jnp.tanh(SQRT_2_OVER_PI * (z + 0.044715 * (z * z * z)))))
    u_ref[0] = g[:, :GM_W].astype(BF16)
    vg = g[:, GM_W:]
    ms = jnp.mean(vg * vg, axis=-1, keepdims=True)
    vg_ref[0] = (vg * lax.rsqrt(ms + EPS) * gain_ref[...]).astype(BF16)


def _in_proj(x, mod, w_in_bf, gm_gain, cos_t, sin_t):
    b, s, _ = x.shape
    tm = TM_IN
    row = lambda w: pl.BlockSpec((1, tm, w), lambda bi, i: (bi, i, 0))
    outs = pl.pallas_call(
        _inproj_kernel,
        grid=(b, s // tm),
        in_specs=[
            row(D),
            pl.BlockSpec((1, 1, 6, D), lambda bi, i: (0, bi, 0, 0)),
            pl.BlockSpec((D, IN_W), lambda bi, i: (0, 0)),
            pl.BlockSpec((1, GM_W), lambda bi, i: (0, 0)),
            pl.BlockSpec((tm, LANES), lambda bi, i: (i, 0)),
            pl.BlockSpec((tm, LANES), lambda bi, i: (i, 0)),
        ],
        out_specs=[row(ATT_W), row(KV_W), row(KV_W), row(KV_W), row(KV_W), row(GM_W), row(GM_W)],
        out_shape=[jax.ShapeDtypeStruct((b, s, w), BF16)
                   for w in (ATT_W, KV_W, KV_W, KV_W, KV_W, GM_W, GM_W)],
        compiler_params=_cparams(("arbitrary", "arbitrary"), VMEM_LIMIT),
        name="in_proj",
    )(x, mod, w_in_bf, gm_gain.reshape(1, GM_W), cos_t, sin_t)
    return outs


def _ctx_kernel(c_ref, mod_ref, w_ref, k_ref, ksw_ref, v_ref, vsw_ref):
    mod = mod_ref[0, 0]
    h = _modulate(c_ref[0], mod[0:1], mod[1:2]).astype(BF16)
    kv = jnp.dot(h, w_ref[...], preferred_element_type=F32)
    kk = kv[:, :KV_W]
    vv = kv[:, KV_W:]
    k_ref[0] = kk.astype(BF16)
    ksw_ref[0] = pltpu.roll(kk, HEAD_DIM, axis=1).astype(BF16)
    v_ref[0] = vv.astype(BF16)
    vsw_ref[0] = pltpu.roll(vv, HEAD_DIM, axis=1).astype(BF16)


def _ctx_kv(ctx, mod, w_in_bf):
    b, l, _ = ctx.shape
    spec = pl.BlockSpec((1, l, KV_W), lambda bi: (bi, 0, 0))
    return pl.pallas_call(
        _ctx_kernel,
        grid=(b,),
        in_specs=[
            pl.BlockSpec((1, l, D), lambda bi: (bi, 0, 0)),
            pl.BlockSpec((1, 1, 6, D), lambda bi: (0, b, 0, 0)),
            pl.BlockSpec((D, 2 * KV_W), lambda bi: (0, ATT_W // (2 * KV_W))),
        ],
        out_specs=[spec] * 4,
        out_shape=[jax.ShapeDtypeStruct((b, l, KV_W), BF16)] * 4,
        compiler_params=_cparams(("arbitrary",)),
        name="ctx_kv",
    )(ctx, mod, w_in_bf)


def _attn_kernel(sink_ref, q_ref, kp_ref, kc_ref, kn_ref, ksp_ref, ksc_ref, ksn_ref,
                 vp_ref, vc_ref, vn_ref, vsp_ref, vsc_ref, vsn_ref,
                 kx_ref, ksx_ref, vx_ref, vsx_ref,
                 u_ref, vg_ref, wcat_ref, bs_ref, wout_ref, x_ref, mod_ref, o_ref):
    n = pl.program_id(1)
    nblk = pl.num_programs(1) * Q_BLOCKS
    lane = lax.broadcasted_iota(jnp.int32, (1, LANES), 1)
    low = lane < HEAD_DIM
    zero = jnp.zeros((), BF16)

    def variants(a0, a1):
        return ((jnp.where(low, a0, zero), jnp.where(low, zero, a1)),
                (jnp.where(low, a1, zero), jnp.where(low, zero, a0)))

    cat = lambda refs: jnp.concatenate([r[0] for r in refs], axis=0)
    kb_var = variants(cat((kp_ref, kc_ref, kn_ref)), cat((ksp_ref, ksc_ref, ksn_ref)))
    vb_var = variants(cat((vp_ref, vc_ref, vn_ref)), cat((vsp_ref, vsc_ref, vsn_ref)))
    kx_var = variants(kx_ref[0], ksx_ref[0])
    vx_var = variants(vx_ref[0], vsx_ref[0])

    row = lax.broadcasted_iota(jnp.int32, (2 * BLOCK, BLOCK), 0) & (BLOCK - 1)
    col = lax.broadcasted_iota(jnp.int32, (2 * BLOCK, BLOCK), 1)
    top = lax.broadcasted_iota(jnp.int32, (2 * BLOCK, 1), 0) < BLOCK
    nt_dims = (((1,), (1,)), ((), ()))

    q = q_ref[0]
    att_blocks = [[None] * 4 for _ in range(Q_BLOCKS)]
    for kvh in range(2):
        qst = jnp.concatenate(
            [q[qb * BLOCK:(qb + 1) * BLOCK, pr * LANES:(pr + 1) * LANES]
             for qb in range(Q_BLOCKS) for pr in (2 * kvh, 2 * kvh + 1)], axis=0)
        accs = [None] * Q_BLOCKS
        for half in range(2):
            sk = jnp.where(top, sink_ref[4 * kvh + half], sink_ref[4 * kvh + 2 + half])
            s_ctx = lax.dot_general(qst, kx_var[kvh][half], nt_dims, preferred_element_type=F32)
            for qb in range(Q_BLOCKS):
                g = n * Q_BLOCKS + qb
                qrows = qst[qb * 2 * BLOCK:(qb + 1) * 2 * BLOCK]
                sb = lax.dot_general(qrows, kb_var[kvh][half][qb * BLOCK:(qb + 3) * BLOCK], nt_dims,
                                     preferred_element_type=F32)
                s0 = jnp.where((col >= row) & (g > 0), sb[:, :BLOCK], NEG_INF)
                s1 = sb[:, BLOCK:2 * BLOCK]
                s2 = jnp.where((col <= row) & (g < nblk - 1), sb[:, 2 * BLOCK:], NEG_INF)
                sc = s_ctx[qb * 2 * BLOCK:(qb + 1) * 2 * BLOCK]
                m = jnp.maximum(jnp.maximum(s0, s1), s2)
                m = jnp.maximum(jnp.max(m, axis=-1, keepdims=True), jnp.max(sc, axis=-1, keepdims=True))
                m = jnp.maximum(m, sk)
                p0, p1, p2, pc = (jnp.exp(t - m) for t in (s0, s1, s2, sc))
                den = (jnp.sum(p0 + p1 + p2, axis=-1, keepdims=True) + jnp.sum(pc, axis=-1, keepdims=True)
                       + jnp.exp(sk - m))
                pb = jnp.concatenate([p0, p1, p2], axis=1).astype(BF16)
                o = (jnp.dot(pb, vb_var[kvh][half][qb * BLOCK:(qb + 3) * BLOCK], preferred_element_type=F32)
                     + jnp.dot(pc.astype(BF16), vx_var[kvh][half], preferred_element_type=F32))
                o = o / den
                accs[qb] = o if accs[qb] is None else accs[qb] + o
        for qb in range(Q_BLOCKS):
            att_blocks[qb][2 * kvh] = accs[qb][:BLOCK]
            att_blocks[qb][2 * kvh + 1] = accs[qb][BLOCK:]

    u = u_ref[0]
    vg = vg_ref[0]
    bs = bs_ref[...]
    gm_blocks = [[None] * 4 for _ in range(Q_BLOCKS)]
    for j in range(GM_W // LANES):
        chunks = [vg[c * BLOCK:(c + 1) * BLOCK, j * LANES:(j + 1) * LANES] for c in range(Q_BLOCKS)]
        rhs = jnp.concatenate(
            [jnp.concatenate([jnp.where(low, v, zero) for v in chunks], axis=1),
             jnp.concatenate([jnp.where(low, zero, v) for v in chunks], axis=1)], axis=0)
        mixed = jnp.dot(wcat_ref[j], rhs, preferred_element_type=F32)
        bias = jnp.where(low, bs[:, 2 * j:2 * j + 1], bs[:, 2 * j + 1:2 * j + 2])
        for c in range(Q_BLOCKS):
            gm_blocks[c][j] = (u[c * BLOCK:(c + 1) * BLOCK, j * LANES:(j + 1) * LANES].astype(F32)
                               * (mixed[:, c * LANES:(c + 1) * LANES] + bias))

    mix = jnp.concatenate([jnp.concatenate(att_blocks[c] + gm_blocks[c], axis=1) for c in range(Q_BLOCKS)],
                          axis=0).astype(BF16)
    y = jnp.dot(mix, wout_ref[...], preferred_element_type=F32)
    mod = mod_ref[0, 0]
    o_ref[0] = x_ref[0] + mod[2:3] * y


def _attn_mixer(x, mod, sink, q, k, ksw, v, vsw, kx, ksx, vx, vsx, u, vg, wcat_bf, bs_t, wout_bf):
    b, s, _ = x.shape
    tq = Q_BLOCKS * BLOCK
    nb = s // BLOCK
    l = kx.shape[1]
    cur = lambda w: pl.BlockSpec((1, tq, w), lambda bi, n: (bi, n, 0))
    prv = lambda w: pl.BlockSpec((1, BLOCK, w), lambda bi, n: (bi, jnp.maximum(n * Q_BLOCKS - 1, 0), 0))
    nxt = lambda w: pl.BlockSpec((1, BLOCK, w), lambda bi, n: (bi, jnp.minimum((n + 1) * Q_BLOCKS, nb - 1), 0))
    cx = pl.BlockSpec((1, l, KV_W), lambda bi, n: (bi, 0, 0))
    return pl.pallas_call(
        _attn_kernel,
        grid=(b, s // tq),
        in_specs=[
            pl.BlockSpec(memory_space=pltpu.SMEM),
            cur(ATT_W),
            prv(KV_W), cur(KV_W), nxt(KV_W), prv(KV_W), cur(KV_W), nxt(KV_W),
            prv(KV_W), cur(KV_W), nxt(KV_W), prv(KV_W), cur(KV_W), nxt(KV_W),
            cx, cx, cx, cx,
            cur(GM_W), cur(GM_W),
            pl.BlockSpec((4, BLOCK, 2 * BLOCK), lambda bi, n: (0, 0, 0)),
            pl.BlockSpec((BLOCK, 8), lambda bi, n: (0, 0)),
            pl.BlockSpec((D, D), lambda bi, n: (0, 0)),
            cur(D),
            pl.BlockSpec((1, 1, 6, D), lambda bi, n: (0, bi, 0, 0)),
        ],
        out_specs=cur(D),
        out_shape=jax.ShapeDtypeStruct((b, s, D), F32),
        compiler_params=_cparams(("arbitrary", "arbitrary"), VMEM_LIMIT),
        name="attn_gmlp_out",
    )(sink, q, k, k, k, ksw, ksw, ksw, v, v, v, vsw, vsw, vsw, kx, ksx, vx, vsx,
      u, vg, wcat_bf, bs_t, wout_bf, x, mod)


def _ffn_kernel(x_ref, mod_ref, wg_ref, wu_ref, wd_ref, o_ref):
    mod = mod_ref[0, 0]
    xf = x_ref[...]
    h = _modulate(xf, mod[3:4], mod[4:5]).astype(BF16)
    g = jnp.dot(h, wg_ref[...], preferred_element_type=F32)
    up = jnp.dot(h, wu_ref[...], preferred_element_type=F32)
    a = (g * _sigmoid(g) * up).astype(BF16)
    o_ref[...] = xf + mod[5:6] * jnp.dot(a, wd_ref[...], preferred_element_type=F32)


def _dense_ffn(x2d, mod, wg, wu, wd, seq):
    n = x2d.shape[0]
    f = wg.shape[1]
    tm = TM_FFN
    per_b = seq // tm
    resident = lambda shp: pl.BlockSpec(shp, lambda i: (0, 0), pipeline_mode=pl.Buffered(1))
    return pl.pallas_call(
        _ffn_kernel,
        grid=(n // tm,),
        in_specs=[
            pl.BlockSpec((tm, D), lambda i: (i, 0)),
            pl.BlockSpec((1, 1, 6, D), lambda i: (0, i // per_b, 0, 0)),
            resident((D, f)), resident((D, f)), resident((f, D)),
        ],
        out_specs=pl.BlockSpec((tm, D), lambda i: (i, 0)),
        out_shape=jax.ShapeDtypeStruct((n, D), F32),
        compiler_params=_cparams(("arbitrary",), VMEM_LIMIT),
        name="dense_ffn",
    )(x2d, mod, wg, wu, wd)


def _pool_route_kernel(x_ref, xp_ref, xn_ref, mod_ref, band_ref, pw_ref, psc_ref, wr_hi_ref, wr_lo_ref,
                       tri_ref, x3_ref, h2_ref, route_ref, cnt_ref, hext, carry):
    bi = pl.program_id(0)
    i = pl.program_id(1)
    ni = pl.num_programs(1)
    tm = x_ref.shape[1]
    seq = tm * ni
    mod = mod_ref[0, 0]

    @pl.when((bi == 0) & (i == 0))
    def _():
        carry[...] = jnp.zeros_like(carry)

    xf = x_ref[0]
    hp = _modulate(xp_ref[0], mod[0:1], mod[1:2])
    hn = _modulate(xn_ref[0], mod[0:1], mod[1:2])
    hext[0:POOL_HALO] = jnp.where(i > 0, hp, 0.0).astype(BF16)
    h_main = _modulate(xf, mod[0:1], mod[1:2])
    hext[POOL_HALO:POOL_HALO + tm] = h_main.astype(BF16)
    hext[POOL_HALO + tm:] = jnp.where(i < ni - 1, hn, 0.0).astype(BF16)

    t_local = lax.broadcasted_iota(jnp.int32, (BLOCK, 1), 0)
    ys = []
    for gi, w in enumerate(POOL_SIZES):
        lo_off = -(w // 2)
        hi_off = w - 1 - w // 2
        cols = slice(gi * POOL_GD, (gi + 1) * POOL_GD)
        outs = []
        for sb in range(tm // BLOCK):
            r0 = sb * BLOCK
            win = jnp.dot(band_ref[gi], hext[r0:r0 + BLOCK + 2 * POOL_HALO, cols],
                          preferred_element_type=F32)
            t = i * tm + r0 + t_local
            cnt = (jnp.minimum(t + hi_off, seq - 1) - jnp.maximum(t + lo_off, 0) + 1).astype(F32)
            diff = win / cnt - h_main[r0:r0 + BLOCK, cols]
            outs.append(diff.astype(BF16))
        dg = jnp.concatenate(outs, axis=0)
        ys.append(jnp.dot(dg, pw_ref[gi], preferred_element_type=F32))
    y = jnp.concatenate(ys, axis=1) * psc_ref[...]
    x3 = xf + mod[2:3] * y
    x3_ref[0] = x3

    h2 = _modulate(x3, mod[3:4], mod[4:5])
    _rows_to_slabs(h2, h2_ref)
    h_hi = h2.astype(BF16)
    h_lo = (h2 - h_hi.astype(F32)).astype(BF16)
    logits = (jnp.dot(h_hi, wr_hi_ref[...], preferred_element_type=F32)
              + jnp.dot(h_hi, wr_lo_ref[...], preferred_element_type=F32)
              + jnp.dot(h_lo, wr_hi_ref[...], preferred_element_type=F32))
    lane = lax.broadcasted_iota(jnp.int32, (tm, LANES), 1)
    lane_f = lane.astype(F32)
    neg = -jnp.inf
    lg = jnp.where(lane < N_EXPERTS, logits, neg)
    m1 = jnp.max(lg, axis=-1, keepdims=True)
    i1 = jnp.min(jnp.where(lg == m1, lane_f, float(LANES)), axis=-1, keepdims=True)
    oh1 = lane_f == i1
    lg2 = jnp.where(oh1, neg, lg)
    m2 = jnp.max(lg2, axis=-1, keepdims=True)
    i2 = jnp.min(jnp.where(lg2 == m2, lane_f, float(LANES)), axis=-1, keepdims=True)
    oh2 = lane_f == i2
    e = jnp.exp(m2 - m1)
    w1 = 1.0 / (1.0 + e)
    w2 = e / (1.0 + e)
    oh = jnp.where(oh1 | oh2, 1.0, 0.0)
    before = jnp.dot(tri_ref[...], oh.astype(BF16), preferred_element_type=F32) + carry[...]
    r1 = jnp.sum(jnp.where(oh1, before, 0.0), axis=-1, keepdims=True)
    r2 = jnp.sum(jnp.where(oh2, before, 0.0), axis=-1, keepdims=True)
    carry[...] = carry[...] + jnp.sum(oh, axis=0, keepdims=True)
    cnt_ref[...] = carry[...]
    info = jnp.where(lane == 0, i1, jnp.where(lane == 1, i2, jnp.where(lane == 2, w1, jnp.where(
        lane == 3, w2, jnp.where(lane == 4, r1, jnp.where(lane == 5, r2, 0.0))))))
    route_ref[...] = info.T[0:8, :]


def _pool_route(x, mod, band, pw_bf, pool_scale, wr_hi, wr_lo, tri):
    b, s, _ = x.shape
    tm = TM_POOL
    ni = s // tm
    hb = tm // POOL_HALO
    row = pl.BlockSpec((1, tm, D), lambda bi, i: (bi, i, 0))
    const2 = lambda shp: pl.BlockSpec(shp, lambda bi, i: (0,) * len(shp))
    return pl.pallas_call(
        _pool_route_kernel,
        grid=(b, ni),
        in_specs=[
            row,
            pl.BlockSpec((1, POOL_HALO, D), lambda bi, i: (bi, jnp.maximum(i * hb - 1, 0), 0)),
            pl.BlockSpec((1, POOL_HALO, D), lambda bi, i: (bi, jnp.minimum((i + 1) * hb, s // POOL_HALO - 1), 0)),
            pl.BlockSpec((1, 1, 6, D), lambda bi, i: (1, bi, 0, 0)),
            const2(band.shape), const2(pw_bf.shape), const2((1, D)),
            const2(wr_hi.shape), const2(wr_lo.shape), const2(tri.shape),
        ],
        out_specs=[row,
                   pl.BlockSpec((tm * SLAB, LANES), lambda bi, i: (bi * ni + i, 0)),
                   pl.BlockSpec((8, tm), lambda bi, i: (0, bi * ni + i)),
                   pl.BlockSpec((1, LANES), lambda bi, i: (0, 0))],
        out_shape=[jax.ShapeDtypeStruct((b, s, D), F32), jax.ShapeDtypeStruct((b * s * SLAB, LANES), F32),
                   jax.ShapeDtypeStruct((8, b * s), F32), jax.ShapeDtypeStruct((1, LANES), F32)],
        scratch_shapes=[pltpu.VMEM((tm + 2 * POOL_HALO, D), BF16), pltpu.VMEM((1, LANES), F32)],
        compiler_params=_cparams(("arbitrary", "arbitrary"), VMEM_LIMIT),
        name="pool_route",
    )(x, x, x, mod, band, pw_bf, pool_scale.reshape(1, D), wr_hi, wr_lo, tri)


def _slotmap_kernel(pos_ref, lo_ref, hi_ref, o_ref):
    n_pairs = pos_ref.shape[0]
    spare_mask = 2 * TM_MOE - 1
    for e in range(lo_ref.shape[0]):
        def fill(p, c):
            o_ref[p] = n_pairs + (p & spare_mask)
            return c
        lax.fori_loop(lo_ref[e], hi_ref[e], fill, 0)

    def place(f, c):
        o_ref[pos_ref[f]] = f
        return c
    lax.fori_loop(0, n_pairs, place, 0, unroll=8)


def _slot_map(pos_flat, lo, hi, n_slots):
    smem = pl.BlockSpec(memory_space=pltpu.SMEM)
    return pl.pallas_call(
        _slotmap_kernel,
        in_specs=[smem, smem, smem],
        out_specs=smem,
        out_shape=jax.ShapeDtypeStruct((n_slots,), jnp.int32),
        name="moe_slot_map",
    )(pos_flat, lo, hi)


def _moe_kernel(te_ref, nused_ref, fnext_ref, fprev_ref, f0_ref, h_hbm, wg_hbm, wu_hbm, wd_hbm,
                y_hbm, wg_res, wu_res, wd_res, stg_col, stg_row, xbuf, xb, act, acc, stage, gsem, ssem, wsem):
    i = pl.program_id(0)
    nt = pl.num_programs(0)
    used_tiles = nused_ref[0]
    tm = xb.shape[0]
    f_dim = wg_res.shape[1]
    tok_mask = h_hbm.shape[0] // SLAB - 1
    tile_rows = tm * SLAB

    def slab(ix):
        return pl.ds(pl.multiple_of(ix * SLAB, SLAB), SLAB)

    def gather_row(fref, r, slot, zero=0):
        tok = fref[0, 0, r + zero] & tok_mask
        return pltpu.make_async_copy(h_hbm.at[slab(tok)], xbuf.at[slot, slab(r)], gsem.at[slot])

    def scatter_row(fref, r, slot, zero=0):
        return pltpu.make_async_copy(stage.at[slot, slab(r)], y_hbm.at[slab(fref[0, 0, r + zero])],
                                     ssem.at[slot])

    def gather_all(slot):
        return pltpu.make_async_copy(h_hbm.at[pl.ds(0, tile_rows)], xbuf.at[slot], gsem.at[slot])

    def scatter_all(slot):
        return pltpu.make_async_copy(stage.at[slot], y_hbm.at[pl.ds(0, tile_rows)], ssem.at[slot])

    cur = i % 2
    used = i < used_tiles

    @pl.when(i == 0)
    def _():
        stage[...] = jnp.zeros_like(stage)
        spare = y_hbm.shape[0] - 2 * tile_rows
        fills = [pltpu.make_async_copy(stage.at[sl], y_hbm.at[pl.ds(spare + sl * tile_rows, tile_rows)],
                                       ssem.at[sl])
                 for sl in range(2)]
        for cp in fills:
            cp.start()
        for cp in fills:
            cp.wait()

    @pl.when((i == 0) & used)
    def _():
        def prime(r, c):
            gather_row(f0_ref, r, 0).start()
            return c
        lax.fori_loop(0, tm, prime, 0)

    expert = te_ref[i]
    new_expert = used & ((i == 0) | (expert != te_ref[jnp.maximum(i - 1, 0)]))

    @pl.when(new_expert)
    def _():
        n_c = f_dim // W_CHUNK
        plan = [(w, res, False, c) for w, res in ((wg_hbm, wg_res), (wu_hbm, wu_res)) for c in range(n_c)]
        plan += [(wd_hbm, wd_res, True, c) for c in range(n_c)]

        def chunk_copy(k):
            w_hbm, _, by_rows, c = plan[k]
            cs = pl.ds(c * W_CHUNK, W_CHUNK)
            buf = k % W_RING
            if by_rows:
                return pltpu.make_async_copy(w_hbm.at[expert, cs, :], stg_row.at[buf], wsem.at[W_RING + buf])
            return pltpu.make_async_copy(w_hbm.at[expert, :, cs], stg_col.at[buf], wsem.at[buf])

        for k in range(W_RING):
            chunk_copy(k).start()
        for k in range(len(plan)):
            _, res, by_rows, c = plan[k]
            chunk_copy(k).wait()
            cs = slice(c * W_CHUNK, (c + 1) * W_CHUNK)
            if by_rows:
                res[cs, :] = stg_row[k % W_RING].astype(BF16)
            else:
                res[:, cs] = stg_col[k % W_RING].astype(BF16)
            if k + W_RING < len(plan):
                chunk_copy(k + W_RING).start()

    @pl.when(used)
    def _():
        gather_all(cur).wait()
        xb[...] = _slabs_to_rows(xbuf.at[cur], tm).astype(BF16)

    @pl.when((i >= 2) & (i - 2 < used_tiles))
    def _():
        scatter_all(cur).wait()

    half = f_dim // MOE_SPLIT
    n_piece = half // W_CHUNK
    n_groups = MOE_SPLIT * (n_piece + 1)

    def tick(v):
        bits = jnp.max(lax.bitcast_convert_type(v[0:SLAB, 0:LANES], jnp.int32))
        return lax.shift_right_logical(lax.shift_right_logical(bits, 16), 16)

    def expert_ffn(issue_group):
        xv = xb[...]
        out = None
        issue_group(0, 0)
        k = 1
        for hf in range(MOE_SPLIT):
            for c in range(n_piece):
                cs = slice(hf * half + c * W_CHUNK, hf * half + (c + 1) * W_CHUNK)
                g = jnp.dot(xv, wg_res[:, cs], preferred_element_type=F32)
                up = jnp.dot(xv, wu_res[:, cs], preferred_element_type=F32)
                act[:, c * W_CHUNK:(c + 1) * W_CHUNK] = (g * _sigmoid(g) * up).astype(BF16)
                issue_group(k, tick(g))
                k += 1
            part = jnp.dot(act[...], wd_res[hf * half:(hf + 1) * half, :], preferred_element_type=F32)
            if hf < MOE_SPLIT - 1:
                acc[...] = part if out is None else acc[...] + part
                out = acc
                issue_group(k, tick(part))
                k += 1
            else:
                _rows_to_slabs(part if out is None else acc[...] + part, stage.at[cur])

    has_next = i + 1 < used_tiles
    has_prev = (i >= 1) & (i - 1 < used_tiles)
    steady = (i >= 2) & has_next

    @pl.when(steady)
    def _():
        def issue_group(k, zero):
            for r in range(k * tm // n_groups, (k + 1) * tm // n_groups):
                gather_row(fnext_ref, r, 1 - cur, zero).start()
                scatter_row(fprev_ref, r, 1 - cur, zero).start()
        expert_ffn(issue_group)

    @pl.when(jnp.logical_not(steady))
    def _():
        @pl.when(used)
        def _():
            expert_ffn(lambda k, zero: None)

        @pl.when(has_next)
        def _():
            def issue(r, c):
                gather_row(fnext_ref, r, 1 - cur).start()
                return c
            lax.fori_loop(0, tm, issue, 0)

        @pl.when(has_prev)
        def _():
            def issue(r, c):
                scatter_row(fprev_ref, r, 1 - cur).start()
                return c
            lax.fori_loop(0, tm, issue, 0)

    @pl.when((i == nt - 1) & (nt - 2 < used_tiles))
    def _():
        scatter_all(1 - cur).wait()


def _moe_experts(h_slabs, fmap, tile_expert, n_used, wg, wu, wd, n_tiles, y_rows):
    f = wg.shape[2]
    tm = TM_MOE
    fblk = lambda imap: pl.BlockSpec((1, 1, tm), imap, memory_space=pltpu.SMEM)
    hbm = pl.BlockSpec(memory_space=pl.ANY)
    return pl.pallas_call(
        _moe_kernel,
        grid_spec=pltpu.PrefetchScalarGridSpec(
            num_scalar_prefetch=2,
            grid=(n_tiles,),
            in_specs=[
                fblk(lambda i, te, nu: (jnp.minimum(i + 1, n_tiles - 1), 0, 0)),
                fblk(lambda i, te, nu: (jnp.maximum(i - 1, 0), 0, 0)),
                fblk(lambda i, te, nu: (0, 0, 0)),
                hbm, hbm, hbm, hbm,
            ],
            out_specs=hbm,
            scratch_shapes=[pltpu.VMEM((D, f), BF16), pltpu.VMEM((D, f), BF16), pltpu.VMEM((f, D), BF16),
                            pltpu.VMEM((W_RING, D, W_CHUNK), F32), pltpu.VMEM((W_RING, W_CHUNK, D), F32),
                            pltpu.VMEM((2, tm * SLAB, LANES), F32), pltpu.VMEM((tm, D), BF16),
                            pltpu.VMEM((tm, f // MOE_SPLIT), BF16), pltpu.VMEM((tm, D), F32),
                            pltpu.VMEM((2, tm * SLAB, LANES), F32),
                            pltpu.SemaphoreType.DMA((2,)), pltpu.SemaphoreType.DMA((2,)),
                            pltpu.SemaphoreType.DMA((2 * W_RING,))],
        ),
        out_shape=jax.ShapeDtypeStruct((y_rows * SLAB, LANES), F32),
        compiler_params=_cparams(("arbitrary",), MOE_VMEM_LIMIT),
        name="moe_experts",
    )(tile_expert, n_used, fmap, fmap, fmap, h_slabs, wg, wu, wd)


def _combine_kernel(y1_ref, y2_ref, x_ref, w_ref, mod_ref, gain_ref, o_ref):
    w = w_ref[...]
    rows = x_ref.shape[0]
    moe = w[:, 0:1] * _slabs_to_rows(y1_ref, rows) + w[:, 1:2] * _slabs_to_rows(y2_ref, rows)
    mod = mod_ref[0, 0]
    x4 = x_ref[...] + mod[5:6] * moe
    ms = jnp.mean(x4 * x4, axis=-1, keepdims=True)
    o_ref[...] = x4 * lax.rsqrt(ms + EPS) * gain_ref[...]


def _combine(y, x3_2d, wts, mod, final_gain, seq):
    n = x3_2d.shape[0]
    tc = TC_COMB
    nt = n // tc
    per_b = seq // tc
    return pl.pallas_call(
        _combine_kernel,
        grid=(nt,),
        in_specs=[
            pl.BlockSpec((tc * SLAB, LANES), lambda i: (i, 0)),
            pl.BlockSpec((tc * SLAB, LANES), lambda i: (i + nt, 0)),
            pl.BlockSpec((tc, D), lambda i: (i, 0)),
            pl.BlockSpec((tc, 2), lambda i: (i, 0)),
            pl.BlockSpec((1, 1, 6, D), lambda i: (1, i // per_b, 0, 0)),
            pl.BlockSpec((1, D), lambda i: (0, 0)),
        ],
        out_specs=pl.BlockSpec((tc, D), lambda i: (i, 0)),
        out_shape=jax.ShapeDtypeStruct((n, D), F32),
        compiler_params=_cparams(("arbitrary",)),
        name="moe_combine",
    )(y, y, x3_2d, wts, mod, final_gain.reshape(1, D))


def _rope_tables(seq):
    rows = seq // GRID_W
    row_pos = jnp.repeat(jnp.arange(rows, dtype=F32), GRID_W)
    col_pos = jnp.tile(jnp.arange(GRID_W, dtype=F32), rows)
    axis_dim = HEAD_DIM // 2
    inv_freq = ROPE_BASE ** (-jnp.arange(0, axis_dim, 2, dtype=F32) / axis_dim)
    ar = row_pos[:, None] * inv_freq
    ac = col_pos[:, None] * inv_freq
    cos64 = jnp.concatenate([jnp.cos(ar), jnp.cos(ar), jnp.cos(ac), jnp.cos(ac)], axis=1)
    sin64 = jnp.concatenate([-jnp.sin(ar), jnp.sin(ar), -jnp.sin(ac), jnp.sin(ac)], axis=1)
    return jnp.tile(cos64, (1, 2)), jnp.tile(sin64, (1, 2))


def _band_matrices():
    r = np.arange(BLOCK)[:, None]
    c = np.arange(BLOCK + 2 * POOL_HALO)[None, :] - POOL_HALO
    mats = []
    for w in POOL_SIZES:
        lo = -(w // 2)
        hi = w - 1 - w // 2
        mats.append(((c >= r + lo) & (c <= r + hi)).astype(np.float32))
    return jnp.asarray(np.stack(mats), dtype=BF16)


def kernel(x, c, ctx, c_ctx, w_ada, b_ada, w_in, attn_sink, gm_gain, gm_w_s, gm_b_s, w_out,
           ffn_w_gate, ffn_w_up, ffn_w_down, pool_w, pool_scale, router_w,
           moe_w_gate, moe_w_up, moe_w_down, final_gain):
    b, s, _ = x.shape
    n = b * s
    assert w_ada.shape[0] == 2 and w_in.shape[0] == 1 and pool_w.shape[0] == 1
    assert s % TM_IN == 0 and s % TM_POOL == 0 and s % TM_FFN == 0 and b <= 4
    assert n & (n - 1) == 0

    cvec = jnp.concatenate([c, c_ctx[None, :], jnp.zeros((8 - b - 1, D), F32)], axis=0)
    mod = _ada_mod(cvec, w_ada, b_ada)

    cos_t, sin_t = _rope_tables(s)
    w_in_bf = w_in[0].astype(BF16)
    q, k, ksw, v, vsw, u, vg = _in_proj(x, mod, w_in_bf, gm_gain[0], cos_t, sin_t)
    kx, ksx, vx, vsx = _ctx_kv(ctx, mod, w_in_bf)
    wcat = gm_w_s[0].reshape(4, 2, BLOCK, BLOCK).transpose(0, 2, 1, 3).reshape(4, BLOCK, 2 * BLOCK).astype(BF16)
    x1 = _attn_mixer(x, mod, attn_sink[0], q, k, ksw, v, vsw, kx, ksx, vx, vsx, u, vg,
                     wcat, gm_b_s[0].T, w_out[0].astype(BF16))
    x2 = _dense_ffn(x1.reshape(n, D), mod, ffn_w_gate[0].astype(BF16), ffn_w_up[0].astype(BF16),
                    ffn_w_down[0].astype(BF16), s)

    wr = jnp.pad(router_w[0], ((0, 0), (0, LANES - N_EXPERTS)))
    wr_hi = wr.astype(BF16)
    wr_lo = (wr - wr_hi.astype(F32)).astype(BF16)
    tri = jnp.asarray(np.tril(np.ones((TM_POOL, TM_POOL), np.float32), -1), dtype=BF16)
    x3, h2, route, counts = _pool_route(x2.reshape(b, s, D), mod, _band_matrices(), pool_w[0].astype(BF16),
                                        pool_scale[0], wr_hi, wr_lo, tri)

    tm = TM_MOE
    n_tiles = (2 * n) // tm + N_EXPERTS
    cnt = counts[0, :N_EXPERTS].astype(jnp.int32)
    tiles_e = (cnt + tm - 1) // tm
    tile_end = jnp.cumsum(tiles_e)
    off = (tile_end - tiles_e) * tm
    n_used = tile_end[-1]
    tix = jnp.arange(n_tiles, dtype=jnp.int32)
    te = jnp.minimum(jnp.sum(tix[:, None] >= tile_end[None, :], axis=1), N_EXPERTS - 1).astype(jnp.int32)
    te_last = te[jnp.maximum(n_used - 1, 0)]
    tile_expert = jnp.where(tix < n_used, te, te_last)
    e1 = route[0].astype(jnp.int32)
    e2 = route[1].astype(jnp.int32)
    pos1 = off[e1] + route[4].astype(jnp.int32)
    pos2 = off[e2] + route[5].astype(jnp.int32)
    n_slots = n_tiles * tm
    pad_lo = jnp.concatenate([off + cnt, (n_used * tm).reshape(1)]).astype(jnp.int32)
    pad_hi = jnp.concatenate([off + tiles_e * tm, jnp.full((1,), n_slots, jnp.int32)]).astype(jnp.int32)
    fmap = _slot_map(jnp.concatenate([pos1, pos2]), pad_lo, pad_hi, n_slots)
    n_used_arr = n_used.reshape(1).astype(jnp.int32)

    y = _moe_experts(h2, fmap.reshape(n_tiles, 1, tm), tile_expert, n_used_arr,
                     moe_w_gate[0], moe_w_up[0], moe_w_down[0], n_tiles, 2 * n + 2 * tm)
    out = _combine(y, x3.reshape(n, D), route[2:4].T, mod, final_gain, s)
    return out.reshape(b, s, D)
```

```python
import functools

import numpy as np
import jax
import jax.numpy as jnp
from jax import lax
from jax.experimental import pallas as pl
from jax.experimental.pallas import tpu as pltpu

F32 = jnp.float32
BF16 = jnp.bfloat16

D = 1024
GRID_W = 64
EPS = 1e-6
NEG_INF = -1e30
HEAD_DIM = 64
N_Q_HEADS = 8
BLOCK = 128
ATT_W = 512
KV_W = 128
GM_W = 512
IN_W = 1792
POOL_SIZES = (2, 4, 8, 16)
POOL_GD = 256
POOL_HALO = 16
N_EXPERTS = 8
ROPE_BASE = 10000.0
LANES = 128
SLAB = D // LANES
SQRT_2_OVER_PI = 0.7978845608028654

TM_IN = 512
TM_FFN = 512
TM_POOL = 512
TM_MOE = 512
MOE_SPLIT = 2
W_CHUNK = 256
W_ROWS_IN = 128
W_ROWS_OUT = 512
W_RING = 3
MOE_VMEM_LIMIT = 60 * 1024 * 1024
TC_COMB = 512
Q_BLOCKS = 4
VMEM_LIMIT = 56 * 1024 * 1024


def _cparams(sem, vmem=None):
    return pltpu.CompilerParams(dimension_semantics=sem, vmem_limit_bytes=vmem)


def _modulate(xf, shift, scale):
    ms = jnp.mean(xf * xf, axis=-1, keepdims=True)
    return xf * lax.rsqrt(ms + EPS) * (1.0 + scale) + shift


def _sigmoid(z):
    return 1.0 / (1.0 + jnp.exp(-z))


def _rows_to_slabs(val, slab_ref):
    rows = val.shape[0]
    for cix in range(SLAB):
        slab_ref[pl.ds(cix, rows, stride=SLAB), :] = val[:, cix * LANES:(cix + 1) * LANES]


def _slabs_to_rows(slab_ref, rows):
    return jnp.concatenate([slab_ref[pl.ds(cix, rows, stride=SLAB), :] for cix in range(SLAB)], axis=1)


def _ada_kernel(c_ref, w_ref, b_ref, o_ref):
    c = c_ref[...]
    s = c * _sigmoid(c)
    o_ref[0] = jnp.dot(s, w_ref[0], precision=lax.Precision.HIGHEST,
                       preferred_element_type=F32) + b_ref[0]


def _ada_mod(cvec, w_ada, b_ada):
    depth, _, n6 = w_ada.shape
    tn = 1536
    out = pl.pallas_call(
        _ada_kernel,
        grid=(depth, n6 // tn),
        in_specs=[
            pl.BlockSpec((8, D), lambda l, j: (0, 0)),
            pl.BlockSpec((1, D, tn), lambda l, j: (l, 0, j)),
            pl.BlockSpec((1, 1, tn), lambda l, j: (l, 0, j)),
        ],
        out_specs=pl.BlockSpec((1, 8, tn), lambda l, j: (l, 0, j)),
        out_shape=jax.ShapeDtypeStruct((depth, 8, n6), F32),
        compiler_params=_cparams(("arbitrary", "arbitrary")),
        name="ada_mod",
    )(cvec, w_ada, b_ada.reshape(depth, 1, n6))
    return out.reshape(depth, 8, 6, D)


def _rope(t, cs, sn, first_half):
    fwd = pltpu.roll(t, LANES - 16, axis=1)
    bwd = pltpu.roll(t, 16, axis=1)
    return t * cs + jnp.where(first_half, fwd, bwd) * sn


def _inproj_kernel(x_ref, mod_ref, w_ref, gain_ref, cos_ref, sin_ref,
                   q_ref, k_ref, ksw_ref, v_ref, vsw_ref, u_ref, vg_ref):
    mod = mod_ref[0, 0]
    h = _modulate(x_ref[0], mod[0:1], mod[1:2]).astype(BF16)
    proj = jnp.dot(h, w_ref[...], preferred_element_type=F32)
    cs = cos_ref[...]
    sn = sin_ref[...]
    lane = lax.broadcasted_iota(jnp.int32, cs.shape, 1)
    first_half = (lane & 16) == 0
    for cix in range(ATT_W // LANES):
        t = proj[:, cix * LANES:(cix + 1) * LANES]
        q_ref[0, :, cix * LANES:(cix + 1) * LANES] = (
            _rope(t, cs, sn, first_half) * (HEAD_DIM ** -0.5)).astype(BF16)
    kr = _rope(proj[:, ATT_W:ATT_W + KV_W], cs, sn, first_half)
    k_ref[0] = kr.astype(BF16)
    ksw_ref[0] = pltpu.roll(kr, HEAD_DIM, axis=1).astype(BF16)
    vv = proj[:, ATT_W + KV_W:ATT_W + 2 * KV_W]
    v_ref[0] = vv.astype(BF16)
    vsw_ref[0] = pltpu.roll(vv, HEAD_DIM, axis=1).astype(BF16)
    z = proj[:, ATT_W + 2 * KV_W:]
    g = z * (0.5 * (1.0 + jnp.tanh(SQRT_2_OVER_PI * (z + 0.044715 * (z * z * z)))))
    u_ref[0] = g[:, :GM_W].astype(BF16)
    vg = g[:, GM_W:]
    ms = jnp.mean(vg * vg, axis=-1, keepdims=True)
    vg_ref[0] = (vg * lax.rsqrt(ms + EPS) * gain_ref[...]).astype(BF16)


def _in_proj(x, mod, w_in_bf, gm_gain, cos_t, sin_t):
    b, s, _ = x.shape
    tm = TM_IN
    row = lambda w: pl.BlockSpec((1, tm, w), lambda bi, i: (bi, i, 0))
    outs = pl.pallas_call(
        _inproj_kernel,
        grid=(b, s // tm),
        in_specs=[
            row(D),
            pl.BlockSpec((1, 1, 6, D), lambda bi, i: (0, bi, 0, 0)),
            pl.BlockSpec((D, IN_W), lambda bi, i: (0, 0)),
            pl.BlockSpec((1, GM_W), lambda bi, i: (0, 0)),
            pl.BlockSpec((tm, LANES), lambda bi, i: (i, 0)),
            pl.BlockSpec((tm, LANES), lambda bi, i: (i, 0)),
        ],
        out_specs=[row(ATT_W), row(KV_W), row(KV_W), row(KV_W), row(KV_W), row(GM_W), row(GM_W)],
        out_shape=[jax.ShapeDtypeStruct((b, s, w), BF16)
                   for w in (ATT_W, KV_W, KV_W, KV_W, KV_W, GM_W, GM_W)],
        compiler_params=_cparams(("arbitrary", "arbitrary"), VMEM_LIMIT),
        name="in_proj",
    )(x, mod, w_in_bf, gm_gain.reshape(1, GM_W), cos_t, sin_t)
    return outs


def _ctx_kernel(c_ref, mod_ref, w_ref, k_ref, ksw_ref, v_ref, vsw_ref):
    mod = mod_ref[0, 0]
    h = _modulate(c_ref[0], mod[0:1], mod[1:2]).astype(BF16)
    kv = jnp.dot(h, w_ref[...], preferred_element_type=F32)
    kk = kv[:, :KV_W]
    vv = kv[:, KV_W:]
    k_ref[0] = kk.astype(BF16)
    ksw_ref[0] = pltpu.roll(kk, HEAD_DIM, axis=1).astype(BF16)
    v_ref[0] = vv.astype(BF16)
    vsw_ref[0] = pltpu.roll(vv, HEAD_DIM, axis=1).astype(BF16)


def _ctx_kv(ctx, mod, w_in_bf):
    b, l, _ = ctx.shape
    spec = pl.BlockSpec((1, l, KV_W), lambda bi: (bi, 0, 0))
    return pl.pallas_call(
        _ctx_kernel,
        grid=(b,),
        in_specs=[
            pl.BlockSpec((1, l, D), lambda bi: (bi, 0, 0)),
            pl.BlockSpec((1, 1, 6, D), lambda bi: (0, b, 0, 0)),
            pl.BlockSpec((D, 2 * KV_W), lambda bi: (0, ATT_W // (2 * KV_W))),
        ],
        out_specs=[spec] * 4,
        out_shape=[jax.ShapeDtypeStruct((b, l, KV_W), BF16)] * 4,
        compiler_params=_cparams(("arbitrary",)),
        name="ctx_kv",
    )(ctx, mod, w_in_bf)


def _attn_kernel(sink_ref, q_ref, kp_ref, kc_ref, kn_ref, ksp_ref, ksc_ref, ksn_ref,
                 vp_ref, vc_ref, vn_ref, vsp_ref, vsc_ref, vsn_ref,
                 kx_ref, ksx_ref, vx_ref, vsx_ref,
                 u_ref, vg_ref, wcat_ref, bs_ref, wout_ref, x_ref, mod_ref, o_ref):
    n = pl.program_id(1)
    nblk = pl.num_programs(1) * Q_BLOCKS
    lane = lax.broadcasted_iota(jnp.int32, (1, LANES), 1)
    low = lane < HEAD_DIM
    zero = jnp.zeros((), BF16)

    def variants(a0, a1):
        return ((jnp.where(low, a0, zero), jnp.where(low, zero, a1)),
                (jnp.where(low, a1, zero), jnp.where(low, zero, a0)))

    cat = lambda refs: jnp.concatenate([r[0] for r in refs], axis=0)
    kb_var = variants(cat((kp_ref, kc_ref, kn_ref)), cat((ksp_ref, ksc_ref, ksn_ref)))
    vb_var = variants(cat((vp_ref, vc_ref, vn_ref)), cat((vsp_ref, vsc_ref, vsn_ref)))
    kx_var = variants(kx_ref[0], ksx_ref[0])
    vx_var = variants(vx_ref[0], vsx_ref[0])

    row = lax.broadcasted_iota(jnp.int32, (2 * BLOCK, BLOCK), 0) & (BLOCK - 1)
    col = lax.broadcasted_iota(jnp.int32, (2 * BLOCK, BLOCK), 1)
    top = lax.broadcasted_iota(jnp.int32, (2 * BLOCK, 1), 0) < BLOCK
    nt_dims = (((1,), (1,)), ((), ()))

    q = q_ref[0]
    att_blocks = [[None] * 4 for _ in range(Q_BLOCKS)]
    for kvh in range(2):
        qst = jnp.concatenate(
            [q[qb * BLOCK:(qb + 1) * BLOCK, pr * LANES:(pr + 1) * LANES]
             for qb in range(Q_BLOCKS) for pr in (2 * kvh, 2 * kvh + 1)], axis=0)
        accs = [None] * Q_BLOCKS
        for half in range(2):
            sk = jnp.where(top, sink_ref[4 * kvh + half], sink_ref[4 * kvh + 2 + half])
            s_ctx = lax.dot_general(qst, kx_var[kvh][half], nt_dims, preferred_element_type=F32)
            for qb in range(Q_BLOCKS):
                g = n * Q_BLOCKS + qb
                qrows = qst[qb * 2 * BLOCK:(qb + 1) * 2 * BLOCK]
                sb = lax.dot_general(qrows, kb_var[kvh][half][qb * BLOCK:(qb + 3) * BLOCK], nt_dims,
                                     preferred_element_type=F32)
                s0 = jnp.where((col >= row) & (g > 0), sb[:, :BLOCK], NEG_INF)
                s1 = sb[:, BLOCK:2 * BLOCK]
                s2 = jnp.where((col <= row) & (g < nblk - 1), sb[:, 2 * BLOCK:], NEG_INF)
                sc = s_ctx[qb * 2 * BLOCK:(qb + 1) * 2 * BLOCK]
                m = jnp.maximum(jnp.maximum(s0, s1), s2)
                m = jnp.maximum(jnp.max(m, axis=-1, keepdims=True), jnp.max(sc, axis=-1, keepdims=True))
                m = jnp.maximum(m, sk)
                p0, p1, p2, pc = (jnp.exp(t - m) for t in (s0, s1, s2, sc))
                den = (jnp.sum(p0 + p1 + p2, axis=-1, keepdims=True) + jnp.sum(pc, axis=-1, keepdims=True)
                       + jnp.exp(sk - m))
                pb = jnp.concatenate([p0, p1, p2], axis=1).astype(BF16)
                o = (jnp.dot(pb, vb_var[kvh][half][qb * BLOCK:(qb + 3) * BLOCK], preferred_element_type=F32)
                     + jnp.dot(pc.astype(BF16), vx_var[kvh][half], preferred_element_type=F32))
                o = o / den
                accs[qb] = o if accs[qb] is None else accs[qb] + o
        for qb in range(Q_BLOCKS):
            att_blocks[qb][2 * kvh] = accs[qb][:BLOCK]
            att_blocks[qb][2 * kvh + 1] = accs[qb][BLOCK:]

    u = u_ref[0]
    vg = vg_ref[0]
    bs = bs_ref[...]
    gm_blocks = [[None] * 4 for _ in range(Q_BLOCKS)]
    for j in range(GM_W // LANES):
        chunks = [vg[c * BLOCK:(c + 1) * BLOCK, j * LANES:(j + 1) * LANES] for c in range(Q_BLOCKS)]
        rhs = jnp.concatenate(
            [jnp.concatenate([jnp.where(low, v, zero) for v in chunks], axis=1),
             jnp.concatenate([jnp.where(low, zero, v) for v in chunks], axis=1)], axis=0)
        mixed = jnp.dot(wcat_ref[j], rhs, preferred_element_type=F32)
        bias = jnp.where(low, bs[:, 2 * j:2 * j + 1], bs[:, 2 * j + 1:2 * j + 2])
        for c in range(Q_BLOCKS):
            gm_blocks[c][j] = (u[c * BLOCK:(c + 1) * BLOCK, j * LANES:(j + 1) * LANES].astype(F32)
                               * (mixed[:, c * LANES:(c + 1) * LANES] + bias))

    mix = jnp.concatenate([jnp.concatenate(att_blocks[c] + gm_blocks[c], axis=1) for c in range(Q_BLOCKS)],
                          axis=0).astype(BF16)
    y = jnp.dot(mix, wout_ref[...], preferred_element_type=F32)
    mod = mod_ref[0, 0]
    o_ref[0] = x_ref[0] + mod[2:3] * y


def _attn_mixer(x, mod, sink, q, k, ksw, v, vsw, kx, ksx, vx, vsx, u, vg, wcat_bf, bs_t, wout_bf):
    b, s, _ = x.shape
    tq = Q_BLOCKS * BLOCK
    nb = s // BLOCK
    l = kx.shape[1]
    cur = lambda w: pl.BlockSpec((1, tq, w), lambda bi, n: (bi, n, 0))
    prv = lambda w: pl.BlockSpec((1, BLOCK, w), lambda bi, n: (bi, jnp.maximum(n * Q_BLOCKS - 1, 0), 0))
    nxt = lambda w: pl.BlockSpec((1, BLOCK, w), lambda bi, n: (bi, jnp.minimum((n + 1) * Q_BLOCKS, nb - 1), 0))
    cx = pl.BlockSpec((1, l, KV_W), lambda bi, n: (bi, 0, 0))
    return pl.pallas_call(
        _attn_kernel,
        grid=(b, s // tq),
        in_specs=[
            pl.BlockSpec(memory_space=pltpu.SMEM),
            cur(ATT_W),
            prv(KV_W), cur(KV_W), nxt(KV_W), prv(KV_W), cur(KV_W), nxt(KV_W),
            prv(KV_W), cur(KV_W), nxt(KV_W), prv(KV_W), cur(KV_W), nxt(KV_W),
            cx, cx, cx, cx,
            cur(GM_W), cur(GM_W),
            pl.BlockSpec((4, BLOCK, 2 * BLOCK), lambda bi, n: (0, 0, 0)),
            pl.BlockSpec((BLOCK, 8), lambda bi, n: (0, 0)),
            pl.BlockSpec((D, D), lambda bi, n: (0, 0)),
            cur(D),
            pl.BlockSpec((1, 1, 6, D), lambda bi, n: (0, bi, 0, 0)),
        ],
        out_specs=cur(D),
        out_shape=jax.ShapeDtypeStruct((b, s, D), F32),
        compiler_params=_cparams(("arbitrary", "arbitrary"), VMEM_LIMIT),
        name="attn_gmlp_out",
    )(sink, q, k, k, k, ksw, ksw, ksw, v, v, v, vsw, vsw, vsw, kx, ksx, vx, vsx,
      u, vg, wcat_bf, bs_t, wout_bf, x, mod)


def _ffn_kernel(x_ref, mod_ref, wg_ref, wu_ref, wd_ref, o_ref):
    mod = mod_ref[0, 0]
    xf = x_ref[...]
    h = _modulate(xf, mod[3:4], mod[4:5]).astype(BF16)
    g = jnp.dot(h, wg_ref[...], preferred_element_type=F32)
    up = jnp.dot(h, wu_ref[...], preferred_element_type=F32)
    a = (g * _sigmoid(g) * up).astype(BF16)
    o_ref[...] = xf + mod[5:6] * jnp.dot(a, wd_ref[...], preferred_element_type=F32)


def _dense_ffn(x2d, mod, wg, wu, wd, seq):
    n = x2d.shape[0]
    f = wg.shape[1]
    tm = TM_FFN
    per_b = seq // tm
    resident = lambda shp: pl.BlockSpec(shp, lambda i: (0, 0), pipeline_mode=pl.Buffered(1))
    return pl.pallas_call(
        _ffn_kernel,
        grid=(n // tm,),
        in_specs=[
            pl.BlockSpec((tm, D), lambda i: (i, 0)),
            pl.BlockSpec((1, 1, 6, D), lambda i: (0, i // per_b, 0, 0)),
            resident((D, f)), resident((D, f)), resident((f, D)),
        ],
        out_specs=pl.BlockSpec((tm, D), lambda i: (i, 0)),
        out_shape=jax.ShapeDtypeStruct((n, D), F32),
        compiler_params=_cparams(("arbitrary",), VMEM_LIMIT),
        name="dense_ffn",
    )(x2d, mod, wg, wu, wd)


def _pool_route_kernel(x_ref, xp_ref, xn_ref, mod_ref, band_ref, pw_ref, psc_ref, wr_hi_ref, wr_lo_ref,
                       tri_ref, x3_ref, h2_ref, route_ref, cnt_ref, hext, carry):
    bi = pl.program_id(0)
    i = pl.program_id(1)
    ni = pl.num_programs(1)
    tm = x_ref.shape[1]
    seq = tm * ni
    mod = mod_ref[0, 0]

    @pl.when((bi == 0) & (i == 0))
    def _():
        carry[...] = jnp.zeros_like(carry)

    xf = x_ref[0]
    hp = _modulate(xp_ref[0], mod[0:1], mod[1:2])
    hn = _modulate(xn_ref[0], mod[0:1], mod[1:2])
    hext[0:POOL_HALO] = jnp.where(i > 0, hp, 0.0).astype(BF16)
    h_main = _modulate(xf, mod[0:1], mod[1:2])
    hext[POOL_HALO:POOL_HALO + tm] = h_main.astype(BF16)
    hext[POOL_HALO + tm:] = jnp.where(i < ni - 1, hn, 0.0).astype(BF16)

    t_local = lax.broadcasted_iota(jnp.int32, (BLOCK, 1), 0)
    ys = []
    for gi, w in enumerate(POOL_SIZES):
        lo_off = -(w // 2)
        hi_off = w - 1 - w // 2
        cols = slice(gi * POOL_GD, (gi + 1) * POOL_GD)
        outs = []
        for sb in range(tm // BLOCK):
            r0 = sb * BLOCK
            win = jnp.dot(band_ref[gi], hext[r0:r0 + BLOCK + 2 * POOL_HALO, cols],
                          preferred_element_type=F32)
            t = i * tm + r0 + t_local
            cnt = (jnp.minimum(t + hi_off, seq - 1) - jnp.maximum(t + lo_off, 0) + 1).astype(F32)
            diff = win / cnt - h_main[r0:r0 + BLOCK, cols]
            outs.append(diff.astype(BF16))
        dg = jnp.concatenate(outs, axis=0)
        ys.append(jnp.dot(dg, pw_ref[gi], preferred_element_type=F32))
    y = jnp.concatenate(ys, axis=1) * psc_ref[...]
    x3 = xf + mod[2:3] * y
    x3_ref[0] = x3

    h2 = _modulate(x3, mod[3:4], mod[4:5])
    _rows_to_slabs(h2, h2_ref)
    h_hi = h2.astype(BF16)
    h_lo = (h2 - h_hi.astype(F32)).astype(BF16)
    logits = (jnp.dot(h_hi, wr_hi_ref[...], preferred_element_type=F32)
              + jnp.dot(h_hi, wr_lo_ref[...], preferred_element_type=F32)
              + jnp.dot(h_lo, wr_hi_ref[...], preferred_element_type=F32))
    lane = lax.broadcasted_iota(jnp.int32, (tm, LANES), 1)
    lane_f = lane.astype(F32)
    neg = -jnp.inf
    lg = jnp.where(lane < N_EXPERTS, logits, neg)
    m1 = jnp.max(lg, axis=-1, keepdims=True)
    i1 = jnp.min(jnp.where(lg == m1, lane_f, float(LANES)), axis=-1, keepdims=True)
    oh1 = lane_f == i1
    lg2 = jnp.where(oh1, neg, lg)
    m2 = jnp.max(lg2, axis=-1, keepdims=True)
    i2 = jnp.min(jnp.where(lg2 == m2, lane_f, float(LANES)), axis=-1, keepdims=True)
    oh2 = lane_f == i2
    e = jnp.exp(m2 - m1)
    w1 = 1.0 / (1.0 + e)
    w2 = e / (1.0 + e)
    oh = jnp.where(oh1 | oh2, 1.0, 0.0)
    before = jnp.dot(tri_ref[...], oh.astype(BF16), preferred_element_type=F32) + carry[...]
    r1 = jnp.sum(jnp.where(oh1, before, 0.0), axis=-1, keepdims=True)
    r2 = jnp.sum(jnp.where(oh2, before, 0.0), axis=-1, keepdims=True)
    carry[...] = carry[...] + jnp.sum(oh, axis=0, keepdims=True)
    cnt_ref[...] = carry[...]
    info = jnp.where(lane == 0, i1, jnp.where(lane == 1, i2, jnp.where(lane == 2, w1, jnp.where(
        lane == 3, w2, jnp.where(lane == 4, r1, jnp.where(lane == 5, r2, 0.0))))))
    route_ref[...] = info.T[0:8, :]


def _pool_route(x, mod, band, pw_bf, pool_scale, wr_hi, wr_lo, tri):
    b, s, _ = x.shape
    tm = TM_POOL
    ni = s // tm
    hb = tm // POOL_HALO
    row = pl.BlockSpec((1, tm, D), lambda bi, i: (bi, i, 0))
    const2 = lambda shp: pl.BlockSpec(shp, lambda bi, i: (0,) * len(shp))
    return pl.pallas_call(
        _pool_route_kernel,
        grid=(b, ni),
        in_specs=[
            row,
            pl.BlockSpec((1, POOL_HALO, D), lambda bi, i: (bi, jnp.maximum(i * hb - 1, 0), 0)),
            pl.BlockSpec((1, POOL_HALO, D), lambda bi, i: (bi, jnp.minimum((i + 1) * hb, s // POOL_HALO - 1), 0)),
            pl.BlockSpec((1, 1, 6, D), lambda bi, i: (1, bi, 0, 0)),
            const2(band.shape), const2(pw_bf.shape), const2((1, D)),
            const2(wr_hi.shape), const2(wr_lo.shape), const2(tri.shape),
        ],
        out_specs=[row,
                   pl.BlockSpec((tm * SLAB, LANES), lambda bi, i: (bi * ni + i, 0)),
                   pl.BlockSpec((8, tm), lambda bi, i: (0, bi * ni + i)),
                   pl.BlockSpec((1, LANES), lambda bi, i: (0, 0))],
        out_shape=[jax.ShapeDtypeStruct((b, s, D), F32), jax.ShapeDtypeStruct((b * s * SLAB, LANES), F32),
                   jax.ShapeDtypeStruct((8, b * s), F32), jax.ShapeDtypeStruct((1, LANES), F32)],
        scratch_shapes=[pltpu.VMEM((tm + 2 * POOL_HALO, D), BF16), pltpu.VMEM((1, LANES), F32)],
        compiler_params=_cparams(("arbitrary", "arbitrary"), VMEM_LIMIT),
        name="pool_route",
    )(x, x, x, mod, band, pw_bf, pool_scale.reshape(1, D), wr_hi, wr_lo, tri)


def _slotmap_kernel(pos_ref, lo_ref, hi_ref, o_ref):
    n_pairs = pos_ref.shape[0]
    spare_mask = 2 * TM_MOE - 1
    for e in range(lo_ref.shape[0]):
        def fill(p, c):
            o_ref[p] = n_pairs + (p & spare_mask)
            return c
        lax.fori_loop(lo_ref[e], hi_ref[e], fill, 0)

    def place(f, c):
        o_ref[pos_ref[f]] = f
        return c
    lax.fori_loop(0, n_pairs, place, 0, unroll=8)


def _slot_map(pos_flat, lo, hi, n_slots):
    smem = pl.BlockSpec(memory_space=pltpu.SMEM)
    return pl.pallas_call(
        _slotmap_kernel,
        in_specs=[smem, smem, smem],
        out_specs=smem,
        out_shape=jax.ShapeDtypeStruct((n_slots,), jnp.int32),
        name="moe_slot_map",
    )(pos_flat, lo, hi)


def _moe_kernel(te_ref, nused_ref, fnext_ref, fprev_ref, f0_ref, h_hbm, wg_hbm, wu_hbm, wd_hbm,
                y_hbm, wg_res, wu_res, wd_res, stg_in, stg_out, xbuf, xb, act, acc, stage, gsem, ssem, wsem):
    i = pl.program_id(0)
    nt = pl.num_programs(0)
    used_tiles = nused_ref[0]
    tm = xb.shape[0]
    f_dim = wg_res.shape[1]
    tok_mask = h_hbm.shape[0] // SLAB - 1
    tile_rows = tm * SLAB

    def slab(ix):
        return pl.ds(pl.multiple_of(ix * SLAB, SLAB), SLAB)

    def gather_row(fref, r, slot, zero=0):
        tok = fref[0, 0, r + zero] & tok_mask
        return pltpu.make_async_copy(h_hbm.at[slab(tok)], xbuf.at[slot, slab(r)], gsem.at[slot])

    def scatter_row(fref, r, slot, zero=0):
        return pltpu.make_async_copy(stage.at[slot, slab(r)], y_hbm.at[slab(fref[0, 0, r + zero])],
                                     ssem.at[slot])

    def gather_all(slot):
        return pltpu.make_async_copy(h_hbm.at[pl.ds(0, tile_rows)], xbuf.at[slot], gsem.at[slot])

    def scatter_all(slot):
        return pltpu.make_async_copy(stage.at[slot], y_hbm.at[pl.ds(0, tile_rows)], ssem.at[slot])

    cur = i % 2
    used = i < used_tiles

    @pl.when(i == 0)
    def _():
        stage[...] = jnp.zeros_like(stage)
        spare = y_hbm.shape[0] - 2 * tile_rows
        fills = [pltpu.make_async_copy(stage.at[sl], y_hbm.at[pl.ds(spare + sl * tile_rows, tile_rows)],
                                       ssem.at[sl])
                 for sl in range(2)]
        for cp in fills:
            cp.start()
        for cp in fills:
            cp.wait()

    @pl.when((i == 0) & used)
    def _():
        def prime(r, c):
            gather_row(f0_ref, r, 0).start()
            return c
        lax.fori_loop(0, tm, prime, 0)

    expert = te_ref[i]
    new_expert = used & ((i == 0) | (expert != te_ref[jnp.maximum(i - 1, 0)]))

    @pl.when(new_expert)
    def _():
        plan = [(w, res, stg_in, 0, W_ROWS_IN, c)
                for w, res in ((wg_hbm, wg_res), (wu_hbm, wu_res)) for c in range(D // W_ROWS_IN)]
        plan += [(wd_hbm, wd_res, stg_out, W_RING, W_ROWS_OUT, c) for c in range(f_dim // W_ROWS_OUT)]

        def chunk_copy(k):
            w_hbm, _, stg, sem0, rows, c = plan[k]
            buf = k % W_RING
            return pltpu.make_async_copy(w_hbm.at[expert, pl.ds(c * rows, rows), :], stg.at[buf],
                                         wsem.at[sem0 + buf])

        for k in range(W_RING):
            chunk_copy(k).start()
        for k in range(len(plan)):
            _, res, stg, _, rows, c = plan[k]
            chunk_copy(k).wait()
            res[c * rows:(c + 1) * rows, :] = stg[k % W_RING].astype(BF16)
            if k + W_RING < len(plan):
                chunk_copy(k + W_RING).start()

    @pl.when(used)
    def _():
        gather_all(cur).wait()
        xb[...] = _slabs_to_rows(xbuf.at[cur], tm).astype(BF16)

    @pl.when((i >= 2) & (i - 2 < used_tiles))
    def _():
        scatter_all(cur).wait()

    half = f_dim // MOE_SPLIT
    n_piece = half // W_CHUNK
    n_groups = MOE_SPLIT * (n_piece + 1)

    def tick(v):
        bits = jnp.max(lax.bitcast_convert_type(v[0:SLAB, 0:LANES], jnp.int32))
        return lax.shift_right_logical(lax.shift_right_logical(bits, 16), 16)

    def expert_ffn(issue_group):
        xv = xb[...]
        out = None
        issue_group(0, 0)
        k = 1
        for hf in range(MOE_SPLIT):
            for c in range(n_piece):
                cs = slice(hf * half + c * W_CHUNK, hf * half + (c + 1) * W_CHUNK)
                g = jnp.dot(xv, wg_res[:, cs], preferred_element_type=F32)
                up = jnp.dot(xv, wu_res[:, cs], preferred_element_type=F32)
                act[:, c * W_CHUNK:(c + 1) * W_CHUNK] = (g * _sigmoid(g) * up).astype(BF16)
                issue_group(k, tick(g))
                k += 1
            part = jnp.dot(act[...], wd_res[hf * half:(hf + 1) * half, :], preferred_element_type=F32)
            if hf < MOE_SPLIT - 1:
                acc[...] = part if out is None else acc[...] + part
                out = acc
                issue_group(k, tick(part))
                k += 1
            else:
                _rows_to_slabs(part if out is None else acc[...] + part, stage.at[cur])

    has_next = i + 1 < used_tiles
    has_prev = (i >= 1) & (i - 1 < used_tiles)
    steady = (i >= 2) & has_next

    @pl.when(steady)
    def _():
        def issue_group(k, zero):
            for r in range(k * tm // n_groups, (k + 1) * tm // n_groups):
                gather_row(fnext_ref, r, 1 - cur, zero).start()
                scatter_row(fprev_ref, r, 1 - cur, zero).start()
        expert_ffn(issue_group)

    @pl.when(jnp.logical_not(steady))
    def _():
        @pl.when(used)
        def _():
            expert_ffn(lambda k, zero: None)

        @pl.when(has_next)
        def _():
            def issue(r, c):
                gather_row(fnext_ref, r, 1 - cur).start()
                return c
            lax.fori_loop(0, tm, issue, 0)

        @pl.when(has_prev)
        def _():
            def issue(r, c):
                scatter_row(fprev_ref, r, 1 - cur).start()
                return c
            lax.fori_loop(0, tm, issue, 0)

    @pl.when((i == nt - 1) & (nt - 2 < used_tiles))
    def _():
        scatter_all(1 - cur).wait()


def _moe_experts(h_slabs, fmap, tile_expert, n_used, wg, wu, wd, n_tiles, y_rows):
    f = wg.shape[2]
    tm = TM_MOE
    fblk = lambda imap: pl.BlockSpec((1, 1, tm), imap, memory_space=pltpu.SMEM)
    hbm = pl.BlockSpec(memory_space=pl.ANY)
    return pl.pallas_call(
        _moe_kernel,
        grid_spec=pltpu.PrefetchScalarGridSpec(
            num_scalar_prefetch=2,
            grid=(n_tiles,),
            in_specs=[
                fblk(lambda i, te, nu: (jnp.minimum(i + 1, n_tiles - 1), 0, 0)),
                fblk(lambda i, te, nu: (jnp.maximum(i - 1, 0), 0, 0)),
                fblk(lambda i, te, nu: (0, 0, 0)),
                hbm, hbm, hbm, hbm,
            ],
            out_specs=hbm,
            scratch_shapes=[pltpu.VMEM((D, f), BF16), pltpu.VMEM((D, f), BF16), pltpu.VMEM((f, D), BF16),
                            pltpu.VMEM((W_RING, W_ROWS_IN, f), F32), pltpu.VMEM((W_RING, W_ROWS_OUT, D), F32),
                            pltpu.VMEM((2, tm * SLAB, LANES), F32), pltpu.VMEM((tm, D), BF16),
                            pltpu.VMEM((tm, f // MOE_SPLIT), BF16), pltpu.VMEM((tm, D), F32),
                            pltpu.VMEM((2, tm * SLAB, LANES), F32),
                            pltpu.SemaphoreType.DMA((2,)), pltpu.SemaphoreType.DMA((2,)),
                            pltpu.SemaphoreType.DMA((2 * W_RING,))],
        ),
        out_shape=jax.ShapeDtypeStruct((y_rows * SLAB, LANES), F32),
        compiler_params=_cparams(("arbitrary",), MOE_VMEM_LIMIT),
        name="moe_experts",
    )(tile_expert, n_used, fmap, fmap, fmap, h_slabs, wg, wu, wd)


def _combine_kernel(y1_ref, y2_ref, x_ref, w_ref, mod_ref, gain_ref, o_ref):
    w = w_ref[...]
    rows = x_ref.shape[0]
    moe = w[:, 0:1] * _slabs_to_rows(y1_ref, rows) + w[:, 1:2] * _slabs_to_rows(y2_ref, rows)
    mod = mod_ref[0, 0]
    x4 = x_ref[...] + mod[5:6] * moe
    ms = jnp.mean(x4 * x4, axis=-1, keepdims=True)
    o_ref[...] = x4 * lax.rsqrt(ms + EPS) * gain_ref[...]


def _combine(y, x3_2d, wts, mod, final_gain, seq):
    n = x3_2d.shape[0]
    tc = TC_COMB
    nt = n // tc
    per_b = seq // tc
    return pl.pallas_call(
        _combine_kernel,
        grid=(nt,),
        in_specs=[
            pl.BlockSpec((tc * SLAB, LANES), lambda i: (i, 0)),
            pl.BlockSpec((tc * SLAB, LANES), lambda i: (i + nt, 0)),
            pl.BlockSpec((tc, D), lambda i: (i, 0)),
            pl.BlockSpec((tc, 2), lambda i: (i, 0)),
            pl.BlockSpec((1, 1, 6, D), lambda i: (1, i // per_b, 0, 0)),
            pl.BlockSpec((1, D), lambda i: (0, 0)),
        ],
        out_specs=pl.BlockSpec((tc, D), lambda i: (i, 0)),
        out_shape=jax.ShapeDtypeStruct((n, D), F32),
        compiler_params=_cparams(("arbitrary",)),
        name="moe_combine",
    )(y, y, x3_2d, wts, mod, final_gain.reshape(1, D))


def _rope_tables(seq):
    rows = seq // GRID_W
    row_pos = jnp.repeat(jnp.arange(rows, dtype=F32), GRID_W)
    col_pos = jnp.tile(jnp.arange(GRID_W, dtype=F32), rows)
    axis_dim = HEAD_DIM // 2
    inv_freq = ROPE_BASE ** (-jnp.arange(0, axis_dim, 2, dtype=F32) / axis_dim)
    ar = row_pos[:, None] * inv_freq
    ac = col_pos[:, None] * inv_freq
    cos64 = jnp.concatenate([jnp.cos(ar), jnp.cos(ar), jnp.cos(ac), jnp.cos(ac)], axis=1)
    sin64 = jnp.concatenate([-jnp.sin(ar), jnp.sin(ar), -jnp.sin(ac), jnp.sin(ac)], axis=1)
    return jnp.tile(cos64, (1, 2)), jnp.tile(sin64, (1, 2))


def _band_matrices():
    r = np.arange(BLOCK)[:, None]
    c = np.arange(BLOCK + 2 * POOL_HALO)[None, :] - POOL_HALO
    mats = []
    for w in POOL_SIZES:
        lo = -(w // 2)
        hi = w - 1 - w // 2
        mats.append(((c >= r + lo) & (c <= r + hi)).astype(np.float32))
    return jnp.asarray(np.stack(mats), dtype=BF16)


def kernel(x, c, ctx, c_ctx, w_ada, b_ada, w_in, attn_sink, gm_gain, gm_w_s, gm_b_s, w_out,
           ffn_w_gate, ffn_w_up, ffn_w_down, pool_w, pool_scale, router_w,
           moe_w_gate, moe_w_up, moe_w_down, final_gain):
    b, s, _ = x.shape
    n = b * s
    assert w_ada.shape[0] == 2 and w_in.shape[0] == 1 and pool_w.shape[0] == 1
    assert s % TM_IN == 0 and s % TM_POOL == 0 and s % TM_FFN == 0 and b <= 4
    assert n & (n - 1) == 0

    cvec = jnp.concatenate([c, c_ctx[None, :], jnp.zeros((8 - b - 1, D), F32)], axis=0)
    mod = _ada_mod(cvec, w_ada, b_ada)

    cos_t, sin_t = _rope_tables(s)
    w_in_bf = w_in[0].astype(BF16)
    q, k, ksw, v, vsw, u, vg = _in_proj(x, mod, w_in_bf, gm_gain[0], cos_t, sin_t)
    kx, ksx, vx, vsx = _ctx_kv(ctx, mod, w_in_bf)
    wcat = gm_w_s[0].reshape(4, 2, BLOCK, BLOCK).transpose(0, 2, 1, 3).reshape(4, BLOCK, 2 * BLOCK).astype(BF16)
    x1 = _attn_mixer(x, mod, attn_sink[0], q, k, ksw, v, vsw, kx, ksx, vx, vsx, u, vg,
                     wcat, gm_b_s[0].T, w_out[0].astype(BF16))
    x2 = _dense_ffn(x1.reshape(n, D), mod, ffn_w_gate[0].astype(BF16), ffn_w_up[0].astype(BF16),
                    ffn_w_down[0].astype(BF16), s)

    wr = jnp.pad(router_w[0], ((0, 0), (0, LANES - N_EXPERTS)))
    wr_hi = wr.astype(BF16)
    wr_lo = (wr - wr_hi.astype(F32)).astype(BF16)
    tri = jnp.asarray(np.tril(np.ones((TM_POOL, TM_POOL), np.float32), -1), dtype=BF16)
    x3, h2, route, counts = _pool_route(x2.reshape(b, s, D), mod, _band_matrices(), pool_w[0].astype(BF16),
                                        pool_scale[0], wr_hi, wr_lo, tri)

    tm = TM_MOE
    n_tiles = (2 * n) // tm + N_EXPERTS
    cnt = counts[0, :N_EXPERTS].astype(jnp.int32)
    tiles_e = (cnt + tm - 1) // tm
    tile_end = jnp.cumsum(tiles_e)
    off = (tile_end - tiles_e) * tm
    n_used = tile_end[-1]
    tix = jnp.arange(n_tiles, dtype=jnp.int32)
    te = jnp.minimum(jnp.sum(tix[:, None] >= tile_end[None, :], axis=1), N_EXPERTS - 1).astype(jnp.int32)
    te_last = te[jnp.maximum(n_used - 1, 0)]
    tile_expert = jnp.where(tix < n_used, te, te_last)
    e1 = route[0].astype(jnp.int32)
    e2 = route[1].astype(jnp.int32)
    pos1 = off[e1] + route[4].astype(jnp.int32)
    pos2 = off[e2] + route[5].astype(jnp.int32)
    n_slots = n_tiles * tm
    pad_lo = jnp.concatenate([off + cnt, (n_used * tm).reshape(1)]).astype(jnp.int32)
    pad_hi = jnp.concatenate([off + tiles_e * tm, jnp.full((1,), n_slots, jnp.int32)]).astype(jnp.int32)
    fmap = _slot_map(jnp.concatenate([pos1, pos2]), pad_lo, pad_hi, n_slots)
    n_used_arr = n_used.reshape(1).astype(jnp.int32)

    y = _moe_experts(h2, fmap.reshape(n_tiles, 1, tm), tile_expert, n_used_arr,
                     moe_w_gate[0], moe_w_up[0], moe_w_down[0], n_tiles, 2 * n + 2 * tm)
    out = _combine(y, x3.reshape(n, D), route[2:4].T, mod, final_gain, s)
    return out.reshape(b, s, D)
```

```python
import functools

import numpy as np
import jax
import jax.numpy as jnp
from jax import lax
from jax.experimental import pallas as pl
from jax.experimental.pallas import tpu as pltpu

F32 = jnp.float32
BF16 = jnp.bfloat16

D = 1024
GRID_W = 64
EPS = 1e-6
NEG_INF = -1e30
HEAD_DIM = 64
N_Q_HEADS = 8
BLOCK = 128
ATT_W = 512
KV_W = 128
GM_W = 512
IN_W = 1792
POOL_SIZES = (2, 4, 8, 16)
POOL_GD = 256
POOL_HALO = 16
N_EXPERTS = 8
ROPE_BASE = 10000.0
LANES = 128
SLAB = D // LANES
SQRT_2_OVER_PI = 0.7978845608028654

TM_IN = 512
TM_FFN = 512
TM_POOL = 512
TM_MOE = 512
MOE_SPLIT = 2
W_CHUNK = 256
W_RING = 4
MOE_VMEM_LIMIT = 60 * 1024 * 1024
TC_COMB = 512
Q_BLOCKS = 4
VMEM_LIMIT = 56 * 1024 * 1024


def _cparams(sem, vmem=None):
    return pltpu.CompilerParams(dimension_semantics=sem, vmem_limit_bytes=vmem)


def _modulate(xf, shift, scale):
    ms = jnp.mean(xf * xf, axis=-1, keepdims=True)
    return xf * lax.rsqrt(ms + EPS) * (1.0 + scale) + shift


def _sigmoid(z):
    return 1.0 / (1.0 + jnp.exp(-z))


def _rows_to_slabs(val, slab_ref):
    rows = val.shape[0]
    for cix in range(SLAB):
        slab_ref[pl.ds(cix, rows, stride=SLAB), :] = val[:, cix * LANES:(cix + 1) * LANES]


def _slabs_to_rows(slab_ref, rows):
    return jnp.concatenate([slab_ref[pl.ds(cix, rows, stride=SLAB), :] for cix in range(SLAB)], axis=1)


def _ada_kernel(c_ref, w_ref, b_ref, o_ref):
    c = c_ref[...]
    s = c * _sigmoid(c)
    o_ref[0] = jnp.dot(s, w_ref[0], precision=lax.Precision.HIGHEST,
                       preferred_element_type=F32) + b_ref[0]


def _ada_mod(cvec, w_ada, b_ada):
    depth, _, n6 = w_ada.shape
    tn = 1536
    out = pl.pallas_call(
        _ada_kernel,
        grid=(depth, n6 // tn),
        in_specs=[
            pl.BlockSpec((8, D), lambda l, j: (0, 0)),
            pl.BlockSpec((1, D, tn), lambda l, j: (l, 0, j)),
            pl.BlockSpec((1, 1, tn), lambda l, j: (l, 0, j)),
        ],
        out_specs=pl.BlockSpec((1, 8, tn), lambda l, j: (l, 0, j)),
        out_shape=jax.ShapeDtypeStruct((depth, 8, n6), F32),
        compiler_params=_cparams(("arbitrary", "arbitrary")),
        name="ada_mod",
    )(cvec, w_ada, b_ada.reshape(depth, 1, n6))
    return out.reshape(depth, 8, 6, D)


def _rope(t, cs, sn, first_half):
    fwd = pltpu.roll(t, LANES - 16, axis=1)
    bwd = pltpu.roll(t, 16, axis=1)
    return t * cs + jnp.where(first_half, fwd, bwd) * sn


def _inproj_kernel(x_ref, mod_ref, w_ref, gain_ref, cos_ref, sin_ref,
                   q_ref, k_ref, ksw_ref, v_ref, vsw_ref, u_ref, vg_ref):
    mod = mod_ref[0, 0]
    h = _modulate(x_ref[0], mod[0:1], mod[1:2]).astype(BF16)
    proj = jnp.dot(h, w_ref[...], preferred_element_type=F32)
    cs = cos_ref[...]
    sn = sin_ref[...]
    lane = lax.broadcasted_iota(jnp.int32, cs.shape, 1)
    first_half = (lane & 16) == 0
    for cix in range(ATT_W // LANES):
        t = proj[:, cix * LANES:(cix + 1) * LANES]
        q_ref[0, :, cix * LANES:(cix + 1) * LANES] = (
            _rope(t, cs, sn, first_half) * (HEAD_DIM ** -0.5)).astype(BF16)
    kr = _rope(proj[:, ATT_W:ATT_W + KV_W], cs, sn, first_half)
    k_ref[0] = kr.astype(BF16)
    ksw_ref[0] = pltpu.roll(kr, HEAD_DIM, axis=1).astype(BF16)
    vv = proj[:, ATT_W + KV_W:ATT_W + 2 * KV_W]
    v_ref[0] = vv.astype(BF16)
    vsw_ref[0] = pltpu.roll(vv, HEAD_DIM, axis=1).astype(BF16)
    z = proj[:, ATT_W + 2 * KV_W:]
    g = z * (0.5 * (1.0 + jnp.tanh(SQRT_2_OVER_PI * (z + 0.044715 * (z * z * z)))))
    u_ref[0] = g[:, :GM_W].astype(BF16)
    vg = g[:, GM_W:]
    ms = jnp.mean(vg * vg, axis=-1, keepdims=True)
    vg_ref[0] = (vg * lax.rsqrt(ms + EPS) * gain_ref[...]).astype(BF16)


def _in_proj(x, mod, w_in_bf, gm_gain, cos_t, sin_t):
    b, s, _ = x.shape
    tm = TM_IN
    row = lambda w: pl.BlockSpec((1, tm, w), lambda bi, i: (bi, i, 0))
    outs = pl.pallas_call(
        _inproj_kernel,
        grid=(b, s // tm),
        in_specs=[
            row(D),
            pl.BlockSpec((1, 1, 6, D), lambda bi, i: (0, bi, 0, 0)),
            pl.BlockSpec((D, IN_W), lambda bi, i: (0, 0)),
            pl.BlockSpec((1, GM_W), lambda bi, i: (0, 0)),
            pl.BlockSpec((tm, LANES), lambda bi, i: (i, 0)),
            pl.BlockSpec((tm, LANES), lambda bi, i: (i, 0)),
        ],
        out_specs=[row(ATT_W), row(KV_W), row(KV_W), row(KV_W), row(KV_W), row(GM_W), row(GM_W)],
        out_shape=[jax.ShapeDtypeStruct((b, s, w), BF16)
                   for w in (ATT_W, KV_W, KV_W, KV_W, KV_W, GM_W, GM_W)],
        compiler_params=_cparams(("arbitrary", "arbitrary"), VMEM_LIMIT),
        name="in_proj",
    )(x, mod, w_in_bf, gm_gain.reshape(1, GM_W), cos_t, sin_t)
    return outs


def _ctx_kernel(c_ref, mod_ref, w_ref, k_ref, ksw_ref, v_ref, vsw_ref):
    mod = mod_ref[0, 0]
    h = _modulate(c_ref[0], mod[0:1], mod[1:2]).astype(BF16)
    kv = jnp.dot(h, w_ref[...], preferred_element_type=F32)
    kk = kv[:, :KV_W]
    vv = kv[:, KV_W:]
    k_ref[0] = kk.astype(BF16)
    ksw_ref[0] = pltpu.roll(kk, HEAD_DIM, axis=1).astype(BF16)
    v_ref[0] = vv.astype(BF16)
    vsw_ref[0] = pltpu.roll(vv, HEAD_DIM, axis=1).astype(BF16)


def _ctx_kv(ctx, mod, w_in_bf):
    b, l, _ = ctx.shape
    spec = pl.BlockSpec((1, l, KV_W), lambda bi: (bi, 0, 0))
    return pl.pallas_call(
        _ctx_kernel,
        grid=(b,),
        in_specs=[
            pl.BlockSpec((1, l, D), lambda bi: (bi, 0, 0)),
            pl.BlockSpec((1, 1, 6, D), lambda bi: (0, b, 0, 0)),
            pl.BlockSpec((D, 2 * KV_W), lambda bi: (0, ATT_W // (2 * KV_W))),
        ],
        out_specs=[spec] * 4,
        out_shape=[jax.ShapeDtypeStruct((b, l, KV_W), BF16)] * 4,
        compiler_params=_cparams(("arbitrary",)),
        name="ctx_kv",
    )(ctx, mod, w_in_bf)


def _attn_kernel(sink_ref, q_ref, kp_ref, kc_ref, kn_ref, ksp_ref, ksc_ref, ksn_ref,
                 vp_ref, vc_ref, vn_ref, vsp_ref, vsc_ref, vsn_ref,
                 kx_ref, ksx_ref, vx_ref, vsx_ref,
                 u_ref, vg_ref, wcat_ref, bs_ref, wout_ref, x_ref, mod_ref, o_ref):
    n = pl.program_id(1)
    nblk = pl.num_programs(1) * Q_BLOCKS
    lane = lax.broadcasted_iota(jnp.int32, (1, LANES), 1)
    low = lane < HEAD_DIM
    zero = jnp.zeros((), BF16)

    def variants(a0, a1):
        return ((jnp.where(low, a0, zero), jnp.where(low, zero, a1)),
                (jnp.where(low, a1, zero), jnp.where(low, zero, a0)))

    cat = lambda refs: jnp.concatenate([r[0] for r in refs], axis=0)
    kb_var = variants(cat((kp_ref, kc_ref, kn_ref)), cat((ksp_ref, ksc_ref, ksn_ref)))
    vb_var = variants(cat((vp_ref, vc_ref, vn_ref)), cat((vsp_ref, vsc_ref, vsn_ref)))
    kx_var = variants(kx_ref[0], ksx_ref[0])
    vx_var = variants(vx_ref[0], vsx_ref[0])

    row = lax.broadcasted_iota(jnp.int32, (2 * BLOCK, BLOCK), 0) & (BLOCK - 1)
    col = lax.broadcasted_iota(jnp.int32, (2 * BLOCK, BLOCK), 1)
    top = lax.broadcasted_iota(jnp.int32, (2 * BLOCK, 1), 0) < BLOCK
    nt_dims = (((1,), (1,)), ((), ()))

    q = q_ref[0]
    att_blocks = [[None] * 4 for _ in range(Q_BLOCKS)]
    for kvh in range(2):
        qst = jnp.concatenate(
            [q[qb * BLOCK:(qb + 1) * BLOCK, pr * LANES:(pr + 1) * LANES]
             for qb in range(Q_BLOCKS) for pr in (2 * kvh, 2 * kvh + 1)], axis=0)
        accs = [None] * Q_BLOCKS
        for half in range(2):
            sk = jnp.where(top, sink_ref[4 * kvh + half], sink_ref[4 * kvh + 2 + half])
            s_ctx = lax.dot_general(qst, kx_var[kvh][half], nt_dims, preferred_element_type=F32)
            for qb in range(Q_BLOCKS):
                g = n * Q_BLOCKS + qb
                qrows = qst[qb * 2 * BLOCK:(qb + 1) * 2 * BLOCK]
                sb = lax.dot_general(qrows, kb_var[kvh][half][qb * BLOCK:(qb + 3) * BLOCK], nt_dims,
                                     preferred_element_type=F32)
                s0 = jnp.where((col >= row) & (g > 0), sb[:, :BLOCK], NEG_INF)
                s1 = sb[:, BLOCK:2 * BLOCK]
                s2 = jnp.where((col <= row) & (g < nblk - 1), sb[:, 2 * BLOCK:], NEG_INF)
                sc = s_ctx[qb * 2 * BLOCK:(qb + 1) * 2 * BLOCK]
                m = jnp.maximum(jnp.maximum(s0, s1), s2)
                m = jnp.maximum(jnp.max(m, axis=-1, keepdims=True), jnp.max(sc, axis=-1, keepdims=True))
                m = jnp.maximum(m, sk)
                p0, p1, p2, pc = (jnp.exp(t - m) for t in (s0, s1, s2, sc))
                den = (jnp.sum(p0 + p1 + p2, axis=-1, keepdims=True) + jnp.sum(pc, axis=-1, keepdims=True)
                       + jnp.exp(sk - m))
                pb = jnp.concatenate([p0, p1, p2], axis=1).astype(BF16)
                o = (jnp.dot(pb, vb_var[kvh][half][qb * BLOCK:(qb + 3) * BLOCK], preferred_element_type=F32)
                     + jnp.dot(pc.astype(BF16), vx_var[kvh][half], preferred_element_type=F32))
                o = o / den
                accs[qb] = o if accs[qb] is None else accs[qb] + o
        for qb in range(Q_BLOCKS):
            att_blocks[qb][2 * kvh] = accs[qb][:BLOCK]
            att_blocks[qb][2 * kvh + 1] = accs[qb][BLOCK:]

    u = u_ref[0]
    vg = vg_ref[0]
    bs = bs_ref[...]
    gm_blocks = [[None] * 4 for _ in range(Q_BLOCKS)]
    for j in range(GM_W // LANES):
        chunks = [vg[c * BLOCK:(c + 1) * BLOCK, j * LANES:(j + 1) * LANES] for c in range(Q_BLOCKS)]
        rhs = jnp.concatenate(
            [jnp.concatenate([jnp.where(low, v, zero) for v in chunks], axis=1),
             jnp.concatenate([jnp.where(low, zero, v) for v in chunks], axis=1)], axis=0)
        mixed = jnp.dot(wcat_ref[j], rhs, preferred_element_type=F32)
        bias = jnp.where(low, bs[:, 2 * j:2 * j + 1], bs[:, 2 * j + 1:2 * j + 2])
        for c in range(Q_BLOCKS):
            gm_blocks[c][j] = (u[c * BLOCK:(c + 1) * BLOCK, j * LANES:(j + 1) * LANES].astype(F32)
                               * (mixed[:, c * LANES:(c + 1) * LANES] + bias))

    mix = jnp.concatenate([jnp.concatenate(att_blocks[c] + gm_blocks[c], axis=1) for c in range(Q_BLOCKS)],
                          axis=0).astype(BF16)
    y = jnp.dot(mix, wout_ref[...], preferred_element_type=F32)
    mod = mod_ref[0, 0]
    o_ref[0] = x_ref[0] + mod[2:3] * y


def _attn_mixer(x, mod, sink, q, k, ksw, v, vsw, kx, ksx, vx, vsx, u, vg, wcat_bf, bs_t, wout_bf):
    b, s, _ = x.shape
    tq = Q_BLOCKS * BLOCK
    nb = s // BLOCK
    l = kx.shape[1]
    cur = lambda w: pl.BlockSpec((1, tq, w), lambda bi, n: (bi, n, 0))
    prv = lambda w: pl.BlockSpec((1, BLOCK, w), lambda bi, n: (bi, jnp.maximum(n * Q_BLOCKS - 1, 0), 0))
    nxt = lambda w: pl.BlockSpec((1, BLOCK, w), lambda bi, n: (bi, jnp.minimum((n + 1) * Q_BLOCKS, nb - 1), 0))
    cx = pl.BlockSpec((1, l, KV_W), lambda bi, n: (bi, 0, 0))
    return pl.pallas_call(
        _attn_kernel,
        grid=(b, s // tq),
        in_specs=[
            pl.BlockSpec(memory_space=pltpu.SMEM),
            cur(ATT_W),
            prv(KV_W), cur(KV_W), nxt(KV_W), prv(KV_W), cur(KV_W), nxt(KV_W),
            prv(KV_W), cur(KV_W), nxt(KV_W), prv(KV_W), cur(KV_W), nxt(KV_W),
            cx, cx, cx, cx,
            cur(GM_W), cur(GM_W),
            pl.BlockSpec((4, BLOCK, 2 * BLOCK), lambda bi, n: (0, 0, 0)),
            pl.BlockSpec((BLOCK, 8), lambda bi, n: (0, 0)),
            pl.BlockSpec((D, D), lambda bi, n: (0, 0)),
            cur(D),
            pl.BlockSpec((1, 1, 6, D), lambda bi, n: (0, bi, 0, 0)),
        ],
        out_specs=cur(D),
        out_shape=jax.ShapeDtypeStruct((b, s, D), F32),
        compiler_params=_cparams(("arbitrary", "arbitrary"), VMEM_LIMIT),
        name="attn_gmlp_out",
    )(sink, q, k, k, k, ksw, ksw, ksw, v, v, v, vsw, vsw, vsw, kx, ksx, vx, vsx,
      u, vg, wcat_bf, bs_t, wout_bf, x, mod)


def _ffn_kernel(x_ref, mod_ref, wg_ref, wu_ref, wd_ref, o_ref):
    mod = mod_ref[0, 0]
    xf = x_ref[...]
    h = _modulate(xf, mod[3:4], mod[4:5]).astype(BF16)
    g = jnp.dot(h, wg_ref[...], preferred_element_type=F32)
    up = jnp.dot(h, wu_ref[...], preferred_element_type=F32)
    a = (g * _sigmoid(g) * up).astype(BF16)
    o_ref[...] = xf + mod[5:6] * jnp.dot(a, wd_ref[...], preferred_element_type=F32)


def _dense_ffn(x2d, mod, wg, wu, wd, seq):
    n = x2d.shape[0]
    f = wg.shape[1]
    tm = TM_FFN
    per_b = seq // tm
    resident = lambda shp: pl.BlockSpec(shp, lambda i: (0, 0), pipeline_mode=pl.Buffered(1))
    return pl.pallas_call(
        _ffn_kernel,
        grid=(n // tm,),
        in_specs=[
            pl.BlockSpec((tm, D), lambda i: (i, 0)),
            pl.BlockSpec((1, 1, 6, D), lambda i: (0, i // per_b, 0, 0)),
            resident((D, f)), resident((D, f)), resident((f, D)),
        ],
        out_specs=pl.BlockSpec((tm, D), lambda i: (i, 0)),
        out_shape=jax.ShapeDtypeStruct((n, D), F32),
        compiler_params=_cparams(("arbitrary",), VMEM_LIMIT),
        name="dense_ffn",
    )(x2d, mod, wg, wu, wd)


def _pool_route_kernel(x_ref, xp_ref, xn_ref, mod_ref, band_ref, pw_ref, psc_ref, wr_hi_ref, wr_lo_ref,
                       tri_ref, x3_ref, h2_ref, route_ref, cnt_ref, hext, carry):
    bi = pl.program_id(0)
    i = pl.program_id(1)
    ni = pl.num_programs(1)
    tm = x_ref.shape[1]
    seq = tm * ni
    mod = mod_ref[0, 0]

    @pl.when((bi == 0) & (i == 0))
    def _():
        carry[...] = jnp.zeros_like(carry)

    xf = x_ref[0]
    hp = _modulate(xp_ref[0], mod[0:1], mod[1:2])
    hn = _modulate(xn_ref[0], mod[0:1], mod[1:2])
    hext[0:POOL_HALO] = jnp.where(i > 0, hp, 0.0).astype(BF16)
    h_main = _modulate(xf, mod[0:1], mod[1:2])
    hext[POOL_HALO:POOL_HALO + tm] = h_main.astype(BF16)
    hext[POOL_HALO + tm:] = jnp.where(i < ni - 1, hn, 0.0).astype(BF16)

    t_local = lax.broadcasted_iota(jnp.int32, (BLOCK, 1), 0)
    ys = []
    for gi, w in enumerate(POOL_SIZES):
        lo_off = -(w // 2)
        hi_off = w - 1 - w // 2
        cols = slice(gi * POOL_GD, (gi + 1) * POOL_GD)
        outs = []
        for sb in range(tm // BLOCK):
            r0 = sb * BLOCK
            win = jnp.dot(band_ref[gi], hext[r0:r0 + BLOCK + 2 * POOL_HALO, cols],
                          preferred_element_type=F32)
            t = i * tm + r0 + t_local
            cnt = (jnp.minimum(t + hi_off, seq - 1) - jnp.maximum(t + lo_off, 0) + 1).astype(F32)
            diff = win / cnt - h_main[r0:r0 + BLOCK, cols]
            outs.append(diff.astype(BF16))
        dg = jnp.concatenate(outs, axis=0)
        ys.append(jnp.dot(dg, pw_ref[gi], preferred_element_type=F32))
    y = jnp.concatenate(ys, axis=1) * psc_ref[...]
    x3 = xf + mod[2:3] * y
    x3_ref[0] = x3

    h2 = _modulate(x3, mod[3:4], mod[4:5])
    _rows_to_slabs(h2, h2_ref)
    h_hi = h2.astype(BF16)
    h_lo = (h2 - h_hi.astype(F32)).astype(BF16)
    logits = (jnp.dot(h_hi, wr_hi_ref[...], preferred_element_type=F32)
              + jnp.dot(h_hi, wr_lo_ref[...], preferred_element_type=F32)
              + jnp.dot(h_lo, wr_hi_ref[...], preferred_element_type=F32))
    lane = lax.broadcasted_iota(jnp.int32, (tm, LANES), 1)
    lane_f = lane.astype(F32)
    neg = -jnp.inf
    lg = jnp.where(lane < N_EXPERTS, logits, neg)
    m1 = jnp.max(lg, axis=-1, keepdims=True)
    i1 = jnp.min(jnp.where(lg == m1, lane_f, float(LANES)), axis=-1, keepdims=True)
    oh1 = lane_f == i1
    lg2 = jnp.where(oh1, neg, lg)
    m2 = jnp.max(lg2, axis=-1, keepdims=True)
    i2 = jnp.min(jnp.where(lg2 == m2, lane_f, float(LANES)), axis=-1, keepdims=True)
    oh2 = lane_f == i2
    e = jnp.exp(m2 - m1)
    w1 = 1.0 / (1.0 + e)
    w2 = e / (1.0 + e)
    oh = jnp.where(oh1 | oh2, 1.0, 0.0)
    before = jnp.dot(tri_ref[...], oh.astype(BF16), preferred_element_type=F32) + carry[...]
    r1 = jnp.sum(jnp.where(oh1, before, 0.0), axis=-1, keepdims=True)
    r2 = jnp.sum(jnp.where(oh2, before, 0.0), axis=-1, keepdims=True)
    carry[...] = carry[...] + jnp.sum(oh, axis=0, keepdims=True)
    cnt_ref[...] = carry[...]
    info = jnp.where(lane == 0, i1, jnp.where(lane == 1, i2, jnp.where(lane == 2, w1, jnp.where(
        lane == 3, w2, jnp.where(lane == 4, r1, jnp.where(lane == 5, r2, 0.0))))))
    route_ref[...] = info.T[0:8, :]


def _pool_route(x, mod, band, pw_bf, pool_scale, wr_hi, wr_lo, tri):
    b, s, _ = x.shape
    tm = TM_POOL
    ni = s // tm
    hb = tm // POOL_HALO
    row = pl.BlockSpec((1, tm, D), lambda bi, i: (bi, i, 0))
    const2 = lambda shp: pl.BlockSpec(shp, lambda bi, i: (0,) * len(shp))
    return pl.pallas_call(
        _pool_route_kernel,
        grid=(b, ni),
        in_specs=[
            row,
            pl.BlockSpec((1, POOL_HALO, D), lambda bi, i: (bi, jnp.maximum(i * hb - 1, 0), 0)),
            pl.BlockSpec((1, POOL_HALO, D), lambda bi, i: (bi, jnp.minimum((i + 1) * hb, s // POOL_HALO - 1), 0)),
            pl.BlockSpec((1, 1, 6, D), lambda bi, i: (1, bi, 0, 0)),
            const2(band.shape), const2(pw_bf.shape), const2((1, D)),
            const2(wr_hi.shape), const2(wr_lo.shape), const2(tri.shape),
        ],
        out_specs=[row,
                   pl.BlockSpec((tm * SLAB, LANES), lambda bi, i: (bi * ni + i, 0)),
                   pl.BlockSpec((8, tm), lambda bi, i: (0, bi * ni + i)),
                   pl.BlockSpec((1, LANES), lambda bi, i: (0, 0))],
        out_shape=[jax.ShapeDtypeStruct((b, s, D), F32), jax.ShapeDtypeStruct((b * s * SLAB, LANES), F32),
                   jax.ShapeDtypeStruct((8, b * s), F32), jax.ShapeDtypeStruct((1, LANES), F32)],
        scratch_shapes=[pltpu.VMEM((tm + 2 * POOL_HALO, D), BF16), pltpu.VMEM((1, LANES), F32)],
        compiler_params=_cparams(("arbitrary", "arbitrary"), VMEM_LIMIT),
        name="pool_route",
    )(x, x, x, mod, band, pw_bf, pool_scale.reshape(1, D), wr_hi, wr_lo, tri)


def _slotmap_kernel(pos_ref, lo_ref, hi_ref, o_ref):
    n_pairs = pos_ref.shape[0]
    spare_mask = 2 * TM_MOE - 1
    for e in range(lo_ref.shape[0]):
        def fill(p, c):
            o_ref[p] = n_pairs + (p & spare_mask)
            return c
        lax.fori_loop(lo_ref[e], hi_ref[e], fill, 0)

    def place(f, c):
        o_ref[pos_ref[f]] = f
        return c
    lax.fori_loop(0, n_pairs, place, 0, unroll=8)


def _slot_map(pos_flat, lo, hi, n_slots):
    smem = pl.BlockSpec(memory_space=pltpu.SMEM)
    return pl.pallas_call(
        _slotmap_kernel,
        in_specs=[smem, smem, smem],
        out_specs=smem,
        out_shape=jax.ShapeDtypeStruct((n_slots,), jnp.int32),
        name="moe_slot_map",
    )(pos_flat, lo, hi)


def _moe_kernel(te_ref, nused_ref, fnext_ref, fprev_ref, f0_ref, h_hbm, wg_hbm, wu_hbm, wd_hbm,
                y_hbm, wg_res, wu_res, wd_res, stg_in, stg_out, xbuf, xb, act, acc, stage, gsem, ssem, wsem):
    i = pl.program_id(0)
    nt = pl.num_programs(0)
    used_tiles = nused_ref[0]
    tm = xb.shape[0]
    f_dim = wg_res.shape[1]
    tok_mask = h_hbm.shape[0] // SLAB - 1
    tile_rows = tm * SLAB

    def slab(ix):
        return pl.ds(pl.multiple_of(ix * SLAB, SLAB), SLAB)

    def gather_row(fref, r, slot, zero=0):
        tok = fref[0, 0, r + zero] & tok_mask
        return pltpu.make_async_copy(h_hbm.at[slab(tok)], xbuf.at[slot, slab(r)], gsem.at[slot])

    def scatter_row(fref, r, slot, zero=0):
        return pltpu.make_async_copy(stage.at[slot, slab(r)], y_hbm.at[slab(fref[0, 0, r + zero])],
                                     ssem.at[slot])

    def gather_all(slot):
        return pltpu.make_async_copy(h_hbm.at[pl.ds(0, tile_rows)], xbuf.at[slot], gsem.at[slot])

    def scatter_all(slot):
        return pltpu.make_async_copy(stage.at[slot], y_hbm.at[pl.ds(0, tile_rows)], ssem.at[slot])

    cur = i % 2
    used = i < used_tiles

    @pl.when(i == 0)
    def _():
        stage[...] = jnp.zeros_like(stage)
        spare = y_hbm.shape[0] - 2 * tile_rows
        fills = [pltpu.make_async_copy(stage.at[sl], y_hbm.at[pl.ds(spare + sl * tile_rows, tile_rows)],
                                       ssem.at[sl])
                 for sl in range(2)]
        for cp in fills:
            cp.start()
        for cp in fills:
            cp.wait()

    @pl.when((i == 0) & used)
    def _():
        def prime(r, c):
            gather_row(f0_ref, r, 0).start()
            return c
        lax.fori_loop(0, tm, prime, 0)

    expert = te_ref[i]
    new_expert = used & ((i == 0) | (expert != te_ref[jnp.maximum(i - 1, 0)]))

    has_next = i + 1 < used_tiles
    has_prev = (i >= 1) & (i - 1 < used_tiles)
    steady = (i >= 2) & has_next

    @pl.when(used)
    def _():
        gather_all(cur).wait()
        xb[...] = _slabs_to_rows(xbuf.at[cur], tm).astype(BF16)

    @pl.when((i >= 2) & (i - 2 < used_tiles))
    def _():
        scatter_all(cur).wait()

    half = f_dim // MOE_SPLIT
    n_piece = half // W_CHUNK
    n_groups = MOE_SPLIT * (n_piece + 1)

    def tick(v):
        bits = jnp.max(lax.bitcast_convert_type(v[0:SLAB, 0:LANES], jnp.int32))
        return lax.shift_right_logical(lax.shift_right_logical(bits, 16), 16)

    def expert_ffn(issue_group, before_piece=None, before_down=None):
        xv = xb[...]
        out = None
        issue_group(0, 0)
        k = 1
        for hf in range(MOE_SPLIT):
            for c in range(n_piece):
                if before_piece is not None:
                    before_piece(hf * n_piece + c)
                cs = slice(hf * half + c * W_CHUNK, hf * half + (c + 1) * W_CHUNK)
                g = jnp.dot(xv, wg_res[:, cs], preferred_element_type=F32)
                up = jnp.dot(xv, wu_res[:, cs], preferred_element_type=F32)
                act[:, c * W_CHUNK:(c + 1) * W_CHUNK] = (g * _sigmoid(g) * up).astype(BF16)
                issue_group(k, tick(g))
                k += 1
            if before_down is not None:
                before_down(hf)
            part = jnp.dot(act[...], wd_res[hf * half:(hf + 1) * half, :], preferred_element_type=F32)
            if hf < MOE_SPLIT - 1:
                acc[...] = part if out is None else acc[...] + part
                out = acc
                issue_group(k, tick(part))
                k += 1
            else:
                _rows_to_slabs(part if out is None else acc[...] + part, stage.at[cur])

    @pl.when(new_expert)
    def _():
        col_plan = []
        for p in range(MOE_SPLIT * n_piece):
            col_plan += [(wg_hbm, wg_res, p), (wu_hbm, wu_res, p)]
        n_row = f_dim // W_CHUNK

        def col_copy(q):
            w_hbm, _, p = col_plan[q]
            return pltpu.make_async_copy(w_hbm.at[expert, :, pl.ds(p * W_CHUNK, W_CHUNK)],
                                         stg_in.at[q % W_RING], wsem.at[q % W_RING])

        def row_copy(q):
            return pltpu.make_async_copy(wd_hbm.at[expert, pl.ds(q * W_CHUNK, W_CHUNK), :],
                                         stg_out.at[q % W_RING], wsem.at[W_RING + q % W_RING])

        for q in range(W_RING):
            col_copy(q).start()
        for q in range(W_RING):
            row_copy(q).start()

        def take_col(q):
            _, res, p = col_plan[q]
            col_copy(q).wait()
            res[:, p * W_CHUNK:(p + 1) * W_CHUNK] = stg_in[q % W_RING].astype(BF16)
            if q + W_RING < len(col_plan):
                col_copy(q + W_RING).start()

        def take_row(q):
            row_copy(q).wait()
            wd_res[q * W_CHUNK:(q + 1) * W_CHUNK, :] = stg_out[q % W_RING].astype(BF16)
            if q + W_RING < n_row:
                row_copy(q + W_RING).start()

        def before_piece(p):
            take_col(2 * p)
            take_col(2 * p + 1)

        def before_down(hf):
            for q in range(hf * n_piece, (hf + 1) * n_piece):
                take_row(q)

        def issue_group(k, zero):
            lo, hi = k * tm // n_groups, (k + 1) * tm // n_groups

            @pl.when(has_next)
            def _():
                def issue(r, c):
                    gather_row(fnext_ref, r, 1 - cur).start()
                    return c
                lax.fori_loop(lo, hi, issue, 0)

            @pl.when(has_prev)
            def _():
                def issue(r, c):
                    scatter_row(fprev_ref, r, 1 - cur).start()
                    return c
                lax.fori_loop(lo, hi, issue, 0)

        expert_ffn(issue_group, before_piece, before_down)

    @pl.when(steady & jnp.logical_not(new_expert))
    def _():
        def issue_group(k, zero):
            for r in range(k * tm // n_groups, (k + 1) * tm // n_groups):
                gather_row(fnext_ref, r, 1 - cur, zero).start()
                scatter_row(fprev_ref, r, 1 - cur, zero).start()
        expert_ffn(issue_group)

    @pl.when(jnp.logical_not(steady) & jnp.logical_not(new_expert))
    def _():
        @pl.when(used)
        def _():
            expert_ffn(lambda k, zero: None)

        @pl.when(has_next)
        def _():
            def issue(r, c):
                gather_row(fnext_ref, r, 1 - cur).start()
                return c
            lax.fori_loop(0, tm, issue, 0)

        @pl.when(has_prev)
        def _():
            def issue(r, c):
                scatter_row(fprev_ref, r, 1 - cur).start()
                return c
            lax.fori_loop(0, tm, issue, 0)

    @pl.when((i == nt - 1) & (nt - 2 < used_tiles))
    def _():
        scatter_all(1 - cur).wait()


def _moe_experts(h_slabs, fmap, tile_expert, n_used, wg, wu, wd, n_tiles, y_rows):
    f = wg.shape[2]
    tm = TM_MOE
    fblk = lambda imap: pl.BlockSpec((1, 1, tm), imap, memory_space=pltpu.SMEM)
    hbm = pl.BlockSpec(memory_space=pl.ANY)
    return pl.pallas_call(
        _moe_kernel,
        grid_spec=pltpu.PrefetchScalarGridSpec(
            num_scalar_prefetch=2,
            grid=(n_tiles,),
            in_specs=[
                fblk(lambda i, te, nu: (jnp.minimum(i + 1, n_tiles - 1), 0, 0)),
                fblk(lambda i, te, nu: (jnp.maximum(i - 1, 0), 0, 0)),
                fblk(lambda i, te, nu: (0, 0, 0)),
                hbm, hbm, hbm, hbm,
            ],
            out_specs=hbm,
            scratch_shapes=[pltpu.VMEM((D, f), BF16), pltpu.VMEM((D, f), BF16), pltpu.VMEM((f, D), BF16),
                            pltpu.VMEM((W_RING, D, W_CHUNK), F32), pltpu.VMEM((W_RING, W_CHUNK, D), F32),
                            pltpu.VMEM((2, tm * SLAB, LANES), F32), pltpu.VMEM((tm, D), BF16),
                            pltpu.VMEM((tm, f // MOE_SPLIT), BF16), pltpu.VMEM((tm, D), F32),
                            pltpu.VMEM((2, tm * SLAB, LANES), F32),
                            pltpu.SemaphoreType.DMA((2,)), pltpu.SemaphoreType.DMA((2,)),
                            pltpu.SemaphoreType.DMA((2 * W_RING,))],
        ),
        out_shape=jax.ShapeDtypeStruct((y_rows * SLAB, LANES), F32),
        compiler_params=_cparams(("arbitrary",), MOE_VMEM_LIMIT),
        name="moe_experts",
    )(tile_expert, n_used, fmap, fmap, fmap, h_slabs, wg, wu, wd)


def _combine_kernel(y1_ref, y2_ref, x_ref, w_ref, mod_ref, gain_ref, o_ref):
    w = w_ref[...]
    rows = x_ref.shape[0]
    moe = w[:, 0:1] * _slabs_to_rows(y1_ref, rows) + w[:, 1:2] * _slabs_to_rows(y2_ref, rows)
    mod = mod_ref[0, 0]
    x4 = x_ref[...] + mod[5:6] * moe
    ms = jnp.mean(x4 * x4, axis=-1, keepdims=True)
    o_ref[...] = x4 * lax.rsqrt(ms + EPS) * gain_ref[...]


def _combine(y, x3_2d, wts, mod, final_gain, seq):
    n = x3_2d.shape[0]
    tc = TC_COMB
    nt = n // tc
    per_b = seq // tc
    return pl.pallas_call(
        _combine_kernel,
        grid=(nt,),
        in_specs=[
            pl.BlockSpec((tc * SLAB, LANES), lambda i: (i, 0)),
            pl.BlockSpec((tc * SLAB, LANES), lambda i: (i + nt, 0)),
            pl.BlockSpec((tc, D), lambda i: (i, 0)),
            pl.BlockSpec((tc, 2), lambda i: (i, 0)),
            pl.BlockSpec((1, 1, 6, D), lambda i: (1, i // per_b, 0, 0)),
            pl.BlockSpec((1, D), lambda i: (0, 0)),
        ],
        out_specs=pl.BlockSpec((tc, D), lambda i: (i, 0)),
        out_shape=jax.ShapeDtypeStruct((n, D), F32),
        compiler_params=_cparams(("arbitrary",)),
        name="moe_combine",
    )(y, y, x3_2d, wts, mod, final_gain.reshape(1, D))


def _rope_tables(seq):
    rows = seq // GRID_W
    row_pos = jnp.repeat(jnp.arange(rows, dtype=F32), GRID_W)
    col_pos = jnp.tile(jnp.arange(GRID_W, dtype=F32), rows)
    axis_dim = HEAD_DIM // 2
    inv_freq = ROPE_BASE ** (-jnp.arange(0, axis_dim, 2, dtype=F32) / axis_dim)
    ar = row_pos[:, None] * inv_freq
    ac = col_pos[:, None] * inv_freq
    cos64 = jnp.concatenate([jnp.cos(ar), jnp.cos(ar), jnp.cos(ac), jnp.cos(ac)], axis=1)
    sin64 = jnp.concatenate([-jnp.sin(ar), jnp.sin(ar), -jnp.sin(ac), jnp.sin(ac)], axis=1)
    return jnp.tile(cos64, (1, 2)), jnp.tile(sin64, (1, 2))


def _band_matrices():
    r = np.arange(BLOCK)[:, None]
    c = np.arange(BLOCK + 2 * POOL_HALO)[None, :] - POOL_HALO
    mats = []
    for w in POOL_SIZES:
        lo = -(w // 2)
        hi = w - 1 - w // 2
        mats.append(((c >= r + lo) & (c <= r + hi)).astype(np.float32))
    return jnp.asarray(np.stack(mats), dtype=BF16)


def kernel(x, c, ctx, c_ctx, w_ada, b_ada, w_in, attn_sink, gm_gain, gm_w_s, gm_b_s, w_out,
           ffn_w_gate, ffn_w_up, ffn_w_down, pool_w, pool_scale, router_w,
           moe_w_gate, moe_w_up, moe_w_down, final_gain):
    b, s, _ = x.shape
    n = b * s
    assert w_ada.shape[0] == 2 and w_in.shape[0] == 1 and pool_w.shape[0] == 1
    assert s % TM_IN == 0 and s % TM_POOL == 0 and s % TM_FFN == 0 and b <= 4
    assert n & (n - 1) == 0

    cvec = jnp.concatenate([c, c_ctx[None, :], jnp.zeros((8 - b - 1, D), F32)], axis=0)
    mod = _ada_mod(cvec, w_ada, b_ada)

    cos_t, sin_t = _rope_tables(s)
    w_in_bf = w_in[0].astype(BF16)
    q, k, ksw, v, vsw, u, vg = _in_proj(x, mod, w_in_bf, gm_gain[0], cos_t, sin_t)
    kx, ksx, vx, vsx = _ctx_kv(ctx, mod, w_in_bf)
    wcat = gm_w_s[0].reshape(4, 2, BLOCK, BLOCK).transpose(0, 2, 1, 3).reshape(4, BLOCK, 2 * BLOCK).astype(BF16)
    x1 = _attn_mixer(x, mod, attn_sink[0], q, k, ksw, v, vsw, kx, ksx, vx, vsx, u, vg,
                     wcat, gm_b_s[0].T, w_out[0].astype(BF16))
    x2 = _dense_ffn(x1.reshape(n, D), mod, ffn_w_gate[0].astype(BF16), ffn_w_up[0].astype(BF16),
                    ffn_w_down[0].astype(BF16), s)

    wr = jnp.pad(router_w[0], ((0, 0), (0, LANES - N_EXPERTS)))
    wr_hi = wr.astype(BF16)
    wr_lo = (wr - wr_hi.astype(F32)).astype(BF16)
    tri = jnp.asarray(np.tril(np.ones((TM_POOL, TM_POOL), np.float32), -1), dtype=BF16)
    x3, h2, route, counts = _pool_route(x2.reshape(b, s, D), mod, _band_matrices(), pool_w[0].astype(BF16),
                                        pool_scale[0], wr_hi, wr_lo, tri)

    tm = TM_MOE
    n_tiles = (2 * n) // tm + N_EXPERTS
    cnt = counts[0, :N_EXPERTS].astype(jnp.int32)
    tiles_e = (cnt + tm - 1) // tm
    tile_end = jnp.cumsum(tiles_e)
    off = (tile_end - tiles_e) * tm
    n_used = tile_end[-1]
    tix = jnp.arange(n_tiles, dtype=jnp.int32)
    te = jnp.minimum(jnp.sum(tix[:, None] >= tile_end[None, :], axis=1), N_EXPERTS - 1).astype(jnp.int32)
    te_last = te[jnp.maximum(n_used - 1, 0)]
    tile_expert = jnp.where(tix < n_used, te, te_last)
    e1 = route[0].astype(jnp.int32)
    e2 = route[1].astype(jnp.int32)
    pos1 = off[e1] + route[4].astype(jnp.int32)
    pos2 = off[e2] + route[5].astype(jnp.int32)
    n_slots = n_tiles * tm
    pad_lo = jnp.concatenate([off + cnt, (n_used * tm).reshape(1)]).astype(jnp.int32)
    pad_hi = jnp.concatenate([off + tiles_e * tm, jnp.full((1,), n_slots, jnp.int32)]).astype(jnp.int32)
    fmap = _slot_map(jnp.concatenate([pos1, pos2]), pad_lo, pad_hi, n_slots)
    n_used_arr = n_used.reshape(1).astype(jnp.int32)

    y = _moe_experts(h2, fmap.reshape(n_tiles, 1, tm), tile_expert, n_used_arr,
                     moe_w_gate[0], moe_w_up[0], moe_w_down[0], n_tiles, 2 * n + 2 * tm)
    out = _combine(y, x3.reshape(n, D), route[2:4].T, mod, final_gain, s)
    return out.reshape(b, s, D)
```

```python
import functools

import numpy as np
import jax
import jax.numpy as jnp
from jax import lax
from jax.experimental import pallas as pl
from jax.experimental.pallas import tpu as pltpu

F32 = jnp.float32
BF16 = jnp.bfloat16

D = 1024
GRID_W = 64
EPS = 1e-6
NEG_INF = -1e30
HEAD_DIM = 64
N_Q_HEADS = 8
BLOCK = 128
ATT_W = 512
KV_W = 128
GM_W = 512
IN_W = 1792
POOL_SIZES = (2, 4, 8, 16)
POOL_GD = 256
POOL_HALO = 16
N_EXPERTS = 8
ROPE_BASE = 10000.0
LANES = 128
SLAB = D // LANES
SQRT_2_OVER_PI = 0.7978845608028654

TM_IN = 1024
TM_FFN = 512
TM_POOL = 512
TM_MOE = 512
MOE_SPLIT = 2
W_CHUNK = 256
W_ROWS_IN = 128
W_ROWS_OUT = 512
W_RING = 3
MOE_VMEM_LIMIT = 60 * 1024 * 1024
TC_COMB = 512
Q_BLOCKS = 4
VMEM_LIMIT = 56 * 1024 * 1024


def _cparams(sem, vmem=None):
    return pltpu.CompilerParams(dimension_semantics=sem, vmem_limit_bytes=vmem)


def _modulate(xf, shift, scale):
    ms = jnp.mean(xf * xf, axis=-1, keepdims=True)
    return xf * lax.rsqrt(ms + EPS) * (1.0 + scale) + shift


def _sigmoid(z):
    return 1.0 / (1.0 + jnp.exp(-z))


def _rows_to_slabs(val, slab_ref):
    rows = val.shape[0]
    for cix in range(SLAB):
        slab_ref[pl.ds(cix, rows, stride=SLAB), :] = val[:, cix * LANES:(cix + 1) * LANES]


def _slabs_to_rows(slab_ref, rows):
    return jnp.concatenate([slab_ref[pl.ds(cix, rows, stride=SLAB), :] for cix in range(SLAB)], axis=1)


def _ada_kernel(c_ref, w_ref, b_ref, o_ref):
    c = c_ref[...]
    s = c * _sigmoid(c)
    o_ref[0] = jnp.dot(s.astype(BF16), w_ref[0].astype(BF16), preferred_element_type=F32) + b_ref[0]


def _ada_mod(cvec, w_ada, b_ada):
    depth, _, n6 = w_ada.shape
    tn = 1536
    out = pl.pallas_call(
        _ada_kernel,
        grid=(depth, n6 // tn),
        in_specs=[
            pl.BlockSpec((8, D), lambda l, j: (0, 0)),
            pl.BlockSpec((1, D, tn), lambda l, j: (l, 0, j)),
            pl.BlockSpec((1, 1, tn), lambda l, j: (l, 0, j)),
        ],
        out_specs=pl.BlockSpec((1, 8, tn), lambda l, j: (l, 0, j)),
        out_shape=jax.ShapeDtypeStruct((depth, 8, n6), F32),
        compiler_params=_cparams(("arbitrary", "arbitrary")),
        name="ada_mod",
    )(cvec, w_ada, b_ada.reshape(depth, 1, n6))
    return out.reshape(depth, 8, 6, D)


def _rope(t, cs, sn, first_half):
    fwd = pltpu.roll(t, LANES - 16, axis=1)
    bwd = pltpu.roll(t, 16, axis=1)
    return t * cs + jnp.where(first_half, fwd, bwd) * sn


def _inproj_kernel(x_ref, mod_ref, w_ref, gain_ref, cos_ref, sin_ref,
                   q_ref, k_ref, ksw_ref, v_ref, vsw_ref, u_ref, vg_ref):
    mod = mod_ref[0, 0]
    h = _modulate(x_ref[0], mod[0:1], mod[1:2]).astype(BF16)
    proj = jnp.dot(h, w_ref[...], preferred_element_type=F32)
    cs = cos_ref[...]
    sn = sin_ref[...]
    lane = lax.broadcasted_iota(jnp.int32, cs.shape, 1)
    first_half = (lane & 16) == 0
    for cix in range(ATT_W // LANES):
        t = proj[:, cix * LANES:(cix + 1) * LANES]
        q_ref[0, :, cix * LANES:(cix + 1) * LANES] = (
            _rope(t, cs, sn, first_half) * (HEAD_DIM ** -0.5)).astype(BF16)
    kr = _rope(proj[:, ATT_W:ATT_W + KV_W], cs, sn, first_half)
    k_ref[0] = kr.astype(BF16)
    ksw_ref[0] = pltpu.roll(kr, HEAD_DIM, axis=1).astype(BF16)
    vv = proj[:, ATT_W + KV_W:ATT_W + 2 * KV_W]
    v_ref[0] = vv.astype(BF16)
    vsw_ref[0] = pltpu.roll(vv, HEAD_DIM, axis=1).astype(BF16)
    z = proj[:, ATT_W + 2 * KV_W:]
    g = z * (0.5 * (1.0 + jnp.tanh(SQRT_2_OVER_PI * (z + 0.044715 * (z * z * z)))))
    u_ref[0] = g[:, :GM_W].astype(BF16)
    vg = g[:, GM_W:]
    ms = jnp.mean(vg * vg, axis=-1, keepdims=True)
    vg_ref[0] = (vg * lax.rsqrt(ms + EPS) * gain_ref[...]).astype(BF16)


def _in_proj(x, mod, w_in_bf, gm_gain, cos_t, sin_t):
    b, s, _ = x.shape
    tm = TM_IN
    row = lambda w: pl.BlockSpec((1, tm, w), lambda bi, i: (bi, i, 0))
    outs = pl.pallas_call(
        _inproj_kernel,
        grid=(b, s // tm),
        in_specs=[
            row(D),
            pl.BlockSpec((1, 1, 6, D), lambda bi, i: (0, bi, 0, 0)),
            pl.BlockSpec((D, IN_W), lambda bi, i: (0, 0)),
            pl.BlockSpec((1, GM_W), lambda bi, i: (0, 0)),
            pl.BlockSpec((tm, LANES), lambda bi, i: (i, 0)),
            pl.BlockSpec((tm, LANES), lambda bi, i: (i, 0)),
        ],
        out_specs=[row(ATT_W), row(KV_W), row(KV_W), row(KV_W), row(KV_W), row(GM_W), row(GM_W)],
        out_shape=[jax.ShapeDtypeStruct((b, s, w), BF16)
                   for w in (ATT_W, KV_W, KV_W, KV_W, KV_W, GM_W, GM_W)],
        compiler_params=_cparams(("arbitrary", "arbitrary"), VMEM_LIMIT),
        name="in_proj",
    )(x, mod, w_in_bf, gm_gain.reshape(1, GM_W), cos_t, sin_t)
    return outs


def _ctx_kernel(c_ref, mod_ref, w_ref, k_ref, ksw_ref, v_ref, vsw_ref):
    mod = mod_ref[0, 0]
    h = _modulate(c_ref[0], mod[0:1], mod[1:2]).astype(BF16)
    kv = jnp.dot(h, w_ref[...], preferred_element_type=F32)
    kk = kv[:, :KV_W]
    vv = kv[:, KV_W:]
    k_ref[0] = kk.astype(BF16)
    ksw_ref[0] = pltpu.roll(kk, HEAD_DIM, axis=1).astype(BF16)
    v_ref[0] = vv.astype(BF16)
    vsw_ref[0] = pltpu.roll(vv, HEAD_DIM, axis=1).astype(BF16)


def _ctx_kv(ctx, mod, w_in_bf):
    b, l, _ = ctx.shape
    spec = pl.BlockSpec((1, l, KV_W), lambda bi: (bi, 0, 0))
    return pl.pallas_call(
        _ctx_kernel,
        grid=(b,),
        in_specs=[
            pl.BlockSpec((1, l, D), lambda bi: (bi, 0, 0)),
            pl.BlockSpec((1, 1, 6, D), lambda bi: (0, b, 0, 0)),
            pl.BlockSpec((D, 2 * KV_W), lambda bi: (0, ATT_W // (2 * KV_W))),
        ],
        out_specs=[spec] * 4,
        out_shape=[jax.ShapeDtypeStruct((b, l, KV_W), BF16)] * 4,
        compiler_params=_cparams(("arbitrary",)),
        name="ctx_kv",
    )(ctx, mod, w_in_bf)


def _attn_kernel(sink_ref, q_ref, kp_ref, kc_ref, kn_ref, ksp_ref, ksc_ref, ksn_ref,
                 vp_ref, vc_ref, vn_ref, vsp_ref, vsc_ref, vsn_ref,
                 kx_ref, ksx_ref, vx_ref, vsx_ref,
                 u_ref, vg_ref, wcat_ref, bs_ref, wout_ref, x_ref, mod_ref, o_ref):
    n = pl.program_id(1)
    nblk = pl.num_programs(1) * Q_BLOCKS
    lane = lax.broadcasted_iota(jnp.int32, (1, LANES), 1)
    low = lane < HEAD_DIM
    zero = jnp.zeros((), BF16)

    def variants(a0, a1):
        return ((jnp.where(low, a0, zero), jnp.where(low, zero, a1)),
                (jnp.where(low, a1, zero), jnp.where(low, zero, a0)))

    cat = lambda refs: jnp.concatenate([r[0] for r in refs], axis=0)
    kb_var = variants(cat((kp_ref, kc_ref, kn_ref)), cat((ksp_ref, ksc_ref, ksn_ref)))
    vb_var = variants(cat((vp_ref, vc_ref, vn_ref)), cat((vsp_ref, vsc_ref, vsn_ref)))
    kx_var = variants(kx_ref[0], ksx_ref[0])
    vx_var = variants(vx_ref[0], vsx_ref[0])

    row = lax.broadcasted_iota(jnp.int32, (2 * BLOCK, BLOCK), 0) & (BLOCK - 1)
    col = lax.broadcasted_iota(jnp.int32, (2 * BLOCK, BLOCK), 1)
    top = lax.broadcasted_iota(jnp.int32, (2 * BLOCK, 1), 0) < BLOCK
    nt_dims = (((1,), (1,)), ((), ()))

    q = q_ref[0]
    att_blocks = [[None] * 4 for _ in range(Q_BLOCKS)]
    for kvh in range(2):
        qst = jnp.concatenate(
            [q[qb * BLOCK:(qb + 1) * BLOCK, pr * LANES:(pr + 1) * LANES]
             for qb in range(Q_BLOCKS) for pr in (2 * kvh, 2 * kvh + 1)], axis=0)
        accs = [None] * Q_BLOCKS
        for half in range(2):
            sk = jnp.where(top, sink_ref[4 * kvh + half], sink_ref[4 * kvh + 2 + half])
            s_ctx = lax.dot_general(qst, kx_var[kvh][half], nt_dims, preferred_element_type=F32)
            for qb in range(Q_BLOCKS):
                g = n * Q_BLOCKS + qb
                qrows = qst[qb * 2 * BLOCK:(qb + 1) * 2 * BLOCK]
                sb = lax.dot_general(qrows, kb_var[kvh][half][qb * BLOCK:(qb + 3) * BLOCK], nt_dims,
                                     preferred_element_type=F32)
                s0 = jnp.where((col >= row) & (g > 0), sb[:, :BLOCK], NEG_INF)
                s1 = sb[:, BLOCK:2 * BLOCK]
                s2 = jnp.where((col <= row) & (g < nblk - 1), sb[:, 2 * BLOCK:], NEG_INF)
                sc = s_ctx[qb * 2 * BLOCK:(qb + 1) * 2 * BLOCK]
                m = jnp.maximum(jnp.maximum(s0, s1), s2)
                m = jnp.maximum(jnp.max(m, axis=-1, keepdims=True), jnp.max(sc, axis=-1, keepdims=True))
                m = jnp.maximum(m, sk)
                p0, p1, p2, pc = (jnp.exp(t - m) for t in (s0, s1, s2, sc))
                den = (jnp.sum(p0 + p1 + p2, axis=-1, keepdims=True) + jnp.sum(pc, axis=-1, keepdims=True)
                       + jnp.exp(sk - m))
                pb = jnp.concatenate([p0, p1, p2], axis=1).astype(BF16)
                o = (jnp.dot(pb, vb_var[kvh][half][qb * BLOCK:(qb + 3) * BLOCK], preferred_element_type=F32)
                     + jnp.dot(pc.astype(BF16), vx_var[kvh][half], preferred_element_type=F32))
                o = o / den
                accs[qb] = o if accs[qb] is None else accs[qb] + o
        for qb in range(Q_BLOCKS):
            att_blocks[qb][2 * kvh] = accs[qb][:BLOCK]
            att_blocks[qb][2 * kvh + 1] = accs[qb][BLOCK:]

    u = u_ref[0]
    vg = vg_ref[0]
    bs = bs_ref[...]
    gm_blocks = [[None] * 4 for _ in range(Q_BLOCKS)]
    for j in range(GM_W // LANES):
        chunks = [vg[c * BLOCK:(c + 1) * BLOCK, j * LANES:(j + 1) * LANES] for c in range(Q_BLOCKS)]
        rhs = jnp.concatenate(
            [jnp.concatenate([jnp.where(low, v, zero) for v in chunks], axis=1),
             jnp.concatenate([jnp.where(low, zero, v) for v in chunks], axis=1)], axis=0)
        mixed = jnp.dot(wcat_ref[j], rhs, preferred_element_type=F32)
        bias = jnp.where(low, bs[:, 2 * j:2 * j + 1], bs[:, 2 * j + 1:2 * j + 2])
        for c in range(Q_BLOCKS):
            gm_blocks[c][j] = (u[c * BLOCK:(c + 1) * BLOCK, j * LANES:(j + 1) * LANES].astype(F32)
                               * (mixed[:, c * LANES:(c + 1) * LANES] + bias))

    mix = jnp.concatenate([jnp.concatenate(att_blocks[c] + gm_blocks[c], axis=1) for c in range(Q_BLOCKS)],
                          axis=0).astype(BF16)
    y = jnp.dot(mix, wout_ref[...], preferred_element_type=F32)
    mod = mod_ref[0, 0]
    o_ref[0] = x_ref[0] + mod[2:3] * y


def _attn_mixer(x, mod, sink, q, k, ksw, v, vsw, kx, ksx, vx, vsx, u, vg, wcat_bf, bs_t, wout_bf):
    b, s, _ = x.shape
    tq = Q_BLOCKS * BLOCK
    nb = s // BLOCK
    l = kx.shape[1]
    cur = lambda w: pl.BlockSpec((1, tq, w), lambda bi, n: (bi, n, 0))
    prv = lambda w: pl.BlockSpec((1, BLOCK, w), lambda bi, n: (bi, jnp.maximum(n * Q_BLOCKS - 1, 0), 0))
    nxt = lambda w: pl.BlockSpec((1, BLOCK, w), lambda bi, n: (bi, jnp.minimum((n + 1) * Q_BLOCKS, nb - 1), 0))
    cx = pl.BlockSpec((1, l, KV_W), lambda bi, n: (bi, 0, 0))
    return pl.pallas_call(
        _attn_kernel,
        grid=(b, s // tq),
        in_specs=[
            pl.BlockSpec(memory_space=pltpu.SMEM),
            cur(ATT_W),
            prv(KV_W), cur(KV_W), nxt(KV_W), prv(KV_W), cur(KV_W), nxt(KV_W),
            prv(KV_W), cur(KV_W), nxt(KV_W), prv(KV_W), cur(KV_W), nxt(KV_W),
            cx, cx, cx, cx,
            cur(GM_W), cur(GM_W),
            pl.BlockSpec((4, BLOCK, 2 * BLOCK), lambda bi, n: (0, 0, 0)),
            pl.BlockSpec((BLOCK, 8), lambda bi, n: (0, 0)),
            pl.BlockSpec((D, D), lambda bi, n: (0, 0)),
            cur(D),
            pl.BlockSpec((1, 1, 6, D), lambda bi, n: (0, bi, 0, 0)),
        ],
        out_specs=cur(D),
        out_shape=jax.ShapeDtypeStruct((b, s, D), F32),
        compiler_params=_cparams(("arbitrary", "arbitrary"), VMEM_LIMIT),
        name="attn_gmlp_out",
    )(sink, q, k, k, k, ksw, ksw, ksw, v, v, v, vsw, vsw, vsw, kx, ksx, vx, vsx,
      u, vg, wcat_bf, bs_t, wout_bf, x, mod)


def _ffn_kernel(x_ref, mod_ref, wg_ref, wu_ref, wd_ref, o_ref):
    mod = mod_ref[0, 0]
    xf = x_ref[...]
    h = _modulate(xf, mod[3:4], mod[4:5]).astype(BF16)
    g = jnp.dot(h, wg_ref[...], preferred_element_type=F32)
    up = jnp.dot(h, wu_ref[...], preferred_element_type=F32)
    a = (g * _sigmoid(g) * up).astype(BF16)
    o_ref[...] = xf + mod[5:6] * jnp.dot(a, wd_ref[...], preferred_element_type=F32)


def _dense_ffn(x2d, mod, wg, wu, wd, seq):
    n = x2d.shape[0]
    f = wg.shape[1]
    tm = TM_FFN
    per_b = seq // tm
    resident = lambda shp: pl.BlockSpec(shp, lambda i: (0, 0), pipeline_mode=pl.Buffered(1))
    return pl.pallas_call(
        _ffn_kernel,
        grid=(n // tm,),
        in_specs=[
            pl.BlockSpec((tm, D), lambda i: (i, 0)),
            pl.BlockSpec((1, 1, 6, D), lambda i: (0, i // per_b, 0, 0)),
            resident((D, f)), resident((D, f)), resident((f, D)),
        ],
        out_specs=pl.BlockSpec((tm, D), lambda i: (i, 0)),
        out_shape=jax.ShapeDtypeStruct((n, D), F32),
        compiler_params=_cparams(("arbitrary",), VMEM_LIMIT),
        name="dense_ffn",
    )(x2d, mod, wg, wu, wd)


def _pool_route_kernel(x_ref, xp_ref, xn_ref, mod_ref, band_ref, pw_ref, psc_ref, wr_hi_ref, wr_lo_ref,
                       tri_ref, x3_ref, h2_ref, route_ref, cnt_ref, hext, carry):
    bi = pl.program_id(0)
    i = pl.program_id(1)
    ni = pl.num_programs(1)
    tm = x_ref.shape[1]
    seq = tm * ni
    mod = mod_ref[0, 0]

    @pl.when((bi == 0) & (i == 0))
    def _():
        carry[...] = jnp.zeros_like(carry)

    xf = x_ref[0]
    hp = _modulate(xp_ref[0], mod[0:1], mod[1:2])
    hn = _modulate(xn_ref[0], mod[0:1], mod[1:2])
    hext[0:POOL_HALO] = jnp.where(i > 0, hp, 0.0).astype(BF16)
    h_main = _modulate(xf, mod[0:1], mod[1:2])
    hext[POOL_HALO:POOL_HALO + tm] = h_main.astype(BF16)
    hext[POOL_HALO + tm:] = jnp.where(i < ni - 1, hn, 0.0).astype(BF16)

    t_local = lax.broadcasted_iota(jnp.int32, (BLOCK, 1), 0)
    ys = []
    for gi, w in enumerate(POOL_SIZES):
        lo_off = -(w // 2)
        hi_off = w - 1 - w // 2
        cols = slice(gi * POOL_GD, (gi + 1) * POOL_GD)
        outs = []
        for sb in range(tm // BLOCK):
            r0 = sb * BLOCK
            win = jnp.dot(band_ref[gi], hext[r0:r0 + BLOCK + 2 * POOL_HALO, cols],
                          preferred_element_type=F32)
            t = i * tm + r0 + t_local
            cnt = (jnp.minimum(t + hi_off, seq - 1) - jnp.maximum(t + lo_off, 0) + 1).astype(F32)
            diff = win / cnt - h_main[r0:r0 + BLOCK, cols]
            outs.append(diff.astype(BF16))
        dg = jnp.concatenate(outs, axis=0)
        ys.append(jnp.dot(dg, pw_ref[gi], preferred_element_type=F32))
    y = jnp.concatenate(ys, axis=1) * psc_ref[...]
    x3 = xf + mod[2:3] * y
    x3_ref[0] = x3

    h2 = _modulate(x3, mod[3:4], mod[4:5])
    _rows_to_slabs(h2, h2_ref)
    h_hi = h2.astype(BF16)
    h_lo = (h2 - h_hi.astype(F32)).astype(BF16)
    logits = (jnp.dot(h_hi, wr_hi_ref[...], preferred_element_type=F32)
              + jnp.dot(h_hi, wr_lo_ref[...], preferred_element_type=F32)
              + jnp.dot(h_lo, wr_hi_ref[...], preferred_element_type=F32))
    lane = lax.broadcasted_iota(jnp.int32, (tm, LANES), 1)
    lane_f = lane.astype(F32)
    neg = -jnp.inf
    lg = jnp.where(lane < N_EXPERTS, logits, neg)
    m1 = jnp.max(lg, axis=-1, keepdims=True)
    i1 = jnp.min(jnp.where(lg == m1, lane_f, float(LANES)), axis=-1, keepdims=True)
    oh1 = lane_f == i1
    lg2 = jnp.where(oh1, neg, lg)
    m2 = jnp.max(lg2, axis=-1, keepdims=True)
    i2 = jnp.min(jnp.where(lg2 == m2, lane_f, float(LANES)), axis=-1, keepdims=True)
    oh2 = lane_f == i2
    e = jnp.exp(m2 - m1)
    w1 = 1.0 / (1.0 + e)
    w2 = e / (1.0 + e)
    oh = jnp.where(oh1 | oh2, 1.0, 0.0)
    before = jnp.dot(tri_ref[...], oh.astype(BF16), preferred_element_type=F32) + carry[...]
    r1 = jnp.sum(jnp.where(oh1, before, 0.0), axis=-1, keepdims=True)
    r2 = jnp.sum(jnp.where(oh2, before, 0.0), axis=-1, keepdims=True)
    carry[...] = carry[...] + jnp.sum(oh, axis=0, keepdims=True)
    cnt_ref[...] = carry[...]
    info = jnp.where(lane == 0, i1, jnp.where(lane == 1, i2, jnp.where(lane == 2, w1, jnp.where(
        lane == 3, w2, jnp.where(lane == 4, r1, jnp.where(lane == 5, r2, 0.0))))))
    route_ref[...] = info.T[0:8, :]


def _pool_route(x, mod, band, pw_bf, pool_scale, wr_hi, wr_lo, tri):
    b, s, _ = x.shape
    tm = TM_POOL
    ni = s // tm
    hb = tm // POOL_HALO
    row = pl.BlockSpec((1, tm, D), lambda bi, i: (bi, i, 0))
    const2 = lambda shp: pl.BlockSpec(shp, lambda bi, i: (0,) * len(shp))
    return pl.pallas_call(
        _pool_route_kernel,
        grid=(b, ni),
        in_specs=[
            row,
            pl.BlockSpec((1, POOL_HALO, D), lambda bi, i: (bi, jnp.maximum(i * hb - 1, 0), 0)),
            pl.BlockSpec((1, POOL_HALO, D), lambda bi, i: (bi, jnp.minimum((i + 1) * hb, s // POOL_HALO - 1), 0)),
            pl.BlockSpec((1, 1, 6, D), lambda bi, i: (1, bi, 0, 0)),
            const2(band.shape), const2(pw_bf.shape), const2((1, D)),
            const2(wr_hi.shape), const2(wr_lo.shape), const2(tri.shape),
        ],
        out_specs=[row,
                   pl.BlockSpec((tm * SLAB, LANES), lambda bi, i: (bi * ni + i, 0)),
                   pl.BlockSpec((8, tm), lambda bi, i: (0, bi * ni + i)),
                   pl.BlockSpec((1, LANES), lambda bi, i: (0, 0))],
        out_shape=[jax.ShapeDtypeStruct((b, s, D), F32), jax.ShapeDtypeStruct((b * s * SLAB, LANES), F32),
                   jax.ShapeDtypeStruct((8, b * s), F32), jax.ShapeDtypeStruct((1, LANES), F32)],
        scratch_shapes=[pltpu.VMEM((tm + 2 * POOL_HALO, D), BF16), pltpu.VMEM((1, LANES), F32)],
        compiler_params=_cparams(("arbitrary", "arbitrary"), VMEM_LIMIT),
        name="pool_route",
    )(x, x, x, mod, band, pw_bf, pool_scale.reshape(1, D), wr_hi, wr_lo, tri)


def _slotmap_kernel(pos_ref, lo_ref, hi_ref, o_ref):
    n_pairs = pos_ref.shape[0]
    spare_mask = 2 * TM_MOE - 1
    for e in range(lo_ref.shape[0]):
        def fill(p, c):
            o_ref[p] = n_pairs + (p & spare_mask)
            return c
        lax.fori_loop(lo_ref[e], hi_ref[e], fill, 0)

    def place(f, c):
        o_ref[pos_ref[f]] = f
        return c
    lax.fori_loop(0, n_pairs, place, 0, unroll=32)


def _slot_map(pos_flat, lo, hi, n_slots):
    smem = pl.BlockSpec(memory_space=pltpu.SMEM)
    return pl.pallas_call(
        _slotmap_kernel,
        in_specs=[smem, smem, smem],
        out_specs=smem,
        out_shape=jax.ShapeDtypeStruct((n_slots,), jnp.int32),
        name="moe_slot_map",
    )(pos_flat, lo, hi)


def _moe_kernel(te_ref, nused_ref, fnext_ref, fprev_ref, f0_ref, h_hbm, wg_hbm, wu_hbm, wd_hbm,
                y_hbm, wg_res, wu_res, wd_res, stg_in, stg_out, xbuf, xb, act, acc, stage, gsem, ssem, wsem):
    i = pl.program_id(0)
    nt = pl.num_programs(0)
    used_tiles = nused_ref[0]
    tm = xb.shape[0]
    f_dim = wg_res.shape[1]
    tok_mask = h_hbm.shape[0] // SLAB - 1
    tile_rows = tm * SLAB

    def slab(ix):
        return pl.ds(pl.multiple_of(ix * SLAB, SLAB), SLAB)

    def gather_row(fref, r, slot, zero=0):
        tok = fref[0, 0, r + zero] & tok_mask
        return pltpu.make_async_copy(h_hbm.at[slab(tok)], xbuf.at[slot, slab(r)], gsem.at[slot])

    def scatter_row(fref, r, slot, zero=0):
        return pltpu.make_async_copy(stage.at[slot, slab(r)], y_hbm.at[slab(fref[0, 0, r + zero])],
                                     ssem.at[slot])

    def gather_all(slot):
        return pltpu.make_async_copy(h_hbm.at[pl.ds(0, tile_rows)], xbuf.at[slot], gsem.at[slot])

    def scatter_all(slot):
        return pltpu.make_async_copy(stage.at[slot], y_hbm.at[pl.ds(0, tile_rows)], ssem.at[slot])

    cur = i % 2
    used = i < used_tiles

    @pl.when(i == 0)
    def _():
        stage[...] = jnp.zeros_like(stage)
        spare = y_hbm.shape[0] - 2 * tile_rows
        fills = [pltpu.make_async_copy(stage.at[sl], y_hbm.at[pl.ds(spare + sl * tile_rows, tile_rows)],
                                       ssem.at[sl])
                 for sl in range(2)]
        for cp in fills:
            cp.start()
        for cp in fills:
            cp.wait()

    @pl.when((i == 0) & used)
    def _():
        def prime(r, c):
            gather_row(f0_ref, r, 0).start()
            return c
        lax.fori_loop(0, tm, prime, 0)

    expert = te_ref[i]
    new_expert = used & ((i == 0) | (expert != te_ref[jnp.maximum(i - 1, 0)]))

    has_next = i + 1 < used_tiles
    has_prev = (i >= 1) & (i - 1 < used_tiles)
    steady = (i >= 2) & has_next

    @pl.when(used)
    def _():
        gather_all(cur).wait()
        xb[...] = _slabs_to_rows(xbuf.at[cur], tm).astype(BF16)

    @pl.when((i >= 2) & (i - 2 < used_tiles))
    def _():
        scatter_all(cur).wait()

    half = f_dim // MOE_SPLIT
    n_piece = half // W_CHUNK
    n_groups = MOE_SPLIT * (n_piece + 1)

    def tick(v):
        bits = jnp.max(lax.bitcast_convert_type(v[0:SLAB, 0:LANES], jnp.int32))
        return lax.shift_right_logical(lax.shift_right_logical(bits, 16), 16)

    def expert_ffn(issue_group):
        xv = xb[...]
        out = None
        issue_group(0, 0)
        k = 1
        for hf in range(MOE_SPLIT):
            for c in range(n_piece):
                cs = slice(hf * half + c * W_CHUNK, hf * half + (c + 1) * W_CHUNK)
                g = jnp.dot(xv, wg_res[:, cs], preferred_element_type=F32)
                up = jnp.dot(xv, wu_res[:, cs], preferred_element_type=F32)
                act[:, c * W_CHUNK:(c + 1) * W_CHUNK] = (g * _sigmoid(g) * up).astype(BF16)
                issue_group(k, tick(g))
                k += 1
            part = jnp.dot(act[...], wd_res[hf * half:(hf + 1) * half, :], preferred_element_type=F32)
            if hf < MOE_SPLIT - 1:
                acc[...] = part if out is None else acc[...] + part
                out = acc
                issue_group(k, tick(part))
                k += 1
            else:
                _rows_to_slabs(part if out is None else acc[...] + part, stage.at[cur])

    def chunk_plan():
        plan = [(w, res, stg_in, 0, W_ROWS_IN, c)
                for w, res in ((wg_hbm, wg_res), (wu_hbm, wu_res)) for c in range(D // W_ROWS_IN)]
        return plan + [(wd_hbm, wd_res, stg_out, W_RING, W_ROWS_OUT, c) for c in range(f_dim // W_ROWS_OUT)]

    def chunk_copy(plan, k, ex):
        w_hbm, _, stg, sem0, rows, c = plan[k]
        buf = k % W_RING
        return pltpu.make_async_copy(w_hbm.at[ex, pl.ds(c * rows, rows), :], stg.at[buf], wsem.at[sem0 + buf])

    @pl.when(new_expert)
    def _():
        plan = chunk_plan()

        @pl.when(i == 0)
        def _():
            for k in range(W_RING):
                chunk_copy(plan, k, expert).start()

        for k in range(len(plan)):
            _, res, stg, _, rows, c = plan[k]
            chunk_copy(plan, k, expert).wait()
            res[c * rows:(c + 1) * rows, :] = stg[k % W_RING].astype(BF16)
            if k + W_RING < len(plan):
                chunk_copy(plan, k + W_RING, expert).start()

    @pl.when(steady)
    def _():
        def issue_group(k, zero):
            for r in range(k * tm // n_groups, (k + 1) * tm // n_groups):
                gather_row(fnext_ref, r, 1 - cur, zero).start()
                scatter_row(fprev_ref, r, 1 - cur, zero).start()
        expert_ffn(issue_group)

    @pl.when(jnp.logical_not(steady))
    def _():
        @pl.when(used)
        def _():
            expert_ffn(lambda k, zero: None)

        @pl.when(has_next)
        def _():
            def issue(r, c):
                gather_row(fnext_ref, r, 1 - cur).start()
                return c
            lax.fori_loop(0, tm, issue, 0)

        @pl.when(has_prev)
        def _():
            def issue(r, c):
                scatter_row(fprev_ref, r, 1 - cur).start()
                return c
            lax.fori_loop(0, tm, issue, 0)

    next_expert = te_ref[jnp.minimum(i + 1, nt - 1)]

    @pl.when(has_next & (next_expert != expert))
    def _():
        plan = chunk_plan()
        for k in range(W_RING):
            chunk_copy(plan, k, next_expert).start()

    @pl.when((i == nt - 1) & (nt - 2 < used_tiles))
    def _():
        scatter_all(1 - cur).wait()


def _moe_experts(h_slabs, fmap, tile_expert, n_used, wg, wu, wd, n_tiles, y_rows):
    f = wg.shape[2]
    tm = TM_MOE
    fblk = lambda imap: pl.BlockSpec((1, 1, tm), imap, memory_space=pltpu.SMEM)
    hbm = pl.BlockSpec(memory_space=pl.ANY)
    return pl.pallas_call(
        _moe_kernel,
        grid_spec=pltpu.PrefetchScalarGridSpec(
            num_scalar_prefetch=2,
            grid=(n_tiles,),
            in_specs=[
                fblk(lambda i, te, nu: (jnp.minimum(i + 1, n_tiles - 1), 0, 0)),
                fblk(lambda i, te, nu: (jnp.maximum(i - 1, 0), 0, 0)),
                fblk(lambda i, te, nu: (0, 0, 0)),
                hbm, hbm, hbm, hbm,
            ],
            out_specs=hbm,
            scratch_shapes=[pltpu.VMEM((D, f), BF16), pltpu.VMEM((D, f), BF16), pltpu.VMEM((f, D), BF16),
                            pltpu.VMEM((W_RING, W_ROWS_IN, f), F32), pltpu.VMEM((W_RING, W_ROWS_OUT, D), F32),
                            pltpu.VMEM((2, tm * SLAB, LANES), F32), pltpu.VMEM((tm, D), BF16),
                            pltpu.VMEM((tm, f // MOE_SPLIT), BF16), pltpu.VMEM((tm, D), F32),
                            pltpu.VMEM((2, tm * SLAB, LANES), F32),
                            pltpu.SemaphoreType.DMA((2,)), pltpu.SemaphoreType.DMA((2,)),
                            pltpu.SemaphoreType.DMA((2 * W_RING,))],
        ),
        out_shape=jax.ShapeDtypeStruct((y_rows * SLAB, LANES), F32),
        compiler_params=_cparams(("arbitrary",), MOE_VMEM_LIMIT),
        name="moe_experts",
    )(tile_expert, n_used, fmap, fmap, fmap, h_slabs, wg, wu, wd)


def _combine_kernel(y1_ref, y2_ref, x_ref, w_ref, mod_ref, gain_ref, o_ref):
    w = w_ref[...]
    rows = x_ref.shape[0]
    moe = w[:, 0:1] * _slabs_to_rows(y1_ref, rows) + w[:, 1:2] * _slabs_to_rows(y2_ref, rows)
    mod = mod_ref[0, 0]
    x4 = x_ref[...] + mod[5:6] * moe
    ms = jnp.mean(x4 * x4, axis=-1, keepdims=True)
    o_ref[...] = x4 * lax.rsqrt(ms + EPS) * gain_ref[...]


def _combine(y, x3_2d, wts, mod, final_gain, seq):
    n = x3_2d.shape[0]
    tc = TC_COMB
    nt = n // tc
    per_b = seq // tc
    return pl.pallas_call(
        _combine_kernel,
        grid=(nt,),
        in_specs=[
            pl.BlockSpec((tc * SLAB, LANES), lambda i: (i, 0)),
            pl.BlockSpec((tc * SLAB, LANES), lambda i: (i + nt, 0)),
            pl.BlockSpec((tc, D), lambda i: (i, 0)),
            pl.BlockSpec((tc, 2), lambda i: (i, 0)),
            pl.BlockSpec((1, 1, 6, D), lambda i: (1, i // per_b, 0, 0)),
            pl.BlockSpec((1, D), lambda i: (0, 0)),
        ],
        out_specs=pl.BlockSpec((tc, D), lambda i: (i, 0)),
        out_shape=jax.ShapeDtypeStruct((n, D), F32),
        compiler_params=_cparams(("arbitrary",)),
        name="moe_combine",
    )(y, y, x3_2d, wts, mod, final_gain.reshape(1, D))


def _rope_tables(seq):
    rows = seq // GRID_W
    row_pos = jnp.repeat(jnp.arange(rows, dtype=F32), GRID_W)
    col_pos = jnp.tile(jnp.arange(GRID_W, dtype=F32), rows)
    axis_dim = HEAD_DIM // 2
    inv_freq = ROPE_BASE ** (-jnp.arange(0, axis_dim, 2, dtype=F32) / axis_dim)
    ar = row_pos[:, None] * inv_freq
    ac = col_pos[:, None] * inv_freq
    cos64 = jnp.concatenate([jnp.cos(ar), jnp.cos(ar), jnp.cos(ac), jnp.cos(ac)], axis=1)
    sin64 = jnp.concatenate([-jnp.sin(ar), jnp.sin(ar), -jnp.sin(ac), jnp.sin(ac)], axis=1)
    return jnp.tile(cos64, (1, 2)), jnp.tile(sin64, (1, 2))


def _band_matrices():
    r = np.arange(BLOCK)[:, None]
    c = np.arange(BLOCK + 2 * POOL_HALO)[None, :] - POOL_HALO
    mats = []
    for w in POOL_SIZES:
        lo = -(w // 2)
        hi = w - 1 - w // 2
        mats.append(((c >= r + lo) & (c <= r + hi)).astype(np.float32))
    return jnp.asarray(np.stack(mats), dtype=BF16)


def kernel(x, c, ctx, c_ctx, w_ada, b_ada, w_in, attn_sink, gm_gain, gm_w_s, gm_b_s, w_out,
           ffn_w_gate, ffn_w_up, ffn_w_down, pool_w, pool_scale, router_w,
           moe_w_gate, moe_w_up, moe_w_down, final_gain):
    b, s, _ = x.shape
    n = b * s
    assert w_ada.shape[0] == 2 and w_in.shape[0] == 1 and pool_w.shape[0] == 1
    assert s % TM_IN == 0 and s % TM_POOL == 0 and s % TM_FFN == 0 and b <= 4
    assert n & (n - 1) == 0

    cvec = jnp.concatenate([c, c_ctx[None, :], jnp.zeros((8 - b - 1, D), F32)], axis=0)
    mod = _ada_mod(cvec, w_ada, b_ada)

    cos_t, sin_t = _rope_tables(s)
    w_in_bf = w_in[0].astype(BF16)
    q, k, ksw, v, vsw, u, vg = _in_proj(x, mod, w_in_bf, gm_gain[0], cos_t, sin_t)
    kx, ksx, vx, vsx = _ctx_kv(ctx, mod, w_in_bf)
    wcat = gm_w_s[0].reshape(4, 2, BLOCK, BLOCK).transpose(0, 2, 1, 3).reshape(4, BLOCK, 2 * BLOCK).astype(BF16)
    x1 = _attn_mixer(x, mod, attn_sink[0], q, k, ksw, v, vsw, kx, ksx, vx, vsx, u, vg,
                     wcat, gm_b_s[0].T, w_out[0].astype(BF16))
    x2 = _dense_ffn(x1.reshape(n, D), mod, ffn_w_gate[0].astype(BF16), ffn_w_up[0].astype(BF16),
                    ffn_w_down[0].astype(BF16), s)

    wr = jnp.pad(router_w[0], ((0, 0), (0, LANES - N_EXPERTS)))
    wr_hi = wr.astype(BF16)
    wr_lo = (wr - wr_hi.astype(F32)).astype(BF16)
    tri = jnp.asarray(np.tril(np.ones((TM_POOL, TM_POOL), np.float32), -1), dtype=BF16)
    x3, h2, route, counts = _pool_route(x2.reshape(b, s, D), mod, _band_matrices(), pool_w[0].astype(BF16),
                                        pool_scale[0], wr_hi, wr_lo, tri)

    tm = TM_MOE
    n_tiles = (2 * n) // tm + N_EXPERTS
    cnt = counts[0, :N_EXPERTS].astype(jnp.int32)
    tiles_e = (cnt + tm - 1) // tm
    tile_end = jnp.cumsum(tiles_e)
    off = (tile_end - tiles_e) * tm
    n_used = tile_end[-1]
    tix = jnp.arange(n_tiles, dtype=jnp.int32)
    te = jnp.minimum(jnp.sum(tix[:, None] >= tile_end[None, :], axis=1), N_EXPERTS - 1).astype(jnp.int32)
    te_last = te[jnp.maximum(n_used - 1, 0)]
    tile_expert = jnp.where(tix < n_used, te, te_last)
    e1 = route[0].astype(jnp.int32)
    e2 = route[1].astype(jnp.int32)
    pos1 = off[e1] + route[4].astype(jnp.int32)
    pos2 = off[e2] + route[5].astype(jnp.int32)
    n_slots = n_tiles * tm
    pad_lo = jnp.concatenate([off + cnt, (n_used * tm).reshape(1)]).astype(jnp.int32)
    pad_hi = jnp.concatenate([off + tiles_e * tm, jnp.full((1,), n_slots, jnp.int32)]).astype(jnp.int32)
    fmap = _slot_map(jnp.concatenate([pos1, pos2]), pad_lo, pad_hi, n_slots)
    n_used_arr = n_used.reshape(1).astype(jnp.int32)

    y = _moe_experts(h2, fmap.reshape(n_tiles, 1, tm), tile_expert, n_used_arr,
                     moe_w_gate[0], moe_w_up[0], moe_w_down[0], n_tiles, 2 * n + 2 * tm)
    out = _combine(y, x3.reshape(n, D), route[2:4].T, mod, final_gain, s)
    return out.reshape(b, s, D)
```

```python
import functools

import numpy as np
import jax
import jax.numpy as jnp
from jax import lax
from jax.experimental import pallas as pl
from jax.experimental.pallas import tpu as pltpu

F32 = jnp.float32
BF16 = jnp.bfloat16

D = 1024
GRID_W = 64
EPS = 1e-6
NEG_INF = -1e30
HEAD_DIM = 64
N_Q_HEADS = 8
BLOCK = 128
ATT_W = 512
KV_W = 128
GM_W = 512
IN_W = 1792
POOL_SIZES = (2, 4, 8, 16)
POOL_GD = 256
POOL_HALO = 16
N_EXPERTS = 8
ROPE_BASE = 10000.0
LANES = 128
SLAB = D // LANES
SQRT_2_OVER_PI = 0.7978845608028654

TM_IN = 1024
TM_FFN = 512
TM_POOL = 512
TM_MOE = 512
MOE_SPLIT = 2
W_CHUNK = 256
W_ROWS_IN = 128
W_ROWS_OUT = 512
W_RING = 3
MOE_VMEM_LIMIT = 60 * 1024 * 1024
TC_COMB = 512
Q_BLOCKS = 4
VMEM_LIMIT = 56 * 1024 * 1024


def _cparams(sem, vmem=None):
    return pltpu.CompilerParams(dimension_semantics=sem, vmem_limit_bytes=vmem)


def _modulate(xf, shift, scale):
    ms = jnp.mean(xf * xf, axis=-1, keepdims=True)
    return xf * lax.rsqrt(ms + EPS) * (1.0 + scale) + shift


def _sigmoid(z):
    return 1.0 / (1.0 + jnp.exp(-z))


def _rows_to_slabs(val, slab_ref):
    rows = val.shape[0]
    for cix in range(SLAB):
        slab_ref[pl.ds(cix, rows, stride=SLAB), :] = val[:, cix * LANES:(cix + 1) * LANES]


def _slabs_to_rows(slab_ref, rows):
    return jnp.concatenate([slab_ref[pl.ds(cix, rows, stride=SLAB), :] for cix in range(SLAB)], axis=1)


def _ada_kernel(c_ref, w_ref, b_ref, o_ref):
    c = c_ref[...]
    s = c * _sigmoid(c)
    o_ref[0] = jnp.dot(s.astype(BF16), w_ref[0].astype(BF16), preferred_element_type=F32) + b_ref[0]


def _ada_mod(cvec, w_ada, b_ada):
    depth, _, n6 = w_ada.shape
    tn = 1536
    out = pl.pallas_call(
        _ada_kernel,
        grid=(depth, n6 // tn),
        in_specs=[
            pl.BlockSpec((8, D), lambda l, j: (0, 0)),
            pl.BlockSpec((1, D, tn), lambda l, j: (l, 0, j)),
            pl.BlockSpec((1, 1, tn), lambda l, j: (l, 0, j)),
        ],
        out_specs=pl.BlockSpec((1, 8, tn), lambda l, j: (l, 0, j)),
        out_shape=jax.ShapeDtypeStruct((depth, 8, n6), F32),
        compiler_params=_cparams(("arbitrary", "arbitrary")),
        name="ada_mod",
    )(cvec, w_ada, b_ada.reshape(depth, 1, n6))
    return out.reshape(depth, 8, 6, D)


def _rope(t, cs, sn, first_half):
    fwd = pltpu.roll(t, LANES - 16, axis=1)
    bwd = pltpu.roll(t, 16, axis=1)
    return t * cs + jnp.where(first_half, fwd, bwd) * sn


def _inproj_kernel(x_ref, mod_ref, w_ref, gain_ref, cos_ref, sin_ref,
                   q_ref, k_ref, ksw_ref, v_ref, vsw_ref, u_ref, vg_ref):
    mod = mod_ref[0, 0]
    h = _modulate(x_ref[0], mod[0:1], mod[1:2]).astype(BF16)
    proj = jnp.dot(h, w_ref[...], preferred_element_type=F32)
    cs = cos_ref[...]
    sn = sin_ref[...]
    lane = lax.broadcasted_iota(jnp.int32, cs.shape, 1)
    first_half = (lane & 16) == 0
    for cix in range(ATT_W // LANES):
        t = proj[:, cix * LANES:(cix + 1) * LANES]
        q_ref[0, :, cix * LANES:(cix + 1) * LANES] = (
            _rope(t, cs, sn, first_half) * (HEAD_DIM ** -0.5)).astype(BF16)
    kr = _rope(proj[:, ATT_W:ATT_W + KV_W], cs, sn, first_half)
    k_ref[0] = kr.astype(BF16)
    ksw_ref[0] = pltpu.roll(kr, HEAD_DIM, axis=1).astype(BF16)
    vv = proj[:, ATT_W + KV_W:ATT_W + 2 * KV_W]
    v_ref[0] = vv.astype(BF16)
    vsw_ref[0] = pltpu.roll(vv, HEAD_DIM, axis=1).astype(BF16)
    z = proj[:, ATT_W + 2 * KV_W:]
    g = z * (0.5 * (1.0 + jnp.tanh(SQRT_2_OVER_PI * (z + 0.044715 * (z * z * z)))))
    u_ref[0] = g[:, :GM_W].astype(BF16)
    vg = g[:, GM_W:]
    ms = jnp.mean(vg * vg, axis=-1, keepdims=True)
    vg_ref[0] = (vg * lax.rsqrt(ms + EPS) * gain_ref[...]).astype(BF16)


def _in_proj(x, mod, w_in_bf, gm_gain, cos_t, sin_t):
    b, s, _ = x.shape
    tm = TM_IN
    row = lambda w: pl.BlockSpec((1, tm, w), lambda bi, i: (bi, i, 0))
    outs = pl.pallas_call(
        _inproj_kernel,
        grid=(b, s // tm),
        in_specs=[
            row(D),
            pl.BlockSpec((1, 1, 6, D), lambda bi, i: (0, bi, 0, 0)),
            pl.BlockSpec((D, IN_W), lambda bi, i: (0, 0)),
            pl.BlockSpec((1, GM_W), lambda bi, i: (0, 0)),
            pl.BlockSpec((tm, LANES), lambda bi, i: (i, 0)),
            pl.BlockSpec((tm, LANES), lambda bi, i: (i, 0)),
        ],
        out_specs=[row(ATT_W), row(KV_W), row(KV_W), row(KV_W), row(KV_W), row(GM_W), row(GM_W)],
        out_shape=[jax.ShapeDtypeStruct((b, s, w), BF16)
                   for w in (ATT_W, KV_W, KV_W, KV_W, KV_W, GM_W, GM_W)],
        compiler_params=_cparams(("arbitrary", "arbitrary"), VMEM_LIMIT),
        name="in_proj",
    )(x, mod, w_in_bf, gm_gain.reshape(1, GM_W), cos_t, sin_t)
    return outs


def _ctx_kernel(c_ref, mod_ref, w_ref, k_ref, ksw_ref, v_ref, vsw_ref):
    mod = mod_ref[0, 0]
    h = _modulate(c_ref[0], mod[0:1], mod[1:2]).astype(BF16)
    kv = jnp.dot(h, w_ref[...], preferred_element_type=F32)
    kk = kv[:, :KV_W]
    vv = kv[:, KV_W:]
    k_ref[0] = kk.astype(BF16)
    ksw_ref[0] = pltpu.roll(kk, HEAD_DIM, axis=1).astype(BF16)
    v_ref[0] = vv.astype(BF16)
    vsw_ref[0] = pltpu.roll(vv, HEAD_DIM, axis=1).astype(BF16)


def _ctx_kv(ctx, mod, w_in_bf):
    b, l, _ = ctx.shape
    spec = pl.BlockSpec((1, l, KV_W), lambda bi: (bi, 0, 0))
    return pl.pallas_call(
        _ctx_kernel,
        grid=(b,),
        in_specs=[
            pl.BlockSpec((1, l, D), lambda bi: (bi, 0, 0)),
            pl.BlockSpec((1, 1, 6, D), lambda bi: (0, b, 0, 0)),
            pl.BlockSpec((D, 2 * KV_W), lambda bi: (0, ATT_W // (2 * KV_W))),
        ],
        out_specs=[spec] * 4,
        out_shape=[jax.ShapeDtypeStruct((b, l, KV_W), BF16)] * 4,
        compiler_params=_cparams(("arbitrary",)),
        name="ctx_kv",
    )(ctx, mod, w_in_bf)


def _attn_kernel(sink_ref, q_ref, kp_ref, kc_ref, kn_ref, ksp_ref, ksc_ref, ksn_ref,
                 vp_ref, vc_ref, vn_ref, vsp_ref, vsc_ref, vsn_ref,
                 kx_ref, ksx_ref, vx_ref, vsx_ref,
                 u_ref, vg_ref, wcat_ref, bs_ref, wout_ref, x_ref, mod_ref, o_ref):
    n = pl.program_id(1)
    nblk = pl.num_programs(1) * Q_BLOCKS
    lane = lax.broadcasted_iota(jnp.int32, (1, LANES), 1)
    low = lane < HEAD_DIM
    zero = jnp.zeros((), BF16)

    def variants(a0, a1):
        return ((jnp.where(low, a0, zero), jnp.where(low, zero, a1)),
                (jnp.where(low, a1, zero), jnp.where(low, zero, a0)))

    cat = lambda refs: jnp.concatenate([r[0] for r in refs], axis=0)
    kb_var = variants(cat((kp_ref, kc_ref, kn_ref)), cat((ksp_ref, ksc_ref, ksn_ref)))
    vb_var = variants(cat((vp_ref, vc_ref, vn_ref)), cat((vsp_ref, vsc_ref, vsn_ref)))
    kx_var = variants(kx_ref[0], ksx_ref[0])
    vx_var = variants(vx_ref[0], vsx_ref[0])

    row = lax.broadcasted_iota(jnp.int32, (2 * BLOCK, BLOCK), 0) & (BLOCK - 1)
    col = lax.broadcasted_iota(jnp.int32, (2 * BLOCK, BLOCK), 1)
    top = lax.broadcasted_iota(jnp.int32, (2 * BLOCK, 1), 0) < BLOCK
    nt_dims = (((1,), (1,)), ((), ()))

    q = q_ref[0]
    att_blocks = [[None] * 4 for _ in range(Q_BLOCKS)]
    for kvh in range(2):
        qst = jnp.concatenate(
            [q[qb * BLOCK:(qb + 1) * BLOCK, pr * LANES:(pr + 1) * LANES]
             for qb in range(Q_BLOCKS) for pr in (2 * kvh, 2 * kvh + 1)], axis=0)
        accs = [None] * Q_BLOCKS
        for half in range(2):
            sk = jnp.where(top, sink_ref[4 * kvh + half], sink_ref[4 * kvh + 2 + half])
            s_ctx = lax.dot_general(qst, kx_var[kvh][half], nt_dims, preferred_element_type=F32)
            for qb in range(Q_BLOCKS):
                g = n * Q_BLOCKS + qb
                qrows = qst[qb * 2 * BLOCK:(qb + 1) * 2 * BLOCK]
                sb = lax.dot_general(qrows, kb_var[kvh][half][qb * BLOCK:(qb + 3) * BLOCK], nt_dims,
                                     preferred_element_type=F32)
                s0 = jnp.where((col >= row) & (g > 0), sb[:, :BLOCK], NEG_INF)
                s1 = sb[:, BLOCK:2 * BLOCK]
                s2 = jnp.where((col <= row) & (g < nblk - 1), sb[:, 2 * BLOCK:], NEG_INF)
                sc = s_ctx[qb * 2 * BLOCK:(qb + 1) * 2 * BLOCK]
                ctx_blocks = [sc[:, cb * BLOCK:(cb + 1) * BLOCK] for cb in range(sc.shape[1] // BLOCK)]
                m = functools.reduce(jnp.maximum, [s0, s1, s2] + ctx_blocks)
                m = jnp.maximum(jnp.max(m, axis=-1, keepdims=True), sk)
                p0, p1, p2, pc = (jnp.exp(t - m) for t in (s0, s1, s2, sc))
                psum = functools.reduce(
                    jnp.add, [p0, p1, p2] + [pc[:, cb * BLOCK:(cb + 1) * BLOCK] for cb in range(len(ctx_blocks))])
                den = jnp.sum(psum, axis=-1, keepdims=True) + jnp.exp(sk - m)
                pb = jnp.concatenate([p0, p1, p2], axis=1).astype(BF16)
                o = (jnp.dot(pb, vb_var[kvh][half][qb * BLOCK:(qb + 3) * BLOCK], preferred_element_type=F32)
                     + jnp.dot(pc.astype(BF16), vx_var[kvh][half], preferred_element_type=F32))
                o = o / den
                accs[qb] = o if accs[qb] is None else accs[qb] + o
        for qb in range(Q_BLOCKS):
            att_blocks[qb][2 * kvh] = accs[qb][:BLOCK]
            att_blocks[qb][2 * kvh + 1] = accs[qb][BLOCK:]

    u = u_ref[0]
    vg = vg_ref[0]
    bs = bs_ref[...]
    gm_blocks = [[None] * 4 for _ in range(Q_BLOCKS)]
    for j in range(GM_W // LANES):
        chunks = [vg[c * BLOCK:(c + 1) * BLOCK, j * LANES:(j + 1) * LANES] for c in range(Q_BLOCKS)]
        rhs = jnp.concatenate(
            [jnp.concatenate([jnp.where(low, v, zero) for v in chunks], axis=1),
             jnp.concatenate([jnp.where(low, zero, v) for v in chunks], axis=1)], axis=0)
        mixed = jnp.dot(wcat_ref[j], rhs, preferred_element_type=F32)
        bias = jnp.where(low, bs[:, 2 * j:2 * j + 1], bs[:, 2 * j + 1:2 * j + 2])
        for c in range(Q_BLOCKS):
            gm_blocks[c][j] = (u[c * BLOCK:(c + 1) * BLOCK, j * LANES:(j + 1) * LANES].astype(F32)
                               * (mixed[:, c * LANES:(c + 1) * LANES] + bias))

    mix = jnp.concatenate([jnp.concatenate(att_blocks[c] + gm_blocks[c], axis=1) for c in range(Q_BLOCKS)],
                          axis=0).astype(BF16)
    y = jnp.dot(mix, wout_ref[...], preferred_element_type=F32)
    mod = mod_ref[0, 0]
    o_ref[0] = x_ref[0] + mod[2:3] * y


def _attn_mixer(x, mod, sink, q, k, ksw, v, vsw, kx, ksx, vx, vsx, u, vg, wcat_bf, bs_t, wout_bf):
    b, s, _ = x.shape
    tq = Q_BLOCKS * BLOCK
    nb = s // BLOCK
    l = kx.shape[1]
    cur = lambda w: pl.BlockSpec((1, tq, w), lambda bi, n: (bi, n, 0))
    prv = lambda w: pl.BlockSpec((1, BLOCK, w), lambda bi, n: (bi, jnp.maximum(n * Q_BLOCKS - 1, 0), 0))
    nxt = lambda w: pl.BlockSpec((1, BLOCK, w), lambda bi, n: (bi, jnp.minimum((n + 1) * Q_BLOCKS, nb - 1), 0))
    cx = pl.BlockSpec((1, l, KV_W), lambda bi, n: (bi, 0, 0))
    return pl.pallas_call(
        _attn_kernel,
        grid=(b, s // tq),
        in_specs=[
            pl.BlockSpec(memory_space=pltpu.SMEM),
            cur(ATT_W),
            prv(KV_W), cur(KV_W), nxt(KV_W), prv(KV_W), cur(KV_W), nxt(KV_W),
            prv(KV_W), cur(KV_W), nxt(KV_W), prv(KV_W), cur(KV_W), nxt(KV_W),
            cx, cx, cx, cx,
            cur(GM_W), cur(GM_W),
            pl.BlockSpec((4, BLOCK, 2 * BLOCK), lambda bi, n: (0, 0, 0)),
            pl.BlockSpec((BLOCK, 8), lambda bi, n: (0, 0)),
            pl.BlockSpec((D, D), lambda bi, n: (0, 0)),
            cur(D),
            pl.BlockSpec((1, 1, 6, D), lambda bi, n: (0, bi, 0, 0)),
        ],
        out_specs=cur(D),
        out_shape=jax.ShapeDtypeStruct((b, s, D), F32),
        compiler_params=_cparams(("arbitrary", "arbitrary"), VMEM_LIMIT),
        name="attn_gmlp_out",
    )(sink, q, k, k, k, ksw, ksw, ksw, v, v, v, vsw, vsw, vsw, kx, ksx, vx, vsx,
      u, vg, wcat_bf, bs_t, wout_bf, x, mod)


def _ffn_kernel(x_ref, mod_ref, wg_ref, wu_ref, wd_ref, o_ref):
    mod = mod_ref[0, 0]
    xf = x_ref[...]
    h = _modulate(xf, mod[3:4], mod[4:5]).astype(BF16)
    g = jnp.dot(h, wg_ref[...], preferred_element_type=F32)
    up = jnp.dot(h, wu_ref[...], preferred_element_type=F32)
    a = (g * _sigmoid(g) * up).astype(BF16)
    o_ref[...] = xf + mod[5:6] * jnp.dot(a, wd_ref[...], preferred_element_type=F32)


def _dense_ffn(x2d, mod, wg, wu, wd, seq):
    n = x2d.shape[0]
    f = wg.shape[1]
    tm = TM_FFN
    per_b = seq // tm
    resident = lambda shp: pl.BlockSpec(shp, lambda i: (0, 0), pipeline_mode=pl.Buffered(1))
    return pl.pallas_call(
        _ffn_kernel,
        grid=(n // tm,),
        in_specs=[
            pl.BlockSpec((tm, D), lambda i: (i, 0)),
            pl.BlockSpec((1, 1, 6, D), lambda i: (0, i // per_b, 0, 0)),
            resident((D, f)), resident((D, f)), resident((f, D)),
        ],
        out_specs=pl.BlockSpec((tm, D), lambda i: (i, 0)),
        out_shape=jax.ShapeDtypeStruct((n, D), F32),
        compiler_params=_cparams(("arbitrary",), VMEM_LIMIT),
        name="dense_ffn",
    )(x2d, mod, wg, wu, wd)


def _pool_route_kernel(x_ref, xp_ref, xn_ref, mod_ref, band_ref, pw_ref, psc_ref, wr_hi_ref, wr_lo_ref,
                       tri_ref, x3_ref, h2_ref, route_ref, cnt_ref, hext, carry):
    bi = pl.program_id(0)
    i = pl.program_id(1)
    ni = pl.num_programs(1)
    tm = x_ref.shape[1]
    seq = tm * ni
    mod = mod_ref[0, 0]

    @pl.when((bi == 0) & (i == 0))
    def _():
        carry[...] = jnp.zeros_like(carry)

    xf = x_ref[0]
    hp = _modulate(xp_ref[0], mod[0:1], mod[1:2])
    hn = _modulate(xn_ref[0], mod[0:1], mod[1:2])
    hext[0:POOL_HALO] = jnp.where(i > 0, hp, 0.0).astype(BF16)
    h_main = _modulate(xf, mod[0:1], mod[1:2])
    hext[POOL_HALO:POOL_HALO + tm] = h_main.astype(BF16)
    hext[POOL_HALO + tm:] = jnp.where(i < ni - 1, hn, 0.0).astype(BF16)

    t_local = lax.broadcasted_iota(jnp.int32, (BLOCK, 1), 0)
    ys = []
    for gi, w in enumerate(POOL_SIZES):
        lo_off = -(w // 2)
        hi_off = w - 1 - w // 2
        cols = slice(gi * POOL_GD, (gi + 1) * POOL_GD)
        outs = []
        for sb in range(tm // BLOCK):
            r0 = sb * BLOCK
            win = jnp.dot(band_ref[gi], hext[r0:r0 + BLOCK + 2 * POOL_HALO, cols],
                          preferred_element_type=F32)
            t = i * tm + r0 + t_local
            cnt = (jnp.minimum(t + hi_off, seq - 1) - jnp.maximum(t + lo_off, 0) + 1).astype(F32)
            diff = win / cnt - h_main[r0:r0 + BLOCK, cols]
            outs.append(diff.astype(BF16))
        dg = jnp.concatenate(outs, axis=0)
        ys.append(jnp.dot(dg, pw_ref[gi], preferred_element_type=F32))
    y = jnp.concatenate(ys, axis=1) * psc_ref[...]
    x3 = xf + mod[2:3] * y
    x3_ref[0] = x3

    h2 = _modulate(x3, mod[3:4], mod[4:5])
    _rows_to_slabs(h2, h2_ref)
    h_hi = h2.astype(BF16)
    h_lo = (h2 - h_hi.astype(F32)).astype(BF16)
    logits = (jnp.dot(h_hi, wr_hi_ref[...], preferred_element_type=F32)
              + jnp.dot(h_hi, wr_lo_ref[...], preferred_element_type=F32)
              + jnp.dot(h_lo, wr_hi_ref[...], preferred_element_type=F32))
    lane = lax.broadcasted_iota(jnp.int32, (tm, LANES), 1)
    lane_f = lane.astype(F32)
    neg = -jnp.inf
    lg = jnp.where(lane < N_EXPERTS, logits, neg)
    m1 = jnp.max(lg, axis=-1, keepdims=True)
    i1 = jnp.min(jnp.where(lg == m1, lane_f, float(LANES)), axis=-1, keepdims=True)
    oh1 = lane_f == i1
    lg2 = jnp.where(oh1, neg, lg)
    m2 = jnp.max(lg2, axis=-1, keepdims=True)
    i2 = jnp.min(jnp.where(lg2 == m2, lane_f, float(LANES)), axis=-1, keepdims=True)
    oh2 = lane_f == i2
    e = jnp.exp(m2 - m1)
    w1 = 1.0 / (1.0 + e)
    w2 = e / (1.0 + e)
    oh = jnp.where(oh1 | oh2, 1.0, 0.0)
    before = jnp.dot(tri_ref[...], oh.astype(BF16), preferred_element_type=F32) + carry[...]
    r1 = jnp.sum(jnp.where(oh1, before, 0.0), axis=-1, keepdims=True)
    r2 = jnp.sum(jnp.where(oh2, before, 0.0), axis=-1, keepdims=True)
    carry[...] = carry[...] + jnp.sum(oh, axis=0, keepdims=True)
    cnt_ref[...] = carry[...]
    info = jnp.where(lane == 0, i1, jnp.where(lane == 1, i2, jnp.where(lane == 2, w1, jnp.where(
        lane == 3, w2, jnp.where(lane == 4, r1, jnp.where(lane == 5, r2, 0.0))))))
    route_ref[...] = info.T[0:8, :]


def _pool_route(x, mod, band, pw_bf, pool_scale, wr_hi, wr_lo, tri):
    b, s, _ = x.shape
    tm = TM_POOL
    ni = s // tm
    hb = tm // POOL_HALO
    row = pl.BlockSpec((1, tm, D), lambda bi, i: (bi, i, 0))
    const2 = lambda shp: pl.BlockSpec(shp, lambda bi, i: (0,) * len(shp))
    return pl.pallas_call(
        _pool_route_kernel,
        grid=(b, ni),
        in_specs=[
            row,
            pl.BlockSpec((1, POOL_HALO, D), lambda bi, i: (bi, jnp.maximum(i * hb - 1, 0), 0)),
            pl.BlockSpec((1, POOL_HALO, D), lambda bi, i: (bi, jnp.minimum((i + 1) * hb, s // POOL_HALO - 1), 0)),
            pl.BlockSpec((1, 1, 6, D), lambda bi, i: (1, bi, 0, 0)),
            const2(band.shape), const2(pw_bf.shape), const2((1, D)),
            const2(wr_hi.shape), const2(wr_lo.shape), const2(tri.shape),
        ],
        out_specs=[row,
                   pl.BlockSpec((tm * SLAB, LANES), lambda bi, i: (bi * ni + i, 0)),
                   pl.BlockSpec((8, tm), lambda bi, i: (0, bi * ni + i)),
                   pl.BlockSpec((1, LANES), lambda bi, i: (0, 0))],
        out_shape=[jax.ShapeDtypeStruct((b, s, D), F32), jax.ShapeDtypeStruct((b * s * SLAB, LANES), F32),
                   jax.ShapeDtypeStruct((8, b * s), F32), jax.ShapeDtypeStruct((1, LANES), F32)],
        scratch_shapes=[pltpu.VMEM((tm + 2 * POOL_HALO, D), BF16), pltpu.VMEM((1, LANES), F32)],
        compiler_params=_cparams(("arbitrary", "arbitrary"), VMEM_LIMIT),
        name="pool_route",
    )(x, x, x, mod, band, pw_bf, pool_scale.reshape(1, D), wr_hi, wr_lo, tri)


def _slotmap_kernel(pos_ref, lo_ref, hi_ref, o_ref):
    n_pairs = pos_ref.shape[0]
    spare_mask = 2 * TM_MOE - 1
    for e in range(lo_ref.shape[0]):
        def fill(p, c):
            o_ref[p] = n_pairs + (p & spare_mask)
            return c
        lax.fori_loop(lo_ref[e], hi_ref[e], fill, 0)

    def place(f, c):
        o_ref[pos_ref[f]] = f
        return c
    lax.fori_loop(0, n_pairs, place, 0, unroll=32)


def _slot_map(pos_flat, lo, hi, n_slots):
    smem = pl.BlockSpec(memory_space=pltpu.SMEM)
    return pl.pallas_call(
        _slotmap_kernel,
        in_specs=[smem, smem, smem],
        out_specs=smem,
        out_shape=jax.ShapeDtypeStruct((n_slots,), jnp.int32),
        name="moe_slot_map",
    )(pos_flat, lo, hi)


def _moe_kernel(te_ref, nused_ref, fnext_ref, fprev_ref, f0_ref, h_hbm, wg_hbm, wu_hbm, wd_hbm,
                y_hbm, wg_res, wu_res, wd_res, stg_in, stg_out, xbuf, xb, act, acc, stage, gsem, ssem, wsem):
    i = pl.program_id(0)
    nt = pl.num_programs(0)
    used_tiles = nused_ref[0]
    tm = xb.shape[0]
    f_dim = wg_res.shape[1]
    tok_mask = h_hbm.shape[0] // SLAB - 1
    tile_rows = tm * SLAB

    def slab(ix):
        return pl.ds(pl.multiple_of(ix * SLAB, SLAB), SLAB)

    def gather_row(fref, r, slot, zero=0):
        tok = fref[0, 0, r + zero] & tok_mask
        return pltpu.make_async_copy(h_hbm.at[slab(tok)], xbuf.at[slot, slab(r)], gsem.at[slot])

    def scatter_row(fref, r, slot, zero=0):
        return pltpu.make_async_copy(stage.at[slot, slab(r)], y_hbm.at[slab(fref[0, 0, r + zero])],
                                     ssem.at[slot])

    def gather_all(slot):
        return pltpu.make_async_copy(h_hbm.at[pl.ds(0, tile_rows)], xbuf.at[slot], gsem.at[slot])

    def scatter_all(slot):
        return pltpu.make_async_copy(stage.at[slot], y_hbm.at[pl.ds(0, tile_rows)], ssem.at[slot])

    cur = i % 2
    used = i < used_tiles

    @pl.when(i == 0)
    def _():
        stage[...] = jnp.zeros_like(stage)
        spare = y_hbm.shape[0] - 2 * tile_rows
        fills = [pltpu.make_async_copy(stage.at[sl], y_hbm.at[pl.ds(spare + sl * tile_rows, tile_rows)],
                                       ssem.at[sl])
                 for sl in range(2)]
        for cp in fills:
            cp.start()
        for cp in fills:
            cp.wait()

    @pl.when((i == 0) & used)
    def _():
        def prime(r, c):
            gather_row(f0_ref, r, 0).start()
            return c
        lax.fori_loop(0, tm, prime, 0)

    expert = te_ref[i]
    new_expert = used & ((i == 0) | (expert != te_ref[jnp.maximum(i - 1, 0)]))

    has_next = i + 1 < used_tiles
    has_prev = (i >= 1) & (i - 1 < used_tiles)
    steady = (i >= 2) & has_next

    @pl.when(used)
    def _():
        gather_all(cur).wait()

    @pl.when((i >= 2) & (i - 2 < used_tiles))
    def _():
        scatter_all(cur).wait()

    half = f_dim // MOE_SPLIT
    n_piece = half // W_CHUNK
    n_groups = MOE_SPLIT * (n_piece + 1)

    def tick(v):
        bits = jnp.max(lax.bitcast_convert_type(v[0:SLAB, 0:LANES], jnp.int32))
        return lax.shift_right_logical(lax.shift_right_logical(bits, 16), 16)

    def expert_ffn(issue_group):
        for cix in range(SLAB):
            xb[:, cix * LANES:(cix + 1) * LANES] = xbuf.at[cur][pl.ds(cix, tm, stride=SLAB), :].astype(BF16)
        xv = xb[...]
        out = None
        issue_group(0, 0)
        k = 1
        for hf in range(MOE_SPLIT):
            for c in range(n_piece):
                cs = slice(hf * half + c * W_CHUNK, hf * half + (c + 1) * W_CHUNK)
                g = jnp.dot(xv, wg_res[:, cs], preferred_element_type=F32)
                up = jnp.dot(xv, wu_res[:, cs], preferred_element_type=F32)
                act[:, c * W_CHUNK:(c + 1) * W_CHUNK] = (g * _sigmoid(g) * up).astype(BF16)
                issue_group(k, tick(g))
                k += 1
            part = jnp.dot(act[...], wd_res[hf * half:(hf + 1) * half, :], preferred_element_type=F32)
            if hf < MOE_SPLIT - 1:
                acc[...] = part if out is None else acc[...] + part
                out = acc
                issue_group(k, tick(part))
                k += 1
            else:
                _rows_to_slabs(part if out is None else acc[...] + part, stage.at[cur])

    def chunk_plan():
        plan = [(w, res, stg_in, 0, W_ROWS_IN, c)
                for w, res in ((wg_hbm, wg_res), (wu_hbm, wu_res)) for c in range(D // W_ROWS_IN)]
        return plan + [(wd_hbm, wd_res, stg_out, W_RING, W_ROWS_OUT, c) for c in range(f_dim // W_ROWS_OUT)]

    def chunk_copy(plan, k, ex):
        w_hbm, _, stg, sem0, rows, c = plan[k]
        buf = k % W_RING
        return pltpu.make_async_copy(w_hbm.at[ex, pl.ds(c * rows, rows), :], stg.at[buf], wsem.at[sem0 + buf])

    @pl.when(new_expert)
    def _():
        plan = chunk_plan()

        @pl.when(i == 0)
        def _():
            for k in range(W_RING):
                chunk_copy(plan, k, expert).start()

        for k in range(len(plan)):
            _, res, stg, _, rows, c = plan[k]
            chunk_copy(plan, k, expert).wait()
            res[c * rows:(c + 1) * rows, :] = stg[k % W_RING].astype(BF16)
            if k + W_RING < len(plan):
                chunk_copy(plan, k + W_RING, expert).start()

    @pl.when(steady)
    def _():
        def issue_group(k, zero):
            for r in range(k * tm // n_groups, (k + 1) * tm // n_groups):
                gather_row(fnext_ref, r, 1 - cur, zero).start()
                scatter_row(fprev_ref, r, 1 - cur, zero).start()
        expert_ffn(issue_group)

    @pl.when(jnp.logical_not(steady))
    def _():
        @pl.when(used)
        def _():
            expert_ffn(lambda k, zero: None)

        @pl.when(has_next)
        def _():
            def issue(r, c):
                gather_row(fnext_ref, r, 1 - cur).start()
                return c
            lax.fori_loop(0, tm, issue, 0)

        @pl.when(has_prev)
        def _():
            def issue(r, c):
                scatter_row(fprev_ref, r, 1 - cur).start()
                return c
            lax.fori_loop(0, tm, issue, 0)

    next_expert = te_ref[jnp.minimum(i + 1, nt - 1)]

    @pl.when(has_next & (next_expert != expert))
    def _():
        plan = chunk_plan()
        for k in range(W_RING):
            chunk_copy(plan, k, next_expert).start()

    @pl.when((i == nt - 1) & (nt - 2 < used_tiles))
    def _():
        scatter_all(1 - cur).wait()


def _moe_experts(h_slabs, fmap, tile_expert, n_used, wg, wu, wd, n_tiles, y_rows):
    f = wg.shape[2]
    tm = TM_MOE
    fblk = lambda imap: pl.BlockSpec((1, 1, tm), imap, memory_space=pltpu.SMEM)
    hbm = pl.BlockSpec(memory_space=pl.ANY)
    return pl.pallas_call(
        _moe_kernel,
        grid_spec=pltpu.PrefetchScalarGridSpec(
            num_scalar_prefetch=2,
            grid=(n_tiles,),
            in_specs=[
                fblk(lambda i, te, nu: (jnp.minimum(i + 1, n_tiles - 1), 0, 0)),
                fblk(lambda i, te, nu: (jnp.maximum(i - 1, 0), 0, 0)),
                fblk(lambda i, te, nu: (0, 0, 0)),
                hbm, hbm, hbm, hbm,
            ],
            out_specs=hbm,
            scratch_shapes=[pltpu.VMEM((D, f), BF16), pltpu.VMEM((D, f), BF16), pltpu.VMEM((f, D), BF16),
                            pltpu.VMEM((W_RING, W_ROWS_IN, f), F32), pltpu.VMEM((W_RING, W_ROWS_OUT, D), F32),
                            pltpu.VMEM((2, tm * SLAB, LANES), F32), pltpu.VMEM((tm, D), BF16),
                            pltpu.VMEM((tm, f // MOE_SPLIT), BF16), pltpu.VMEM((tm, D), F32),
                            pltpu.VMEM((2, tm * SLAB, LANES), F32),
                            pltpu.SemaphoreType.DMA((2,)), pltpu.SemaphoreType.DMA((2,)),
                            pltpu.SemaphoreType.DMA((2 * W_RING,))],
        ),
        out_shape=jax.ShapeDtypeStruct((y_rows * SLAB, LANES), F32),
        compiler_params=_cparams(("arbitrary",), MOE_VMEM_LIMIT),
        name="moe_experts",
    )(tile_expert, n_used, fmap, fmap, fmap, h_slabs, wg, wu, wd)


def _combine_kernel(y1_ref, y2_ref, x_ref, w_ref, mod_ref, gain_ref, o_ref):
    w = w_ref[...]
    rows = x_ref.shape[0]
    moe = w[:, 0:1] * _slabs_to_rows(y1_ref, rows) + w[:, 1:2] * _slabs_to_rows(y2_ref, rows)
    mod = mod_ref[0, 0]
    x4 = x_ref[...] + mod[5:6] * moe
    ms = jnp.mean(x4 * x4, axis=-1, keepdims=True)
    o_ref[...] = x4 * lax.rsqrt(ms + EPS) * gain_ref[...]


def _combine(y, x3_2d, wts, mod, final_gain, seq):
    n = x3_2d.shape[0]
    tc = TC_COMB
    nt = n // tc
    per_b = seq // tc
    return pl.pallas_call(
        _combine_kernel,
        grid=(nt,),
        in_specs=[
            pl.BlockSpec((tc * SLAB, LANES), lambda i: (i, 0)),
            pl.BlockSpec((tc * SLAB, LANES), lambda i: (i + nt, 0)),
            pl.BlockSpec((tc, D), lambda i: (i, 0)),
            pl.BlockSpec((tc, 2), lambda i: (i, 0)),
            pl.BlockSpec((1, 1, 6, D), lambda i: (1, i // per_b, 0, 0)),
            pl.BlockSpec((1, D), lambda i: (0, 0)),
        ],
        out_specs=pl.BlockSpec((tc, D), lambda i: (i, 0)),
        out_shape=jax.ShapeDtypeStruct((n, D), F32),
        compiler_params=_cparams(("arbitrary",)),
        name="moe_combine",
    )(y, y, x3_2d, wts, mod, final_gain.reshape(1, D))


def _rope_tables(seq):
    rows = seq // GRID_W
    row_pos = jnp.repeat(jnp.arange(rows, dtype=F32), GRID_W)
    col_pos = jnp.tile(jnp.arange(GRID_W, dtype=F32), rows)
    axis_dim = HEAD_DIM // 2
    inv_freq = ROPE_BASE ** (-jnp.arange(0, axis_dim, 2, dtype=F32) / axis_dim)
    ar = row_pos[:, None] * inv_freq
    ac = col_pos[:, None] * inv_freq
    cos64 = jnp.concatenate([jnp.cos(ar), jnp.cos(ar), jnp.cos(ac), jnp.cos(ac)], axis=1)
    sin64 = jnp.concatenate([-jnp.sin(ar), jnp.sin(ar), -jnp.sin(ac), jnp.sin(ac)], axis=1)
    return jnp.tile(cos64, (1, 2)), jnp.tile(sin64, (1, 2))


def _band_matrices():
    r = np.arange(BLOCK)[:, None]
    c = np.arange(BLOCK + 2 * POOL_HALO)[None, :] - POOL_HALO
    mats = []
    for w in POOL_SIZES:
        lo = -(w // 2)
        hi = w - 1 - w // 2
        mats.append(((c >= r + lo) & (c <= r + hi)).astype(np.float32))
    return jnp.asarray(np.stack(mats), dtype=BF16)


def kernel(x, c, ctx, c_ctx, w_ada, b_ada, w_in, attn_sink, gm_gain, gm_w_s, gm_b_s, w_out,
           ffn_w_gate, ffn_w_up, ffn_w_down, pool_w, pool_scale, router_w,
           moe_w_gate, moe_w_up, moe_w_down, final_gain):
    b, s, _ = x.shape
    n = b * s
    assert w_ada.shape[0] == 2 and w_in.shape[0] == 1 and pool_w.shape[0] == 1
    assert s % TM_IN == 0 and s % TM_POOL == 0 and s % TM_FFN == 0 and b <= 4
    assert n & (n - 1) == 0

    cvec = jnp.concatenate([c, c_ctx[None, :], jnp.zeros((8 - b - 1, D), F32)], axis=0)
    mod = _ada_mod(cvec, w_ada, b_ada)

    cos_t, sin_t = _rope_tables(s)
    w_in_bf = w_in[0].astype(BF16)
    q, k, ksw, v, vsw, u, vg = _in_proj(x, mod, w_in_bf, gm_gain[0], cos_t, sin_t)
    kx, ksx, vx, vsx = _ctx_kv(ctx, mod, w_in_bf)
    wcat = gm_w_s[0].reshape(4, 2, BLOCK, BLOCK).transpose(0, 2, 1, 3).reshape(4, BLOCK, 2 * BLOCK).astype(BF16)
    x1 = _attn_mixer(x, mod, attn_sink[0], q, k, ksw, v, vsw, kx, ksx, vx, vsx, u, vg,
                     wcat, gm_b_s[0].T, w_out[0].astype(BF16))
    x2 = _dense_ffn(x1.reshape(n, D), mod, ffn_w_gate[0].astype(BF16), ffn_w_up[0].astype(BF16),
                    ffn_w_down[0].astype(BF16), s)

    wr = jnp.pad(router_w[0], ((0, 0), (0, LANES - N_EXPERTS)))
    wr_hi = wr.astype(BF16)
    wr_lo = (wr - wr_hi.astype(F32)).astype(BF16)
    tri = jnp.asarray(np.tril(np.ones((TM_POOL, TM_POOL), np.float32), -1), dtype=BF16)
    x3, h2, route, counts = _pool_route(x2.reshape(b, s, D), mod, _band_matrices(), pool_w[0].astype(BF16),
                                        pool_scale[0], wr_hi, wr_lo, tri)

    tm = TM_MOE
    n_tiles = (2 * n) // tm + N_EXPERTS
    cnt = counts[0, :N_EXPERTS].astype(jnp.int32)
    tiles_e = (cnt + tm - 1) // tm
    tile_end = jnp.cumsum(tiles_e)
    off = (tile_end - tiles_e) * tm
    n_used = tile_end[-1]
    tix = jnp.arange(n_tiles, dtype=jnp.int32)
    te = jnp.minimum(jnp.sum(tix[:, None] >= tile_end[None, :], axis=1), N_EXPERTS - 1).astype(jnp.int32)
    te_last = te[jnp.maximum(n_used - 1, 0)]
    tile_expert = jnp.where(tix < n_used, te, te_last)
    e1 = route[0].astype(jnp.int32)
    e2 = route[1].astype(jnp.int32)
    pos1 = off[e1] + route[4].astype(jnp.int32)
    pos2 = off[e2] + route[5].astype(jnp.int32)
    n_slots = n_tiles * tm
    pad_lo = jnp.concatenate([off + cnt, (n_used * tm).reshape(1)]).astype(jnp.int32)
    pad_hi = jnp.concatenate([off + tiles_e * tm, jnp.full((1,), n_slots, jnp.int32)]).astype(jnp.int32)
    fmap = _slot_map(jnp.concatenate([pos1, pos2]), pad_lo, pad_hi, n_slots)
    n_used_arr = n_used.reshape(1).astype(jnp.int32)

    y = _moe_experts(h2, fmap.reshape(n_tiles, 1, tm), tile_expert, n_used_arr,
                     moe_w_gate[0], moe_w_up[0], moe_w_down[0], n_tiles, 2 * n + 2 * tm)
    out = _combine(y, x3.reshape(n, D), route[2:4].T, mod, final_gain, s)
    return out.reshape(b, s, D)
```

```python
import functools

import numpy as np
import jax
import jax.numpy as jnp
from jax import lax
from jax.experimental import pallas as pl
from jax.experimental.pallas import tpu as pltpu

F32 = jnp.float32
BF16 = jnp.bfloat16

D = 1024
GRID_W = 64
EPS = 1e-6
NEG_INF = -1e30
HEAD_DIM = 64
N_Q_HEADS = 8
BLOCK = 128
ATT_W = 512
KV_W = 128
GM_W = 512
IN_W = 1792
POOL_SIZES = (2, 4, 8, 16)
POOL_GD = 256
POOL_HALO = 16
N_EXPERTS = 8
ROPE_BASE = 10000.0
LANES = 128
SLAB = D // LANES
SQRT_2_OVER_PI = 0.7978845608028654

TM_IN = 1024
TM_FFN = 512
TM_POOL = 512
TM_MOE = 512
MOE_SPLIT = 2
W_CHUNK = 256
W_ROWS_IN = 128
W_ROWS_OUT = 512
W_RING = 3
MOE_VMEM_LIMIT = 60 * 1024 * 1024
TC_COMB = 1024
Q_BLOCKS = 4
VMEM_LIMIT = 56 * 1024 * 1024


def _cparams(sem, vmem=None):
    return pltpu.CompilerParams(dimension_semantics=sem, vmem_limit_bytes=vmem)


def _modulate(xf, shift, scale):
    ms = jnp.mean(xf * xf, axis=-1, keepdims=True)
    return xf * lax.rsqrt(ms + EPS) * (1.0 + scale) + shift


def _sigmoid(z):
    return 1.0 / (1.0 + jnp.exp(-z))


def _rows_to_slabs(val, slab_ref):
    rows = val.shape[0]
    for cix in range(SLAB):
        slab_ref[pl.ds(cix, rows, stride=SLAB), :] = val[:, cix * LANES:(cix + 1) * LANES]


def _slabs_to_rows(slab_ref, rows):
    return jnp.concatenate([slab_ref[pl.ds(cix, rows, stride=SLAB), :] for cix in range(SLAB)], axis=1)


def _ada_kernel(c_ref, w_ref, b_ref, o_ref):
    c = c_ref[...]
    s = c * _sigmoid(c)
    o_ref[0] = jnp.dot(s.astype(BF16), w_ref[0].astype(BF16), preferred_element_type=F32) + b_ref[0]


def _ada_mod(cvec, w_ada, b_ada):
    depth, _, n6 = w_ada.shape
    tn = 1536
    out = pl.pallas_call(
        _ada_kernel,
        grid=(depth, n6 // tn),
        in_specs=[
            pl.BlockSpec((8, D), lambda l, j: (0, 0)),
            pl.BlockSpec((1, D, tn), lambda l, j: (l, 0, j)),
            pl.BlockSpec((1, 1, tn), lambda l, j: (l, 0, j)),
        ],
        out_specs=pl.BlockSpec((1, 8, tn), lambda l, j: (l, 0, j)),
        out_shape=jax.ShapeDtypeStruct((depth, 8, n6), F32),
        compiler_params=_cparams(("arbitrary", "arbitrary")),
        name="ada_mod",
    )(cvec, w_ada, b_ada.reshape(depth, 1, n6))
    return out.reshape(depth, 8, 6, D)


def _rope(t, cs, sn, first_half):
    fwd = pltpu.roll(t, LANES - 16, axis=1)
    bwd = pltpu.roll(t, 16, axis=1)
    return t * cs + jnp.where(first_half, fwd, bwd) * sn


def _inproj_kernel(x_ref, mod_ref, w_ref, gain_ref, cos_ref, sin_ref,
                   q_ref, k_ref, ksw_ref, v_ref, vsw_ref, u_ref, vg_ref):
    mod = mod_ref[0, 0]
    h = _modulate(x_ref[0], mod[0:1], mod[1:2]).astype(BF16)
    proj = jnp.dot(h, w_ref[...], preferred_element_type=F32)
    cs = cos_ref[...]
    sn = sin_ref[...]
    lane = lax.broadcasted_iota(jnp.int32, cs.shape, 1)
    first_half = (lane & 16) == 0
    for cix in range(ATT_W // LANES):
        t = proj[:, cix * LANES:(cix + 1) * LANES]
        q_ref[0, :, cix * LANES:(cix + 1) * LANES] = (
            _rope(t, cs, sn, first_half) * (HEAD_DIM ** -0.5)).astype(BF16)
    kr = _rope(proj[:, ATT_W:ATT_W + KV_W], cs, sn, first_half)
    k_ref[0] = kr.astype(BF16)
    ksw_ref[0] = pltpu.roll(kr, HEAD_DIM, axis=1).astype(BF16)
    vv = proj[:, ATT_W + KV_W:ATT_W + 2 * KV_W]
    v_ref[0] = vv.astype(BF16)
    vsw_ref[0] = pltpu.roll(vv, HEAD_DIM, axis=1).astype(BF16)
    z = proj[:, ATT_W + 2 * KV_W:]
    g = z * (0.5 * (1.0 + jnp.tanh(SQRT_2_OVER_PI * (z + 0.044715 * (z * z * z)))))
    u_ref[0] = g[:, :GM_W].astype(BF16)
    vg = g[:, GM_W:]
    ms = jnp.mean(vg * vg, axis=-1, keepdims=True)
    vg_ref[0] = (vg * lax.rsqrt(ms + EPS) * gain_ref[...]).astype(BF16)


def _in_proj(x, mod, w_in_bf, gm_gain, cos_t, sin_t):
    b, s, _ = x.shape
    tm = TM_IN
    row = lambda w: pl.BlockSpec((1, tm, w), lambda bi, i: (bi, i, 0))
    outs = pl.pallas_call(
        _inproj_kernel,
        grid=(b, s // tm),
        in_specs=[
            row(D),
            pl.BlockSpec((1, 1, 6, D), lambda bi, i: (0, bi, 0, 0)),
            pl.BlockSpec((D, IN_W), lambda bi, i: (0, 0)),
            pl.BlockSpec((1, GM_W), lambda bi, i: (0, 0)),
            pl.BlockSpec((tm, LANES), lambda bi, i: (i, 0)),
            pl.BlockSpec((tm, LANES), lambda bi, i: (i, 0)),
        ],
        out_specs=[row(ATT_W), row(KV_W), row(KV_W), row(KV_W), row(KV_W), row(GM_W), row(GM_W)],
        out_shape=[jax.ShapeDtypeStruct((b, s, w), BF16)
                   for w in (ATT_W, KV_W, KV_W, KV_W, KV_W, GM_W, GM_W)],
        compiler_params=_cparams(("arbitrary", "arbitrary"), VMEM_LIMIT),
        name="in_proj",
    )(x, mod, w_in_bf, gm_gain.reshape(1, GM_W), cos_t, sin_t)
    return outs


def _ctx_kernel(c_ref, mod_ref, w_ref, k_ref, ksw_ref, v_ref, vsw_ref):
    mod = mod_ref[0, 0]
    h = _modulate(c_ref[0], mod[0:1], mod[1:2]).astype(BF16)
    kv = jnp.dot(h, w_ref[...], preferred_element_type=F32)
    kk = kv[:, :KV_W]
    vv = kv[:, KV_W:]
    k_ref[0] = kk.astype(BF16)
    ksw_ref[0] = pltpu.roll(kk, HEAD_DIM, axis=1).astype(BF16)
    v_ref[0] = vv.astype(BF16)
    vsw_ref[0] = pltpu.roll(vv, HEAD_DIM, axis=1).astype(BF16)


def _ctx_kv(ctx, mod, w_in_bf):
    b, l, _ = ctx.shape
    spec = pl.BlockSpec((1, l, KV_W), lambda bi: (bi, 0, 0))
    return pl.pallas_call(
        _ctx_kernel,
        grid=(b,),
        in_specs=[
            pl.BlockSpec((1, l, D), lambda bi: (bi, 0, 0)),
            pl.BlockSpec((1, 1, 6, D), lambda bi: (0, b, 0, 0)),
            pl.BlockSpec((D, 2 * KV_W), lambda bi: (0, ATT_W // (2 * KV_W))),
        ],
        out_specs=[spec] * 4,
        out_shape=[jax.ShapeDtypeStruct((b, l, KV_W), BF16)] * 4,
        compiler_params=_cparams(("arbitrary",)),
        name="ctx_kv",
    )(ctx, mod, w_in_bf)


def _attn_kernel(sink_ref, q_ref, kp_ref, kc_ref, kn_ref, ksp_ref, ksc_ref, ksn_ref,
                 vp_ref, vc_ref, vn_ref, vsp_ref, vsc_ref, vsn_ref,
                 kx_ref, ksx_ref, vx_ref, vsx_ref,
                 u_ref, vg_ref, wcat_ref, bs_ref, wout_ref, x_ref, mod_ref, o_ref):
    n = pl.program_id(1)
    nblk = pl.num_programs(1) * Q_BLOCKS
    lane = lax.broadcasted_iota(jnp.int32, (1, LANES), 1)
    low = lane < HEAD_DIM
    zero = jnp.zeros((), BF16)

    def variants(a0, a1):
        return ((jnp.where(low, a0, zero), jnp.where(low, zero, a1)),
                (jnp.where(low, a1, zero), jnp.where(low, zero, a0)))

    cat = lambda refs: jnp.concatenate([r[0] for r in refs], axis=0)
    kb_var = variants(cat((kp_ref, kc_ref, kn_ref)), cat((ksp_ref, ksc_ref, ksn_ref)))
    vb_var = variants(cat((vp_ref, vc_ref, vn_ref)), cat((vsp_ref, vsc_ref, vsn_ref)))
    kx_var = variants(kx_ref[0], ksx_ref[0])
    vx_var = variants(vx_ref[0], vsx_ref[0])

    row = lax.broadcasted_iota(jnp.int32, (2 * BLOCK, BLOCK), 0) & (BLOCK - 1)
    col = lax.broadcasted_iota(jnp.int32, (2 * BLOCK, BLOCK), 1)
    top = lax.broadcasted_iota(jnp.int32, (2 * BLOCK, 1), 0) < BLOCK
    nt_dims = (((1,), (1,)), ((), ()))

    q = q_ref[0]
    att_blocks = [[None] * 4 for _ in range(Q_BLOCKS)]
    for kvh in range(2):
        qst = jnp.concatenate(
            [q[qb * BLOCK:(qb + 1) * BLOCK, pr * LANES:(pr + 1) * LANES]
             for qb in range(Q_BLOCKS) for pr in (2 * kvh, 2 * kvh + 1)], axis=0)
        accs = [None] * Q_BLOCKS
        for half in range(2):
            sk = jnp.where(top, sink_ref[4 * kvh + half], sink_ref[4 * kvh + 2 + half])
            s_ctx = lax.dot_general(qst, kx_var[kvh][half], nt_dims, preferred_element_type=F32)
            o_band, p_ctx, dens = [], [], []
            for qb in range(Q_BLOCKS):
                g = n * Q_BLOCKS + qb
                qrows = qst[qb * 2 * BLOCK:(qb + 1) * 2 * BLOCK]
                sb = lax.dot_general(qrows, kb_var[kvh][half][qb * BLOCK:(qb + 3) * BLOCK], nt_dims,
                                     preferred_element_type=F32)
                s0 = jnp.where((col >= row) & (g > 0), sb[:, :BLOCK], NEG_INF)
                s1 = sb[:, BLOCK:2 * BLOCK]
                s2 = jnp.where((col <= row) & (g < nblk - 1), sb[:, 2 * BLOCK:], NEG_INF)
                sc = s_ctx[qb * 2 * BLOCK:(qb + 1) * 2 * BLOCK]
                ctx_blocks = [sc[:, cb * BLOCK:(cb + 1) * BLOCK] for cb in range(sc.shape[1] // BLOCK)]
                m = functools.reduce(jnp.maximum, [s0, s1, s2] + ctx_blocks)
                m = jnp.maximum(jnp.max(m, axis=-1, keepdims=True), sk)
                p0, p1, p2, pc = (jnp.exp(t - m) for t in (s0, s1, s2, sc))
                psum = functools.reduce(
                    jnp.add, [p0, p1, p2] + [pc[:, cb * BLOCK:(cb + 1) * BLOCK] for cb in range(len(ctx_blocks))])
                den = jnp.sum(psum, axis=-1, keepdims=True) + jnp.exp(sk - m)
                pb = jnp.concatenate([p0, p1, p2], axis=1).astype(BF16)
                o_band.append(jnp.dot(pb, vb_var[kvh][half][qb * BLOCK:(qb + 3) * BLOCK],
                                      preferred_element_type=F32))
                p_ctx.append(pc.astype(BF16))
                dens.append(den)
            o_ctx = jnp.dot(jnp.concatenate(p_ctx, axis=0), vx_var[kvh][half], preferred_element_type=F32)
            for qb in range(Q_BLOCKS):
                o = (o_band[qb] + o_ctx[qb * 2 * BLOCK:(qb + 1) * 2 * BLOCK]) / dens[qb]
                accs[qb] = o if accs[qb] is None else accs[qb] + o
        for qb in range(Q_BLOCKS):
            att_blocks[qb][2 * kvh] = accs[qb][:BLOCK]
            att_blocks[qb][2 * kvh + 1] = accs[qb][BLOCK:]

    u = u_ref[0]
    vg = vg_ref[0]
    bs = bs_ref[...]
    gm_blocks = [[None] * 4 for _ in range(Q_BLOCKS)]
    for j in range(GM_W // LANES):
        chunks = [vg[c * BLOCK:(c + 1) * BLOCK, j * LANES:(j + 1) * LANES] for c in range(Q_BLOCKS)]
        rhs = jnp.concatenate(
            [jnp.concatenate([jnp.where(low, v, zero) for v in chunks], axis=1),
             jnp.concatenate([jnp.where(low, zero, v) for v in chunks], axis=1)], axis=0)
        mixed = jnp.dot(wcat_ref[j], rhs, preferred_element_type=F32)
        bias = jnp.where(low, bs[:, 2 * j:2 * j + 1], bs[:, 2 * j + 1:2 * j + 2])
        for c in range(Q_BLOCKS):
            gm_blocks[c][j] = (u[c * BLOCK:(c + 1) * BLOCK, j * LANES:(j + 1) * LANES].astype(F32)
                               * (mixed[:, c * LANES:(c + 1) * LANES] + bias))

    mix = jnp.concatenate([jnp.concatenate(att_blocks[c] + gm_blocks[c], axis=1) for c in range(Q_BLOCKS)],
                          axis=0).astype(BF16)
    y = jnp.dot(mix, wout_ref[...], preferred_element_type=F32)
    mod = mod_ref[0, 0]
    o_ref[0] = x_ref[0] + mod[2:3] * y


def _attn_mixer(x, mod, sink, q, k, ksw, v, vsw, kx, ksx, vx, vsx, u, vg, wcat_bf, bs_t, wout_bf):
    b, s, _ = x.shape
    tq = Q_BLOCKS * BLOCK
    nb = s // BLOCK
    l = kx.shape[1]
    cur = lambda w: pl.BlockSpec((1, tq, w), lambda bi, n: (bi, n, 0))
    prv = lambda w: pl.BlockSpec((1, BLOCK, w), lambda bi, n: (bi, jnp.maximum(n * Q_BLOCKS - 1, 0), 0))
    nxt = lambda w: pl.BlockSpec((1, BLOCK, w), lambda bi, n: (bi, jnp.minimum((n + 1) * Q_BLOCKS, nb - 1), 0))
    cx = pl.BlockSpec((1, l, KV_W), lambda bi, n: (bi, 0, 0))
    return pl.pallas_call(
        _attn_kernel,
        grid=(b, s // tq),
        in_specs=[
            pl.BlockSpec(memory_space=pltpu.SMEM),
            cur(ATT_W),
            prv(KV_W), cur(KV_W), nxt(KV_W), prv(KV_W), cur(KV_W), nxt(KV_W),
            prv(KV_W), cur(KV_W), nxt(KV_W), prv(KV_W), cur(KV_W), nxt(KV_W),
            cx, cx, cx, cx,
            cur(GM_W), cur(GM_W),
            pl.BlockSpec((4, BLOCK, 2 * BLOCK), lambda bi, n: (0, 0, 0)),
            pl.BlockSpec((BLOCK, 8), lambda bi, n: (0, 0)),
            pl.BlockSpec((D, D), lambda bi, n: (0, 0)),
            cur(D),
            pl.BlockSpec((1, 1, 6, D), lambda bi, n: (0, bi, 0, 0)),
        ],
        out_specs=cur(D),
        out_shape=jax.ShapeDtypeStruct((b, s, D), F32),
        compiler_params=_cparams(("arbitrary", "arbitrary"), VMEM_LIMIT),
        name="attn_gmlp_out",
    )(sink, q, k, k, k, ksw, ksw, ksw, v, v, v, vsw, vsw, vsw, kx, ksx, vx, vsx,
      u, vg, wcat_bf, bs_t, wout_bf, x, mod)


def _ffn_kernel(x_ref, mod_ref, wg_ref, wu_ref, wd_ref, o_ref):
    mod = mod_ref[0, 0]
    xf = x_ref[...]
    h = _modulate(xf, mod[3:4], mod[4:5]).astype(BF16)
    g = jnp.dot(h, wg_ref[...], preferred_element_type=F32)
    up = jnp.dot(h, wu_ref[...], preferred_element_type=F32)
    a = (g * _sigmoid(g) * up).astype(BF16)
    o_ref[...] = xf + mod[5:6] * jnp.dot(a, wd_ref[...], preferred_element_type=F32)


def _dense_ffn(x2d, mod, wg, wu, wd, seq):
    n = x2d.shape[0]
    f = wg.shape[1]
    tm = TM_FFN
    per_b = seq // tm
    resident = lambda shp: pl.BlockSpec(shp, lambda i: (0, 0), pipeline_mode=pl.Buffered(1))
    return pl.pallas_call(
        _ffn_kernel,
        grid=(n // tm,),
        in_specs=[
            pl.BlockSpec((tm, D), lambda i: (i, 0)),
            pl.BlockSpec((1, 1, 6, D), lambda i: (0, i // per_b, 0, 0)),
            resident((D, f)), resident((D, f)), resident((f, D)),
        ],
        out_specs=pl.BlockSpec((tm, D), lambda i: (i, 0)),
        out_shape=jax.ShapeDtypeStruct((n, D), F32),
        compiler_params=_cparams(("arbitrary",), VMEM_LIMIT),
        name="dense_ffn",
    )(x2d, mod, wg, wu, wd)


def _pool_route_kernel(x_ref, xp_ref, xn_ref, mod_ref, band_ref, pw_ref, psc_ref, wr_hi_ref, wr_lo_ref,
                       tri_ref, x3_ref, h2_ref, route_ref, cnt_ref, hext, carry):
    bi = pl.program_id(0)
    i = pl.program_id(1)
    ni = pl.num_programs(1)
    tm = x_ref.shape[1]
    seq = tm * ni
    mod = mod_ref[0, 0]

    @pl.when((bi == 0) & (i == 0))
    def _():
        carry[...] = jnp.zeros_like(carry)

    xf = x_ref[0]
    hp = _modulate(xp_ref[0], mod[0:1], mod[1:2])
    hn = _modulate(xn_ref[0], mod[0:1], mod[1:2])
    hext[0:POOL_HALO] = jnp.where(i > 0, hp, 0.0).astype(BF16)
    h_main = _modulate(xf, mod[0:1], mod[1:2])
    hext[POOL_HALO:POOL_HALO + tm] = h_main.astype(BF16)
    hext[POOL_HALO + tm:] = jnp.where(i < ni - 1, hn, 0.0).astype(BF16)

    t_local = lax.broadcasted_iota(jnp.int32, (BLOCK, 1), 0)
    ys = []
    for gi, w in enumerate(POOL_SIZES):
        lo_off = -(w // 2)
        hi_off = w - 1 - w // 2
        cols = slice(gi * POOL_GD, (gi + 1) * POOL_GD)
        outs = []
        for sb in range(tm // BLOCK):
            r0 = sb * BLOCK
            win = jnp.dot(band_ref[gi], hext[r0:r0 + BLOCK + 2 * POOL_HALO, cols],
                          preferred_element_type=F32)
            t = i * tm + r0 + t_local
            cnt = (jnp.minimum(t + hi_off, seq - 1) - jnp.maximum(t + lo_off, 0) + 1).astype(F32)
            diff = win / cnt - h_main[r0:r0 + BLOCK, cols]
            outs.append(diff.astype(BF16))
        dg = jnp.concatenate(outs, axis=0)
        ys.append(jnp.dot(dg, pw_ref[gi], preferred_element_type=F32))
    y = jnp.concatenate(ys, axis=1) * psc_ref[...]
    x3 = xf + mod[2:3] * y
    x3_ref[0] = x3

    h2 = _modulate(x3, mod[3:4], mod[4:5])
    _rows_to_slabs(h2, h2_ref)
    h_hi = h2.astype(BF16)
    h_lo = (h2 - h_hi.astype(F32)).astype(BF16)
    logits = (jnp.dot(h_hi, wr_hi_ref[...], preferred_element_type=F32)
              + jnp.dot(h_hi, wr_lo_ref[...], preferred_element_type=F32)
              + jnp.dot(h_lo, wr_hi_ref[...], preferred_element_type=F32))
    lane = lax.broadcasted_iota(jnp.int32, (tm, LANES), 1)
    lane_f = lane.astype(F32)
    neg = -jnp.inf
    lg = jnp.where(lane < N_EXPERTS, logits, neg)
    m1 = jnp.max(lg, axis=-1, keepdims=True)
    i1 = jnp.min(jnp.where(lg == m1, lane_f, float(LANES)), axis=-1, keepdims=True)
    oh1 = lane_f == i1
    lg2 = jnp.where(oh1, neg, lg)
    m2 = jnp.max(lg2, axis=-1, keepdims=True)
    i2 = jnp.min(jnp.where(lg2 == m2, lane_f, float(LANES)), axis=-1, keepdims=True)
    oh2 = lane_f == i2
    e = jnp.exp(m2 - m1)
    w1 = 1.0 / (1.0 + e)
    w2 = e / (1.0 + e)
    oh = jnp.where(oh1 | oh2, 1.0, 0.0)
    before = jnp.dot(tri_ref[...], oh.astype(BF16), preferred_element_type=F32) + carry[...]
    r1 = jnp.sum(jnp.where(oh1, before, 0.0), axis=-1, keepdims=True)
    r2 = jnp.sum(jnp.where(oh2, before, 0.0), axis=-1, keepdims=True)
    carry[...] = carry[...] + jnp.sum(oh, axis=0, keepdims=True)
    cnt_ref[...] = carry[...]
    info = jnp.where(lane == 0, i1, jnp.where(lane == 1, i2, jnp.where(lane == 2, w1, jnp.where(
        lane == 3, w2, jnp.where(lane == 4, r1, jnp.where(lane == 5, r2, 0.0))))))
    route_ref[...] = info.T[0:8, :]


def _pool_route(x, mod, band, pw_bf, pool_scale, wr_hi, wr_lo, tri):
    b, s, _ = x.shape
    tm = TM_POOL
    ni = s // tm
    hb = tm // POOL_HALO
    row = pl.BlockSpec((1, tm, D), lambda bi, i: (bi, i, 0))
    const2 = lambda shp: pl.BlockSpec(shp, lambda bi, i: (0,) * len(shp))
    return pl.pallas_call(
        _pool_route_kernel,
        grid=(b, ni),
        in_specs=[
            row,
            pl.BlockSpec((1, POOL_HALO, D), lambda bi, i: (bi, jnp.maximum(i * hb - 1, 0), 0)),
            pl.BlockSpec((1, POOL_HALO, D), lambda bi, i: (bi, jnp.minimum((i + 1) * hb, s // POOL_HALO - 1), 0)),
            pl.BlockSpec((1, 1, 6, D), lambda bi, i: (1, bi, 0, 0)),
            const2(band.shape), const2(pw_bf.shape), const2((1, D)),
            const2(wr_hi.shape), const2(wr_lo.shape), const2(tri.shape),
        ],
        out_specs=[row,
                   pl.BlockSpec((tm * SLAB, LANES), lambda bi, i: (bi * ni + i, 0)),
                   pl.BlockSpec((8, tm), lambda bi, i: (0, bi * ni + i)),
                   pl.BlockSpec((1, LANES), lambda bi, i: (0, 0))],
        out_shape=[jax.ShapeDtypeStruct((b, s, D), F32), jax.ShapeDtypeStruct((b * s * SLAB, LANES), F32),
                   jax.ShapeDtypeStruct((8, b * s), F32), jax.ShapeDtypeStruct((1, LANES), F32)],
        scratch_shapes=[pltpu.VMEM((tm + 2 * POOL_HALO, D), BF16), pltpu.VMEM((1, LANES), F32)],
        compiler_params=_cparams(("arbitrary", "arbitrary"), VMEM_LIMIT),
        name="pool_route",
    )(x, x, x, mod, band, pw_bf, pool_scale.reshape(1, D), wr_hi, wr_lo, tri)


def _slotmap_kernel(pos_ref, lo_ref, hi_ref, o_ref):
    n_pairs = pos_ref.shape[0]
    spare_mask = 2 * TM_MOE - 1
    for e in range(lo_ref.shape[0]):
        def fill(p, c):
            o_ref[p] = n_pairs + (p & spare_mask)
            return c
        lax.fori_loop(lo_ref[e], hi_ref[e], fill, 0)

    def place(f, c):
        o_ref[pos_ref[f]] = f
        return c
    lax.fori_loop(0, n_pairs, place, 0, unroll=32)


def _slot_map(pos_flat, lo, hi, n_slots):
    smem = pl.BlockSpec(memory_space=pltpu.SMEM)
    return pl.pallas_call(
        _slotmap_kernel,
        in_specs=[smem, smem, smem],
        out_specs=smem,
        out_shape=jax.ShapeDtypeStruct((n_slots,), jnp.int32),
        name="moe_slot_map",
    )(pos_flat, lo, hi)


def _moe_kernel(te_ref, nused_ref, fnext_ref, fprev_ref, f0_ref, h_hbm, wg_hbm, wu_hbm, wd_hbm,
                y_hbm, wg_res, wu_res, wd_res, stg_in, stg_out, xbuf, xb, act, acc, stage, gsem, ssem, wsem):
    i = pl.program_id(0)
    nt = pl.num_programs(0)
    used_tiles = nused_ref[0]
    tm = xb.shape[0]
    f_dim = wg_res.shape[1]
    tok_mask = h_hbm.shape[0] // SLAB - 1
    tile_rows = tm * SLAB

    def slab(ix):
        return pl.ds(pl.multiple_of(ix * SLAB, SLAB), SLAB)

    def gather_row(fref, r, slot, zero=0):
        tok = fref[0, 0, r + zero] & tok_mask
        return pltpu.make_async_copy(h_hbm.at[slab(tok)], xbuf.at[slot, slab(r)], gsem.at[slot])

    def scatter_row(fref, r, slot, zero=0):
        return pltpu.make_async_copy(stage.at[slot, slab(r)], y_hbm.at[slab(fref[0, 0, r + zero])],
                                     ssem.at[slot])

    def gather_all(slot):
        return pltpu.make_async_copy(h_hbm.at[pl.ds(0, tile_rows)], xbuf.at[slot], gsem.at[slot])

    def scatter_all(slot):
        return pltpu.make_async_copy(stage.at[slot], y_hbm.at[pl.ds(0, tile_rows)], ssem.at[slot])

    cur = i % 2
    used = i < used_tiles

    @pl.when(i == 0)
    def _():
        stage[...] = jnp.zeros_like(stage)
        spare = y_hbm.shape[0] - 2 * tile_rows
        fills = [pltpu.make_async_copy(stage.at[sl], y_hbm.at[pl.ds(spare + sl * tile_rows, tile_rows)],
                                       ssem.at[sl])
                 for sl in range(2)]
        for cp in fills:
            cp.start()
        for cp in fills:
            cp.wait()

    @pl.when((i == 0) & used)
    def _():
        def prime(r, c):
            gather_row(f0_ref, r, 0).start()
            return c
        lax.fori_loop(0, tm, prime, 0)

    expert = te_ref[i]
    new_expert = used & ((i == 0) | (expert != te_ref[jnp.maximum(i - 1, 0)]))

    has_next = i + 1 < used_tiles
    has_prev = (i >= 1) & (i - 1 < used_tiles)
    steady = (i >= 2) & has_next

    @pl.when(used)
    def _():
        gather_all(cur).wait()

    @pl.when((i >= 2) & (i - 2 < used_tiles))
    def _():
        scatter_all(cur).wait()

    half = f_dim // MOE_SPLIT
    n_piece = half // W_CHUNK
    n_groups = MOE_SPLIT * (n_piece + 1)

    def tick(v):
        bits = jnp.max(lax.bitcast_convert_type(v[0:SLAB, 0:LANES], jnp.int32))
        return lax.shift_right_logical(lax.shift_right_logical(bits, 16), 16)

    def expert_ffn(issue_group):
        for cix in range(SLAB):
            xb[:, cix * LANES:(cix + 1) * LANES] = xbuf.at[cur][pl.ds(cix, tm, stride=SLAB), :].astype(BF16)
        xv = xb[...]
        out = None
        issue_group(0, 0)
        k = 1
        for hf in range(MOE_SPLIT):
            for c in range(n_piece):
                cs = slice(hf * half + c * W_CHUNK, hf * half + (c + 1) * W_CHUNK)
                g = jnp.dot(xv, wg_res[:, cs], preferred_element_type=F32)
                up = jnp.dot(xv, wu_res[:, cs], preferred_element_type=F32)
                act[:, c * W_CHUNK:(c + 1) * W_CHUNK] = (g * _sigmoid(g) * up).astype(BF16)
                issue_group(k, tick(g))
                k += 1
            part = jnp.dot(act[...], wd_res[hf * half:(hf + 1) * half, :], preferred_element_type=F32)
            if hf < MOE_SPLIT - 1:
                acc[...] = part if out is None else acc[...] + part
                out = acc
                issue_group(k, tick(part))
                k += 1
            else:
                _rows_to_slabs(part if out is None else acc[...] + part, stage.at[cur])

    def chunk_plan():
        plan = [(w, res, stg_in, 0, W_ROWS_IN, c)
                for w, res in ((wg_hbm, wg_res), (wu_hbm, wu_res)) for c in range(D // W_ROWS_IN)]
        return plan + [(wd_hbm, wd_res, stg_out, W_RING, W_ROWS_OUT, c) for c in range(f_dim // W_ROWS_OUT)]

    def chunk_copy(plan, k, ex):
        w_hbm, _, stg, sem0, rows, c = plan[k]
        buf = k % W_RING
        return pltpu.make_async_copy(w_hbm.at[ex, pl.ds(c * rows, rows), :], stg.at[buf], wsem.at[sem0 + buf])

    @pl.when(new_expert)
    def _():
        plan = chunk_plan()

        @pl.when(i == 0)
        def _():
            for k in range(W_RING):
                chunk_copy(plan, k, expert).start(priority=k % 2)

        for k in range(len(plan)):
            _, res, stg, _, rows, c = plan[k]
            chunk_copy(plan, k, expert).wait()
            res[c * rows:(c + 1) * rows, :] = stg[k % W_RING].astype(BF16)
            if k + W_RING < len(plan):
                chunk_copy(plan, k + W_RING, expert).start(priority=(k + W_RING) % 2)

    @pl.when(steady)
    def _():
        def issue_group(k, zero):
            for r in range(k * tm // n_groups, (k + 1) * tm // n_groups):
                gather_row(fnext_ref, r, 1 - cur, zero).start()
                scatter_row(fprev_ref, r, 1 - cur, zero).start()
        expert_ffn(issue_group)

    @pl.when(jnp.logical_not(steady))
    def _():
        @pl.when(used)
        def _():
            expert_ffn(lambda k, zero: None)

        @pl.when(has_next)
        def _():
            def issue(r, c):
                gather_row(fnext_ref, r, 1 - cur).start()
                return c
            lax.fori_loop(0, tm, issue, 0)

        @pl.when(has_prev)
        def _():
            def issue(r, c):
                scatter_row(fprev_ref, r, 1 - cur).start()
                return c
            lax.fori_loop(0, tm, issue, 0)

    next_expert = te_ref[jnp.minimum(i + 1, nt - 1)]

    @pl.when(has_next & (next_expert != expert))
    def _():
        plan = chunk_plan()
        for k in range(W_RING):
            chunk_copy(plan, k, next_expert).start(priority=k % 2)

    @pl.when((i == nt - 1) & (nt - 2 < used_tiles))
    def _():
        scatter_all(1 - cur).wait()


def _moe_experts(h_slabs, fmap, tile_expert, n_used, wg, wu, wd, n_tiles, y_rows):
    f = wg.shape[2]
    tm = TM_MOE
    fblk = lambda imap: pl.BlockSpec((1, 1, tm), imap, memory_space=pltpu.SMEM)
    hbm = pl.BlockSpec(memory_space=pl.ANY)
    return pl.pallas_call(
        _moe_kernel,
        grid_spec=pltpu.PrefetchScalarGridSpec(
            num_scalar_prefetch=2,
            grid=(n_tiles,),
            in_specs=[
                fblk(lambda i, te, nu: (jnp.minimum(i + 1, n_tiles - 1), 0, 0)),
                fblk(lambda i, te, nu: (jnp.maximum(i - 1, 0), 0, 0)),
                fblk(lambda i, te, nu: (0, 0, 0)),
                hbm, hbm, hbm, hbm,
            ],
            out_specs=hbm,
            scratch_shapes=[pltpu.VMEM((D, f), BF16), pltpu.VMEM((D, f), BF16), pltpu.VMEM((f, D), BF16),
                            pltpu.VMEM((W_RING, W_ROWS_IN, f), F32), pltpu.VMEM((W_RING, W_ROWS_OUT, D), F32),
                            pltpu.VMEM((2, tm * SLAB, LANES), F32), pltpu.VMEM((tm, D), BF16),
                            pltpu.VMEM((tm, f // MOE_SPLIT), BF16), pltpu.VMEM((tm, D), F32),
                            pltpu.VMEM((2, tm * SLAB, LANES), F32),
                            pltpu.SemaphoreType.DMA((2,)), pltpu.SemaphoreType.DMA((2,)),
                            pltpu.SemaphoreType.DMA((2 * W_RING,))],
        ),
        out_shape=jax.ShapeDtypeStruct((y_rows * SLAB, LANES), F32),
        compiler_params=_cparams(("arbitrary",), MOE_VMEM_LIMIT),
        name="moe_experts",
    )(tile_expert, n_used, fmap, fmap, fmap, h_slabs, wg, wu, wd)


def _combine_kernel(y1_ref, y2_ref, x_ref, w_ref, mod_ref, gain_ref, o_ref):
    w = w_ref[...]
    rows = x_ref.shape[0]
    moe = w[:, 0:1] * _slabs_to_rows(y1_ref, rows) + w[:, 1:2] * _slabs_to_rows(y2_ref, rows)
    mod = mod_ref[0, 0]
    x4 = x_ref[...] + mod[5:6] * moe
    ms = jnp.mean(x4 * x4, axis=-1, keepdims=True)
    o_ref[...] = x4 * lax.rsqrt(ms + EPS) * gain_ref[...]


def _combine(y, x3_2d, wts, mod, final_gain, seq):
    n = x3_2d.shape[0]
    tc = TC_COMB
    nt = n // tc
    per_b = seq // tc
    return pl.pallas_call(
        _combine_kernel,
        grid=(nt,),
        in_specs=[
            pl.BlockSpec((tc * SLAB, LANES), lambda i: (i, 0)),
            pl.BlockSpec((tc * SLAB, LANES), lambda i: (i + nt, 0)),
            pl.BlockSpec((tc, D), lambda i: (i, 0)),
            pl.BlockSpec((tc, 2), lambda i: (i, 0)),
            pl.BlockSpec((1, 1, 6, D), lambda i: (1, i // per_b, 0, 0)),
            pl.BlockSpec((1, D), lambda i: (0, 0)),
        ],
        out_specs=pl.BlockSpec((tc, D), lambda i: (i, 0)),
        out_shape=jax.ShapeDtypeStruct((n, D), F32),
        compiler_params=_cparams(("arbitrary",)),
        name="moe_combine",
    )(y, y, x3_2d, wts, mod, final_gain.reshape(1, D))


def _rope_tables(seq):
    rows = seq // GRID_W
    row_pos = jnp.repeat(jnp.arange(rows, dtype=F32), GRID_W)
    col_pos = jnp.tile(jnp.arange(GRID_W, dtype=F32), rows)
    axis_dim = HEAD_DIM // 2
    inv_freq = ROPE_BASE ** (-jnp.arange(0, axis_dim, 2, dtype=F32) / axis_dim)
    ar = row_pos[:, None] * inv_freq
    ac = col_pos[:, None] * inv_freq
    cos64 = jnp.concatenate([jnp.cos(ar), jnp.cos(ar), jnp.cos(ac), jnp.cos(ac)], axis=1)
    sin64 = jnp.concatenate([-jnp.sin(ar), jnp.sin(ar), -jnp.sin(ac), jnp.sin(ac)], axis=1)
    return jnp.tile(cos64, (1, 2)), jnp.tile(sin64, (1, 2))


def _band_matrices():
    r = np.arange(BLOCK)[:, None]
    c = np.arange(BLOCK + 2 * POOL_HALO)[None, :] - POOL_HALO
    mats = []
    for w in POOL_SIZES:
        lo = -(w // 2)
        hi = w - 1 - w // 2
        mats.append(((c >= r + lo) & (c <= r + hi)).astype(np.float32))
    return jnp.asarray(np.stack(mats), dtype=BF16)


def kernel(x, c, ctx, c_ctx, w_ada, b_ada, w_in, attn_sink, gm_gain, gm_w_s, gm_b_s, w_out,
           ffn_w_gate, ffn_w_up, ffn_w_down, pool_w, pool_scale, router_w,
           moe_w_gate, moe_w_up, moe_w_down, final_gain):
    b, s, _ = x.shape
    n = b * s
    assert w_ada.shape[0] == 2 and w_in.shape[0] == 1 and pool_w.shape[0] == 1
    assert s % TM_IN == 0 and s % TM_POOL == 0 and s % TM_FFN == 0 and b <= 4
    assert n & (n - 1) == 0

    cvec = jnp.concatenate([c, c_ctx[None, :], jnp.zeros((8 - b - 1, D), F32)], axis=0)
    mod = _ada_mod(cvec, w_ada, b_ada)

    cos_t, sin_t = _rope_tables(s)
    w_in_bf = w_in[0].astype(BF16)
    q, k, ksw, v, vsw, u, vg = _in_proj(x, mod, w_in_bf, gm_gain[0], cos_t, sin_t)
    kx, ksx, vx, vsx = _ctx_kv(ctx, mod, w_in_bf)
    wcat = gm_w_s[0].reshape(4, 2, BLOCK, BLOCK).transpose(0, 2, 1, 3).reshape(4, BLOCK, 2 * BLOCK).astype(BF16)
    x1 = _attn_mixer(x, mod, attn_sink[0], q, k, ksw, v, vsw, kx, ksx, vx, vsx, u, vg,
                     wcat, gm_b_s[0].T, w_out[0].astype(BF16))
    x2 = _dense_ffn(x1.reshape(n, D), mod, ffn_w_gate[0].astype(BF16), ffn_w_up[0].astype(BF16),
                    ffn_w_down[0].astype(BF16), s)

    wr = jnp.pad(router_w[0], ((0, 0), (0, LANES - N_EXPERTS)))
    wr_hi = wr.astype(BF16)
    wr_lo = (wr - wr_hi.astype(F32)).astype(BF16)
    tri = jnp.asarray(np.tril(np.ones((TM_POOL, TM_POOL), np.float32), -1), dtype=BF16)
    x3, h2, route, counts = _pool_route(x2.reshape(b, s, D), mod, _band_matrices(), pool_w[0].astype(BF16),
                                        pool_scale[0], wr_hi, wr_lo, tri)

    tm = TM_MOE
    n_tiles = (2 * n) // tm + N_EXPERTS
    cnt = counts[0, :N_EXPERTS].astype(jnp.int32)
    tiles_e = (cnt + tm - 1) // tm
    tile_end = jnp.cumsum(tiles_e)
    off = (tile_end - tiles_e) * tm
    n_used = tile_end[-1]
    tix = jnp.arange(n_tiles, dtype=jnp.int32)
    te = jnp.minimum(jnp.sum(tix[:, None] >= tile_end[None, :], axis=1), N_EXPERTS - 1).astype(jnp.int32)
    te_last = te[jnp.maximum(n_used - 1, 0)]
    tile_expert = jnp.where(tix < n_used, te, te_last)
    e1 = route[0].astype(jnp.int32)
    e2 = route[1].astype(jnp.int32)
    pos1 = off[e1] + route[4].astype(jnp.int32)
    pos2 = off[e2] + route[5].astype(jnp.int32)
    n_slots = n_tiles * tm
    pad_lo = jnp.concatenate([off + cnt, (n_used * tm).reshape(1)]).astype(jnp.int32)
    pad_hi = jnp.concatenate([off + tiles_e * tm, jnp.full((1,), n_slots, jnp.int32)]).astype(jnp.int32)
    fmap = _slot_map(jnp.concatenate([pos1, pos2]), pad_lo, pad_hi, n_slots)
    n_used_arr = n_used.reshape(1).astype(jnp.int32)

    y = _moe_experts(h2, fmap.reshape(n_tiles, 1, tm), tile_expert, n_used_arr,
                     moe_w_gate[0], moe_w_up[0], moe_w_down[0], n_tiles, 2 * n + 2 * tm)
    out = _combine(y, x3.reshape(n, D), route[2:4].T, mod, final_gain, s)
    return out.reshape(b, s, D)
```

```python
import functools

import numpy as np
import jax
import jax.numpy as jnp
from jax import lax
from jax.experimental import pallas as pl
from jax.experimental.pallas import tpu as pltpu

F32 = jnp.float32
BF16 = jnp.bfloat16

D = 1024
GRID_W = 64
EPS = 1e-6
NEG_INF = -1e30
HEAD_DIM = 64
N_Q_HEADS = 8
BLOCK = 128
ATT_W = 512
KV_W = 128
GM_W = 512
IN_W = 1792
POOL_SIZES = (2, 4, 8, 16)
POOL_GD = 256
POOL_HALO = 16
N_EXPERTS = 8
ROPE_BASE = 10000.0
LANES = 128
SLAB = D // LANES
SQRT_2_OVER_PI = 0.7978845608028654

TM_IN = 1024
TM_FFN = 512
TM_POOL = 512
TM_MOE = 512
MOE_SPLIT = 2
W_CHUNK = 256
W_ROWS_IN = 128
W_ROWS_OUT = 512
W_RING_IN = 8
W_RING_OUT = 2
MOE_VMEM_LIMIT = 60 * 1024 * 1024
TC_COMB = 1024
Q_BLOCKS = 8
VMEM_LIMIT = 56 * 1024 * 1024


def _cparams(sem, vmem=None):
    return pltpu.CompilerParams(dimension_semantics=sem, vmem_limit_bytes=vmem)


def _modulate(xf, shift, scale):
    ms = jnp.mean(xf * xf, axis=-1, keepdims=True)
    return xf * lax.rsqrt(ms + EPS) * (1.0 + scale) + shift


def _sigmoid(z):
    return 1.0 / (1.0 + jnp.exp(-z))


def _rows_to_slabs(val, slab_ref):
    rows = val.shape[0]
    for cix in range(SLAB):
        slab_ref[pl.ds(cix, rows, stride=SLAB), :] = val[:, cix * LANES:(cix + 1) * LANES]


def _slabs_to_rows(slab_ref, rows):
    return jnp.concatenate([slab_ref[pl.ds(cix, rows, stride=SLAB), :] for cix in range(SLAB)], axis=1)


def _ada_kernel(c_ref, w_ref, b_ref, o_ref):
    c = c_ref[...]
    s = c * _sigmoid(c)
    o_ref[0] = jnp.dot(s.astype(BF16), w_ref[0].astype(BF16), preferred_element_type=F32) + b_ref[0]


def _ada_mod(cvec, w_ada, b_ada):
    depth, _, n6 = w_ada.shape
    tn = 1536
    out = pl.pallas_call(
        _ada_kernel,
        grid=(depth, n6 // tn),
        in_specs=[
            pl.BlockSpec((8, D), lambda l, j: (0, 0)),
            pl.BlockSpec((1, D, tn), lambda l, j: (l, 0, j)),
            pl.BlockSpec((1, 1, tn), lambda l, j: (l, 0, j)),
        ],
        out_specs=pl.BlockSpec((1, 8, tn), lambda l, j: (l, 0, j)),
        out_shape=jax.ShapeDtypeStruct((depth, 8, n6), F32),
        compiler_params=_cparams(("arbitrary", "arbitrary")),
        name="ada_mod",
    )(cvec, w_ada, b_ada.reshape(depth, 1, n6))
    return out.reshape(depth, 8, 6, D)


def _rope(t, cs, sn, first_half):
    fwd = pltpu.roll(t, LANES - 16, axis=1)
    bwd = pltpu.roll(t, 16, axis=1)
    return t * cs + jnp.where(first_half, fwd, bwd) * sn


def _inproj_kernel(x_ref, mod_ref, w_ref, gain_ref, cos_ref, sin_ref,
                   q_ref, k_ref, ksw_ref, v_ref, vsw_ref, u_ref, vg_ref):
    mod = mod_ref[0, 0]
    h = _modulate(x_ref[0], mod[0:1], mod[1:2]).astype(BF16)
    proj = jnp.dot(h, w_ref[...], preferred_element_type=F32)
    cs = cos_ref[...]
    sn = sin_ref[...]
    lane = lax.broadcasted_iota(jnp.int32, cs.shape, 1)
    first_half = (lane & 16) == 0
    for cix in range(ATT_W // LANES):
        t = proj[:, cix * LANES:(cix + 1) * LANES]
        q_ref[0, :, cix * LANES:(cix + 1) * LANES] = (
            _rope(t, cs, sn, first_half) * (HEAD_DIM ** -0.5)).astype(BF16)
    kr = _rope(proj[:, ATT_W:ATT_W + KV_W], cs, sn, first_half)
    k_ref[0] = kr.astype(BF16)
    ksw_ref[0] = pltpu.roll(kr, HEAD_DIM, axis=1).astype(BF16)
    vv = proj[:, ATT_W + KV_W:ATT_W + 2 * KV_W]
    v_ref[0] = vv.astype(BF16)
    vsw_ref[0] = pltpu.roll(vv, HEAD_DIM, axis=1).astype(BF16)
    z = proj[:, ATT_W + 2 * KV_W:]
    g = z * (0.5 * (1.0 + jnp.tanh(SQRT_2_OVER_PI * (z + 0.044715 * (z * z * z)))))
    u_ref[0] = g[:, :GM_W].astype(BF16)
    vg = g[:, GM_W:]
    ms = jnp.mean(vg * vg, axis=-1, keepdims=True)
    vg_ref[0] = (vg * lax.rsqrt(ms + EPS) * gain_ref[...]).astype(BF16)


def _in_proj(x, mod, w_in_bf, gm_gain, cos_t, sin_t):
    b, s, _ = x.shape
    tm = TM_IN
    row = lambda w: pl.BlockSpec((1, tm, w), lambda bi, i: (bi, i, 0))
    outs = pl.pallas_call(
        _inproj_kernel,
        grid=(b, s // tm),
        in_specs=[
            row(D),
            pl.BlockSpec((1, 1, 6, D), lambda bi, i: (0, bi, 0, 0)),
            pl.BlockSpec((D, IN_W), lambda bi, i: (0, 0)),
            pl.BlockSpec((1, GM_W), lambda bi, i: (0, 0)),
            pl.BlockSpec((tm, LANES), lambda bi, i: (i, 0)),
            pl.BlockSpec((tm, LANES), lambda bi, i: (i, 0)),
        ],
        out_specs=[row(ATT_W), row(KV_W), row(KV_W), row(KV_W), row(KV_W), row(GM_W), row(GM_W)],
        out_shape=[jax.ShapeDtypeStruct((b, s, w), BF16)
                   for w in (ATT_W, KV_W, KV_W, KV_W, KV_W, GM_W, GM_W)],
        compiler_params=_cparams(("arbitrary", "arbitrary"), VMEM_LIMIT),
        name="in_proj",
    )(x, mod, w_in_bf, gm_gain.reshape(1, GM_W), cos_t, sin_t)
    return outs


def _ctx_kernel(c_ref, mod_ref, w_ref, k_ref, ksw_ref, v_ref, vsw_ref):
    mod = mod_ref[0, 0]
    h = _modulate(c_ref[0], mod[0:1], mod[1:2]).astype(BF16)
    kv = jnp.dot(h, w_ref[...], preferred_element_type=F32)
    kk = kv[:, :KV_W]
    vv = kv[:, KV_W:]
    k_ref[0] = kk.astype(BF16)
    ksw_ref[0] = pltpu.roll(kk, HEAD_DIM, axis=1).astype(BF16)
    v_ref[0] = vv.astype(BF16)
    vsw_ref[0] = pltpu.roll(vv, HEAD_DIM, axis=1).astype(BF16)


def _ctx_kv(ctx, mod, w_in_bf):
    b, l, _ = ctx.shape
    spec = pl.BlockSpec((1, l, KV_W), lambda bi: (bi, 0, 0))
    return pl.pallas_call(
        _ctx_kernel,
        grid=(b,),
        in_specs=[
            pl.BlockSpec((1, l, D), lambda bi: (bi, 0, 0)),
            pl.BlockSpec((1, 1, 6, D), lambda bi: (0, b, 0, 0)),
            pl.BlockSpec((D, 2 * KV_W), lambda bi: (0, ATT_W // (2 * KV_W))),
        ],
        out_specs=[spec] * 4,
        out_shape=[jax.ShapeDtypeStruct((b, l, KV_W), BF16)] * 4,
        compiler_params=_cparams(("arbitrary",)),
        name="ctx_kv",
    )(ctx, mod, w_in_bf)


def _attn_kernel(sink_ref, q_ref, kp_ref, kc_ref, kn_ref, ksp_ref, ksc_ref, ksn_ref,
                 vp_ref, vc_ref, vn_ref, vsp_ref, vsc_ref, vsn_ref,
                 kx_ref, ksx_ref, vx_ref, vsx_ref,
                 u_ref, vg_ref, wcat_ref, bs_ref, wout_ref, x_ref, mod_ref, o_ref):
    n = pl.program_id(1)
    nblk = pl.num_programs(1) * Q_BLOCKS
    lane = lax.broadcasted_iota(jnp.int32, (1, LANES), 1)
    low = lane < HEAD_DIM
    zero = jnp.zeros((), BF16)

    def variants(a0, a1):
        return ((jnp.where(low, a0, zero), jnp.where(low, zero, a1)),
                (jnp.where(low, a1, zero), jnp.where(low, zero, a0)))

    cat = lambda refs: jnp.concatenate([r[0] for r in refs], axis=0)
    kb_var = variants(cat((kp_ref, kc_ref, kn_ref)), cat((ksp_ref, ksc_ref, ksn_ref)))
    vb_var = variants(cat((vp_ref, vc_ref, vn_ref)), cat((vsp_ref, vsc_ref, vsn_ref)))
    kx_var = variants(kx_ref[0], ksx_ref[0])
    vx_var = variants(vx_ref[0], vsx_ref[0])

    row = lax.broadcasted_iota(jnp.int32, (2 * BLOCK, BLOCK), 0) & (BLOCK - 1)
    col = lax.broadcasted_iota(jnp.int32, (2 * BLOCK, BLOCK), 1)
    top = lax.broadcasted_iota(jnp.int32, (2 * BLOCK, 1), 0) < BLOCK
    nt_dims = (((1,), (1,)), ((), ()))

    q = q_ref[0]
    att_blocks = [[None] * 4 for _ in range(Q_BLOCKS)]
    for kvh in range(2):
        qst = jnp.concatenate(
            [q[qb * BLOCK:(qb + 1) * BLOCK, pr * LANES:(pr + 1) * LANES]
             for qb in range(Q_BLOCKS) for pr in (2 * kvh, 2 * kvh + 1)], axis=0)
        accs = [None] * Q_BLOCKS
        for half in range(2):
            sk = jnp.where(top, sink_ref[4 * kvh + half], sink_ref[4 * kvh + 2 + half])
            s_ctx = lax.dot_general(qst, kx_var[kvh][half], nt_dims, preferred_element_type=F32)
            o_band, p_ctx, dens = [], [], []
            for qb in range(Q_BLOCKS):
                g = n * Q_BLOCKS + qb
                qrows = qst[qb * 2 * BLOCK:(qb + 1) * 2 * BLOCK]
                sb = lax.dot_general(qrows, kb_var[kvh][half][qb * BLOCK:(qb + 3) * BLOCK], nt_dims,
                                     preferred_element_type=F32)
                s0 = jnp.where((col >= row) & (g > 0), sb[:, :BLOCK], NEG_INF)
                s1 = sb[:, BLOCK:2 * BLOCK]
                s2 = jnp.where((col <= row) & (g < nblk - 1), sb[:, 2 * BLOCK:], NEG_INF)
                sc = s_ctx[qb * 2 * BLOCK:(qb + 1) * 2 * BLOCK]
                ctx_blocks = [sc[:, cb * BLOCK:(cb + 1) * BLOCK] for cb in range(sc.shape[1] // BLOCK)]
                m = functools.reduce(jnp.maximum, [s0, s1, s2] + ctx_blocks)
                m = jnp.maximum(jnp.max(m, axis=-1, keepdims=True), sk)
                p0, p1, p2, pc = (jnp.exp(t - m) for t in (s0, s1, s2, sc))
                psum = functools.reduce(
                    jnp.add, [p0, p1, p2] + [pc[:, cb * BLOCK:(cb + 1) * BLOCK] for cb in range(len(ctx_blocks))])
                den = jnp.sum(psum, axis=-1, keepdims=True) + jnp.exp(sk - m)
                pb = jnp.concatenate([p0, p1, p2], axis=1).astype(BF16)
                o_band.append(jnp.dot(pb, vb_var[kvh][half][qb * BLOCK:(qb + 3) * BLOCK],
                                      preferred_element_type=F32))
                p_ctx.append(pc.astype(BF16))
                dens.append(den)
            o_ctx = jnp.dot(jnp.concatenate(p_ctx, axis=0), vx_var[kvh][half], preferred_element_type=F32)
            for qb in range(Q_BLOCKS):
                o = (o_band[qb] + o_ctx[qb * 2 * BLOCK:(qb + 1) * 2 * BLOCK]) / dens[qb]
                accs[qb] = o if accs[qb] is None else accs[qb] + o
        for qb in range(Q_BLOCKS):
            att_blocks[qb][2 * kvh] = accs[qb][:BLOCK]
            att_blocks[qb][2 * kvh + 1] = accs[qb][BLOCK:]

    u = u_ref[0]
    vg = vg_ref[0]
    bs = bs_ref[...]
    gm_blocks = [[None] * 4 for _ in range(Q_BLOCKS)]
    for j in range(GM_W // LANES):
        chunks = [vg[c * BLOCK:(c + 1) * BLOCK, j * LANES:(j + 1) * LANES] for c in range(Q_BLOCKS)]
        rhs = jnp.concatenate(
            [jnp.concatenate([jnp.where(low, v, zero) for v in chunks], axis=1),
             jnp.concatenate([jnp.where(low, zero, v) for v in chunks], axis=1)], axis=0)
        mixed = jnp.dot(wcat_ref[j], rhs, preferred_element_type=F32)
        bias = jnp.where(low, bs[:, 2 * j:2 * j + 1], bs[:, 2 * j + 1:2 * j + 2])
        for c in range(Q_BLOCKS):
            gm_blocks[c][j] = (u[c * BLOCK:(c + 1) * BLOCK, j * LANES:(j + 1) * LANES].astype(F32)
                               * (mixed[:, c * LANES:(c + 1) * LANES] + bias))

    mix = jnp.concatenate([jnp.concatenate(att_blocks[c] + gm_blocks[c], axis=1) for c in range(Q_BLOCKS)],
                          axis=0).astype(BF16)
    y = jnp.dot(mix, wout_ref[...], preferred_element_type=F32)
    mod = mod_ref[0, 0]
    o_ref[0] = x_ref[0] + mod[2:3] * y


def _attn_mixer(x, mod, sink, q, k, ksw, v, vsw, kx, ksx, vx, vsx, u, vg, wcat_bf, bs_t, wout_bf):
    b, s, _ = x.shape
    tq = Q_BLOCKS * BLOCK
    nb = s // BLOCK
    l = kx.shape[1]
    cur = lambda w: pl.BlockSpec((1, tq, w), lambda bi, n: (bi, n, 0))
    prv = lambda w: pl.BlockSpec((1, BLOCK, w), lambda bi, n: (bi, jnp.maximum(n * Q_BLOCKS - 1, 0), 0))
    nxt = lambda w: pl.BlockSpec((1, BLOCK, w), lambda bi, n: (bi, jnp.minimum((n + 1) * Q_BLOCKS, nb - 1), 0))
    cx = pl.BlockSpec((1, l, KV_W), lambda bi, n: (bi, 0, 0))
    return pl.pallas_call(
        _attn_kernel,
        grid=(b, s // tq),
        in_specs=[
            pl.BlockSpec(memory_space=pltpu.SMEM),
            cur(ATT_W),
            prv(KV_W), cur(KV_W), nxt(KV_W), prv(KV_W), cur(KV_W), nxt(KV_W),
            prv(KV_W), cur(KV_W), nxt(KV_W), prv(KV_W), cur(KV_W), nxt(KV_W),
            cx, cx, cx, cx,
            cur(GM_W), cur(GM_W),
            pl.BlockSpec((4, BLOCK, 2 * BLOCK), lambda bi, n: (0, 0, 0)),
            pl.BlockSpec((BLOCK, 8), lambda bi, n: (0, 0)),
            pl.BlockSpec((D, D), lambda bi, n: (0, 0)),
            cur(D),
            pl.BlockSpec((1, 1, 6, D), lambda bi, n: (0, bi, 0, 0)),
        ],
        out_specs=cur(D),
        out_shape=jax.ShapeDtypeStruct((b, s, D), F32),
        compiler_params=_cparams(("arbitrary", "arbitrary"), VMEM_LIMIT),
        name="attn_gmlp_out",
    )(sink, q, k, k, k, ksw, ksw, ksw, v, v, v, vsw, vsw, vsw, kx, ksx, vx, vsx,
      u, vg, wcat_bf, bs_t, wout_bf, x, mod)


def _ffn_kernel(x_ref, mod_ref, wg_ref, wu_ref, wd_ref, o_ref):
    mod = mod_ref[0, 0]
    xf = x_ref[...]
    h = _modulate(xf, mod[3:4], mod[4:5]).astype(BF16)
    g = jnp.dot(h, wg_ref[...], preferred_element_type=F32)
    up = jnp.dot(h, wu_ref[...], preferred_element_type=F32)
    a = (g * _sigmoid(g) * up).astype(BF16)
    o_ref[...] = xf + mod[5:6] * jnp.dot(a, wd_ref[...], preferred_element_type=F32)


def _dense_ffn(x2d, mod, wg, wu, wd, seq):
    n = x2d.shape[0]
    f = wg.shape[1]
    tm = TM_FFN
    per_b = seq // tm
    resident = lambda shp: pl.BlockSpec(shp, lambda i: (0, 0), pipeline_mode=pl.Buffered(1))
    return pl.pallas_call(
        _ffn_kernel,
        grid=(n // tm,),
        in_specs=[
            pl.BlockSpec((tm, D), lambda i: (i, 0)),
            pl.BlockSpec((1, 1, 6, D), lambda i: (0, i // per_b, 0, 0)),
            resident((D, f)), resident((D, f)), resident((f, D)),
        ],
        out_specs=pl.BlockSpec((tm, D), lambda i: (i, 0)),
        out_shape=jax.ShapeDtypeStruct((n, D), F32),
        compiler_params=_cparams(("arbitrary",), VMEM_LIMIT),
        name="dense_ffn",
    )(x2d, mod, wg, wu, wd)


def _pool_route_kernel(x_ref, xp_ref, xn_ref, mod_ref, band_ref, pw_ref, psc_ref, wr_hi_ref, wr_lo_ref,
                       tri_ref, x3_ref, h2_ref, route_ref, cnt_ref, hext, carry):
    bi = pl.program_id(0)
    i = pl.program_id(1)
    ni = pl.num_programs(1)
    tm = x_ref.shape[1]
    seq = tm * ni
    mod = mod_ref[0, 0]

    @pl.when((bi == 0) & (i == 0))
    def _():
        carry[...] = jnp.zeros_like(carry)

    xf = x_ref[0]
    hp = _modulate(xp_ref[0], mod[0:1], mod[1:2])
    hn = _modulate(xn_ref[0], mod[0:1], mod[1:2])
    hext[0:POOL_HALO] = jnp.where(i > 0, hp, 0.0).astype(BF16)
    h_main = _modulate(xf, mod[0:1], mod[1:2])
    hext[POOL_HALO:POOL_HALO + tm] = h_main.astype(BF16)
    hext[POOL_HALO + tm:] = jnp.where(i < ni - 1, hn, 0.0).astype(BF16)

    t_local = lax.broadcasted_iota(jnp.int32, (BLOCK, 1), 0)
    ys = []
    for gi, w in enumerate(POOL_SIZES):
        lo_off = -(w // 2)
        hi_off = w - 1 - w // 2
        cols = slice(gi * POOL_GD, (gi + 1) * POOL_GD)
        outs = []
        for sb in range(tm // BLOCK):
            r0 = sb * BLOCK
            win = jnp.dot(band_ref[gi], hext[r0:r0 + BLOCK + 2 * POOL_HALO, cols],
                          preferred_element_type=F32)
            t = i * tm + r0 + t_local
            cnt = (jnp.minimum(t + hi_off, seq - 1) - jnp.maximum(t + lo_off, 0) + 1).astype(F32)
            diff = win / cnt - h_main[r0:r0 + BLOCK, cols]
            outs.append(diff.astype(BF16))
        dg = jnp.concatenate(outs, axis=0)
        ys.append(jnp.dot(dg, pw_ref[gi], preferred_element_type=F32))
    y = jnp.concatenate(ys, axis=1) * psc_ref[...]
    x3 = xf + mod[2:3] * y
    x3_ref[0] = x3

    h2 = _modulate(x3, mod[3:4], mod[4:5])
    _rows_to_slabs(h2, h2_ref)
    h_hi = h2.astype(BF16)
    h_lo = (h2 - h_hi.astype(F32)).astype(BF16)
    logits = (jnp.dot(h_hi, wr_hi_ref[...], preferred_element_type=F32)
              + jnp.dot(h_hi, wr_lo_ref[...], preferred_element_type=F32)
              + jnp.dot(h_lo, wr_hi_ref[...], preferred_element_type=F32))
    lane = lax.broadcasted_iota(jnp.int32, (tm, LANES), 1)
    lane_f = lane.astype(F32)
    neg = -jnp.inf
    lg = jnp.where(lane < N_EXPERTS, logits, neg)
    m1 = jnp.max(lg, axis=-1, keepdims=True)
    i1 = jnp.min(jnp.where(lg == m1, lane_f, float(LANES)), axis=-1, keepdims=True)
    oh1 = lane_f == i1
    lg2 = jnp.where(oh1, neg, lg)
    m2 = jnp.max(lg2, axis=-1, keepdims=True)
    i2 = jnp.min(jnp.where(lg2 == m2, lane_f, float(LANES)), axis=-1, keepdims=True)
    oh2 = lane_f == i2
    e = jnp.exp(m2 - m1)
    w1 = 1.0 / (1.0 + e)
    w2 = e / (1.0 + e)
    oh = jnp.where(oh1 | oh2, 1.0, 0.0)
    before = jnp.dot(tri_ref[...], oh.astype(BF16), preferred_element_type=F32) + carry[...]
    r1 = jnp.sum(jnp.where(oh1, before, 0.0), axis=-1, keepdims=True)
    r2 = jnp.sum(jnp.where(oh2, before, 0.0), axis=-1, keepdims=True)
    carry[...] = carry[...] + jnp.sum(oh, axis=0, keepdims=True)
    cnt_ref[...] = carry[...]
    info = jnp.where(lane == 0, i1, jnp.where(lane == 1, i2, jnp.where(lane == 2, w1, jnp.where(
        lane == 3, w2, jnp.where(lane == 4, r1, jnp.where(lane == 5, r2, 0.0))))))
    route_ref[...] = info.T[0:8, :]


def _pool_route(x, mod, band, pw_bf, pool_scale, wr_hi, wr_lo, tri):
    b, s, _ = x.shape
    tm = TM_POOL
    ni = s // tm
    hb = tm // POOL_HALO
    row = pl.BlockSpec((1, tm, D), lambda bi, i: (bi, i, 0))
    const2 = lambda shp: pl.BlockSpec(shp, lambda bi, i: (0,) * len(shp))
    return pl.pallas_call(
        _pool_route_kernel,
        grid=(b, ni),
        in_specs=[
            row,
            pl.BlockSpec((1, POOL_HALO, D), lambda bi, i: (bi, jnp.maximum(i * hb - 1, 0), 0)),
            pl.BlockSpec((1, POOL_HALO, D), lambda bi, i: (bi, jnp.minimum((i + 1) * hb, s // POOL_HALO - 1), 0)),
            pl.BlockSpec((1, 1, 6, D), lambda bi, i: (1, bi, 0, 0)),
            const2(band.shape), const2(pw_bf.shape), const2((1, D)),
            const2(wr_hi.shape), const2(wr_lo.shape), const2(tri.shape),
        ],
        out_specs=[row,
                   pl.BlockSpec((tm * SLAB, LANES), lambda bi, i: (bi * ni + i, 0)),
                   pl.BlockSpec((8, tm), lambda bi, i: (0, bi * ni + i)),
                   pl.BlockSpec((1, LANES), lambda bi, i: (0, 0))],
        out_shape=[jax.ShapeDtypeStruct((b, s, D), F32), jax.ShapeDtypeStruct((b * s * SLAB, LANES), F32),
                   jax.ShapeDtypeStruct((8, b * s), F32), jax.ShapeDtypeStruct((1, LANES), F32)],
        scratch_shapes=[pltpu.VMEM((tm + 2 * POOL_HALO, D), BF16), pltpu.VMEM((1, LANES), F32)],
        compiler_params=_cparams(("arbitrary", "arbitrary"), VMEM_LIMIT),
        name="pool_route",
    )(x, x, x, mod, band, pw_bf, pool_scale.reshape(1, D), wr_hi, wr_lo, tri)


def _slotmap_kernel(pos_ref, lo_ref, hi_ref, o_ref):
    n_pairs = pos_ref.shape[0]
    spare_mask = 2 * TM_MOE - 1
    for e in range(lo_ref.shape[0]):
        def fill(p, c):
            o_ref[p] = n_pairs + (p & spare_mask)
            return c
        lax.fori_loop(lo_ref[e], hi_ref[e], fill, 0)

    def place(f, c):
        o_ref[pos_ref[f]] = f
        return c
    lax.fori_loop(0, n_pairs, place, 0, unroll=32)


def _slot_map(pos_flat, lo, hi, n_slots):
    smem = pl.BlockSpec(memory_space=pltpu.SMEM)
    return pl.pallas_call(
        _slotmap_kernel,
        in_specs=[smem, smem, smem],
        out_specs=smem,
        out_shape=jax.ShapeDtypeStruct((n_slots,), jnp.int32),
        name="moe_slot_map",
    )(pos_flat, lo, hi)


def _moe_kernel(te_ref, nused_ref, fnext_ref, fprev_ref, f0_ref, h_hbm, wg_hbm, wu_hbm, wd_hbm,
                y_hbm, wg_res, wu_res, wd_res, stg_in, stg_out, xbuf, xb, act, acc, stage, gsem, ssem, wsem):
    i = pl.program_id(0)
    nt = pl.num_programs(0)
    used_tiles = nused_ref[0]
    tm = xb.shape[0]
    f_dim = wg_res.shape[1]
    tok_mask = h_hbm.shape[0] // SLAB - 1
    tile_rows = tm * SLAB

    def slab(ix):
        return pl.ds(pl.multiple_of(ix * SLAB, SLAB), SLAB)

    def gather_row(fref, r, slot, zero=0):
        tok = fref[0, 0, r + zero] & tok_mask
        return pltpu.make_async_copy(h_hbm.at[slab(tok)], xbuf.at[slot, slab(r)], gsem.at[slot])

    def scatter_row(fref, r, slot, zero=0):
        return pltpu.make_async_copy(stage.at[slot, slab(r)], y_hbm.at[slab(fref[0, 0, r + zero])],
                                     ssem.at[slot])

    def gather_all(slot):
        return pltpu.make_async_copy(h_hbm.at[pl.ds(0, tile_rows)], xbuf.at[slot], gsem.at[slot])

    def scatter_all(slot):
        return pltpu.make_async_copy(stage.at[slot], y_hbm.at[pl.ds(0, tile_rows)], ssem.at[slot])

    cur = i % 2
    used = i < used_tiles

    @pl.when(i == 0)
    def _():
        stage[...] = jnp.zeros_like(stage)
        spare = y_hbm.shape[0] - 2 * tile_rows
        fills = [pltpu.make_async_copy(stage.at[sl], y_hbm.at[pl.ds(spare + sl * tile_rows, tile_rows)],
                                       ssem.at[sl])
                 for sl in range(2)]
        for cp in fills:
            cp.start()
        for cp in fills:
            cp.wait()

    @pl.when((i == 0) & used)
    def _():
        def prime(r, c):
            gather_row(f0_ref, r, 0).start()
            return c
        lax.fori_loop(0, tm, prime, 0)

    expert = te_ref[i]
    new_expert = used & ((i == 0) | (expert != te_ref[jnp.maximum(i - 1, 0)]))

    has_next = i + 1 < used_tiles
    has_prev = (i >= 1) & (i - 1 < used_tiles)
    steady = (i >= 2) & has_next

    @pl.when(used)
    def _():
        gather_all(cur).wait()

    @pl.when((i >= 2) & (i - 2 < used_tiles))
    def _():
        scatter_all(cur).wait()

    half = f_dim // MOE_SPLIT
    n_piece = half // W_CHUNK
    n_groups = MOE_SPLIT * (n_piece + 1)

    def tick(v):
        bits = jnp.max(lax.bitcast_convert_type(v[0:SLAB, 0:LANES], jnp.int32))
        return lax.shift_right_logical(lax.shift_right_logical(bits, 16), 16)

    def expert_ffn(issue_group):
        for cix in range(SLAB):
            xb[:, cix * LANES:(cix + 1) * LANES] = xbuf.at[cur][pl.ds(cix, tm, stride=SLAB), :].astype(BF16)
        xv = xb[...]
        out = None
        issue_group(0, 0)
        k = 1
        for hf in range(MOE_SPLIT):
            for c in range(n_piece):
                cs = slice(hf * half + c * W_CHUNK, hf * half + (c + 1) * W_CHUNK)
                g = jnp.dot(xv, wg_res[:, cs], preferred_element_type=F32)
                up = jnp.dot(xv, wu_res[:, cs], preferred_element_type=F32)
                act[:, c * W_CHUNK:(c + 1) * W_CHUNK] = (g * _sigmoid(g) * up).astype(BF16)
                issue_group(k, tick(g))
                k += 1
            part = jnp.dot(act[...], wd_res[hf * half:(hf + 1) * half, :], preferred_element_type=F32)
            if hf < MOE_SPLIT - 1:
                acc[...] = part if out is None else acc[...] + part
                out = acc
                issue_group(k, tick(part))
                k += 1
            else:
                _rows_to_slabs(part if out is None else acc[...] + part, stage.at[cur])

    def chunk_plan():
        n_in = D // W_ROWS_IN
        plan = [(w, res, stg_in, W_RING_IN, 0, W_ROWS_IN, c, wi * n_in + c)
                for wi, (w, res) in enumerate(((wg_hbm, wg_res), (wu_hbm, wu_res))) for c in range(n_in)]
        return plan + [(wd_hbm, wd_res, stg_out, W_RING_OUT, W_RING_IN, W_ROWS_OUT, c, c)
                       for c in range(f_dim // W_ROWS_OUT)]

    def chunk_copy(entry, ex):
        w_hbm, _, stg, ring, sem0, rows, c, j = entry
        return pltpu.make_async_copy(w_hbm.at[ex, pl.ds(c * rows, rows), :], stg.at[j % ring],
                                     wsem.at[sem0 + j % ring])

    def start_first_chunks(ex):
        for entry in chunk_plan():
            if entry[7] < entry[3]:
                chunk_copy(entry, ex).start()

    @pl.when(new_expert)
    def _():
        plan = chunk_plan()

        @pl.when(i == 0)
        def _():
            start_first_chunks(expert)

        for k, entry in enumerate(plan):
            _, res, stg, ring, _, rows, c, j = entry
            chunk_copy(entry, expert).wait()
            res[c * rows:(c + 1) * rows, :] = stg[j % ring].astype(BF16)
            later = [e for e in plan[k + 1:] if e[2] is stg and e[7] == j + ring]
            if later:
                chunk_copy(later[0], expert).start()

    next_expert = te_ref[jnp.minimum(i + 1, nt - 1)]

    @pl.when(has_next & (next_expert != expert))
    def _():
        start_first_chunks(next_expert)

    @pl.when(steady)
    def _():
        def issue_group(k, zero):
            for r in range(k * tm // n_groups, (k + 1) * tm // n_groups):
                gather_row(fnext_ref, r, 1 - cur, zero).start()
                scatter_row(fprev_ref, r, 1 - cur, zero).start()
        expert_ffn(issue_group)

    @pl.when(jnp.logical_not(steady))
    def _():
        @pl.when(used)
        def _():
            expert_ffn(lambda k, zero: None)

        @pl.when(has_next)
        def _():
            def issue(r, c):
                gather_row(fnext_ref, r, 1 - cur).start()
                return c
            lax.fori_loop(0, tm, issue, 0)

        @pl.when(has_prev)
        def _():
            def issue(r, c):
                scatter_row(fprev_ref, r, 1 - cur).start()
                return c
            lax.fori_loop(0, tm, issue, 0)

    @pl.when((i == nt - 1) & (nt - 2 < used_tiles))
    def _():
        scatter_all(1 - cur).wait()


def _moe_experts(h_slabs, fmap, tile_expert, n_used, wg, wu, wd, n_tiles, y_rows):
    f = wg.shape[2]
    tm = TM_MOE
    fblk = lambda imap: pl.BlockSpec((1, 1, tm), imap, memory_space=pltpu.SMEM)
    hbm = pl.BlockSpec(memory_space=pl.ANY)
    return pl.pallas_call(
        _moe_kernel,
        grid_spec=pltpu.PrefetchScalarGridSpec(
            num_scalar_prefetch=2,
            grid=(n_tiles,),
            in_specs=[
                fblk(lambda i, te, nu: (jnp.minimum(i + 1, n_tiles - 1), 0, 0)),
                fblk(lambda i, te, nu: (jnp.maximum(i - 1, 0), 0, 0)),
                fblk(lambda i, te, nu: (0, 0, 0)),
                hbm, hbm, hbm, hbm,
            ],
            out_specs=hbm,
            scratch_shapes=[pltpu.VMEM((D, f), BF16), pltpu.VMEM((D, f), BF16), pltpu.VMEM((f, D), BF16),
                            pltpu.VMEM((W_RING_IN, W_ROWS_IN, f), F32), pltpu.VMEM((W_RING_OUT, W_ROWS_OUT, D), F32),
                            pltpu.VMEM((2, tm * SLAB, LANES), F32), pltpu.VMEM((tm, D), BF16),
                            pltpu.VMEM((tm, f // MOE_SPLIT), BF16), pltpu.VMEM((tm, D), F32),
                            pltpu.VMEM((2, tm * SLAB, LANES), F32),
                            pltpu.SemaphoreType.DMA((2,)), pltpu.SemaphoreType.DMA((2,)),
                            pltpu.SemaphoreType.DMA((W_RING_IN + W_RING_OUT,))],
        ),
        out_shape=jax.ShapeDtypeStruct((y_rows * SLAB, LANES), F32),
        compiler_params=_cparams(("arbitrary",), MOE_VMEM_LIMIT),
        name="moe_experts",
    )(tile_expert, n_used, fmap, fmap, fmap, h_slabs, wg, wu, wd)


def _combine_kernel(y1_ref, y2_ref, x_ref, w_ref, mod_ref, gain_ref, o_ref):
    w = w_ref[...]
    rows = x_ref.shape[0]
    moe = w[:, 0:1] * _slabs_to_rows(y1_ref, rows) + w[:, 1:2] * _slabs_to_rows(y2_ref, rows)
    mod = mod_ref[0, 0]
    x4 = x_ref[...] + mod[5:6] * moe
    ms = jnp.mean(x4 * x4, axis=-1, keepdims=True)
    o_ref[...] = x4 * lax.rsqrt(ms + EPS) * gain_ref[...]


def _combine(y, x3_2d, wts, mod, final_gain, seq):
    n = x3_2d.shape[0]
    tc = TC_COMB
    nt = n // tc
    per_b = seq // tc
    return pl.pallas_call(
        _combine_kernel,
        grid=(nt,),
        in_specs=[
            pl.BlockSpec((tc * SLAB, LANES), lambda i: (i, 0)),
            pl.BlockSpec((tc * SLAB, LANES), lambda i: (i + nt, 0)),
            pl.BlockSpec((tc, D), lambda i: (i, 0)),
            pl.BlockSpec((tc, 2), lambda i: (i, 0)),
            pl.BlockSpec((1, 1, 6, D), lambda i: (1, i // per_b, 0, 0)),
            pl.BlockSpec((1, D), lambda i: (0, 0)),
        ],
        out_specs=pl.BlockSpec((tc, D), lambda i: (i, 0)),
        out_shape=jax.ShapeDtypeStruct((n, D), F32),
        compiler_params=_cparams(("arbitrary",)),
        name="moe_combine",
    )(y, y, x3_2d, wts, mod, final_gain.reshape(1, D))


def _rope_tables(seq):
    rows = seq // GRID_W
    row_pos = jnp.repeat(jnp.arange(rows, dtype=F32), GRID_W)
    col_pos = jnp.tile(jnp.arange(GRID_W, dtype=F32), rows)
    axis_dim = HEAD_DIM // 2
    inv_freq = ROPE_BASE ** (-jnp.arange(0, axis_dim, 2, dtype=F32) / axis_dim)
    ar = row_pos[:, None] * inv_freq
    ac = col_pos[:, None] * inv_freq
    cos64 = jnp.concatenate([jnp.cos(ar), jnp.cos(ar), jnp.cos(ac), jnp.cos(ac)], axis=1)
    sin64 = jnp.concatenate([-jnp.sin(ar), jnp.sin(ar), -jnp.sin(ac), jnp.sin(ac)], axis=1)
    return jnp.tile(cos64, (1, 2)), jnp.tile(sin64, (1, 2))


def _band_matrices():
    r = np.arange(BLOCK)[:, None]
    c = np.arange(BLOCK + 2 * POOL_HALO)[None, :] - POOL_HALO
    mats = []
    for w in POOL_SIZES:
        lo = -(w // 2)
        hi = w - 1 - w // 2
        mats.append(((c >= r + lo) & (c <= r + hi)).astype(np.float32))
    return jnp.asarray(np.stack(mats), dtype=BF16)


def kernel(x, c, ctx, c_ctx, w_ada, b_ada, w_in, attn_sink, gm_gain, gm_w_s, gm_b_s, w_out,
           ffn_w_gate, ffn_w_up, ffn_w_down, pool_w, pool_scale, router_w,
           moe_w_gate, moe_w_up, moe_w_down, final_gain):
    b, s, _ = x.shape
    n = b * s
    assert w_ada.shape[0] == 2 and w_in.shape[0] == 1 and pool_w.shape[0] == 1
    assert s % TM_IN == 0 and s % TM_POOL == 0 and s % TM_FFN == 0 and b <= 4
    assert n & (n - 1) == 0

    cvec = jnp.concatenate([c, c_ctx[None, :], jnp.zeros((8 - b - 1, D), F32)], axis=0)
    mod = _ada_mod(cvec, w_ada, b_ada)

    cos_t, sin_t = _rope_tables(s)
    w_in_bf = w_in[0].astype(BF16)
    q, k, ksw, v, vsw, u, vg = _in_proj(x, mod, w_in_bf, gm_gain[0], cos_t, sin_t)
    kx, ksx, vx, vsx = _ctx_kv(ctx, mod, w_in_bf)
    wcat = gm_w_s[0].reshape(4, 2, BLOCK, BLOCK).transpose(0, 2, 1, 3).reshape(4, BLOCK, 2 * BLOCK).astype(BF16)
    x1 = _attn_mixer(x, mod, attn_sink[0], q, k, ksw, v, vsw, kx, ksx, vx, vsx, u, vg,
                     wcat, gm_b_s[0].T, w_out[0].astype(BF16))
    x2 = _dense_ffn(x1.reshape(n, D), mod, ffn_w_gate[0].astype(BF16), ffn_w_up[0].astype(BF16),
                    ffn_w_down[0].astype(BF16), s)

    wr = jnp.pad(router_w[0], ((0, 0), (0, LANES - N_EXPERTS)))
    wr_hi = wr.astype(BF16)
    wr_lo = (wr - wr_hi.astype(F32)).astype(BF16)
    tri = jnp.asarray(np.tril(np.ones((TM_POOL, TM_POOL), np.float32), -1), dtype=BF16)
    x3, h2, route, counts = _pool_route(x2.reshape(b, s, D), mod, _band_matrices(), pool_w[0].astype(BF16),
                                        pool_scale[0], wr_hi, wr_lo, tri)

    tm = TM_MOE
    n_tiles = (2 * n) // tm + N_EXPERTS
    cnt = counts[0, :N_EXPERTS].astype(jnp.int32)
    tiles_e = (cnt + tm - 1) // tm
    tile_end = jnp.cumsum(tiles_e)
    off = (tile_end - tiles_e) * tm
    n_used = tile_end[-1]
    tix = jnp.arange(n_tiles, dtype=jnp.int32)
    te = jnp.minimum(jnp.sum(tix[:, None] >= tile_end[None, :], axis=1), N_EXPERTS - 1).astype(jnp.int32)
    te_last = te[jnp.maximum(n_used - 1, 0)]
    tile_expert = jnp.where(tix < n_used, te, te_last)
    e1 = route[0].astype(jnp.int32)
    e2 = route[1].astype(jnp.int32)
    pos1 = off[e1] + route[4].astype(jnp.int32)
    pos2 = off[e2] + route[5].astype(jnp.int32)
    n_slots = n_tiles * tm
    pad_lo = jnp.concatenate([off + cnt, (n_used * tm).reshape(1)]).astype(jnp.int32)
    pad_hi = jnp.concatenate([off + tiles_e * tm, jnp.full((1,), n_slots, jnp.int32)]).astype(jnp.int32)
    fmap = _slot_map(jnp.concatenate([pos1, pos2]), pad_lo, pad_hi, n_slots)
    n_used_arr = n_used.reshape(1).astype(jnp.int32)

    y = _moe_experts(h2, fmap.reshape(n_tiles, 1, tm), tile_expert, n_used_arr,
                     moe_w_gate[0], moe_w_up[0], moe_w_down[0], n_tiles, 2 * n + 2 * tm)
    out = _combine(y, x3.reshape(n, D), route[2:4].T, mod, final_gain, s)
    return out.reshape(b, s, D)
```

```python
import functools

import numpy as np
import jax
import jax.numpy as jnp
from jax import lax
from jax.experimental import pallas as pl
from jax.experimental.pallas import tpu as pltpu

F32 = jnp.float32
BF16 = jnp.bfloat16

D = 1024
GRID_W = 64
EPS = 1e-6
NEG_INF = -1e30
HEAD_DIM = 64
N_Q_HEADS = 8
BLOCK = 128
ATT_W = 512
KV_W = 128
GM_W = 512
IN_W = 1792
POOL_SIZES = (2, 4, 8, 16)
POOL_GD = 256
POOL_HALO = 16
N_EXPERTS = 8
ROPE_BASE = 10000.0
LANES = 128
SLAB = D // LANES
PACK = SLAB // 2
SQRT_2_OVER_PI = 0.7978845608028654

TM_IN = 1024
TM_FFN = 512
TM_POOL = 512
TM_MOE = 512
MOE_SPLIT = 2
W_CHUNK = 256
W_ROWS_IN = 128
W_ROWS_OUT = 512
W_RING_IN = 8
W_RING_OUT = 2
MOE_VMEM_LIMIT = 60 * 1024 * 1024
TC_COMB = 1024
Q_BLOCKS = 8
VMEM_LIMIT = 56 * 1024 * 1024


def _cparams(sem, vmem=None):
    return pltpu.CompilerParams(dimension_semantics=sem, vmem_limit_bytes=vmem)


def _modulate(xf, shift, scale):
    ms = jnp.mean(xf * xf, axis=-1, keepdims=True)
    return xf * lax.rsqrt(ms + EPS) * (1.0 + scale) + shift


def _sigmoid(z):
    return 1.0 / (1.0 + jnp.exp(-z))


def _rows_to_slabs(val, slab_ref):
    rows = val.shape[0]
    for cix in range(SLAB):
        slab_ref[pl.ds(cix, rows, stride=SLAB), :] = val[:, cix * LANES:(cix + 1) * LANES]


def _slabs_to_rows(slab_ref, rows):
    return jnp.concatenate([slab_ref[pl.ds(cix, rows, stride=SLAB), :] for cix in range(SLAB)], axis=1)


def _pack_rows(val, pack_ref):
    rows = val.shape[0]
    bits = lambda v: lax.bitcast_convert_type(v.astype(BF16).astype(F32), jnp.uint32)
    for cix in range(PACK):
        hi = bits(val[:, cix * LANES:(cix + 1) * LANES])
        lo = bits(val[:, D // 2 + cix * LANES:D // 2 + (cix + 1) * LANES])
        pack_ref[pl.ds(cix, rows, stride=PACK), :] = hi | (lo >> 16)


def _unpack_rows(pack_ref, rows):
    words = [pack_ref[pl.ds(cix, rows, stride=PACK), :] for cix in range(PACK)]
    his = [lax.bitcast_convert_type(w & jnp.uint32(0xFFFF0000), F32) for w in words]
    los = [lax.bitcast_convert_type(w << 16, F32) for w in words]
    return jnp.concatenate(his + los, axis=1)


def _ada_kernel(c_ref, w_ref, b_ref, o_ref):
    c = c_ref[...]
    s = c * _sigmoid(c)
    o_ref[0] = jnp.dot(s.astype(BF16), w_ref[0].astype(BF16), preferred_element_type=F32) + b_ref[0]


def _ada_mod(cvec, w_ada, b_ada):
    depth, _, n6 = w_ada.shape
    tn = 1536
    out = pl.pallas_call(
        _ada_kernel,
        grid=(depth, n6 // tn),
        in_specs=[
            pl.BlockSpec((8, D), lambda l, j: (0, 0)),
            pl.BlockSpec((1, D, tn), lambda l, j: (l, 0, j)),
            pl.BlockSpec((1, 1, tn), lambda l, j: (l, 0, j)),
        ],
        out_specs=pl.BlockSpec((1, 8, tn), lambda l, j: (l, 0, j)),
        out_shape=jax.ShapeDtypeStruct((depth, 8, n6), F32),
        compiler_params=_cparams(("arbitrary", "arbitrary")),
        name="ada_mod",
    )(cvec, w_ada, b_ada.reshape(depth, 1, n6))
    return out.reshape(depth, 8, 6, D)


def _rope(t, cs, sn, first_half):
    fwd = pltpu.roll(t, LANES - 16, axis=1)
    bwd = pltpu.roll(t, 16, axis=1)
    return t * cs + jnp.where(first_half, fwd, bwd) * sn


def _inproj_kernel(x_ref, mod_ref, w_ref, gain_ref, cos_ref, sin_ref,
                   q_ref, k_ref, ksw_ref, v_ref, vsw_ref, u_ref, vg_ref):
    mod = mod_ref[0, 0]
    h = _modulate(x_ref[0], mod[0:1], mod[1:2]).astype(BF16)
    proj = jnp.dot(h, w_ref[...], preferred_element_type=F32)
    cs = cos_ref[...]
    sn = sin_ref[...]
    lane = lax.broadcasted_iota(jnp.int32, cs.shape, 1)
    first_half = (lane & 16) == 0
    for cix in range(ATT_W // LANES):
        t = proj[:, cix * LANES:(cix + 1) * LANES]
        q_ref[0, :, cix * LANES:(cix + 1) * LANES] = (
            _rope(t, cs, sn, first_half) * (HEAD_DIM ** -0.5)).astype(BF16)
    kr = _rope(proj[:, ATT_W:ATT_W + KV_W], cs, sn, first_half)
    k_ref[0] = kr.astype(BF16)
    ksw_ref[0] = pltpu.roll(kr, HEAD_DIM, axis=1).astype(BF16)
    vv = proj[:, ATT_W + KV_W:ATT_W + 2 * KV_W]
    v_ref[0] = vv.astype(BF16)
    vsw_ref[0] = pltpu.roll(vv, HEAD_DIM, axis=1).astype(BF16)
    z = proj[:, ATT_W + 2 * KV_W:]
    g = z * (0.5 * (1.0 + jnp.tanh(SQRT_2_OVER_PI * (z + 0.044715 * (z * z * z)))))
    u_ref[0] = g[:, :GM_W].astype(BF16)
    vg = g[:, GM_W:]
    ms = jnp.mean(vg * vg, axis=-1, keepdims=True)
    vg_ref[0] = (vg * lax.rsqrt(ms + EPS) * gain_ref[...]).astype(BF16)


def _in_proj(x, mod, w_in_bf, gm_gain, cos_t, sin_t):
    b, s, _ = x.shape
    tm = TM_IN
    row = lambda w: pl.BlockSpec((1, tm, w), lambda bi, i: (bi, i, 0))
    outs = pl.pallas_call(
        _inproj_kernel,
        grid=(b, s // tm),
        in_specs=[
            row(D),
            pl.BlockSpec((1, 1, 6, D), lambda bi, i: (0, bi, 0, 0)),
            pl.BlockSpec((D, IN_W), lambda bi, i: (0, 0)),
            pl.BlockSpec((1, GM_W), lambda bi, i: (0, 0)),
            pl.BlockSpec((tm, LANES), lambda bi, i: (i, 0)),
            pl.BlockSpec((tm, LANES), lambda bi, i: (i, 0)),
        ],
        out_specs=[row(ATT_W), row(KV_W), row(KV_W), row(KV_W), row(KV_W), row(GM_W), row(GM_W)],
        out_shape=[jax.ShapeDtypeStruct((b, s, w), BF16)
                   for w in (ATT_W, KV_W, KV_W, KV_W, KV_W, GM_W, GM_W)],
        compiler_params=_cparams(("arbitrary", "arbitrary"), VMEM_LIMIT),
        name="in_proj",
    )(x, mod, w_in_bf, gm_gain.reshape(1, GM_W), cos_t, sin_t)
    return outs


def _ctx_kernel(c_ref, mod_ref, w_ref, k_ref, ksw_ref, v_ref, vsw_ref):
    mod = mod_ref[0, 0]
    h = _modulate(c_ref[0], mod[0:1], mod[1:2]).astype(BF16)
    kv = jnp.dot(h, w_ref[...], preferred_element_type=F32)
    kk = kv[:, :KV_W]
    vv = kv[:, KV_W:]
    k_ref[0] = kk.astype(BF16)
    ksw_ref[0] = pltpu.roll(kk, HEAD_DIM, axis=1).astype(BF16)
    v_ref[0] = vv.astype(BF16)
    vsw_ref[0] = pltpu.roll(vv, HEAD_DIM, axis=1).astype(BF16)


def _ctx_kv(ctx, mod, w_in_bf):
    b, l, _ = ctx.shape
    spec = pl.BlockSpec((1, l, KV_W), lambda bi: (bi, 0, 0))
    return pl.pallas_call(
        _ctx_kernel,
        grid=(b,),
        in_specs=[
            pl.BlockSpec((1, l, D), lambda bi: (bi, 0, 0)),
            pl.BlockSpec((1, 1, 6, D), lambda bi: (0, b, 0, 0)),
            pl.BlockSpec((D, 2 * KV_W), lambda bi: (0, ATT_W // (2 * KV_W))),
        ],
        out_specs=[spec] * 4,
        out_shape=[jax.ShapeDtypeStruct((b, l, KV_W), BF16)] * 4,
        compiler_params=_cparams(("arbitrary",)),
        name="ctx_kv",
    )(ctx, mod, w_in_bf)


def _attn_kernel(sink_ref, q_ref, kp_ref, kc_ref, kn_ref, ksp_ref, ksc_ref, ksn_ref,
                 vp_ref, vc_ref, vn_ref, vsp_ref, vsc_ref, vsn_ref,
                 kx_ref, ksx_ref, vx_ref, vsx_ref,
                 u_ref, vg_ref, wcat_ref, bs_ref, wout_ref, x_ref, mod_ref, o_ref):
    n = pl.program_id(1)
    nblk = pl.num_programs(1) * Q_BLOCKS
    lane = lax.broadcasted_iota(jnp.int32, (1, LANES), 1)
    low = lane < HEAD_DIM
    zero = jnp.zeros((), BF16)

    def variants(a0, a1):
        return ((jnp.where(low, a0, zero), jnp.where(low, zero, a1)),
                (jnp.where(low, a1, zero), jnp.where(low, zero, a0)))

    cat = lambda refs: jnp.concatenate([r[0] for r in refs], axis=0)
    kb_var = variants(cat((kp_ref, kc_ref, kn_ref)), cat((ksp_ref, ksc_ref, ksn_ref)))
    vb_var = variants(cat((vp_ref, vc_ref, vn_ref)), cat((vsp_ref, vsc_ref, vsn_ref)))
    kx_var = variants(kx_ref[0], ksx_ref[0])
    vx_var = variants(vx_ref[0], vsx_ref[0])

    row = lax.broadcasted_iota(jnp.int32, (2 * BLOCK, BLOCK), 0) & (BLOCK - 1)
    col = lax.broadcasted_iota(jnp.int32, (2 * BLOCK, BLOCK), 1)
    top = lax.broadcasted_iota(jnp.int32, (2 * BLOCK, 1), 0) < BLOCK
    nt_dims = (((1,), (1,)), ((), ()))

    q = q_ref[0]
    att_blocks = [[None] * 4 for _ in range(Q_BLOCKS)]
    for kvh in range(2):
        qst = jnp.concatenate(
            [q[qb * BLOCK:(qb + 1) * BLOCK, pr * LANES:(pr + 1) * LANES]
             for qb in range(Q_BLOCKS) for pr in (2 * kvh, 2 * kvh + 1)], axis=0)
        accs = [None] * Q_BLOCKS
        for half in range(2):
            sk = jnp.where(top, sink_ref[4 * kvh + half], sink_ref[4 * kvh + 2 + half])
            s_ctx = lax.dot_general(qst, kx_var[kvh][half], nt_dims, preferred_element_type=F32)
            o_band, p_ctx, dens = [], [], []
            for qb in range(Q_BLOCKS):
                g = n * Q_BLOCKS + qb
                qrows = qst[qb * 2 * BLOCK:(qb + 1) * 2 * BLOCK]
                sb = lax.dot_general(qrows, kb_var[kvh][half][qb * BLOCK:(qb + 3) * BLOCK], nt_dims,
                                     preferred_element_type=F32)
                s0 = jnp.where((col >= row) & (g > 0), sb[:, :BLOCK], NEG_INF)
                s1 = sb[:, BLOCK:2 * BLOCK]
                s2 = jnp.where((col <= row) & (g < nblk - 1), sb[:, 2 * BLOCK:], NEG_INF)
                sc = s_ctx[qb * 2 * BLOCK:(qb + 1) * 2 * BLOCK]
                ctx_blocks = [sc[:, cb * BLOCK:(cb + 1) * BLOCK] for cb in range(sc.shape[1] // BLOCK)]
                m = functools.reduce(jnp.maximum, [s0, s1, s2] + ctx_blocks)
                m = jnp.maximum(jnp.max(m, axis=-1, keepdims=True), sk)
                p0, p1, p2, pc = (jnp.exp(t - m) for t in (s0, s1, s2, sc))
                psum = functools.reduce(
                    jnp.add, [p0, p1, p2] + [pc[:, cb * BLOCK:(cb + 1) * BLOCK] for cb in range(len(ctx_blocks))])
                den = jnp.sum(psum, axis=-1, keepdims=True) + jnp.exp(sk - m)
                pb = jnp.concatenate([p0, p1, p2], axis=1).astype(BF16)
                o_band.append(jnp.dot(pb, vb_var[kvh][half][qb * BLOCK:(qb + 3) * BLOCK],
                                      preferred_element_type=F32))
                p_ctx.append(pc.astype(BF16))
                dens.append(den)
            o_ctx = jnp.dot(jnp.concatenate(p_ctx, axis=0), vx_var[kvh][half], preferred_element_type=F32)
            for qb in range(Q_BLOCKS):
                o = (o_band[qb] + o_ctx[qb * 2 * BLOCK:(qb + 1) * 2 * BLOCK]) / dens[qb]
                accs[qb] = o if accs[qb] is None else accs[qb] + o
        for qb in range(Q_BLOCKS):
            att_blocks[qb][2 * kvh] = accs[qb][:BLOCK]
            att_blocks[qb][2 * kvh + 1] = accs[qb][BLOCK:]

    u = u_ref[0]
    vg = vg_ref[0]
    bs = bs_ref[...]
    gm_blocks = [[None] * 4 for _ in range(Q_BLOCKS)]
    for j in range(GM_W // LANES):
        chunks = [vg[c * BLOCK:(c + 1) * BLOCK, j * LANES:(j + 1) * LANES] for c in range(Q_BLOCKS)]
        rhs = jnp.concatenate(
            [jnp.concatenate([jnp.where(low, v, zero) for v in chunks], axis=1),
             jnp.concatenate([jnp.where(low, zero, v) for v in chunks], axis=1)], axis=0)
        mixed = jnp.dot(wcat_ref[j], rhs, preferred_element_type=F32)
        bias = jnp.where(low, bs[:, 2 * j:2 * j + 1], bs[:, 2 * j + 1:2 * j + 2])
        for c in range(Q_BLOCKS):
            gm_blocks[c][j] = (u[c * BLOCK:(c + 1) * BLOCK, j * LANES:(j + 1) * LANES].astype(F32)
                               * (mixed[:, c * LANES:(c + 1) * LANES] + bias))

    mix = jnp.concatenate([jnp.concatenate(att_blocks[c] + gm_blocks[c], axis=1) for c in range(Q_BLOCKS)],
                          axis=0).astype(BF16)
    y = jnp.dot(mix, wout_ref[...], preferred_element_type=F32)
    mod = mod_ref[0, 0]
    o_ref[0] = x_ref[0] + mod[2:3] * y


def _attn_mixer(x, mod, sink, q, k, ksw, v, vsw, kx, ksx, vx, vsx, u, vg, wcat_bf, bs_t, wout_bf):
    b, s, _ = x.shape
    tq = Q_BLOCKS * BLOCK
    nb = s // BLOCK
    l = kx.shape[1]
    cur = lambda w: pl.BlockSpec((1, tq, w), lambda bi, n: (bi, n, 0))
    prv = lambda w: pl.BlockSpec((1, BLOCK, w), lambda bi, n: (bi, jnp.maximum(n * Q_BLOCKS - 1, 0), 0))
    nxt = lambda w: pl.BlockSpec((1, BLOCK, w), lambda bi, n: (bi, jnp.minimum((n + 1) * Q_BLOCKS, nb - 1), 0))
    cx = pl.BlockSpec((1, l, KV_W), lambda bi, n: (bi, 0, 0))
    return pl.pallas_call(
        _attn_kernel,
        grid=(b, s // tq),
        in_specs=[
            pl.BlockSpec(memory_space=pltpu.SMEM),
            cur(ATT_W),
            prv(KV_W), cur(KV_W), nxt(KV_W), prv(KV_W), cur(KV_W), nxt(KV_W),
            prv(KV_W), cur(KV_W), nxt(KV_W), prv(KV_W), cur(KV_W), nxt(KV_W),
            cx, cx, cx, cx,
            cur(GM_W), cur(GM_W),
            pl.BlockSpec((4, BLOCK, 2 * BLOCK), lambda bi, n: (0, 0, 0)),
            pl.BlockSpec((BLOCK, 8), lambda bi, n: (0, 0)),
            pl.BlockSpec((D, D), lambda bi, n: (0, 0)),
            cur(D),
            pl.BlockSpec((1, 1, 6, D), lambda bi, n: (0, bi, 0, 0)),
        ],
        out_specs=cur(D),
        out_shape=jax.ShapeDtypeStruct((b, s, D), F32),
        compiler_params=_cparams(("arbitrary", "arbitrary"), VMEM_LIMIT),
        name="attn_gmlp_out",
    )(sink, q, k, k, k, ksw, ksw, ksw, v, v, v, vsw, vsw, vsw, kx, ksx, vx, vsx,
      u, vg, wcat_bf, bs_t, wout_bf, x, mod)


def _ffn_kernel(x_ref, mod_ref, wg_ref, wu_ref, wd_ref, o_ref):
    mod = mod_ref[0, 0]
    xf = x_ref[...]
    h = _modulate(xf, mod[3:4], mod[4:5]).astype(BF16)
    g = jnp.dot(h, wg_ref[...], preferred_element_type=F32)
    up = jnp.dot(h, wu_ref[...], preferred_element_type=F32)
    a = (g * _sigmoid(g) * up).astype(BF16)
    o_ref[...] = xf + mod[5:6] * jnp.dot(a, wd_ref[...], preferred_element_type=F32)


def _dense_ffn(x2d, mod, wg, wu, wd, seq):
    n = x2d.shape[0]
    f = wg.shape[1]
    tm = TM_FFN
    per_b = seq // tm
    resident = lambda shp: pl.BlockSpec(shp, lambda i: (0, 0), pipeline_mode=pl.Buffered(1))
    return pl.pallas_call(
        _ffn_kernel,
        grid=(n // tm,),
        in_specs=[
            pl.BlockSpec((tm, D), lambda i: (i, 0)),
            pl.BlockSpec((1, 1, 6, D), lambda i: (0, i // per_b, 0, 0)),
            resident((D, f)), resident((D, f)), resident((f, D)),
        ],
        out_specs=pl.BlockSpec((tm, D), lambda i: (i, 0)),
        out_shape=jax.ShapeDtypeStruct((n, D), F32),
        compiler_params=_cparams(("arbitrary",), VMEM_LIMIT),
        name="dense_ffn",
    )(x2d, mod, wg, wu, wd)


def _pool_route_kernel(x_ref, xp_ref, xn_ref, mod_ref, band_ref, pw_ref, psc_ref, wr_hi_ref, wr_lo_ref,
                       tri_ref, x3_ref, h2_ref, route_ref, cnt_ref, hext, carry):
    bi = pl.program_id(0)
    i = pl.program_id(1)
    ni = pl.num_programs(1)
    tm = x_ref.shape[1]
    seq = tm * ni
    mod = mod_ref[0, 0]

    @pl.when((bi == 0) & (i == 0))
    def _():
        carry[...] = jnp.zeros_like(carry)

    xf = x_ref[0]
    hp = _modulate(xp_ref[0], mod[0:1], mod[1:2])
    hn = _modulate(xn_ref[0], mod[0:1], mod[1:2])
    hext[0:POOL_HALO] = jnp.where(i > 0, hp, 0.0).astype(BF16)
    h_main = _modulate(xf, mod[0:1], mod[1:2])
    hext[POOL_HALO:POOL_HALO + tm] = h_main.astype(BF16)
    hext[POOL_HALO + tm:] = jnp.where(i < ni - 1, hn, 0.0).astype(BF16)

    t_local = lax.broadcasted_iota(jnp.int32, (BLOCK, 1), 0)
    ys = []
    for gi, w in enumerate(POOL_SIZES):
        lo_off = -(w // 2)
        hi_off = w - 1 - w // 2
        cols = slice(gi * POOL_GD, (gi + 1) * POOL_GD)
        outs = []
        for sb in range(tm // BLOCK):
            r0 = sb * BLOCK
            win = jnp.dot(band_ref[gi], hext[r0:r0 + BLOCK + 2 * POOL_HALO, cols],
                          preferred_element_type=F32)
            t = i * tm + r0 + t_local
            cnt = (jnp.minimum(t + hi_off, seq - 1) - jnp.maximum(t + lo_off, 0) + 1).astype(F32)
            diff = win / cnt - h_main[r0:r0 + BLOCK, cols]
            outs.append(diff.astype(BF16))
        dg = jnp.concatenate(outs, axis=0)
        ys.append(jnp.dot(dg, pw_ref[gi], preferred_element_type=F32))
    y = jnp.concatenate(ys, axis=1) * psc_ref[...]
    x3 = xf + mod[2:3] * y
    x3_ref[0] = x3

    h2 = _modulate(x3, mod[3:4], mod[4:5])
    _rows_to_slabs(h2, h2_ref)
    h_hi = h2.astype(BF16)
    h_lo = (h2 - h_hi.astype(F32)).astype(BF16)
    logits = (jnp.dot(h_hi, wr_hi_ref[...], preferred_element_type=F32)
              + jnp.dot(h_hi, wr_lo_ref[...], preferred_element_type=F32)
              + jnp.dot(h_lo, wr_hi_ref[...], preferred_element_type=F32))
    lane = lax.broadcasted_iota(jnp.int32, (tm, LANES), 1)
    lane_f = lane.astype(F32)
    neg = -jnp.inf
    lg = jnp.where(lane < N_EXPERTS, logits, neg)
    m1 = jnp.max(lg, axis=-1, keepdims=True)
    i1 = jnp.min(jnp.where(lg == m1, lane_f, float(LANES)), axis=-1, keepdims=True)
    oh1 = lane_f == i1
    lg2 = jnp.where(oh1, neg, lg)
    m2 = jnp.max(lg2, axis=-1, keepdims=True)
    i2 = jnp.min(jnp.where(lg2 == m2, lane_f, float(LANES)), axis=-1, keepdims=True)
    oh2 = lane_f == i2
    e = jnp.exp(m2 - m1)
    w1 = 1.0 / (1.0 + e)
    w2 = e / (1.0 + e)
    oh = jnp.where(oh1 | oh2, 1.0, 0.0)
    before = jnp.dot(tri_ref[...], oh.astype(BF16), preferred_element_type=F32) + carry[...]
    r1 = jnp.sum(jnp.where(oh1, before, 0.0), axis=-1, keepdims=True)
    r2 = jnp.sum(jnp.where(oh2, before, 0.0), axis=-1, keepdims=True)
    carry[...] = carry[...] + jnp.sum(oh, axis=0, keepdims=True)
    cnt_ref[...] = carry[...]
    info = jnp.where(lane == 0, i1, jnp.where(lane == 1, i2, jnp.where(lane == 2, w1, jnp.where(
        lane == 3, w2, jnp.where(lane == 4, r1, jnp.where(lane == 5, r2, 0.0))))))
    route_ref[...] = info.T[0:8, :]


def _pool_route(x, mod, band, pw_bf, pool_scale, wr_hi, wr_lo, tri):
    b, s, _ = x.shape
    tm = TM_POOL
    ni = s // tm
    hb = tm // POOL_HALO
    row = pl.BlockSpec((1, tm, D), lambda bi, i: (bi, i, 0))
    const2 = lambda shp: pl.BlockSpec(shp, lambda bi, i: (0,) * len(shp))
    return pl.pallas_call(
        _pool_route_kernel,
        grid=(b, ni),
        in_specs=[
            row,
            pl.BlockSpec((1, POOL_HALO, D), lambda bi, i: (bi, jnp.maximum(i * hb - 1, 0), 0)),
            pl.BlockSpec((1, POOL_HALO, D), lambda bi, i: (bi, jnp.minimum((i + 1) * hb, s // POOL_HALO - 1), 0)),
            pl.BlockSpec((1, 1, 6, D), lambda bi, i: (1, bi, 0, 0)),
            const2(band.shape), const2(pw_bf.shape), const2((1, D)),
            const2(wr_hi.shape), const2(wr_lo.shape), const2(tri.shape),
        ],
        out_specs=[row,
                   pl.BlockSpec((tm * SLAB, LANES), lambda bi, i: (bi * ni + i, 0)),
                   pl.BlockSpec((8, tm), lambda bi, i: (0, bi * ni + i)),
                   pl.BlockSpec((1, LANES), lambda bi, i: (0, 0))],
        out_shape=[jax.ShapeDtypeStruct((b, s, D), F32), jax.ShapeDtypeStruct((b * s * SLAB, LANES), F32),
                   jax.ShapeDtypeStruct((8, b * s), F32), jax.ShapeDtypeStruct((1, LANES), F32)],
        scratch_shapes=[pltpu.VMEM((tm + 2 * POOL_HALO, D), BF16), pltpu.VMEM((1, LANES), F32)],
        compiler_params=_cparams(("arbitrary", "arbitrary"), VMEM_LIMIT),
        name="pool_route",
    )(x, x, x, mod, band, pw_bf, pool_scale.reshape(1, D), wr_hi, wr_lo, tri)


def _slotmap_kernel(pos_ref, lo_ref, hi_ref, o_ref):
    n_pairs = pos_ref.shape[0]
    spare_mask = 2 * TM_MOE - 1
    for e in range(lo_ref.shape[0]):
        def fill(p, c):
            o_ref[p] = n_pairs + (p & spare_mask)
            return c
        lax.fori_loop(lo_ref[e], hi_ref[e], fill, 0)

    def place(f, c):
        o_ref[pos_ref[f]] = f
        return c
    lax.fori_loop(0, n_pairs, place, 0, unroll=32)


def _slot_map(pos_flat, lo, hi, n_slots):
    smem = pl.BlockSpec(memory_space=pltpu.SMEM)
    return pl.pallas_call(
        _slotmap_kernel,
        in_specs=[smem, smem, smem],
        out_specs=smem,
        out_shape=jax.ShapeDtypeStruct((n_slots,), jnp.int32),
        name="moe_slot_map",
    )(pos_flat, lo, hi)


def _moe_kernel(te_ref, nused_ref, fnext_ref, fprev_ref, f0_ref, h_hbm, wg_hbm, wu_hbm, wd_hbm,
                y_hbm, wg_res, wu_res, wd_res, stg_in, stg_out, xbuf, xb, act, acc, stage, gsem, ssem, wsem):
    i = pl.program_id(0)
    nt = pl.num_programs(0)
    used_tiles = nused_ref[0]
    tm = xb.shape[0]
    f_dim = wg_res.shape[1]
    tok_mask = h_hbm.shape[0] // SLAB - 1
    tile_rows = tm * SLAB
    out_rows = tm * PACK

    def slab(ix, width=SLAB):
        return pl.ds(pl.multiple_of(ix * width, width), width)

    def gather_row(fref, r, slot, zero=0):
        tok = fref[0, 0, r + zero] & tok_mask
        return pltpu.make_async_copy(h_hbm.at[slab(tok)], xbuf.at[slot, slab(r)], gsem.at[slot])

    def scatter_row(fref, r, slot, zero=0):
        return pltpu.make_async_copy(stage.at[slot, slab(r, PACK)], y_hbm.at[slab(fref[0, 0, r + zero], PACK)],
                                     ssem.at[slot])

    def gather_all(slot):
        return pltpu.make_async_copy(h_hbm.at[pl.ds(0, tile_rows)], xbuf.at[slot], gsem.at[slot])

    def scatter_all(slot):
        return pltpu.make_async_copy(stage.at[slot], y_hbm.at[pl.ds(0, out_rows)], ssem.at[slot])

    cur = i % 2
    used = i < used_tiles

    @pl.when(i == 0)
    def _():
        stage[...] = jnp.zeros_like(stage)
        spare = y_hbm.shape[0] - 2 * out_rows
        fills = [pltpu.make_async_copy(stage.at[sl], y_hbm.at[pl.ds(spare + sl * out_rows, out_rows)],
                                       ssem.at[sl])
                 for sl in range(2)]
        for cp in fills:
            cp.start()
        for cp in fills:
            cp.wait()

    @pl.when((i == 0) & used)
    def _():
        def prime(r, c):
            gather_row(f0_ref, r, 0).start()
            return c
        lax.fori_loop(0, tm, prime, 0)

    expert = te_ref[i]
    new_expert = used & ((i == 0) | (expert != te_ref[jnp.maximum(i - 1, 0)]))

    has_next = i + 1 < used_tiles
    has_prev = (i >= 1) & (i - 1 < used_tiles)
    steady = (i >= 2) & has_next

    @pl.when(used)
    def _():
        gather_all(cur).wait()

    @pl.when((i >= 2) & (i - 2 < used_tiles))
    def _():
        scatter_all(cur).wait()

    half = f_dim // MOE_SPLIT
    n_piece = half // W_CHUNK
    n_groups = MOE_SPLIT * (n_piece + 1)

    def tick(v):
        bits = jnp.max(lax.bitcast_convert_type(v[0:SLAB, 0:LANES], jnp.int32))
        return lax.shift_right_logical(lax.shift_right_logical(bits, 16), 16)

    def expert_ffn(issue_group):
        for cix in range(SLAB):
            xb[:, cix * LANES:(cix + 1) * LANES] = xbuf.at[cur][pl.ds(cix, tm, stride=SLAB), :].astype(BF16)
        xv = xb[...]
        out = None
        issue_group(0, 0)
        k = 1
        for hf in range(MOE_SPLIT):
            for c in range(n_piece):
                cs = slice(hf * half + c * W_CHUNK, hf * half + (c + 1) * W_CHUNK)
                g = jnp.dot(xv, wg_res[:, cs], preferred_element_type=F32)
                up = jnp.dot(xv, wu_res[:, cs], preferred_element_type=F32)
                act[:, c * W_CHUNK:(c + 1) * W_CHUNK] = (g * _sigmoid(g) * up).astype(BF16)
                issue_group(k, tick(g))
                k += 1
            part = jnp.dot(act[...], wd_res[hf * half:(hf + 1) * half, :], preferred_element_type=F32)
            if hf < MOE_SPLIT - 1:
                acc[...] = part if out is None else acc[...] + part
                out = acc
                issue_group(k, tick(part))
                k += 1
            else:
                _pack_rows(part if out is None else acc[...] + part, stage.at[cur])

    def chunk_plan():
        n_in = D // W_ROWS_IN
        plan = [(w, res, stg_in, W_RING_IN, 0, W_ROWS_IN, c, wi * n_in + c)
                for wi, (w, res) in enumerate(((wg_hbm, wg_res), (wu_hbm, wu_res))) for c in range(n_in)]
        return plan + [(wd_hbm, wd_res, stg_out, W_RING_OUT, W_RING_IN, W_ROWS_OUT, c, c)
                       for c in range(f_dim // W_ROWS_OUT)]

    def chunk_copy(entry, ex):
        w_hbm, _, stg, ring, sem0, rows, c, j = entry
        return pltpu.make_async_copy(w_hbm.at[ex, pl.ds(c * rows, rows), :], stg.at[j % ring],
                                     wsem.at[sem0 + j % ring])

    def start_first_chunks(ex):
        for entry in chunk_plan():
            if entry[7] < entry[3]:
                chunk_copy(entry, ex).start()

    @pl.when(new_expert)
    def _():
        plan = chunk_plan()

        @pl.when(i == 0)
        def _():
            start_first_chunks(expert)

        for k, entry in enumerate(plan):
            _, res, stg, ring, _, rows, c, j = entry
            chunk_copy(entry, expert).wait()
            res[c * rows:(c + 1) * rows, :] = stg[j % ring].astype(BF16)
            later = [e for e in plan[k + 1:] if e[2] is stg and e[7] == j + ring]
            if later:
                chunk_copy(later[0], expert).start()

    next_expert = te_ref[jnp.minimum(i + 1, nt - 1)]

    @pl.when(has_next & (next_expert != expert))
    def _():
        start_first_chunks(next_expert)

    @pl.when(steady)
    def _():
        def issue_group(k, zero):
            for r in range(k * tm // n_groups, (k + 1) * tm // n_groups):
                gather_row(fnext_ref, r, 1 - cur, zero).start()
                scatter_row(fprev_ref, r, 1 - cur, zero).start()
        expert_ffn(issue_group)

    @pl.when(jnp.logical_not(steady))
    def _():
        @pl.when(used)
        def _():
            expert_ffn(lambda k, zero: None)

        @pl.when(has_next)
        def _():
            def issue(r, c):
                gather_row(fnext_ref, r, 1 - cur).start()
                return c
            lax.fori_loop(0, tm, issue, 0)

        @pl.when(has_prev)
        def _():
            def issue(r, c):
                scatter_row(fprev_ref, r, 1 - cur).start()
                return c
            lax.fori_loop(0, tm, issue, 0)

    @pl.when((i == nt - 1) & (nt - 2 < used_tiles))
    def _():
        scatter_all(1 - cur).wait()


def _moe_experts(h_slabs, fmap, tile_expert, n_used, wg, wu, wd, n_tiles, y_rows):
    f = wg.shape[2]
    tm = TM_MOE
    fblk = lambda imap: pl.BlockSpec((1, 1, tm), imap, memory_space=pltpu.SMEM)
    hbm = pl.BlockSpec(memory_space=pl.ANY)
    return pl.pallas_call(
        _moe_kernel,
        grid_spec=pltpu.PrefetchScalarGridSpec(
            num_scalar_prefetch=2,
            grid=(n_tiles,),
            in_specs=[
                fblk(lambda i, te, nu: (jnp.minimum(i + 1, n_tiles - 1), 0, 0)),
                fblk(lambda i, te, nu: (jnp.maximum(i - 1, 0), 0, 0)),
                fblk(lambda i, te, nu: (0, 0, 0)),
                hbm, hbm, hbm, hbm,
            ],
            out_specs=hbm,
            scratch_shapes=[pltpu.VMEM((D, f), BF16), pltpu.VMEM((D, f), BF16), pltpu.VMEM((f, D), BF16),
                            pltpu.VMEM((W_RING_IN, W_ROWS_IN, f), F32), pltpu.VMEM((W_RING_OUT, W_ROWS_OUT, D), F32),
                            pltpu.VMEM((2, tm * SLAB, LANES), F32), pltpu.VMEM((tm, D), BF16),
                            pltpu.VMEM((tm, f // MOE_SPLIT), BF16), pltpu.VMEM((tm, D), F32),
                            pltpu.VMEM((2, tm * PACK, LANES), jnp.uint32),
                            pltpu.SemaphoreType.DMA((2,)), pltpu.SemaphoreType.DMA((2,)),
                            pltpu.SemaphoreType.DMA((W_RING_IN + W_RING_OUT,))],
        ),
        out_shape=jax.ShapeDtypeStruct((y_rows * PACK, LANES), jnp.uint32),
        compiler_params=_cparams(("arbitrary",), MOE_VMEM_LIMIT),
        name="moe_experts",
    )(tile_expert, n_used, fmap, fmap, fmap, h_slabs, wg, wu, wd)


def _combine_kernel(y1_ref, y2_ref, x_ref, w_ref, mod_ref, gain_ref, o_ref):
    w = w_ref[...]
    rows = x_ref.shape[0]
    moe = w[:, 0:1] * _unpack_rows(y1_ref, rows) + w[:, 1:2] * _unpack_rows(y2_ref, rows)
    mod = mod_ref[0, 0]
    x4 = x_ref[...] + mod[5:6] * moe
    ms = jnp.mean(x4 * x4, axis=-1, keepdims=True)
    o_ref[...] = x4 * lax.rsqrt(ms + EPS) * gain_ref[...]


def _combine(y, x3_2d, wts, mod, final_gain, seq):
    n = x3_2d.shape[0]
    tc = TC_COMB
    nt = n // tc
    per_b = seq // tc
    return pl.pallas_call(
        _combine_kernel,
        grid=(nt,),
        in_specs=[
            pl.BlockSpec((tc * PACK, LANES), lambda i: (i, 0)),
            pl.BlockSpec((tc * PACK, LANES), lambda i: (i + nt, 0)),
            pl.BlockSpec((tc, D), lambda i: (i, 0)),
            pl.BlockSpec((tc, 2), lambda i: (i, 0)),
            pl.BlockSpec((1, 1, 6, D), lambda i: (1, i // per_b, 0, 0)),
            pl.BlockSpec((1, D), lambda i: (0, 0)),
        ],
        out_specs=pl.BlockSpec((tc, D), lambda i: (i, 0)),
        out_shape=jax.ShapeDtypeStruct((n, D), F32),
        compiler_params=_cparams(("arbitrary",)),
        name="moe_combine",
    )(y, y, x3_2d, wts, mod, final_gain.reshape(1, D))


def _rope_tables(seq):
    rows = seq // GRID_W
    row_pos = jnp.repeat(jnp.arange(rows, dtype=F32), GRID_W)
    col_pos = jnp.tile(jnp.arange(GRID_W, dtype=F32), rows)
    axis_dim = HEAD_DIM // 2
    inv_freq = ROPE_BASE ** (-jnp.arange(0, axis_dim, 2, dtype=F32) / axis_dim)
    ar = row_pos[:, None] * inv_freq
    ac = col_pos[:, None] * inv_freq
    cos64 = jnp.concatenate([jnp.cos(ar), jnp.cos(ar), jnp.cos(ac), jnp.cos(ac)], axis=1)
    sin64 = jnp.concatenate([-jnp.sin(ar), jnp.sin(ar), -jnp.sin(ac), jnp.sin(ac)], axis=1)
    return jnp.tile(cos64, (1, 2)), jnp.tile(sin64, (1, 2))


def _band_matrices():
    r = np.arange(BLOCK)[:, None]
    c = np.arange(BLOCK + 2 * POOL_HALO)[None, :] - POOL_HALO
    mats = []
    for w in POOL_SIZES:
        lo = -(w // 2)
        hi = w - 1 - w // 2
        mats.append(((c >= r + lo) & (c <= r + hi)).astype(np.float32))
    return jnp.asarray(np.stack(mats), dtype=BF16)


def kernel(x, c, ctx, c_ctx, w_ada, b_ada, w_in, attn_sink, gm_gain, gm_w_s, gm_b_s, w_out,
           ffn_w_gate, ffn_w_up, ffn_w_down, pool_w, pool_scale, router_w,
           moe_w_gate, moe_w_up, moe_w_down, final_gain):
    b, s, _ = x.shape
    n = b * s
    assert w_ada.shape[0] == 2 and w_in.shape[0] == 1 and pool_w.shape[0] == 1
    assert s % TM_IN == 0 and s % TM_POOL == 0 and s % TM_FFN == 0 and b <= 4
    assert n & (n - 1) == 0

    cvec = jnp.concatenate([c, c_ctx[None, :], jnp.zeros((8 - b - 1, D), F32)], axis=0)
    mod = _ada_mod(cvec, w_ada, b_ada)

    cos_t, sin_t = _rope_tables(s)
    w_in_bf = w_in[0].astype(BF16)
    q, k, ksw, v, vsw, u, vg = _in_proj(x, mod, w_in_bf, gm_gain[0], cos_t, sin_t)
    kx, ksx, vx, vsx = _ctx_kv(ctx, mod, w_in_bf)
    wcat = gm_w_s[0].reshape(4, 2, BLOCK, BLOCK).transpose(0, 2, 1, 3).reshape(4, BLOCK, 2 * BLOCK).astype(BF16)
    x1 = _attn_mixer(x, mod, attn_sink[0], q, k, ksw, v, vsw, kx, ksx, vx, vsx, u, vg,
                     wcat, gm_b_s[0].T, w_out[0].astype(BF16))
    x2 = _dense_ffn(x1.reshape(n, D), mod, ffn_w_gate[0].astype(BF16), ffn_w_up[0].astype(BF16),
                    ffn_w_down[0].astype(BF16), s)

    wr = jnp.pad(router_w[0], ((0, 0), (0, LANES - N_EXPERTS)))
    wr_hi = wr.astype(BF16)
    wr_lo = (wr - wr_hi.astype(F32)).astype(BF16)
    tri = jnp.asarray(np.tril(np.ones((TM_POOL, TM_POOL), np.float32), -1), dtype=BF16)
    x3, h2, route, counts = _pool_route(x2.reshape(b, s, D), mod, _band_matrices(), pool_w[0].astype(BF16),
                                        pool_scale[0], wr_hi, wr_lo, tri)

    tm = TM_MOE
    n_tiles = (2 * n) // tm + N_EXPERTS
    cnt = counts[0, :N_EXPERTS].astype(jnp.int32)
    tiles_e = (cnt + tm - 1) // tm
    tile_end = jnp.cumsum(tiles_e)
    off = (tile_end - tiles_e) * tm
    n_used = tile_end[-1]
    tix = jnp.arange(n_tiles, dtype=jnp.int32)
    te = jnp.minimum(jnp.sum(tix[:, None] >= tile_end[None, :], axis=1), N_EXPERTS - 1).astype(jnp.int32)
    te_last = te[jnp.maximum(n_used - 1, 0)]
    tile_expert = jnp.where(tix < n_used, te, te_last)
    e1 = route[0].astype(jnp.int32)
    e2 = route[1].astype(jnp.int32)
    pos1 = off[e1] + route[4].astype(jnp.int32)
    pos2 = off[e2] + route[5].astype(jnp.int32)
    n_slots = n_tiles * tm
    pad_lo = jnp.concatenate([off + cnt, (n_used * tm).reshape(1)]).astype(jnp.int32)
    pad_hi = jnp.concatenate([off + tiles_e * tm, jnp.full((1,), n_slots, jnp.int32)]).astype(jnp.int32)
    fmap = _slot_map(jnp.concatenate([pos1, pos2]), pad_lo, pad_hi, n_slots)
    n_used_arr = n_used.reshape(1).astype(jnp.int32)

    y = _moe_experts(h2, fmap.reshape(n_tiles, 1, tm), tile_expert, n_used_arr,
                     moe_w_gate[0], moe_w_up[0], moe_w_down[0], n_tiles, 2 * n + 2 * tm)
    out = _combine(y, x3.reshape(n, D), route[2:4].T, mod, final_gain, s)
    return out.reshape(b, s, D)
```

```python
import functools

import numpy as np
import jax
import jax.numpy as jnp
from jax import lax
from jax.experimental import pallas as pl
from jax.experimental.pallas import tpu as pltpu

F32 = jnp.float32
BF16 = jnp.bfloat16

D = 1024
GRID_W = 64
EPS = 1e-6
NEG_INF = -1e30
HEAD_DIM = 64
N_Q_HEADS = 8
BLOCK = 128
ATT_W = 512
KV_W = 128
GM_W = 512
IN_W = 1792
POOL_SIZES = (2, 4, 8, 16)
POOL_GD = 256
POOL_HALO = 16
N_EXPERTS = 8
ROPE_BASE = 10000.0
LANES = 128
SLAB = D // LANES
PACK = SLAB // 2
SQRT_2_OVER_PI = 0.7978845608028654

TM_IN = 1024
TM_FFN = 1024
TM_POOL = 512
TM_MOE = 512
MOE_SPLIT = 2
W_CHUNK = 256
W_ROWS_IN = 128
W_ROWS_OUT = 512
W_RING_IN = 12
W_RING_OUT = 2
MOE_VMEM_LIMIT = 60 * 1024 * 1024
TC_COMB = 1024
Q_BLOCKS = 8
VMEM_LIMIT = 56 * 1024 * 1024


def _cparams(sem, vmem=None):
    return pltpu.CompilerParams(dimension_semantics=sem, vmem_limit_bytes=vmem)


def _modulate(xf, shift, scale):
    ms = jnp.mean(xf * xf, axis=-1, keepdims=True)
    return xf * lax.rsqrt(ms + EPS) * (1.0 + scale) + shift


def _sigmoid(z):
    return 1.0 / (1.0 + jnp.exp(-z))


def _rows_to_slabs(val, slab_ref):
    rows = val.shape[0]
    for cix in range(SLAB):
        slab_ref[pl.ds(cix, rows, stride=SLAB), :] = val[:, cix * LANES:(cix + 1) * LANES]


def _slabs_to_rows(slab_ref, rows):
    return jnp.concatenate([slab_ref[pl.ds(cix, rows, stride=SLAB), :] for cix in range(SLAB)], axis=1)


def _pack_rows(val, pack_ref):
    rows = val.shape[0]
    bits = lambda v: lax.bitcast_convert_type(v.astype(BF16).astype(F32), jnp.uint32)
    for cix in range(PACK):
        hi = bits(val[:, cix * LANES:(cix + 1) * LANES])
        lo = bits(val[:, D // 2 + cix * LANES:D // 2 + (cix + 1) * LANES])
        pack_ref[pl.ds(cix, rows, stride=PACK), :] = hi | (lo >> 16)


def _unpack_rows(pack_ref, rows):
    words = [pack_ref[pl.ds(cix, rows, stride=PACK), :] for cix in range(PACK)]
    his = [lax.bitcast_convert_type(w & jnp.uint32(0xFFFF0000), F32) for w in words]
    los = [lax.bitcast_convert_type(w << 16, F32) for w in words]
    return jnp.concatenate(his + los, axis=1)


def _ada_kernel(c_ref, w_ref, b_ref, o_ref):
    c = c_ref[...]
    s = c * _sigmoid(c)
    o_ref[0] = jnp.dot(s.astype(BF16), w_ref[0].astype(BF16), preferred_element_type=F32) + b_ref[0]


def _ada_mod(cvec, w_ada, b_ada):
    depth, _, n6 = w_ada.shape
    tn = 1536
    out = pl.pallas_call(
        _ada_kernel,
        grid=(depth, n6 // tn),
        in_specs=[
            pl.BlockSpec((8, D), lambda l, j: (0, 0)),
            pl.BlockSpec((1, D, tn), lambda l, j: (l, 0, j)),
            pl.BlockSpec((1, 1, tn), lambda l, j: (l, 0, j)),
        ],
        out_specs=pl.BlockSpec((1, 8, tn), lambda l, j: (l, 0, j)),
        out_shape=jax.ShapeDtypeStruct((depth, 8, n6), F32),
        compiler_params=_cparams(("arbitrary", "arbitrary")),
        name="ada_mod",
    )(cvec, w_ada, b_ada.reshape(depth, 1, n6))
    return out.reshape(depth, 8, 6, D)


def _rope(t, cs, sn, first_half):
    fwd = pltpu.roll(t, LANES - 16, axis=1)
    bwd = pltpu.roll(t, 16, axis=1)
    return t * cs + jnp.where(first_half, fwd, bwd) * sn


def _inproj_kernel(x_ref, mod_ref, w_ref, gain_ref, cos_ref, sin_ref,
                   q_ref, k_ref, ksw_ref, v_ref, vsw_ref, u_ref, vg_ref):
    mod = mod_ref[0, 0]
    h = _modulate(x_ref[0], mod[0:1], mod[1:2]).astype(BF16)
    proj = jnp.dot(h, w_ref[...], preferred_element_type=F32)
    cs = cos_ref[...]
    sn = sin_ref[...]
    lane = lax.broadcasted_iota(jnp.int32, cs.shape, 1)
    first_half = (lane & 16) == 0
    for cix in range(ATT_W // LANES):
        t = proj[:, cix * LANES:(cix + 1) * LANES]
        q_ref[0, :, cix * LANES:(cix + 1) * LANES] = (
            _rope(t, cs, sn, first_half) * (HEAD_DIM ** -0.5)).astype(BF16)
    kr = _rope(proj[:, ATT_W:ATT_W + KV_W], cs, sn, first_half)
    k_ref[0] = kr.astype(BF16)
    ksw_ref[0] = pltpu.roll(kr, HEAD_DIM, axis=1).astype(BF16)
    vv = proj[:, ATT_W + KV_W:ATT_W + 2 * KV_W]
    v_ref[0] = vv.astype(BF16)
    vsw_ref[0] = pltpu.roll(vv, HEAD_DIM, axis=1).astype(BF16)
    z = proj[:, ATT_W + 2 * KV_W:]
    g = z * (0.5 * (1.0 + jnp.tanh(SQRT_2_OVER_PI * (z + 0.044715 * (z * z * z)))))
    u_ref[0] = g[:, :GM_W].astype(BF16)
    vg = g[:, GM_W:]
    ms = jnp.mean(vg * vg, axis=-1, keepdims=True)
    vg_ref[0] = (vg * lax.rsqrt(ms + EPS) * gain_ref[...]).astype(BF16)


def _in_proj(x, mod, w_in_bf, gm_gain, cos_t, sin_t):
    b, s, _ = x.shape
    tm = TM_IN
    row = lambda w: pl.BlockSpec((1, tm, w), lambda bi, i: (bi, i, 0))
    outs = pl.pallas_call(
        _inproj_kernel,
        grid=(b, s // tm),
        in_specs=[
            row(D),
            pl.BlockSpec((1, 1, 6, D), lambda bi, i: (0, bi, 0, 0)),
            pl.BlockSpec((D, IN_W), lambda bi, i: (0, 0)),
            pl.BlockSpec((1, GM_W), lambda bi, i: (0, 0)),
            pl.BlockSpec((tm, LANES), lambda bi, i: (i, 0)),
            pl.BlockSpec((tm, LANES), lambda bi, i: (i, 0)),
        ],
        out_specs=[row(ATT_W), row(KV_W), row(KV_W), row(KV_W), row(KV_W), row(GM_W), row(GM_W)],
        out_shape=[jax.ShapeDtypeStruct((b, s, w), BF16)
                   for w in (ATT_W, KV_W, KV_W, KV_W, KV_W, GM_W, GM_W)],
        compiler_params=_cparams(("arbitrary", "arbitrary"), VMEM_LIMIT),
        name="in_proj",
    )(x, mod, w_in_bf, gm_gain.reshape(1, GM_W), cos_t, sin_t)
    return outs


def _ctx_kernel(c_ref, mod_ref, w_ref, k_ref, ksw_ref, v_ref, vsw_ref):
    mod = mod_ref[0, 0]
    h = _modulate(c_ref[0], mod[0:1], mod[1:2]).astype(BF16)
    kv = jnp.dot(h, w_ref[...], preferred_element_type=F32)
    kk = kv[:, :KV_W]
    vv = kv[:, KV_W:]
    k_ref[0] = kk.astype(BF16)
    ksw_ref[0] = pltpu.roll(kk, HEAD_DIM, axis=1).astype(BF16)
    v_ref[0] = vv.astype(BF16)
    vsw_ref[0] = pltpu.roll(vv, HEAD_DIM, axis=1).astype(BF16)


def _ctx_kv(ctx, mod, w_in_bf):
    b, l, _ = ctx.shape
    spec = pl.BlockSpec((1, l, KV_W), lambda bi: (bi, 0, 0))
    return pl.pallas_call(
        _ctx_kernel,
        grid=(b,),
        in_specs=[
            pl.BlockSpec((1, l, D), lambda bi: (bi, 0, 0)),
            pl.BlockSpec((1, 1, 6, D), lambda bi: (0, b, 0, 0)),
            pl.BlockSpec((D, 2 * KV_W), lambda bi: (0, ATT_W // (2 * KV_W))),
        ],
        out_specs=[spec] * 4,
        out_shape=[jax.ShapeDtypeStruct((b, l, KV_W), BF16)] * 4,
        compiler_params=_cparams(("arbitrary",)),
        name="ctx_kv",
    )(ctx, mod, w_in_bf)


def _attn_kernel(sink_ref, q_ref, kp_ref, kc_ref, kn_ref, ksp_ref, ksc_ref, ksn_ref,
                 vp_ref, vc_ref, vn_ref, vsp_ref, vsc_ref, vsn_ref,
                 kx_ref, ksx_ref, vx_ref, vsx_ref,
                 u_ref, vg_ref, wcat_ref, bs_ref, wout_ref, x_ref, mod_ref, o_ref):
    n = pl.program_id(1)
    nblk = pl.num_programs(1) * Q_BLOCKS
    lane = lax.broadcasted_iota(jnp.int32, (1, LANES), 1)
    low = lane < HEAD_DIM
    zero = jnp.zeros((), BF16)

    def variants(a0, a1):
        return ((jnp.where(low, a0, zero), jnp.where(low, zero, a1)),
                (jnp.where(low, a1, zero), jnp.where(low, zero, a0)))

    cat = lambda refs: jnp.concatenate([r[0] for r in refs], axis=0)
    kb_var = variants(cat((kp_ref, kc_ref, kn_ref)), cat((ksp_ref, ksc_ref, ksn_ref)))
    vb_var = variants(cat((vp_ref, vc_ref, vn_ref)), cat((vsp_ref, vsc_ref, vsn_ref)))
    kx_var = variants(kx_ref[0], ksx_ref[0])
    vx_var = variants(vx_ref[0], vsx_ref[0])

    row = lax.broadcasted_iota(jnp.int32, (2 * BLOCK, BLOCK), 0) & (BLOCK - 1)
    col = lax.broadcasted_iota(jnp.int32, (2 * BLOCK, BLOCK), 1)
    top = lax.broadcasted_iota(jnp.int32, (2 * BLOCK, 1), 0) < BLOCK
    nt_dims = (((1,), (1,)), ((), ()))

    q = q_ref[0]
    att_blocks = [[None] * 4 for _ in range(Q_BLOCKS)]
    for kvh in range(2):
        qst = jnp.concatenate(
            [q[qb * BLOCK:(qb + 1) * BLOCK, pr * LANES:(pr + 1) * LANES]
             for qb in range(Q_BLOCKS) for pr in (2 * kvh, 2 * kvh + 1)], axis=0)
        accs = [None] * Q_BLOCKS
        for half in range(2):
            sk = jnp.where(top, sink_ref[4 * kvh + half], sink_ref[4 * kvh + 2 + half])
            s_ctx = lax.dot_general(qst, kx_var[kvh][half], nt_dims, preferred_element_type=F32)
            o_band, p_ctx, dens = [], [], []
            for qb in range(Q_BLOCKS):
                g = n * Q_BLOCKS + qb
                qrows = qst[qb * 2 * BLOCK:(qb + 1) * 2 * BLOCK]
                sb = lax.dot_general(qrows, kb_var[kvh][half][qb * BLOCK:(qb + 3) * BLOCK], nt_dims,
                                     preferred_element_type=F32)
                s0 = jnp.where((col >= row) & (g > 0), sb[:, :BLOCK], NEG_INF)
                s1 = sb[:, BLOCK:2 * BLOCK]
                s2 = jnp.where((col <= row) & (g < nblk - 1), sb[:, 2 * BLOCK:], NEG_INF)
                sc = s_ctx[qb * 2 * BLOCK:(qb + 1) * 2 * BLOCK]
                ctx_blocks = [sc[:, cb * BLOCK:(cb + 1) * BLOCK] for cb in range(sc.shape[1] // BLOCK)]
                m = functools.reduce(jnp.maximum, [s0, s1, s2] + ctx_blocks)
                m = jnp.maximum(jnp.max(m, axis=-1, keepdims=True), sk)
                p0, p1, p2, pc = (jnp.exp(t - m) for t in (s0, s1, s2, sc))
                psum = functools.reduce(
                    jnp.add, [p0, p1, p2] + [pc[:, cb * BLOCK:(cb + 1) * BLOCK] for cb in range(len(ctx_blocks))])
                den = jnp.sum(psum, axis=-1, keepdims=True) + jnp.exp(sk - m)
                pb = jnp.concatenate([p0, p1, p2], axis=1).astype(BF16)
                o_band.append(jnp.dot(pb, vb_var[kvh][half][qb * BLOCK:(qb + 3) * BLOCK],
                                      preferred_element_type=F32))
                p_ctx.append(pc.astype(BF16))
                dens.append(den)
            o_ctx = jnp.dot(jnp.concatenate(p_ctx, axis=0), vx_var[kvh][half], preferred_element_type=F32)
            for qb in range(Q_BLOCKS):
                o = (o_band[qb] + o_ctx[qb * 2 * BLOCK:(qb + 1) * 2 * BLOCK]) / dens[qb]
                accs[qb] = o if accs[qb] is None else accs[qb] + o
        for qb in range(Q_BLOCKS):
            att_blocks[qb][2 * kvh] = accs[qb][:BLOCK]
            att_blocks[qb][2 * kvh + 1] = accs[qb][BLOCK:]

    u = u_ref[0]
    vg = vg_ref[0]
    bs = bs_ref[...]
    gm_blocks = [[None] * 4 for _ in range(Q_BLOCKS)]
    for j in range(GM_W // LANES):
        chunks = [vg[c * BLOCK:(c + 1) * BLOCK, j * LANES:(j + 1) * LANES] for c in range(Q_BLOCKS)]
        rhs = jnp.concatenate(
            [jnp.concatenate([jnp.where(low, v, zero) for v in chunks], axis=1),
             jnp.concatenate([jnp.where(low, zero, v) for v in chunks], axis=1)], axis=0)
        mixed = jnp.dot(wcat_ref[j], rhs, preferred_element_type=F32)
        bias = jnp.where(low, bs[:, 2 * j:2 * j + 1], bs[:, 2 * j + 1:2 * j + 2])
        for c in range(Q_BLOCKS):
            gm_blocks[c][j] = (u[c * BLOCK:(c + 1) * BLOCK, j * LANES:(j + 1) * LANES].astype(F32)
                               * (mixed[:, c * LANES:(c + 1) * LANES] + bias))

    mix = jnp.concatenate([jnp.concatenate(att_blocks[c] + gm_blocks[c], axis=1) for c in range(Q_BLOCKS)],
                          axis=0).astype(BF16)
    y = jnp.dot(mix, wout_ref[...], preferred_element_type=F32)
    mod = mod_ref[0, 0]
    o_ref[0] = x_ref[0] + mod[2:3] * y


def _attn_mixer(x, mod, sink, q, k, ksw, v, vsw, kx, ksx, vx, vsx, u, vg, wcat_bf, bs_t, wout_bf):
    b, s, _ = x.shape
    tq = Q_BLOCKS * BLOCK
    nb = s // BLOCK
    l = kx.shape[1]
    cur = lambda w: pl.BlockSpec((1, tq, w), lambda bi, n: (bi, n, 0))
    prv = lambda w: pl.BlockSpec((1, BLOCK, w), lambda bi, n: (bi, jnp.maximum(n * Q_BLOCKS - 1, 0), 0))
    nxt = lambda w: pl.BlockSpec((1, BLOCK, w), lambda bi, n: (bi, jnp.minimum((n + 1) * Q_BLOCKS, nb - 1), 0))
    cx = pl.BlockSpec((1, l, KV_W), lambda bi, n: (bi, 0, 0))
    return pl.pallas_call(
        _attn_kernel,
        grid=(b, s // tq),
        in_specs=[
            pl.BlockSpec(memory_space=pltpu.SMEM),
            cur(ATT_W),
            prv(KV_W), cur(KV_W), nxt(KV_W), prv(KV_W), cur(KV_W), nxt(KV_W),
            prv(KV_W), cur(KV_W), nxt(KV_W), prv(KV_W), cur(KV_W), nxt(KV_W),
            cx, cx, cx, cx,
            cur(GM_W), cur(GM_W),
            pl.BlockSpec((4, BLOCK, 2 * BLOCK), lambda bi, n: (0, 0, 0)),
            pl.BlockSpec((BLOCK, 8), lambda bi, n: (0, 0)),
            pl.BlockSpec((D, D), lambda bi, n: (0, 0)),
            cur(D),
            pl.BlockSpec((1, 1, 6, D), lambda bi, n: (0, bi, 0, 0)),
        ],
        out_specs=cur(D),
        out_shape=jax.ShapeDtypeStruct((b, s, D), F32),
        compiler_params=_cparams(("arbitrary", "arbitrary"), VMEM_LIMIT),
        name="attn_gmlp_out",
    )(sink, q, k, k, k, ksw, ksw, ksw, v, v, v, vsw, vsw, vsw, kx, ksx, vx, vsx,
      u, vg, wcat_bf, bs_t, wout_bf, x, mod)


def _ffn_kernel(x_ref, mod_ref, wg_ref, wu_ref, wd_ref, o_ref, act):
    mod = mod_ref[0, 0]
    xf = x_ref[...]
    h = _modulate(xf, mod[3:4], mod[4:5]).astype(BF16)
    for c in range(wg_ref.shape[1] // W_CHUNK):
        cs = slice(c * W_CHUNK, (c + 1) * W_CHUNK)
        g = jnp.dot(h, wg_ref[:, cs], preferred_element_type=F32)
        up = jnp.dot(h, wu_ref[:, cs], preferred_element_type=F32)
        act[:, cs] = (g * _sigmoid(g) * up).astype(BF16)
    o_ref[...] = xf + mod[5:6] * jnp.dot(act[...], wd_ref[...], preferred_element_type=F32)


def _dense_ffn(x2d, mod, wg, wu, wd, seq):
    n = x2d.shape[0]
    f = wg.shape[1]
    tm = TM_FFN
    per_b = seq // tm
    resident = lambda shp: pl.BlockSpec(shp, lambda i: (0, 0), pipeline_mode=pl.Buffered(1))
    return pl.pallas_call(
        _ffn_kernel,
        grid=(n // tm,),
        in_specs=[
            pl.BlockSpec((tm, D), lambda i: (i, 0)),
            pl.BlockSpec((1, 1, 6, D), lambda i: (0, i // per_b, 0, 0)),
            resident((D, f)), resident((D, f)), resident((f, D)),
        ],
        out_specs=pl.BlockSpec((tm, D), lambda i: (i, 0)),
        out_shape=jax.ShapeDtypeStruct((n, D), F32),
        scratch_shapes=[pltpu.VMEM((tm, f), BF16)],
        compiler_params=_cparams(("arbitrary",), VMEM_LIMIT),
        name="dense_ffn",
    )(x2d, mod, wg, wu, wd)


def _pool_route_kernel(x_ref, xp_ref, xn_ref, mod_ref, band_ref, pw_ref, psc_ref, wr_hi_ref, wr_lo_ref,
                       tri_ref, x3_ref, h2_ref, route_ref, cnt_ref, hext, carry):
    bi = pl.program_id(0)
    i = pl.program_id(1)
    ni = pl.num_programs(1)
    tm = x_ref.shape[1]
    seq = tm * ni
    mod = mod_ref[0, 0]

    @pl.when((bi == 0) & (i == 0))
    def _():
        carry[...] = jnp.zeros_like(carry)

    xf = x_ref[0]
    hp = _modulate(xp_ref[0], mod[0:1], mod[1:2])
    hn = _modulate(xn_ref[0], mod[0:1], mod[1:2])
    hext[0:POOL_HALO] = jnp.where(i > 0, hp, 0.0).astype(BF16)
    h_main = _modulate(xf, mod[0:1], mod[1:2])
    hext[POOL_HALO:POOL_HALO + tm] = h_main.astype(BF16)
    hext[POOL_HALO + tm:] = jnp.where(i < ni - 1, hn, 0.0).astype(BF16)

    t_local = lax.broadcasted_iota(jnp.int32, (BLOCK, 1), 0)
    ys = []
    for gi, w in enumerate(POOL_SIZES):
        lo_off = -(w // 2)
        hi_off = w - 1 - w // 2
        cols = slice(gi * POOL_GD, (gi + 1) * POOL_GD)
        outs = []
        for sb in range(tm // BLOCK):
            r0 = sb * BLOCK
            win = jnp.dot(band_ref[gi], hext[r0:r0 + BLOCK + 2 * POOL_HALO, cols],
                          preferred_element_type=F32)
            t = i * tm + r0 + t_local
            cnt = (jnp.minimum(t + hi_off, seq - 1) - jnp.maximum(t + lo_off, 0) + 1).astype(F32)
            diff = win / cnt - h_main[r0:r0 + BLOCK, cols]
            outs.append(diff.astype(BF16))
        dg = jnp.concatenate(outs, axis=0)
        ys.append(jnp.dot(dg, pw_ref[gi], preferred_element_type=F32))
    y = jnp.concatenate(ys, axis=1) * psc_ref[...]
    x3 = xf + mod[2:3] * y
    x3_ref[0] = x3

    h2 = _modulate(x3, mod[3:4], mod[4:5])
    _rows_to_slabs(h2, h2_ref)
    h_hi = h2.astype(BF16)
    h_lo = (h2 - h_hi.astype(F32)).astype(BF16)
    logits = (jnp.dot(h_hi, wr_hi_ref[...], preferred_element_type=F32)
              + jnp.dot(h_hi, wr_lo_ref[...], preferred_element_type=F32)
              + jnp.dot(h_lo, wr_hi_ref[...], preferred_element_type=F32))
    lane = lax.broadcasted_iota(jnp.int32, (tm, LANES), 1)
    lane_f = lane.astype(F32)
    neg = -jnp.inf
    lg = jnp.where(lane < N_EXPERTS, logits, neg)
    m1 = jnp.max(lg, axis=-1, keepdims=True)
    i1 = jnp.min(jnp.where(lg == m1, lane_f, float(LANES)), axis=-1, keepdims=True)
    oh1 = lane_f == i1
    lg2 = jnp.where(oh1, neg, lg)
    m2 = jnp.max(lg2, axis=-1, keepdims=True)
    i2 = jnp.min(jnp.where(lg2 == m2, lane_f, float(LANES)), axis=-1, keepdims=True)
    oh2 = lane_f == i2
    e = jnp.exp(m2 - m1)
    w1 = 1.0 / (1.0 + e)
    w2 = e / (1.0 + e)
    oh = jnp.where(oh1 | oh2, 1.0, 0.0)
    before = jnp.dot(tri_ref[...], oh.astype(BF16), preferred_element_type=F32) + carry[...]
    r1 = jnp.sum(jnp.where(oh1, before, 0.0), axis=-1, keepdims=True)
    r2 = jnp.sum(jnp.where(oh2, before, 0.0), axis=-1, keepdims=True)
    carry[...] = carry[...] + jnp.sum(oh, axis=0, keepdims=True)
    cnt_ref[...] = carry[...]
    info = jnp.where(lane == 0, i1, jnp.where(lane == 1, i2, jnp.where(lane == 2, w1, jnp.where(
        lane == 3, w2, jnp.where(lane == 4, r1, jnp.where(lane == 5, r2, 0.0))))))
    route_ref[...] = info.T[0:8, :]


def _pool_route(x, mod, band, pw_bf, pool_scale, wr_hi, wr_lo, tri):
    b, s, _ = x.shape
    tm = TM_POOL
    ni = s // tm
    hb = tm // POOL_HALO
    row = pl.BlockSpec((1, tm, D), lambda bi, i: (bi, i, 0))
    const2 = lambda shp: pl.BlockSpec(shp, lambda bi, i: (0,) * len(shp))
    return pl.pallas_call(
        _pool_route_kernel,
        grid=(b, ni),
        in_specs=[
            row,
            pl.BlockSpec((1, POOL_HALO, D), lambda bi, i: (bi, jnp.maximum(i * hb - 1, 0), 0)),
            pl.BlockSpec((1, POOL_HALO, D), lambda bi, i: (bi, jnp.minimum((i + 1) * hb, s // POOL_HALO - 1), 0)),
            pl.BlockSpec((1, 1, 6, D), lambda bi, i: (1, bi, 0, 0)),
            const2(band.shape), const2(pw_bf.shape), const2((1, D)),
            const2(wr_hi.shape), const2(wr_lo.shape), const2(tri.shape),
        ],
        out_specs=[row,
                   pl.BlockSpec((tm * SLAB, LANES), lambda bi, i: (bi * ni + i, 0)),
                   pl.BlockSpec((8, tm), lambda bi, i: (0, bi * ni + i)),
                   pl.BlockSpec((1, LANES), lambda bi, i: (0, 0))],
        out_shape=[jax.ShapeDtypeStruct((b, s, D), F32), jax.ShapeDtypeStruct((b * s * SLAB, LANES), F32),
                   jax.ShapeDtypeStruct((8, b * s), F32), jax.ShapeDtypeStruct((1, LANES), F32)],
        scratch_shapes=[pltpu.VMEM((tm + 2 * POOL_HALO, D), BF16), pltpu.VMEM((1, LANES), F32)],
        compiler_params=_cparams(("arbitrary", "arbitrary"), VMEM_LIMIT),
        name="pool_route",
    )(x, x, x, mod, band, pw_bf, pool_scale.reshape(1, D), wr_hi, wr_lo, tri)


def _slotmap_kernel(pos_ref, lo_ref, hi_ref, o_ref):
    n_pairs = pos_ref.shape[0]
    spare_mask = 2 * TM_MOE - 1
    for e in range(lo_ref.shape[0]):
        def fill(p, c):
            o_ref[p] = n_pairs + (p & spare_mask)
            return c
        lax.fori_loop(lo_ref[e], hi_ref[e], fill, 0)

    def place(f, c):
        o_ref[pos_ref[f]] = f
        return c
    lax.fori_loop(0, n_pairs, place, 0, unroll=32)


def _slot_map(pos_flat, lo, hi, n_slots):
    smem = pl.BlockSpec(memory_space=pltpu.SMEM)
    return pl.pallas_call(
        _slotmap_kernel,
        in_specs=[smem, smem, smem],
        out_specs=smem,
        out_shape=jax.ShapeDtypeStruct((n_slots,), jnp.int32),
        name="moe_slot_map",
    )(pos_flat, lo, hi)


def _moe_kernel(te_ref, nused_ref, fnext_ref, fprev_ref, f0_ref, h_hbm, wg_hbm, wu_hbm, wd_hbm,
                y_hbm, wg_res, wu_res, wd_res, stg_in, stg_out, xbuf, xb, act, acc, stage, gsem, ssem, wsem):
    i = pl.program_id(0)
    nt = pl.num_programs(0)
    used_tiles = nused_ref[0]
    tm = xb.shape[0]
    f_dim = wg_res.shape[1]
    tok_mask = h_hbm.shape[0] // SLAB - 1
    tile_rows = tm * SLAB
    out_rows = tm * PACK

    def slab(ix, width=SLAB):
        return pl.ds(pl.multiple_of(ix * width, width), width)

    def gather_row(fref, r, slot, zero=0):
        tok = fref[0, 0, r + zero] & tok_mask
        return pltpu.make_async_copy(h_hbm.at[slab(tok)], xbuf.at[slot, slab(r)], gsem.at[slot])

    def scatter_row(fref, r, slot, zero=0):
        return pltpu.make_async_copy(stage.at[slot, slab(r, PACK)], y_hbm.at[slab(fref[0, 0, r + zero], PACK)],
                                     ssem.at[slot])

    def gather_all(slot):
        return pltpu.make_async_copy(h_hbm.at[pl.ds(0, tile_rows)], xbuf.at[slot], gsem.at[slot])

    def scatter_all(slot):
        return pltpu.make_async_copy(stage.at[slot], y_hbm.at[pl.ds(0, out_rows)], ssem.at[slot])

    cur = i % 2
    used = i < used_tiles

    @pl.when(i == 0)
    def _():
        stage[...] = jnp.zeros_like(stage)
        spare = y_hbm.shape[0] - 2 * out_rows
        fills = [pltpu.make_async_copy(stage.at[sl], y_hbm.at[pl.ds(spare + sl * out_rows, out_rows)],
                                       ssem.at[sl])
                 for sl in range(2)]
        for cp in fills:
            cp.start()
        for cp in fills:
            cp.wait()

    @pl.when((i == 0) & used)
    def _():
        def prime(r, c):
            gather_row(f0_ref, r, 0).start()
            return c
        lax.fori_loop(0, tm, prime, 0)

    expert = te_ref[i]
    new_expert = used & ((i == 0) | (expert != te_ref[jnp.maximum(i - 1, 0)]))

    has_next = i + 1 < used_tiles
    has_prev = (i >= 1) & (i - 1 < used_tiles)
    steady = (i >= 2) & has_next

    @pl.when(used)
    def _():
        gather_all(cur).wait()

    @pl.when((i >= 2) & (i - 2 < used_tiles))
    def _():
        scatter_all(cur).wait()

    half = f_dim // MOE_SPLIT
    n_piece = half // W_CHUNK
    n_groups = MOE_SPLIT * (n_piece + 1)

    def tick(v):
        bits = jnp.max(lax.bitcast_convert_type(v[0:SLAB, 0:LANES], jnp.int32))
        return lax.shift_right_logical(lax.shift_right_logical(bits, 16), 16)

    def expert_ffn(issue_group):
        for cix in range(SLAB):
            xb[:, cix * LANES:(cix + 1) * LANES] = xbuf.at[cur][pl.ds(cix, tm, stride=SLAB), :].astype(BF16)
        xv = xb[...]
        out = None
        issue_group(0, 0)
        k = 1
        for hf in range(MOE_SPLIT):
            for c in range(n_piece):
                cs = slice(hf * half + c * W_CHUNK, hf * half + (c + 1) * W_CHUNK)
                g = jnp.dot(xv, wg_res[:, cs], preferred_element_type=F32)
                up = jnp.dot(xv, wu_res[:, cs], preferred_element_type=F32)
                act[:, c * W_CHUNK:(c + 1) * W_CHUNK] = (g * _sigmoid(g) * up).astype(BF16)
                issue_group(k, tick(g))
                k += 1
            part = jnp.dot(act[...], wd_res[hf * half:(hf + 1) * half, :], preferred_element_type=F32)
            if hf < MOE_SPLIT - 1:
                acc[...] = part if out is None else acc[...] + part
                out = acc
                issue_group(k, tick(part))
                k += 1
            else:
                _pack_rows(part if out is None else acc[...] + part, stage.at[cur])

    def chunk_plan():
        n_in = D // W_ROWS_IN
        plan = [(w, res, stg_in, W_RING_IN, 0, W_ROWS_IN, c, wi * n_in + c)
                for wi, (w, res) in enumerate(((wg_hbm, wg_res), (wu_hbm, wu_res))) for c in range(n_in)]
        return plan + [(wd_hbm, wd_res, stg_out, W_RING_OUT, W_RING_IN, W_ROWS_OUT, c, c)
                       for c in range(f_dim // W_ROWS_OUT)]

    def chunk_copy(entry, ex):
        w_hbm, _, stg, ring, sem0, rows, c, j = entry
        return pltpu.make_async_copy(w_hbm.at[ex, pl.ds(c * rows, rows), :], stg.at[j % ring],
                                     wsem.at[sem0 + j % ring])

    def start_first_chunks(ex):
        for entry in chunk_plan():
            if entry[7] < entry[3]:
                chunk_copy(entry, ex).start()

    @pl.when(new_expert)
    def _():
        plan = chunk_plan()

        @pl.when(i == 0)
        def _():
            start_first_chunks(expert)

        for k, entry in enumerate(plan):
            _, res, stg, ring, _, rows, c, j = entry
            chunk_copy(entry, expert).wait()
            res[c * rows:(c + 1) * rows, :] = stg[j % ring].astype(BF16)
            later = [e for e in plan[k + 1:] if e[2] is stg and e[7] == j + ring]
            if later:
                chunk_copy(later[0], expert).start()

    next_expert = te_ref[jnp.minimum(i + 1, nt - 1)]

    @pl.when(has_next & (next_expert != expert))
    def _():
        start_first_chunks(next_expert)

    @pl.when(steady)
    def _():
        def issue_group(k, zero):
            for r in range(k * tm // n_groups, (k + 1) * tm // n_groups):
                gather_row(fnext_ref, r, 1 - cur, zero).start()
                scatter_row(fprev_ref, r, 1 - cur, zero).start()
        expert_ffn(issue_group)

    @pl.when(jnp.logical_not(steady))
    def _():
        @pl.when(used)
        def _():
            expert_ffn(lambda k, zero: None)

        @pl.when(has_next)
        def _():
            def issue(r, c):
                gather_row(fnext_ref, r, 1 - cur).start()
                return c
            lax.fori_loop(0, tm, issue, 0)

        @pl.when(has_prev)
        def _():
            def issue(r, c):
                scatter_row(fprev_ref, r, 1 - cur).start()
                return c
            lax.fori_loop(0, tm, issue, 0)

    @pl.when((i == nt - 1) & (nt - 2 < used_tiles))
    def _():
        scatter_all(1 - cur).wait()


def _moe_experts(h_slabs, fmap, tile_expert, n_used, wg, wu, wd, n_tiles, y_rows):
    f = wg.shape[2]
    tm = TM_MOE
    fblk = lambda imap: pl.BlockSpec((1, 1, tm), imap, memory_space=pltpu.SMEM)
    hbm = pl.BlockSpec(memory_space=pl.ANY)
    return pl.pallas_call(
        _moe_kernel,
        grid_spec=pltpu.PrefetchScalarGridSpec(
            num_scalar_prefetch=2,
            grid=(n_tiles,),
            in_specs=[
                fblk(lambda i, te, nu: (jnp.minimum(i + 1, n_tiles - 1), 0, 0)),
                fblk(lambda i, te, nu: (jnp.maximum(i - 1, 0), 0, 0)),
                fblk(lambda i, te, nu: (0, 0, 0)),
                hbm, hbm, hbm, hbm,
            ],
            out_specs=hbm,
            scratch_shapes=[pltpu.VMEM((D, f), BF16), pltpu.VMEM((D, f), BF16), pltpu.VMEM((f, D), BF16),
                            pltpu.VMEM((W_RING_IN, W_ROWS_IN, f), F32), pltpu.VMEM((W_RING_OUT, W_ROWS_OUT, D), F32),
                            pltpu.VMEM((2, tm * SLAB, LANES), F32), pltpu.VMEM((tm, D), BF16),
                            pltpu.VMEM((tm, f // MOE_SPLIT), BF16), pltpu.VMEM((tm, D), F32),
                            pltpu.VMEM((2, tm * PACK, LANES), jnp.uint32),
                            pltpu.SemaphoreType.DMA((2,)), pltpu.SemaphoreType.DMA((2,)),
                            pltpu.SemaphoreType.DMA((W_RING_IN + W_RING_OUT,))],
        ),
        out_shape=jax.ShapeDtypeStruct((y_rows * PACK, LANES), jnp.uint32),
        compiler_params=_cparams(("arbitrary",), MOE_VMEM_LIMIT),
        name="moe_experts",
    )(tile_expert, n_used, fmap, fmap, fmap, h_slabs, wg, wu, wd)


def _combine_kernel(y1_ref, y2_ref, x_ref, w_ref, mod_ref, gain_ref, o_ref):
    w = w_ref[...]
    rows = x_ref.shape[0]
    moe = w[:, 0:1] * _unpack_rows(y1_ref, rows) + w[:, 1:2] * _unpack_rows(y2_ref, rows)
    mod = mod_ref[0, 0]
    x4 = x_ref[...] + mod[5:6] * moe
    ms = jnp.mean(x4 * x4, axis=-1, keepdims=True)
    o_ref[...] = x4 * lax.rsqrt(ms + EPS) * gain_ref[...]


def _combine(y, x3_2d, wts, mod, final_gain, seq):
    n = x3_2d.shape[0]
    tc = TC_COMB
    nt = n // tc
    per_b = seq // tc
    return pl.pallas_call(
        _combine_kernel,
        grid=(nt,),
        in_specs=[
            pl.BlockSpec((tc * PACK, LANES), lambda i: (i, 0)),
            pl.BlockSpec((tc * PACK, LANES), lambda i: (i + nt, 0)),
            pl.BlockSpec((tc, D), lambda i: (i, 0)),
            pl.BlockSpec((tc, 2), lambda i: (i, 0)),
            pl.BlockSpec((1, 1, 6, D), lambda i: (1, i // per_b, 0, 0)),
            pl.BlockSpec((1, D), lambda i: (0, 0)),
        ],
        out_specs=pl.BlockSpec((tc, D), lambda i: (i, 0)),
        out_shape=jax.ShapeDtypeStruct((n, D), F32),
        compiler_params=_cparams(("arbitrary",)),
        name="moe_combine",
    )(y, y, x3_2d, wts, mod, final_gain.reshape(1, D))


def _rope_tables(seq):
    rows = seq // GRID_W
    row_pos = jnp.repeat(jnp.arange(rows, dtype=F32), GRID_W)
    col_pos = jnp.tile(jnp.arange(GRID_W, dtype=F32), rows)
    axis_dim = HEAD_DIM // 2
    inv_freq = ROPE_BASE ** (-jnp.arange(0, axis_dim, 2, dtype=F32) / axis_dim)
    ar = row_pos[:, None] * inv_freq
    ac = col_pos[:, None] * inv_freq
    cos64 = jnp.concatenate([jnp.cos(ar), jnp.cos(ar), jnp.cos(ac), jnp.cos(ac)], axis=1)
    sin64 = jnp.concatenate([-jnp.sin(ar), jnp.sin(ar), -jnp.sin(ac), jnp.sin(ac)], axis=1)
    return jnp.tile(cos64, (1, 2)), jnp.tile(sin64, (1, 2))


def _band_matrices():
    r = np.arange(BLOCK)[:, None]
    c = np.arange(BLOCK + 2 * POOL_HALO)[None, :] - POOL_HALO
    mats = []
    for w in POOL_SIZES:
        lo = -(w // 2)
        hi = w - 1 - w // 2
        mats.append(((c >= r + lo) & (c <= r + hi)).astype(np.float32))
    return jnp.asarray(np.stack(mats), dtype=BF16)


def kernel(x, c, ctx, c_ctx, w_ada, b_ada, w_in, attn_sink, gm_gain, gm_w_s, gm_b_s, w_out,
           ffn_w_gate, ffn_w_up, ffn_w_down, pool_w, pool_scale, router_w,
           moe_w_gate, moe_w_up, moe_w_down, final_gain):
    b, s, _ = x.shape
    n = b * s
    assert w_ada.shape[0] == 2 and w_in.shape[0] == 1 and pool_w.shape[0] == 1
    assert s % TM_IN == 0 and s % TM_POOL == 0 and s % TM_FFN == 0 and b <= 4
    assert n & (n - 1) == 0

    cvec = jnp.concatenate([c, c_ctx[None, :], jnp.zeros((8 - b - 1, D), F32)], axis=0)
    mod = _ada_mod(cvec, w_ada, b_ada)

    cos_t, sin_t = _rope_tables(s)
    w_in_bf = w_in[0].astype(BF16)
    q, k, ksw, v, vsw, u, vg = _in_proj(x, mod, w_in_bf, gm_gain[0], cos_t, sin_t)
    kx, ksx, vx, vsx = _ctx_kv(ctx, mod, w_in_bf)
    wcat = gm_w_s[0].reshape(4, 2, BLOCK, BLOCK).transpose(0, 2, 1, 3).reshape(4, BLOCK, 2 * BLOCK).astype(BF16)
    x1 = _attn_mixer(x, mod, attn_sink[0], q, k, ksw, v, vsw, kx, ksx, vx, vsx, u, vg,
                     wcat, gm_b_s[0].T, w_out[0].astype(BF16))
    x2 = _dense_ffn(x1.reshape(n, D), mod, ffn_w_gate[0].astype(BF16), ffn_w_up[0].astype(BF16),
                    ffn_w_down[0].astype(BF16), s)

    wr = jnp.pad(router_w[0], ((0, 0), (0, LANES - N_EXPERTS)))
    wr_hi = wr.astype(BF16)
    wr_lo = (wr - wr_hi.astype(F32)).astype(BF16)
    tri = jnp.asarray(np.tril(np.ones((TM_POOL, TM_POOL), np.float32), -1), dtype=BF16)
    x3, h2, route, counts = _pool_route(x2.reshape(b, s, D), mod, _band_matrices(), pool_w[0].astype(BF16),
                                        pool_scale[0], wr_hi, wr_lo, tri)

    tm = TM_MOE
    n_tiles = (2 * n) // tm + N_EXPERTS
    cnt = counts[0, :N_EXPERTS].astype(jnp.int32)
    tiles_e = (cnt + tm - 1) // tm
    tile_end = jnp.cumsum(tiles_e)
    off = (tile_end - tiles_e) * tm
    n_used = tile_end[-1]
    tix = jnp.arange(n_tiles, dtype=jnp.int32)
    te = jnp.minimum(jnp.sum(tix[:, None] >= tile_end[None, :], axis=1), N_EXPERTS - 1).astype(jnp.int32)
    te_last = te[jnp.maximum(n_used - 1, 0)]
    tile_expert = jnp.where(tix < n_used, te, te_last)
    e1 = route[0].astype(jnp.int32)
    e2 = route[1].astype(jnp.int32)
    pos1 = off[e1] + route[4].astype(jnp.int32)
    pos2 = off[e2] + route[5].astype(jnp.int32)
    n_slots = n_tiles * tm
    pad_lo = jnp.concatenate([off + cnt, (n_used * tm).reshape(1)]).astype(jnp.int32)
    pad_hi = jnp.concatenate([off + tiles_e * tm, jnp.full((1,), n_slots, jnp.int32)]).astype(jnp.int32)
    fmap = _slot_map(jnp.concatenate([pos1, pos2]), pad_lo, pad_hi, n_slots)
    n_used_arr = n_used.reshape(1).astype(jnp.int32)

    y = _moe_experts(h2, fmap.reshape(n_tiles, 1, tm), tile_expert, n_used_arr,
                     moe_w_gate[0], moe_w_up[0], moe_w_down[0], n_tiles, 2 * n + 2 * tm)
    out = _combine(y, x3.reshape(n, D), route[2:4].T, mod, final_gain, s)
    return out.reshape(b, s, D)
```

```python
import functools

import numpy as np
import jax
import jax.numpy as jnp
from jax import lax
from jax.experimental import pallas as pl
from jax.experimental.pallas import tpu as pltpu

F32 = jnp.float32
BF16 = jnp.bfloat16

D = 1024
GRID_W = 64
EPS = 1e-6
NEG_INF = -1e30
HEAD_DIM = 64
N_Q_HEADS = 8
BLOCK = 128
ATT_W = 512
KV_W = 128
GM_W = 512
IN_W = 1792
POOL_SIZES = (2, 4, 8, 16)
POOL_GD = 256
POOL_HALO = 16
N_EXPERTS = 8
ROPE_BASE = 10000.0
LANES = 128
SLAB = D // LANES
PACK = SLAB // 2
SQRT_2_OVER_PI = 0.7978845608028654

TM_IN = 1024
TM_FFN = 1024
TM_POOL = 512
TM_MOE = 512
MOE_SPLIT = 2
W_CHUNK = 256
W_ROWS_IN = 128
W_ROWS_OUT = 512
W_RING_IN = 12
W_RING_OUT = 2
MOE_VMEM_LIMIT = 60 * 1024 * 1024
TC_COMB = 1024
Q_BLOCKS = 8
VMEM_LIMIT = 56 * 1024 * 1024


def _cparams(sem, vmem=None):
    return pltpu.CompilerParams(dimension_semantics=sem, vmem_limit_bytes=vmem)


def _modulate(xf, shift, scale):
    ms = jnp.mean(xf * xf, axis=-1, keepdims=True)
    return xf * lax.rsqrt(ms + EPS) * (1.0 + scale) + shift


def _sigmoid(z):
    return 1.0 / (1.0 + jnp.exp(-z))


def _rows_to_slabs(val, slab_ref):
    rows = val.shape[0]
    for cix in range(SLAB):
        slab_ref[pl.ds(cix, rows, stride=SLAB), :] = val[:, cix * LANES:(cix + 1) * LANES]


def _slabs_to_rows(slab_ref, rows):
    return jnp.concatenate([slab_ref[pl.ds(cix, rows, stride=SLAB), :] for cix in range(SLAB)], axis=1)


def _pack_rows(val, pack_ref):
    rows = val.shape[0]
    bits = lambda v: lax.bitcast_convert_type(v.astype(BF16).astype(F32), jnp.uint32)
    for cix in range(PACK):
        hi = bits(val[:, cix * LANES:(cix + 1) * LANES])
        lo = bits(val[:, D // 2 + cix * LANES:D // 2 + (cix + 1) * LANES])
        pack_ref[pl.ds(cix, rows, stride=PACK), :] = hi | (lo >> 16)


def _unpack_rows(pack_ref, rows):
    words = [pack_ref[pl.ds(cix, rows, stride=PACK), :] for cix in range(PACK)]
    his = [lax.bitcast_convert_type(w & jnp.uint32(0xFFFF0000), F32) for w in words]
    los = [lax.bitcast_convert_type(w << 16, F32) for w in words]
    return jnp.concatenate(his + los, axis=1)


def _ada_kernel(c_ref, w_ref, b_ref, o_ref):
    c = c_ref[...]
    s = c * _sigmoid(c)
    o_ref[0] = jnp.dot(s.astype(BF16), w_ref[0].astype(BF16), preferred_element_type=F32) + b_ref[0]


def _ada_mod(cvec, w_ada, b_ada):
    depth, _, n6 = w_ada.shape
    tn = 1536
    out = pl.pallas_call(
        _ada_kernel,
        grid=(depth, n6 // tn),
        in_specs=[
            pl.BlockSpec((8, D), lambda l, j: (0, 0)),
            pl.BlockSpec((1, D, tn), lambda l, j: (l, 0, j)),
            pl.BlockSpec((1, 1, tn), lambda l, j: (l, 0, j)),
        ],
        out_specs=pl.BlockSpec((1, 8, tn), lambda l, j: (l, 0, j)),
        out_shape=jax.ShapeDtypeStruct((depth, 8, n6), F32),
        compiler_params=_cparams(("arbitrary", "arbitrary")),
        name="ada_mod",
    )(cvec, w_ada, b_ada.reshape(depth, 1, n6))
    return out.reshape(depth, 8, 6, D)


def _rope(t, cs, sn, first_half):
    fwd = pltpu.roll(t, LANES - 16, axis=1)
    bwd = pltpu.roll(t, 16, axis=1)
    return t * cs + jnp.where(first_half, fwd, bwd) * sn


def _inproj_kernel(x_ref, mod_ref, w_ref, gain_ref, cos_ref, sin_ref,
                   q_ref, k_ref, ksw_ref, v_ref, vsw_ref, u_ref, vg_ref):
    mod = mod_ref[0, 0]
    h = _modulate(x_ref[0], mod[0:1], mod[1:2]).astype(BF16)
    proj = jnp.dot(h, w_ref[...], preferred_element_type=F32)
    cs = cos_ref[...]
    sn = sin_ref[...]
    lane = lax.broadcasted_iota(jnp.int32, cs.shape, 1)
    first_half = (lane & 16) == 0
    for cix in range(ATT_W // LANES):
        t = proj[:, cix * LANES:(cix + 1) * LANES]
        q_ref[0, :, cix * LANES:(cix + 1) * LANES] = (
            _rope(t, cs, sn, first_half) * (HEAD_DIM ** -0.5)).astype(BF16)
    kr = _rope(proj[:, ATT_W:ATT_W + KV_W], cs, sn, first_half)
    k_ref[0] = kr.astype(BF16)
    ksw_ref[0] = pltpu.roll(kr, HEAD_DIM, axis=1).astype(BF16)
    vv = proj[:, ATT_W + KV_W:ATT_W + 2 * KV_W]
    v_ref[0] = vv.astype(BF16)
    vsw_ref[0] = pltpu.roll(vv, HEAD_DIM, axis=1).astype(BF16)
    z = proj[:, ATT_W + 2 * KV_W:]
    g = z * (0.5 * (1.0 + jnp.tanh(SQRT_2_OVER_PI * (z + 0.044715 * (z * z * z)))))
    u_ref[0] = g[:, :GM_W].astype(BF16)
    vg = g[:, GM_W:]
    ms = jnp.mean(vg * vg, axis=-1, keepdims=True)
    vg_ref[0] = (vg * lax.rsqrt(ms + EPS) * gain_ref[...]).astype(BF16)


def _in_proj(x, mod, w_in_bf, gm_gain, cos_t, sin_t):
    b, s, _ = x.shape
    tm = TM_IN
    row = lambda w: pl.BlockSpec((1, tm, w), lambda bi, i: (bi, i, 0))
    outs = pl.pallas_call(
        _inproj_kernel,
        grid=(b, s // tm),
        in_specs=[
            row(D),
            pl.BlockSpec((1, 1, 6, D), lambda bi, i: (0, bi, 0, 0)),
            pl.BlockSpec((D, IN_W), lambda bi, i: (0, 0)),
            pl.BlockSpec((1, GM_W), lambda bi, i: (0, 0)),
            pl.BlockSpec((tm, LANES), lambda bi, i: (i, 0)),
            pl.BlockSpec((tm, LANES), lambda bi, i: (i, 0)),
        ],
        out_specs=[row(ATT_W), row(KV_W), row(KV_W), row(KV_W), row(KV_W), row(GM_W), row(GM_W)],
        out_shape=[jax.ShapeDtypeStruct((b, s, w), BF16)
                   for w in (ATT_W, KV_W, KV_W, KV_W, KV_W, GM_W, GM_W)],
        compiler_params=_cparams(("arbitrary", "arbitrary"), VMEM_LIMIT),
        name="in_proj",
    )(x, mod, w_in_bf, gm_gain.reshape(1, GM_W), cos_t, sin_t)
    return outs


def _ctx_kernel(c_ref, mod_ref, w_ref, k_ref, ksw_ref, v_ref, vsw_ref):
    mod = mod_ref[0, 0]
    h = _modulate(c_ref[0], mod[0:1], mod[1:2]).astype(BF16)
    kv = jnp.dot(h, w_ref[...], preferred_element_type=F32)
    kk = kv[:, :KV_W]
    vv = kv[:, KV_W:]
    k_ref[0] = kk.astype(BF16)
    ksw_ref[0] = pltpu.roll(kk, HEAD_DIM, axis=1).astype(BF16)
    v_ref[0] = vv.astype(BF16)
    vsw_ref[0] = pltpu.roll(vv, HEAD_DIM, axis=1).astype(BF16)


def _ctx_kv(ctx, mod, w_in_bf):
    b, l, _ = ctx.shape
    spec = pl.BlockSpec((1, l, KV_W), lambda bi: (bi, 0, 0))
    return pl.pallas_call(
        _ctx_kernel,
        grid=(b,),
        in_specs=[
            pl.BlockSpec((1, l, D), lambda bi: (bi, 0, 0)),
            pl.BlockSpec((1, 1, 6, D), lambda bi: (0, b, 0, 0)),
            pl.BlockSpec((D, 2 * KV_W), lambda bi: (0, ATT_W // (2 * KV_W))),
        ],
        out_specs=[spec] * 4,
        out_shape=[jax.ShapeDtypeStruct((b, l, KV_W), BF16)] * 4,
        compiler_params=_cparams(("arbitrary",)),
        name="ctx_kv",
    )(ctx, mod, w_in_bf)


def _attn_kernel(sink_ref, q_ref, kp_ref, kc_ref, kn_ref, ksp_ref, ksc_ref, ksn_ref,
                 vp_ref, vc_ref, vn_ref, vsp_ref, vsc_ref, vsn_ref,
                 kx_ref, ksx_ref, vx_ref, vsx_ref,
                 u_ref, vg_ref, wcat_ref, bs_ref, wout_ref, x_ref, mod_ref, o_ref):
    n = pl.program_id(1)
    nblk = pl.num_programs(1) * Q_BLOCKS
    lane = lax.broadcasted_iota(jnp.int32, (1, LANES), 1)
    low = lane < HEAD_DIM
    zero = jnp.zeros((), BF16)

    def variants(a0, a1):
        return ((jnp.where(low, a0, zero), jnp.where(low, zero, a1)),
                (jnp.where(low, a1, zero), jnp.where(low, zero, a0)))

    cat = lambda refs: jnp.concatenate([r[0] for r in refs], axis=0)
    kb_var = variants(cat((kp_ref, kc_ref, kn_ref)), cat((ksp_ref, ksc_ref, ksn_ref)))
    vb_var = variants(cat((vp_ref, vc_ref, vn_ref)), cat((vsp_ref, vsc_ref, vsn_ref)))
    kx_var = variants(kx_ref[0], ksx_ref[0])
    vx_var = variants(vx_ref[0], vsx_ref[0])

    row = lax.broadcasted_iota(jnp.int32, (2 * BLOCK, BLOCK), 0) & (BLOCK - 1)
    col = lax.broadcasted_iota(jnp.int32, (2 * BLOCK, BLOCK), 1)
    top = lax.broadcasted_iota(jnp.int32, (2 * BLOCK, 1), 0) < BLOCK
    nt_dims = (((1,), (1,)), ((), ()))

    q = q_ref[0]
    att_blocks = [[None] * 4 for _ in range(Q_BLOCKS)]
    for kvh in range(2):
        qst = jnp.concatenate(
            [q[qb * BLOCK:(qb + 1) * BLOCK, pr * LANES:(pr + 1) * LANES]
             for qb in range(Q_BLOCKS) for pr in (2 * kvh, 2 * kvh + 1)], axis=0)
        accs = [None] * Q_BLOCKS
        for half in range(2):
            sk = jnp.where(top, sink_ref[4 * kvh + half], sink_ref[4 * kvh + 2 + half])
            s_ctx = lax.dot_general(qst, kx_var[kvh][half], nt_dims, preferred_element_type=F32)
            o_band, p_ctx, dens = [], [], []
            for qb in range(Q_BLOCKS):
                g = n * Q_BLOCKS + qb
                qrows = qst[qb * 2 * BLOCK:(qb + 1) * 2 * BLOCK]
                sb = lax.dot_general(qrows, kb_var[kvh][half][qb * BLOCK:(qb + 3) * BLOCK], nt_dims,
                                     preferred_element_type=F32)
                s0 = jnp.where((col >= row) & (g > 0), sb[:, :BLOCK], NEG_INF)
                s1 = sb[:, BLOCK:2 * BLOCK]
                s2 = jnp.where((col <= row) & (g < nblk - 1), sb[:, 2 * BLOCK:], NEG_INF)
                sc = s_ctx[qb * 2 * BLOCK:(qb + 1) * 2 * BLOCK]
                ctx_blocks = [sc[:, cb * BLOCK:(cb + 1) * BLOCK] for cb in range(sc.shape[1] // BLOCK)]
                m = functools.reduce(jnp.maximum, [s0, s1, s2] + ctx_blocks)
                m = jnp.maximum(jnp.max(m, axis=-1, keepdims=True), sk)
                p0, p1, p2, pc = (jnp.exp(t - m) for t in (s0, s1, s2, sc))
                psum = functools.reduce(
                    jnp.add, [p0, p1, p2] + [pc[:, cb * BLOCK:(cb + 1) * BLOCK] for cb in range(len(ctx_blocks))])
                den = jnp.sum(psum, axis=-1, keepdims=True) + jnp.exp(sk - m)
                pb = jnp.concatenate([p0, p1, p2], axis=1).astype(BF16)
                o_band.append(jnp.dot(pb, vb_var[kvh][half][qb * BLOCK:(qb + 3) * BLOCK],
                                      preferred_element_type=F32))
                p_ctx.append(pc.astype(BF16))
                dens.append(den)
            o_ctx = jnp.dot(jnp.concatenate(p_ctx, axis=0), vx_var[kvh][half], preferred_element_type=F32)
            for qb in range(Q_BLOCKS):
                o = (o_band[qb] + o_ctx[qb * 2 * BLOCK:(qb + 1) * 2 * BLOCK]) / dens[qb]
                accs[qb] = o if accs[qb] is None else accs[qb] + o
        for qb in range(Q_BLOCKS):
            att_blocks[qb][2 * kvh] = accs[qb][:BLOCK]
            att_blocks[qb][2 * kvh + 1] = accs[qb][BLOCK:]

    u = u_ref[0]
    vg = vg_ref[0]
    bs = bs_ref[...]
    gm_blocks = [[None] * 4 for _ in range(Q_BLOCKS)]
    for j in range(GM_W // LANES):
        chunks = [vg[c * BLOCK:(c + 1) * BLOCK, j * LANES:(j + 1) * LANES] for c in range(Q_BLOCKS)]
        rhs = jnp.concatenate(
            [jnp.concatenate([jnp.where(low, v, zero) for v in chunks], axis=1),
             jnp.concatenate([jnp.where(low, zero, v) for v in chunks], axis=1)], axis=0)
        mixed = jnp.dot(wcat_ref[j], rhs, preferred_element_type=F32)
        bias = jnp.where(low, bs[:, 2 * j:2 * j + 1], bs[:, 2 * j + 1:2 * j + 2])
        for c in range(Q_BLOCKS):
            gm_blocks[c][j] = (u[c * BLOCK:(c + 1) * BLOCK, j * LANES:(j + 1) * LANES].astype(F32)
                               * (mixed[:, c * LANES:(c + 1) * LANES] + bias))

    mix = jnp.concatenate([jnp.concatenate(att_blocks[c] + gm_blocks[c], axis=1) for c in range(Q_BLOCKS)],
                          axis=0).astype(BF16)
    y = jnp.dot(mix, wout_ref[...], preferred_element_type=F32)
    mod = mod_ref[0, 0]
    o_ref[0] = x_ref[0] + mod[2:3] * y


def _attn_mixer(x, mod, sink, q, k, ksw, v, vsw, kx, ksx, vx, vsx, u, vg, wcat_bf, bs_t, wout_bf):
    b, s, _ = x.shape
    tq = Q_BLOCKS * BLOCK
    nb = s // BLOCK
    l = kx.shape[1]
    cur = lambda w: pl.BlockSpec((1, tq, w), lambda bi, n: (bi, n, 0))
    prv = lambda w: pl.BlockSpec((1, BLOCK, w), lambda bi, n: (bi, jnp.maximum(n * Q_BLOCKS - 1, 0), 0))
    nxt = lambda w: pl.BlockSpec((1, BLOCK, w), lambda bi, n: (bi, jnp.minimum((n + 1) * Q_BLOCKS, nb - 1), 0))
    cx = pl.BlockSpec((1, l, KV_W), lambda bi, n: (bi, 0, 0))
    return pl.pallas_call(
        _attn_kernel,
        grid=(b, s // tq),
        in_specs=[
            pl.BlockSpec(memory_space=pltpu.SMEM),
            cur(ATT_W),
            prv(KV_W), cur(KV_W), nxt(KV_W), prv(KV_W), cur(KV_W), nxt(KV_W),
            prv(KV_W), cur(KV_W), nxt(KV_W), prv(KV_W), cur(KV_W), nxt(KV_W),
            cx, cx, cx, cx,
            cur(GM_W), cur(GM_W),
            pl.BlockSpec((4, BLOCK, 2 * BLOCK), lambda bi, n: (0, 0, 0)),
            pl.BlockSpec((BLOCK, 8), lambda bi, n: (0, 0)),
            pl.BlockSpec((D, D), lambda bi, n: (0, 0)),
            cur(D),
            pl.BlockSpec((1, 1, 6, D), lambda bi, n: (0, bi, 0, 0)),
        ],
        out_specs=cur(D),
        out_shape=jax.ShapeDtypeStruct((b, s, D), F32),
        compiler_params=_cparams(("arbitrary", "arbitrary"), VMEM_LIMIT),
        name="attn_gmlp_out",
    )(sink, q, k, k, k, ksw, ksw, ksw, v, v, v, vsw, vsw, vsw, kx, ksx, vx, vsx,
      u, vg, wcat_bf, bs_t, wout_bf, x, mod)


def _ffn_kernel(x_ref, mod_ref, wg_ref, wu_ref, wd_ref, o_ref, act):
    mod = mod_ref[0, 0]
    xf = x_ref[...]
    h = _modulate(xf, mod[3:4], mod[4:5]).astype(BF16)
    for c in range(wg_ref.shape[1] // W_CHUNK):
        cs = slice(c * W_CHUNK, (c + 1) * W_CHUNK)
        g = jnp.dot(h, wg_ref[:, cs], preferred_element_type=F32)
        up = jnp.dot(h, wu_ref[:, cs], preferred_element_type=F32)
        act[:, cs] = (g * _sigmoid(g) * up).astype(BF16)
    o_ref[...] = xf + mod[5:6] * jnp.dot(act[...], wd_ref[...], preferred_element_type=F32)


def _dense_ffn(x2d, mod, wg, wu, wd, seq):
    n = x2d.shape[0]
    f = wg.shape[1]
    tm = TM_FFN
    per_b = seq // tm
    resident = lambda shp: pl.BlockSpec(shp, lambda i: (0, 0), pipeline_mode=pl.Buffered(1))
    return pl.pallas_call(
        _ffn_kernel,
        grid=(n // tm,),
        in_specs=[
            pl.BlockSpec((tm, D), lambda i: (i, 0)),
            pl.BlockSpec((1, 1, 6, D), lambda i: (0, i // per_b, 0, 0)),
            resident((D, f)), resident((D, f)), resident((f, D)),
        ],
        out_specs=pl.BlockSpec((tm, D), lambda i: (i, 0)),
        out_shape=jax.ShapeDtypeStruct((n, D), F32),
        scratch_shapes=[pltpu.VMEM((tm, f), BF16)],
        compiler_params=_cparams(("arbitrary",), VMEM_LIMIT),
        name="dense_ffn",
    )(x2d, mod, wg, wu, wd)


def _pool_route_kernel(x_ref, xp_ref, xn_ref, mod_ref, band_ref, pw_ref, psc_ref, wr_hi_ref, wr_lo_ref,
                       tri_ref, x3_ref, h2_ref, route_ref, cnt_ref, hext, carry):
    bi = pl.program_id(0)
    i = pl.program_id(1)
    ni = pl.num_programs(1)
    tm = x_ref.shape[1]
    seq = tm * ni
    mod = mod_ref[0, 0]

    @pl.when((bi == 0) & (i == 0))
    def _():
        carry[...] = jnp.zeros_like(carry)

    xf = x_ref[0]
    hp = _modulate(xp_ref[0], mod[0:1], mod[1:2])
    hn = _modulate(xn_ref[0], mod[0:1], mod[1:2])
    hext[0:POOL_HALO] = jnp.where(i > 0, hp, 0.0).astype(BF16)
    h_main = _modulate(xf, mod[0:1], mod[1:2])
    hext[POOL_HALO:POOL_HALO + tm] = h_main.astype(BF16)
    hext[POOL_HALO + tm:] = jnp.where(i < ni - 1, hn, 0.0).astype(BF16)

    t_local = lax.broadcasted_iota(jnp.int32, (BLOCK, 1), 0)
    ys = []
    for gi, w in enumerate(POOL_SIZES):
        lo_off = -(w // 2)
        hi_off = w - 1 - w // 2
        cols = slice(gi * POOL_GD, (gi + 1) * POOL_GD)
        outs = []
        for sb in range(tm // BLOCK):
            r0 = sb * BLOCK
            win = jnp.dot(band_ref[gi], hext[r0:r0 + BLOCK + 2 * POOL_HALO, cols],
                          preferred_element_type=F32)
            t = i * tm + r0 + t_local
            cnt = (jnp.minimum(t + hi_off, seq - 1) - jnp.maximum(t + lo_off, 0) + 1).astype(F32)
            diff = win / cnt - h_main[r0:r0 + BLOCK, cols]
            outs.append(diff.astype(BF16))
        dg = jnp.concatenate(outs, axis=0)
        ys.append(jnp.dot(dg, pw_ref[gi], preferred_element_type=F32))
    y = jnp.concatenate(ys, axis=1) * psc_ref[...]
    x3 = xf + mod[2:3] * y
    x3_ref[0] = x3

    h2 = _modulate(x3, mod[3:4], mod[4:5])
    _rows_to_slabs(h2, h2_ref)
    h_hi = h2.astype(BF16)
    h_lo = (h2 - h_hi.astype(F32)).astype(BF16)
    logits = (jnp.dot(h_hi, wr_hi_ref[...], preferred_element_type=F32)
              + jnp.dot(h_hi, wr_lo_ref[...], preferred_element_type=F32)
              + jnp.dot(h_lo, wr_hi_ref[...], preferred_element_type=F32))
    lane = lax.broadcasted_iota(jnp.int32, (tm, LANES), 1)
    lane_f = lane.astype(F32)
    neg = -jnp.inf
    lg = jnp.where(lane < N_EXPERTS, logits, neg)
    m1 = jnp.max(lg, axis=-1, keepdims=True)
    i1 = jnp.min(jnp.where(lg == m1, lane_f, float(LANES)), axis=-1, keepdims=True)
    oh1 = lane_f == i1
    lg2 = jnp.where(oh1, neg, lg)
    m2 = jnp.max(lg2, axis=-1, keepdims=True)
    i2 = jnp.min(jnp.where(lg2 == m2, lane_f, float(LANES)), axis=-1, keepdims=True)
    oh2 = lane_f == i2
    e = jnp.exp(m2 - m1)
    w1 = 1.0 / (1.0 + e)
    w2 = e / (1.0 + e)
    oh = jnp.where(oh1 | oh2, 1.0, 0.0)
    before = jnp.dot(tri_ref[...], oh.astype(BF16), preferred_element_type=F32) + carry[...]
    r1 = jnp.sum(jnp.where(oh1, before, 0.0), axis=-1, keepdims=True)
    r2 = jnp.sum(jnp.where(oh2, before, 0.0), axis=-1, keepdims=True)
    carry[...] = carry[...] + jnp.sum(oh, axis=0, keepdims=True)
    cnt_ref[...] = carry[...]
    info = jnp.where(lane == 0, i1, jnp.where(lane == 1, i2, jnp.where(lane == 2, w1, jnp.where(
        lane == 3, w2, jnp.where(lane == 4, r1, jnp.where(lane == 5, r2, 0.0))))))
    route_ref[...] = info.T[0:8, :]


def _pool_route(x, mod, band, pw_bf, pool_scale, wr_hi, wr_lo, tri):
    b, s, _ = x.shape
    tm = TM_POOL
    ni = s // tm
    hb = tm // POOL_HALO
    row = pl.BlockSpec((1, tm, D), lambda bi, i: (bi, i, 0))
    const2 = lambda shp: pl.BlockSpec(shp, lambda bi, i: (0,) * len(shp))
    return pl.pallas_call(
        _pool_route_kernel,
        grid=(b, ni),
        in_specs=[
            row,
            pl.BlockSpec((1, POOL_HALO, D), lambda bi, i: (bi, jnp.maximum(i * hb - 1, 0), 0)),
            pl.BlockSpec((1, POOL_HALO, D), lambda bi, i: (bi, jnp.minimum((i + 1) * hb, s // POOL_HALO - 1), 0)),
            pl.BlockSpec((1, 1, 6, D), lambda bi, i: (1, bi, 0, 0)),
            const2(band.shape), const2(pw_bf.shape), const2((1, D)),
            const2(wr_hi.shape), const2(wr_lo.shape), const2(tri.shape),
        ],
        out_specs=[row,
                   pl.BlockSpec((tm * SLAB, LANES), lambda bi, i: (bi * ni + i, 0)),
                   pl.BlockSpec((8, tm), lambda bi, i: (0, bi * ni + i)),
                   pl.BlockSpec((1, LANES), lambda bi, i: (0, 0))],
        out_shape=[jax.ShapeDtypeStruct((b, s, D), F32), jax.ShapeDtypeStruct((b * s * SLAB, LANES), F32),
                   jax.ShapeDtypeStruct((8, b * s), F32), jax.ShapeDtypeStruct((1, LANES), F32)],
        scratch_shapes=[pltpu.VMEM((tm + 2 * POOL_HALO, D), BF16), pltpu.VMEM((1, LANES), F32)],
        compiler_params=_cparams(("arbitrary", "arbitrary"), VMEM_LIMIT),
        name="pool_route",
    )(x, x, x, mod, band, pw_bf, pool_scale.reshape(1, D), wr_hi, wr_lo, tri)


def _slotmap_kernel(pos_ref, lo_ref, hi_ref, o_ref):
    n_pairs = pos_ref.shape[0]
    spare_mask = 2 * TM_MOE - 1
    for e in range(lo_ref.shape[0]):
        def fill(p, c):
            o_ref[p] = n_pairs + (p & spare_mask)
            return c
        lax.fori_loop(lo_ref[e], hi_ref[e], fill, 0)

    def place(f, c):
        o_ref[pos_ref[f]] = f
        return c
    lax.fori_loop(0, n_pairs, place, 0, unroll=32)


def _slot_map(pos_flat, lo, hi, n_slots):
    smem = pl.BlockSpec(memory_space=pltpu.SMEM)
    return pl.pallas_call(
        _slotmap_kernel,
        in_specs=[smem, smem, smem],
        out_specs=smem,
        out_shape=jax.ShapeDtypeStruct((n_slots,), jnp.int32),
        name="moe_slot_map",
    )(pos_flat, lo, hi)


def _moe_kernel(te_ref, nused_ref, half_ref, fnext_ref, fprev_ref, f0_ref, h_hbm, wg_hbm, wu_hbm, wd_hbm,
                y_hbm, wg_res, wu_res, wd_res, stg_in, stg_out, xbuf, xb, act, acc, stage, gsem, ssem, wsem):
    i = pl.program_id(0)
    nt = pl.num_programs(0)
    used_tiles = nused_ref[0]
    tm = xb.shape[0]
    f_dim = wg_res.shape[1]
    tok_mask = h_hbm.shape[0] // SLAB - 1
    tile_rows = tm * SLAB
    out_rows = tm * PACK

    def slab(ix, width=SLAB):
        return pl.ds(pl.multiple_of(ix * width, width), width)

    def gather_row(fref, r, slot, zero=0):
        tok = fref[0, 0, r + zero] & tok_mask
        return pltpu.make_async_copy(h_hbm.at[slab(tok)], xbuf.at[slot, slab(r)], gsem.at[slot])

    def scatter_row(fref, r, slot, zero=0):
        return pltpu.make_async_copy(stage.at[slot, slab(r, PACK)], y_hbm.at[slab(fref[0, 0, r + zero], PACK)],
                                     ssem.at[slot])

    def gather_all(slot):
        return pltpu.make_async_copy(h_hbm.at[pl.ds(0, tile_rows)], xbuf.at[slot], gsem.at[slot])

    def scatter_all(slot):
        return pltpu.make_async_copy(stage.at[slot], y_hbm.at[pl.ds(0, out_rows)], ssem.at[slot])

    cur = i % 2
    used = i < used_tiles

    @pl.when(i == 0)
    def _():
        stage[...] = jnp.zeros_like(stage)
        spare = y_hbm.shape[0] - 2 * out_rows
        fills = [pltpu.make_async_copy(stage.at[sl], y_hbm.at[pl.ds(spare + sl * out_rows, out_rows)],
                                       ssem.at[sl])
                 for sl in range(2)]
        for cp in fills:
            cp.start()
        for cp in fills:
            cp.wait()

    @pl.when((i == 0) & used)
    def _():
        def prime(r, c):
            gather_row(f0_ref, r, 0).start()
            return c
        lax.fori_loop(0, tm, prime, 0)

    expert = te_ref[i]
    new_expert = used & ((i == 0) | (expert != te_ref[jnp.maximum(i - 1, 0)]))

    has_next = i + 1 < used_tiles
    has_prev = (i >= 1) & (i - 1 < used_tiles)
    steady = (i >= 1) & has_next
    half_tile = half_ref[i] == 1

    @pl.when(used)
    def _():
        gather_all(cur).wait()

    @pl.when((i >= 2) & (i - 2 < used_tiles))
    def _():
        scatter_all(cur).wait()

    half = f_dim // MOE_SPLIT
    n_piece = half // W_CHUNK
    n_groups = MOE_SPLIT * (n_piece + 1)

    def tick(v):
        bits = jnp.max(lax.bitcast_convert_type(v[0:SLAB, 0:LANES], jnp.int32))
        return lax.shift_right_logical(lax.shift_right_logical(bits, 16), 16)

    def expert_ffn(issue_group, rows=tm):
        for cix in range(SLAB):
            xb[0:rows, cix * LANES:(cix + 1) * LANES] = (
                xbuf.at[cur][pl.ds(cix, rows, stride=SLAB), :].astype(BF16))
        xv = xb[0:rows]
        out = None
        issue_group(0, 0)
        k = 1
        for hf in range(MOE_SPLIT):
            for c in range(n_piece):
                cs = slice(hf * half + c * W_CHUNK, hf * half + (c + 1) * W_CHUNK)
                g = jnp.dot(xv, wg_res[:, cs], preferred_element_type=F32)
                up = jnp.dot(xv, wu_res[:, cs], preferred_element_type=F32)
                act[0:rows, c * W_CHUNK:(c + 1) * W_CHUNK] = (g * _sigmoid(g) * up).astype(BF16)
                issue_group(k, tick(g))
                k += 1
            part = jnp.dot(act[0:rows], wd_res[hf * half:(hf + 1) * half, :], preferred_element_type=F32)
            if hf < MOE_SPLIT - 1:
                acc[0:rows] = part if out is None else acc[0:rows] + part
                out = acc
                issue_group(k, tick(part))
                k += 1
            else:
                _pack_rows(part if out is None else acc[0:rows] + part, stage.at[cur])

    def chunk_plan():
        n_in = D // W_ROWS_IN
        plan = [(w, res, stg_in, W_RING_IN, 0, W_ROWS_IN, c, wi * n_in + c)
                for wi, (w, res) in enumerate(((wg_hbm, wg_res), (wu_hbm, wu_res))) for c in range(n_in)]
        return plan + [(wd_hbm, wd_res, stg_out, W_RING_OUT, W_RING_IN, W_ROWS_OUT, c, c)
                       for c in range(f_dim // W_ROWS_OUT)]

    def chunk_copy(entry, ex):
        w_hbm, _, stg, ring, sem0, rows, c, j = entry
        return pltpu.make_async_copy(w_hbm.at[ex, pl.ds(c * rows, rows), :], stg.at[j % ring],
                                     wsem.at[sem0 + j % ring])

    def start_first_chunks(ex):
        for entry in chunk_plan():
            if entry[7] < entry[3]:
                chunk_copy(entry, ex).start()

    @pl.when(new_expert)
    def _():
        plan = chunk_plan()

        @pl.when(i == 0)
        def _():
            start_first_chunks(expert)

        for k, entry in enumerate(plan):
            _, res, stg, ring, _, rows, c, j = entry
            chunk_copy(entry, expert).wait()
            res[c * rows:(c + 1) * rows, :] = stg[j % ring].astype(BF16)
            later = [e for e in plan[k + 1:] if e[2] is stg and e[7] == j + ring]
            if later:
                chunk_copy(later[0], expert).start()

    next_expert = te_ref[jnp.minimum(i + 1, nt - 1)]

    @pl.when(has_next & (next_expert != expert))
    def _():
        start_first_chunks(next_expert)

    def issue_group(k, zero):
        for r in range(k * tm // n_groups, (k + 1) * tm // n_groups):
            gather_row(fnext_ref, r, 1 - cur, zero).start()
            scatter_row(fprev_ref, r, 1 - cur, zero).start()

    @pl.when(steady & jnp.logical_not(half_tile))
    def _():
        expert_ffn(issue_group)

    @pl.when(steady & half_tile)
    def _():
        expert_ffn(issue_group, tm // 2)

    @pl.when(jnp.logical_not(steady))
    def _():
        @pl.when(used)
        def _():
            expert_ffn(lambda k, zero: None)

        @pl.when(has_next)
        def _():
            def issue(r, c):
                gather_row(fnext_ref, r, 1 - cur).start()
                return c
            lax.fori_loop(0, tm, issue, 0)

        @pl.when(has_prev)
        def _():
            def issue(r, c):
                scatter_row(fprev_ref, r, 1 - cur).start()
                return c
            lax.fori_loop(0, tm, issue, 0)

    @pl.when((i == nt - 1) & (nt - 2 < used_tiles))
    def _():
        scatter_all(1 - cur).wait()


def _moe_experts(h_slabs, fmap, tile_expert, n_used, tile_half, wg, wu, wd, n_tiles, y_rows):
    f = wg.shape[2]
    tm = TM_MOE
    fblk = lambda imap: pl.BlockSpec((1, 1, tm), imap, memory_space=pltpu.SMEM)
    hbm = pl.BlockSpec(memory_space=pl.ANY)
    return pl.pallas_call(
        _moe_kernel,
        grid_spec=pltpu.PrefetchScalarGridSpec(
            num_scalar_prefetch=3,
            grid=(n_tiles,),
            in_specs=[
                fblk(lambda i, te, nu, hf: (jnp.minimum(i + 1, n_tiles - 1), 0, 0)),
                fblk(lambda i, te, nu, hf: (jnp.maximum(i - 1, 0), 0, 0)),
                fblk(lambda i, te, nu, hf: (0, 0, 0)),
                hbm, hbm, hbm, hbm,
            ],
            out_specs=hbm,
            scratch_shapes=[pltpu.VMEM((D, f), BF16), pltpu.VMEM((D, f), BF16), pltpu.VMEM((f, D), BF16),
                            pltpu.VMEM((W_RING_IN, W_ROWS_IN, f), F32), pltpu.VMEM((W_RING_OUT, W_ROWS_OUT, D), F32),
                            pltpu.VMEM((2, tm * SLAB, LANES), F32), pltpu.VMEM((tm, D), BF16),
                            pltpu.VMEM((tm, f // MOE_SPLIT), BF16), pltpu.VMEM((tm, D), F32),
                            pltpu.VMEM((2, tm * PACK, LANES), jnp.uint32),
                            pltpu.SemaphoreType.DMA((2,)), pltpu.SemaphoreType.DMA((2,)),
                            pltpu.SemaphoreType.DMA((W_RING_IN + W_RING_OUT,))],
        ),
        out_shape=jax.ShapeDtypeStruct((y_rows * PACK, LANES), jnp.uint32),
        compiler_params=_cparams(("arbitrary",), MOE_VMEM_LIMIT),
        name="moe_experts",
    )(tile_expert, n_used, tile_half, fmap, fmap, fmap, h_slabs, wg, wu, wd)


def _combine_kernel(y1_ref, y2_ref, x_ref, w_ref, mod_ref, gain_ref, o_ref):
    w = w_ref[...]
    rows = x_ref.shape[0]
    moe = w[:, 0:1] * _unpack_rows(y1_ref, rows) + w[:, 1:2] * _unpack_rows(y2_ref, rows)
    mod = mod_ref[0, 0]
    x4 = x_ref[...] + mod[5:6] * moe
    ms = jnp.mean(x4 * x4, axis=-1, keepdims=True)
    o_ref[...] = x4 * lax.rsqrt(ms + EPS) * gain_ref[...]


def _combine(y, x3_2d, wts, mod, final_gain, seq):
    n = x3_2d.shape[0]
    tc = TC_COMB
    nt = n // tc
    per_b = seq // tc
    return pl.pallas_call(
        _combine_kernel,
        grid=(nt,),
        in_specs=[
            pl.BlockSpec((tc * PACK, LANES), lambda i: (i, 0)),
            pl.BlockSpec((tc * PACK, LANES), lambda i: (i + nt, 0)),
            pl.BlockSpec((tc, D), lambda i: (i, 0)),
            pl.BlockSpec((tc, 2), lambda i: (i, 0)),
            pl.BlockSpec((1, 1, 6, D), lambda i: (1, i // per_b, 0, 0)),
            pl.BlockSpec((1, D), lambda i: (0, 0)),
        ],
        out_specs=pl.BlockSpec((tc, D), lambda i: (i, 0)),
        out_shape=jax.ShapeDtypeStruct((n, D), F32),
        compiler_params=_cparams(("arbitrary",)),
        name="moe_combine",
    )(y, y, x3_2d, wts, mod, final_gain.reshape(1, D))


def _rope_tables(seq):
    rows = seq // GRID_W
    row_pos = jnp.repeat(jnp.arange(rows, dtype=F32), GRID_W)
    col_pos = jnp.tile(jnp.arange(GRID_W, dtype=F32), rows)
    axis_dim = HEAD_DIM // 2
    inv_freq = ROPE_BASE ** (-jnp.arange(0, axis_dim, 2, dtype=F32) / axis_dim)
    ar = row_pos[:, None] * inv_freq
    ac = col_pos[:, None] * inv_freq
    cos64 = jnp.concatenate([jnp.cos(ar), jnp.cos(ar), jnp.cos(ac), jnp.cos(ac)], axis=1)
    sin64 = jnp.concatenate([-jnp.sin(ar), jnp.sin(ar), -jnp.sin(ac), jnp.sin(ac)], axis=1)
    return jnp.tile(cos64, (1, 2)), jnp.tile(sin64, (1, 2))


def _band_matrices():
    r = np.arange(BLOCK)[:, None]
    c = np.arange(BLOCK + 2 * POOL_HALO)[None, :] - POOL_HALO
    mats = []
    for w in POOL_SIZES:
        lo = -(w // 2)
        hi = w - 1 - w // 2
        mats.append(((c >= r + lo) & (c <= r + hi)).astype(np.float32))
    return jnp.asarray(np.stack(mats), dtype=BF16)


def kernel(x, c, ctx, c_ctx, w_ada, b_ada, w_in, attn_sink, gm_gain, gm_w_s, gm_b_s, w_out,
           ffn_w_gate, ffn_w_up, ffn_w_down, pool_w, pool_scale, router_w,
           moe_w_gate, moe_w_up, moe_w_down, final_gain):
    b, s, _ = x.shape
    n = b * s
    assert w_ada.shape[0] == 2 and w_in.shape[0] == 1 and pool_w.shape[0] == 1
    assert s % TM_IN == 0 and s % TM_POOL == 0 and s % TM_FFN == 0 and b <= 4
    assert n & (n - 1) == 0

    cvec = jnp.concatenate([c, c_ctx[None, :], jnp.zeros((8 - b - 1, D), F32)], axis=0)
    mod = _ada_mod(cvec, w_ada, b_ada)

    cos_t, sin_t = _rope_tables(s)
    w_in_bf = w_in[0].astype(BF16)
    q, k, ksw, v, vsw, u, vg = _in_proj(x, mod, w_in_bf, gm_gain[0], cos_t, sin_t)
    kx, ksx, vx, vsx = _ctx_kv(ctx, mod, w_in_bf)
    wcat = gm_w_s[0].reshape(4, 2, BLOCK, BLOCK).transpose(0, 2, 1, 3).reshape(4, BLOCK, 2 * BLOCK).astype(BF16)
    x1 = _attn_mixer(x, mod, attn_sink[0], q, k, ksw, v, vsw, kx, ksx, vx, vsx, u, vg,
                     wcat, gm_b_s[0].T, w_out[0].astype(BF16))
    x2 = _dense_ffn(x1.reshape(n, D), mod, ffn_w_gate[0].astype(BF16), ffn_w_up[0].astype(BF16),
                    ffn_w_down[0].astype(BF16), s)

    wr = jnp.pad(router_w[0], ((0, 0), (0, LANES - N_EXPERTS)))
    wr_hi = wr.astype(BF16)
    wr_lo = (wr - wr_hi.astype(F32)).astype(BF16)
    tri = jnp.asarray(np.tril(np.ones((TM_POOL, TM_POOL), np.float32), -1), dtype=BF16)
    x3, h2, route, counts = _pool_route(x2.reshape(b, s, D), mod, _band_matrices(), pool_w[0].astype(BF16),
                                        pool_scale[0], wr_hi, wr_lo, tri)

    tm = TM_MOE
    n_tiles = (2 * n) // tm + N_EXPERTS
    cnt = counts[0, :N_EXPERTS].astype(jnp.int32)
    tiles_e = (cnt + tm - 1) // tm
    tile_end = jnp.cumsum(tiles_e)
    off = (tile_end - tiles_e) * tm
    n_used = tile_end[-1]
    tix = jnp.arange(n_tiles, dtype=jnp.int32)
    te = jnp.minimum(jnp.sum(tix[:, None] >= tile_end[None, :], axis=1), N_EXPERTS - 1).astype(jnp.int32)
    te_last = te[jnp.maximum(n_used - 1, 0)]
    tile_expert = jnp.where(tix < n_used, te, te_last)
    rows_left = cnt[te] - (tix - (tile_end - tiles_e)[te]) * tm
    tile_half = ((tix < n_used) & (rows_left <= tm // 2)).astype(jnp.int32)
    e1 = route[0].astype(jnp.int32)
    e2 = route[1].astype(jnp.int32)
    pos1 = off[e1] + route[4].astype(jnp.int32)
    pos2 = off[e2] + route[5].astype(jnp.int32)
    n_slots = n_tiles * tm
    pad_lo = jnp.concatenate([off + cnt, (n_used * tm).reshape(1)]).astype(jnp.int32)
    pad_hi = jnp.concatenate([off + tiles_e * tm, jnp.full((1,), n_slots, jnp.int32)]).astype(jnp.int32)
    fmap = _slot_map(jnp.concatenate([pos1, pos2]), pad_lo, pad_hi, n_slots)
    n_used_arr = n_used.reshape(1).astype(jnp.int32)

    y = _moe_experts(h2, fmap.reshape(n_tiles, 1, tm), tile_expert, n_used_arr, tile_half,
                     moe_w_gate[0], moe_w_up[0], moe_w_down[0], n_tiles, 2 * n + 2 * tm)
    out = _combine(y, x3.reshape(n, D), route[2:4].T, mod, final_gain, s)
    return out.reshape(b, s, D)
```

```python
import functools

import numpy as np
import jax
import jax.numpy as jnp
from jax import lax
from jax.experimental import pallas as pl
from jax.experimental.pallas import tpu as pltpu

F32 = jnp.float32
BF16 = jnp.bfloat16

D = 1024
GRID_W = 64
EPS = 1e-6
NEG_INF = -1e30
HEAD_DIM = 64
N_Q_HEADS = 8
BLOCK = 128
ATT_W = 512
KV_W = 128
GM_W = 512
IN_W = 1792
POOL_SIZES = (2, 4, 8, 16)
POOL_GD = 256
POOL_HALO = 16
N_EXPERTS = 8
ROPE_BASE = 10000.0
LANES = 128
SLAB = D // LANES
PACK = SLAB // 2
SQRT_2_OVER_PI = 0.7978845608028654

TM_IN = 1024
TM_FFN = 1024
TM_POOL = 512
TM_MOE = 512
MOE_SPLIT = 2
W_CHUNK = 256
W_ROWS_IN = 128
W_ROWS_OUT = 512
W_RING_IN = 12
W_RING_OUT = 2
MOE_VMEM_LIMIT = 60 * 1024 * 1024
TC_COMB = 1024
Q_BLOCKS = 8
VMEM_LIMIT = 56 * 1024 * 1024


def _cparams(sem, vmem=None):
    return pltpu.CompilerParams(dimension_semantics=sem, vmem_limit_bytes=vmem)


def _modulate(xf, shift, scale):
    ms = jnp.mean(xf * xf, axis=-1, keepdims=True)
    return xf * lax.rsqrt(ms + EPS) * (1.0 + scale) + shift


def _sigmoid(z):
    return 1.0 / (1.0 + jnp.exp(-z))


def _rows_to_slabs(val, slab_ref):
    rows = val.shape[0]
    for cix in range(SLAB):
        slab_ref[pl.ds(cix, rows, stride=SLAB), :] = val[:, cix * LANES:(cix + 1) * LANES]


def _slabs_to_rows(slab_ref, rows):
    return jnp.concatenate([slab_ref[pl.ds(cix, rows, stride=SLAB), :] for cix in range(SLAB)], axis=1)


def _pack_rows(val, pack_ref):
    rows = val.shape[0]
    bits = lambda v: lax.bitcast_convert_type(v.astype(BF16).astype(F32), jnp.uint32)
    for cix in range(PACK):
        hi = bits(val[:, cix * LANES:(cix + 1) * LANES])
        lo = bits(val[:, D // 2 + cix * LANES:D // 2 + (cix + 1) * LANES])
        pack_ref[pl.ds(cix, rows, stride=PACK), :] = hi | (lo >> 16)


def _unpack_rows(pack_ref, rows):
    words = [pack_ref[pl.ds(cix, rows, stride=PACK), :] for cix in range(PACK)]
    his = [lax.bitcast_convert_type(w & jnp.uint32(0xFFFF0000), F32) for w in words]
    los = [lax.bitcast_convert_type(w << 16, F32) for w in words]
    return jnp.concatenate(his + los, axis=1)


def _ada_kernel(c_ref, w_ref, b_ref, o_ref):
    c = c_ref[...]
    s = c * _sigmoid(c)
    o_ref[0] = jnp.dot(s.astype(BF16), w_ref[0].astype(BF16), preferred_element_type=F32) + b_ref[0]


def _ada_mod(cvec, w_ada, b_ada):
    depth, _, n6 = w_ada.shape
    tn = 1536
    out = pl.pallas_call(
        _ada_kernel,
        grid=(depth, n6 // tn),
        in_specs=[
            pl.BlockSpec((8, D), lambda l, j: (0, 0)),
            pl.BlockSpec((1, D, tn), lambda l, j: (l, 0, j)),
            pl.BlockSpec((1, 1, tn), lambda l, j: (l, 0, j)),
        ],
        out_specs=pl.BlockSpec((1, 8, tn), lambda l, j: (l, 0, j)),
        out_shape=jax.ShapeDtypeStruct((depth, 8, n6), F32),
        compiler_params=_cparams(("arbitrary", "arbitrary")),
        name="ada_mod",
    )(cvec, w_ada, b_ada.reshape(depth, 1, n6))
    return out.reshape(depth, 8, 6, D)


def _rope(t, cs, sn, first_half):
    fwd = pltpu.roll(t, LANES - 16, axis=1)
    bwd = pltpu.roll(t, 16, axis=1)
    return t * cs + jnp.where(first_half, fwd, bwd) * sn


def _inproj_kernel(x_ref, mod_ref, w_ref, gain_ref, cos_ref, sin_ref,
                   q_ref, k_ref, ksw_ref, v_ref, vsw_ref, u_ref, vg_ref):
    mod = mod_ref[0, 0]
    h = _modulate(x_ref[0], mod[0:1], mod[1:2]).astype(BF16)
    proj = jnp.dot(h, w_ref[...], preferred_element_type=F32)
    cs = cos_ref[...]
    sn = sin_ref[...]
    lane = lax.broadcasted_iota(jnp.int32, cs.shape, 1)
    first_half = (lane & 16) == 0
    for cix in range(ATT_W // LANES):
        t = proj[:, cix * LANES:(cix + 1) * LANES]
        q_ref[0, :, cix * LANES:(cix + 1) * LANES] = (
            _rope(t, cs, sn, first_half) * (HEAD_DIM ** -0.5)).astype(BF16)
    kr = _rope(proj[:, ATT_W:ATT_W + KV_W], cs, sn, first_half)
    k_ref[0] = kr.astype(BF16)
    ksw_ref[0] = pltpu.roll(kr, HEAD_DIM, axis=1).astype(BF16)
    vv = proj[:, ATT_W + KV_W:ATT_W + 2 * KV_W]
    v_ref[0] = vv.astype(BF16)
    vsw_ref[0] = pltpu.roll(vv, HEAD_DIM, axis=1).astype(BF16)
    z = proj[:, ATT_W + 2 * KV_W:]
    g = z * (0.5 * (1.0 + jnp.tanh(SQRT_2_OVER_PI * (z + 0.044715 * (z * z * z)))))
    u_ref[0] = g[:, :GM_W].astype(BF16)
    vg = g[:, GM_W:]
    ms = jnp.mean(vg * vg, axis=-1, keepdims=True)
    vg_ref[0] = (vg * lax.rsqrt(ms + EPS) * gain_ref[...]).astype(BF16)


def _in_proj(x, mod, w_in_bf, gm_gain, cos_t, sin_t):
    b, s, _ = x.shape
    tm = TM_IN
    row = lambda w: pl.BlockSpec((1, tm, w), lambda bi, i: (bi, i, 0))
    outs = pl.pallas_call(
        _inproj_kernel,
        grid=(b, s // tm),
        in_specs=[
            row(D),
            pl.BlockSpec((1, 1, 6, D), lambda bi, i: (0, bi, 0, 0)),
            pl.BlockSpec((D, IN_W), lambda bi, i: (0, 0)),
            pl.BlockSpec((1, GM_W), lambda bi, i: (0, 0)),
            pl.BlockSpec((tm, LANES), lambda bi, i: (i, 0)),
            pl.BlockSpec((tm, LANES), lambda bi, i: (i, 0)),
        ],
        out_specs=[row(ATT_W), row(KV_W), row(KV_W), row(KV_W), row(KV_W), row(GM_W), row(GM_W)],
        out_shape=[jax.ShapeDtypeStruct((b, s, w), BF16)
                   for w in (ATT_W, KV_W, KV_W, KV_W, KV_W, GM_W, GM_W)],
        compiler_params=_cparams(("arbitrary", "arbitrary"), VMEM_LIMIT),
        name="in_proj",
    )(x, mod, w_in_bf, gm_gain.reshape(1, GM_W), cos_t, sin_t)
    return outs


def _ctx_kernel(c_ref, mod_ref, w_ref, k_ref, ksw_ref, v_ref, vsw_ref):
    mod = mod_ref[0, 0]
    h = _modulate(c_ref[0], mod[0:1], mod[1:2]).astype(BF16)
    kv = jnp.dot(h, w_ref[...], preferred_element_type=F32)
    kk = kv[:, :KV_W]
    vv = kv[:, KV_W:]
    k_ref[0] = kk.astype(BF16)
    ksw_ref[0] = pltpu.roll(kk, HEAD_DIM, axis=1).astype(BF16)
    v_ref[0] = vv.astype(BF16)
    vsw_ref[0] = pltpu.roll(vv, HEAD_DIM, axis=1).astype(BF16)


def _ctx_kv(ctx, mod, w_in_bf):
    b, l, _ = ctx.shape
    spec = pl.BlockSpec((1, l, KV_W), lambda bi: (bi, 0, 0))
    return pl.pallas_call(
        _ctx_kernel,
        grid=(b,),
        in_specs=[
            pl.BlockSpec((1, l, D), lambda bi: (bi, 0, 0)),
            pl.BlockSpec((1, 1, 6, D), lambda bi: (0, b, 0, 0)),
            pl.BlockSpec((D, 2 * KV_W), lambda bi: (0, ATT_W // (2 * KV_W))),
        ],
        out_specs=[spec] * 4,
        out_shape=[jax.ShapeDtypeStruct((b, l, KV_W), BF16)] * 4,
        compiler_params=_cparams(("arbitrary",)),
        name="ctx_kv",
    )(ctx, mod, w_in_bf)


def _attn_kernel(sink_ref, q_ref, kp_ref, kc_ref, kn_ref, ksp_ref, ksc_ref, ksn_ref,
                 vp_ref, vc_ref, vn_ref, vsp_ref, vsc_ref, vsn_ref,
                 kx_ref, ksx_ref, vx_ref, vsx_ref,
                 u_ref, vg_ref, wcat_ref, bs_ref, wout_ref, x_ref, mod_ref, o_ref):
    n = pl.program_id(1)
    nblk = pl.num_programs(1) * Q_BLOCKS
    lane = lax.broadcasted_iota(jnp.int32, (1, LANES), 1)
    low = lane < HEAD_DIM
    zero = jnp.zeros((), BF16)

    def variants(a0, a1):
        return ((jnp.where(low, a0, zero), jnp.where(low, zero, a1)),
                (jnp.where(low, a1, zero), jnp.where(low, zero, a0)))

    cat = lambda refs: jnp.concatenate([r[0] for r in refs], axis=0)
    kb_var = variants(cat((kp_ref, kc_ref, kn_ref)), cat((ksp_ref, ksc_ref, ksn_ref)))
    vb_var = variants(cat((vp_ref, vc_ref, vn_ref)), cat((vsp_ref, vsc_ref, vsn_ref)))
    kx_var = variants(kx_ref[0], ksx_ref[0])
    vx_var = variants(vx_ref[0], vsx_ref[0])

    row = lax.broadcasted_iota(jnp.int32, (2 * BLOCK, BLOCK), 0) & (BLOCK - 1)
    col = lax.broadcasted_iota(jnp.int32, (2 * BLOCK, BLOCK), 1)
    top = lax.broadcasted_iota(jnp.int32, (2 * BLOCK, 1), 0) < BLOCK
    nt_dims = (((1,), (1,)), ((), ()))

    q = q_ref[0]
    att_blocks = [[None] * 4 for _ in range(Q_BLOCKS)]
    for kvh in range(2):
        qst = jnp.concatenate(
            [q[qb * BLOCK:(qb + 1) * BLOCK, pr * LANES:(pr + 1) * LANES]
             for qb in range(Q_BLOCKS) for pr in (2 * kvh, 2 * kvh + 1)], axis=0)
        accs = [None] * Q_BLOCKS
        for half in range(2):
            sk = jnp.where(top, sink_ref[4 * kvh + half], sink_ref[4 * kvh + 2 + half])
            s_ctx = lax.dot_general(qst, kx_var[kvh][half], nt_dims, preferred_element_type=F32)
            o_band, p_ctx, dens = [], [], []
            for qb in range(Q_BLOCKS):
                g = n * Q_BLOCKS + qb
                qrows = qst[qb * 2 * BLOCK:(qb + 1) * 2 * BLOCK]
                sb = lax.dot_general(qrows, kb_var[kvh][half][qb * BLOCK:(qb + 3) * BLOCK], nt_dims,
                                     preferred_element_type=F32)
                s0 = jnp.where((col >= row) & (g > 0), sb[:, :BLOCK], NEG_INF)
                s1 = sb[:, BLOCK:2 * BLOCK]
                s2 = jnp.where((col <= row) & (g < nblk - 1), sb[:, 2 * BLOCK:], NEG_INF)
                sc = s_ctx[qb * 2 * BLOCK:(qb + 1) * 2 * BLOCK]
                ctx_blocks = [sc[:, cb * BLOCK:(cb + 1) * BLOCK] for cb in range(sc.shape[1] // BLOCK)]
                m = functools.reduce(jnp.maximum, [s0, s1, s2] + ctx_blocks)
                m = jnp.maximum(jnp.max(m, axis=-1, keepdims=True), sk)
                p0, p1, p2, pc = (jnp.exp(t - m) for t in (s0, s1, s2, sc))
                psum = functools.reduce(
                    jnp.add, [p0, p1, p2] + [pc[:, cb * BLOCK:(cb + 1) * BLOCK] for cb in range(len(ctx_blocks))])
                den = jnp.sum(psum, axis=-1, keepdims=True) + jnp.exp(sk - m)
                pb = jnp.concatenate([p0, p1, p2], axis=1).astype(BF16)
                o_band.append(jnp.dot(pb, vb_var[kvh][half][qb * BLOCK:(qb + 3) * BLOCK],
                                      preferred_element_type=F32))
                p_ctx.append(pc.astype(BF16))
                dens.append(den)
            o_ctx = jnp.dot(jnp.concatenate(p_ctx, axis=0), vx_var[kvh][half], preferred_element_type=F32)
            for qb in range(Q_BLOCKS):
                o = (o_band[qb] + o_ctx[qb * 2 * BLOCK:(qb + 1) * 2 * BLOCK]) / dens[qb]
                accs[qb] = o if accs[qb] is None else accs[qb] + o
        for qb in range(Q_BLOCKS):
            att_blocks[qb][2 * kvh] = accs[qb][:BLOCK]
            att_blocks[qb][2 * kvh + 1] = accs[qb][BLOCK:]

    u = u_ref[0]
    vg = vg_ref[0]
    bs = bs_ref[...]
    gm_blocks = [[None] * 4 for _ in range(Q_BLOCKS)]
    for j in range(GM_W // LANES):
        chunks = [vg[c * BLOCK:(c + 1) * BLOCK, j * LANES:(j + 1) * LANES] for c in range(Q_BLOCKS)]
        rhs = jnp.concatenate(
            [jnp.concatenate([jnp.where(low, v, zero) for v in chunks], axis=1),
             jnp.concatenate([jnp.where(low, zero, v) for v in chunks], axis=1)], axis=0)
        mixed = jnp.dot(wcat_ref[j], rhs, preferred_element_type=F32)
        bias = jnp.where(low, bs[:, 2 * j:2 * j + 1], bs[:, 2 * j + 1:2 * j + 2])
        for c in range(Q_BLOCKS):
            gm_blocks[c][j] = (u[c * BLOCK:(c + 1) * BLOCK, j * LANES:(j + 1) * LANES].astype(F32)
                               * (mixed[:, c * LANES:(c + 1) * LANES] + bias))

    mix = jnp.concatenate([jnp.concatenate(att_blocks[c] + gm_blocks[c], axis=1) for c in range(Q_BLOCKS)],
                          axis=0).astype(BF16)
    y = jnp.dot(mix, wout_ref[...], preferred_element_type=F32)
    mod = mod_ref[0, 0]
    o_ref[0] = x_ref[0] + mod[2:3] * y


def _attn_mixer(x, mod, sink, q, k, ksw, v, vsw, kx, ksx, vx, vsx, u, vg, wcat_bf, bs_t, wout_bf):
    b, s, _ = x.shape
    tq = Q_BLOCKS * BLOCK
    nb = s // BLOCK
    l = kx.shape[1]
    cur = lambda w: pl.BlockSpec((1, tq, w), lambda bi, n: (bi, n, 0))
    prv = lambda w: pl.BlockSpec((1, BLOCK, w), lambda bi, n: (bi, jnp.maximum(n * Q_BLOCKS - 1, 0), 0))
    nxt = lambda w: pl.BlockSpec((1, BLOCK, w), lambda bi, n: (bi, jnp.minimum((n + 1) * Q_BLOCKS, nb - 1), 0))
    cx = pl.BlockSpec((1, l, KV_W), lambda bi, n: (bi, 0, 0))
    return pl.pallas_call(
        _attn_kernel,
        grid=(b, s // tq),
        in_specs=[
            pl.BlockSpec(memory_space=pltpu.SMEM),
            cur(ATT_W),
            prv(KV_W), cur(KV_W), nxt(KV_W), prv(KV_W), cur(KV_W), nxt(KV_W),
            prv(KV_W), cur(KV_W), nxt(KV_W), prv(KV_W), cur(KV_W), nxt(KV_W),
            cx, cx, cx, cx,
            cur(GM_W), cur(GM_W),
            pl.BlockSpec((4, BLOCK, 2 * BLOCK), lambda bi, n: (0, 0, 0)),
            pl.BlockSpec((BLOCK, 8), lambda bi, n: (0, 0)),
            pl.BlockSpec((D, D), lambda bi, n: (0, 0)),
            cur(D),
            pl.BlockSpec((1, 1, 6, D), lambda bi, n: (0, bi, 0, 0)),
        ],
        out_specs=cur(D),
        out_shape=jax.ShapeDtypeStruct((b, s, D), F32),
        compiler_params=_cparams(("arbitrary", "arbitrary"), VMEM_LIMIT),
        name="attn_gmlp_out",
    )(sink, q, k, k, k, ksw, ksw, ksw, v, v, v, vsw, vsw, vsw, kx, ksx, vx, vsx,
      u, vg, wcat_bf, bs_t, wout_bf, x, mod)


def _ffn_kernel(x_ref, mod_ref, wg_ref, wu_ref, wd_ref, o_ref, act):
    mod = mod_ref[0, 0]
    xf = x_ref[...]
    h = _modulate(xf, mod[3:4], mod[4:5]).astype(BF16)
    for c in range(wg_ref.shape[1] // W_CHUNK):
        cs = slice(c * W_CHUNK, (c + 1) * W_CHUNK)
        g = jnp.dot(h, wg_ref[:, cs], preferred_element_type=F32)
        up = jnp.dot(h, wu_ref[:, cs], preferred_element_type=F32)
        act[:, cs] = (g * _sigmoid(g) * up).astype(BF16)
    o_ref[...] = xf + mod[5:6] * jnp.dot(act[...], wd_ref[...], preferred_element_type=F32)


def _dense_ffn(x2d, mod, wg, wu, wd, seq):
    n = x2d.shape[0]
    f = wg.shape[1]
    tm = TM_FFN
    per_b = seq // tm
    resident = lambda shp: pl.BlockSpec(shp, lambda i: (0, 0), pipeline_mode=pl.Buffered(1))
    return pl.pallas_call(
        _ffn_kernel,
        grid=(n // tm,),
        in_specs=[
            pl.BlockSpec((tm, D), lambda i: (i, 0)),
            pl.BlockSpec((1, 1, 6, D), lambda i: (0, i // per_b, 0, 0)),
            resident((D, f)), resident((D, f)), resident((f, D)),
        ],
        out_specs=pl.BlockSpec((tm, D), lambda i: (i, 0)),
        out_shape=jax.ShapeDtypeStruct((n, D), F32),
        scratch_shapes=[pltpu.VMEM((tm, f), BF16)],
        compiler_params=_cparams(("arbitrary",), VMEM_LIMIT),
        name="dense_ffn",
    )(x2d, mod, wg, wu, wd)


def _pool_route_kernel(x_ref, xp_ref, xn_ref, mod_ref, band_ref, pw_ref, psc_ref, wr_hi_ref,
                       tri_ref, x3_ref, h2_ref, route_ref, cnt_ref, hext, carry):
    bi = pl.program_id(0)
    i = pl.program_id(1)
    ni = pl.num_programs(1)
    tm = x_ref.shape[1]
    seq = tm * ni
    mod = mod_ref[0, 0]

    @pl.when((bi == 0) & (i == 0))
    def _():
        carry[...] = jnp.zeros_like(carry)

    xf = x_ref[0]
    hp = _modulate(xp_ref[0], mod[0:1], mod[1:2])
    hn = _modulate(xn_ref[0], mod[0:1], mod[1:2])
    hext[0:POOL_HALO] = jnp.where(i > 0, hp, 0.0).astype(BF16)
    h_main = _modulate(xf, mod[0:1], mod[1:2])
    hext[POOL_HALO:POOL_HALO + tm] = h_main.astype(BF16)
    hext[POOL_HALO + tm:] = jnp.where(i < ni - 1, hn, 0.0).astype(BF16)

    t_local = lax.broadcasted_iota(jnp.int32, (BLOCK, 1), 0)
    ys = []
    for gi, w in enumerate(POOL_SIZES):
        lo_off = -(w // 2)
        hi_off = w - 1 - w // 2
        cols = slice(gi * POOL_GD, (gi + 1) * POOL_GD)
        outs = []
        for sb in range(tm // BLOCK):
            r0 = sb * BLOCK
            win = jnp.dot(band_ref[gi], hext[r0:r0 + BLOCK + 2 * POOL_HALO, cols],
                          preferred_element_type=F32)
            t = i * tm + r0 + t_local
            cnt = (jnp.minimum(t + hi_off, seq - 1) - jnp.maximum(t + lo_off, 0) + 1).astype(F32)
            diff = win / cnt - h_main[r0:r0 + BLOCK, cols]
            outs.append(diff.astype(BF16))
        dg = jnp.concatenate(outs, axis=0)
        ys.append(jnp.dot(dg, pw_ref[gi], preferred_element_type=F32))
    y = jnp.concatenate(ys, axis=1) * psc_ref[...]
    x3 = xf + mod[2:3] * y
    x3_ref[0] = x3

    h2 = _modulate(x3, mod[3:4], mod[4:5])
    _rows_to_slabs(h2, h2_ref)
    logits = jnp.dot(h2.astype(BF16), wr_hi_ref[...], preferred_element_type=F32)
    lane = lax.broadcasted_iota(jnp.int32, (tm, LANES), 1)
    lane_f = lane.astype(F32)
    neg = -jnp.inf
    lg = jnp.where(lane < N_EXPERTS, logits, neg)
    m1 = jnp.max(lg, axis=-1, keepdims=True)
    i1 = jnp.min(jnp.where(lg == m1, lane_f, float(LANES)), axis=-1, keepdims=True)
    oh1 = lane_f == i1
    lg2 = jnp.where(oh1, neg, lg)
    m2 = jnp.max(lg2, axis=-1, keepdims=True)
    i2 = jnp.min(jnp.where(lg2 == m2, lane_f, float(LANES)), axis=-1, keepdims=True)
    oh2 = lane_f == i2
    e = jnp.exp(m2 - m1)
    w1 = 1.0 / (1.0 + e)
    w2 = e / (1.0 + e)
    oh = jnp.where(oh1 | oh2, 1.0, 0.0)
    before = jnp.dot(tri_ref[...], oh.astype(BF16), preferred_element_type=F32) + carry[...]
    r1 = jnp.sum(jnp.where(oh1, before, 0.0), axis=-1, keepdims=True)
    r2 = jnp.sum(jnp.where(oh2, before, 0.0), axis=-1, keepdims=True)
    carry[...] = carry[...] + jnp.sum(oh, axis=0, keepdims=True)
    cnt_ref[...] = carry[...]
    info = jnp.where(lane == 0, i1, jnp.where(lane == 1, i2, jnp.where(lane == 2, w1, jnp.where(
        lane == 3, w2, jnp.where(lane == 4, r1, jnp.where(lane == 5, r2, 0.0))))))
    route_ref[...] = info.T[0:8, :]


def _pool_route(x, mod, band, pw_bf, pool_scale, wr_hi, tri):
    b, s, _ = x.shape
    tm = TM_POOL
    ni = s // tm
    hb = tm // POOL_HALO
    row = pl.BlockSpec((1, tm, D), lambda bi, i: (bi, i, 0))
    const2 = lambda shp: pl.BlockSpec(shp, lambda bi, i: (0,) * len(shp))
    return pl.pallas_call(
        _pool_route_kernel,
        grid=(b, ni),
        in_specs=[
            row,
            pl.BlockSpec((1, POOL_HALO, D), lambda bi, i: (bi, jnp.maximum(i * hb - 1, 0), 0)),
            pl.BlockSpec((1, POOL_HALO, D), lambda bi, i: (bi, jnp.minimum((i + 1) * hb, s // POOL_HALO - 1), 0)),
            pl.BlockSpec((1, 1, 6, D), lambda bi, i: (1, bi, 0, 0)),
            const2(band.shape), const2(pw_bf.shape), const2((1, D)),
            const2(wr_hi.shape), const2(tri.shape),
        ],
        out_specs=[row,
                   pl.BlockSpec((tm * SLAB, LANES), lambda bi, i: (bi * ni + i, 0)),
                   pl.BlockSpec((8, tm), lambda bi, i: (0, bi * ni + i)),
                   pl.BlockSpec((1, LANES), lambda bi, i: (0, 0))],
        out_shape=[jax.ShapeDtypeStruct((b, s, D), F32), jax.ShapeDtypeStruct((b * s * SLAB, LANES), F32),
                   jax.ShapeDtypeStruct((8, b * s), F32), jax.ShapeDtypeStruct((1, LANES), F32)],
        scratch_shapes=[pltpu.VMEM((tm + 2 * POOL_HALO, D), BF16), pltpu.VMEM((1, LANES), F32)],
        compiler_params=_cparams(("arbitrary", "arbitrary"), VMEM_LIMIT),
        name="pool_route",
    )(x, x, x, mod, band, pw_bf, pool_scale.reshape(1, D), wr_hi, tri)


def _slotmap_kernel(pos_ref, lo_ref, hi_ref, o_ref):
    n_pairs = pos_ref.shape[0]
    spare_mask = 2 * TM_MOE - 1
    for e in range(lo_ref.shape[0]):
        def fill(p, c):
            o_ref[p] = n_pairs + (p & spare_mask)
            return c
        lax.fori_loop(lo_ref[e], hi_ref[e], fill, 0)

    def place(f, c):
        o_ref[pos_ref[f]] = f
        return c
    lax.fori_loop(0, n_pairs, place, 0, unroll=32)


def _slot_map(pos_flat, lo, hi, n_slots):
    smem = pl.BlockSpec(memory_space=pltpu.SMEM)
    return pl.pallas_call(
        _slotmap_kernel,
        in_specs=[smem, smem, smem],
        out_specs=smem,
        out_shape=jax.ShapeDtypeStruct((n_slots,), jnp.int32),
        name="moe_slot_map",
    )(pos_flat, lo, hi)


def _moe_kernel(te_ref, nused_ref, half_ref, fnext_ref, fprev_ref, f0_ref, h_hbm, wg_hbm, wu_hbm, wd_hbm,
                y_hbm, wg_res, wu_res, wd_res, stg_in, stg_out, xbuf, xb, act, acc, stage, gsem, ssem, wsem):
    i = pl.program_id(0)
    nt = pl.num_programs(0)
    used_tiles = nused_ref[0]
    tm = xb.shape[0]
    f_dim = wg_res.shape[1]
    tok_mask = h_hbm.shape[0] // SLAB - 1
    tile_rows = tm * SLAB
    out_rows = tm * PACK

    def slab(ix, width=SLAB):
        return pl.ds(pl.multiple_of(ix * width, width), width)

    def gather_row(fref, r, slot, zero=0):
        tok = fref[0, 0, r + zero] & tok_mask
        return pltpu.make_async_copy(h_hbm.at[slab(tok)], xbuf.at[slot, slab(r)], gsem.at[slot])

    def scatter_row(fref, r, slot, zero=0):
        return pltpu.make_async_copy(stage.at[slot, slab(r, PACK)], y_hbm.at[slab(fref[0, 0, r + zero], PACK)],
                                     ssem.at[slot])

    def gather_all(slot):
        return pltpu.make_async_copy(h_hbm.at[pl.ds(0, tile_rows)], xbuf.at[slot], gsem.at[slot])

    def scatter_all(slot):
        return pltpu.make_async_copy(stage.at[slot], y_hbm.at[pl.ds(0, out_rows)], ssem.at[slot])

    cur = i % 2
    used = i < used_tiles

    @pl.when(i == 0)
    def _():
        stage[...] = jnp.zeros_like(stage)
        spare = y_hbm.shape[0] - 2 * out_rows
        fills = [pltpu.make_async_copy(stage.at[sl], y_hbm.at[pl.ds(spare + sl * out_rows, out_rows)],
                                       ssem.at[sl])
                 for sl in range(2)]
        for cp in fills:
            cp.start()
        for cp in fills:
            cp.wait()

    @pl.when((i == 0) & used)
    def _():
        def prime(r, c):
            gather_row(f0_ref, r, 0).start()
            return c
        lax.fori_loop(0, tm, prime, 0)

    expert = te_ref[i]
    new_expert = used & ((i == 0) | (expert != te_ref[jnp.maximum(i - 1, 0)]))

    has_next = i + 1 < used_tiles
    has_prev = (i >= 1) & (i - 1 < used_tiles)
    steady = (i >= 1) & has_next
    half_tile = half_ref[i] == 1

    @pl.when(used)
    def _():
        gather_all(cur).wait()

    @pl.when((i >= 2) & (i - 2 < used_tiles))
    def _():
        scatter_all(cur).wait()

    half = f_dim // MOE_SPLIT
    n_piece = half // W_CHUNK
    n_groups = MOE_SPLIT * (n_piece + 1)

    def tick(v):
        bits = jnp.max(lax.bitcast_convert_type(v[0:SLAB, 0:LANES], jnp.int32))
        return lax.shift_right_logical(lax.shift_right_logical(bits, 16), 16)

    def expert_ffn(issue_group, rows=tm):
        for cix in range(SLAB):
            xb[0:rows, cix * LANES:(cix + 1) * LANES] = (
                xbuf.at[cur][pl.ds(cix, rows, stride=SLAB), :].astype(BF16))
        xv = xb[0:rows]
        out = None
        issue_group(0, 0)
        k = 1
        for hf in range(MOE_SPLIT):
            for c in range(n_piece):
                cs = slice(hf * half + c * W_CHUNK, hf * half + (c + 1) * W_CHUNK)
                g = jnp.dot(xv, wg_res[:, cs], preferred_element_type=F32)
                up = jnp.dot(xv, wu_res[:, cs], preferred_element_type=F32)
                act[0:rows, c * W_CHUNK:(c + 1) * W_CHUNK] = (g * _sigmoid(g) * up).astype(BF16)
                issue_group(k, tick(g))
                k += 1
            part = jnp.dot(act[0:rows], wd_res[hf * half:(hf + 1) * half, :], preferred_element_type=F32)
            if hf < MOE_SPLIT - 1:
                acc[0:rows] = part if out is None else acc[0:rows] + part
                out = acc
                issue_group(k, tick(part))
                k += 1
            else:
                _pack_rows(part if out is None else acc[0:rows] + part, stage.at[cur])

    def chunk_plan():
        n_in = D // W_ROWS_IN
        plan = [(w, res, stg_in, W_RING_IN, 0, W_ROWS_IN, c, wi * n_in + c)
                for wi, (w, res) in enumerate(((wg_hbm, wg_res), (wu_hbm, wu_res))) for c in range(n_in)]
        return plan + [(wd_hbm, wd_res, stg_out, W_RING_OUT, W_RING_IN, W_ROWS_OUT, c, c)
                       for c in range(f_dim // W_ROWS_OUT)]

    def chunk_copy(entry, ex):
        w_hbm, _, stg, ring, sem0, rows, c, j = entry
        return pltpu.make_async_copy(w_hbm.at[ex, pl.ds(c * rows, rows), :], stg.at[j % ring],
                                     wsem.at[sem0 + j % ring])

    def start_first_chunks(ex):
        for entry in chunk_plan():
            if entry[7] < entry[3]:
                chunk_copy(entry, ex).start()

    @pl.when(new_expert)
    def _():
        plan = chunk_plan()

        @pl.when(i == 0)
        def _():
            start_first_chunks(expert)

        for k, entry in enumerate(plan):
            _, res, stg, ring, _, rows, c, j = entry
            chunk_copy(entry, expert).wait()
            res[c * rows:(c + 1) * rows, :] = stg[j % ring].astype(BF16)
            later = [e for e in plan[k + 1:] if e[2] is stg and e[7] == j + ring]
            if later:
                chunk_copy(later[0], expert).start()

    next_expert = te_ref[jnp.minimum(i + 1, nt - 1)]

    @pl.when(has_next & (next_expert != expert))
    def _():
        start_first_chunks(next_expert)

    def issue_group(k, zero):
        for r in range(k * tm // n_groups, (k + 1) * tm // n_groups):
            gather_row(fnext_ref, r, 1 - cur, zero).start()
            scatter_row(fprev_ref, r, 1 - cur, zero).start()

    @pl.when(steady & jnp.logical_not(half_tile))
    def _():
        expert_ffn(issue_group)

    @pl.when(steady & half_tile)
    def _():
        expert_ffn(issue_group, tm // 2)

    @pl.when(jnp.logical_not(steady))
    def _():
        @pl.when(used)
        def _():
            expert_ffn(lambda k, zero: None)

        @pl.when(has_next)
        def _():
            def issue(r, c):
                gather_row(fnext_ref, r, 1 - cur).start()
                return c
            lax.fori_loop(0, tm, issue, 0)

        @pl.when(has_prev)
        def _():
            def issue(r, c):
                scatter_row(fprev_ref, r, 1 - cur).start()
                return c
            lax.fori_loop(0, tm, issue, 0)

    @pl.when((i == nt - 1) & (nt - 2 < used_tiles))
    def _():
        scatter_all(1 - cur).wait()


def _moe_experts(h_slabs, fmap, tile_expert, n_used, tile_half, wg, wu, wd, n_tiles, y_rows):
    f = wg.shape[2]
    tm = TM_MOE
    fblk = lambda imap: pl.BlockSpec((1, 1, tm), imap, memory_space=pltpu.SMEM)
    hbm = pl.BlockSpec(memory_space=pl.ANY)
    return pl.pallas_call(
        _moe_kernel,
        grid_spec=pltpu.PrefetchScalarGridSpec(
            num_scalar_prefetch=3,
            grid=(n_tiles,),
            in_specs=[
                fblk(lambda i, te, nu, hf: (jnp.minimum(i + 1, n_tiles - 1), 0, 0)),
                fblk(lambda i, te, nu, hf: (jnp.maximum(i - 1, 0), 0, 0)),
                fblk(lambda i, te, nu, hf: (0, 0, 0)),
                hbm, hbm, hbm, hbm,
            ],
            out_specs=hbm,
            scratch_shapes=[pltpu.VMEM((D, f), BF16), pltpu.VMEM((D, f), BF16), pltpu.VMEM((f, D), BF16),
                            pltpu.VMEM((W_RING_IN, W_ROWS_IN, f), F32), pltpu.VMEM((W_RING_OUT, W_ROWS_OUT, D), F32),
                            pltpu.VMEM((2, tm * SLAB, LANES), F32), pltpu.VMEM((tm, D), BF16),
                            pltpu.VMEM((tm, f // MOE_SPLIT), BF16), pltpu.VMEM((tm, D), F32),
                            pltpu.VMEM((2, tm * PACK, LANES), jnp.uint32),
                            pltpu.SemaphoreType.DMA((2,)), pltpu.SemaphoreType.DMA((2,)),
                            pltpu.SemaphoreType.DMA((W_RING_IN + W_RING_OUT,))],
        ),
        out_shape=jax.ShapeDtypeStruct((y_rows * PACK, LANES), jnp.uint32),
        compiler_params=_cparams(("arbitrary",), MOE_VMEM_LIMIT),
        name="moe_experts",
    )(tile_expert, n_used, tile_half, fmap, fmap, fmap, h_slabs, wg, wu, wd)


def _combine_kernel(y1_ref, y2_ref, x_ref, w_ref, mod_ref, gain_ref, o_ref):
    w = w_ref[...]
    rows = x_ref.shape[0]
    moe = w[:, 0:1] * _unpack_rows(y1_ref, rows) + w[:, 1:2] * _unpack_rows(y2_ref, rows)
    mod = mod_ref[0, 0]
    x4 = x_ref[...] + mod[5:6] * moe
    ms = jnp.mean(x4 * x4, axis=-1, keepdims=True)
    o_ref[...] = x4 * lax.rsqrt(ms + EPS) * gain_ref[...]


def _combine(y, x3_2d, wts, mod, final_gain, seq):
    n = x3_2d.shape[0]
    tc = TC_COMB
    nt = n // tc
    per_b = seq // tc
    return pl.pallas_call(
        _combine_kernel,
        grid=(nt,),
        in_specs=[
            pl.BlockSpec((tc * PACK, LANES), lambda i: (i, 0)),
            pl.BlockSpec((tc * PACK, LANES), lambda i: (i + nt, 0)),
            pl.BlockSpec((tc, D), lambda i: (i, 0)),
            pl.BlockSpec((tc, 2), lambda i: (i, 0)),
            pl.BlockSpec((1, 1, 6, D), lambda i: (1, i // per_b, 0, 0)),
            pl.BlockSpec((1, D), lambda i: (0, 0)),
        ],
        out_specs=pl.BlockSpec((tc, D), lambda i: (i, 0)),
        out_shape=jax.ShapeDtypeStruct((n, D), F32),
        compiler_params=_cparams(("arbitrary",)),
        name="moe_combine",
    )(y, y, x3_2d, wts, mod, final_gain.reshape(1, D))


def _rope_tables(seq):
    rows = seq // GRID_W
    row_pos = jnp.repeat(jnp.arange(rows, dtype=F32), GRID_W)
    col_pos = jnp.tile(jnp.arange(GRID_W, dtype=F32), rows)
    axis_dim = HEAD_DIM // 2
    inv_freq = ROPE_BASE ** (-jnp.arange(0, axis_dim, 2, dtype=F32) / axis_dim)
    ar = row_pos[:, None] * inv_freq
    ac = col_pos[:, None] * inv_freq
    cos64 = jnp.concatenate([jnp.cos(ar), jnp.cos(ar), jnp.cos(ac), jnp.cos(ac)], axis=1)
    sin64 = jnp.concatenate([-jnp.sin(ar), jnp.sin(ar), -jnp.sin(ac), jnp.sin(ac)], axis=1)
    return jnp.tile(cos64, (1, 2)), jnp.tile(sin64, (1, 2))


def _band_matrices():
    r = np.arange(BLOCK)[:, None]
    c = np.arange(BLOCK + 2 * POOL_HALO)[None, :] - POOL_HALO
    mats = []
    for w in POOL_SIZES:
        lo = -(w // 2)
        hi = w - 1 - w // 2
        mats.append(((c >= r + lo) & (c <= r + hi)).astype(np.float32))
    return jnp.asarray(np.stack(mats), dtype=BF16)


def kernel(x, c, ctx, c_ctx, w_ada, b_ada, w_in, attn_sink, gm_gain, gm_w_s, gm_b_s, w_out,
           ffn_w_gate, ffn_w_up, ffn_w_down, pool_w, pool_scale, router_w,
           moe_w_gate, moe_w_up, moe_w_down, final_gain):
    b, s, _ = x.shape
    n = b * s
    assert w_ada.shape[0] == 2 and w_in.shape[0] == 1 and pool_w.shape[0] == 1
    assert s % TM_IN == 0 and s % TM_POOL == 0 and s % TM_FFN == 0 and b <= 4
    assert n & (n - 1) == 0

    cvec = jnp.concatenate([c, c_ctx[None, :], jnp.zeros((8 - b - 1, D), F32)], axis=0)
    mod = _ada_mod(cvec, w_ada, b_ada)

    cos_t, sin_t = _rope_tables(s)
    w_in_bf = w_in[0].astype(BF16)
    q, k, ksw, v, vsw, u, vg = _in_proj(x, mod, w_in_bf, gm_gain[0], cos_t, sin_t)
    kx, ksx, vx, vsx = _ctx_kv(ctx, mod, w_in_bf)
    wcat = gm_w_s[0].reshape(4, 2, BLOCK, BLOCK).transpose(0, 2, 1, 3).reshape(4, BLOCK, 2 * BLOCK).astype(BF16)
    x1 = _attn_mixer(x, mod, attn_sink[0], q, k, ksw, v, vsw, kx, ksx, vx, vsx, u, vg,
                     wcat, gm_b_s[0].T, w_out[0].astype(BF16))
    x2 = _dense_ffn(x1.reshape(n, D), mod, ffn_w_gate[0].astype(BF16), ffn_w_up[0].astype(BF16),
                    ffn_w_down[0].astype(BF16), s)

    wr = jnp.pad(router_w[0], ((0, 0), (0, LANES - N_EXPERTS)))
    wr_hi = wr.astype(BF16)
    tri = jnp.asarray(np.tril(np.ones((TM_POOL, TM_POOL), np.float32), -1), dtype=BF16)
    x3, h2, route, counts = _pool_route(x2.reshape(b, s, D), mod, _band_matrices(), pool_w[0].astype(BF16),
                                        pool_scale[0], wr_hi, tri)

    tm = TM_MOE
    n_tiles = (2 * n) // tm + N_EXPERTS
    cnt = counts[0, :N_EXPERTS].astype(jnp.int32)
    tiles_e = (cnt + tm - 1) // tm
    tile_end = jnp.cumsum(tiles_e)
    off = (tile_end - tiles_e) * tm
    n_used = tile_end[-1]
    tix = jnp.arange(n_tiles, dtype=jnp.int32)
    te = jnp.minimum(jnp.sum(tix[:, None] >= tile_end[None, :], axis=1), N_EXPERTS - 1).astype(jnp.int32)
    te_last = te[jnp.maximum(n_used - 1, 0)]
    tile_expert = jnp.where(tix < n_used, te, te_last)
    rows_left = cnt[te] - (tix - (tile_end - tiles_e)[te]) * tm
    tile_half = ((tix < n_used) & (rows_left <= tm // 2)).astype(jnp.int32)
    e1 = route[0].astype(jnp.int32)
    e2 = route[1].astype(jnp.int32)
    pos1 = off[e1] + route[4].astype(jnp.int32)
    pos2 = off[e2] + route[5].astype(jnp.int32)
    n_slots = n_tiles * tm
    pad_lo = jnp.concatenate([off + cnt, (n_used * tm).reshape(1)]).astype(jnp.int32)
    pad_hi = jnp.concatenate([off + tiles_e * tm, jnp.full((1,), n_slots, jnp.int32)]).astype(jnp.int32)
    fmap = _slot_map(jnp.concatenate([pos1, pos2]), pad_lo, pad_hi, n_slots)
    n_used_arr = n_used.reshape(1).astype(jnp.int32)

    y = _moe_experts(h2, fmap.reshape(n_tiles, 1, tm), tile_expert, n_used_arr, tile_half,
                     moe_w_gate[0], moe_w_up[0], moe_w_down[0], n_tiles, 2 * n + 2 * tm)
    out = _combine(y, x3.reshape(n, D), route[2:4].T, mod, final_gain, s)
    return out.reshape(b, s, D)
```

```python
import functools

import numpy as np
import jax
import jax.numpy as jnp
from jax import lax
from jax.experimental import pallas as pl
from jax.experimental.pallas import tpu as pltpu

F32 = jnp.float32
BF16 = jnp.bfloat16

D = 1024
GRID_W = 64
EPS = 1e-6
NEG_INF = -1e30
HEAD_DIM = 64
N_Q_HEADS = 8
BLOCK = 128
ATT_W = 512
KV_W = 128
GM_W = 512
IN_W = 1792
POOL_SIZES = (2, 4, 8, 16)
POOL_GD = 256
POOL_HALO = 16
N_EXPERTS = 8
ROPE_BASE = 10000.0
LANES = 128
SLAB = D // LANES
PACK = SLAB // 2
SQRT_2_OVER_PI = 0.7978845608028654

TM_IN = 1024
TM_FFN = 1024
TM_POOL = 1024
TM_MOE = 512
MOE_SPLIT = 2
W_CHUNK = 256
W_ROWS_IN = 128
W_ROWS_OUT = 512
W_RING_IN = 12
W_RING_OUT = 2
MOE_VMEM_LIMIT = 60 * 1024 * 1024
TC_COMB = 1024
Q_BLOCKS = 8
VMEM_LIMIT = 56 * 1024 * 1024


def _cparams(sem, vmem=None):
    return pltpu.CompilerParams(dimension_semantics=sem, vmem_limit_bytes=vmem)


def _modulate(xf, shift, scale):
    ms = jnp.mean(xf * xf, axis=-1, keepdims=True)
    return xf * lax.rsqrt(ms + EPS) * (1.0 + scale) + shift


def _sigmoid(z):
    return 1.0 / (1.0 + jnp.exp(-z))


def _rows_to_slabs(val, slab_ref):
    rows = val.shape[0]
    for cix in range(SLAB):
        slab_ref[pl.ds(cix, rows, stride=SLAB), :] = val[:, cix * LANES:(cix + 1) * LANES]


def _slabs_to_rows(slab_ref, rows):
    return jnp.concatenate([slab_ref[pl.ds(cix, rows, stride=SLAB), :] for cix in range(SLAB)], axis=1)


def _pack_rows(val, pack_ref):
    rows = val.shape[0]
    bits = lambda v: lax.bitcast_convert_type(v.astype(BF16).astype(F32), jnp.uint32)
    for cix in range(PACK):
        hi = bits(val[:, cix * LANES:(cix + 1) * LANES])
        lo = bits(val[:, D // 2 + cix * LANES:D // 2 + (cix + 1) * LANES])
        pack_ref[pl.ds(cix, rows, stride=PACK), :] = hi | (lo >> 16)


def _unpack_rows(pack_ref, rows):
    words = [pack_ref[pl.ds(cix, rows, stride=PACK), :] for cix in range(PACK)]
    his = [lax.bitcast_convert_type(w & jnp.uint32(0xFFFF0000), F32) for w in words]
    los = [lax.bitcast_convert_type(w << 16, F32) for w in words]
    return jnp.concatenate(his + los, axis=1)


def _ada_kernel(c_ref, w_ref, b_ref, o_ref):
    c = c_ref[...]
    s = c * _sigmoid(c)
    o_ref[0] = jnp.dot(s.astype(BF16), w_ref[0].astype(BF16), preferred_element_type=F32) + b_ref[0]


def _ada_mod(cvec, w_ada, b_ada):
    depth, _, n6 = w_ada.shape
    tn = 1536
    out = pl.pallas_call(
        _ada_kernel,
        grid=(depth, n6 // tn),
        in_specs=[
            pl.BlockSpec((8, D), lambda l, j: (0, 0)),
            pl.BlockSpec((1, D, tn), lambda l, j: (l, 0, j)),
            pl.BlockSpec((1, 1, tn), lambda l, j: (l, 0, j)),
        ],
        out_specs=pl.BlockSpec((1, 8, tn), lambda l, j: (l, 0, j)),
        out_shape=jax.ShapeDtypeStruct((depth, 8, n6), F32),
        compiler_params=_cparams(("arbitrary", "arbitrary")),
        name="ada_mod",
    )(cvec, w_ada, b_ada.reshape(depth, 1, n6))
    return out.reshape(depth, 8, 6, D)


def _rope(t, cs, sn, first_half):
    fwd = pltpu.roll(t, LANES - 16, axis=1)
    bwd = pltpu.roll(t, 16, axis=1)
    return t * cs + jnp.where(first_half, fwd, bwd) * sn


def _inproj_kernel(x_ref, mod_ref, w_ref, gain_ref, cos_ref, sin_ref,
                   q_ref, k_ref, ksw_ref, v_ref, vsw_ref, u_ref, vg_ref):
    mod = mod_ref[0, 0]
    h = _modulate(x_ref[0], mod[0:1], mod[1:2]).astype(BF16)
    proj = jnp.dot(h, w_ref[...], preferred_element_type=F32)
    cs = cos_ref[...]
    sn = sin_ref[...]
    lane = lax.broadcasted_iota(jnp.int32, cs.shape, 1)
    first_half = (lane & 16) == 0
    for cix in range(ATT_W // LANES):
        t = proj[:, cix * LANES:(cix + 1) * LANES]
        q_ref[0, :, cix * LANES:(cix + 1) * LANES] = (
            _rope(t, cs, sn, first_half) * (HEAD_DIM ** -0.5)).astype(BF16)
    kr = _rope(proj[:, ATT_W:ATT_W + KV_W], cs, sn, first_half)
    k_ref[0] = kr.astype(BF16)
    ksw_ref[0] = pltpu.roll(kr, HEAD_DIM, axis=1).astype(BF16)
    vv = proj[:, ATT_W + KV_W:ATT_W + 2 * KV_W]
    v_ref[0] = vv.astype(BF16)
    vsw_ref[0] = pltpu.roll(vv, HEAD_DIM, axis=1).astype(BF16)
    z = proj[:, ATT_W + 2 * KV_W:]
    g = z * (0.5 * (1.0 + jnp.tanh(SQRT_2_OVER_PI * (z + 0.044715 * (z * z * z)))))
    u_ref[0] = g[:, :GM_W].astype(BF16)
    vg = g[:, GM_W:]
    ms = jnp.mean(vg * vg, axis=-1, keepdims=True)
    vg_ref[0] = (vg * lax.rsqrt(ms + EPS) * gain_ref[...]).astype(BF16)


def _in_proj(x, mod, w_in_bf, gm_gain, cos_t, sin_t):
    b, s, _ = x.shape
    tm = TM_IN
    row = lambda w: pl.BlockSpec((1, tm, w), lambda bi, i: (bi, i, 0))
    outs = pl.pallas_call(
        _inproj_kernel,
        grid=(b, s // tm),
        in_specs=[
            row(D),
            pl.BlockSpec((1, 1, 6, D), lambda bi, i: (0, bi, 0, 0)),
            pl.BlockSpec((D, IN_W), lambda bi, i: (0, 0)),
            pl.BlockSpec((1, GM_W), lambda bi, i: (0, 0)),
            pl.BlockSpec((tm, LANES), lambda bi, i: (i, 0)),
            pl.BlockSpec((tm, LANES), lambda bi, i: (i, 0)),
        ],
        out_specs=[row(ATT_W), row(KV_W), row(KV_W), row(KV_W), row(KV_W), row(GM_W), row(GM_W)],
        out_shape=[jax.ShapeDtypeStruct((b, s, w), BF16)
                   for w in (ATT_W, KV_W, KV_W, KV_W, KV_W, GM_W, GM_W)],
        compiler_params=_cparams(("arbitrary", "arbitrary"), VMEM_LIMIT),
        name="in_proj",
    )(x, mod, w_in_bf, gm_gain.reshape(1, GM_W), cos_t, sin_t)
    return outs


def _ctx_kernel(c_ref, mod_ref, w_ref, k_ref, ksw_ref, v_ref, vsw_ref):
    mod = mod_ref[0, 0]
    h = _modulate(c_ref[0], mod[0:1], mod[1:2]).astype(BF16)
    kv = jnp.dot(h, w_ref[...], preferred_element_type=F32)
    kk = kv[:, :KV_W]
    vv = kv[:, KV_W:]
    k_ref[0] = kk.astype(BF16)
    ksw_ref[0] = pltpu.roll(kk, HEAD_DIM, axis=1).astype(BF16)
    v_ref[0] = vv.astype(BF16)
    vsw_ref[0] = pltpu.roll(vv, HEAD_DIM, axis=1).astype(BF16)


def _ctx_kv(ctx, mod, w_in_bf):
    b, l, _ = ctx.shape
    spec = pl.BlockSpec((1, l, KV_W), lambda bi: (bi, 0, 0))
    return pl.pallas_call(
        _ctx_kernel,
        grid=(b,),
        in_specs=[
            pl.BlockSpec((1, l, D), lambda bi: (bi, 0, 0)),
            pl.BlockSpec((1, 1, 6, D), lambda bi: (0, b, 0, 0)),
            pl.BlockSpec((D, 2 * KV_W), lambda bi: (0, ATT_W // (2 * KV_W))),
        ],
        out_specs=[spec] * 4,
        out_shape=[jax.ShapeDtypeStruct((b, l, KV_W), BF16)] * 4,
        compiler_params=_cparams(("arbitrary",)),
        name="ctx_kv",
    )(ctx, mod, w_in_bf)


def _attn_kernel(sink_ref, q_ref, kp_ref, kc_ref, kn_ref, ksp_ref, ksc_ref, ksn_ref,
                 vp_ref, vc_ref, vn_ref, vsp_ref, vsc_ref, vsn_ref,
                 kx_ref, ksx_ref, vx_ref, vsx_ref,
                 u_ref, vg_ref, wcat_ref, bs_ref, wout_ref, x_ref, mod_ref, o_ref):
    n = pl.program_id(1)
    nblk = pl.num_programs(1) * Q_BLOCKS
    lane = lax.broadcasted_iota(jnp.int32, (1, LANES), 1)
    low = lane < HEAD_DIM
    zero = jnp.zeros((), BF16)

    def variants(a0, a1):
        return ((jnp.where(low, a0, zero), jnp.where(low, zero, a1)),
                (jnp.where(low, a1, zero), jnp.where(low, zero, a0)))

    cat = lambda refs: jnp.concatenate([r[0] for r in refs], axis=0)
    kb_var = variants(cat((kp_ref, kc_ref, kn_ref)), cat((ksp_ref, ksc_ref, ksn_ref)))
    vb_var = variants(cat((vp_ref, vc_ref, vn_ref)), cat((vsp_ref, vsc_ref, vsn_ref)))
    kx_var = variants(kx_ref[0], ksx_ref[0])
    vx_var = variants(vx_ref[0], vsx_ref[0])

    row = lax.broadcasted_iota(jnp.int32, (2 * BLOCK, BLOCK), 0) & (BLOCK - 1)
    col = lax.broadcasted_iota(jnp.int32, (2 * BLOCK, BLOCK), 1)
    top = lax.broadcasted_iota(jnp.int32, (2 * BLOCK, 1), 0) < BLOCK
    nt_dims = (((1,), (1,)), ((), ()))

    q = q_ref[0]
    att_blocks = [[None] * 4 for _ in range(Q_BLOCKS)]
    for kvh in range(2):
        qst = jnp.concatenate(
            [q[qb * BLOCK:(qb + 1) * BLOCK, pr * LANES:(pr + 1) * LANES]
             for qb in range(Q_BLOCKS) for pr in (2 * kvh, 2 * kvh + 1)], axis=0)
        accs = [None] * Q_BLOCKS
        for half in range(2):
            sk = jnp.where(top, sink_ref[4 * kvh + half], sink_ref[4 * kvh + 2 + half])
            s_ctx = lax.dot_general(qst, kx_var[kvh][half], nt_dims, preferred_element_type=F32)
            o_band, p_ctx, dens = [], [], []
            for qb in range(Q_BLOCKS):
                g = n * Q_BLOCKS + qb
                qrows = qst[qb * 2 * BLOCK:(qb + 1) * 2 * BLOCK]
                sb = lax.dot_general(qrows, kb_var[kvh][half][qb * BLOCK:(qb + 3) * BLOCK], nt_dims,
                                     preferred_element_type=F32)
                s0 = jnp.where((col >= row) & (g > 0), sb[:, :BLOCK], NEG_INF)
                s1 = sb[:, BLOCK:2 * BLOCK]
                s2 = jnp.where((col <= row) & (g < nblk - 1), sb[:, 2 * BLOCK:], NEG_INF)
                sc = s_ctx[qb * 2 * BLOCK:(qb + 1) * 2 * BLOCK]
                ctx_blocks = [sc[:, cb * BLOCK:(cb + 1) * BLOCK] for cb in range(sc.shape[1] // BLOCK)]
                m = functools.reduce(jnp.maximum, [s0, s1, s2] + ctx_blocks)
                m = jnp.maximum(jnp.max(m, axis=-1, keepdims=True), sk)
                p0, p1, p2, pc = (jnp.exp(t - m) for t in (s0, s1, s2, sc))
                psum = functools.reduce(
                    jnp.add, [p0, p1, p2] + [pc[:, cb * BLOCK:(cb + 1) * BLOCK] for cb in range(len(ctx_blocks))])
                den = jnp.sum(psum, axis=-1, keepdims=True) + jnp.exp(sk - m)
                pb = jnp.concatenate([p0, p1, p2], axis=1).astype(BF16)
                o_band.append(jnp.dot(pb, vb_var[kvh][half][qb * BLOCK:(qb + 3) * BLOCK],
                                      preferred_element_type=F32))
                p_ctx.append(pc.astype(BF16))
                dens.append(den)
            o_ctx = jnp.dot(jnp.concatenate(p_ctx, axis=0), vx_var[kvh][half], preferred_element_type=F32)
            for qb in range(Q_BLOCKS):
                o = (o_band[qb] + o_ctx[qb * 2 * BLOCK:(qb + 1) * 2 * BLOCK]) / dens[qb]
                accs[qb] = o if accs[qb] is None else accs[qb] + o
        for qb in range(Q_BLOCKS):
            att_blocks[qb][2 * kvh] = accs[qb][:BLOCK]
            att_blocks[qb][2 * kvh + 1] = accs[qb][BLOCK:]

    u = u_ref[0]
    vg = vg_ref[0]
    bs = bs_ref[...]
    gm_blocks = [[None] * 4 for _ in range(Q_BLOCKS)]
    for j in range(GM_W // LANES):
        chunks = [vg[c * BLOCK:(c + 1) * BLOCK, j * LANES:(j + 1) * LANES] for c in range(Q_BLOCKS)]
        rhs = jnp.concatenate(
            [jnp.concatenate([jnp.where(low, v, zero) for v in chunks], axis=1),
             jnp.concatenate([jnp.where(low, zero, v) for v in chunks], axis=1)], axis=0)
        mixed = jnp.dot(wcat_ref[j], rhs, preferred_element_type=F32)
        bias = jnp.where(low, bs[:, 2 * j:2 * j + 1], bs[:, 2 * j + 1:2 * j + 2])
        for c in range(Q_BLOCKS):
            gm_blocks[c][j] = (u[c * BLOCK:(c + 1) * BLOCK, j * LANES:(j + 1) * LANES].astype(F32)
                               * (mixed[:, c * LANES:(c + 1) * LANES] + bias))

    mix = jnp.concatenate([jnp.concatenate(att_blocks[c] + gm_blocks[c], axis=1) for c in range(Q_BLOCKS)],
                          axis=0).astype(BF16)
    y = jnp.dot(mix, wout_ref[...], preferred_element_type=F32)
    mod = mod_ref[0, 0]
    o_ref[0] = x_ref[0] + mod[2:3] * y


def _attn_mixer(x, mod, sink, q, k, ksw, v, vsw, kx, ksx, vx, vsx, u, vg, wcat_bf, bs_t, wout_bf):
    b, s, _ = x.shape
    tq = Q_BLOCKS * BLOCK
    nb = s // BLOCK
    l = kx.shape[1]
    cur = lambda w: pl.BlockSpec((1, tq, w), lambda bi, n: (bi, n, 0))
    prv = lambda w: pl.BlockSpec((1, BLOCK, w), lambda bi, n: (bi, jnp.maximum(n * Q_BLOCKS - 1, 0), 0))
    nxt = lambda w: pl.BlockSpec((1, BLOCK, w), lambda bi, n: (bi, jnp.minimum((n + 1) * Q_BLOCKS, nb - 1), 0))
    cx = pl.BlockSpec((1, l, KV_W), lambda bi, n: (bi, 0, 0))
    return pl.pallas_call(
        _attn_kernel,
        grid=(b, s // tq),
        in_specs=[
            pl.BlockSpec(memory_space=pltpu.SMEM),
            cur(ATT_W),
            prv(KV_W), cur(KV_W), nxt(KV_W), prv(KV_W), cur(KV_W), nxt(KV_W),
            prv(KV_W), cur(KV_W), nxt(KV_W), prv(KV_W), cur(KV_W), nxt(KV_W),
            cx, cx, cx, cx,
            cur(GM_W), cur(GM_W),
            pl.BlockSpec((4, BLOCK, 2 * BLOCK), lambda bi, n: (0, 0, 0)),
            pl.BlockSpec((BLOCK, 8), lambda bi, n: (0, 0)),
            pl.BlockSpec((D, D), lambda bi, n: (0, 0)),
            cur(D),
            pl.BlockSpec((1, 1, 6, D), lambda bi, n: (0, bi, 0, 0)),
        ],
        out_specs=cur(D),
        out_shape=jax.ShapeDtypeStruct((b, s, D), F32),
        compiler_params=_cparams(("arbitrary", "arbitrary"), VMEM_LIMIT),
        name="attn_gmlp_out",
    )(sink, q, k, k, k, ksw, ksw, ksw, v, v, v, vsw, vsw, vsw, kx, ksx, vx, vsx,
      u, vg, wcat_bf, bs_t, wout_bf, x, mod)


def _ffn_kernel(x_ref, mod_ref, wg_ref, wu_ref, wd_ref, o_ref, act):
    mod = mod_ref[0, 0]
    xf = x_ref[...]
    h = _modulate(xf, mod[3:4], mod[4:5]).astype(BF16)
    for c in range(wg_ref.shape[1] // W_CHUNK):
        cs = slice(c * W_CHUNK, (c + 1) * W_CHUNK)
        g = jnp.dot(h, wg_ref[:, cs], preferred_element_type=F32)
        up = jnp.dot(h, wu_ref[:, cs], preferred_element_type=F32)
        act[:, cs] = (g * _sigmoid(g) * up).astype(BF16)
    o_ref[...] = xf + mod[5:6] * jnp.dot(act[...], wd_ref[...], preferred_element_type=F32)


def _dense_ffn(x2d, mod, wg, wu, wd, seq):
    n = x2d.shape[0]
    f = wg.shape[1]
    tm = TM_FFN
    per_b = seq // tm
    resident = lambda shp: pl.BlockSpec(shp, lambda i: (0, 0), pipeline_mode=pl.Buffered(1))
    return pl.pallas_call(
        _ffn_kernel,
        grid=(n // tm,),
        in_specs=[
            pl.BlockSpec((tm, D), lambda i: (i, 0)),
            pl.BlockSpec((1, 1, 6, D), lambda i: (0, i // per_b, 0, 0)),
            resident((D, f)), resident((D, f)), resident((f, D)),
        ],
        out_specs=pl.BlockSpec((tm, D), lambda i: (i, 0)),
        out_shape=jax.ShapeDtypeStruct((n, D), F32),
        scratch_shapes=[pltpu.VMEM((tm, f), BF16)],
        compiler_params=_cparams(("arbitrary",), VMEM_LIMIT),
        name="dense_ffn",
    )(x2d, mod, wg, wu, wd)


def _pool_route_kernel(x_ref, xp_ref, xn_ref, mod_ref, band_ref, pw_ref, psc_ref, wr_hi_ref,
                       tri_ref, x3_ref, h2_ref, route_ref, cnt_ref, hext, carry):
    bi = pl.program_id(0)
    i = pl.program_id(1)
    ni = pl.num_programs(1)
    tm = x_ref.shape[1]
    seq = tm * ni
    mod = mod_ref[0, 0]

    @pl.when((bi == 0) & (i == 0))
    def _():
        carry[...] = jnp.zeros_like(carry)

    xf = x_ref[0]
    hp = _modulate(xp_ref[0], mod[0:1], mod[1:2])
    hn = _modulate(xn_ref[0], mod[0:1], mod[1:2])
    hext[0:POOL_HALO] = jnp.where(i > 0, hp, 0.0).astype(BF16)
    h_main = _modulate(xf, mod[0:1], mod[1:2])
    hext[POOL_HALO:POOL_HALO + tm] = h_main.astype(BF16)
    hext[POOL_HALO + tm:] = jnp.where(i < ni - 1, hn, 0.0).astype(BF16)

    t_local = lax.broadcasted_iota(jnp.int32, (BLOCK, 1), 0)
    ys = []
    for gi, w in enumerate(POOL_SIZES):
        lo_off = -(w // 2)
        hi_off = w - 1 - w // 2
        cols = slice(gi * POOL_GD, (gi + 1) * POOL_GD)
        outs = []
        for sb in range(tm // BLOCK):
            r0 = sb * BLOCK
            win = jnp.dot(band_ref[gi], hext[r0:r0 + BLOCK + 2 * POOL_HALO, cols],
                          preferred_element_type=F32)
            t = i * tm + r0 + t_local
            cnt = (jnp.minimum(t + hi_off, seq - 1) - jnp.maximum(t + lo_off, 0) + 1).astype(F32)
            diff = win / cnt - h_main[r0:r0 + BLOCK, cols]
            outs.append(diff.astype(BF16))
        dg = jnp.concatenate(outs, axis=0)
        ys.append(jnp.dot(dg, pw_ref[gi], preferred_element_type=F32))
    y = jnp.concatenate(ys, axis=1) * psc_ref[...]
    x3 = xf + mod[2:3] * y
    x3_ref[0] = x3

    h2 = _modulate(x3, mod[3:4], mod[4:5])
    _rows_to_slabs(h2, h2_ref)
    logits = jnp.dot(h2.astype(BF16), wr_hi_ref[...], preferred_element_type=F32)
    lane = lax.broadcasted_iota(jnp.int32, (tm, LANES), 1)
    lane_f = lane.astype(F32)
    neg = -jnp.inf
    lg = jnp.where(lane < N_EXPERTS, logits, neg)
    m1 = jnp.max(lg, axis=-1, keepdims=True)
    i1 = jnp.min(jnp.where(lg == m1, lane_f, float(LANES)), axis=-1, keepdims=True)
    oh1 = lane_f == i1
    lg2 = jnp.where(oh1, neg, lg)
    m2 = jnp.max(lg2, axis=-1, keepdims=True)
    i2 = jnp.min(jnp.where(lg2 == m2, lane_f, float(LANES)), axis=-1, keepdims=True)
    oh2 = lane_f == i2
    e = jnp.exp(m2 - m1)
    w1 = 1.0 / (1.0 + e)
    w2 = e / (1.0 + e)
    oh = jnp.where(oh1 | oh2, 1.0, 0.0)
    before = jnp.dot(tri_ref[...], oh.astype(BF16), preferred_element_type=F32) + carry[...]
    r1 = jnp.sum(jnp.where(oh1, before, 0.0), axis=-1, keepdims=True)
    r2 = jnp.sum(jnp.where(oh2, before, 0.0), axis=-1, keepdims=True)
    carry[...] = carry[...] + jnp.sum(oh, axis=0, keepdims=True)
    cnt_ref[...] = carry[...]
    info = jnp.where(lane == 0, i1, jnp.where(lane == 1, i2, jnp.where(lane == 2, w1, jnp.where(
        lane == 3, w2, jnp.where(lane == 4, r1, jnp.where(lane == 5, r2, 0.0))))))
    route_ref[...] = info.T[0:8, :]


def _pool_route(x, mod, band, pw_bf, pool_scale, wr_hi, tri):
    b, s, _ = x.shape
    tm = TM_POOL
    ni = s // tm
    hb = tm // POOL_HALO
    row = pl.BlockSpec((1, tm, D), lambda bi, i: (bi, i, 0))
    const2 = lambda shp: pl.BlockSpec(shp, lambda bi, i: (0,) * len(shp))
    return pl.pallas_call(
        _pool_route_kernel,
        grid=(b, ni),
        in_specs=[
            row,
            pl.BlockSpec((1, POOL_HALO, D), lambda bi, i: (bi, jnp.maximum(i * hb - 1, 0), 0)),
            pl.BlockSpec((1, POOL_HALO, D), lambda bi, i: (bi, jnp.minimum((i + 1) * hb, s // POOL_HALO - 1), 0)),
            pl.BlockSpec((1, 1, 6, D), lambda bi, i: (1, bi, 0, 0)),
            const2(band.shape), const2(pw_bf.shape), const2((1, D)),
            const2(wr_hi.shape), const2(tri.shape),
        ],
        out_specs=[row,
                   pl.BlockSpec((tm * SLAB, LANES), lambda bi, i: (bi * ni + i, 0)),
                   pl.BlockSpec((8, tm), lambda bi, i: (0, bi * ni + i)),
                   pl.BlockSpec((1, LANES), lambda bi, i: (0, 0))],
        out_shape=[jax.ShapeDtypeStruct((b, s, D), F32), jax.ShapeDtypeStruct((b * s * SLAB, LANES), F32),
                   jax.ShapeDtypeStruct((8, b * s), F32), jax.ShapeDtypeStruct((1, LANES), F32)],
        scratch_shapes=[pltpu.VMEM((tm + 2 * POOL_HALO, D), BF16), pltpu.VMEM((1, LANES), F32)],
        compiler_params=_cparams(("arbitrary", "arbitrary"), VMEM_LIMIT),
        name="pool_route",
    )(x, x, x, mod, band, pw_bf, pool_scale.reshape(1, D), wr_hi, tri)


def _slotmap_kernel(pos_ref, lo_ref, hi_ref, o_ref):
    n_pairs = pos_ref.shape[0]
    spare_mask = 2 * TM_MOE - 1
    for e in range(lo_ref.shape[0]):
        def fill(p, c):
            o_ref[p] = n_pairs + (p & spare_mask)
            return c
        lax.fori_loop(lo_ref[e], hi_ref[e], fill, 0)

    def place(f, c):
        o_ref[pos_ref[f]] = f
        return c
    lax.fori_loop(0, n_pairs, place, 0, unroll=32)


def _slot_map(pos_flat, lo, hi, n_slots):
    smem = pl.BlockSpec(memory_space=pltpu.SMEM)
    return pl.pallas_call(
        _slotmap_kernel,
        in_specs=[smem, smem, smem],
        out_specs=smem,
        out_shape=jax.ShapeDtypeStruct((n_slots,), jnp.int32),
        name="moe_slot_map",
    )(pos_flat, lo, hi)


def _moe_kernel(te_ref, nused_ref, half_ref, fnext_ref, fprev_ref, f0_ref, h_hbm, wg_hbm, wu_hbm, wd_hbm,
                y_hbm, wg_res, wu_res, wd_res, stg_in, stg_out, xbuf, xb, act, acc, stage, gsem, ssem, wsem):
    i = pl.program_id(0)
    nt = pl.num_programs(0)
    used_tiles = nused_ref[0]
    tm = xb.shape[0]
    f_dim = wg_res.shape[1]
    tok_mask = h_hbm.shape[0] // SLAB - 1
    tile_rows = tm * SLAB
    out_rows = tm * PACK

    def slab(ix, width=SLAB):
        return pl.ds(pl.multiple_of(ix * width, width), width)

    def gather_row(fref, r, slot, zero=0):
        tok = fref[0, 0, r + zero] & tok_mask
        return pltpu.make_async_copy(h_hbm.at[slab(tok)], xbuf.at[slot, slab(r)], gsem.at[slot])

    def scatter_row(fref, r, slot, zero=0):
        return pltpu.make_async_copy(stage.at[slot, slab(r, PACK)], y_hbm.at[slab(fref[0, 0, r + zero], PACK)],
                                     ssem.at[slot])

    def gather_all(slot):
        return pltpu.make_async_copy(h_hbm.at[pl.ds(0, tile_rows)], xbuf.at[slot], gsem.at[slot])

    def scatter_all(slot):
        return pltpu.make_async_copy(stage.at[slot], y_hbm.at[pl.ds(0, out_rows)], ssem.at[slot])

    cur = i % 2
    used = i < used_tiles

    @pl.when(i == 0)
    def _():
        stage[...] = jnp.zeros_like(stage)
        spare = y_hbm.shape[0] - 2 * out_rows
        fills = [pltpu.make_async_copy(stage.at[sl], y_hbm.at[pl.ds(spare + sl * out_rows, out_rows)],
                                       ssem.at[sl])
                 for sl in range(2)]
        for cp in fills:
            cp.start()
        for cp in fills:
            cp.wait()

    @pl.when((i == 0) & used)
    def _():
        def prime(r, c):
            gather_row(f0_ref, r, 0).start()
            return c
        lax.fori_loop(0, tm, prime, 0)

    expert = te_ref[i]
    new_expert = used & ((i == 0) | (expert != te_ref[jnp.maximum(i - 1, 0)]))

    has_next = i + 1 < used_tiles
    has_prev = (i >= 1) & (i - 1 < used_tiles)
    steady = (i >= 1) & has_next
    half_tile = half_ref[i] == 1

    @pl.when(used)
    def _():
        gather_all(cur).wait()

    @pl.when((i >= 2) & (i - 2 < used_tiles))
    def _():
        scatter_all(cur).wait()

    half = f_dim // MOE_SPLIT
    n_piece = half // W_CHUNK
    n_groups = MOE_SPLIT * (n_piece + 1)

    def tick(v):
        bits = jnp.max(lax.bitcast_convert_type(v[0:SLAB, 0:LANES], jnp.int32))
        return lax.shift_right_logical(lax.shift_right_logical(bits, 16), 16)

    def expert_ffn(issue_group, rows=tm):
        for cix in range(SLAB):
            xb[0:rows, cix * LANES:(cix + 1) * LANES] = (
                xbuf.at[cur][pl.ds(cix, rows, stride=SLAB), :].astype(BF16))
        xv = xb[0:rows]
        out = None
        issue_group(0, 0)
        k = 1
        for hf in range(MOE_SPLIT):
            for c in range(n_piece):
                cs = slice(hf * half + c * W_CHUNK, hf * half + (c + 1) * W_CHUNK)
                g = jnp.dot(xv, wg_res[:, cs], preferred_element_type=F32)
                up = jnp.dot(xv, wu_res[:, cs], preferred_element_type=F32)
                act[0:rows, c * W_CHUNK:(c + 1) * W_CHUNK] = (g * _sigmoid(g) * up).astype(BF16)
                issue_group(k, tick(g))
                k += 1
            part = jnp.dot(act[0:rows], wd_res[hf * half:(hf + 1) * half, :], preferred_element_type=F32)
            if hf < MOE_SPLIT - 1:
                acc[0:rows] = part if out is None else acc[0:rows] + part
                out = acc
                issue_group(k, tick(part))
                k += 1
            else:
                _pack_rows(part if out is None else acc[0:rows] + part, stage.at[cur])

    def chunk_plan():
        n_in = D // W_ROWS_IN
        plan = [(w, res, stg_in, W_RING_IN, 0, W_ROWS_IN, c, wi * n_in + c)
                for wi, (w, res) in enumerate(((wg_hbm, wg_res), (wu_hbm, wu_res))) for c in range(n_in)]
        return plan + [(wd_hbm, wd_res, stg_out, W_RING_OUT, W_RING_IN, W_ROWS_OUT, c, c)
                       for c in range(f_dim // W_ROWS_OUT)]

    def chunk_copy(entry, ex):
        w_hbm, _, stg, ring, sem0, rows, c, j = entry
        return pltpu.make_async_copy(w_hbm.at[ex, pl.ds(c * rows, rows), :], stg.at[j % ring],
                                     wsem.at[sem0 + j % ring])

    def start_first_chunks(ex):
        for entry in chunk_plan():
            if entry[7] < entry[3]:
                chunk_copy(entry, ex).start()

    @pl.when(new_expert)
    def _():
        plan = chunk_plan()

        @pl.when(i == 0)
        def _():
            start_first_chunks(expert)

        for k, entry in enumerate(plan):
            _, res, stg, ring, _, rows, c, j = entry
            chunk_copy(entry, expert).wait()
            res[c * rows:(c + 1) * rows, :] = stg[j % ring].astype(BF16)
            later = [e for e in plan[k + 1:] if e[2] is stg and e[7] == j + ring]
            if later:
                chunk_copy(later[0], expert).start()

    next_expert = te_ref[jnp.minimum(i + 1, nt - 1)]

    @pl.when(has_next & (next_expert != expert))
    def _():
        start_first_chunks(next_expert)

    def issue_group(k, zero):
        for r in range(k * tm // n_groups, (k + 1) * tm // n_groups):
            gather_row(fnext_ref, r, 1 - cur, zero).start()
            scatter_row(fprev_ref, r, 1 - cur, zero).start()

    @pl.when(steady & jnp.logical_not(half_tile))
    def _():
        expert_ffn(issue_group)

    @pl.when(steady & half_tile)
    def _():
        expert_ffn(issue_group, tm // 2)

    @pl.when(jnp.logical_not(steady))
    def _():
        @pl.when(used)
        def _():
            expert_ffn(lambda k, zero: None)

        @pl.when(has_next)
        def _():
            def issue(r, c):
                gather_row(fnext_ref, r, 1 - cur).start()
                return c
            lax.fori_loop(0, tm, issue, 0)

        @pl.when(has_prev)
        def _():
            def issue(r, c):
                scatter_row(fprev_ref, r, 1 - cur).start()
                return c
            lax.fori_loop(0, tm, issue, 0)

    @pl.when((i == nt - 1) & (nt - 2 < used_tiles))
    def _():
        scatter_all(1 - cur).wait()


def _moe_experts(h_slabs, fmap, tile_expert, n_used, tile_half, wg, wu, wd, n_tiles, y_rows):
    f = wg.shape[2]
    tm = TM_MOE
    fblk = lambda imap: pl.BlockSpec((1, 1, tm), imap, memory_space=pltpu.SMEM)
    hbm = pl.BlockSpec(memory_space=pl.ANY)
    return pl.pallas_call(
        _moe_kernel,
        grid_spec=pltpu.PrefetchScalarGridSpec(
            num_scalar_prefetch=3,
            grid=(n_tiles,),
            in_specs=[
                fblk(lambda i, te, nu, hf: (jnp.minimum(i + 1, n_tiles - 1), 0, 0)),
                fblk(lambda i, te, nu, hf: (jnp.maximum(i - 1, 0), 0, 0)),
                fblk(lambda i, te, nu, hf: (0, 0, 0)),
                hbm, hbm, hbm, hbm,
            ],
            out_specs=hbm,
            scratch_shapes=[pltpu.VMEM((D, f), BF16), pltpu.VMEM((D, f), BF16), pltpu.VMEM((f, D), BF16),
                            pltpu.VMEM((W_RING_IN, W_ROWS_IN, f), F32), pltpu.VMEM((W_RING_OUT, W_ROWS_OUT, D), F32),
                            pltpu.VMEM((2, tm * SLAB, LANES), F32), pltpu.VMEM((tm, D), BF16),
                            pltpu.VMEM((tm, f // MOE_SPLIT), BF16), pltpu.VMEM((tm, D), F32),
                            pltpu.VMEM((2, tm * PACK, LANES), jnp.uint32),
                            pltpu.SemaphoreType.DMA((2,)), pltpu.SemaphoreType.DMA((2,)),
                            pltpu.SemaphoreType.DMA((W_RING_IN + W_RING_OUT,))],
        ),
        out_shape=jax.ShapeDtypeStruct((y_rows * PACK, LANES), jnp.uint32),
        compiler_params=_cparams(("arbitrary",), MOE_VMEM_LIMIT),
        name="moe_experts",
    )(tile_expert, n_used, tile_half, fmap, fmap, fmap, h_slabs, wg, wu, wd)


def _combine_kernel(y1_ref, y2_ref, x_ref, w_ref, mod_ref, gain_ref, o_ref):
    w = w_ref[...]
    rows = x_ref.shape[0]
    moe = w[:, 0:1] * _unpack_rows(y1_ref, rows) + w[:, 1:2] * _unpack_rows(y2_ref, rows)
    mod = mod_ref[0, 0]
    x4 = x_ref[...] + mod[5:6] * moe
    ms = jnp.mean(x4 * x4, axis=-1, keepdims=True)
    o_ref[...] = x4 * lax.rsqrt(ms + EPS) * gain_ref[...]


def _combine(y, x3_2d, wts, mod, final_gain, seq):
    n = x3_2d.shape[0]
    tc = TC_COMB
    nt = n // tc
    per_b = seq // tc
    return pl.pallas_call(
        _combine_kernel,
        grid=(nt,),
        in_specs=[
            pl.BlockSpec((tc * PACK, LANES), lambda i: (i, 0)),
            pl.BlockSpec((tc * PACK, LANES), lambda i: (i + nt, 0)),
            pl.BlockSpec((tc, D), lambda i: (i, 0)),
            pl.BlockSpec((tc, 2), lambda i: (i, 0)),
            pl.BlockSpec((1, 1, 6, D), lambda i: (1, i // per_b, 0, 0)),
            pl.BlockSpec((1, D), lambda i: (0, 0)),
        ],
        out_specs=pl.BlockSpec((tc, D), lambda i: (i, 0)),
        out_shape=jax.ShapeDtypeStruct((n, D), F32),
        compiler_params=_cparams(("arbitrary",)),
        name="moe_combine",
    )(y, y, x3_2d, wts, mod, final_gain.reshape(1, D))


def _rope_tables(seq):
    rows = seq // GRID_W
    row_pos = jnp.repeat(jnp.arange(rows, dtype=F32), GRID_W)
    col_pos = jnp.tile(jnp.arange(GRID_W, dtype=F32), rows)
    axis_dim = HEAD_DIM // 2
    inv_freq = ROPE_BASE ** (-jnp.arange(0, axis_dim, 2, dtype=F32) / axis_dim)
    ar = row_pos[:, None] * inv_freq
    ac = col_pos[:, None] * inv_freq
    cos64 = jnp.concatenate([jnp.cos(ar), jnp.cos(ar), jnp.cos(ac), jnp.cos(ac)], axis=1)
    sin64 = jnp.concatenate([-jnp.sin(ar), jnp.sin(ar), -jnp.sin(ac), jnp.sin(ac)], axis=1)
    return jnp.tile(cos64, (1, 2)), jnp.tile(sin64, (1, 2))


def _band_matrices():
    r = np.arange(BLOCK)[:, None]
    c = np.arange(BLOCK + 2 * POOL_HALO)[None, :] - POOL_HALO
    mats = []
    for w in POOL_SIZES:
        lo = -(w // 2)
        hi = w - 1 - w // 2
        mats.append(((c >= r + lo) & (c <= r + hi)).astype(np.float32))
    return jnp.asarray(np.stack(mats), dtype=BF16)


def kernel(x, c, ctx, c_ctx, w_ada, b_ada, w_in, attn_sink, gm_gain, gm_w_s, gm_b_s, w_out,
           ffn_w_gate, ffn_w_up, ffn_w_down, pool_w, pool_scale, router_w,
           moe_w_gate, moe_w_up, moe_w_down, final_gain):
    b, s, _ = x.shape
    n = b * s
    assert w_ada.shape[0] == 2 and w_in.shape[0] == 1 and pool_w.shape[0] == 1
    assert s % TM_IN == 0 and s % TM_POOL == 0 and s % TM_FFN == 0 and b <= 4
    assert n & (n - 1) == 0

    cvec = jnp.concatenate([c, c_ctx[None, :], jnp.zeros((8 - b - 1, D), F32)], axis=0)
    mod = _ada_mod(cvec, w_ada, b_ada)

    cos_t, sin_t = _rope_tables(s)
    w_in_bf = w_in[0].astype(BF16)
    q, k, ksw, v, vsw, u, vg = _in_proj(x, mod, w_in_bf, gm_gain[0], cos_t, sin_t)
    kx, ksx, vx, vsx = _ctx_kv(ctx, mod, w_in_bf)
    wcat = gm_w_s[0].reshape(4, 2, BLOCK, BLOCK).transpose(0, 2, 1, 3).reshape(4, BLOCK, 2 * BLOCK).astype(BF16)
    x1 = _attn_mixer(x, mod, attn_sink[0], q, k, ksw, v, vsw, kx, ksx, vx, vsx, u, vg,
                     wcat, gm_b_s[0].T, w_out[0].astype(BF16))
    x2 = _dense_ffn(x1.reshape(n, D), mod, ffn_w_gate[0].astype(BF16), ffn_w_up[0].astype(BF16),
                    ffn_w_down[0].astype(BF16), s)

    wr = jnp.pad(router_w[0], ((0, 0), (0, LANES - N_EXPERTS)))
    wr_hi = wr.astype(BF16)
    tri = jnp.asarray(np.tril(np.ones((TM_POOL, TM_POOL), np.float32), -1), dtype=BF16)
    x3, h2, route, counts = _pool_route(x2.reshape(b, s, D), mod, _band_matrices(), pool_w[0].astype(BF16),
                                        pool_scale[0], wr_hi, tri)

    tm = TM_MOE
    n_tiles = (2 * n) // tm + N_EXPERTS
    cnt = counts[0, :N_EXPERTS].astype(jnp.int32)
    tiles_e = (cnt + tm - 1) // tm
    tile_end = jnp.cumsum(tiles_e)
    off = (tile_end - tiles_e) * tm
    n_used = tile_end[-1]
    tix = jnp.arange(n_tiles, dtype=jnp.int32)
    te = jnp.minimum(jnp.sum(tix[:, None] >= tile_end[None, :], axis=1), N_EXPERTS - 1).astype(jnp.int32)
    te_last = te[jnp.maximum(n_used - 1, 0)]
    tile_expert = jnp.where(tix < n_used, te, te_last)
    rows_left = cnt[te] - (tix - (tile_end - tiles_e)[te]) * tm
    tile_half = ((tix < n_used) & (rows_left <= tm // 2)).astype(jnp.int32)
    e1 = route[0].astype(jnp.int32)
    e2 = route[1].astype(jnp.int32)
    pos1 = off[e1] + route[4].astype(jnp.int32)
    pos2 = off[e2] + route[5].astype(jnp.int32)
    n_slots = n_tiles * tm
    pad_lo = jnp.concatenate([off + cnt, (n_used * tm).reshape(1)]).astype(jnp.int32)
    pad_hi = jnp.concatenate([off + tiles_e * tm, jnp.full((1,), n_slots, jnp.int32)]).astype(jnp.int32)
    fmap = _slot_map(jnp.concatenate([pos1, pos2]), pad_lo, pad_hi, n_slots)
    n_used_arr = n_used.reshape(1).astype(jnp.int32)

    y = _moe_experts(h2, fmap.reshape(n_tiles, 1, tm), tile_expert, n_used_arr, tile_half,
                     moe_w_gate[0], moe_w_up[0], moe_w_down[0], n_tiles, 2 * n + 2 * tm)
    out = _combine(y, x3.reshape(n, D), route[2:4].T, mod, final_gain, s)
    return out.reshape(b, s, D)
```

```python
import functools

import numpy as np
import jax
import jax.numpy as jnp
from jax import lax
from jax.experimental import pallas as pl
from jax.experimental.pallas import tpu as pltpu

F32 = jnp.float32
BF16 = jnp.bfloat16

D = 1024
GRID_W = 64
EPS = 1e-6
NEG_INF = -1e30
HEAD_DIM = 64
N_Q_HEADS = 8
BLOCK = 128
ATT_W = 512
KV_W = 128
GM_W = 512
IN_W = 1792
POOL_SIZES = (2, 4, 8, 16)
POOL_GD = 256
POOL_HALO = 16
N_EXPERTS = 8
ROPE_BASE = 10000.0
LANES = 128
SLAB = D // LANES
PACK = SLAB // 2
SQRT_2_OVER_PI = 0.7978845608028654

TM_IN = 1024
TM_FFN = 1024
TM_POOL = 1024
TM_MOE = 512
MOE_SPLIT = 2
W_CHUNK = 256
W_ROWS_IN = 128
W_ROWS_OUT = 512
W_RING_IN = 12
W_RING_OUT = 2
MOE_VMEM_LIMIT = 60 * 1024 * 1024
TC_COMB = 1024
Q_BLOCKS = 8
VMEM_LIMIT = 56 * 1024 * 1024


def _cparams(sem, vmem=None):
    return pltpu.CompilerParams(dimension_semantics=sem, vmem_limit_bytes=vmem)


def _modulate(xf, shift, scale):
    ms = jnp.mean(xf * xf, axis=-1, keepdims=True)
    return xf * lax.rsqrt(ms + EPS) * (1.0 + scale) + shift


def _sigmoid(z):
    return 1.0 / (1.0 + jnp.exp(-z))


def _rows_to_slabs(val, slab_ref):
    rows = val.shape[0]
    for cix in range(SLAB):
        slab_ref[pl.ds(cix, rows, stride=SLAB), :] = val[:, cix * LANES:(cix + 1) * LANES]


def _slabs_to_rows(slab_ref, rows):
    return jnp.concatenate([slab_ref[pl.ds(cix, rows, stride=SLAB), :] for cix in range(SLAB)], axis=1)


def _pack_rows(val, pack_ref):
    rows = val.shape[0]
    bits = lambda v: lax.bitcast_convert_type(v.astype(BF16).astype(F32), jnp.uint32)
    for cix in range(PACK):
        hi = bits(val[:, cix * LANES:(cix + 1) * LANES])
        lo = bits(val[:, D // 2 + cix * LANES:D // 2 + (cix + 1) * LANES])
        pack_ref[pl.ds(cix, rows, stride=PACK), :] = hi | (lo >> 16)


def _unpack_rows(pack_ref, rows):
    words = [pack_ref[pl.ds(cix, rows, stride=PACK), :] for cix in range(PACK)]
    his = [lax.bitcast_convert_type(w & jnp.uint32(0xFFFF0000), F32) for w in words]
    los = [lax.bitcast_convert_type(w << 16, F32) for w in words]
    return jnp.concatenate(his + los, axis=1)


def _ada_kernel(c_ref, w_ref, b_ref, o_ref):
    c = c_ref[...]
    s = c * _sigmoid(c)
    o_ref[0] = jnp.dot(s.astype(BF16), w_ref[0].astype(BF16), preferred_element_type=F32) + b_ref[0]


def _ada_mod(cvec, w_ada, b_ada):
    depth, _, n6 = w_ada.shape
    tn = 1536
    out = pl.pallas_call(
        _ada_kernel,
        grid=(depth, n6 // tn),
        in_specs=[
            pl.BlockSpec((8, D), lambda l, j: (0, 0)),
            pl.BlockSpec((1, D, tn), lambda l, j: (l, 0, j)),
            pl.BlockSpec((1, 1, tn), lambda l, j: (l, 0, j)),
        ],
        out_specs=pl.BlockSpec((1, 8, tn), lambda l, j: (l, 0, j)),
        out_shape=jax.ShapeDtypeStruct((depth, 8, n6), F32),
        compiler_params=_cparams(("arbitrary", "arbitrary")),
        name="ada_mod",
    )(cvec, w_ada, b_ada.reshape(depth, 1, n6))
    return out.reshape(depth, 8, 6, D)


def _rope(t, cs, sn, first_half):
    fwd = pltpu.roll(t, LANES - 16, axis=1)
    bwd = pltpu.roll(t, 16, axis=1)
    return t * cs + jnp.where(first_half, fwd, bwd) * sn


def _inproj_kernel(x_ref, mod_ref, w_ref, gain_ref, cos_ref, sin_ref,
                   q_ref, k_ref, ksw_ref, v_ref, vsw_ref, u_ref, vg_ref, w_bf):
    @pl.when((pl.program_id(0) == 0) & (pl.program_id(1) == 0))
    def _():
        w_bf[...] = w_ref[...].astype(BF16)

    mod = mod_ref[0, 0]
    h = _modulate(x_ref[0], mod[0:1], mod[1:2]).astype(BF16)
    proj = jnp.dot(h, w_bf[...], preferred_element_type=F32)
    cs = cos_ref[...]
    sn = sin_ref[...]
    lane = lax.broadcasted_iota(jnp.int32, cs.shape, 1)
    first_half = (lane & 16) == 0
    for cix in range(ATT_W // LANES):
        t = proj[:, cix * LANES:(cix + 1) * LANES]
        q_ref[0, :, cix * LANES:(cix + 1) * LANES] = (
            _rope(t, cs, sn, first_half) * (HEAD_DIM ** -0.5)).astype(BF16)
    kr = _rope(proj[:, ATT_W:ATT_W + KV_W], cs, sn, first_half)
    k_ref[0] = kr.astype(BF16)
    ksw_ref[0] = pltpu.roll(kr, HEAD_DIM, axis=1).astype(BF16)
    vv = proj[:, ATT_W + KV_W:ATT_W + 2 * KV_W]
    v_ref[0] = vv.astype(BF16)
    vsw_ref[0] = pltpu.roll(vv, HEAD_DIM, axis=1).astype(BF16)
    z = proj[:, ATT_W + 2 * KV_W:]
    g = z * (0.5 * (1.0 + jnp.tanh(SQRT_2_OVER_PI * (z + 0.044715 * (z * z * z)))))
    u_ref[0] = g[:, :GM_W].astype(BF16)
    vg = g[:, GM_W:]
    ms = jnp.mean(vg * vg, axis=-1, keepdims=True)
    vg_ref[0] = (vg * lax.rsqrt(ms + EPS) * gain_ref[...]).astype(BF16)


def _in_proj(x, mod, w_in, gm_gain, cos_t, sin_t):
    b, s, _ = x.shape
    tm = TM_IN
    row = lambda w: pl.BlockSpec((1, tm, w), lambda bi, i: (bi, i, 0))
    outs = pl.pallas_call(
        _inproj_kernel,
        grid=(b, s // tm),
        in_specs=[
            row(D),
            pl.BlockSpec((1, 1, 6, D), lambda bi, i: (0, bi, 0, 0)),
            pl.BlockSpec((D, IN_W), lambda bi, i: (0, 0), pipeline_mode=pl.Buffered(1)),
            pl.BlockSpec((1, GM_W), lambda bi, i: (0, 0)),
            pl.BlockSpec((tm, LANES), lambda bi, i: (i, 0)),
            pl.BlockSpec((tm, LANES), lambda bi, i: (i, 0)),
        ],
        out_specs=[row(ATT_W), row(KV_W), row(KV_W), row(KV_W), row(KV_W), row(GM_W), row(GM_W)],
        out_shape=[jax.ShapeDtypeStruct((b, s, w), BF16)
                   for w in (ATT_W, KV_W, KV_W, KV_W, KV_W, GM_W, GM_W)],
        scratch_shapes=[pltpu.VMEM((D, IN_W), BF16)],
        compiler_params=_cparams(("arbitrary", "arbitrary"), VMEM_LIMIT),
        name="in_proj",
    )(x, mod, w_in, gm_gain.reshape(1, GM_W), cos_t, sin_t)
    return outs


def _ctx_kernel(c_ref, mod_ref, w_ref, k_ref, ksw_ref, v_ref, vsw_ref):
    mod = mod_ref[0, 0]
    h = _modulate(c_ref[0], mod[0:1], mod[1:2]).astype(BF16)
    kv = jnp.dot(h, w_ref[...].astype(BF16), preferred_element_type=F32)
    kk = kv[:, :KV_W]
    vv = kv[:, KV_W:]
    k_ref[0] = kk.astype(BF16)
    ksw_ref[0] = pltpu.roll(kk, HEAD_DIM, axis=1).astype(BF16)
    v_ref[0] = vv.astype(BF16)
    vsw_ref[0] = pltpu.roll(vv, HEAD_DIM, axis=1).astype(BF16)


def _ctx_kv(ctx, mod, w_in):
    b, l, _ = ctx.shape
    spec = pl.BlockSpec((1, l, KV_W), lambda bi: (bi, 0, 0))
    return pl.pallas_call(
        _ctx_kernel,
        grid=(b,),
        in_specs=[
            pl.BlockSpec((1, l, D), lambda bi: (bi, 0, 0)),
            pl.BlockSpec((1, 1, 6, D), lambda bi: (0, b, 0, 0)),
            pl.BlockSpec((D, 2 * KV_W), lambda bi: (0, ATT_W // (2 * KV_W))),
        ],
        out_specs=[spec] * 4,
        out_shape=[jax.ShapeDtypeStruct((b, l, KV_W), BF16)] * 4,
        compiler_params=_cparams(("arbitrary",)),
        name="ctx_kv",
    )(ctx, mod, w_in)


def _attn_kernel(sink_ref, q_ref, kp_ref, kc_ref, kn_ref, ksp_ref, ksc_ref, ksn_ref,
                 vp_ref, vc_ref, vn_ref, vsp_ref, vsc_ref, vsn_ref,
                 kx_ref, ksx_ref, vx_ref, vsx_ref,
                 u_ref, vg_ref, wcat_ref, bs_ref, wout_ref, x_ref, mod_ref, o_ref, wout_bf):
    n = pl.program_id(1)

    @pl.when((pl.program_id(0) == 0) & (n == 0))
    def _():
        wout_bf[...] = wout_ref[...].astype(BF16)

    nblk = pl.num_programs(1) * Q_BLOCKS
    lane = lax.broadcasted_iota(jnp.int32, (1, LANES), 1)
    low = lane < HEAD_DIM
    zero = jnp.zeros((), BF16)

    def variants(a0, a1):
        return ((jnp.where(low, a0, zero), jnp.where(low, zero, a1)),
                (jnp.where(low, a1, zero), jnp.where(low, zero, a0)))

    cat = lambda refs: jnp.concatenate([r[0] for r in refs], axis=0)
    kb_var = variants(cat((kp_ref, kc_ref, kn_ref)), cat((ksp_ref, ksc_ref, ksn_ref)))
    vb_var = variants(cat((vp_ref, vc_ref, vn_ref)), cat((vsp_ref, vsc_ref, vsn_ref)))
    kx_var = variants(kx_ref[0], ksx_ref[0])
    vx_var = variants(vx_ref[0], vsx_ref[0])

    row = lax.broadcasted_iota(jnp.int32, (2 * BLOCK, BLOCK), 0) & (BLOCK - 1)
    col = lax.broadcasted_iota(jnp.int32, (2 * BLOCK, BLOCK), 1)
    top = lax.broadcasted_iota(jnp.int32, (2 * BLOCK, 1), 0) < BLOCK
    nt_dims = (((1,), (1,)), ((), ()))

    q = q_ref[0]
    att_blocks = [[None] * 4 for _ in range(Q_BLOCKS)]
    for kvh in range(2):
        qst = jnp.concatenate(
            [q[qb * BLOCK:(qb + 1) * BLOCK, pr * LANES:(pr + 1) * LANES]
             for qb in range(Q_BLOCKS) for pr in (2 * kvh, 2 * kvh + 1)], axis=0)
        accs = [None] * Q_BLOCKS
        for half in range(2):
            sk = jnp.where(top, sink_ref[4 * kvh + half], sink_ref[4 * kvh + 2 + half])
            s_ctx = lax.dot_general(qst, kx_var[kvh][half], nt_dims, preferred_element_type=F32)
            o_band, p_ctx, dens = [], [], []
            for qb in range(Q_BLOCKS):
                g = n * Q_BLOCKS + qb
                qrows = qst[qb * 2 * BLOCK:(qb + 1) * 2 * BLOCK]
                sb = lax.dot_general(qrows, kb_var[kvh][half][qb * BLOCK:(qb + 3) * BLOCK], nt_dims,
                                     preferred_element_type=F32)
                s0 = jnp.where((col >= row) & (g > 0), sb[:, :BLOCK], NEG_INF)
                s1 = sb[:, BLOCK:2 * BLOCK]
                s2 = jnp.where((col <= row) & (g < nblk - 1), sb[:, 2 * BLOCK:], NEG_INF)
                sc = s_ctx[qb * 2 * BLOCK:(qb + 1) * 2 * BLOCK]
                ctx_blocks = [sc[:, cb * BLOCK:(cb + 1) * BLOCK] for cb in range(sc.shape[1] // BLOCK)]
                m = functools.reduce(jnp.maximum, [s0, s1, s2] + ctx_blocks)
                m = jnp.maximum(jnp.max(m, axis=-1, keepdims=True), sk)
                p0, p1, p2, pc = (jnp.exp(t - m) for t in (s0, s1, s2, sc))
                psum = functools.reduce(
                    jnp.add, [p0, p1, p2] + [pc[:, cb * BLOCK:(cb + 1) * BLOCK] for cb in range(len(ctx_blocks))])
                den = jnp.sum(psum, axis=-1, keepdims=True) + jnp.exp(sk - m)
                pb = jnp.concatenate([p0, p1, p2], axis=1).astype(BF16)
                o_band.append(jnp.dot(pb, vb_var[kvh][half][qb * BLOCK:(qb + 3) * BLOCK],
                                      preferred_element_type=F32))
                p_ctx.append(pc.astype(BF16))
                dens.append(den)
            o_ctx = jnp.dot(jnp.concatenate(p_ctx, axis=0), vx_var[kvh][half], preferred_element_type=F32)
            for qb in range(Q_BLOCKS):
                o = (o_band[qb] + o_ctx[qb * 2 * BLOCK:(qb + 1) * 2 * BLOCK]) / dens[qb]
                accs[qb] = o if accs[qb] is None else accs[qb] + o
        for qb in range(Q_BLOCKS):
            att_blocks[qb][2 * kvh] = accs[qb][:BLOCK]
            att_blocks[qb][2 * kvh + 1] = accs[qb][BLOCK:]

    u = u_ref[0]
    vg = vg_ref[0]
    bs = bs_ref[...]
    gm_blocks = [[None] * 4 for _ in range(Q_BLOCKS)]
    for j in range(GM_W // LANES):
        chunks = [vg[c * BLOCK:(c + 1) * BLOCK, j * LANES:(j + 1) * LANES] for c in range(Q_BLOCKS)]
        rhs = jnp.concatenate(
            [jnp.concatenate([jnp.where(low, v, zero) for v in chunks], axis=1),
             jnp.concatenate([jnp.where(low, zero, v) for v in chunks], axis=1)], axis=0)
        mixed = jnp.dot(wcat_ref[j], rhs, preferred_element_type=F32)
        bias = jnp.where(low, bs[:, 2 * j:2 * j + 1], bs[:, 2 * j + 1:2 * j + 2])
        for c in range(Q_BLOCKS):
            gm_blocks[c][j] = (u[c * BLOCK:(c + 1) * BLOCK, j * LANES:(j + 1) * LANES].astype(F32)
                               * (mixed[:, c * LANES:(c + 1) * LANES] + bias))

    mix = jnp.concatenate([jnp.concatenate(att_blocks[c] + gm_blocks[c], axis=1) for c in range(Q_BLOCKS)],
                          axis=0).astype(BF16)
    y = jnp.dot(mix, wout_bf[...], preferred_element_type=F32)
    mod = mod_ref[0, 0]
    o_ref[0] = x_ref[0] + mod[2:3] * y


def _attn_mixer(x, mod, sink, q, k, ksw, v, vsw, kx, ksx, vx, vsx, u, vg, wcat_bf, bs_t, wout):
    b, s, _ = x.shape
    tq = Q_BLOCKS * BLOCK
    nb = s // BLOCK
    l = kx.shape[1]
    cur = lambda w: pl.BlockSpec((1, tq, w), lambda bi, n: (bi, n, 0))
    prv = lambda w: pl.BlockSpec((1, BLOCK, w), lambda bi, n: (bi, jnp.maximum(n * Q_BLOCKS - 1, 0), 0))
    nxt = lambda w: pl.BlockSpec((1, BLOCK, w), lambda bi, n: (bi, jnp.minimum((n + 1) * Q_BLOCKS, nb - 1), 0))
    cx = pl.BlockSpec((1, l, KV_W), lambda bi, n: (bi, 0, 0))
    return pl.pallas_call(
        _attn_kernel,
        grid=(b, s // tq),
        in_specs=[
            pl.BlockSpec(memory_space=pltpu.SMEM),
            cur(ATT_W),
            prv(KV_W), cur(KV_W), nxt(KV_W), prv(KV_W), cur(KV_W), nxt(KV_W),
            prv(KV_W), cur(KV_W), nxt(KV_W), prv(KV_W), cur(KV_W), nxt(KV_W),
            cx, cx, cx, cx,
            cur(GM_W), cur(GM_W),
            pl.BlockSpec((4, BLOCK, 2 * BLOCK), lambda bi, n: (0, 0, 0)),
            pl.BlockSpec((BLOCK, 8), lambda bi, n: (0, 0)),
            pl.BlockSpec((D, D), lambda bi, n: (0, 0), pipeline_mode=pl.Buffered(1)),
            cur(D),
            pl.BlockSpec((1, 1, 6, D), lambda bi, n: (0, bi, 0, 0)),
        ],
        out_specs=cur(D),
        out_shape=jax.ShapeDtypeStruct((b, s, D), F32),
        scratch_shapes=[pltpu.VMEM((D, D), BF16)],
        compiler_params=_cparams(("arbitrary", "arbitrary"), VMEM_LIMIT),
        name="attn_gmlp_out",
    )(sink, q, k, k, k, ksw, ksw, ksw, v, v, v, vsw, vsw, vsw, kx, ksx, vx, vsx,
      u, vg, wcat_bf, bs_t, wout, x, mod)


def _ffn_kernel(x_ref, mod_ref, wg_ref, wu_ref, wd_ref, o_ref, act):
    mod = mod_ref[0, 0]
    xf = x_ref[...]
    h = _modulate(xf, mod[3:4], mod[4:5]).astype(BF16)
    for c in range(wg_ref.shape[1] // W_CHUNK):
        cs = slice(c * W_CHUNK, (c + 1) * W_CHUNK)
        g = jnp.dot(h, wg_ref[:, cs], preferred_element_type=F32)
        up = jnp.dot(h, wu_ref[:, cs], preferred_element_type=F32)
        act[:, cs] = (g * _sigmoid(g) * up).astype(BF16)
    o_ref[...] = xf + mod[5:6] * jnp.dot(act[...], wd_ref[...], preferred_element_type=F32)


def _dense_ffn(x2d, mod, wg, wu, wd, seq):
    n = x2d.shape[0]
    f = wg.shape[1]
    tm = TM_FFN
    per_b = seq // tm
    resident = lambda shp: pl.BlockSpec(shp, lambda i: (0, 0), pipeline_mode=pl.Buffered(1))
    return pl.pallas_call(
        _ffn_kernel,
        grid=(n // tm,),
        in_specs=[
            pl.BlockSpec((tm, D), lambda i: (i, 0)),
            pl.BlockSpec((1, 1, 6, D), lambda i: (0, i // per_b, 0, 0)),
            resident((D, f)), resident((D, f)), resident((f, D)),
        ],
        out_specs=pl.BlockSpec((tm, D), lambda i: (i, 0)),
        out_shape=jax.ShapeDtypeStruct((n, D), F32),
        scratch_shapes=[pltpu.VMEM((tm, f), BF16)],
        compiler_params=_cparams(("arbitrary",), VMEM_LIMIT),
        name="dense_ffn",
    )(x2d, mod, wg, wu, wd)


def _pool_route_kernel(x_ref, xp_ref, xn_ref, mod_ref, band_ref, pw_ref, psc_ref, wr_hi_ref,
                       tri_ref, x3_ref, h2_ref, route_ref, cnt_ref, hext, carry):
    bi = pl.program_id(0)
    i = pl.program_id(1)
    ni = pl.num_programs(1)
    tm = x_ref.shape[1]
    seq = tm * ni
    mod = mod_ref[0, 0]

    @pl.when((bi == 0) & (i == 0))
    def _():
        carry[...] = jnp.zeros_like(carry)

    xf = x_ref[0]
    hp = _modulate(xp_ref[0], mod[0:1], mod[1:2])
    hn = _modulate(xn_ref[0], mod[0:1], mod[1:2])
    hext[0:POOL_HALO] = jnp.where(i > 0, hp, 0.0).astype(BF16)
    h_main = _modulate(xf, mod[0:1], mod[1:2])
    hext[POOL_HALO:POOL_HALO + tm] = h_main.astype(BF16)
    hext[POOL_HALO + tm:] = jnp.where(i < ni - 1, hn, 0.0).astype(BF16)

    t_local = lax.broadcasted_iota(jnp.int32, (BLOCK, 1), 0)
    ys = []
    for gi, w in enumerate(POOL_SIZES):
        lo_off = -(w // 2)
        hi_off = w - 1 - w // 2
        cols = slice(gi * POOL_GD, (gi + 1) * POOL_GD)
        outs = []
        for sb in range(tm // BLOCK):
            r0 = sb * BLOCK
            win = jnp.dot(band_ref[gi], hext[r0:r0 + BLOCK + 2 * POOL_HALO, cols],
                          preferred_element_type=F32)
            t = i * tm + r0 + t_local
            cnt = (jnp.minimum(t + hi_off, seq - 1) - jnp.maximum(t + lo_off, 0) + 1).astype(F32)
            diff = win / cnt - h_main[r0:r0 + BLOCK, cols]
            outs.append(diff.astype(BF16))
        dg = jnp.concatenate(outs, axis=0)
        ys.append(jnp.dot(dg, pw_ref[gi], preferred_element_type=F32))
    y = jnp.concatenate(ys, axis=1) * psc_ref[...]
    x3 = xf + mod[2:3] * y
    x3_ref[0] = x3

    h2 = _modulate(x3, mod[3:4], mod[4:5])
    _rows_to_slabs(h2, h2_ref)
    logits = jnp.dot(h2.astype(BF16), wr_hi_ref[...], preferred_element_type=F32)
    lane = lax.broadcasted_iota(jnp.int32, (tm, LANES), 1)
    lane_f = lane.astype(F32)
    neg = -jnp.inf
    lg = jnp.where(lane < N_EXPERTS, logits, neg)
    m1 = jnp.max(lg, axis=-1, keepdims=True)
    i1 = jnp.min(jnp.where(lg == m1, lane_f, float(LANES)), axis=-1, keepdims=True)
    oh1 = lane_f == i1
    lg2 = jnp.where(oh1, neg, lg)
    m2 = jnp.max(lg2, axis=-1, keepdims=True)
    i2 = jnp.min(jnp.where(lg2 == m2, lane_f, float(LANES)), axis=-1, keepdims=True)
    oh2 = lane_f == i2
    e = jnp.exp(m2 - m1)
    w1 = 1.0 / (1.0 + e)
    w2 = e / (1.0 + e)
    oh = jnp.where(oh1 | oh2, 1.0, 0.0)
    before = jnp.dot(tri_ref[...], oh.astype(BF16), preferred_element_type=F32) + carry[...]
    r1 = jnp.sum(jnp.where(oh1, before, 0.0), axis=-1, keepdims=True)
    r2 = jnp.sum(jnp.where(oh2, before, 0.0), axis=-1, keepdims=True)
    carry[...] = carry[...] + jnp.sum(oh, axis=0, keepdims=True)
    cnt_ref[...] = carry[...]
    info = jnp.where(lane == 0, i1, jnp.where(lane == 1, i2, jnp.where(lane == 2, w1, jnp.where(
        lane == 3, w2, jnp.where(lane == 4, r1, jnp.where(lane == 5, r2, 0.0))))))
    route_ref[...] = info.T[0:8, :]


def _pool_route(x, mod, band, pw_bf, pool_scale, wr_hi, tri):
    b, s, _ = x.shape
    tm = TM_POOL
    ni = s // tm
    hb = tm // POOL_HALO
    row = pl.BlockSpec((1, tm, D), lambda bi, i: (bi, i, 0))
    const2 = lambda shp: pl.BlockSpec(shp, lambda bi, i: (0,) * len(shp))
    return pl.pallas_call(
        _pool_route_kernel,
        grid=(b, ni),
        in_specs=[
            row,
            pl.BlockSpec((1, POOL_HALO, D), lambda bi, i: (bi, jnp.maximum(i * hb - 1, 0), 0)),
            pl.BlockSpec((1, POOL_HALO, D), lambda bi, i: (bi, jnp.minimum((i + 1) * hb, s // POOL_HALO - 1), 0)),
            pl.BlockSpec((1, 1, 6, D), lambda bi, i: (1, bi, 0, 0)),
            const2(band.shape), const2(pw_bf.shape), const2((1, D)),
            const2(wr_hi.shape), const2(tri.shape),
        ],
        out_specs=[row,
                   pl.BlockSpec((tm * SLAB, LANES), lambda bi, i: (bi * ni + i, 0)),
                   pl.BlockSpec((8, tm), lambda bi, i: (0, bi * ni + i)),
                   pl.BlockSpec((1, LANES), lambda bi, i: (0, 0))],
        out_shape=[jax.ShapeDtypeStruct((b, s, D), F32), jax.ShapeDtypeStruct((b * s * SLAB, LANES), F32),
                   jax.ShapeDtypeStruct((8, b * s), F32), jax.ShapeDtypeStruct((1, LANES), F32)],
        scratch_shapes=[pltpu.VMEM((tm + 2 * POOL_HALO, D), BF16), pltpu.VMEM((1, LANES), F32)],
        compiler_params=_cparams(("arbitrary", "arbitrary"), VMEM_LIMIT),
        name="pool_route",
    )(x, x, x, mod, band, pw_bf, pool_scale.reshape(1, D), wr_hi, tri)


def _slotmap_kernel(pos_ref, lo_ref, hi_ref, o_ref):
    n_pairs = pos_ref.shape[0]
    spare_mask = 2 * TM_MOE - 1
    for e in range(lo_ref.shape[0]):
        def fill(p, c):
            o_ref[p] = n_pairs + (p & spare_mask)
            return c
        lax.fori_loop(lo_ref[e], hi_ref[e], fill, 0)

    def place(f, c):
        o_ref[pos_ref[f]] = f
        return c
    lax.fori_loop(0, n_pairs, place, 0, unroll=32)


def _slot_map(pos_flat, lo, hi, n_slots):
    smem = pl.BlockSpec(memory_space=pltpu.SMEM)
    return pl.pallas_call(
        _slotmap_kernel,
        in_specs=[smem, smem, smem],
        out_specs=smem,
        out_shape=jax.ShapeDtypeStruct((n_slots,), jnp.int32),
        name="moe_slot_map",
    )(pos_flat, lo, hi)


def _moe_kernel(te_ref, nused_ref, half_ref, fnext_ref, fprev_ref, f0_ref, h_hbm, wg_hbm, wu_hbm, wd_hbm,
                y_hbm, wg_res, wu_res, wd_res, stg_in, stg_out, xbuf, xb, act, acc, stage, gsem, ssem, wsem):
    i = pl.program_id(0)
    nt = pl.num_programs(0)
    used_tiles = nused_ref[0]
    tm = xb.shape[0]
    f_dim = wg_res.shape[1]
    tok_mask = h_hbm.shape[0] // SLAB - 1
    tile_rows = tm * SLAB
    out_rows = tm * PACK

    def slab(ix, width=SLAB):
        return pl.ds(pl.multiple_of(ix * width, width), width)

    def gather_row(fref, r, slot, zero=0):
        tok = fref[0, 0, r + zero] & tok_mask
        return pltpu.make_async_copy(h_hbm.at[slab(tok)], xbuf.at[slot, slab(r)], gsem.at[slot])

    def scatter_row(fref, r, slot, zero=0):
        return pltpu.make_async_copy(stage.at[slot, slab(r, PACK)], y_hbm.at[slab(fref[0, 0, r + zero], PACK)],
                                     ssem.at[slot])

    def gather_all(slot):
        return pltpu.make_async_copy(h_hbm.at[pl.ds(0, tile_rows)], xbuf.at[slot], gsem.at[slot])

    def scatter_all(slot):
        return pltpu.make_async_copy(stage.at[slot], y_hbm.at[pl.ds(0, out_rows)], ssem.at[slot])

    cur = i % 2
    used = i < used_tiles

    @pl.when(i == 0)
    def _():
        stage[...] = jnp.zeros_like(stage)
        spare = y_hbm.shape[0] - 2 * out_rows
        fills = [pltpu.make_async_copy(stage.at[sl], y_hbm.at[pl.ds(spare + sl * out_rows, out_rows)],
                                       ssem.at[sl])
                 for sl in range(2)]
        for cp in fills:
            cp.start()
        for cp in fills:
            cp.wait()

    @pl.when((i == 0) & used)
    def _():
        def prime(r, c):
            gather_row(f0_ref, r, 0).start()
            return c
        lax.fori_loop(0, tm, prime, 0)

    expert = te_ref[i]
    new_expert = used & ((i == 0) | (expert != te_ref[jnp.maximum(i - 1, 0)]))

    has_next = i + 1 < used_tiles
    has_prev = (i >= 1) & (i - 1 < used_tiles)
    steady = (i >= 1) & has_next
    half_tile = half_ref[i] == 1

    @pl.when(used)
    def _():
        gather_all(cur).wait()

    @pl.when((i >= 2) & (i - 2 < used_tiles))
    def _():
        scatter_all(cur).wait()

    half = f_dim // MOE_SPLIT
    n_piece = half // W_CHUNK
    n_groups = MOE_SPLIT * (n_piece + 1)

    def tick(v):
        bits = jnp.max(lax.bitcast_convert_type(v[0:SLAB, 0:LANES], jnp.int32))
        return lax.shift_right_logical(lax.shift_right_logical(bits, 16), 16)

    def expert_ffn(issue_group, rows=tm):
        for cix in range(SLAB):
            xb[0:rows, cix * LANES:(cix + 1) * LANES] = (
                xbuf.at[cur][pl.ds(cix, rows, stride=SLAB), :].astype(BF16))
        xv = xb[0:rows]
        out = None
        issue_group(0, 0)
        k = 1
        for hf in range(MOE_SPLIT):
            for c in range(n_piece):
                cs = slice(hf * half + c * W_CHUNK, hf * half + (c + 1) * W_CHUNK)
                g = jnp.dot(xv, wg_res[:, cs], preferred_element_type=F32)
                up = jnp.dot(xv, wu_res[:, cs], preferred_element_type=F32)
                act[0:rows, c * W_CHUNK:(c + 1) * W_CHUNK] = (g * _sigmoid(g) * up).astype(BF16)
                issue_group(k, tick(g))
                k += 1
            part = jnp.dot(act[0:rows], wd_res[hf * half:(hf + 1) * half, :], preferred_element_type=F32)
            if hf < MOE_SPLIT - 1:
                acc[0:rows] = part if out is None else acc[0:rows] + part
                out = acc
                issue_group(k, tick(part))
                k += 1
            else:
                _pack_rows(part if out is None else acc[0:rows] + part, stage.at[cur])

    def chunk_plan():
        n_in = D // W_ROWS_IN
        plan = [(w, res, stg_in, W_RING_IN, 0, W_ROWS_IN, c, wi * n_in + c)
                for wi, (w, res) in enumerate(((wg_hbm, wg_res), (wu_hbm, wu_res))) for c in range(n_in)]
        return plan + [(wd_hbm, wd_res, stg_out, W_RING_OUT, W_RING_IN, W_ROWS_OUT, c, c)
                       for c in range(f_dim // W_ROWS_OUT)]

    def chunk_copy(entry, ex):
        w_hbm, _, stg, ring, sem0, rows, c, j = entry
        return pltpu.make_async_copy(w_hbm.at[ex, pl.ds(c * rows, rows), :], stg.at[j % ring],
                                     wsem.at[sem0 + j % ring])

    def start_first_chunks(ex):
        for entry in chunk_plan():
            if entry[7] < entry[3]:
                chunk_copy(entry, ex).start()

    @pl.when(new_expert)
    def _():
        plan = chunk_plan()

        @pl.when(i == 0)
        def _():
            start_first_chunks(expert)

        for k, entry in enumerate(plan):
            _, res, stg, ring, _, rows, c, j = entry
            chunk_copy(entry, expert).wait()
            res[c * rows:(c + 1) * rows, :] = stg[j % ring].astype(BF16)
            later = [e for e in plan[k + 1:] if e[2] is stg and e[7] == j + ring]
            if later:
                chunk_copy(later[0], expert).start()

    next_expert = te_ref[jnp.minimum(i + 1, nt - 1)]

    @pl.when(has_next & (next_expert != expert))
    def _():
        start_first_chunks(next_expert)

    def issue_group(k, zero):
        for r in range(k * tm // n_groups, (k + 1) * tm // n_groups):
            gather_row(fnext_ref, r, 1 - cur, zero).start()
            scatter_row(fprev_ref, r, 1 - cur, zero).start()

    @pl.when(steady & jnp.logical_not(half_tile))
    def _():
        expert_ffn(issue_group)

    @pl.when(steady & half_tile)
    def _():
        expert_ffn(issue_group, tm // 2)

    @pl.when(jnp.logical_not(steady))
    def _():
        @pl.when(used)
        def _():
            expert_ffn(lambda k, zero: None)

        @pl.when(has_next)
        def _():
            def issue(r, c):
                gather_row(fnext_ref, r, 1 - cur).start()
                return c
            lax.fori_loop(0, tm, issue, 0)

        @pl.when(has_prev)
        def _():
            def issue(r, c):
                scatter_row(fprev_ref, r, 1 - cur).start()
                return c
            lax.fori_loop(0, tm, issue, 0)

    @pl.when((i == nt - 1) & (nt - 2 < used_tiles))
    def _():
        scatter_all(1 - cur).wait()


def _moe_experts(h_slabs, fmap, tile_expert, n_used, tile_half, wg, wu, wd, n_tiles, y_rows):
    f = wg.shape[2]
    tm = TM_MOE
    fblk = lambda imap: pl.BlockSpec((1, 1, tm), imap, memory_space=pltpu.SMEM)
    hbm = pl.BlockSpec(memory_space=pl.ANY)
    return pl.pallas_call(
        _moe_kernel,
        grid_spec=pltpu.PrefetchScalarGridSpec(
            num_scalar_prefetch=3,
            grid=(n_tiles,),
            in_specs=[
                fblk(lambda i, te, nu, hf: (jnp.minimum(i + 1, n_tiles - 1), 0, 0)),
                fblk(lambda i, te, nu, hf: (jnp.maximum(i - 1, 0), 0, 0)),
                fblk(lambda i, te, nu, hf: (0, 0, 0)),
                hbm, hbm, hbm, hbm,
            ],
            out_specs=hbm,
            scratch_shapes=[pltpu.VMEM((D, f), BF16), pltpu.VMEM((D, f), BF16), pltpu.VMEM((f, D), BF16),
                            pltpu.VMEM((W_RING_IN, W_ROWS_IN, f), F32), pltpu.VMEM((W_RING_OUT, W_ROWS_OUT, D), F32),
                            pltpu.VMEM((2, tm * SLAB, LANES), F32), pltpu.VMEM((tm, D), BF16),
                            pltpu.VMEM((tm, f // MOE_SPLIT), BF16), pltpu.VMEM((tm, D), F32),
                            pltpu.VMEM((2, tm * PACK, LANES), jnp.uint32),
                            pltpu.SemaphoreType.DMA((2,)), pltpu.SemaphoreType.DMA((2,)),
                            pltpu.SemaphoreType.DMA((W_RING_IN + W_RING_OUT,))],
        ),
        out_shape=jax.ShapeDtypeStruct((y_rows * PACK, LANES), jnp.uint32),
        compiler_params=_cparams(("arbitrary",), MOE_VMEM_LIMIT),
        name="moe_experts",
    )(tile_expert, n_used, tile_half, fmap, fmap, fmap, h_slabs, wg, wu, wd)


def _combine_kernel(y1_ref, y2_ref, x_ref, w_ref, mod_ref, gain_ref, o_ref):
    w = w_ref[...]
    rows = x_ref.shape[0]
    moe = w[:, 0:1] * _unpack_rows(y1_ref, rows) + w[:, 1:2] * _unpack_rows(y2_ref, rows)
    mod = mod_ref[0, 0]
    x4 = x_ref[...] + mod[5:6] * moe
    ms = jnp.mean(x4 * x4, axis=-1, keepdims=True)
    o_ref[...] = x4 * lax.rsqrt(ms + EPS) * gain_ref[...]


def _combine(y, x3_2d, wts, mod, final_gain, seq):
    n = x3_2d.shape[0]
    tc = TC_COMB
    nt = n // tc
    per_b = seq // tc
    return pl.pallas_call(
        _combine_kernel,
        grid=(nt,),
        in_specs=[
            pl.BlockSpec((tc * PACK, LANES), lambda i: (i, 0)),
            pl.BlockSpec((tc * PACK, LANES), lambda i: (i + nt, 0)),
            pl.BlockSpec((tc, D), lambda i: (i, 0)),
            pl.BlockSpec((tc, 2), lambda i: (i, 0)),
            pl.BlockSpec((1, 1, 6, D), lambda i: (1, i // per_b, 0, 0)),
            pl.BlockSpec((1, D), lambda i: (0, 0)),
        ],
        out_specs=pl.BlockSpec((tc, D), lambda i: (i, 0)),
        out_shape=jax.ShapeDtypeStruct((n, D), F32),
        compiler_params=_cparams(("arbitrary",)),
        name="moe_combine",
    )(y, y, x3_2d, wts, mod, final_gain.reshape(1, D))


def _rope_tables(seq):
    rows = seq // GRID_W
    row_pos = jnp.repeat(jnp.arange(rows, dtype=F32), GRID_W)
    col_pos = jnp.tile(jnp.arange(GRID_W, dtype=F32), rows)
    axis_dim = HEAD_DIM // 2
    inv_freq = ROPE_BASE ** (-jnp.arange(0, axis_dim, 2, dtype=F32) / axis_dim)
    ar = row_pos[:, None] * inv_freq
    ac = col_pos[:, None] * inv_freq
    cos64 = jnp.concatenate([jnp.cos(ar), jnp.cos(ar), jnp.cos(ac), jnp.cos(ac)], axis=1)
    sin64 = jnp.concatenate([-jnp.sin(ar), jnp.sin(ar), -jnp.sin(ac), jnp.sin(ac)], axis=1)
    return jnp.tile(cos64, (1, 2)), jnp.tile(sin64, (1, 2))


def _band_matrices():
    r = np.arange(BLOCK)[:, None]
    c = np.arange(BLOCK + 2 * POOL_HALO)[None, :] - POOL_HALO
    mats = []
    for w in POOL_SIZES:
        lo = -(w // 2)
        hi = w - 1 - w // 2
        mats.append(((c >= r + lo) & (c <= r + hi)).astype(np.float32))
    return jnp.asarray(np.stack(mats), dtype=BF16)


def kernel(x, c, ctx, c_ctx, w_ada, b_ada, w_in, attn_sink, gm_gain, gm_w_s, gm_b_s, w_out,
           ffn_w_gate, ffn_w_up, ffn_w_down, pool_w, pool_scale, router_w,
           moe_w_gate, moe_w_up, moe_w_down, final_gain):
    b, s, _ = x.shape
    n = b * s
    assert w_ada.shape[0] == 2 and w_in.shape[0] == 1 and pool_w.shape[0] == 1
    assert s % TM_IN == 0 and s % TM_POOL == 0 and s % TM_FFN == 0 and b <= 4
    assert n & (n - 1) == 0

    cvec = jnp.concatenate([c, c_ctx[None, :], jnp.zeros((8 - b - 1, D), F32)], axis=0)
    mod = _ada_mod(cvec, w_ada, b_ada)

    cos_t, sin_t = _rope_tables(s)
    q, k, ksw, v, vsw, u, vg = _in_proj(x, mod, w_in[0], gm_gain[0], cos_t, sin_t)
    kx, ksx, vx, vsx = _ctx_kv(ctx, mod, w_in[0])
    wcat = gm_w_s[0].reshape(4, 2, BLOCK, BLOCK).transpose(0, 2, 1, 3).reshape(4, BLOCK, 2 * BLOCK).astype(BF16)
    x1 = _attn_mixer(x, mod, attn_sink[0], q, k, ksw, v, vsw, kx, ksx, vx, vsx, u, vg,
                     wcat, gm_b_s[0].T, w_out[0])
    x2 = _dense_ffn(x1.reshape(n, D), mod, ffn_w_gate[0].astype(BF16), ffn_w_up[0].astype(BF16),
                    ffn_w_down[0].astype(BF16), s)

    wr = jnp.pad(router_w[0], ((0, 0), (0, LANES - N_EXPERTS)))
    wr_hi = wr.astype(BF16)
    tri = jnp.asarray(np.tril(np.ones((TM_POOL, TM_POOL), np.float32), -1), dtype=BF16)
    x3, h2, route, counts = _pool_route(x2.reshape(b, s, D), mod, _band_matrices(), pool_w[0].astype(BF16),
                                        pool_scale[0], wr_hi, tri)

    tm = TM_MOE
    n_tiles = (2 * n) // tm + N_EXPERTS
    cnt = counts[0, :N_EXPERTS].astype(jnp.int32)
    tiles_e = (cnt + tm - 1) // tm
    tile_end = jnp.cumsum(tiles_e)
    off = (tile_end - tiles_e) * tm
    n_used = tile_end[-1]
    tix = jnp.arange(n_tiles, dtype=jnp.int32)
    te = jnp.minimum(jnp.sum(tix[:, None] >= tile_end[None, :], axis=1), N_EXPERTS - 1).astype(jnp.int32)
    te_last = te[jnp.maximum(n_used - 1, 0)]
    tile_expert = jnp.where(tix < n_used, te, te_last)
    rows_left = cnt[te] - (tix - (tile_end - tiles_e)[te]) * tm
    tile_half = ((tix < n_used) & (rows_left <= tm // 2)).astype(jnp.int32)
    e1 = route[0].astype(jnp.int32)
    e2 = route[1].astype(jnp.int32)
    pos1 = off[e1] + route[4].astype(jnp.int32)
    pos2 = off[e2] + route[5].astype(jnp.int32)
    n_slots = n_tiles * tm
    pad_lo = jnp.concatenate([off + cnt, (n_used * tm).reshape(1)]).astype(jnp.int32)
    pad_hi = jnp.concatenate([off + tiles_e * tm, jnp.full((1,), n_slots, jnp.int32)]).astype(jnp.int32)
    fmap = _slot_map(jnp.concatenate([pos1, pos2]), pad_lo, pad_hi, n_slots)
    n_used_arr = n_used.reshape(1).astype(jnp.int32)

    y = _moe_experts(h2, fmap.reshape(n_tiles, 1, tm), tile_expert, n_used_arr, tile_half,
                     moe_w_gate[0], moe_w_up[0], moe_w_down[0], n_tiles, 2 * n + 2 * tm)
    out = _combine(y, x3.reshape(n, D), route[2:4].T, mod, final_gain, s)
    return out.reshape(b, s, D)
```

```python
import functools

import numpy as np
import jax
import jax.numpy as jnp
from jax import lax
from jax.experimental import pallas as pl
from jax.experimental.pallas import tpu as pltpu

F32 = jnp.float32
BF16 = jnp.bfloat16

D = 1024
GRID_W = 64
EPS = 1e-6
NEG_INF = -1e30
HEAD_DIM = 64
N_Q_HEADS = 8
BLOCK = 128
ATT_W = 512
KV_W = 128
GM_W = 512
IN_W = 1792
POOL_SIZES = (2, 4, 8, 16)
POOL_GD = 256
POOL_HALO = 16
N_EXPERTS = 8
ROPE_BASE = 10000.0
LANES = 128
SLAB = D // LANES
PACK = SLAB // 2
SQRT_2_OVER_PI = 0.7978845608028654

TM_IN = 1024
TM_FFN = 1024
TM_POOL = 1024
TM_MOE = 512
MOE_SPLIT = 2
MOE_QUARTERS = 4
W_CHUNK = 256
W_ROWS_IN = 128
W_ROWS_OUT = 512
W_RING_IN = 12
W_RING_OUT = 2
MOE_VMEM_LIMIT = 60 * 1024 * 1024
TC_COMB = 1024
Q_BLOCKS = 8
VMEM_LIMIT = 56 * 1024 * 1024


def _cparams(sem, vmem=None):
    return pltpu.CompilerParams(dimension_semantics=sem, vmem_limit_bytes=vmem)


def _modulate(xf, shift, scale):
    ms = jnp.mean(xf * xf, axis=-1, keepdims=True)
    return xf * lax.rsqrt(ms + EPS) * (1.0 + scale) + shift


def _sigmoid(z):
    return 1.0 / (1.0 + jnp.exp(-z))


def _rows_to_slabs(val, slab_ref):
    rows = val.shape[0]
    for cix in range(SLAB):
        slab_ref[pl.ds(cix, rows, stride=SLAB), :] = val[:, cix * LANES:(cix + 1) * LANES]


def _slabs_to_rows(slab_ref, rows):
    return jnp.concatenate([slab_ref[pl.ds(cix, rows, stride=SLAB), :] for cix in range(SLAB)], axis=1)


def _pack_rows(val, pack_ref):
    rows = val.shape[0]
    bits = lambda v: lax.bitcast_convert_type(v.astype(BF16).astype(F32), jnp.uint32)
    for cix in range(PACK):
        hi = bits(val[:, cix * LANES:(cix + 1) * LANES])
        lo = bits(val[:, D // 2 + cix * LANES:D // 2 + (cix + 1) * LANES])
        pack_ref[pl.ds(cix, rows, stride=PACK), :] = hi | (lo >> 16)


def _unpack_rows(pack_ref, rows):
    words = [pack_ref[pl.ds(cix, rows, stride=PACK), :] for cix in range(PACK)]
    his = [lax.bitcast_convert_type(w & jnp.uint32(0xFFFF0000), F32) for w in words]
    los = [lax.bitcast_convert_type(w << 16, F32) for w in words]
    return jnp.concatenate(his + los, axis=1)


def _ada_kernel(c_ref, w_ref, b_ref, o_ref):
    c = c_ref[...]
    s = c * _sigmoid(c)
    o_ref[0] = jnp.dot(s.astype(BF16), w_ref[0].astype(BF16), preferred_element_type=F32) + b_ref[0]


def _ada_mod(cvec, w_ada, b_ada):
    depth, _, n6 = w_ada.shape
    tn = 1536
    out = pl.pallas_call(
        _ada_kernel,
        grid=(depth, n6 // tn),
        in_specs=[
            pl.BlockSpec((8, D), lambda l, j: (0, 0)),
            pl.BlockSpec((1, D, tn), lambda l, j: (l, 0, j)),
            pl.BlockSpec((1, 1, tn), lambda l, j: (l, 0, j)),
        ],
        out_specs=pl.BlockSpec((1, 8, tn), lambda l, j: (l, 0, j)),
        out_shape=jax.ShapeDtypeStruct((depth, 8, n6), F32),
        compiler_params=_cparams(("arbitrary", "arbitrary")),
        name="ada_mod",
    )(cvec, w_ada, b_ada.reshape(depth, 1, n6))
    return out.reshape(depth, 8, 6, D)


def _rope(t, cs, sn, first_half):
    fwd = pltpu.roll(t, LANES - 16, axis=1)
    bwd = pltpu.roll(t, 16, axis=1)
    return t * cs + jnp.where(first_half, fwd, bwd) * sn


def _inproj_kernel(x_ref, mod_ref, w_ref, gain_ref, cos_ref, sin_ref,
                   q_ref, k_ref, ksw_ref, v_ref, vsw_ref, u_ref, vg_ref, w_bf):
    @pl.when((pl.program_id(0) == 0) & (pl.program_id(1) == 0))
    def _():
        w_bf[...] = w_ref[...].astype(BF16)

    mod = mod_ref[0, 0]
    h = _modulate(x_ref[0], mod[0:1], mod[1:2]).astype(BF16)
    proj = jnp.dot(h, w_bf[...], preferred_element_type=F32)
    cs = cos_ref[...]
    sn = sin_ref[...]
    lane = lax.broadcasted_iota(jnp.int32, cs.shape, 1)
    first_half = (lane & 16) == 0
    for cix in range(ATT_W // LANES):
        t = proj[:, cix * LANES:(cix + 1) * LANES]
        q_ref[0, :, cix * LANES:(cix + 1) * LANES] = (
            _rope(t, cs, sn, first_half) * (HEAD_DIM ** -0.5)).astype(BF16)
    kr = _rope(proj[:, ATT_W:ATT_W + KV_W], cs, sn, first_half)
    k_ref[0] = kr.astype(BF16)
    ksw_ref[0] = pltpu.roll(kr, HEAD_DIM, axis=1).astype(BF16)
    vv = proj[:, ATT_W + KV_W:ATT_W + 2 * KV_W]
    v_ref[0] = vv.astype(BF16)
    vsw_ref[0] = pltpu.roll(vv, HEAD_DIM, axis=1).astype(BF16)
    z = proj[:, ATT_W + 2 * KV_W:]
    g = z * (0.5 * (1.0 + jnp.tanh(SQRT_2_OVER_PI * (z + 0.044715 * (z * z * z)))))
    u_ref[0] = g[:, :GM_W].astype(BF16)
    vg = g[:, GM_W:]
    ms = jnp.mean(vg * vg, axis=-1, keepdims=True)
    vg_ref[0] = (vg * lax.rsqrt(ms + EPS) * gain_ref[...]).astype(BF16)


def _in_proj(x, mod, w_in, gm_gain, cos_t, sin_t):
    b, s, _ = x.shape
    tm = TM_IN
    row = lambda w: pl.BlockSpec((1, tm, w), lambda bi, i: (bi, i, 0))
    outs = pl.pallas_call(
        _inproj_kernel,
        grid=(b, s // tm),
        in_specs=[
            row(D),
            pl.BlockSpec((1, 1, 6, D), lambda bi, i: (0, bi, 0, 0)),
            pl.BlockSpec((D, IN_W), lambda bi, i: (0, 0), pipeline_mode=pl.Buffered(1)),
            pl.BlockSpec((1, GM_W), lambda bi, i: (0, 0)),
            pl.BlockSpec((tm, LANES), lambda bi, i: (i, 0)),
            pl.BlockSpec((tm, LANES), lambda bi, i: (i, 0)),
        ],
        out_specs=[row(ATT_W), row(KV_W), row(KV_W), row(KV_W), row(KV_W), row(GM_W), row(GM_W)],
        out_shape=[jax.ShapeDtypeStruct((b, s, w), BF16)
                   for w in (ATT_W, KV_W, KV_W, KV_W, KV_W, GM_W, GM_W)],
        scratch_shapes=[pltpu.VMEM((D, IN_W), BF16)],
        compiler_params=_cparams(("arbitrary", "arbitrary"), VMEM_LIMIT),
        name="in_proj",
    )(x, mod, w_in, gm_gain.reshape(1, GM_W), cos_t, sin_t)
    return outs


def _ctx_kernel(c_ref, mod_ref, w_ref, k_ref, ksw_ref, v_ref, vsw_ref):
    mod = mod_ref[0, 0]
    h = _modulate(c_ref[0], mod[0:1], mod[1:2]).astype(BF16)
    kv = jnp.dot(h, w_ref[...].astype(BF16), preferred_element_type=F32)
    kk = kv[:, :KV_W]
    vv = kv[:, KV_W:]
    k_ref[0] = kk.astype(BF16)
    ksw_ref[0] = pltpu.roll(kk, HEAD_DIM, axis=1).astype(BF16)
    v_ref[0] = vv.astype(BF16)
    vsw_ref[0] = pltpu.roll(vv, HEAD_DIM, axis=1).astype(BF16)


def _ctx_kv(ctx, mod, w_in):
    b, l, _ = ctx.shape
    spec = pl.BlockSpec((1, l, KV_W), lambda bi: (bi, 0, 0))
    return pl.pallas_call(
        _ctx_kernel,
        grid=(b,),
        in_specs=[
            pl.BlockSpec((1, l, D), lambda bi: (bi, 0, 0)),
            pl.BlockSpec((1, 1, 6, D), lambda bi: (0, b, 0, 0)),
            pl.BlockSpec((D, 2 * KV_W), lambda bi: (0, ATT_W // (2 * KV_W))),
        ],
        out_specs=[spec] * 4,
        out_shape=[jax.ShapeDtypeStruct((b, l, KV_W), BF16)] * 4,
        compiler_params=_cparams(("arbitrary",)),
        name="ctx_kv",
    )(ctx, mod, w_in)


def _attn_kernel(sink_ref, q_ref, kp_ref, kc_ref, kn_ref, ksp_ref, ksc_ref, ksn_ref,
                 vp_ref, vc_ref, vn_ref, vsp_ref, vsc_ref, vsn_ref,
                 kx_ref, ksx_ref, vx_ref, vsx_ref,
                 u_ref, vg_ref, wcat_ref, bs_ref, wout_ref, x_ref, mod_ref, o_ref, wout_bf):
    n = pl.program_id(1)

    @pl.when((pl.program_id(0) == 0) & (n == 0))
    def _():
        wout_bf[...] = wout_ref[...].astype(BF16)

    nblk = pl.num_programs(1) * Q_BLOCKS
    lane = lax.broadcasted_iota(jnp.int32, (1, LANES), 1)
    low = lane < HEAD_DIM
    zero = jnp.zeros((), BF16)

    def variants(a0, a1):
        return ((jnp.where(low, a0, zero), jnp.where(low, zero, a1)),
                (jnp.where(low, a1, zero), jnp.where(low, zero, a0)))

    cat = lambda refs: jnp.concatenate([r[0] for r in refs], axis=0)
    kb_var = variants(cat((kp_ref, kc_ref, kn_ref)), cat((ksp_ref, ksc_ref, ksn_ref)))
    vb_var = variants(cat((vp_ref, vc_ref, vn_ref)), cat((vsp_ref, vsc_ref, vsn_ref)))
    kx_var = variants(kx_ref[0], ksx_ref[0])
    vx_var = variants(vx_ref[0], vsx_ref[0])

    row = lax.broadcasted_iota(jnp.int32, (2 * BLOCK, BLOCK), 0) & (BLOCK - 1)
    col = lax.broadcasted_iota(jnp.int32, (2 * BLOCK, BLOCK), 1)
    top = lax.broadcasted_iota(jnp.int32, (2 * BLOCK, 1), 0) < BLOCK
    nt_dims = (((1,), (1,)), ((), ()))

    q = q_ref[0]
    att_blocks = [[None] * 4 for _ in range(Q_BLOCKS)]
    for kvh in range(2):
        qst = jnp.concatenate(
            [q[qb * BLOCK:(qb + 1) * BLOCK, pr * LANES:(pr + 1) * LANES]
             for qb in range(Q_BLOCKS) for pr in (2 * kvh, 2 * kvh + 1)], axis=0)
        accs = [None] * Q_BLOCKS
        for half in range(2):
            sk = jnp.where(top, sink_ref[4 * kvh + half], sink_ref[4 * kvh + 2 + half])
            s_ctx = lax.dot_general(qst, kx_var[kvh][half], nt_dims, preferred_element_type=F32)
            o_band, p_ctx, dens = [], [], []
            for qb in range(Q_BLOCKS):
                g = n * Q_BLOCKS + qb
                qrows = qst[qb * 2 * BLOCK:(qb + 1) * 2 * BLOCK]
                sb = lax.dot_general(qrows, kb_var[kvh][half][qb * BLOCK:(qb + 3) * BLOCK], nt_dims,
                                     preferred_element_type=F32)
                s0 = jnp.where((col >= row) & (g > 0), sb[:, :BLOCK], NEG_INF)
                s1 = sb[:, BLOCK:2 * BLOCK]
                s2 = jnp.where((col <= row) & (g < nblk - 1), sb[:, 2 * BLOCK:], NEG_INF)
                sc = s_ctx[qb * 2 * BLOCK:(qb + 1) * 2 * BLOCK]
                ctx_blocks = [sc[:, cb * BLOCK:(cb + 1) * BLOCK] for cb in range(sc.shape[1] // BLOCK)]
                m = functools.reduce(jnp.maximum, [s0, s1, s2] + ctx_blocks)
                m = jnp.maximum(jnp.max(m, axis=-1, keepdims=True), sk)
                p0, p1, p2, pc = (jnp.exp(t - m) for t in (s0, s1, s2, sc))
                psum = functools.reduce(
                    jnp.add, [p0, p1, p2] + [pc[:, cb * BLOCK:(cb + 1) * BLOCK] for cb in range(len(ctx_blocks))])
                den = jnp.sum(psum, axis=-1, keepdims=True) + jnp.exp(sk - m)
                pb = jnp.concatenate([p0, p1, p2], axis=1).astype(BF16)
                o_band.append(jnp.dot(pb, vb_var[kvh][half][qb * BLOCK:(qb + 3) * BLOCK],
                                      preferred_element_type=F32))
                p_ctx.append(pc.astype(BF16))
                dens.append(den)
            o_ctx = jnp.dot(jnp.concatenate(p_ctx, axis=0), vx_var[kvh][half], preferred_element_type=F32)
            for qb in range(Q_BLOCKS):
                o = (o_band[qb] + o_ctx[qb * 2 * BLOCK:(qb + 1) * 2 * BLOCK]) / dens[qb]
                accs[qb] = o if accs[qb] is None else accs[qb] + o
        for qb in range(Q_BLOCKS):
            att_blocks[qb][2 * kvh] = accs[qb][:BLOCK]
            att_blocks[qb][2 * kvh + 1] = accs[qb][BLOCK:]

    u = u_ref[0]
    vg = vg_ref[0]
    bs = bs_ref[...]
    gm_blocks = [[None] * 4 for _ in range(Q_BLOCKS)]
    for j in range(GM_W // LANES):
        chunks = [vg[c * BLOCK:(c + 1) * BLOCK, j * LANES:(j + 1) * LANES] for c in range(Q_BLOCKS)]
        rhs = jnp.concatenate(
            [jnp.concatenate([jnp.where(low, v, zero) for v in chunks], axis=1),
             jnp.concatenate([jnp.where(low, zero, v) for v in chunks], axis=1)], axis=0)
        mixed = jnp.dot(wcat_ref[j], rhs, preferred_element_type=F32)
        bias = jnp.where(low, bs[:, 2 * j:2 * j + 1], bs[:, 2 * j + 1:2 * j + 2])
        for c in range(Q_BLOCKS):
            gm_blocks[c][j] = (u[c * BLOCK:(c + 1) * BLOCK, j * LANES:(j + 1) * LANES].astype(F32)
                               * (mixed[:, c * LANES:(c + 1) * LANES] + bias))

    mix = jnp.concatenate([jnp.concatenate(att_blocks[c] + gm_blocks[c], axis=1) for c in range(Q_BLOCKS)],
                          axis=0).astype(BF16)
    y = jnp.dot(mix, wout_bf[...], preferred_element_type=F32)
    mod = mod_ref[0, 0]
    o_ref[0] = x_ref[0] + mod[2:3] * y


def _attn_mixer(x, mod, sink, q, k, ksw, v, vsw, kx, ksx, vx, vsx, u, vg, wcat_bf, bs_t, wout):
    b, s, _ = x.shape
    tq = Q_BLOCKS * BLOCK
    nb = s // BLOCK
    l = kx.shape[1]
    cur = lambda w: pl.BlockSpec((1, tq, w), lambda bi, n: (bi, n, 0))
    prv = lambda w: pl.BlockSpec((1, BLOCK, w), lambda bi, n: (bi, jnp.maximum(n * Q_BLOCKS - 1, 0), 0))
    nxt = lambda w: pl.BlockSpec((1, BLOCK, w), lambda bi, n: (bi, jnp.minimum((n + 1) * Q_BLOCKS, nb - 1), 0))
    cx = pl.BlockSpec((1, l, KV_W), lambda bi, n: (bi, 0, 0))
    return pl.pallas_call(
        _attn_kernel,
        grid=(b, s // tq),
        in_specs=[
            pl.BlockSpec(memory_space=pltpu.SMEM),
            cur(ATT_W),
            prv(KV_W), cur(KV_W), nxt(KV_W), prv(KV_W), cur(KV_W), nxt(KV_W),
            prv(KV_W), cur(KV_W), nxt(KV_W), prv(KV_W), cur(KV_W), nxt(KV_W),
            cx, cx, cx, cx,
            cur(GM_W), cur(GM_W),
            pl.BlockSpec((4, BLOCK, 2 * BLOCK), lambda bi, n: (0, 0, 0)),
            pl.BlockSpec((BLOCK, 8), lambda bi, n: (0, 0)),
            pl.BlockSpec((D, D), lambda bi, n: (0, 0), pipeline_mode=pl.Buffered(1)),
            cur(D),
            pl.BlockSpec((1, 1, 6, D), lambda bi, n: (0, bi, 0, 0)),
        ],
        out_specs=cur(D),
        out_shape=jax.ShapeDtypeStruct((b, s, D), F32),
        scratch_shapes=[pltpu.VMEM((D, D), BF16)],
        compiler_params=_cparams(("arbitrary", "arbitrary"), VMEM_LIMIT),
        name="attn_gmlp_out",
    )(sink, q, k, k, k, ksw, ksw, ksw, v, v, v, vsw, vsw, vsw, kx, ksx, vx, vsx,
      u, vg, wcat_bf, bs_t, wout, x, mod)


def _ffn_kernel(x_ref, mod_ref, wg_ref, wu_ref, wd_ref, o_ref, act):
    mod = mod_ref[0, 0]
    xf = x_ref[...]
    h = _modulate(xf, mod[3:4], mod[4:5]).astype(BF16)
    for c in range(wg_ref.shape[1] // W_CHUNK):
        cs = slice(c * W_CHUNK, (c + 1) * W_CHUNK)
        g = jnp.dot(h, wg_ref[:, cs], preferred_element_type=F32)
        up = jnp.dot(h, wu_ref[:, cs], preferred_element_type=F32)
        act[:, cs] = (g * _sigmoid(g) * up).astype(BF16)
    o_ref[...] = xf + mod[5:6] * jnp.dot(act[...], wd_ref[...], preferred_element_type=F32)


def _dense_ffn(x2d, mod, wg, wu, wd, seq):
    n = x2d.shape[0]
    f = wg.shape[1]
    tm = TM_FFN
    per_b = seq // tm
    resident = lambda shp: pl.BlockSpec(shp, lambda i: (0, 0), pipeline_mode=pl.Buffered(1))
    return pl.pallas_call(
        _ffn_kernel,
        grid=(n // tm,),
        in_specs=[
            pl.BlockSpec((tm, D), lambda i: (i, 0)),
            pl.BlockSpec((1, 1, 6, D), lambda i: (0, i // per_b, 0, 0)),
            resident((D, f)), resident((D, f)), resident((f, D)),
        ],
        out_specs=pl.BlockSpec((tm, D), lambda i: (i, 0)),
        out_shape=jax.ShapeDtypeStruct((n, D), F32),
        scratch_shapes=[pltpu.VMEM((tm, f), BF16)],
        compiler_params=_cparams(("arbitrary",), VMEM_LIMIT),
        name="dense_ffn",
    )(x2d, mod, wg, wu, wd)


def _pool_route_kernel(x_ref, xp_ref, xn_ref, mod_ref, band_ref, pw_ref, psc_ref, wr_hi_ref,
                       tri_ref, x3_ref, h2_ref, route_ref, cnt_ref, hext, carry):
    bi = pl.program_id(0)
    i = pl.program_id(1)
    ni = pl.num_programs(1)
    tm = x_ref.shape[1]
    seq = tm * ni
    mod = mod_ref[0, 0]

    @pl.when((bi == 0) & (i == 0))
    def _():
        carry[...] = jnp.zeros_like(carry)

    xf = x_ref[0]
    hp = _modulate(xp_ref[0], mod[0:1], mod[1:2])
    hn = _modulate(xn_ref[0], mod[0:1], mod[1:2])
    hext[0:POOL_HALO] = jnp.where(i > 0, hp, 0.0).astype(BF16)
    h_main = _modulate(xf, mod[0:1], mod[1:2])
    hext[POOL_HALO:POOL_HALO + tm] = h_main.astype(BF16)
    hext[POOL_HALO + tm:] = jnp.where(i < ni - 1, hn, 0.0).astype(BF16)

    t_local = lax.broadcasted_iota(jnp.int32, (BLOCK, 1), 0)
    ys = []
    for gi, w in enumerate(POOL_SIZES):
        lo_off = -(w // 2)
        hi_off = w - 1 - w // 2
        cols = slice(gi * POOL_GD, (gi + 1) * POOL_GD)
        outs = []
        for sb in range(tm // BLOCK):
            r0 = sb * BLOCK
            win = jnp.dot(band_ref[gi], hext[r0:r0 + BLOCK + 2 * POOL_HALO, cols],
                          preferred_element_type=F32)
            t = i * tm + r0 + t_local
            cnt = (jnp.minimum(t + hi_off, seq - 1) - jnp.maximum(t + lo_off, 0) + 1).astype(F32)
            diff = win / cnt - h_main[r0:r0 + BLOCK, cols]
            outs.append(diff.astype(BF16))
        dg = jnp.concatenate(outs, axis=0)
        ys.append(jnp.dot(dg, pw_ref[gi], preferred_element_type=F32))
    y = jnp.concatenate(ys, axis=1) * psc_ref[...]
    x3 = xf + mod[2:3] * y
    x3_ref[0] = x3

    h2 = _modulate(x3, mod[3:4], mod[4:5])
    _rows_to_slabs(h2, h2_ref)
    logits = jnp.dot(h2.astype(BF16), wr_hi_ref[...], preferred_element_type=F32)
    lane = lax.broadcasted_iota(jnp.int32, (tm, LANES), 1)
    lane_f = lane.astype(F32)
    neg = -jnp.inf
    lg = jnp.where(lane < N_EXPERTS, logits, neg)
    m1 = jnp.max(lg, axis=-1, keepdims=True)
    i1 = jnp.min(jnp.where(lg == m1, lane_f, float(LANES)), axis=-1, keepdims=True)
    oh1 = lane_f == i1
    lg2 = jnp.where(oh1, neg, lg)
    m2 = jnp.max(lg2, axis=-1, keepdims=True)
    i2 = jnp.min(jnp.where(lg2 == m2, lane_f, float(LANES)), axis=-1, keepdims=True)
    oh2 = lane_f == i2
    e = jnp.exp(m2 - m1)
    w1 = 1.0 / (1.0 + e)
    w2 = e / (1.0 + e)
    oh = jnp.where(oh1 | oh2, 1.0, 0.0)
    before = jnp.dot(tri_ref[...], oh.astype(BF16), preferred_element_type=F32) + carry[...]
    r1 = jnp.sum(jnp.where(oh1, before, 0.0), axis=-1, keepdims=True)
    r2 = jnp.sum(jnp.where(oh2, before, 0.0), axis=-1, keepdims=True)
    carry[...] = carry[...] + jnp.sum(oh, axis=0, keepdims=True)
    cnt_ref[...] = carry[...]
    info = jnp.where(lane == 0, i1, jnp.where(lane == 1, i2, jnp.where(lane == 2, w1, jnp.where(
        lane == 3, w2, jnp.where(lane == 4, r1, jnp.where(lane == 5, r2, 0.0))))))
    route_ref[...] = info.T[0:8, :]


def _pool_route(x, mod, band, pw_bf, pool_scale, wr_hi, tri):
    b, s, _ = x.shape
    tm = TM_POOL
    ni = s // tm
    hb = tm // POOL_HALO
    row = pl.BlockSpec((1, tm, D), lambda bi, i: (bi, i, 0))
    const2 = lambda shp: pl.BlockSpec(shp, lambda bi, i: (0,) * len(shp))
    return pl.pallas_call(
        _pool_route_kernel,
        grid=(b, ni),
        in_specs=[
            row,
            pl.BlockSpec((1, POOL_HALO, D), lambda bi, i: (bi, jnp.maximum(i * hb - 1, 0), 0)),
            pl.BlockSpec((1, POOL_HALO, D), lambda bi, i: (bi, jnp.minimum((i + 1) * hb, s // POOL_HALO - 1), 0)),
            pl.BlockSpec((1, 1, 6, D), lambda bi, i: (1, bi, 0, 0)),
            const2(band.shape), const2(pw_bf.shape), const2((1, D)),
            const2(wr_hi.shape), const2(tri.shape),
        ],
        out_specs=[row,
                   pl.BlockSpec((tm * SLAB, LANES), lambda bi, i: (bi * ni + i, 0)),
                   pl.BlockSpec((8, tm), lambda bi, i: (0, bi * ni + i)),
                   pl.BlockSpec((1, LANES), lambda bi, i: (0, 0))],
        out_shape=[jax.ShapeDtypeStruct((b, s, D), F32), jax.ShapeDtypeStruct((b * s * SLAB, LANES), F32),
                   jax.ShapeDtypeStruct((8, b * s), F32), jax.ShapeDtypeStruct((1, LANES), F32)],
        scratch_shapes=[pltpu.VMEM((tm + 2 * POOL_HALO, D), BF16), pltpu.VMEM((1, LANES), F32)],
        compiler_params=_cparams(("arbitrary", "arbitrary"), VMEM_LIMIT),
        name="pool_route",
    )(x, x, x, mod, band, pw_bf, pool_scale.reshape(1, D), wr_hi, tri)


def _slotmap_kernel(pos_ref, lo_ref, hi_ref, o_ref):
    n_pairs = pos_ref.shape[0]
    spare_mask = 2 * TM_MOE - 1
    for e in range(lo_ref.shape[0]):
        def fill(p, c):
            o_ref[p] = n_pairs + (p & spare_mask)
            return c
        lax.fori_loop(lo_ref[e], hi_ref[e], fill, 0)

    def place(f, c):
        o_ref[pos_ref[f]] = f
        return c
    lax.fori_loop(0, n_pairs, place, 0, unroll=32)


def _slot_map(pos_flat, lo, hi, n_slots):
    smem = pl.BlockSpec(memory_space=pltpu.SMEM)
    return pl.pallas_call(
        _slotmap_kernel,
        in_specs=[smem, smem, smem],
        out_specs=smem,
        out_shape=jax.ShapeDtypeStruct((n_slots,), jnp.int32),
        name="moe_slot_map",
    )(pos_flat, lo, hi)


def _moe_kernel(te_ref, nused_ref, quart_ref, fnext_ref, fprev_ref, f0_ref, h_hbm, wg_hbm, wu_hbm, wd_hbm,
                y_hbm, wg_res, wu_res, wd_res, stg_in, stg_out, xbuf, xb, act, acc, stage, gsem, ssem, wsem):
    i = pl.program_id(0)
    nt = pl.num_programs(0)
    used_tiles = nused_ref[0]
    tm = xb.shape[0]
    f_dim = wg_res.shape[1]
    tok_mask = h_hbm.shape[0] // SLAB - 1
    tile_rows = tm * SLAB
    out_rows = tm * PACK

    def slab(ix, width=SLAB):
        return pl.ds(pl.multiple_of(ix * width, width), width)

    def gather_row(fref, r, slot, zero=0):
        tok = fref[0, 0, r + zero] & tok_mask
        return pltpu.make_async_copy(h_hbm.at[slab(tok)], xbuf.at[slot, slab(r)], gsem.at[slot])

    def scatter_row(fref, r, slot, zero=0):
        return pltpu.make_async_copy(stage.at[slot, slab(r, PACK)], y_hbm.at[slab(fref[0, 0, r + zero], PACK)],
                                     ssem.at[slot])

    def gather_all(slot):
        return pltpu.make_async_copy(h_hbm.at[pl.ds(0, tile_rows)], xbuf.at[slot], gsem.at[slot])

    def scatter_all(slot):
        return pltpu.make_async_copy(stage.at[slot], y_hbm.at[pl.ds(0, out_rows)], ssem.at[slot])

    cur = i % 2
    used = i < used_tiles

    @pl.when(i == 0)
    def _():
        stage[...] = jnp.zeros_like(stage)
        spare = y_hbm.shape[0] - 2 * out_rows
        fills = [pltpu.make_async_copy(stage.at[sl], y_hbm.at[pl.ds(spare + sl * out_rows, out_rows)],
                                       ssem.at[sl])
                 for sl in range(2)]
        for cp in fills:
            cp.start()
        for cp in fills:
            cp.wait()

    @pl.when((i == 0) & used)
    def _():
        def prime(r, c):
            gather_row(f0_ref, r, 0).start()
            return c
        lax.fori_loop(0, tm, prime, 0)

    expert = te_ref[i]
    new_expert = used & ((i == 0) | (expert != te_ref[jnp.maximum(i - 1, 0)]))

    has_next = i + 1 < used_tiles
    has_prev = (i >= 1) & (i - 1 < used_tiles)
    steady = (i >= 1) & has_next
    quarters = quart_ref[i]

    @pl.when(used)
    def _():
        gather_all(cur).wait()

    @pl.when((i >= 2) & (i - 2 < used_tiles))
    def _():
        scatter_all(cur).wait()

    half = f_dim // MOE_SPLIT
    n_piece = half // W_CHUNK
    n_groups = MOE_SPLIT * (n_piece + 1)

    def tick(v):
        bits = jnp.max(lax.bitcast_convert_type(v[0:SLAB, 0:LANES], jnp.int32))
        return lax.shift_right_logical(lax.shift_right_logical(bits, 16), 16)

    def expert_ffn(issue_group, rows=tm):
        for cix in range(SLAB):
            xb[0:rows, cix * LANES:(cix + 1) * LANES] = (
                xbuf.at[cur][pl.ds(cix, rows, stride=SLAB), :].astype(BF16))
        xv = xb[0:rows]
        out = None
        issue_group(0, 0)
        k = 1
        for hf in range(MOE_SPLIT):
            for c in range(n_piece):
                cs = slice(hf * half + c * W_CHUNK, hf * half + (c + 1) * W_CHUNK)
                g = jnp.dot(xv, wg_res[:, cs], preferred_element_type=F32)
                up = jnp.dot(xv, wu_res[:, cs], preferred_element_type=F32)
                act[0:rows, c * W_CHUNK:(c + 1) * W_CHUNK] = (g * _sigmoid(g) * up).astype(BF16)
                issue_group(k, tick(g))
                k += 1
            part = jnp.dot(act[0:rows], wd_res[hf * half:(hf + 1) * half, :], preferred_element_type=F32)
            if hf < MOE_SPLIT - 1:
                acc[0:rows] = part if out is None else acc[0:rows] + part
                out = acc
                issue_group(k, tick(part))
                k += 1
            else:
                _pack_rows(part if out is None else acc[0:rows] + part, stage.at[cur])

    def chunk_plan():
        n_in = D // W_ROWS_IN
        plan = [(w, res, stg_in, W_RING_IN, 0, W_ROWS_IN, c, wi * n_in + c)
                for wi, (w, res) in enumerate(((wg_hbm, wg_res), (wu_hbm, wu_res))) for c in range(n_in)]
        return plan + [(wd_hbm, wd_res, stg_out, W_RING_OUT, W_RING_IN, W_ROWS_OUT, c, c)
                       for c in range(f_dim // W_ROWS_OUT)]

    def chunk_copy(entry, ex):
        w_hbm, _, stg, ring, sem0, rows, c, j = entry
        return pltpu.make_async_copy(w_hbm.at[ex, pl.ds(c * rows, rows), :], stg.at[j % ring],
                                     wsem.at[sem0 + j % ring])

    def start_first_chunks(ex):
        for entry in chunk_plan():
            if entry[7] < entry[3]:
                chunk_copy(entry, ex).start()

    @pl.when(new_expert)
    def _():
        plan = chunk_plan()

        @pl.when(i == 0)
        def _():
            start_first_chunks(expert)

        for k, entry in enumerate(plan):
            _, res, stg, ring, _, rows, c, j = entry
            chunk_copy(entry, expert).wait()
            res[c * rows:(c + 1) * rows, :] = stg[j % ring].astype(BF16)
            later = [e for e in plan[k + 1:] if e[2] is stg and e[7] == j + ring]
            if later:
                chunk_copy(later[0], expert).start()

    next_expert = te_ref[jnp.minimum(i + 1, nt - 1)]

    @pl.when(has_next & (next_expert != expert))
    def _():
        start_first_chunks(next_expert)

    def issue_group(k, zero):
        for r in range(k * tm // n_groups, (k + 1) * tm // n_groups):
            gather_row(fnext_ref, r, 1 - cur, zero).start()
            scatter_row(fprev_ref, r, 1 - cur, zero).start()

    for nq in range(1, MOE_QUARTERS + 1):
        @pl.when(steady & (quarters == nq))
        def _(nq=nq):
            expert_ffn(issue_group, nq * tm // MOE_QUARTERS)

    @pl.when(jnp.logical_not(steady))
    def _():
        @pl.when(used)
        def _():
            expert_ffn(lambda k, zero: None)

        @pl.when(has_next)
        def _():
            def issue(r, c):
                gather_row(fnext_ref, r, 1 - cur).start()
                return c
            lax.fori_loop(0, tm, issue, 0)

        @pl.when(has_prev)
        def _():
            def issue(r, c):
                scatter_row(fprev_ref, r, 1 - cur).start()
                return c
            lax.fori_loop(0, tm, issue, 0)

    @pl.when((i == nt - 1) & (nt - 2 < used_tiles))
    def _():
        scatter_all(1 - cur).wait()


def _moe_experts(h_slabs, fmap, tile_expert, n_used, tile_quarters, wg, wu, wd, n_tiles, y_rows):
    f = wg.shape[2]
    tm = TM_MOE
    fblk = lambda imap: pl.BlockSpec((1, 1, tm), imap, memory_space=pltpu.SMEM)
    hbm = pl.BlockSpec(memory_space=pl.ANY)
    return pl.pallas_call(
        _moe_kernel,
        grid_spec=pltpu.PrefetchScalarGridSpec(
            num_scalar_prefetch=3,
            grid=(n_tiles,),
            in_specs=[
                fblk(lambda i, te, nu, hf: (jnp.minimum(i + 1, n_tiles - 1), 0, 0)),
                fblk(lambda i, te, nu, hf: (jnp.maximum(i - 1, 0), 0, 0)),
                fblk(lambda i, te, nu, hf: (0, 0, 0)),
                hbm, hbm, hbm, hbm,
            ],
            out_specs=hbm,
            scratch_shapes=[pltpu.VMEM((D, f), BF16), pltpu.VMEM((D, f), BF16), pltpu.VMEM((f, D), BF16),
                            pltpu.VMEM((W_RING_IN, W_ROWS_IN, f), F32), pltpu.VMEM((W_RING_OUT, W_ROWS_OUT, D), F32),
                            pltpu.VMEM((2, tm * SLAB, LANES), F32), pltpu.VMEM((tm, D), BF16),
                            pltpu.VMEM((tm, f // MOE_SPLIT), BF16), pltpu.VMEM((tm, D), F32),
                            pltpu.VMEM((2, tm * PACK, LANES), jnp.uint32),
                            pltpu.SemaphoreType.DMA((2,)), pltpu.SemaphoreType.DMA((2,)),
                            pltpu.SemaphoreType.DMA((W_RING_IN + W_RING_OUT,))],
        ),
        out_shape=jax.ShapeDtypeStruct((y_rows * PACK, LANES), jnp.uint32),
        compiler_params=_cparams(("arbitrary",), MOE_VMEM_LIMIT),
        name="moe_experts",
    )(tile_expert, n_used, tile_quarters, fmap, fmap, fmap, h_slabs, wg, wu, wd)


def _combine_kernel(y1_ref, y2_ref, x_ref, w_ref, mod_ref, gain_ref, o_ref):
    w = w_ref[...]
    rows = x_ref.shape[0]
    moe = w[:, 0:1] * _unpack_rows(y1_ref, rows) + w[:, 1:2] * _unpack_rows(y2_ref, rows)
    mod = mod_ref[0, 0]
    x4 = x_ref[...] + mod[5:6] * moe
    ms = jnp.mean(x4 * x4, axis=-1, keepdims=True)
    o_ref[...] = x4 * lax.rsqrt(ms + EPS) * gain_ref[...]


def _combine(y, x3_2d, wts, mod, final_gain, seq):
    n = x3_2d.shape[0]
    tc = TC_COMB
    nt = n // tc
    per_b = seq // tc
    return pl.pallas_call(
        _combine_kernel,
        grid=(nt,),
        in_specs=[
            pl.BlockSpec((tc * PACK, LANES), lambda i: (i, 0)),
            pl.BlockSpec((tc * PACK, LANES), lambda i: (i + nt, 0)),
            pl.BlockSpec((tc, D), lambda i: (i, 0)),
            pl.BlockSpec((tc, 2), lambda i: (i, 0)),
            pl.BlockSpec((1, 1, 6, D), lambda i: (1, i // per_b, 0, 0)),
            pl.BlockSpec((1, D), lambda i: (0, 0)),
        ],
        out_specs=pl.BlockSpec((tc, D), lambda i: (i, 0)),
        out_shape=jax.ShapeDtypeStruct((n, D), F32),
        compiler_params=_cparams(("arbitrary",)),
        name="moe_combine",
    )(y, y, x3_2d, wts, mod, final_gain.reshape(1, D))


def _rope_tables(seq):
    rows = seq // GRID_W
    row_pos = jnp.repeat(jnp.arange(rows, dtype=F32), GRID_W)
    col_pos = jnp.tile(jnp.arange(GRID_W, dtype=F32), rows)
    axis_dim = HEAD_DIM // 2
    inv_freq = ROPE_BASE ** (-jnp.arange(0, axis_dim, 2, dtype=F32) / axis_dim)
    ar = row_pos[:, None] * inv_freq
    ac = col_pos[:, None] * inv_freq
    cos64 = jnp.concatenate([jnp.cos(ar), jnp.cos(ar), jnp.cos(ac), jnp.cos(ac)], axis=1)
    sin64 = jnp.concatenate([-jnp.sin(ar), jnp.sin(ar), -jnp.sin(ac), jnp.sin(ac)], axis=1)
    return jnp.tile(cos64, (1, 2)), jnp.tile(sin64, (1, 2))


def _band_matrices():
    r = np.arange(BLOCK)[:, None]
    c = np.arange(BLOCK + 2 * POOL_HALO)[None, :] - POOL_HALO
    mats = []
    for w in POOL_SIZES:
        lo = -(w // 2)
        hi = w - 1 - w // 2
        mats.append(((c >= r + lo) & (c <= r + hi)).astype(np.float32))
    return jnp.asarray(np.stack(mats), dtype=BF16)


def kernel(x, c, ctx, c_ctx, w_ada, b_ada, w_in, attn_sink, gm_gain, gm_w_s, gm_b_s, w_out,
           ffn_w_gate, ffn_w_up, ffn_w_down, pool_w, pool_scale, router_w,
           moe_w_gate, moe_w_up, moe_w_down, final_gain):
    b, s, _ = x.shape
    n = b * s
    assert w_ada.shape[0] == 2 and w_in.shape[0] == 1 and pool_w.shape[0] == 1
    assert s % TM_IN == 0 and s % TM_POOL == 0 and s % TM_FFN == 0 and b <= 4
    assert n & (n - 1) == 0

    cvec = jnp.concatenate([c, c_ctx[None, :], jnp.zeros((8 - b - 1, D), F32)], axis=0)
    mod = _ada_mod(cvec, w_ada, b_ada)

    cos_t, sin_t = _rope_tables(s)
    q, k, ksw, v, vsw, u, vg = _in_proj(x, mod, w_in[0], gm_gain[0], cos_t, sin_t)
    kx, ksx, vx, vsx = _ctx_kv(ctx, mod, w_in[0])
    wcat = gm_w_s[0].reshape(4, 2, BLOCK, BLOCK).transpose(0, 2, 1, 3).reshape(4, BLOCK, 2 * BLOCK).astype(BF16)
    x1 = _attn_mixer(x, mod, attn_sink[0], q, k, ksw, v, vsw, kx, ksx, vx, vsx, u, vg,
                     wcat, gm_b_s[0].T, w_out[0])
    x2 = _dense_ffn(x1.reshape(n, D), mod, ffn_w_gate[0].astype(BF16), ffn_w_up[0].astype(BF16),
                    ffn_w_down[0].astype(BF16), s)

    wr = jnp.pad(router_w[0], ((0, 0), (0, LANES - N_EXPERTS)))
    wr_hi = wr.astype(BF16)
    tri = jnp.asarray(np.tril(np.ones((TM_POOL, TM_POOL), np.float32), -1), dtype=BF16)
    x3, h2, route, counts = _pool_route(x2.reshape(b, s, D), mod, _band_matrices(), pool_w[0].astype(BF16),
                                        pool_scale[0], wr_hi, tri)

    tm = TM_MOE
    n_tiles = (2 * n) // tm + N_EXPERTS
    cnt = counts[0, :N_EXPERTS].astype(jnp.int32)
    tiles_e = (cnt + tm - 1) // tm
    tile_end = jnp.cumsum(tiles_e)
    off = (tile_end - tiles_e) * tm
    n_used = tile_end[-1]
    tix = jnp.arange(n_tiles, dtype=jnp.int32)
    te = jnp.minimum(jnp.sum(tix[:, None] >= tile_end[None, :], axis=1), N_EXPERTS - 1).astype(jnp.int32)
    te_last = te[jnp.maximum(n_used - 1, 0)]
    tile_expert = jnp.where(tix < n_used, te, te_last)
    rows_left = cnt[te] - (tix - (tile_end - tiles_e)[te]) * tm
    quarter = tm // MOE_QUARTERS
    tile_quarters = jnp.clip((rows_left + quarter - 1) // quarter, 1, MOE_QUARTERS).astype(jnp.int32)
    e1 = route[0].astype(jnp.int32)
    e2 = route[1].astype(jnp.int32)
    pos1 = off[e1] + route[4].astype(jnp.int32)
    pos2 = off[e2] + route[5].astype(jnp.int32)
    n_slots = n_tiles * tm
    pad_lo = jnp.concatenate([off + cnt, (n_used * tm).reshape(1)]).astype(jnp.int32)
    pad_hi = jnp.concatenate([off + tiles_e * tm, jnp.full((1,), n_slots, jnp.int32)]).astype(jnp.int32)
    fmap = _slot_map(jnp.concatenate([pos1, pos2]), pad_lo, pad_hi, n_slots)
    n_used_arr = n_used.reshape(1).astype(jnp.int32)

    y = _moe_experts(h2, fmap.reshape(n_tiles, 1, tm), tile_expert, n_used_arr, tile_quarters,
                     moe_w_gate[0], moe_w_up[0], moe_w_down[0], n_tiles, 2 * n + 2 * tm)
    out = _combine(y, x3.reshape(n, D), route[2:4].T, mod, final_gain, s)
    return out.reshape(b, s, D)
```

```python
import functools

import numpy as np
import jax
import jax.numpy as jnp
from jax import lax
from jax.experimental import pallas as pl
from jax.experimental.pallas import tpu as pltpu

F32 = jnp.float32
BF16 = jnp.bfloat16

D = 1024
GRID_W = 64
EPS = 1e-6
NEG_INF = -1e30
HEAD_DIM = 64
N_Q_HEADS = 8
BLOCK = 128
ATT_W = 512
KV_W = 128
GM_W = 512
IN_W = 1792
POOL_SIZES = (2, 4, 8, 16)
POOL_GD = 256
POOL_HALO = 16
N_EXPERTS = 8
ROPE_BASE = 10000.0
LANES = 128
SLAB = D // LANES
PACK = SLAB // 2
SQRT_2_OVER_PI = 0.7978845608028654

TM_IN = 1024
TM_FFN = 1024
TM_POOL = 1024
TM_MOE = 512
MOE_SPLIT = 2
W_CHUNK = 256
W_ROWS_IN = 128
W_ROWS_OUT = 512
W_RING_IN = 12
W_RING_OUT = 3
MOE_VMEM_LIMIT = 60 * 1024 * 1024
TC_COMB = 1024
Q_BLOCKS = 8
VMEM_LIMIT = 56 * 1024 * 1024


def _cparams(sem, vmem=None):
    return pltpu.CompilerParams(dimension_semantics=sem, vmem_limit_bytes=vmem)


def _modulate(xf, shift, scale):
    ms = jnp.mean(xf * xf, axis=-1, keepdims=True)
    return xf * lax.rsqrt(ms + EPS) * (1.0 + scale) + shift


def _sigmoid(z):
    return 1.0 / (1.0 + jnp.exp(-z))


def _rows_to_slabs(val, slab_ref):
    rows = val.shape[0]
    for cix in range(SLAB):
        slab_ref[pl.ds(cix, rows, stride=SLAB), :] = val[:, cix * LANES:(cix + 1) * LANES]


def _slabs_to_rows(slab_ref, rows):
    return jnp.concatenate([slab_ref[pl.ds(cix, rows, stride=SLAB), :] for cix in range(SLAB)], axis=1)


def _pack_rows(val, pack_ref):
    rows = val.shape[0]
    bits = lambda v: lax.bitcast_convert_type(v.astype(BF16).astype(F32), jnp.uint32)
    for cix in range(PACK):
        hi = bits(val[:, cix * LANES:(cix + 1) * LANES])
        lo = bits(val[:, D // 2 + cix * LANES:D // 2 + (cix + 1) * LANES])
        pack_ref[pl.ds(cix, rows, stride=PACK), :] = hi | (lo >> 16)


def _unpack_rows(pack_ref, rows):
    words = [pack_ref[pl.ds(cix, rows, stride=PACK), :] for cix in range(PACK)]
    his = [lax.bitcast_convert_type(w & jnp.uint32(0xFFFF0000), F32) for w in words]
    los = [lax.bitcast_convert_type(w << 16, F32) for w in words]
    return jnp.concatenate(his + los, axis=1)


def _ada_kernel(c_ref, w_ref, b_ref, o_ref):
    c = c_ref[...]
    s = c * _sigmoid(c)
    o_ref[0] = jnp.dot(s.astype(BF16), w_ref[0].astype(BF16), preferred_element_type=F32) + b_ref[0]


def _ada_mod(cvec, w_ada, b_ada):
    depth, _, n6 = w_ada.shape
    tn = 1536
    out = pl.pallas_call(
        _ada_kernel,
        grid=(depth, n6 // tn),
        in_specs=[
            pl.BlockSpec((8, D), lambda l, j: (0, 0)),
            pl.BlockSpec((1, D, tn), lambda l, j: (l, 0, j)),
            pl.BlockSpec((1, 1, tn), lambda l, j: (l, 0, j)),
        ],
        out_specs=pl.BlockSpec((1, 8, tn), lambda l, j: (l, 0, j)),
        out_shape=jax.ShapeDtypeStruct((depth, 8, n6), F32),
        compiler_params=_cparams(("arbitrary", "arbitrary")),
        name="ada_mod",
    )(cvec, w_ada, b_ada.reshape(depth, 1, n6))
    return out.reshape(depth, 8, 6, D)


def _rope(t, cs, sn, first_half):
    fwd = pltpu.roll(t, LANES - 16, axis=1)
    bwd = pltpu.roll(t, 16, axis=1)
    return t * cs + jnp.where(first_half, fwd, bwd) * sn


def _inproj_kernel(x_ref, mod_ref, w_ref, gain_ref, cos_ref, sin_ref,
                   q_ref, k_ref, ksw_ref, v_ref, vsw_ref, u_ref, vg_ref, w_bf):
    @pl.when((pl.program_id(0) == 0) & (pl.program_id(1) == 0))
    def _():
        w_bf[...] = w_ref[...].astype(BF16)

    mod = mod_ref[0, 0]
    h = _modulate(x_ref[0], mod[0:1], mod[1:2]).astype(BF16)
    proj = jnp.dot(h, w_bf[...], preferred_element_type=F32)
    cs = cos_ref[...]
    sn = sin_ref[...]
    lane = lax.broadcasted_iota(jnp.int32, cs.shape, 1)
    first_half = (lane & 16) == 0
    for cix in range(ATT_W // LANES):
        t = proj[:, cix * LANES:(cix + 1) * LANES]
        q_ref[0, :, cix * LANES:(cix + 1) * LANES] = (
            _rope(t, cs, sn, first_half) * (HEAD_DIM ** -0.5)).astype(BF16)
    kr = _rope(proj[:, ATT_W:ATT_W + KV_W], cs, sn, first_half)
    k_ref[0] = kr.astype(BF16)
    ksw_ref[0] = pltpu.roll(kr, HEAD_DIM, axis=1).astype(BF16)
    vv = proj[:, ATT_W + KV_W:ATT_W + 2 * KV_W]
    v_ref[0] = vv.astype(BF16)
    vsw_ref[0] = pltpu.roll(vv, HEAD_DIM, axis=1).astype(BF16)
    z = proj[:, ATT_W + 2 * KV_W:]
    g = z * (0.5 * (1.0 + jnp.tanh(SQRT_2_OVER_PI * (z + 0.044715 * (z * z * z)))))
    u_ref[0] = g[:, :GM_W].astype(BF16)
    vg = g[:, GM_W:]
    ms = jnp.mean(vg * vg, axis=-1, keepdims=True)
    vg_ref[0] = (vg * lax.rsqrt(ms + EPS) * gain_ref[...]).astype(BF16)


def _in_proj(x, mod, w_in, gm_gain, cos_t, sin_t):
    b, s, _ = x.shape
    tm = TM_IN
    row = lambda w: pl.BlockSpec((1, tm, w), lambda bi, i: (bi, i, 0))
    outs = pl.pallas_call(
        _inproj_kernel,
        grid=(b, s // tm),
        in_specs=[
            row(D),
            pl.BlockSpec((1, 1, 6, D), lambda bi, i: (0, bi, 0, 0)),
            pl.BlockSpec((D, IN_W), lambda bi, i: (0, 0), pipeline_mode=pl.Buffered(1)),
            pl.BlockSpec((1, GM_W), lambda bi, i: (0, 0)),
            pl.BlockSpec((tm, LANES), lambda bi, i: (i, 0)),
            pl.BlockSpec((tm, LANES), lambda bi, i: (i, 0)),
        ],
        out_specs=[row(ATT_W), row(KV_W), row(KV_W), row(KV_W), row(KV_W), row(GM_W), row(GM_W)],
        out_shape=[jax.ShapeDtypeStruct((b, s, w), BF16)
                   for w in (ATT_W, KV_W, KV_W, KV_W, KV_W, GM_W, GM_W)],
        scratch_shapes=[pltpu.VMEM((D, IN_W), BF16)],
        compiler_params=_cparams(("arbitrary", "arbitrary"), VMEM_LIMIT),
        name="in_proj",
    )(x, mod, w_in, gm_gain.reshape(1, GM_W), cos_t, sin_t)
    return outs


def _ctx_kernel(c_ref, mod_ref, w_ref, k_ref, ksw_ref, v_ref, vsw_ref):
    mod = mod_ref[0, 0]
    h = _modulate(c_ref[0], mod[0:1], mod[1:2]).astype(BF16)
    kv = jnp.dot(h, w_ref[...].astype(BF16), preferred_element_type=F32)
    kk = kv[:, :KV_W]
    vv = kv[:, KV_W:]
    k_ref[0] = kk.astype(BF16)
    ksw_ref[0] = pltpu.roll(kk, HEAD_DIM, axis=1).astype(BF16)
    v_ref[0] = vv.astype(BF16)
    vsw_ref[0] = pltpu.roll(vv, HEAD_DIM, axis=1).astype(BF16)


def _ctx_kv(ctx, mod, w_in):
    b, l, _ = ctx.shape
    spec = pl.BlockSpec((1, l, KV_W), lambda bi: (bi, 0, 0))
    return pl.pallas_call(
        _ctx_kernel,
        grid=(b,),
        in_specs=[
            pl.BlockSpec((1, l, D), lambda bi: (bi, 0, 0)),
            pl.BlockSpec((1, 1, 6, D), lambda bi: (0, b, 0, 0)),
            pl.BlockSpec((D, 2 * KV_W), lambda bi: (0, ATT_W // (2 * KV_W))),
        ],
        out_specs=[spec] * 4,
        out_shape=[jax.ShapeDtypeStruct((b, l, KV_W), BF16)] * 4,
        compiler_params=_cparams(("arbitrary",)),
        name="ctx_kv",
    )(ctx, mod, w_in)


def _attn_kernel(sink_ref, q_ref, kp_ref, kc_ref, kn_ref, ksp_ref, ksc_ref, ksn_ref,
                 vp_ref, vc_ref, vn_ref, vsp_ref, vsc_ref, vsn_ref,
                 kx_ref, ksx_ref, vx_ref, vsx_ref,
                 u_ref, vg_ref, wcat_ref, bs_ref, wout_ref, x_ref, mod_ref, o_ref, wout_bf):
    n = pl.program_id(1)

    @pl.when((pl.program_id(0) == 0) & (n == 0))
    def _():
        wout_bf[...] = wout_ref[...].astype(BF16)

    nblk = pl.num_programs(1) * Q_BLOCKS
    lane = lax.broadcasted_iota(jnp.int32, (1, LANES), 1)
    low = lane < HEAD_DIM
    zero = jnp.zeros((), BF16)

    def variants(a0, a1):
        return ((jnp.where(low, a0, zero), jnp.where(low, zero, a1)),
                (jnp.where(low, a1, zero), jnp.where(low, zero, a0)))

    cat = lambda refs: jnp.concatenate([r[0] for r in refs], axis=0)
    kb_var = variants(cat((kp_ref, kc_ref, kn_ref)), cat((ksp_ref, ksc_ref, ksn_ref)))
    vb_var = variants(cat((vp_ref, vc_ref, vn_ref)), cat((vsp_ref, vsc_ref, vsn_ref)))
    kx_var = variants(kx_ref[0], ksx_ref[0])
    vx_var = variants(vx_ref[0], vsx_ref[0])

    row = lax.broadcasted_iota(jnp.int32, (2 * BLOCK, BLOCK), 0) & (BLOCK - 1)
    col = lax.broadcasted_iota(jnp.int32, (2 * BLOCK, BLOCK), 1)
    top = lax.broadcasted_iota(jnp.int32, (2 * BLOCK, 1), 0) < BLOCK
    nt_dims = (((1,), (1,)), ((), ()))

    q = q_ref[0]
    att_blocks = [[None] * 4 for _ in range(Q_BLOCKS)]
    for kvh in range(2):
        qst = jnp.concatenate(
            [q[qb * BLOCK:(qb + 1) * BLOCK, pr * LANES:(pr + 1) * LANES]
             for qb in range(Q_BLOCKS) for pr in (2 * kvh, 2 * kvh + 1)], axis=0)
        accs = [None] * Q_BLOCKS
        for half in range(2):
            sk = jnp.where(top, sink_ref[4 * kvh + half], sink_ref[4 * kvh + 2 + half])
            s_ctx = lax.dot_general(qst, kx_var[kvh][half], nt_dims, preferred_element_type=F32)
            o_band, p_ctx, dens = [], [], []
            for qb in range(Q_BLOCKS):
                g = n * Q_BLOCKS + qb
                qrows = qst[qb * 2 * BLOCK:(qb + 1) * 2 * BLOCK]
                sb = lax.dot_general(qrows, kb_var[kvh][half][qb * BLOCK:(qb + 3) * BLOCK], nt_dims,
                                     preferred_element_type=F32)
                s0 = jnp.where((col >= row) & (g > 0), sb[:, :BLOCK], NEG_INF)
                s1 = sb[:, BLOCK:2 * BLOCK]
                s2 = jnp.where((col <= row) & (g < nblk - 1), sb[:, 2 * BLOCK:], NEG_INF)
                sc = s_ctx[qb * 2 * BLOCK:(qb + 1) * 2 * BLOCK]
                ctx_blocks = [sc[:, cb * BLOCK:(cb + 1) * BLOCK] for cb in range(sc.shape[1] // BLOCK)]
                m = functools.reduce(jnp.maximum, [s0, s1, s2] + ctx_blocks)
                m = jnp.maximum(jnp.max(m, axis=-1, keepdims=True), sk)
                p0, p1, p2, pc = (jnp.exp(t - m) for t in (s0, s1, s2, sc))
                psum = functools.reduce(
                    jnp.add, [p0, p1, p2] + [pc[:, cb * BLOCK:(cb + 1) * BLOCK] for cb in range(len(ctx_blocks))])
                den = jnp.sum(psum, axis=-1, keepdims=True) + jnp.exp(sk - m)
                pb = jnp.concatenate([p0, p1, p2], axis=1).astype(BF16)
                o_band.append(jnp.dot(pb, vb_var[kvh][half][qb * BLOCK:(qb + 3) * BLOCK],
                                      preferred_element_type=F32))
                p_ctx.append(pc.astype(BF16))
                dens.append(den)
            o_ctx = jnp.dot(jnp.concatenate(p_ctx, axis=0), vx_var[kvh][half], preferred_element_type=F32)
            for qb in range(Q_BLOCKS):
                o = (o_band[qb] + o_ctx[qb * 2 * BLOCK:(qb + 1) * 2 * BLOCK]) / dens[qb]
                accs[qb] = o if accs[qb] is None else accs[qb] + o
        for qb in range(Q_BLOCKS):
            att_blocks[qb][2 * kvh] = accs[qb][:BLOCK]
            att_blocks[qb][2 * kvh + 1] = accs[qb][BLOCK:]

    u = u_ref[0]
    vg = vg_ref[0]
    bs = bs_ref[...]
    gm_blocks = [[None] * 4 for _ in range(Q_BLOCKS)]
    for j in range(GM_W // LANES):
        chunks = [vg[c * BLOCK:(c + 1) * BLOCK, j * LANES:(j + 1) * LANES] for c in range(Q_BLOCKS)]
        rhs = jnp.concatenate(
            [jnp.concatenate([jnp.where(low, v, zero) for v in chunks], axis=1),
             jnp.concatenate([jnp.where(low, zero, v) for v in chunks], axis=1)], axis=0)
        mixed = jnp.dot(wcat_ref[j], rhs, preferred_element_type=F32)
        bias = jnp.where(low, bs[:, 2 * j:2 * j + 1], bs[:, 2 * j + 1:2 * j + 2])
        for c in range(Q_BLOCKS):
            gm_blocks[c][j] = (u[c * BLOCK:(c + 1) * BLOCK, j * LANES:(j + 1) * LANES].astype(F32)
                               * (mixed[:, c * LANES:(c + 1) * LANES] + bias))

    mix = jnp.concatenate([jnp.concatenate(att_blocks[c] + gm_blocks[c], axis=1) for c in range(Q_BLOCKS)],
                          axis=0).astype(BF16)
    y = jnp.dot(mix, wout_bf[...], preferred_element_type=F32)
    mod = mod_ref[0, 0]
    o_ref[0] = x_ref[0] + mod[2:3] * y


def _attn_mixer(x, mod, sink, q, k, ksw, v, vsw, kx, ksx, vx, vsx, u, vg, wcat_bf, bs_t, wout):
    b, s, _ = x.shape
    tq = Q_BLOCKS * BLOCK
    nb = s // BLOCK
    l = kx.shape[1]
    cur = lambda w: pl.BlockSpec((1, tq, w), lambda bi, n: (bi, n, 0))
    prv = lambda w: pl.BlockSpec((1, BLOCK, w), lambda bi, n: (bi, jnp.maximum(n * Q_BLOCKS - 1, 0), 0))
    nxt = lambda w: pl.BlockSpec((1, BLOCK, w), lambda bi, n: (bi, jnp.minimum((n + 1) * Q_BLOCKS, nb - 1), 0))
    cx = pl.BlockSpec((1, l, KV_W), lambda bi, n: (bi, 0, 0))
    return pl.pallas_call(
        _attn_kernel,
        grid=(b, s // tq),
        in_specs=[
            pl.BlockSpec(memory_space=pltpu.SMEM),
            cur(ATT_W),
            prv(KV_W), cur(KV_W), nxt(KV_W), prv(KV_W), cur(KV_W), nxt(KV_W),
            prv(KV_W), cur(KV_W), nxt(KV_W), prv(KV_W), cur(KV_W), nxt(KV_W),
            cx, cx, cx, cx,
            cur(GM_W), cur(GM_W),
            pl.BlockSpec((4, BLOCK, 2 * BLOCK), lambda bi, n: (0, 0, 0)),
            pl.BlockSpec((BLOCK, 8), lambda bi, n: (0, 0)),
            pl.BlockSpec((D, D), lambda bi, n: (0, 0), pipeline_mode=pl.Buffered(1)),
            cur(D),
            pl.BlockSpec((1, 1, 6, D), lambda bi, n: (0, bi, 0, 0)),
        ],
        out_specs=cur(D),
        out_shape=jax.ShapeDtypeStruct((b, s, D), F32),
        scratch_shapes=[pltpu.VMEM((D, D), BF16)],
        compiler_params=_cparams(("arbitrary", "arbitrary"), VMEM_LIMIT),
        name="attn_gmlp_out",
    )(sink, q, k, k, k, ksw, ksw, ksw, v, v, v, vsw, vsw, vsw, kx, ksx, vx, vsx,
      u, vg, wcat_bf, bs_t, wout, x, mod)


def _ffn_kernel(x_ref, mod_ref, wg_ref, wu_ref, wd_ref, o_ref, act):
    mod = mod_ref[0, 0]
    xf = x_ref[...]
    h = _modulate(xf, mod[3:4], mod[4:5]).astype(BF16)
    for c in range(wg_ref.shape[1] // W_CHUNK):
        cs = slice(c * W_CHUNK, (c + 1) * W_CHUNK)
        g = jnp.dot(h, wg_ref[:, cs], preferred_element_type=F32)
        up = jnp.dot(h, wu_ref[:, cs], preferred_element_type=F32)
        act[:, cs] = (g * _sigmoid(g) * up).astype(BF16)
    o_ref[...] = xf + mod[5:6] * jnp.dot(act[...], wd_ref[...], preferred_element_type=F32)


def _dense_ffn(x2d, mod, wg, wu, wd, seq):
    n = x2d.shape[0]
    f = wg.shape[1]
    tm = TM_FFN
    per_b = seq // tm
    resident = lambda shp: pl.BlockSpec(shp, lambda i: (0, 0), pipeline_mode=pl.Buffered(1))
    return pl.pallas_call(
        _ffn_kernel,
        grid=(n // tm,),
        in_specs=[
            pl.BlockSpec((tm, D), lambda i: (i, 0)),
            pl.BlockSpec((1, 1, 6, D), lambda i: (0, i // per_b, 0, 0)),
            resident((D, f)), resident((D, f)), resident((f, D)),
        ],
        out_specs=pl.BlockSpec((tm, D), lambda i: (i, 0)),
        out_shape=jax.ShapeDtypeStruct((n, D), F32),
        scratch_shapes=[pltpu.VMEM((tm, f), BF16)],
        compiler_params=_cparams(("arbitrary",), VMEM_LIMIT),
        name="dense_ffn",
    )(x2d, mod, wg, wu, wd)


def _pool_route_kernel(x_ref, xp_ref, xn_ref, mod_ref, band_ref, pw_ref, psc_ref, wr_hi_ref,
                       tri_ref, x3_ref, h2_ref, route_ref, cnt_ref, hext, carry):
    bi = pl.program_id(0)
    i = pl.program_id(1)
    ni = pl.num_programs(1)
    tm = x_ref.shape[1]
    seq = tm * ni
    mod = mod_ref[0, 0]

    @pl.when((bi == 0) & (i == 0))
    def _():
        carry[...] = jnp.zeros_like(carry)

    xf = x_ref[0]
    hp = _modulate(xp_ref[0], mod[0:1], mod[1:2])
    hn = _modulate(xn_ref[0], mod[0:1], mod[1:2])
    hext[0:POOL_HALO] = jnp.where(i > 0, hp, 0.0).astype(BF16)
    h_main = _modulate(xf, mod[0:1], mod[1:2])
    hext[POOL_HALO:POOL_HALO + tm] = h_main.astype(BF16)
    hext[POOL_HALO + tm:] = jnp.where(i < ni - 1, hn, 0.0).astype(BF16)

    t_local = lax.broadcasted_iota(jnp.int32, (BLOCK, 1), 0)
    ys = []
    for gi, w in enumerate(POOL_SIZES):
        lo_off = -(w // 2)
        hi_off = w - 1 - w // 2
        cols = slice(gi * POOL_GD, (gi + 1) * POOL_GD)
        outs = []
        for sb in range(tm // BLOCK):
            r0 = sb * BLOCK
            win = jnp.dot(band_ref[gi], hext[r0:r0 + BLOCK + 2 * POOL_HALO, cols],
                          preferred_element_type=F32)
            t = i * tm + r0 + t_local
            cnt = (jnp.minimum(t + hi_off, seq - 1) - jnp.maximum(t + lo_off, 0) + 1).astype(F32)
            diff = win / cnt - h_main[r0:r0 + BLOCK, cols]
            outs.append(diff.astype(BF16))
        dg = jnp.concatenate(outs, axis=0)
        ys.append(jnp.dot(dg, pw_ref[gi], preferred_element_type=F32))
    y = jnp.concatenate(ys, axis=1) * psc_ref[...]
    x3 = xf + mod[2:3] * y
    x3_ref[0] = x3

    h2 = _modulate(x3, mod[3:4], mod[4:5])
    _rows_to_slabs(h2, h2_ref)
    logits = jnp.dot(h2.astype(BF16), wr_hi_ref[...], preferred_element_type=F32)
    lane = lax.broadcasted_iota(jnp.int32, (tm, LANES), 1)
    lane_f = lane.astype(F32)
    neg = -jnp.inf
    lg = jnp.where(lane < N_EXPERTS, logits, neg)
    m1 = jnp.max(lg, axis=-1, keepdims=True)
    i1 = jnp.min(jnp.where(lg == m1, lane_f, float(LANES)), axis=-1, keepdims=True)
    oh1 = lane_f == i1
    lg2 = jnp.where(oh1, neg, lg)
    m2 = jnp.max(lg2, axis=-1, keepdims=True)
    i2 = jnp.min(jnp.where(lg2 == m2, lane_f, float(LANES)), axis=-1, keepdims=True)
    oh2 = lane_f == i2
    e = jnp.exp(m2 - m1)
    w1 = 1.0 / (1.0 + e)
    w2 = e / (1.0 + e)
    oh = jnp.where(oh1 | oh2, 1.0, 0.0)
    before = jnp.dot(tri_ref[...], oh.astype(BF16), preferred_element_type=F32) + carry[...]
    r1 = jnp.sum(jnp.where(oh1, before, 0.0), axis=-1, keepdims=True)
    r2 = jnp.sum(jnp.where(oh2, before, 0.0), axis=-1, keepdims=True)
    carry[...] = carry[...] + jnp.sum(oh, axis=0, keepdims=True)
    cnt_ref[...] = carry[...]
    info = jnp.where(lane == 0, i1, jnp.where(lane == 1, i2, jnp.where(lane == 2, w1, jnp.where(
        lane == 3, w2, jnp.where(lane == 4, r1, jnp.where(lane == 5, r2, 0.0))))))
    route_ref[...] = info.T[0:8, :]


def _pool_route(x, mod, band, pw_bf, pool_scale, wr_hi, tri):
    b, s, _ = x.shape
    tm = TM_POOL
    ni = s // tm
    hb = tm // POOL_HALO
    row = pl.BlockSpec((1, tm, D), lambda bi, i: (bi, i, 0))
    const2 = lambda shp: pl.BlockSpec(shp, lambda bi, i: (0,) * len(shp))
    return pl.pallas_call(
        _pool_route_kernel,
        grid=(b, ni),
        in_specs=[
            row,
            pl.BlockSpec((1, POOL_HALO, D), lambda bi, i: (bi, jnp.maximum(i * hb - 1, 0), 0)),
            pl.BlockSpec((1, POOL_HALO, D), lambda bi, i: (bi, jnp.minimum((i + 1) * hb, s // POOL_HALO - 1), 0)),
            pl.BlockSpec((1, 1, 6, D), lambda bi, i: (1, bi, 0, 0)),
            const2(band.shape), const2(pw_bf.shape), const2((1, D)),
            const2(wr_hi.shape), const2(tri.shape),
        ],
        out_specs=[row,
                   pl.BlockSpec((tm * SLAB, LANES), lambda bi, i: (bi * ni + i, 0)),
                   pl.BlockSpec((8, tm), lambda bi, i: (0, bi * ni + i)),
                   pl.BlockSpec((1, LANES), lambda bi, i: (0, 0))],
        out_shape=[jax.ShapeDtypeStruct((b, s, D), F32), jax.ShapeDtypeStruct((b * s * SLAB, LANES), F32),
                   jax.ShapeDtypeStruct((8, b * s), F32), jax.ShapeDtypeStruct((1, LANES), F32)],
        scratch_shapes=[pltpu.VMEM((tm + 2 * POOL_HALO, D), BF16), pltpu.VMEM((1, LANES), F32)],
        compiler_params=_cparams(("arbitrary", "arbitrary"), VMEM_LIMIT),
        name="pool_route",
    )(x, x, x, mod, band, pw_bf, pool_scale.reshape(1, D), wr_hi, tri)


def _slotmap_kernel(pos_ref, lo_ref, hi_ref, o_ref):
    n_pairs = pos_ref.shape[0]
    spare_mask = 2 * TM_MOE - 1
    for e in range(lo_ref.shape[0]):
        def fill(p, c):
            o_ref[p] = n_pairs + (p & spare_mask)
            return c
        lax.fori_loop(lo_ref[e], hi_ref[e], fill, 0)

    def place(f, c):
        o_ref[pos_ref[f]] = f
        return c
    lax.fori_loop(0, n_pairs, place, 0, unroll=32)


def _slot_map(pos_flat, lo, hi, n_slots):
    smem = pl.BlockSpec(memory_space=pltpu.SMEM)
    return pl.pallas_call(
        _slotmap_kernel,
        in_specs=[smem, smem, smem],
        out_specs=smem,
        out_shape=jax.ShapeDtypeStruct((n_slots,), jnp.int32),
        name="moe_slot_map",
    )(pos_flat, lo, hi)


def _moe_kernel(te_ref, nused_ref, half_ref, fnext_ref, fprev_ref, f0_ref, h_hbm, wg_hbm, wu_hbm, wd_hbm,
                y_hbm, wg_res, wu_res, wd_res, stg_in, stg_out, xbuf, xb, act, acc, stage, gsem, ssem, wsem):
    i = pl.program_id(0)
    nt = pl.num_programs(0)
    used_tiles = nused_ref[0]
    tm = xb.shape[0]
    f_dim = wg_res.shape[1]
    tok_mask = h_hbm.shape[0] // SLAB - 1
    tile_rows = tm * SLAB
    out_rows = tm * PACK

    def slab(ix, width=SLAB):
        return pl.ds(pl.multiple_of(ix * width, width), width)

    def gather_row(fref, r, slot, zero=0):
        tok = fref[0, 0, r + zero] & tok_mask
        return pltpu.make_async_copy(h_hbm.at[slab(tok)], xbuf.at[slot, slab(r)], gsem.at[slot])

    def scatter_row(fref, r, slot, zero=0):
        return pltpu.make_async_copy(stage.at[slot, slab(r, PACK)], y_hbm.at[slab(fref[0, 0, r + zero], PACK)],
                                     ssem.at[slot])

    def gather_all(slot):
        return pltpu.make_async_copy(h_hbm.at[pl.ds(0, tile_rows)], xbuf.at[slot], gsem.at[slot])

    def scatter_all(slot):
        return pltpu.make_async_copy(stage.at[slot], y_hbm.at[pl.ds(0, out_rows)], ssem.at[slot])

    cur = i % 2
    used = i < used_tiles

    @pl.when(i == 0)
    def _():
        stage[...] = jnp.zeros_like(stage)
        spare = y_hbm.shape[0] - 2 * out_rows
        fills = [pltpu.make_async_copy(stage.at[sl], y_hbm.at[pl.ds(spare + sl * out_rows, out_rows)],
                                       ssem.at[sl])
                 for sl in range(2)]
        for cp in fills:
            cp.start()
        for cp in fills:
            cp.wait()

    @pl.when((i == 0) & used)
    def _():
        def prime(r, c):
            gather_row(f0_ref, r, 0).start()
            return c
        lax.fori_loop(0, tm, prime, 0)

    expert = te_ref[i]
    new_expert = used & ((i == 0) | (expert != te_ref[jnp.maximum(i - 1, 0)]))

    has_next = i + 1 < used_tiles
    has_prev = (i >= 1) & (i - 1 < used_tiles)
    steady = (i >= 1) & has_next
    half_tile = half_ref[i] == 1

    @pl.when(used)
    def _():
        gather_all(cur).wait()

    @pl.when((i >= 2) & (i - 2 < used_tiles))
    def _():
        scatter_all(cur).wait()

    half = f_dim // MOE_SPLIT
    n_piece = half // W_CHUNK
    n_groups = MOE_SPLIT * (n_piece + 1)

    def tick(v):
        bits = jnp.max(lax.bitcast_convert_type(v[0:SLAB, 0:LANES], jnp.int32))
        return lax.shift_right_logical(lax.shift_right_logical(bits, 16), 16)

    def expert_ffn(issue_group, rows=tm):
        for cix in range(SLAB):
            xb[0:rows, cix * LANES:(cix + 1) * LANES] = (
                xbuf.at[cur][pl.ds(cix, rows, stride=SLAB), :].astype(BF16))
        xv = xb[0:rows]
        out = None
        issue_group(0, 0)
        k = 1
        for hf in range(MOE_SPLIT):
            for c in range(n_piece):
                cs = slice(hf * half + c * W_CHUNK, hf * half + (c + 1) * W_CHUNK)
                g = jnp.dot(xv, wg_res[:, cs], preferred_element_type=F32)
                up = jnp.dot(xv, wu_res[:, cs], preferred_element_type=F32)
                act[0:rows, c * W_CHUNK:(c + 1) * W_CHUNK] = (g * _sigmoid(g) * up).astype(BF16)
                issue_group(k, tick(g))
                k += 1
            part = jnp.dot(act[0:rows], wd_res[hf * half:(hf + 1) * half, :], preferred_element_type=F32)
            if hf < MOE_SPLIT - 1:
                acc[0:rows] = part if out is None else acc[0:rows] + part
                out = acc
                issue_group(k, tick(part))
                k += 1
            else:
                _pack_rows(part if out is None else acc[0:rows] + part, stage.at[cur])

    def chunk_plan():
        n_in = D // W_ROWS_IN
        plan = [(w, res, stg_in, W_RING_IN, 0, W_ROWS_IN, c, wi * n_in + c)
                for wi, (w, res) in enumerate(((wg_hbm, wg_res), (wu_hbm, wu_res))) for c in range(n_in)]
        return plan + [(wd_hbm, wd_res, stg_out, W_RING_OUT, W_RING_IN, W_ROWS_OUT, c, c)
                       for c in range(f_dim // W_ROWS_OUT)]

    def chunk_copy(entry, ex):
        w_hbm, _, stg, ring, sem0, rows, c, j = entry
        return pltpu.make_async_copy(w_hbm.at[ex, pl.ds(c * rows, rows), :], stg.at[j % ring],
                                     wsem.at[sem0 + j % ring])

    def start_first_chunks(ex):
        for entry in chunk_plan():
            if entry[7] < entry[3]:
                chunk_copy(entry, ex).start()

    @pl.when(new_expert)
    def _():
        plan = chunk_plan()

        @pl.when(i == 0)
        def _():
            start_first_chunks(expert)

        for k, entry in enumerate(plan):
            _, res, stg, ring, _, rows, c, j = entry
            chunk_copy(entry, expert).wait()
            res[c * rows:(c + 1) * rows, :] = stg[j % ring].astype(BF16)
            later = [e for e in plan[k + 1:] if e[2] is stg and e[7] == j + ring]
            if later:
                chunk_copy(later[0], expert).start()

    next_expert = te_ref[jnp.minimum(i + 1, nt - 1)]

    @pl.when(has_next & (next_expert != expert))
    def _():
        start_first_chunks(next_expert)

    def issue_group(k, zero):
        for r in range(k * tm // n_groups, (k + 1) * tm // n_groups):
            gather_row(fnext_ref, r, 1 - cur, zero).start()
            scatter_row(fprev_ref, r, 1 - cur, zero).start()

    @pl.when(steady & jnp.logical_not(half_tile))
    def _():
        expert_ffn(issue_group)

    @pl.when(steady & half_tile)
    def _():
        expert_ffn(issue_group, tm // 2)

    @pl.when(jnp.logical_not(steady))
    def _():
        @pl.when(used)
        def _():
            expert_ffn(lambda k, zero: None)

        @pl.when(has_next)
        def _():
            def issue(r, c):
                gather_row(fnext_ref, r, 1 - cur).start()
                return c
            lax.fori_loop(0, tm, issue, 0)

        @pl.when(has_prev)
        def _():
            def issue(r, c):
                scatter_row(fprev_ref, r, 1 - cur).start()
                return c
            lax.fori_loop(0, tm, issue, 0)

    @pl.when((i == nt - 1) & (nt - 2 < used_tiles))
    def _():
        scatter_all(1 - cur).wait()


def _moe_experts(h_slabs, fmap, tile_expert, n_used, tile_half, wg, wu, wd, n_tiles, y_rows):
    f = wg.shape[2]
    tm = TM_MOE
    fblk = lambda imap: pl.BlockSpec((1, 1, tm), imap, memory_space=pltpu.SMEM)
    hbm = pl.BlockSpec(memory_space=pl.ANY)
    return pl.pallas_call(
        _moe_kernel,
        grid_spec=pltpu.PrefetchScalarGridSpec(
            num_scalar_prefetch=3,
            grid=(n_tiles,),
            in_specs=[
                fblk(lambda i, te, nu, hf: (jnp.minimum(i + 1, n_tiles - 1), 0, 0)),
                fblk(lambda i, te, nu, hf: (jnp.maximum(i - 1, 0), 0, 0)),
                fblk(lambda i, te, nu, hf: (0, 0, 0)),
                hbm, hbm, hbm, hbm,
            ],
            out_specs=hbm,
            scratch_shapes=[pltpu.VMEM((D, f), BF16), pltpu.VMEM((D, f), BF16), pltpu.VMEM((f, D), BF16),
                            pltpu.VMEM((W_RING_IN, W_ROWS_IN, f), F32), pltpu.VMEM((W_RING_OUT, W_ROWS_OUT, D), F32),
                            pltpu.VMEM((2, tm * SLAB, LANES), F32), pltpu.VMEM((tm, D), BF16),
                            pltpu.VMEM((tm, f // MOE_SPLIT), BF16), pltpu.VMEM((tm, D), F32),
                            pltpu.VMEM((2, tm * PACK, LANES), jnp.uint32),
                            pltpu.SemaphoreType.DMA((2,)), pltpu.SemaphoreType.DMA((2,)),
                            pltpu.SemaphoreType.DMA((W_RING_IN + W_RING_OUT,))],
        ),
        out_shape=jax.ShapeDtypeStruct((y_rows * PACK, LANES), jnp.uint32),
        compiler_params=_cparams(("arbitrary",), MOE_VMEM_LIMIT),
        name="moe_experts",
    )(tile_expert, n_used, tile_half, fmap, fmap, fmap, h_slabs, wg, wu, wd)


def _combine_kernel(y1_ref, y2_ref, x_ref, w_ref, mod_ref, gain_ref, o_ref):
    rows = x_ref.shape[0]
    info = jnp.concatenate([w_ref[...], jnp.zeros((LANES - w_ref.shape[0], rows), F32)], axis=0).T
    moe = info[:, 2:3] * _unpack_rows(y1_ref, rows) + info[:, 3:4] * _unpack_rows(y2_ref, rows)
    mod = mod_ref[0, 0]
    x4 = x_ref[...] + mod[5:6] * moe
    ms = jnp.mean(x4 * x4, axis=-1, keepdims=True)
    o_ref[...] = x4 * lax.rsqrt(ms + EPS) * gain_ref[...]


def _combine(y, x3_2d, wts, mod, final_gain, seq):
    n = x3_2d.shape[0]
    tc = TC_COMB
    nt = n // tc
    per_b = seq // tc
    return pl.pallas_call(
        _combine_kernel,
        grid=(nt,),
        in_specs=[
            pl.BlockSpec((tc * PACK, LANES), lambda i: (i, 0)),
            pl.BlockSpec((tc * PACK, LANES), lambda i: (i + nt, 0)),
            pl.BlockSpec((tc, D), lambda i: (i, 0)),
            pl.BlockSpec((8, tc), lambda i: (0, i)),
            pl.BlockSpec((1, 1, 6, D), lambda i: (1, i // per_b, 0, 0)),
            pl.BlockSpec((1, D), lambda i: (0, 0)),
        ],
        out_specs=pl.BlockSpec((tc, D), lambda i: (i, 0)),
        out_shape=jax.ShapeDtypeStruct((n, D), F32),
        compiler_params=_cparams(("arbitrary",)),
        name="moe_combine",
    )(y, y, x3_2d, wts, mod, final_gain.reshape(1, D))


def _rope_tables(seq):
    rows = seq // GRID_W
    row_pos = jnp.repeat(jnp.arange(rows, dtype=F32), GRID_W)
    col_pos = jnp.tile(jnp.arange(GRID_W, dtype=F32), rows)
    axis_dim = HEAD_DIM // 2
    inv_freq = ROPE_BASE ** (-jnp.arange(0, axis_dim, 2, dtype=F32) / axis_dim)
    ar = row_pos[:, None] * inv_freq
    ac = col_pos[:, None] * inv_freq
    cos64 = jnp.concatenate([jnp.cos(ar), jnp.cos(ar), jnp.cos(ac), jnp.cos(ac)], axis=1)
    sin64 = jnp.concatenate([-jnp.sin(ar), jnp.sin(ar), -jnp.sin(ac), jnp.sin(ac)], axis=1)
    return jnp.tile(cos64, (1, 2)), jnp.tile(sin64, (1, 2))


def _band_matrices():
    r = np.arange(BLOCK)[:, None]
    c = np.arange(BLOCK + 2 * POOL_HALO)[None, :] - POOL_HALO
    mats = []
    for w in POOL_SIZES:
        lo = -(w // 2)
        hi = w - 1 - w // 2
        mats.append(((c >= r + lo) & (c <= r + hi)).astype(np.float32))
    return jnp.asarray(np.stack(mats), dtype=BF16)


def kernel(x, c, ctx, c_ctx, w_ada, b_ada, w_in, attn_sink, gm_gain, gm_w_s, gm_b_s, w_out,
           ffn_w_gate, ffn_w_up, ffn_w_down, pool_w, pool_scale, router_w,
           moe_w_gate, moe_w_up, moe_w_down, final_gain):
    b, s, _ = x.shape
    n = b * s
    assert w_ada.shape[0] == 2 and w_in.shape[0] == 1 and pool_w.shape[0] == 1
    assert s % TM_IN == 0 and s % TM_POOL == 0 and s % TM_FFN == 0 and b <= 4
    assert n & (n - 1) == 0

    cvec = jnp.concatenate([c, c_ctx[None, :], jnp.zeros((8 - b - 1, D), F32)], axis=0)
    mod = _ada_mod(cvec, w_ada, b_ada)

    cos_t, sin_t = _rope_tables(s)
    q, k, ksw, v, vsw, u, vg = _in_proj(x, mod, w_in[0], gm_gain[0], cos_t, sin_t)
    kx, ksx, vx, vsx = _ctx_kv(ctx, mod, w_in[0])
    wcat = gm_w_s[0].reshape(4, 2, BLOCK, BLOCK).transpose(0, 2, 1, 3).reshape(4, BLOCK, 2 * BLOCK).astype(BF16)
    x1 = _attn_mixer(x, mod, attn_sink[0], q, k, ksw, v, vsw, kx, ksx, vx, vsx, u, vg,
                     wcat, gm_b_s[0].T, w_out[0])
    x2 = _dense_ffn(x1.reshape(n, D), mod, ffn_w_gate[0].astype(BF16), ffn_w_up[0].astype(BF16),
                    ffn_w_down[0].astype(BF16), s)

    wr = jnp.pad(router_w[0], ((0, 0), (0, LANES - N_EXPERTS)))
    wr_hi = wr.astype(BF16)
    tri = jnp.asarray(np.tril(np.ones((TM_POOL, TM_POOL), np.float32), -1), dtype=BF16)
    x3, h2, route, counts = _pool_route(x2.reshape(b, s, D), mod, _band_matrices(), pool_w[0].astype(BF16),
                                        pool_scale[0], wr_hi, tri)

    tm = TM_MOE
    n_tiles = (2 * n) // tm + N_EXPERTS
    cnt = counts[0, :N_EXPERTS].astype(jnp.int32)
    tiles_e = (cnt + tm - 1) // tm
    tile_end = jnp.cumsum(tiles_e)
    off = (tile_end - tiles_e) * tm
    n_used = tile_end[-1]
    tix = jnp.arange(n_tiles, dtype=jnp.int32)
    te = jnp.minimum(jnp.sum(tix[:, None] >= tile_end[None, :], axis=1), N_EXPERTS - 1).astype(jnp.int32)
    te_last = te[jnp.maximum(n_used - 1, 0)]
    tile_expert = jnp.where(tix < n_used, te, te_last)
    rows_left = cnt[te] - (tix - (tile_end - tiles_e)[te]) * tm
    tile_half = ((tix < n_used) & (rows_left <= tm // 2)).astype(jnp.int32)
    e1 = route[0].astype(jnp.int32)
    e2 = route[1].astype(jnp.int32)
    pos1 = off[e1] + route[4].astype(jnp.int32)
    pos2 = off[e2] + route[5].astype(jnp.int32)
    n_slots = n_tiles * tm
    pad_lo = jnp.concatenate([off + cnt, (n_used * tm).reshape(1)]).astype(jnp.int32)
    pad_hi = jnp.concatenate([off + tiles_e * tm, jnp.full((1,), n_slots, jnp.int32)]).astype(jnp.int32)
    fmap = _slot_map(jnp.concatenate([pos1, pos2]), pad_lo, pad_hi, n_slots)
    n_used_arr = n_used.reshape(1).astype(jnp.int32)

    y = _moe_experts(h2, fmap.reshape(n_tiles, 1, tm), tile_expert, n_used_arr, tile_half,
                     moe_w_gate[0], moe_w_up[0], moe_w_down[0], n_tiles, 2 * n + 2 * tm)
    out = _combine(y, x3.reshape(n, D), route, mod, final_gain, s)
    return out.reshape(b, s, D)
```

```python
import functools

import numpy as np
import jax
import jax.numpy as jnp
from jax import lax
from jax.experimental import pallas as pl
from jax.experimental.pallas import tpu as pltpu

F32 = jnp.float32
BF16 = jnp.bfloat16

D = 1024
GRID_W = 64
EPS = 1e-6
NEG_INF = -1e30
HEAD_DIM = 64
N_Q_HEADS = 8
BLOCK = 128
ATT_W = 512
KV_W = 128
GM_W = 512
IN_W = 1792
POOL_SIZES = (2, 4, 8, 16)
POOL_GD = 256
POOL_HALO = 16
N_EXPERTS = 8
ROPE_BASE = 10000.0
LANES = 128
SLAB = D // LANES
PACK = SLAB // 2
SQRT_2_OVER_PI = 0.7978845608028654

TM_IN = 1024
TM_FFN = 1024
TM_POOL = 1024
TM_MOE = 512
MOE_SPLIT = 2
W_CHUNK = 256
W_ROWS_IN = 128
W_ROWS_OUT = 512
W_RING_IN = 12
W_RING_OUT = 3
MOE_VMEM_LIMIT = 60 * 1024 * 1024
TC_COMB = 1024
Q_BLOCKS = 8
VMEM_LIMIT = 56 * 1024 * 1024


def _cparams(sem, vmem=None):
    return pltpu.CompilerParams(dimension_semantics=sem, vmem_limit_bytes=vmem)


def _modulate(xf, shift, scale):
    ms = jnp.mean(xf * xf, axis=-1, keepdims=True)
    return xf * lax.rsqrt(ms + EPS) * (1.0 + scale) + shift


def _sigmoid(z):
    return 1.0 / (1.0 + jnp.exp(-z))


def _rows_to_slabs(val, slab_ref):
    rows = val.shape[0]
    for cix in range(SLAB):
        slab_ref[pl.ds(cix, rows, stride=SLAB), :] = val[:, cix * LANES:(cix + 1) * LANES]


def _slabs_to_rows(slab_ref, rows):
    return jnp.concatenate([slab_ref[pl.ds(cix, rows, stride=SLAB), :] for cix in range(SLAB)], axis=1)


def _pack_rows(val, pack_ref):
    rows = val.shape[0]
    bits = lambda v: lax.bitcast_convert_type(v.astype(BF16).astype(F32), jnp.uint32)
    for cix in range(PACK):
        hi = bits(val[:, cix * LANES:(cix + 1) * LANES])
        lo = bits(val[:, D // 2 + cix * LANES:D // 2 + (cix + 1) * LANES])
        pack_ref[pl.ds(cix, rows, stride=PACK), :] = hi | (lo >> 16)


def _unpack_rows(pack_ref, rows):
    words = [pack_ref[pl.ds(cix, rows, stride=PACK), :] for cix in range(PACK)]
    his = [lax.bitcast_convert_type(w & jnp.uint32(0xFFFF0000), F32) for w in words]
    los = [lax.bitcast_convert_type(w << 16, F32) for w in words]
    return jnp.concatenate(his + los, axis=1)


def _ada_kernel(c_ref, w_ref, b_ref, o_ref):
    c = c_ref[...]
    s = c * _sigmoid(c)
    o_ref[0] = jnp.dot(s.astype(BF16), w_ref[0].astype(BF16), preferred_element_type=F32) + b_ref[0]


def _ada_mod(cvec, w_ada, b_ada):
    depth, _, n6 = w_ada.shape
    tn = 1536
    out = pl.pallas_call(
        _ada_kernel,
        grid=(depth, n6 // tn),
        in_specs=[
            pl.BlockSpec((8, D), lambda l, j: (0, 0)),
            pl.BlockSpec((1, D, tn), lambda l, j: (l, 0, j)),
            pl.BlockSpec((1, 1, tn), lambda l, j: (l, 0, j)),
        ],
        out_specs=pl.BlockSpec((1, 8, tn), lambda l, j: (l, 0, j)),
        out_shape=jax.ShapeDtypeStruct((depth, 8, n6), F32),
        compiler_params=_cparams(("arbitrary", "arbitrary")),
        name="ada_mod",
    )(cvec, w_ada, b_ada.reshape(depth, 1, n6))
    return out.reshape(depth, 8, 6, D)


def _rope(t, cs, sn, first_half):
    fwd = pltpu.roll(t, LANES - 16, axis=1)
    bwd = pltpu.roll(t, 16, axis=1)
    return t * cs + jnp.where(first_half, fwd, bwd) * sn


def _inproj_kernel(x_ref, mod_ref, w_ref, gain_ref, cos_ref, sin_ref,
                   q_ref, k_ref, ksw_ref, v_ref, vsw_ref, u_ref, vg_ref, w_bf):
    @pl.when((pl.program_id(0) == 0) & (pl.program_id(1) == 0))
    def _():
        w_bf[...] = w_ref[...].astype(BF16)

    mod = mod_ref[0, 0]
    h = _modulate(x_ref[0], mod[0:1], mod[1:2]).astype(BF16)
    proj = jnp.dot(h, w_bf[...], preferred_element_type=F32)
    cs = cos_ref[...]
    sn = sin_ref[...]
    lane = lax.broadcasted_iota(jnp.int32, cs.shape, 1)
    first_half = (lane & 16) == 0
    for cix in range(ATT_W // LANES):
        t = proj[:, cix * LANES:(cix + 1) * LANES]
        q_ref[0, :, cix * LANES:(cix + 1) * LANES] = (
            _rope(t, cs, sn, first_half) * (HEAD_DIM ** -0.5)).astype(BF16)
    kr = _rope(proj[:, ATT_W:ATT_W + KV_W], cs, sn, first_half)
    k_ref[0] = kr.astype(BF16)
    ksw_ref[0] = pltpu.roll(kr, HEAD_DIM, axis=1).astype(BF16)
    vv = proj[:, ATT_W + KV_W:ATT_W + 2 * KV_W]
    v_ref[0] = vv.astype(BF16)
    vsw_ref[0] = pltpu.roll(vv, HEAD_DIM, axis=1).astype(BF16)
    z = proj[:, ATT_W + 2 * KV_W:]
    g = z * (0.5 * (1.0 + jnp.tanh(SQRT_2_OVER_PI * (z + 0.044715 * (z * z * z)))))
    u_ref[0] = g[:, :GM_W].astype(BF16)
    vg = g[:, GM_W:]
    ms = jnp.mean(vg * vg, axis=-1, keepdims=True)
    vg_ref[0] = (vg * lax.rsqrt(ms + EPS) * gain_ref[...]).astype(BF16)


def _in_proj(x, mod, w_in, gm_gain, cos_t, sin_t):
    b, s, _ = x.shape
    tm = TM_IN
    row = lambda w: pl.BlockSpec((1, tm, w), lambda bi, i: (bi, i, 0))
    outs = pl.pallas_call(
        _inproj_kernel,
        grid=(b, s // tm),
        in_specs=[
            row(D),
            pl.BlockSpec((1, 1, 6, D), lambda bi, i: (0, bi, 0, 0)),
            pl.BlockSpec((D, IN_W), lambda bi, i: (0, 0), pipeline_mode=pl.Buffered(1)),
            pl.BlockSpec((1, GM_W), lambda bi, i: (0, 0)),
            pl.BlockSpec((tm, LANES), lambda bi, i: (i, 0)),
            pl.BlockSpec((tm, LANES), lambda bi, i: (i, 0)),
        ],
        out_specs=[row(ATT_W), row(KV_W), row(KV_W), row(KV_W), row(KV_W), row(GM_W), row(GM_W)],
        out_shape=[jax.ShapeDtypeStruct((b, s, w), BF16)
                   for w in (ATT_W, KV_W, KV_W, KV_W, KV_W, GM_W, GM_W)],
        scratch_shapes=[pltpu.VMEM((D, IN_W), BF16)],
        compiler_params=_cparams(("arbitrary", "arbitrary"), VMEM_LIMIT),
        name="in_proj",
    )(x, mod, w_in, gm_gain.reshape(1, GM_W), cos_t, sin_t)
    return outs


def _ctx_kernel(c_ref, mod_ref, w_ref, k_ref, ksw_ref, v_ref, vsw_ref):
    mod = mod_ref[0, 0]
    h = _modulate(c_ref[0], mod[0:1], mod[1:2]).astype(BF16)
    kv = jnp.dot(h, w_ref[...].astype(BF16), preferred_element_type=F32)
    kk = kv[:, :KV_W]
    vv = kv[:, KV_W:]
    k_ref[0] = kk.astype(BF16)
    ksw_ref[0] = pltpu.roll(kk, HEAD_DIM, axis=1).astype(BF16)
    v_ref[0] = vv.astype(BF16)
    vsw_ref[0] = pltpu.roll(vv, HEAD_DIM, axis=1).astype(BF16)


def _ctx_kv(ctx, mod, w_in):
    b, l, _ = ctx.shape
    spec = pl.BlockSpec((1, l, KV_W), lambda bi: (bi, 0, 0))
    return pl.pallas_call(
        _ctx_kernel,
        grid=(b,),
        in_specs=[
            pl.BlockSpec((1, l, D), lambda bi: (bi, 0, 0)),
            pl.BlockSpec((1, 1, 6, D), lambda bi: (0, b, 0, 0)),
            pl.BlockSpec((D, 2 * KV_W), lambda bi: (0, ATT_W // (2 * KV_W))),
        ],
        out_specs=[spec] * 4,
        out_shape=[jax.ShapeDtypeStruct((b, l, KV_W), BF16)] * 4,
        compiler_params=_cparams(("arbitrary",)),
        name="ctx_kv",
    )(ctx, mod, w_in)


def _attn_kernel(sink_ref, q_ref, kp_ref, kc_ref, kn_ref, ksp_ref, ksc_ref, ksn_ref,
                 vp_ref, vc_ref, vn_ref, vsp_ref, vsc_ref, vsn_ref,
                 kx_ref, ksx_ref, vx_ref, vsx_ref,
                 u_ref, vg_ref, wcat_ref, bs_ref, wout_ref, x_ref, mod_ref, o_ref, wout_bf):
    n = pl.program_id(1)

    @pl.when((pl.program_id(0) == 0) & (n == 0))
    def _():
        wout_bf[...] = wout_ref[...].astype(BF16)

    nblk = pl.num_programs(1) * Q_BLOCKS
    lane = lax.broadcasted_iota(jnp.int32, (1, LANES), 1)
    low = lane < HEAD_DIM
    zero = jnp.zeros((), BF16)

    def variants(a0, a1):
        return ((jnp.where(low, a0, zero), jnp.where(low, zero, a1)),
                (jnp.where(low, a1, zero), jnp.where(low, zero, a0)))

    cat = lambda refs: jnp.concatenate([r[0] for r in refs], axis=0)
    kb_var = variants(cat((kp_ref, kc_ref, kn_ref)), cat((ksp_ref, ksc_ref, ksn_ref)))
    vb_var = variants(cat((vp_ref, vc_ref, vn_ref)), cat((vsp_ref, vsc_ref, vsn_ref)))
    kx_var = variants(kx_ref[0], ksx_ref[0])
    vx_var = variants(vx_ref[0], vsx_ref[0])

    row = lax.broadcasted_iota(jnp.int32, (2 * BLOCK, BLOCK), 0) & (BLOCK - 1)
    col = lax.broadcasted_iota(jnp.int32, (2 * BLOCK, BLOCK), 1)
    top = lax.broadcasted_iota(jnp.int32, (2 * BLOCK, 1), 0) < BLOCK
    nt_dims = (((1,), (1,)), ((), ()))

    q = q_ref[0]
    att_blocks = [[None] * 4 for _ in range(Q_BLOCKS)]
    for kvh in range(2):
        qst = jnp.concatenate(
            [q[qb * BLOCK:(qb + 1) * BLOCK, pr * LANES:(pr + 1) * LANES]
             for qb in range(Q_BLOCKS) for pr in (2 * kvh, 2 * kvh + 1)], axis=0)
        accs = [None] * Q_BLOCKS
        for half in range(2):
            sk = jnp.where(top, sink_ref[4 * kvh + half], sink_ref[4 * kvh + 2 + half])
            s_ctx = lax.dot_general(qst, kx_var[kvh][half], nt_dims, preferred_element_type=F32)
            o_band, p_ctx, dens = [], [], []
            for qb in range(Q_BLOCKS):
                g = n * Q_BLOCKS + qb
                qrows = qst[qb * 2 * BLOCK:(qb + 1) * 2 * BLOCK]
                sb = lax.dot_general(qrows, kb_var[kvh][half][qb * BLOCK:(qb + 3) * BLOCK], nt_dims,
                                     preferred_element_type=F32)
                s0 = jnp.where((col >= row) & (g > 0), sb[:, :BLOCK], NEG_INF)
                s1 = sb[:, BLOCK:2 * BLOCK]
                s2 = jnp.where((col <= row) & (g < nblk - 1), sb[:, 2 * BLOCK:], NEG_INF)
                sc = s_ctx[qb * 2 * BLOCK:(qb + 1) * 2 * BLOCK]
                ctx_blocks = [sc[:, cb * BLOCK:(cb + 1) * BLOCK] for cb in range(sc.shape[1] // BLOCK)]
                m = functools.reduce(jnp.maximum, [s0, s1, s2] + ctx_blocks)
                m = jnp.maximum(jnp.max(m, axis=-1, keepdims=True), sk)
                p0, p1, p2, pc = (jnp.exp(t - m) for t in (s0, s1, s2, sc))
                psum = functools.reduce(
                    jnp.add, [p0, p1, p2] + [pc[:, cb * BLOCK:(cb + 1) * BLOCK] for cb in range(len(ctx_blocks))])
                den = jnp.sum(psum, axis=-1, keepdims=True) + jnp.exp(sk - m)
                pb = jnp.concatenate([p0, p1, p2], axis=1).astype(BF16)
                o_band.append(jnp.dot(pb, vb_var[kvh][half][qb * BLOCK:(qb + 3) * BLOCK],
                                      preferred_element_type=F32))
                p_ctx.append(pc.astype(BF16))
                dens.append(den)
            o_ctx = jnp.dot(jnp.concatenate(p_ctx, axis=0), vx_var[kvh][half], preferred_element_type=F32)
            for qb in range(Q_BLOCKS):
                o = (o_band[qb] + o_ctx[qb * 2 * BLOCK:(qb + 1) * 2 * BLOCK]) / dens[qb]
                accs[qb] = o if accs[qb] is None else accs[qb] + o
        for qb in range(Q_BLOCKS):
            att_blocks[qb][2 * kvh] = accs[qb][:BLOCK]
            att_blocks[qb][2 * kvh + 1] = accs[qb][BLOCK:]

    u = u_ref[0]
    vg = vg_ref[0]
    bs = bs_ref[...]
    gm_blocks = [[None] * 4 for _ in range(Q_BLOCKS)]
    for j in range(GM_W // LANES):
        chunks = [vg[c * BLOCK:(c + 1) * BLOCK, j * LANES:(j + 1) * LANES] for c in range(Q_BLOCKS)]
        rhs = jnp.concatenate(
            [jnp.concatenate([jnp.where(low, v, zero) for v in chunks], axis=1),
             jnp.concatenate([jnp.where(low, zero, v) for v in chunks], axis=1)], axis=0)
        mixed = jnp.dot(wcat_ref[j], rhs, preferred_element_type=F32)
        bias = jnp.where(low, bs[:, 2 * j:2 * j + 1], bs[:, 2 * j + 1:2 * j + 2])
        for c in range(Q_BLOCKS):
            gm_blocks[c][j] = (u[c * BLOCK:(c + 1) * BLOCK, j * LANES:(j + 1) * LANES].astype(F32)
                               * (mixed[:, c * LANES:(c + 1) * LANES] + bias))

    mix = jnp.concatenate([jnp.concatenate(att_blocks[c] + gm_blocks[c], axis=1) for c in range(Q_BLOCKS)],
                          axis=0).astype(BF16)
    y = jnp.dot(mix, wout_bf[...], preferred_element_type=F32)
    mod = mod_ref[0, 0]
    o_ref[0] = x_ref[0] + mod[2:3] * y


def _attn_mixer(x, mod, sink, q, k, ksw, v, vsw, kx, ksx, vx, vsx, u, vg, wcat_bf, bs_t, wout):
    b, s, _ = x.shape
    tq = Q_BLOCKS * BLOCK
    nb = s // BLOCK
    l = kx.shape[1]
    cur = lambda w: pl.BlockSpec((1, tq, w), lambda bi, n: (bi, n, 0))
    prv = lambda w: pl.BlockSpec((1, BLOCK, w), lambda bi, n: (bi, jnp.maximum(n * Q_BLOCKS - 1, 0), 0))
    nxt = lambda w: pl.BlockSpec((1, BLOCK, w), lambda bi, n: (bi, jnp.minimum((n + 1) * Q_BLOCKS, nb - 1), 0))
    cx = pl.BlockSpec((1, l, KV_W), lambda bi, n: (bi, 0, 0))
    return pl.pallas_call(
        _attn_kernel,
        grid=(b, s // tq),
        in_specs=[
            pl.BlockSpec(memory_space=pltpu.SMEM),
            cur(ATT_W),
            prv(KV_W), cur(KV_W), nxt(KV_W), prv(KV_W), cur(KV_W), nxt(KV_W),
            prv(KV_W), cur(KV_W), nxt(KV_W), prv(KV_W), cur(KV_W), nxt(KV_W),
            cx, cx, cx, cx,
            cur(GM_W), cur(GM_W),
            pl.BlockSpec((4, BLOCK, 2 * BLOCK), lambda bi, n: (0, 0, 0)),
            pl.BlockSpec((BLOCK, 8), lambda bi, n: (0, 0)),
            pl.BlockSpec((D, D), lambda bi, n: (0, 0), pipeline_mode=pl.Buffered(1)),
            cur(D),
            pl.BlockSpec((1, 1, 6, D), lambda bi, n: (0, bi, 0, 0)),
        ],
        out_specs=cur(D),
        out_shape=jax.ShapeDtypeStruct((b, s, D), F32),
        scratch_shapes=[pltpu.VMEM((D, D), BF16)],
        compiler_params=_cparams(("arbitrary", "arbitrary"), VMEM_LIMIT),
        name="attn_gmlp_out",
    )(sink, q, k, k, k, ksw, ksw, ksw, v, v, v, vsw, vsw, vsw, kx, ksx, vx, vsx,
      u, vg, wcat_bf, bs_t, wout, x, mod)


def _ffn_kernel(x_ref, mod_ref, wg_ref, wu_ref, wd_ref, o_ref, act):
    mod = mod_ref[0, 0]
    xf = x_ref[...]
    h = _modulate(xf, mod[3:4], mod[4:5]).astype(BF16)
    for c in range(wg_ref.shape[1] // W_CHUNK):
        cs = slice(c * W_CHUNK, (c + 1) * W_CHUNK)
        g = jnp.dot(h, wg_ref[:, cs], preferred_element_type=F32)
        up = jnp.dot(h, wu_ref[:, cs], preferred_element_type=F32)
        act[:, cs] = (g * _sigmoid(g) * up).astype(BF16)
    o_ref[...] = xf + mod[5:6] * jnp.dot(act[...], wd_ref[...], preferred_element_type=F32)


def _dense_ffn(x2d, mod, wg, wu, wd, seq):
    n = x2d.shape[0]
    f = wg.shape[1]
    tm = TM_FFN
    per_b = seq // tm
    resident = lambda shp: pl.BlockSpec(shp, lambda i: (0, 0), pipeline_mode=pl.Buffered(1))
    return pl.pallas_call(
        _ffn_kernel,
        grid=(n // tm,),
        in_specs=[
            pl.BlockSpec((tm, D), lambda i: (i, 0)),
            pl.BlockSpec((1, 1, 6, D), lambda i: (0, i // per_b, 0, 0)),
            resident((D, f)), resident((D, f)), resident((f, D)),
        ],
        out_specs=pl.BlockSpec((tm, D), lambda i: (i, 0)),
        out_shape=jax.ShapeDtypeStruct((n, D), F32),
        scratch_shapes=[pltpu.VMEM((tm, f), BF16)],
        compiler_params=_cparams(("arbitrary",), VMEM_LIMIT),
        name="dense_ffn",
    )(x2d, mod, wg, wu, wd)


def _pool_route_kernel(x_ref, xp_ref, xn_ref, mod_ref, band_ref, pw_ref, psc_ref, wr_hi_ref,
                       tri_ref, x3_ref, h2_ref, route_ref, cnt_ref, hext, carry):
    bi = pl.program_id(0)
    i = pl.program_id(1)
    ni = pl.num_programs(1)
    tm = x_ref.shape[1]
    seq = tm * ni
    mod = mod_ref[0, 0]

    @pl.when((bi == 0) & (i == 0))
    def _():
        carry[...] = jnp.zeros_like(carry)

    xf = x_ref[0]
    hp = _modulate(xp_ref[0], mod[0:1], mod[1:2])
    hn = _modulate(xn_ref[0], mod[0:1], mod[1:2])
    hext[0:POOL_HALO] = jnp.where(i > 0, hp, 0.0).astype(BF16)
    h_main = _modulate(xf, mod[0:1], mod[1:2])
    hext[POOL_HALO:POOL_HALO + tm] = h_main.astype(BF16)
    hext[POOL_HALO + tm:] = jnp.where(i < ni - 1, hn, 0.0).astype(BF16)

    t_local = lax.broadcasted_iota(jnp.int32, (BLOCK, 1), 0)
    ys = []
    for gi, w in enumerate(POOL_SIZES):
        lo_off = -(w // 2)
        hi_off = w - 1 - w // 2
        cols = slice(gi * POOL_GD, (gi + 1) * POOL_GD)
        outs = []
        for sb in range(tm // BLOCK):
            r0 = sb * BLOCK
            win = jnp.dot(band_ref[gi], hext[r0:r0 + BLOCK + 2 * POOL_HALO, cols],
                          preferred_element_type=F32)
            t = i * tm + r0 + t_local
            cnt = (jnp.minimum(t + hi_off, seq - 1) - jnp.maximum(t + lo_off, 0) + 1).astype(F32)
            diff = win / cnt - h_main[r0:r0 + BLOCK, cols]
            outs.append(diff.astype(BF16))
        dg = jnp.concatenate(outs, axis=0)
        ys.append(jnp.dot(dg, pw_ref[gi], preferred_element_type=F32))
    y = jnp.concatenate(ys, axis=1) * psc_ref[...]
    x3 = xf + mod[2:3] * y
    x3_ref[0] = x3

    h2 = _modulate(x3, mod[3:4], mod[4:5])
    _rows_to_slabs(h2, h2_ref)
    logits = jnp.dot(h2.astype(BF16), wr_hi_ref[...], preferred_element_type=F32)
    lane = lax.broadcasted_iota(jnp.int32, (tm, LANES), 1)
    lane_f = lane.astype(F32)
    neg = -jnp.inf
    lg = jnp.where(lane < N_EXPERTS, logits, neg)
    m1 = jnp.max(lg, axis=-1, keepdims=True)
    i1 = jnp.min(jnp.where(lg == m1, lane_f, float(LANES)), axis=-1, keepdims=True)
    oh1 = lane_f == i1
    lg2 = jnp.where(oh1, neg, lg)
    m2 = jnp.max(lg2, axis=-1, keepdims=True)
    i2 = jnp.min(jnp.where(lg2 == m2, lane_f, float(LANES)), axis=-1, keepdims=True)
    oh2 = lane_f == i2
    e = jnp.exp(m2 - m1)
    w1 = 1.0 / (1.0 + e)
    w2 = e / (1.0 + e)
    oh = jnp.where(oh1 | oh2, 1.0, 0.0)
    before = jnp.dot(tri_ref[...], oh.astype(BF16), preferred_element_type=F32) + carry[...]
    r1 = jnp.sum(jnp.where(oh1, before, 0.0), axis=-1, keepdims=True)
    r2 = jnp.sum(jnp.where(oh2, before, 0.0), axis=-1, keepdims=True)
    carry[...] = carry[...] + jnp.sum(oh, axis=0, keepdims=True)
    cnt_ref[...] = carry[...]
    info = jnp.where(lane == 0, i1, jnp.where(lane == 1, i2, jnp.where(lane == 2, w1, jnp.where(
        lane == 3, w2, jnp.where(lane == 4, r1, jnp.where(lane == 5, r2, 0.0))))))
    route_ref[...] = info.T[0:8, :]


def _pool_route(x, mod, band, pw_bf, pool_scale, wr_hi, tri):
    b, s, _ = x.shape
    tm = TM_POOL
    ni = s // tm
    hb = tm // POOL_HALO
    row = pl.BlockSpec((1, tm, D), lambda bi, i: (bi, i, 0))
    const2 = lambda shp: pl.BlockSpec(shp, lambda bi, i: (0,) * len(shp))
    return pl.pallas_call(
        _pool_route_kernel,
        grid=(b, ni),
        in_specs=[
            row,
            pl.BlockSpec((1, POOL_HALO, D), lambda bi, i: (bi, jnp.maximum(i * hb - 1, 0), 0)),
            pl.BlockSpec((1, POOL_HALO, D), lambda bi, i: (bi, jnp.minimum((i + 1) * hb, s // POOL_HALO - 1), 0)),
            pl.BlockSpec((1, 1, 6, D), lambda bi, i: (1, bi, 0, 0)),
            const2(band.shape), const2(pw_bf.shape), const2((1, D)),
            const2(wr_hi.shape), const2(tri.shape),
        ],
        out_specs=[row,
                   pl.BlockSpec((tm * SLAB, LANES), lambda bi, i: (bi * ni + i, 0)),
                   pl.BlockSpec((8, tm), lambda bi, i: (0, bi * ni + i)),
                   pl.BlockSpec((1, LANES), lambda bi, i: (0, 0))],
        out_shape=[jax.ShapeDtypeStruct((b, s, D), F32), jax.ShapeDtypeStruct((b * s * SLAB, LANES), F32),
                   jax.ShapeDtypeStruct((8, b * s), F32), jax.ShapeDtypeStruct((1, LANES), F32)],
        scratch_shapes=[pltpu.VMEM((tm + 2 * POOL_HALO, D), BF16), pltpu.VMEM((1, LANES), F32)],
        compiler_params=_cparams(("arbitrary", "arbitrary"), VMEM_LIMIT),
        name="pool_route",
    )(x, x, x, mod, band, pw_bf, pool_scale.reshape(1, D), wr_hi, tri)


def _slotmap_kernel(pos_ref, lo_ref, hi_ref, o_ref):
    n_pairs = pos_ref.shape[0]
    spare_mask = 2 * TM_MOE - 1
    for e in range(lo_ref.shape[0]):
        def fill(p, c):
            o_ref[p] = n_pairs + (p & spare_mask)
            return c
        lax.fori_loop(lo_ref[e], hi_ref[e], fill, 0)

    def place(f, c):
        o_ref[pos_ref[f]] = f
        return c
    lax.fori_loop(0, n_pairs, place, 0, unroll=32)


def _slot_map(pos_flat, lo, hi, n_slots):
    smem = pl.BlockSpec(memory_space=pltpu.SMEM)
    return pl.pallas_call(
        _slotmap_kernel,
        in_specs=[smem, smem, smem],
        out_specs=smem,
        out_shape=jax.ShapeDtypeStruct((n_slots,), jnp.int32),
        name="moe_slot_map",
    )(pos_flat, lo, hi)


def _moe_kernel(te_ref, nused_ref, half_ref, fnext_ref, fprev_ref, f0_ref, h_hbm, wg_hbm, wu_hbm, wd_hbm,
                y_hbm, wg_res, wu_res, wd_res, stg_in, stg_out, xbuf, xb, act, acc, stage, gsem, ssem, wsem):
    i = pl.program_id(0)
    nt = pl.num_programs(0)
    used_tiles = nused_ref[0]
    tm = xb.shape[0]
    f_dim = wg_res.shape[1]
    tok_mask = h_hbm.shape[0] // SLAB - 1
    tile_rows = tm * SLAB
    out_rows = tm * PACK

    def slab(ix, width=SLAB):
        return pl.ds(pl.multiple_of(ix * width, width), width)

    def gather_row(fref, r, slot, zero=0):
        tok = fref[0, 0, r + zero] & tok_mask
        return pltpu.make_async_copy(h_hbm.at[slab(tok)], xbuf.at[slot, slab(r)], gsem.at[slot])

    def scatter_row(fref, r, slot, zero=0):
        return pltpu.make_async_copy(stage.at[slot, slab(r, PACK)], y_hbm.at[slab(fref[0, 0, r + zero], PACK)],
                                     ssem.at[slot])

    def gather_all(slot):
        return pltpu.make_async_copy(h_hbm.at[pl.ds(0, tile_rows)], xbuf.at[slot], gsem.at[slot])

    def scatter_all(slot):
        return pltpu.make_async_copy(stage.at[slot], y_hbm.at[pl.ds(0, out_rows)], ssem.at[slot])

    cur = i % 2
    used = i < used_tiles

    @pl.when(i == 0)
    def _():
        stage[...] = jnp.zeros_like(stage)
        spare = y_hbm.shape[0] - 2 * out_rows
        fills = [pltpu.make_async_copy(stage.at[sl], y_hbm.at[pl.ds(spare + sl * out_rows, out_rows)],
                                       ssem.at[sl])
                 for sl in range(2)]
        for cp in fills:
            cp.start()
        for cp in fills:
            cp.wait()

    @pl.when((i == 0) & used)
    def _():
        def prime(r, c):
            gather_row(f0_ref, r, 0).start()
            return c
        lax.fori_loop(0, tm, prime, 0)

    expert = te_ref[i]
    new_expert = used & ((i == 0) | (expert != te_ref[jnp.maximum(i - 1, 0)]))

    has_next = i + 1 < used_tiles
    has_prev = (i >= 1) & (i - 1 < used_tiles)
    steady = (i >= 1) & has_next
    half_tile = half_ref[i] == 1

    @pl.when(used)
    def _():
        gather_all(cur).wait()

    @pl.when((i >= 2) & (i - 2 < used_tiles))
    def _():
        scatter_all(cur).wait()

    half = f_dim // MOE_SPLIT
    n_piece = half // W_CHUNK
    n_groups = MOE_SPLIT * (n_piece + 1)

    def tick(v):
        bits = jnp.max(lax.bitcast_convert_type(v[0:SLAB, 0:LANES], jnp.int32))
        return lax.shift_right_logical(lax.shift_right_logical(bits, 16), 16)

    def expert_ffn(issue_group, rows=tm):
        for cix in range(SLAB):
            xb[0:rows, cix * LANES:(cix + 1) * LANES] = (
                xbuf.at[cur][pl.ds(cix, rows, stride=SLAB), :].astype(BF16))
        xv = xb[0:rows]
        out = None
        issue_group(0, 0)
        k = 1
        for hf in range(MOE_SPLIT):
            for c in range(n_piece):
                cs = slice(hf * half + c * W_CHUNK, hf * half + (c + 1) * W_CHUNK)
                g = jnp.dot(xv, wg_res[:, cs], preferred_element_type=F32)
                up = jnp.dot(xv, wu_res[:, cs], preferred_element_type=F32)
                act[0:rows, c * W_CHUNK:(c + 1) * W_CHUNK] = (g * _sigmoid(g) * up).astype(BF16)
                issue_group(k, tick(g))
                k += 1
            part = jnp.dot(act[0:rows], wd_res[hf * half:(hf + 1) * half, :], preferred_element_type=F32)
            if hf < MOE_SPLIT - 1:
                acc[0:rows] = part if out is None else acc[0:rows] + part
                out = acc
                issue_group(k, tick(part))
                k += 1
            else:
                _pack_rows(part if out is None else acc[0:rows] + part, stage.at[cur])

    def chunk_plan():
        n_in = D // W_ROWS_IN
        ins = [(w, res, stg_in, W_RING_IN, 0, W_ROWS_IN, c, wi * n_in + c)
               for wi, (w, res) in enumerate(((wg_hbm, wg_res), (wu_hbm, wu_res))) for c in range(n_in)]
        outs = [(wd_hbm, wd_res, stg_out, W_RING_OUT, W_RING_IN, W_ROWS_OUT, c, c)
                for c in range(f_dim // W_ROWS_OUT)]
        return outs[:W_RING_OUT] + ins + outs[W_RING_OUT:]

    def chunk_copy(entry, ex):
        w_hbm, _, stg, ring, sem0, rows, c, j = entry
        return pltpu.make_async_copy(w_hbm.at[ex, pl.ds(c * rows, rows), :], stg.at[j % ring],
                                     wsem.at[sem0 + j % ring])

    def start_first_chunks(ex):
        for entry in chunk_plan():
            if entry[7] < entry[3]:
                chunk_copy(entry, ex).start()

    @pl.when(new_expert)
    def _():
        plan = chunk_plan()

        @pl.when(i == 0)
        def _():
            start_first_chunks(expert)

        for k, entry in enumerate(plan):
            _, res, stg, ring, _, rows, c, j = entry
            chunk_copy(entry, expert).wait()
            res[c * rows:(c + 1) * rows, :] = stg[j % ring].astype(BF16)
            later = [e for e in plan[k + 1:] if e[2] is stg and e[7] == j + ring]
            if later:
                chunk_copy(later[0], expert).start()

    next_expert = te_ref[jnp.minimum(i + 1, nt - 1)]

    @pl.when(has_next & (next_expert != expert))
    def _():
        start_first_chunks(next_expert)

    def issue_group(k, zero):
        for r in range(k * tm // n_groups, (k + 1) * tm // n_groups):
            gather_row(fnext_ref, r, 1 - cur, zero).start()
            scatter_row(fprev_ref, r, 1 - cur, zero).start()

    @pl.when(steady & jnp.logical_not(half_tile))
    def _():
        expert_ffn(issue_group)

    @pl.when(steady & half_tile)
    def _():
        expert_ffn(issue_group, tm // 2)

    @pl.when(jnp.logical_not(steady))
    def _():
        @pl.when(used)
        def _():
            expert_ffn(lambda k, zero: None)

        @pl.when(has_next)
        def _():
            def issue(r, c):
                gather_row(fnext_ref, r, 1 - cur).start()
                return c
            lax.fori_loop(0, tm, issue, 0)

        @pl.when(has_prev)
        def _():
            def issue(r, c):
                scatter_row(fprev_ref, r, 1 - cur).start()
                return c
            lax.fori_loop(0, tm, issue, 0)

    @pl.when((i == nt - 1) & (nt - 2 < used_tiles))
    def _():
        scatter_all(1 - cur).wait()


def _moe_experts(h_slabs, fmap, tile_expert, n_used, tile_half, wg, wu, wd, n_tiles, y_rows):
    f = wg.shape[2]
    tm = TM_MOE
    fblk = lambda imap: pl.BlockSpec((1, 1, tm), imap, memory_space=pltpu.SMEM)
    hbm = pl.BlockSpec(memory_space=pl.ANY)
    return pl.pallas_call(
        _moe_kernel,
        grid_spec=pltpu.PrefetchScalarGridSpec(
            num_scalar_prefetch=3,
            grid=(n_tiles,),
            in_specs=[
                fblk(lambda i, te, nu, hf: (jnp.minimum(i + 1, n_tiles - 1), 0, 0)),
                fblk(lambda i, te, nu, hf: (jnp.maximum(i - 1, 0), 0, 0)),
                fblk(lambda i, te, nu, hf: (0, 0, 0)),
                hbm, hbm, hbm, hbm,
            ],
            out_specs=hbm,
            scratch_shapes=[pltpu.VMEM((D, f), BF16), pltpu.VMEM((D, f), BF16), pltpu.VMEM((f, D), BF16),
                            pltpu.VMEM((W_RING_IN, W_ROWS_IN, f), F32), pltpu.VMEM((W_RING_OUT, W_ROWS_OUT, D), F32),
                            pltpu.VMEM((2, tm * SLAB, LANES), F32), pltpu.VMEM((tm, D), BF16),
                            pltpu.VMEM((tm, f // MOE_SPLIT), BF16), pltpu.VMEM((tm, D), F32),
                            pltpu.VMEM((2, tm * PACK, LANES), jnp.uint32),
                            pltpu.SemaphoreType.DMA((2,)), pltpu.SemaphoreType.DMA((2,)),
                            pltpu.SemaphoreType.DMA((W_RING_IN + W_RING_OUT,))],
        ),
        out_shape=jax.ShapeDtypeStruct((y_rows * PACK, LANES), jnp.uint32),
        compiler_params=_cparams(("arbitrary",), MOE_VMEM_LIMIT),
        name="moe_experts",
    )(tile_expert, n_used, tile_half, fmap, fmap, fmap, h_slabs, wg, wu, wd)


def _combine_kernel(y1_ref, y2_ref, x_ref, w_ref, mod_ref, gain_ref, o_ref):
    rows = x_ref.shape[0]
    info = jnp.concatenate([w_ref[...], jnp.zeros((LANES - w_ref.shape[0], rows), F32)], axis=0).T
    moe = info[:, 2:3] * _unpack_rows(y1_ref, rows) + info[:, 3:4] * _unpack_rows(y2_ref, rows)
    mod = mod_ref[0, 0]
    x4 = x_ref[...] + mod[5:6] * moe
    ms = jnp.mean(x4 * x4, axis=-1, keepdims=True)
    o_ref[...] = x4 * lax.rsqrt(ms + EPS) * gain_ref[...]


def _combine(y, x3_2d, wts, mod, final_gain, seq):
    n = x3_2d.shape[0]
    tc = TC_COMB
    nt = n // tc
    per_b = seq // tc
    return pl.pallas_call(
        _combine_kernel,
        grid=(nt,),
        in_specs=[
            pl.BlockSpec((tc * PACK, LANES), lambda i: (i, 0)),
            pl.BlockSpec((tc * PACK, LANES), lambda i: (i + nt, 0)),
            pl.BlockSpec((tc, D), lambda i: (i, 0)),
            pl.BlockSpec((8, tc), lambda i: (0, i)),
            pl.BlockSpec((1, 1, 6, D), lambda i: (1, i // per_b, 0, 0)),
            pl.BlockSpec((1, D), lambda i: (0, 0)),
        ],
        out_specs=pl.BlockSpec((tc, D), lambda i: (i, 0)),
        out_shape=jax.ShapeDtypeStruct((n, D), F32),
        compiler_params=_cparams(("arbitrary",)),
        name="moe_combine",
    )(y, y, x3_2d, wts, mod, final_gain.reshape(1, D))


def _rope_tables(seq):
    rows = seq // GRID_W
    row_pos = jnp.repeat(jnp.arange(rows, dtype=F32), GRID_W)
    col_pos = jnp.tile(jnp.arange(GRID_W, dtype=F32), rows)
    axis_dim = HEAD_DIM // 2
    inv_freq = ROPE_BASE ** (-jnp.arange(0, axis_dim, 2, dtype=F32) / axis_dim)
    ar = row_pos[:, None] * inv_freq
    ac = col_pos[:, None] * inv_freq
    cos64 = jnp.concatenate([jnp.cos(ar), jnp.cos(ar), jnp.cos(ac), jnp.cos(ac)], axis=1)
    sin64 = jnp.concatenate([-jnp.sin(ar), jnp.sin(ar), -jnp.sin(ac), jnp.sin(ac)], axis=1)
    return jnp.tile(cos64, (1, 2)), jnp.tile(sin64, (1, 2))


def _band_matrices():
    r = np.arange(BLOCK)[:, None]
    c = np.arange(BLOCK + 2 * POOL_HALO)[None, :] - POOL_HALO
    mats = []
    for w in POOL_SIZES:
        lo = -(w // 2)
        hi = w - 1 - w // 2
        mats.append(((c >= r + lo) & (c <= r + hi)).astype(np.float32))
    return jnp.asarray(np.stack(mats), dtype=BF16)


def kernel(x, c, ctx, c_ctx, w_ada, b_ada, w_in, attn_sink, gm_gain, gm_w_s, gm_b_s, w_out,
           ffn_w_gate, ffn_w_up, ffn_w_down, pool_w, pool_scale, router_w,
           moe_w_gate, moe_w_up, moe_w_down, final_gain):
    b, s, _ = x.shape
    n = b * s
    assert w_ada.shape[0] == 2 and w_in.shape[0] == 1 and pool_w.shape[0] == 1
    assert s % TM_IN == 0 and s % TM_POOL == 0 and s % TM_FFN == 0 and b <= 4
    assert n & (n - 1) == 0

    cvec = jnp.concatenate([c, c_ctx[None, :], jnp.zeros((8 - b - 1, D), F32)], axis=0)
    mod = _ada_mod(cvec, w_ada, b_ada)

    cos_t, sin_t = _rope_tables(s)
    q, k, ksw, v, vsw, u, vg = _in_proj(x, mod, w_in[0], gm_gain[0], cos_t, sin_t)
    kx, ksx, vx, vsx = _ctx_kv(ctx, mod, w_in[0])
    wcat = gm_w_s[0].reshape(4, 2, BLOCK, BLOCK).transpose(0, 2, 1, 3).reshape(4, BLOCK, 2 * BLOCK).astype(BF16)
    x1 = _attn_mixer(x, mod, attn_sink[0], q, k, ksw, v, vsw, kx, ksx, vx, vsx, u, vg,
                     wcat, gm_b_s[0].T, w_out[0])
    x2 = _dense_ffn(x1.reshape(n, D), mod, ffn_w_gate[0].astype(BF16), ffn_w_up[0].astype(BF16),
                    ffn_w_down[0].astype(BF16), s)

    wr = jnp.pad(router_w[0], ((0, 0), (0, LANES - N_EXPERTS)))
    wr_hi = wr.astype(BF16)
    tri = jnp.asarray(np.tril(np.ones((TM_POOL, TM_POOL), np.float32), -1), dtype=BF16)
    x3, h2, route, counts = _pool_route(x2.reshape(b, s, D), mod, _band_matrices(), pool_w[0].astype(BF16),
                                        pool_scale[0], wr_hi, tri)

    tm = TM_MOE
    n_tiles = (2 * n) // tm + N_EXPERTS
    cnt = counts[0, :N_EXPERTS].astype(jnp.int32)
    tiles_e = (cnt + tm - 1) // tm
    tile_end = jnp.cumsum(tiles_e)
    off = (tile_end - tiles_e) * tm
    n_used = tile_end[-1]
    tix = jnp.arange(n_tiles, dtype=jnp.int32)
    te = jnp.minimum(jnp.sum(tix[:, None] >= tile_end[None, :], axis=1), N_EXPERTS - 1).astype(jnp.int32)
    te_last = te[jnp.maximum(n_used - 1, 0)]
    tile_expert = jnp.where(tix < n_used, te, te_last)
    rows_left = cnt[te] - (tix - (tile_end - tiles_e)[te]) * tm
    tile_half = ((tix < n_used) & (rows_left <= tm // 2)).astype(jnp.int32)
    e1 = route[0].astype(jnp.int32)
    e2 = route[1].astype(jnp.int32)
    pos1 = off[e1] + route[4].astype(jnp.int32)
    pos2 = off[e2] + route[5].astype(jnp.int32)
    n_slots = n_tiles * tm
    pad_lo = jnp.concatenate([off + cnt, (n_used * tm).reshape(1)]).astype(jnp.int32)
    pad_hi = jnp.concatenate([off + tiles_e * tm, jnp.full((1,), n_slots, jnp.int32)]).astype(jnp.int32)
    fmap = _slot_map(jnp.concatenate([pos1, pos2]), pad_lo, pad_hi, n_slots)
    n_used_arr = n_used.reshape(1).astype(jnp.int32)

    y = _moe_experts(h2, fmap.reshape(n_tiles, 1, tm), tile_expert, n_used_arr, tile_half,
                     moe_w_gate[0], moe_w_up[0], moe_w_down[0], n_tiles, 2 * n + 2 * tm)
    out = _combine(y, x3.reshape(n, D), route, mod, final_gain, s)
    return out.reshape(b, s, D)
```

```python
import functools

import numpy as np
import jax
import jax.numpy as jnp
from jax import lax
from jax.experimental import pallas as pl
from jax.experimental.pallas import tpu as pltpu

F32 = jnp.float32
BF16 = jnp.bfloat16

D = 1024
GRID_W = 64
EPS = 1e-6
NEG_INF = -1e30
HEAD_DIM = 64
N_Q_HEADS = 8
BLOCK = 128
ATT_W = 512
KV_W = 128
GM_W = 512
IN_W = 1792
POOL_SIZES = (2, 4, 8, 16)
POOL_GD = 256
POOL_HALO = 16
N_EXPERTS = 8
ROPE_BASE = 10000.0
LANES = 128
SLAB = D // LANES
PACK = SLAB // 2
SQRT_2_OVER_PI = 0.7978845608028654

TM_IN = 1024
TM_FFN = 1024
TM_POOL = 1024
TM_MOE = 512
MOE_SPLIT = 2
W_CHUNK = 256
W_ROWS_IN = 128
W_ROWS_OUT = 512
W_RING_IN = 13
W_RING_OUT = 3
MOE_VMEM_LIMIT = 62 * 1024 * 1024
TC_COMB = 1024
Q_BLOCKS = 8
VMEM_LIMIT = 56 * 1024 * 1024


def _cparams(sem, vmem=None):
    return pltpu.CompilerParams(dimension_semantics=sem, vmem_limit_bytes=vmem)


def _modulate(xf, shift, scale):
    ms = jnp.mean(xf * xf, axis=-1, keepdims=True)
    return xf * lax.rsqrt(ms + EPS) * (1.0 + scale) + shift


def _sigmoid(z):
    return 1.0 / (1.0 + jnp.exp(-z))


def _rows_to_slabs(val, slab_ref):
    rows = val.shape[0]
    for cix in range(SLAB):
        slab_ref[pl.ds(cix, rows, stride=SLAB), :] = val[:, cix * LANES:(cix + 1) * LANES]


def _slabs_to_rows(slab_ref, rows):
    return jnp.concatenate([slab_ref[pl.ds(cix, rows, stride=SLAB), :] for cix in range(SLAB)], axis=1)


def _pack_rows(val, pack_ref):
    rows = val.shape[0]
    bits = lambda v: lax.bitcast_convert_type(v.astype(BF16).astype(F32), jnp.uint32)
    for cix in range(PACK):
        hi = bits(val[:, cix * LANES:(cix + 1) * LANES])
        lo = bits(val[:, D // 2 + cix * LANES:D // 2 + (cix + 1) * LANES])
        pack_ref[pl.ds(cix, rows, stride=PACK), :] = hi | (lo >> 16)


def _unpack_rows(pack_ref, rows):
    words = [pack_ref[pl.ds(cix, rows, stride=PACK), :] for cix in range(PACK)]
    his = [lax.bitcast_convert_type(w & jnp.uint32(0xFFFF0000), F32) for w in words]
    los = [lax.bitcast_convert_type(w << 16, F32) for w in words]
    return jnp.concatenate(his + los, axis=1)


def _ada_kernel(c_ref, w_ref, b_ref, o_ref):
    c = c_ref[...]
    s = c * _sigmoid(c)
    o_ref[0] = jnp.dot(s.astype(BF16), w_ref[0].astype(BF16), preferred_element_type=F32) + b_ref[0]


def _ada_mod(cvec, w_ada, b_ada):
    depth, _, n6 = w_ada.shape
    tn = 1536
    out = pl.pallas_call(
        _ada_kernel,
        grid=(depth, n6 // tn),
        in_specs=[
            pl.BlockSpec((8, D), lambda l, j: (0, 0)),
            pl.BlockSpec((1, D, tn), lambda l, j: (l, 0, j)),
            pl.BlockSpec((1, 1, tn), lambda l, j: (l, 0, j)),
        ],
        out_specs=pl.BlockSpec((1, 8, tn), lambda l, j: (l, 0, j)),
        out_shape=jax.ShapeDtypeStruct((depth, 8, n6), F32),
        compiler_params=_cparams(("arbitrary", "arbitrary")),
        name="ada_mod",
    )(cvec, w_ada, b_ada.reshape(depth, 1, n6))
    return out.reshape(depth, 8, 6, D)


def _rope(t, cs, sn, first_half):
    fwd = pltpu.roll(t, LANES - 16, axis=1)
    bwd = pltpu.roll(t, 16, axis=1)
    return t * cs + jnp.where(first_half, fwd, bwd) * sn


def _inproj_kernel(x_ref, mod_ref, w_ref, gain_ref, cos_ref, sin_ref,
                   q_ref, k_ref, ksw_ref, v_ref, vsw_ref, u_ref, vg_ref, w_bf):
    @pl.when((pl.program_id(0) == 0) & (pl.program_id(1) == 0))
    def _():
        w_bf[...] = w_ref[...].astype(BF16)

    mod = mod_ref[0, 0]
    h = _modulate(x_ref[0], mod[0:1], mod[1:2]).astype(BF16)
    proj = jnp.dot(h, w_bf[...], preferred_element_type=F32)
    cs = cos_ref[...]
    sn = sin_ref[...]
    lane = lax.broadcasted_iota(jnp.int32, cs.shape, 1)
    first_half = (lane & 16) == 0
    for cix in range(ATT_W // LANES):
        t = proj[:, cix * LANES:(cix + 1) * LANES]
        q_ref[0, :, cix * LANES:(cix + 1) * LANES] = (
            _rope(t, cs, sn, first_half) * (HEAD_DIM ** -0.5)).astype(BF16)
    kr = _rope(proj[:, ATT_W:ATT_W + KV_W], cs, sn, first_half)
    k_ref[0] = kr.astype(BF16)
    ksw_ref[0] = pltpu.roll(kr, HEAD_DIM, axis=1).astype(BF16)
    vv = proj[:, ATT_W + KV_W:ATT_W + 2 * KV_W]
    v_ref[0] = vv.astype(BF16)
    vsw_ref[0] = pltpu.roll(vv, HEAD_DIM, axis=1).astype(BF16)
    z = proj[:, ATT_W + 2 * KV_W:]
    g = z * (0.5 * (1.0 + jnp.tanh(SQRT_2_OVER_PI * (z + 0.044715 * (z * z * z)))))
    u_ref[0] = g[:, :GM_W].astype(BF16)
    vg = g[:, GM_W:]
    ms = jnp.mean(vg * vg, axis=-1, keepdims=True)
    vg_ref[0] = (vg * lax.rsqrt(ms + EPS) * gain_ref[...]).astype(BF16)


def _in_proj(x, mod, w_in, gm_gain, cos_t, sin_t):
    b, s, _ = x.shape
    tm = TM_IN
    row = lambda w: pl.BlockSpec((1, tm, w), lambda bi, i: (bi, i, 0))
    outs = pl.pallas_call(
        _inproj_kernel,
        grid=(b, s // tm),
        in_specs=[
            row(D),
            pl.BlockSpec((1, 1, 6, D), lambda bi, i: (0, bi, 0, 0)),
            pl.BlockSpec((D, IN_W), lambda bi, i: (0, 0), pipeline_mode=pl.Buffered(1)),
            pl.BlockSpec((1, GM_W), lambda bi, i: (0, 0)),
            pl.BlockSpec((tm, LANES), lambda bi, i: (i, 0)),
            pl.BlockSpec((tm, LANES), lambda bi, i: (i, 0)),
        ],
        out_specs=[row(ATT_W), row(KV_W), row(KV_W), row(KV_W), row(KV_W), row(GM_W), row(GM_W)],
        out_shape=[jax.ShapeDtypeStruct((b, s, w), BF16)
                   for w in (ATT_W, KV_W, KV_W, KV_W, KV_W, GM_W, GM_W)],
        scratch_shapes=[pltpu.VMEM((D, IN_W), BF16)],
        compiler_params=_cparams(("arbitrary", "arbitrary"), VMEM_LIMIT),
        name="in_proj",
    )(x, mod, w_in, gm_gain.reshape(1, GM_W), cos_t, sin_t)
    return outs


def _ctx_kernel(c_ref, mod_ref, w_ref, k_ref, ksw_ref, v_ref, vsw_ref):
    mod = mod_ref[0, 0]
    h = _modulate(c_ref[0], mod[0:1], mod[1:2]).astype(BF16)
    kv = jnp.dot(h, w_ref[...].astype(BF16), preferred_element_type=F32)
    kk = kv[:, :KV_W]
    vv = kv[:, KV_W:]
    k_ref[0] = kk.astype(BF16)
    ksw_ref[0] = pltpu.roll(kk, HEAD_DIM, axis=1).astype(BF16)
    v_ref[0] = vv.astype(BF16)
    vsw_ref[0] = pltpu.roll(vv, HEAD_DIM, axis=1).astype(BF16)


def _ctx_kv(ctx, mod, w_in):
    b, l, _ = ctx.shape
    spec = pl.BlockSpec((1, l, KV_W), lambda bi: (bi, 0, 0))
    return pl.pallas_call(
        _ctx_kernel,
        grid=(b,),
        in_specs=[
            pl.BlockSpec((1, l, D), lambda bi: (bi, 0, 0)),
            pl.BlockSpec((1, 1, 6, D), lambda bi: (0, b, 0, 0)),
            pl.BlockSpec((D, 2 * KV_W), lambda bi: (0, ATT_W // (2 * KV_W))),
        ],
        out_specs=[spec] * 4,
        out_shape=[jax.ShapeDtypeStruct((b, l, KV_W), BF16)] * 4,
        compiler_params=_cparams(("arbitrary",)),
        name="ctx_kv",
    )(ctx, mod, w_in)


def _attn_kernel(sink_ref, q_ref, kp_ref, kc_ref, kn_ref, ksp_ref, ksc_ref, ksn_ref,
                 vp_ref, vc_ref, vn_ref, vsp_ref, vsc_ref, vsn_ref,
                 kx_ref, ksx_ref, vx_ref, vsx_ref,
                 u_ref, vg_ref, wcat_ref, bs_ref, wout_ref, x_ref, mod_ref, o_ref, wout_bf):
    n = pl.program_id(1)

    @pl.when((pl.program_id(0) == 0) & (n == 0))
    def _():
        wout_bf[...] = wout_ref[...].astype(BF16)

    nblk = pl.num_programs(1) * Q_BLOCKS
    lane = lax.broadcasted_iota(jnp.int32, (1, LANES), 1)
    low = lane < HEAD_DIM
    zero = jnp.zeros((), BF16)

    def variants(a0, a1):
        return ((jnp.where(low, a0, zero), jnp.where(low, zero, a1)),
                (jnp.where(low, a1, zero), jnp.where(low, zero, a0)))

    cat = lambda refs: jnp.concatenate([r[0] for r in refs], axis=0)
    kb_var = variants(cat((kp_ref, kc_ref, kn_ref)), cat((ksp_ref, ksc_ref, ksn_ref)))
    vb_var = variants(cat((vp_ref, vc_ref, vn_ref)), cat((vsp_ref, vsc_ref, vsn_ref)))
    kx_var = variants(kx_ref[0], ksx_ref[0])
    vx_var = variants(vx_ref[0], vsx_ref[0])

    row = lax.broadcasted_iota(jnp.int32, (2 * BLOCK, BLOCK), 0) & (BLOCK - 1)
    col = lax.broadcasted_iota(jnp.int32, (2 * BLOCK, BLOCK), 1)
    top = lax.broadcasted_iota(jnp.int32, (2 * BLOCK, 1), 0) < BLOCK
    nt_dims = (((1,), (1,)), ((), ()))

    q = q_ref[0]
    att_blocks = [[None] * 4 for _ in range(Q_BLOCKS)]
    for kvh in range(2):
        qst = jnp.concatenate(
            [q[qb * BLOCK:(qb + 1) * BLOCK, pr * LANES:(pr + 1) * LANES]
             for qb in range(Q_BLOCKS) for pr in (2 * kvh, 2 * kvh + 1)], axis=0)
        accs = [None] * Q_BLOCKS
        for half in range(2):
            sk = jnp.where(top, sink_ref[4 * kvh + half], sink_ref[4 * kvh + 2 + half])
            s_ctx = lax.dot_general(qst, kx_var[kvh][half], nt_dims, preferred_element_type=F32)
            o_band, p_ctx, dens = [], [], []
            for qb in range(Q_BLOCKS):
                g = n * Q_BLOCKS + qb
                qrows = qst[qb * 2 * BLOCK:(qb + 1) * 2 * BLOCK]
                sb = lax.dot_general(qrows, kb_var[kvh][half][qb * BLOCK:(qb + 3) * BLOCK], nt_dims,
                                     preferred_element_type=F32)
                s0 = jnp.where((col >= row) & (g > 0), sb[:, :BLOCK], NEG_INF)
                s1 = sb[:, BLOCK:2 * BLOCK]
                s2 = jnp.where((col <= row) & (g < nblk - 1), sb[:, 2 * BLOCK:], NEG_INF)
                sc = s_ctx[qb * 2 * BLOCK:(qb + 1) * 2 * BLOCK]
                ctx_blocks = [sc[:, cb * BLOCK:(cb + 1) * BLOCK] for cb in range(sc.shape[1] // BLOCK)]
                m = functools.reduce(jnp.maximum, [s0, s1, s2] + ctx_blocks)
                m = jnp.maximum(jnp.max(m, axis=-1, keepdims=True), sk)
                p0, p1, p2, pc = (jnp.exp(t - m) for t in (s0, s1, s2, sc))
                psum = functools.reduce(
                    jnp.add, [p0, p1, p2] + [pc[:, cb * BLOCK:(cb + 1) * BLOCK] for cb in range(len(ctx_blocks))])
                den = jnp.sum(psum, axis=-1, keepdims=True) + jnp.exp(sk - m)
                pb = jnp.concatenate([p0, p1, p2], axis=1).astype(BF16)
                o_band.append(jnp.dot(pb, vb_var[kvh][half][qb * BLOCK:(qb + 3) * BLOCK],
                                      preferred_element_type=F32))
                p_ctx.append(pc.astype(BF16))
                dens.append(den)
            o_ctx = jnp.dot(jnp.concatenate(p_ctx, axis=0), vx_var[kvh][half], preferred_element_type=F32)
            for qb in range(Q_BLOCKS):
                o = (o_band[qb] + o_ctx[qb * 2 * BLOCK:(qb + 1) * 2 * BLOCK]) / dens[qb]
                accs[qb] = o if accs[qb] is None else accs[qb] + o
        for qb in range(Q_BLOCKS):
            att_blocks[qb][2 * kvh] = accs[qb][:BLOCK]
            att_blocks[qb][2 * kvh + 1] = accs[qb][BLOCK:]

    u = u_ref[0]
    vg = vg_ref[0]
    bs = bs_ref[...]
    gm_blocks = [[None] * 4 for _ in range(Q_BLOCKS)]
    for j in range(GM_W // LANES):
        chunks = [vg[c * BLOCK:(c + 1) * BLOCK, j * LANES:(j + 1) * LANES] for c in range(Q_BLOCKS)]
        rhs = jnp.concatenate(
            [jnp.concatenate([jnp.where(low, v, zero) for v in chunks], axis=1),
             jnp.concatenate([jnp.where(low, zero, v) for v in chunks], axis=1)], axis=0)
        mixed = jnp.dot(wcat_ref[j], rhs, preferred_element_type=F32)
        bias = jnp.where(low, bs[:, 2 * j:2 * j + 1], bs[:, 2 * j + 1:2 * j + 2])
        for c in range(Q_BLOCKS):
            gm_blocks[c][j] = (u[c * BLOCK:(c + 1) * BLOCK, j * LANES:(j + 1) * LANES].astype(F32)
                               * (mixed[:, c * LANES:(c + 1) * LANES] + bias))

    mix = jnp.concatenate([jnp.concatenate(att_blocks[c] + gm_blocks[c], axis=1) for c in range(Q_BLOCKS)],
                          axis=0).astype(BF16)
    y = jnp.dot(mix, wout_bf[...], preferred_element_type=F32)
    mod = mod_ref[0, 0]
    o_ref[0] = x_ref[0] + mod[2:3] * y


def _attn_mixer(x, mod, sink, q, k, ksw, v, vsw, kx, ksx, vx, vsx, u, vg, wcat_bf, bs_t, wout):
    b, s, _ = x.shape
    tq = Q_BLOCKS * BLOCK
    nb = s // BLOCK
    l = kx.shape[1]
    cur = lambda w: pl.BlockSpec((1, tq, w), lambda bi, n: (bi, n, 0))
    prv = lambda w: pl.BlockSpec((1, BLOCK, w), lambda bi, n: (bi, jnp.maximum(n * Q_BLOCKS - 1, 0), 0))
    nxt = lambda w: pl.BlockSpec((1, BLOCK, w), lambda bi, n: (bi, jnp.minimum((n + 1) * Q_BLOCKS, nb - 1), 0))
    cx = pl.BlockSpec((1, l, KV_W), lambda bi, n: (bi, 0, 0))
    return pl.pallas_call(
        _attn_kernel,
        grid=(b, s // tq),
        in_specs=[
            pl.BlockSpec(memory_space=pltpu.SMEM),
            cur(ATT_W),
            prv(KV_W), cur(KV_W), nxt(KV_W), prv(KV_W), cur(KV_W), nxt(KV_W),
            prv(KV_W), cur(KV_W), nxt(KV_W), prv(KV_W), cur(KV_W), nxt(KV_W),
            cx, cx, cx, cx,
            cur(GM_W), cur(GM_W),
            pl.BlockSpec((4, BLOCK, 2 * BLOCK), lambda bi, n: (0, 0, 0)),
            pl.BlockSpec((BLOCK, 8), lambda bi, n: (0, 0)),
            pl.BlockSpec((D, D), lambda bi, n: (0, 0), pipeline_mode=pl.Buffered(1)),
            cur(D),
            pl.BlockSpec((1, 1, 6, D), lambda bi, n: (0, bi, 0, 0)),
        ],
        out_specs=cur(D),
        out_shape=jax.ShapeDtypeStruct((b, s, D), F32),
        scratch_shapes=[pltpu.VMEM((D, D), BF16)],
        compiler_params=_cparams(("arbitrary", "arbitrary"), VMEM_LIMIT),
        name="attn_gmlp_out",
    )(sink, q, k, k, k, ksw, ksw, ksw, v, v, v, vsw, vsw, vsw, kx, ksx, vx, vsx,
      u, vg, wcat_bf, bs_t, wout, x, mod)


def _ffn_kernel(x_ref, mod_ref, wg_ref, wu_ref, wd_ref, o_ref, act):
    mod = mod_ref[0, 0]
    xf = x_ref[...]
    h = _modulate(xf, mod[3:4], mod[4:5]).astype(BF16)
    for c in range(wg_ref.shape[1] // W_CHUNK):
        cs = slice(c * W_CHUNK, (c + 1) * W_CHUNK)
        g = jnp.dot(h, wg_ref[:, cs], preferred_element_type=F32)
        up = jnp.dot(h, wu_ref[:, cs], preferred_element_type=F32)
        act[:, cs] = (g * _sigmoid(g) * up).astype(BF16)
    o_ref[...] = xf + mod[5:6] * jnp.dot(act[...], wd_ref[...], preferred_element_type=F32)


def _dense_ffn(x2d, mod, wg, wu, wd, seq):
    n = x2d.shape[0]
    f = wg.shape[1]
    tm = TM_FFN
    per_b = seq // tm
    resident = lambda shp: pl.BlockSpec(shp, lambda i: (0, 0), pipeline_mode=pl.Buffered(1))
    return pl.pallas_call(
        _ffn_kernel,
        grid=(n // tm,),
        in_specs=[
            pl.BlockSpec((tm, D), lambda i: (i, 0)),
            pl.BlockSpec((1, 1, 6, D), lambda i: (0, i // per_b, 0, 0)),
            resident((D, f)), resident((D, f)), resident((f, D)),
        ],
        out_specs=pl.BlockSpec((tm, D), lambda i: (i, 0)),
        out_shape=jax.ShapeDtypeStruct((n, D), F32),
        scratch_shapes=[pltpu.VMEM((tm, f), BF16)],
        compiler_params=_cparams(("arbitrary",), VMEM_LIMIT),
        name="dense_ffn",
    )(x2d, mod, wg, wu, wd)


def _pool_route_kernel(x_ref, xp_ref, xn_ref, mod_ref, band_ref, pw_ref, psc_ref, wr_hi_ref,
                       tri_ref, x3_ref, h2_ref, route_ref, cnt_ref, hext, carry):
    bi = pl.program_id(0)
    i = pl.program_id(1)
    ni = pl.num_programs(1)
    tm = x_ref.shape[1]
    seq = tm * ni
    mod = mod_ref[0, 0]

    @pl.when((bi == 0) & (i == 0))
    def _():
        carry[...] = jnp.zeros_like(carry)

    xf = x_ref[0]
    hp = _modulate(xp_ref[0], mod[0:1], mod[1:2])
    hn = _modulate(xn_ref[0], mod[0:1], mod[1:2])
    hext[0:POOL_HALO] = jnp.where(i > 0, hp, 0.0).astype(BF16)
    h_main = _modulate(xf, mod[0:1], mod[1:2])
    hext[POOL_HALO:POOL_HALO + tm] = h_main.astype(BF16)
    hext[POOL_HALO + tm:] = jnp.where(i < ni - 1, hn, 0.0).astype(BF16)

    t_local = lax.broadcasted_iota(jnp.int32, (BLOCK, 1), 0)
    ys = []
    for gi, w in enumerate(POOL_SIZES):
        lo_off = -(w // 2)
        hi_off = w - 1 - w // 2
        cols = slice(gi * POOL_GD, (gi + 1) * POOL_GD)
        outs = []
        for sb in range(tm // BLOCK):
            r0 = sb * BLOCK
            win = jnp.dot(band_ref[gi], hext[r0:r0 + BLOCK + 2 * POOL_HALO, cols],
                          preferred_element_type=F32)
            t = i * tm + r0 + t_local
            cnt = (jnp.minimum(t + hi_off, seq - 1) - jnp.maximum(t + lo_off, 0) + 1).astype(F32)
            diff = win / cnt - h_main[r0:r0 + BLOCK, cols]
            outs.append(diff.astype(BF16))
        dg = jnp.concatenate(outs, axis=0)
        ys.append(jnp.dot(dg, pw_ref[gi], preferred_element_type=F32))
    y = jnp.concatenate(ys, axis=1) * psc_ref[...]
    x3 = xf + mod[2:3] * y
    x3_ref[0] = x3

    h2 = _modulate(x3, mod[3:4], mod[4:5])
    _rows_to_slabs(h2, h2_ref)
    logits = jnp.dot(h2.astype(BF16), wr_hi_ref[...], preferred_element_type=F32)
    lane = lax.broadcasted_iota(jnp.int32, (tm, LANES), 1)
    lane_f = lane.astype(F32)
    neg = -jnp.inf
    lg = jnp.where(lane < N_EXPERTS, logits, neg)
    m1 = jnp.max(lg, axis=-1, keepdims=True)
    i1 = jnp.min(jnp.where(lg == m1, lane_f, float(LANES)), axis=-1, keepdims=True)
    oh1 = lane_f == i1
    lg2 = jnp.where(oh1, neg, lg)
    m2 = jnp.max(lg2, axis=-1, keepdims=True)
    i2 = jnp.min(jnp.where(lg2 == m2, lane_f, float(LANES)), axis=-1, keepdims=True)
    oh2 = lane_f == i2
    e = jnp.exp(m2 - m1)
    w1 = 1.0 / (1.0 + e)
    w2 = e / (1.0 + e)
    oh = jnp.where(oh1 | oh2, 1.0, 0.0)
    before = jnp.dot(tri_ref[...], oh.astype(BF16), preferred_element_type=F32) + carry[...]
    r1 = jnp.sum(jnp.where(oh1, before, 0.0), axis=-1, keepdims=True)
    r2 = jnp.sum(jnp.where(oh2, before, 0.0), axis=-1, keepdims=True)
    carry[...] = carry[...] + jnp.sum(oh, axis=0, keepdims=True)
    cnt_ref[...] = carry[...]
    info = jnp.where(lane == 0, i1, jnp.where(lane == 1, i2, jnp.where(lane == 2, w1, jnp.where(
        lane == 3, w2, jnp.where(lane == 4, r1, jnp.where(lane == 5, r2, 0.0))))))
    route_ref[...] = info.T[0:8, :]


def _pool_route(x, mod, band, pw_bf, pool_scale, wr_hi, tri):
    b, s, _ = x.shape
    tm = TM_POOL
    ni = s // tm
    hb = tm // POOL_HALO
    row = pl.BlockSpec((1, tm, D), lambda bi, i: (bi, i, 0))
    const2 = lambda shp: pl.BlockSpec(shp, lambda bi, i: (0,) * len(shp))
    return pl.pallas_call(
        _pool_route_kernel,
        grid=(b, ni),
        in_specs=[
            row,
            pl.BlockSpec((1, POOL_HALO, D), lambda bi, i: (bi, jnp.maximum(i * hb - 1, 0), 0)),
            pl.BlockSpec((1, POOL_HALO, D), lambda bi, i: (bi, jnp.minimum((i + 1) * hb, s // POOL_HALO - 1), 0)),
            pl.BlockSpec((1, 1, 6, D), lambda bi, i: (1, bi, 0, 0)),
            const2(band.shape), const2(pw_bf.shape), const2((1, D)),
            const2(wr_hi.shape), const2(tri.shape),
        ],
        out_specs=[row,
                   pl.BlockSpec((tm * SLAB, LANES), lambda bi, i: (bi * ni + i, 0)),
                   pl.BlockSpec((8, tm), lambda bi, i: (0, bi * ni + i)),
                   pl.BlockSpec((1, LANES), lambda bi, i: (0, 0))],
        out_shape=[jax.ShapeDtypeStruct((b, s, D), F32), jax.ShapeDtypeStruct((b * s * SLAB, LANES), F32),
                   jax.ShapeDtypeStruct((8, b * s), F32), jax.ShapeDtypeStruct((1, LANES), F32)],
        scratch_shapes=[pltpu.VMEM((tm + 2 * POOL_HALO, D), BF16), pltpu.VMEM((1, LANES), F32)],
        compiler_params=_cparams(("arbitrary", "arbitrary"), VMEM_LIMIT),
        name="pool_route",
    )(x, x, x, mod, band, pw_bf, pool_scale.reshape(1, D), wr_hi, tri)


def _slotmap_kernel(pos_ref, lo_ref, hi_ref, o_ref):
    n_pairs = pos_ref.shape[0]
    spare_mask = 2 * TM_MOE - 1
    for e in range(lo_ref.shape[0]):
        def fill(p, c):
            o_ref[p] = n_pairs + (p & spare_mask)
            return c
        lax.fori_loop(lo_ref[e], hi_ref[e], fill, 0)

    def place(f, c):
        o_ref[pos_ref[f]] = f
        return c
    lax.fori_loop(0, n_pairs, place, 0, unroll=32)


def _slot_map(pos_flat, lo, hi, n_slots):
    smem = pl.BlockSpec(memory_space=pltpu.SMEM)
    return pl.pallas_call(
        _slotmap_kernel,
        in_specs=[smem, smem, smem],
        out_specs=smem,
        out_shape=jax.ShapeDtypeStruct((n_slots,), jnp.int32),
        name="moe_slot_map",
    )(pos_flat, lo, hi)


def _moe_kernel(te_ref, nused_ref, half_ref, fnext_ref, fprev_ref, f0_ref, h_hbm, wg_hbm, wu_hbm, wd_hbm,
                y_hbm, wg_res, wu_res, wd_res, stg_in, stg_out, xbuf, xb, act, acc, stage, gsem, ssem, wsem):
    i = pl.program_id(0)
    nt = pl.num_programs(0)
    used_tiles = nused_ref[0]
    tm = xb.shape[0]
    f_dim = wg_res.shape[1]
    tok_mask = h_hbm.shape[0] // SLAB - 1
    tile_rows = tm * SLAB
    out_rows = tm * PACK

    def slab(ix, width=SLAB):
        return pl.ds(pl.multiple_of(ix * width, width), width)

    def gather_row(fref, r, slot, zero=0):
        tok = fref[0, 0, r + zero] & tok_mask
        return pltpu.make_async_copy(h_hbm.at[slab(tok)], xbuf.at[slot, slab(r)], gsem.at[slot])

    def scatter_row(fref, r, slot, zero=0):
        return pltpu.make_async_copy(stage.at[slot, slab(r, PACK)], y_hbm.at[slab(fref[0, 0, r + zero], PACK)],
                                     ssem.at[slot])

    def gather_all(slot):
        return pltpu.make_async_copy(h_hbm.at[pl.ds(0, tile_rows)], xbuf.at[slot], gsem.at[slot])

    def scatter_all(slot):
        return pltpu.make_async_copy(stage.at[slot], y_hbm.at[pl.ds(0, out_rows)], ssem.at[slot])

    cur = i % 2
    used = i < used_tiles

    @pl.when(i == 0)
    def _():
        stage[...] = jnp.zeros_like(stage)
        spare = y_hbm.shape[0] - 2 * out_rows
        fills = [pltpu.make_async_copy(stage.at[sl], y_hbm.at[pl.ds(spare + sl * out_rows, out_rows)],
                                       ssem.at[sl])
                 for sl in range(2)]
        for cp in fills:
            cp.start()
        for cp in fills:
            cp.wait()

    @pl.when((i == 0) & used)
    def _():
        def prime(r, c):
            gather_row(f0_ref, r, 0).start()
            return c
        lax.fori_loop(0, tm, prime, 0)

    expert = te_ref[i]
    new_expert = used & ((i == 0) | (expert != te_ref[jnp.maximum(i - 1, 0)]))

    has_next = i + 1 < used_tiles
    has_prev = (i >= 1) & (i - 1 < used_tiles)
    steady = (i >= 1) & has_next
    half_tile = half_ref[i] == 1

    @pl.when(used)
    def _():
        gather_all(cur).wait()

    @pl.when((i >= 2) & (i - 2 < used_tiles))
    def _():
        scatter_all(cur).wait()

    half = f_dim // MOE_SPLIT
    n_piece = half // W_CHUNK
    n_groups = MOE_SPLIT * (n_piece + 1)

    def tick(v):
        bits = jnp.max(lax.bitcast_convert_type(v[0:SLAB, 0:LANES], jnp.int32))
        return lax.shift_right_logical(lax.shift_right_logical(bits, 16), 16)

    def expert_ffn(issue_group, rows=tm):
        for cix in range(SLAB):
            xb[0:rows, cix * LANES:(cix + 1) * LANES] = (
                xbuf.at[cur][pl.ds(cix, rows, stride=SLAB), :].astype(BF16))
        xv = xb[0:rows]
        out = None
        issue_group(0, 0)
        k = 1
        for hf in range(MOE_SPLIT):
            for c in range(n_piece):
                cs = slice(hf * half + c * W_CHUNK, hf * half + (c + 1) * W_CHUNK)
                g = jnp.dot(xv, wg_res[:, cs], preferred_element_type=F32)
                up = jnp.dot(xv, wu_res[:, cs], preferred_element_type=F32)
                act[0:rows, c * W_CHUNK:(c + 1) * W_CHUNK] = (g * _sigmoid(g) * up).astype(BF16)
                issue_group(k, tick(g))
                k += 1
            part = jnp.dot(act[0:rows], wd_res[hf * half:(hf + 1) * half, :], preferred_element_type=F32)
            if hf < MOE_SPLIT - 1:
                acc[0:rows] = part if out is None else acc[0:rows] + part
                out = acc
                issue_group(k, tick(part))
                k += 1
            else:
                _pack_rows(part if out is None else acc[0:rows] + part, stage.at[cur])

    def chunk_plan():
        n_in = D // W_ROWS_IN
        ins = [(w, res, stg_in, W_RING_IN, 0, W_ROWS_IN, c, wi * n_in + c)
               for wi, (w, res) in enumerate(((wg_hbm, wg_res), (wu_hbm, wu_res))) for c in range(n_in)]
        outs = [(wd_hbm, wd_res, stg_out, W_RING_OUT, W_RING_IN, W_ROWS_OUT, c, c)
                for c in range(f_dim // W_ROWS_OUT)]
        return outs[:W_RING_OUT] + ins + outs[W_RING_OUT:]

    def chunk_copy(entry, ex):
        w_hbm, _, stg, ring, sem0, rows, c, j = entry
        return pltpu.make_async_copy(w_hbm.at[ex, pl.ds(c * rows, rows), :], stg.at[j % ring],
                                     wsem.at[sem0 + j % ring])

    def start_first_chunks(ex):
        for entry in chunk_plan():
            if entry[7] < entry[3]:
                chunk_copy(entry, ex).start()

    @pl.when(new_expert)
    def _():
        plan = chunk_plan()

        @pl.when(i == 0)
        def _():
            start_first_chunks(expert)

        for k, entry in enumerate(plan):
            _, res, stg, ring, _, rows, c, j = entry
            chunk_copy(entry, expert).wait()
            res[c * rows:(c + 1) * rows, :] = stg[j % ring].astype(BF16)
            later = [e for e in plan[k + 1:] if e[2] is stg and e[7] == j + ring]
            if later:
                chunk_copy(later[0], expert).start()

    next_expert = te_ref[jnp.minimum(i + 1, nt - 1)]

    @pl.when(has_next & (next_expert != expert))
    def _():
        start_first_chunks(next_expert)

    def issue_group(k, zero):
        for r in range(k * tm // n_groups, (k + 1) * tm // n_groups):
            gather_row(fnext_ref, r, 1 - cur, zero).start()
            scatter_row(fprev_ref, r, 1 - cur, zero).start()

    @pl.when(steady & jnp.logical_not(half_tile))
    def _():
        expert_ffn(issue_group)

    @pl.when(steady & half_tile)
    def _():
        expert_ffn(issue_group, tm // 2)

    @pl.when(jnp.logical_not(steady))
    def _():
        @pl.when(used)
        def _():
            expert_ffn(lambda k, zero: None)

        @pl.when(has_next)
        def _():
            def issue(r, c):
                gather_row(fnext_ref, r, 1 - cur).start()
                return c
            lax.fori_loop(0, tm, issue, 0)

        @pl.when(has_prev)
        def _():
            def issue(r, c):
                scatter_row(fprev_ref, r, 1 - cur).start()
                return c
            lax.fori_loop(0, tm, issue, 0)

    @pl.when((i == nt - 1) & (nt - 2 < used_tiles))
    def _():
        scatter_all(1 - cur).wait()


def _moe_experts(h_slabs, fmap, tile_expert, n_used, tile_half, wg, wu, wd, n_tiles, y_rows):
    f = wg.shape[2]
    tm = TM_MOE
    fblk = lambda imap: pl.BlockSpec((1, 1, tm), imap, memory_space=pltpu.SMEM)
    hbm = pl.BlockSpec(memory_space=pl.ANY)
    return pl.pallas_call(
        _moe_kernel,
        grid_spec=pltpu.PrefetchScalarGridSpec(
            num_scalar_prefetch=3,
            grid=(n_tiles,),
            in_specs=[
                fblk(lambda i, te, nu, hf: (jnp.minimum(i + 1, n_tiles - 1), 0, 0)),
                fblk(lambda i, te, nu, hf: (jnp.maximum(i - 1, 0), 0, 0)),
                fblk(lambda i, te, nu, hf: (0, 0, 0)),
                hbm, hbm, hbm, hbm,
            ],
            out_specs=hbm,
            scratch_shapes=[pltpu.VMEM((D, f), BF16), pltpu.VMEM((D, f), BF16), pltpu.VMEM((f, D), BF16),
                            pltpu.VMEM((W_RING_IN, W_ROWS_IN, f), F32), pltpu.VMEM((W_RING_OUT, W_ROWS_OUT, D), F32),
                            pltpu.VMEM((2, tm * SLAB, LANES), F32), pltpu.VMEM((tm, D), BF16),
                            pltpu.VMEM((tm, f // MOE_SPLIT), BF16), pltpu.VMEM((tm, D), F32),
                            pltpu.VMEM((2, tm * PACK, LANES), jnp.uint32),
                            pltpu.SemaphoreType.DMA((2,)), pltpu.SemaphoreType.DMA((2,)),
                            pltpu.SemaphoreType.DMA((W_RING_IN + W_RING_OUT,))],
        ),
        out_shape=jax.ShapeDtypeStruct((y_rows * PACK, LANES), jnp.uint32),
        compiler_params=_cparams(("arbitrary",), MOE_VMEM_LIMIT),
        name="moe_experts",
    )(tile_expert, n_used, tile_half, fmap, fmap, fmap, h_slabs, wg, wu, wd)


def _combine_kernel(y1_ref, y2_ref, x_ref, w_ref, mod_ref, gain_ref, o_ref):
    rows = x_ref.shape[0]
    info = jnp.concatenate([w_ref[...], jnp.zeros((LANES - w_ref.shape[0], rows), F32)], axis=0).T
    moe = info[:, 2:3] * _unpack_rows(y1_ref, rows) + info[:, 3:4] * _unpack_rows(y2_ref, rows)
    mod = mod_ref[0, 0]
    x4 = x_ref[...] + mod[5:6] * moe
    ms = jnp.mean(x4 * x4, axis=-1, keepdims=True)
    o_ref[...] = x4 * lax.rsqrt(ms + EPS) * gain_ref[...]


def _combine(y, x3_2d, wts, mod, final_gain, seq):
    n = x3_2d.shape[0]
    tc = TC_COMB
    nt = n // tc
    per_b = seq // tc
    return pl.pallas_call(
        _combine_kernel,
        grid=(nt,),
        in_specs=[
            pl.BlockSpec((tc * PACK, LANES), lambda i: (i, 0)),
            pl.BlockSpec((tc * PACK, LANES), lambda i: (i + nt, 0)),
            pl.BlockSpec((tc, D), lambda i: (i, 0)),
            pl.BlockSpec((8, tc), lambda i: (0, i)),
            pl.BlockSpec((1, 1, 6, D), lambda i: (1, i // per_b, 0, 0)),
            pl.BlockSpec((1, D), lambda i: (0, 0)),
        ],
        out_specs=pl.BlockSpec((tc, D), lambda i: (i, 0)),
        out_shape=jax.ShapeDtypeStruct((n, D), F32),
        compiler_params=_cparams(("arbitrary",)),
        name="moe_combine",
    )(y, y, x3_2d, wts, mod, final_gain.reshape(1, D))


def _rope_tables(seq):
    rows = seq // GRID_W
    row_pos = jnp.repeat(jnp.arange(rows, dtype=F32), GRID_W)
    col_pos = jnp.tile(jnp.arange(GRID_W, dtype=F32), rows)
    axis_dim = HEAD_DIM // 2
    inv_freq = ROPE_BASE ** (-jnp.arange(0, axis_dim, 2, dtype=F32) / axis_dim)
    ar = row_pos[:, None] * inv_freq
    ac = col_pos[:, None] * inv_freq
    cos64 = jnp.concatenate([jnp.cos(ar), jnp.cos(ar), jnp.cos(ac), jnp.cos(ac)], axis=1)
    sin64 = jnp.concatenate([-jnp.sin(ar), jnp.sin(ar), -jnp.sin(ac), jnp.sin(ac)], axis=1)
    return jnp.tile(cos64, (1, 2)), jnp.tile(sin64, (1, 2))


def _band_matrices():
    r = np.arange(BLOCK)[:, None]
    c = np.arange(BLOCK + 2 * POOL_HALO)[None, :] - POOL_HALO
    mats = []
    for w in POOL_SIZES:
        lo = -(w // 2)
        hi = w - 1 - w // 2
        mats.append(((c >= r + lo) & (c <= r + hi)).astype(np.float32))
    return jnp.asarray(np.stack(mats), dtype=BF16)


def kernel(x, c, ctx, c_ctx, w_ada, b_ada, w_in, attn_sink, gm_gain, gm_w_s, gm_b_s, w_out,
           ffn_w_gate, ffn_w_up, ffn_w_down, pool_w, pool_scale, router_w,
           moe_w_gate, moe_w_up, moe_w_down, final_gain):
    b, s, _ = x.shape
    n = b * s
    assert w_ada.shape[0] == 2 and w_in.shape[0] == 1 and pool_w.shape[0] == 1
    assert s % TM_IN == 0 and s % TM_POOL == 0 and s % TM_FFN == 0 and b <= 4
    assert n & (n - 1) == 0

    cvec = jnp.concatenate([c, c_ctx[None, :], jnp.zeros((8 - b - 1, D), F32)], axis=0)
    mod = _ada_mod(cvec, w_ada, b_ada)

    cos_t, sin_t = _rope_tables(s)
    q, k, ksw, v, vsw, u, vg = _in_proj(x, mod, w_in[0], gm_gain[0], cos_t, sin_t)
    kx, ksx, vx, vsx = _ctx_kv(ctx, mod, w_in[0])
    wcat = gm_w_s[0].reshape(4, 2, BLOCK, BLOCK).transpose(0, 2, 1, 3).reshape(4, BLOCK, 2 * BLOCK).astype(BF16)
    x1 = _attn_mixer(x, mod, attn_sink[0], q, k, ksw, v, vsw, kx, ksx, vx, vsx, u, vg,
                     wcat, gm_b_s[0].T, w_out[0])
    x2 = _dense_ffn(x1.reshape(n, D), mod, ffn_w_gate[0].astype(BF16), ffn_w_up[0].astype(BF16),
                    ffn_w_down[0].astype(BF16), s)

    wr = jnp.pad(router_w[0], ((0, 0), (0, LANES - N_EXPERTS)))
    wr_hi = wr.astype(BF16)
    tri = jnp.asarray(np.tril(np.ones((TM_POOL, TM_POOL), np.float32), -1), dtype=BF16)
    x3, h2, route, counts = _pool_route(x2.reshape(b, s, D), mod, _band_matrices(), pool_w[0].astype(BF16),
                                        pool_scale[0], wr_hi, tri)

    tm = TM_MOE
    n_tiles = (2 * n) // tm + N_EXPERTS
    cnt = counts[0, :N_EXPERTS].astype(jnp.int32)
    tiles_e = (cnt + tm - 1) // tm
    tile_end = jnp.cumsum(tiles_e)
    off = (tile_end - tiles_e) * tm
    n_used = tile_end[-1]
    tix = jnp.arange(n_tiles, dtype=jnp.int32)
    te = jnp.minimum(jnp.sum(tix[:, None] >= tile_end[None, :], axis=1), N_EXPERTS - 1).astype(jnp.int32)
    te_last = te[jnp.maximum(n_used - 1, 0)]
    tile_expert = jnp.where(tix < n_used, te, te_last)
    rows_left = cnt[te] - (tix - (tile_end - tiles_e)[te]) * tm
    tile_half = ((tix < n_used) & (rows_left <= tm // 2)).astype(jnp.int32)
    e1 = route[0].astype(jnp.int32)
    e2 = route[1].astype(jnp.int32)
    pos1 = off[e1] + route[4].astype(jnp.int32)
    pos2 = off[e2] + route[5].astype(jnp.int32)
    n_slots = n_tiles * tm
    pad_lo = jnp.concatenate([off + cnt, (n_used * tm).reshape(1)]).astype(jnp.int32)
    pad_hi = jnp.concatenate([off + tiles_e * tm, jnp.full((1,), n_slots, jnp.int32)]).astype(jnp.int32)
    fmap = _slot_map(jnp.concatenate([pos1, pos2]), pad_lo, pad_hi, n_slots)
    n_used_arr = n_used.reshape(1).astype(jnp.int32)

    y = _moe_experts(h2, fmap.reshape(n_tiles, 1, tm), tile_expert, n_used_arr, tile_half,
                     moe_w_gate[0], moe_w_up[0], moe_w_down[0], n_tiles, 2 * n + 2 * tm)
    out = _combine(y, x3.reshape(n, D), route, mod, final_gain, s)
    return out.reshape(b, s, D)
```

```python
import functools

import numpy as np
import jax
import jax.numpy as jnp
from jax import lax
from jax.experimental import pallas as pl
from jax.experimental.pallas import tpu as pltpu

F32 = jnp.float32
BF16 = jnp.bfloat16

D = 1024
GRID_W = 64
EPS = 1e-6
NEG_INF = -1e30
HEAD_DIM = 64
N_Q_HEADS = 8
BLOCK = 128
ATT_W = 512
KV_W = 128
GM_W = 512
IN_W = 1792
POOL_SIZES = (2, 4, 8, 16)
POOL_GD = 256
POOL_HALO = 16
N_EXPERTS = 8
ROPE_BASE = 10000.0
LANES = 128
SLAB = D // LANES
PACK = SLAB // 2
SQRT_2_OVER_PI = 0.7978845608028654

TM_IN = 1024
TM_FFN = 1024
TM_POOL = 1024
TM_MOE = 512
MOE_SPLIT = 2
W_CHUNK = 256
W_ROWS_IN = 128
W_ROWS_OUT = 512
W_RING_IN = 12
W_RING_OUT = 4
MOE_VMEM_LIMIT = 62 * 1024 * 1024
TC_COMB = 1024
Q_BLOCKS = 8
VMEM_LIMIT = 56 * 1024 * 1024


def _cparams(sem, vmem=None):
    return pltpu.CompilerParams(dimension_semantics=sem, vmem_limit_bytes=vmem)


def _modulate(xf, shift, scale):
    ms = jnp.mean(xf * xf, axis=-1, keepdims=True)
    return xf * lax.rsqrt(ms + EPS) * (1.0 + scale) + shift


def _sigmoid(z):
    return 1.0 / (1.0 + jnp.exp(-z))


def _rows_to_slabs(val, slab_ref):
    rows = val.shape[0]
    for cix in range(SLAB):
        slab_ref[pl.ds(cix, rows, stride=SLAB), :] = val[:, cix * LANES:(cix + 1) * LANES]


def _slabs_to_rows(slab_ref, rows):
    return jnp.concatenate([slab_ref[pl.ds(cix, rows, stride=SLAB), :] for cix in range(SLAB)], axis=1)


def _pack_rows(val, pack_ref):
    rows = val.shape[0]
    bits = lambda v: lax.bitcast_convert_type(v.astype(BF16).astype(F32), jnp.uint32)
    for cix in range(PACK):
        hi = bits(val[:, cix * LANES:(cix + 1) * LANES])
        lo = bits(val[:, D // 2 + cix * LANES:D // 2 + (cix + 1) * LANES])
        pack_ref[pl.ds(cix, rows, stride=PACK), :] = hi | (lo >> 16)


def _unpack_rows(pack_ref, rows):
    words = [pack_ref[pl.ds(cix, rows, stride=PACK), :] for cix in range(PACK)]
    his = [lax.bitcast_convert_type(w & jnp.uint32(0xFFFF0000), F32) for w in words]
    los = [lax.bitcast_convert_type(w << 16, F32) for w in words]
    return jnp.concatenate(his + los, axis=1)


def _ada_kernel(c_ref, w_ref, b_ref, o_ref):
    c = c_ref[...]
    s = c * _sigmoid(c)
    o_ref[0] = jnp.dot(s.astype(BF16), w_ref[0].astype(BF16), preferred_element_type=F32) + b_ref[0]


def _ada_mod(cvec, w_ada, b_ada):
    depth, _, n6 = w_ada.shape
    tn = 1536
    out = pl.pallas_call(
        _ada_kernel,
        grid=(depth, n6 // tn),
        in_specs=[
            pl.BlockSpec((8, D), lambda l, j: (0, 0)),
            pl.BlockSpec((1, D, tn), lambda l, j: (l, 0, j)),
            pl.BlockSpec((1, 1, tn), lambda l, j: (l, 0, j)),
        ],
        out_specs=pl.BlockSpec((1, 8, tn), lambda l, j: (l, 0, j)),
        out_shape=jax.ShapeDtypeStruct((depth, 8, n6), F32),
        compiler_params=_cparams(("arbitrary", "arbitrary")),
        name="ada_mod",
    )(cvec, w_ada, b_ada.reshape(depth, 1, n6))
    return out.reshape(depth, 8, 6, D)


def _rope(t, cs, sn, first_half):
    fwd = pltpu.roll(t, LANES - 16, axis=1)
    bwd = pltpu.roll(t, 16, axis=1)
    return t * cs + jnp.where(first_half, fwd, bwd) * sn


def _inproj_kernel(x_ref, mod_ref, w_ref, gain_ref, cos_ref, sin_ref,
                   q_ref, k_ref, ksw_ref, v_ref, vsw_ref, u_ref, vg_ref, w_bf):
    @pl.when((pl.program_id(0) == 0) & (pl.program_id(1) == 0))
    def _():
        w_bf[...] = w_ref[...].astype(BF16)

    mod = mod_ref[0, 0]
    h = _modulate(x_ref[0], mod[0:1], mod[1:2]).astype(BF16)
    proj = jnp.dot(h, w_bf[...], preferred_element_type=F32)
    cs = cos_ref[...]
    sn = sin_ref[...]
    lane = lax.broadcasted_iota(jnp.int32, cs.shape, 1)
    first_half = (lane & 16) == 0
    for cix in range(ATT_W // LANES):
        t = proj[:, cix * LANES:(cix + 1) * LANES]
        q_ref[0, :, cix * LANES:(cix + 1) * LANES] = (
            _rope(t, cs, sn, first_half) * (HEAD_DIM ** -0.5)).astype(BF16)
    kr = _rope(proj[:, ATT_W:ATT_W + KV_W], cs, sn, first_half)
    k_ref[0] = kr.astype(BF16)
    ksw_ref[0] = pltpu.roll(kr, HEAD_DIM, axis=1).astype(BF16)
    vv = proj[:, ATT_W + KV_W:ATT_W + 2 * KV_W]
    v_ref[0] = vv.astype(BF16)
    vsw_ref[0] = pltpu.roll(vv, HEAD_DIM, axis=1).astype(BF16)
    z = proj[:, ATT_W + 2 * KV_W:]
    g = z * (0.5 * (1.0 + jnp.tanh(SQRT_2_OVER_PI * (z + 0.044715 * (z * z * z)))))
    u_ref[0] = g[:, :GM_W].astype(BF16)
    vg = g[:, GM_W:]
    ms = jnp.mean(vg * vg, axis=-1, keepdims=True)
    vg_ref[0] = (vg * lax.rsqrt(ms + EPS) * gain_ref[...]).astype(BF16)


def _in_proj(x, mod, w_in, gm_gain, cos_t, sin_t):
    b, s, _ = x.shape
    tm = TM_IN
    row = lambda w: pl.BlockSpec((1, tm, w), lambda bi, i: (bi, i, 0))
    outs = pl.pallas_call(
        _inproj_kernel,
        grid=(b, s // tm),
        in_specs=[
            row(D),
            pl.BlockSpec((1, 1, 6, D), lambda bi, i: (0, bi, 0, 0)),
            pl.BlockSpec((D, IN_W), lambda bi, i: (0, 0), pipeline_mode=pl.Buffered(1)),
            pl.BlockSpec((1, GM_W), lambda bi, i: (0, 0)),
            pl.BlockSpec((tm, LANES), lambda bi, i: (i, 0)),
            pl.BlockSpec((tm, LANES), lambda bi, i: (i, 0)),
        ],
        out_specs=[row(ATT_W), row(KV_W), row(KV_W), row(KV_W), row(KV_W), row(GM_W), row(GM_W)],
        out_shape=[jax.ShapeDtypeStruct((b, s, w), BF16)
                   for w in (ATT_W, KV_W, KV_W, KV_W, KV_W, GM_W, GM_W)],
        scratch_shapes=[pltpu.VMEM((D, IN_W), BF16)],
        compiler_params=_cparams(("arbitrary", "arbitrary"), VMEM_LIMIT),
        name="in_proj",
    )(x, mod, w_in, gm_gain.reshape(1, GM_W), cos_t, sin_t)
    return outs


def _ctx_kernel(c_ref, mod_ref, w_ref, k_ref, ksw_ref, v_ref, vsw_ref):
    mod = mod_ref[0, 0]
    h = _modulate(c_ref[0], mod[0:1], mod[1:2]).astype(BF16)
    kv = jnp.dot(h, w_ref[...].astype(BF16), preferred_element_type=F32)
    kk = kv[:, :KV_W]
    vv = kv[:, KV_W:]
    k_ref[0] = kk.astype(BF16)
    ksw_ref[0] = pltpu.roll(kk, HEAD_DIM, axis=1).astype(BF16)
    v_ref[0] = vv.astype(BF16)
    vsw_ref[0] = pltpu.roll(vv, HEAD_DIM, axis=1).astype(BF16)


def _ctx_kv(ctx, mod, w_in):
    b, l, _ = ctx.shape
    spec = pl.BlockSpec((1, l, KV_W), lambda bi: (bi, 0, 0))
    return pl.pallas_call(
        _ctx_kernel,
        grid=(b,),
        in_specs=[
            pl.BlockSpec((1, l, D), lambda bi: (bi, 0, 0)),
            pl.BlockSpec((1, 1, 6, D), lambda bi: (0, b, 0, 0)),
            pl.BlockSpec((D, 2 * KV_W), lambda bi: (0, ATT_W // (2 * KV_W))),
        ],
        out_specs=[spec] * 4,
        out_shape=[jax.ShapeDtypeStruct((b, l, KV_W), BF16)] * 4,
        compiler_params=_cparams(("arbitrary",)),
        name="ctx_kv",
    )(ctx, mod, w_in)


def _attn_kernel(sink_ref, q_ref, kp_ref, kc_ref, kn_ref, ksp_ref, ksc_ref, ksn_ref,
                 vp_ref, vc_ref, vn_ref, vsp_ref, vsc_ref, vsn_ref,
                 kx_ref, ksx_ref, vx_ref, vsx_ref,
                 u_ref, vg_ref, wcat_ref, bs_ref, wout_ref, x_ref, mod_ref, o_ref, wout_bf):
    n = pl.program_id(1)

    @pl.when((pl.program_id(0) == 0) & (n == 0))
    def _():
        wout_bf[...] = wout_ref[...].astype(BF16)

    nblk = pl.num_programs(1) * Q_BLOCKS
    lane = lax.broadcasted_iota(jnp.int32, (1, LANES), 1)
    low = lane < HEAD_DIM
    zero = jnp.zeros((), BF16)

    def variants(a0, a1):
        return ((jnp.where(low, a0, zero), jnp.where(low, zero, a1)),
                (jnp.where(low, a1, zero), jnp.where(low, zero, a0)))

    cat = lambda refs: jnp.concatenate([r[0] for r in refs], axis=0)
    kb_var = variants(cat((kp_ref, kc_ref, kn_ref)), cat((ksp_ref, ksc_ref, ksn_ref)))
    vb_var = variants(cat((vp_ref, vc_ref, vn_ref)), cat((vsp_ref, vsc_ref, vsn_ref)))
    kx_var = variants(kx_ref[0], ksx_ref[0])
    vx_var = variants(vx_ref[0], vsx_ref[0])

    row = lax.broadcasted_iota(jnp.int32, (2 * BLOCK, BLOCK), 0) & (BLOCK - 1)
    col = lax.broadcasted_iota(jnp.int32, (2 * BLOCK, BLOCK), 1)
    top = lax.broadcasted_iota(jnp.int32, (2 * BLOCK, 1), 0) < BLOCK
    nt_dims = (((1,), (1,)), ((), ()))

    q = q_ref[0]
    att_blocks = [[None] * 4 for _ in range(Q_BLOCKS)]
    for kvh in range(2):
        qst = jnp.concatenate(
            [q[qb * BLOCK:(qb + 1) * BLOCK, pr * LANES:(pr + 1) * LANES]
             for qb in range(Q_BLOCKS) for pr in (2 * kvh, 2 * kvh + 1)], axis=0)
        accs = [None] * Q_BLOCKS
        for half in range(2):
            sk = jnp.where(top, sink_ref[4 * kvh + half], sink_ref[4 * kvh + 2 + half])
            s_ctx = lax.dot_general(qst, kx_var[kvh][half], nt_dims, preferred_element_type=F32)
            o_band, p_ctx, dens = [], [], []
            for qb in range(Q_BLOCKS):
                g = n * Q_BLOCKS + qb
                qrows = qst[qb * 2 * BLOCK:(qb + 1) * 2 * BLOCK]
                sb = lax.dot_general(qrows, kb_var[kvh][half][qb * BLOCK:(qb + 3) * BLOCK], nt_dims,
                                     preferred_element_type=F32)
                s0 = jnp.where((col >= row) & (g > 0), sb[:, :BLOCK], NEG_INF)
                s1 = sb[:, BLOCK:2 * BLOCK]
                s2 = jnp.where((col <= row) & (g < nblk - 1), sb[:, 2 * BLOCK:], NEG_INF)
                sc = s_ctx[qb * 2 * BLOCK:(qb + 1) * 2 * BLOCK]
                ctx_blocks = [sc[:, cb * BLOCK:(cb + 1) * BLOCK] for cb in range(sc.shape[1] // BLOCK)]
                m = functools.reduce(jnp.maximum, [s0, s1, s2] + ctx_blocks)
                m = jnp.maximum(jnp.max(m, axis=-1, keepdims=True), sk)
                p0, p1, p2, pc = (jnp.exp(t - m) for t in (s0, s1, s2, sc))
                psum = functools.reduce(
                    jnp.add, [p0, p1, p2] + [pc[:, cb * BLOCK:(cb + 1) * BLOCK] for cb in range(len(ctx_blocks))])
                den = jnp.sum(psum, axis=-1, keepdims=True) + jnp.exp(sk - m)
                pb = jnp.concatenate([p0, p1, p2], axis=1).astype(BF16)
                o_band.append(jnp.dot(pb, vb_var[kvh][half][qb * BLOCK:(qb + 3) * BLOCK],
                                      preferred_element_type=F32))
                p_ctx.append(pc.astype(BF16))
                dens.append(den)
            o_ctx = jnp.dot(jnp.concatenate(p_ctx, axis=0), vx_var[kvh][half], preferred_element_type=F32)
            for qb in range(Q_BLOCKS):
                o = (o_band[qb] + o_ctx[qb * 2 * BLOCK:(qb + 1) * 2 * BLOCK]) / dens[qb]
                accs[qb] = o if accs[qb] is None else accs[qb] + o
        for qb in range(Q_BLOCKS):
            att_blocks[qb][2 * kvh] = accs[qb][:BLOCK]
            att_blocks[qb][2 * kvh + 1] = accs[qb][BLOCK:]

    u = u_ref[0]
    vg = vg_ref[0]
    bs = bs_ref[...]
    gm_blocks = [[None] * 4 for _ in range(Q_BLOCKS)]
    for j in range(GM_W // LANES):
        chunks = [vg[c * BLOCK:(c + 1) * BLOCK, j * LANES:(j + 1) * LANES] for c in range(Q_BLOCKS)]
        rhs = jnp.concatenate(
            [jnp.concatenate([jnp.where(low, v, zero) for v in chunks], axis=1),
             jnp.concatenate([jnp.where(low, zero, v) for v in chunks], axis=1)], axis=0)
        mixed = jnp.dot(wcat_ref[j], rhs, preferred_element_type=F32)
        bias = jnp.where(low, bs[:, 2 * j:2 * j + 1], bs[:, 2 * j + 1:2 * j + 2])
        for c in range(Q_BLOCKS):
            gm_blocks[c][j] = (u[c * BLOCK:(c + 1) * BLOCK, j * LANES:(j + 1) * LANES].astype(F32)
                               * (mixed[:, c * LANES:(c + 1) * LANES] + bias))

    mix = jnp.concatenate([jnp.concatenate(att_blocks[c] + gm_blocks[c], axis=1) for c in range(Q_BLOCKS)],
                          axis=0).astype(BF16)
    y = jnp.dot(mix, wout_bf[...], preferred_element_type=F32)
    mod = mod_ref[0, 0]
    o_ref[0] = x_ref[0] + mod[2:3] * y


def _attn_mixer(x, mod, sink, q, k, ksw, v, vsw, kx, ksx, vx, vsx, u, vg, wcat_bf, bs_t, wout):
    b, s, _ = x.shape
    tq = Q_BLOCKS * BLOCK
    nb = s // BLOCK
    l = kx.shape[1]
    cur = lambda w: pl.BlockSpec((1, tq, w), lambda bi, n: (bi, n, 0))
    prv = lambda w: pl.BlockSpec((1, BLOCK, w), lambda bi, n: (bi, jnp.maximum(n * Q_BLOCKS - 1, 0), 0))
    nxt = lambda w: pl.BlockSpec((1, BLOCK, w), lambda bi, n: (bi, jnp.minimum((n + 1) * Q_BLOCKS, nb - 1), 0))
    cx = pl.BlockSpec((1, l, KV_W), lambda bi, n: (bi, 0, 0))
    return pl.pallas_call(
        _attn_kernel,
        grid=(b, s // tq),
        in_specs=[
            pl.BlockSpec(memory_space=pltpu.SMEM),
            cur(ATT_W),
            prv(KV_W), cur(KV_W), nxt(KV_W), prv(KV_W), cur(KV_W), nxt(KV_W),
            prv(KV_W), cur(KV_W), nxt(KV_W), prv(KV_W), cur(KV_W), nxt(KV_W),
            cx, cx, cx, cx,
            cur(GM_W), cur(GM_W),
            pl.BlockSpec((4, BLOCK, 2 * BLOCK), lambda bi, n: (0, 0, 0)),
            pl.BlockSpec((BLOCK, 8), lambda bi, n: (0, 0)),
            pl.BlockSpec((D, D), lambda bi, n: (0, 0), pipeline_mode=pl.Buffered(1)),
            cur(D),
            pl.BlockSpec((1, 1, 6, D), lambda bi, n: (0, bi, 0, 0)),
        ],
        out_specs=cur(D),
        out_shape=jax.ShapeDtypeStruct((b, s, D), F32),
        scratch_shapes=[pltpu.VMEM((D, D), BF16)],
        compiler_params=_cparams(("arbitrary", "arbitrary"), VMEM_LIMIT),
        name="attn_gmlp_out",
    )(sink, q, k, k, k, ksw, ksw, ksw, v, v, v, vsw, vsw, vsw, kx, ksx, vx, vsx,
      u, vg, wcat_bf, bs_t, wout, x, mod)


def _ffn_kernel(x_ref, mod_ref, wg_ref, wu_ref, wd_ref, o_ref, act):
    mod = mod_ref[0, 0]
    xf = x_ref[...]
    h = _modulate(xf, mod[3:4], mod[4:5]).astype(BF16)
    for c in range(wg_ref.shape[1] // W_CHUNK):
        cs = slice(c * W_CHUNK, (c + 1) * W_CHUNK)
        g = jnp.dot(h, wg_ref[:, cs], preferred_element_type=F32)
        up = jnp.dot(h, wu_ref[:, cs], preferred_element_type=F32)
        act[:, cs] = (g * _sigmoid(g) * up).astype(BF16)
    o_ref[...] = xf + mod[5:6] * jnp.dot(act[...], wd_ref[...], preferred_element_type=F32)


def _dense_ffn(x2d, mod, wg, wu, wd, seq):
    n = x2d.shape[0]
    f = wg.shape[1]
    tm = TM_FFN
    per_b = seq // tm
    resident = lambda shp: pl.BlockSpec(shp, lambda i: (0, 0), pipeline_mode=pl.Buffered(1))
    return pl.pallas_call(
        _ffn_kernel,
        grid=(n // tm,),
        in_specs=[
            pl.BlockSpec((tm, D), lambda i: (i, 0)),
            pl.BlockSpec((1, 1, 6, D), lambda i: (0, i // per_b, 0, 0)),
            resident((D, f)), resident((D, f)), resident((f, D)),
        ],
        out_specs=pl.BlockSpec((tm, D), lambda i: (i, 0)),
        out_shape=jax.ShapeDtypeStruct((n, D), F32),
        scratch_shapes=[pltpu.VMEM((tm, f), BF16)],
        compiler_params=_cparams(("arbitrary",), VMEM_LIMIT),
        name="dense_ffn",
    )(x2d, mod, wg, wu, wd)


def _pool_route_kernel(x_ref, xp_ref, xn_ref, mod_ref, band_ref, pw_ref, psc_ref, wr_hi_ref,
                       tri_ref, x3_ref, h2_ref, route_ref, cnt_ref, hext, carry):
    bi = pl.program_id(0)
    i = pl.program_id(1)
    ni = pl.num_programs(1)
    tm = x_ref.shape[1]
    seq = tm * ni
    mod = mod_ref[0, 0]

    @pl.when((bi == 0) & (i == 0))
    def _():
        carry[...] = jnp.zeros_like(carry)

    xf = x_ref[0]
    hp = _modulate(xp_ref[0], mod[0:1], mod[1:2])
    hn = _modulate(xn_ref[0], mod[0:1], mod[1:2])
    hext[0:POOL_HALO] = jnp.where(i > 0, hp, 0.0).astype(BF16)
    h_main = _modulate(xf, mod[0:1], mod[1:2])
    hext[POOL_HALO:POOL_HALO + tm] = h_main.astype(BF16)
    hext[POOL_HALO + tm:] = jnp.where(i < ni - 1, hn, 0.0).astype(BF16)

    t_local = lax.broadcasted_iota(jnp.int32, (BLOCK, 1), 0)
    ys = []
    for gi, w in enumerate(POOL_SIZES):
        lo_off = -(w // 2)
        hi_off = w - 1 - w // 2
        cols = slice(gi * POOL_GD, (gi + 1) * POOL_GD)
        outs = []
        for sb in range(tm // BLOCK):
            r0 = sb * BLOCK
            win = jnp.dot(band_ref[gi], hext[r0:r0 + BLOCK + 2 * POOL_HALO, cols],
                          preferred_element_type=F32)
            t = i * tm + r0 + t_local
            cnt = (jnp.minimum(t + hi_off, seq - 1) - jnp.maximum(t + lo_off, 0) + 1).astype(F32)
            diff = win / cnt - h_main[r0:r0 + BLOCK, cols]
            outs.append(diff.astype(BF16))
        dg = jnp.concatenate(outs, axis=0)
        ys.append(jnp.dot(dg, pw_ref[gi], preferred_element_type=F32))
    y = jnp.concatenate(ys, axis=1) * psc_ref[...]
    x3 = xf + mod[2:3] * y
    x3_ref[0] = x3

    h2 = _modulate(x3, mod[3:4], mod[4:5])
    _rows_to_slabs(h2, h2_ref)
    logits = jnp.dot(h2.astype(BF16), wr_hi_ref[...], preferred_element_type=F32)
    lane = lax.broadcasted_iota(jnp.int32, (tm, LANES), 1)
    lane_f = lane.astype(F32)
    neg = -jnp.inf
    lg = jnp.where(lane < N_EXPERTS, logits, neg)
    m1 = jnp.max(lg, axis=-1, keepdims=True)
    i1 = jnp.min(jnp.where(lg == m1, lane_f, float(LANES)), axis=-1, keepdims=True)
    oh1 = lane_f == i1
    lg2 = jnp.where(oh1, neg, lg)
    m2 = jnp.max(lg2, axis=-1, keepdims=True)
    i2 = jnp.min(jnp.where(lg2 == m2, lane_f, float(LANES)), axis=-1, keepdims=True)
    oh2 = lane_f == i2
    e = jnp.exp(m2 - m1)
    w1 = 1.0 / (1.0 + e)
    w2 = e / (1.0 + e)
    oh = jnp.where(oh1 | oh2, 1.0, 0.0)
    before = jnp.dot(tri_ref[...], oh.astype(BF16), preferred_element_type=F32) + carry[...]
    r1 = jnp.sum(jnp.where(oh1, before, 0.0), axis=-1, keepdims=True)
    r2 = jnp.sum(jnp.where(oh2, before, 0.0), axis=-1, keepdims=True)
    carry[...] = carry[...] + jnp.sum(oh, axis=0, keepdims=True)
    cnt_ref[...] = carry[...]
    info = jnp.where(lane == 0, i1, jnp.where(lane == 1, i2, jnp.where(lane == 2, w1, jnp.where(
        lane == 3, w2, jnp.where(lane == 4, r1, jnp.where(lane == 5, r2, 0.0))))))
    route_ref[...] = info.T[0:8, :]


def _pool_route(x, mod, band, pw_bf, pool_scale, wr_hi, tri):
    b, s, _ = x.shape
    tm = TM_POOL
    ni = s // tm
    hb = tm // POOL_HALO
    row = pl.BlockSpec((1, tm, D), lambda bi, i: (bi, i, 0))
    const2 = lambda shp: pl.BlockSpec(shp, lambda bi, i: (0,) * len(shp))
    return pl.pallas_call(
        _pool_route_kernel,
        grid=(b, ni),
        in_specs=[
            row,
            pl.BlockSpec((1, POOL_HALO, D), lambda bi, i: (bi, jnp.maximum(i * hb - 1, 0), 0)),
            pl.BlockSpec((1, POOL_HALO, D), lambda bi, i: (bi, jnp.minimum((i + 1) * hb, s // POOL_HALO - 1), 0)),
            pl.BlockSpec((1, 1, 6, D), lambda bi, i: (1, bi, 0, 0)),
            const2(band.shape), const2(pw_bf.shape), const2((1, D)),
            const2(wr_hi.shape), const2(tri.shape),
        ],
        out_specs=[row,
                   pl.BlockSpec((tm * SLAB, LANES), lambda bi, i: (bi * ni + i, 0)),
                   pl.BlockSpec((8, tm), lambda bi, i: (0, bi * ni + i)),
                   pl.BlockSpec((1, LANES), lambda bi, i: (0, 0))],
        out_shape=[jax.ShapeDtypeStruct((b, s, D), F32), jax.ShapeDtypeStruct((b * s * SLAB, LANES), F32),
                   jax.ShapeDtypeStruct((8, b * s), F32), jax.ShapeDtypeStruct((1, LANES), F32)],
        scratch_shapes=[pltpu.VMEM((tm + 2 * POOL_HALO, D), BF16), pltpu.VMEM((1, LANES), F32)],
        compiler_params=_cparams(("arbitrary", "arbitrary"), VMEM_LIMIT),
        name="pool_route",
    )(x, x, x, mod, band, pw_bf, pool_scale.reshape(1, D), wr_hi, tri)


def _slotmap_kernel(pos_ref, lo_ref, hi_ref, o_ref):
    n_pairs = pos_ref.shape[0]
    spare_mask = 2 * TM_MOE - 1
    for e in range(lo_ref.shape[0]):
        def fill(p, c):
            o_ref[p] = n_pairs + (p & spare_mask)
            return c
        lax.fori_loop(lo_ref[e], hi_ref[e], fill, 0)

    def place(f, c):
        o_ref[pos_ref[f]] = f
        return c
    lax.fori_loop(0, n_pairs, place, 0, unroll=32)


def _slot_map(pos_flat, lo, hi, n_slots):
    smem = pl.BlockSpec(memory_space=pltpu.SMEM)
    return pl.pallas_call(
        _slotmap_kernel,
        in_specs=[smem, smem, smem],
        out_specs=smem,
        out_shape=jax.ShapeDtypeStruct((n_slots,), jnp.int32),
        name="moe_slot_map",
    )(pos_flat, lo, hi)


def _moe_kernel(te_ref, nused_ref, half_ref, fnext_ref, fprev_ref, f0_ref, h_hbm, wg_hbm, wu_hbm, wd_hbm,
                y_hbm, wg_res, wu_res, wd_res, stg_in, stg_out, xbuf, xb, act, acc, stage, gsem, ssem, wsem):
    i = pl.program_id(0)
    nt = pl.num_programs(0)
    used_tiles = nused_ref[0]
    tm = xb.shape[0]
    f_dim = wg_res.shape[1]
    tok_mask = h_hbm.shape[0] // SLAB - 1
    tile_rows = tm * SLAB
    out_rows = tm * PACK

    def slab(ix, width=SLAB):
        return pl.ds(pl.multiple_of(ix * width, width), width)

    def gather_row(fref, r, slot, zero=0):
        tok = fref[0, 0, r + zero] & tok_mask
        return pltpu.make_async_copy(h_hbm.at[slab(tok)], xbuf.at[slot, slab(r)], gsem.at[slot])

    def scatter_row(fref, r, slot, zero=0):
        return pltpu.make_async_copy(stage.at[slot, slab(r, PACK)], y_hbm.at[slab(fref[0, 0, r + zero], PACK)],
                                     ssem.at[slot])

    def gather_all(slot):
        return pltpu.make_async_copy(h_hbm.at[pl.ds(0, tile_rows)], xbuf.at[slot], gsem.at[slot])

    def scatter_all(slot):
        return pltpu.make_async_copy(stage.at[slot], y_hbm.at[pl.ds(0, out_rows)], ssem.at[slot])

    cur = i % 2
    used = i < used_tiles

    @pl.when(i == 0)
    def _():
        stage[...] = jnp.zeros_like(stage)
        spare = y_hbm.shape[0] - 2 * out_rows
        fills = [pltpu.make_async_copy(stage.at[sl], y_hbm.at[pl.ds(spare + sl * out_rows, out_rows)],
                                       ssem.at[sl])
                 for sl in range(2)]
        for cp in fills:
            cp.start()
        for cp in fills:
            cp.wait()

    @pl.when((i == 0) & used)
    def _():
        def prime(r, c):
            gather_row(f0_ref, r, 0).start()
            return c
        lax.fori_loop(0, tm, prime, 0)

    expert = te_ref[i]
    new_expert = used & ((i == 0) | (expert != te_ref[jnp.maximum(i - 1, 0)]))

    has_next = i + 1 < used_tiles
    has_prev = (i >= 1) & (i - 1 < used_tiles)
    steady = (i >= 1) & has_next
    half_tile = half_ref[i] == 1

    @pl.when(used)
    def _():
        gather_all(cur).wait()

    @pl.when((i >= 2) & (i - 2 < used_tiles))
    def _():
        scatter_all(cur).wait()

    half = f_dim // MOE_SPLIT
    n_piece = half // W_CHUNK
    n_groups = MOE_SPLIT * (n_piece + 1)

    def tick(v):
        bits = jnp.max(lax.bitcast_convert_type(v[0:SLAB, 0:LANES], jnp.int32))
        return lax.shift_right_logical(lax.shift_right_logical(bits, 16), 16)

    def expert_ffn(issue_group, rows=tm):
        for cix in range(SLAB):
            xb[0:rows, cix * LANES:(cix + 1) * LANES] = (
                xbuf.at[cur][pl.ds(cix, rows, stride=SLAB), :].astype(BF16))
        xv = xb[0:rows]
        out = None
        issue_group(0, 0)
        k = 1
        for hf in range(MOE_SPLIT):
            for c in range(n_piece):
                cs = slice(hf * half + c * W_CHUNK, hf * half + (c + 1) * W_CHUNK)
                g = jnp.dot(xv, wg_res[:, cs], preferred_element_type=F32)
                up = jnp.dot(xv, wu_res[:, cs], preferred_element_type=F32)
                act[0:rows, c * W_CHUNK:(c + 1) * W_CHUNK] = (g * _sigmoid(g) * up).astype(BF16)
                issue_group(k, tick(g))
                k += 1
            part = jnp.dot(act[0:rows], wd_res[hf * half:(hf + 1) * half, :], preferred_element_type=F32)
            if hf < MOE_SPLIT - 1:
                acc[0:rows] = part if out is None else acc[0:rows] + part
                out = acc
                issue_group(k, tick(part))
                k += 1
            else:
                _pack_rows(part if out is None else acc[0:rows] + part, stage.at[cur])

    def chunk_plan():
        n_in = D // W_ROWS_IN
        ins = [(w, res, stg_in, W_RING_IN, 0, W_ROWS_IN, c, wi * n_in + c)
               for wi, (w, res) in enumerate(((wg_hbm, wg_res), (wu_hbm, wu_res))) for c in range(n_in)]
        outs = [(wd_hbm, wd_res, stg_out, W_RING_OUT, W_RING_IN, W_ROWS_OUT, c, c)
                for c in range(f_dim // W_ROWS_OUT)]
        return outs[:W_RING_OUT] + ins + outs[W_RING_OUT:]

    def chunk_copy(entry, ex):
        w_hbm, _, stg, ring, sem0, rows, c, j = entry
        return pltpu.make_async_copy(w_hbm.at[ex, pl.ds(c * rows, rows), :], stg.at[j % ring],
                                     wsem.at[sem0 + j % ring])

    def start_first_chunks(ex):
        for entry in chunk_plan():
            if entry[7] < entry[3]:
                chunk_copy(entry, ex).start()

    @pl.when(new_expert)
    def _():
        plan = chunk_plan()

        @pl.when(i == 0)
        def _():
            start_first_chunks(expert)

        for k, entry in enumerate(plan):
            _, res, stg, ring, _, rows, c, j = entry
            chunk_copy(entry, expert).wait()
            res[c * rows:(c + 1) * rows, :] = stg[j % ring].astype(BF16)
            later = [e for e in plan[k + 1:] if e[2] is stg and e[7] == j + ring]
            if later:
                chunk_copy(later[0], expert).start()

    next_expert = te_ref[jnp.minimum(i + 1, nt - 1)]

    @pl.when(has_next & (next_expert != expert))
    def _():
        start_first_chunks(next_expert)

    def issue_group(k, zero):
        for r in range(k * tm // n_groups, (k + 1) * tm // n_groups):
            gather_row(fnext_ref, r, 1 - cur, zero).start()
            scatter_row(fprev_ref, r, 1 - cur, zero).start()

    @pl.when(steady & jnp.logical_not(half_tile))
    def _():
        expert_ffn(issue_group)

    @pl.when(steady & half_tile)
    def _():
        expert_ffn(issue_group, tm // 2)

    @pl.when(jnp.logical_not(steady))
    def _():
        @pl.when(used)
        def _():
            expert_ffn(lambda k, zero: None)

        @pl.when(has_next)
        def _():
            def issue(r, c):
                gather_row(fnext_ref, r, 1 - cur).start()
                return c
            lax.fori_loop(0, tm, issue, 0)

        @pl.when(has_prev)
        def _():
            def issue(r, c):
                scatter_row(fprev_ref, r, 1 - cur).start()
                return c
            lax.fori_loop(0, tm, issue, 0)

    @pl.when((i == nt - 1) & (nt - 2 < used_tiles))
    def _():
        scatter_all(1 - cur).wait()


def _moe_experts(h_slabs, fmap, tile_expert, n_used, tile_half, wg, wu, wd, n_tiles, y_rows):
    f = wg.shape[2]
    tm = TM_MOE
    fblk = lambda imap: pl.BlockSpec((1, 1, tm), imap, memory_space=pltpu.SMEM)
    hbm = pl.BlockSpec(memory_space=pl.ANY)
    return pl.pallas_call(
        _moe_kernel,
        grid_spec=pltpu.PrefetchScalarGridSpec(
            num_scalar_prefetch=3,
            grid=(n_tiles,),
            in_specs=[
                fblk(lambda i, te, nu, hf: (jnp.minimum(i + 1, n_tiles - 1), 0, 0)),
                fblk(lambda i, te, nu, hf: (jnp.maximum(i - 1, 0), 0, 0)),
                fblk(lambda i, te, nu, hf: (0, 0, 0)),
                hbm, hbm, hbm, hbm,
            ],
            out_specs=hbm,
            scratch_shapes=[pltpu.VMEM((D, f), BF16), pltpu.VMEM((D, f), BF16), pltpu.VMEM((f, D), BF16),
                            pltpu.VMEM((W_RING_IN, W_ROWS_IN, f), F32), pltpu.VMEM((W_RING_OUT, W_ROWS_OUT, D), F32),
                            pltpu.VMEM((2, tm * SLAB, LANES), F32), pltpu.VMEM((tm, D), BF16),
                            pltpu.VMEM((tm, f // MOE_SPLIT), BF16), pltpu.VMEM((tm, D), F32),
                            pltpu.VMEM((2, tm * PACK, LANES), jnp.uint32),
                            pltpu.SemaphoreType.DMA((2,)), pltpu.SemaphoreType.DMA((2,)),
                            pltpu.SemaphoreType.DMA((W_RING_IN + W_RING_OUT,))],
        ),
        out_shape=jax.ShapeDtypeStruct((y_rows * PACK, LANES), jnp.uint32),
        compiler_params=_cparams(("arbitrary",), MOE_VMEM_LIMIT),
        name="moe_experts",
    )(tile_expert, n_used, tile_half, fmap, fmap, fmap, h_slabs, wg, wu, wd)


def _combine_kernel(y1_ref, y2_ref, x_ref, w_ref, mod_ref, gain_ref, o_ref):
    rows = x_ref.shape[0]
    info = jnp.concatenate([w_ref[...], jnp.zeros((LANES - w_ref.shape[0], rows), F32)], axis=0).T
    moe = info[:, 2:3] * _unpack_rows(y1_ref, rows) + info[:, 3:4] * _unpack_rows(y2_ref, rows)
    mod = mod_ref[0, 0]
    x4 = x_ref[...] + mod[5:6] * moe
    ms = jnp.mean(x4 * x4, axis=-1, keepdims=True)
    o_ref[...] = x4 * lax.rsqrt(ms + EPS) * gain_ref[...]


def _combine(y, x3_2d, wts, mod, final_gain, seq):
    n = x3_2d.shape[0]
    tc = TC_COMB
    nt = n // tc
    per_b = seq // tc
    return pl.pallas_call(
        _combine_kernel,
        grid=(nt,),
        in_specs=[
            pl.BlockSpec((tc * PACK, LANES), lambda i: (i, 0)),
            pl.BlockSpec((tc * PACK, LANES), lambda i: (i + nt, 0)),
            pl.BlockSpec((tc, D), lambda i: (i, 0)),
            pl.BlockSpec((8, tc), lambda i: (0, i)),
            pl.BlockSpec((1, 1, 6, D), lambda i: (1, i // per_b, 0, 0)),
            pl.BlockSpec((1, D), lambda i: (0, 0)),
        ],
        out_specs=pl.BlockSpec((tc, D), lambda i: (i, 0)),
        out_shape=jax.ShapeDtypeStruct((n, D), F32),
        compiler_params=_cparams(("arbitrary",)),
        name="moe_combine",
    )(y, y, x3_2d, wts, mod, final_gain.reshape(1, D))


def _rope_tables(seq):
    rows = seq // GRID_W
    row_pos = jnp.repeat(jnp.arange(rows, dtype=F32), GRID_W)
    col_pos = jnp.tile(jnp.arange(GRID_W, dtype=F32), rows)
    axis_dim = HEAD_DIM // 2
    inv_freq = ROPE_BASE ** (-jnp.arange(0, axis_dim, 2, dtype=F32) / axis_dim)
    ar = row_pos[:, None] * inv_freq
    ac = col_pos[:, None] * inv_freq
    cos64 = jnp.concatenate([jnp.cos(ar), jnp.cos(ar), jnp.cos(ac), jnp.cos(ac)], axis=1)
    sin64 = jnp.concatenate([-jnp.sin(ar), jnp.sin(ar), -jnp.sin(ac), jnp.sin(ac)], axis=1)
    return jnp.tile(cos64, (1, 2)), jnp.tile(sin64, (1, 2))


def _band_matrices():
    r = np.arange(BLOCK)[:, None]
    c = np.arange(BLOCK + 2 * POOL_HALO)[None, :] - POOL_HALO
    mats = []
    for w in POOL_SIZES:
        lo = -(w // 2)
        hi = w - 1 - w // 2
        mats.append(((c >= r + lo) & (c <= r + hi)).astype(np.float32))
    return jnp.asarray(np.stack(mats), dtype=BF16)


def kernel(x, c, ctx, c_ctx, w_ada, b_ada, w_in, attn_sink, gm_gain, gm_w_s, gm_b_s, w_out,
           ffn_w_gate, ffn_w_up, ffn_w_down, pool_w, pool_scale, router_w,
           moe_w_gate, moe_w_up, moe_w_down, final_gain):
    b, s, _ = x.shape
    n = b * s
    assert w_ada.shape[0] == 2 and w_in.shape[0] == 1 and pool_w.shape[0] == 1
    assert s % TM_IN == 0 and s % TM_POOL == 0 and s % TM_FFN == 0 and b <= 4
    assert n & (n - 1) == 0

    cvec = jnp.concatenate([c, c_ctx[None, :], jnp.zeros((8 - b - 1, D), F32)], axis=0)
    mod = _ada_mod(cvec, w_ada, b_ada)

    cos_t, sin_t = _rope_tables(s)
    q, k, ksw, v, vsw, u, vg = _in_proj(x, mod, w_in[0], gm_gain[0], cos_t, sin_t)
    kx, ksx, vx, vsx = _ctx_kv(ctx, mod, w_in[0])
    wcat = gm_w_s[0].reshape(4, 2, BLOCK, BLOCK).transpose(0, 2, 1, 3).reshape(4, BLOCK, 2 * BLOCK).astype(BF16)
    x1 = _attn_mixer(x, mod, attn_sink[0], q, k, ksw, v, vsw, kx, ksx, vx, vsx, u, vg,
                     wcat, gm_b_s[0].T, w_out[0])
    x2 = _dense_ffn(x1.reshape(n, D), mod, ffn_w_gate[0].astype(BF16), ffn_w_up[0].astype(BF16),
                    ffn_w_down[0].astype(BF16), s)

    wr = jnp.pad(router_w[0], ((0, 0), (0, LANES - N_EXPERTS)))
    wr_hi = wr.astype(BF16)
    tri = jnp.asarray(np.tril(np.ones((TM_POOL, TM_POOL), np.float32), -1), dtype=BF16)
    x3, h2, route, counts = _pool_route(x2.reshape(b, s, D), mod, _band_matrices(), pool_w[0].astype(BF16),
                                        pool_scale[0], wr_hi, tri)

    tm = TM_MOE
    n_tiles = (2 * n) // tm + N_EXPERTS
    cnt = counts[0, :N_EXPERTS].astype(jnp.int32)
    tiles_e = (cnt + tm - 1) // tm
    tile_end = jnp.cumsum(tiles_e)
    off = (tile_end - tiles_e) * tm
    n_used = tile_end[-1]
    tix = jnp.arange(n_tiles, dtype=jnp.int32)
    te = jnp.minimum(jnp.sum(tix[:, None] >= tile_end[None, :], axis=1), N_EXPERTS - 1).astype(jnp.int32)
    te_last = te[jnp.maximum(n_used - 1, 0)]
    tile_expert = jnp.where(tix < n_used, te, te_last)
    rows_left = cnt[te] - (tix - (tile_end - tiles_e)[te]) * tm
    tile_half = ((tix < n_used) & (rows_left <= tm // 2)).astype(jnp.int32)
    e1 = route[0].astype(jnp.int32)
    e2 = route[1].astype(jnp.int32)
    pos1 = off[e1] + route[4].astype(jnp.int32)
    pos2 = off[e2] + route[5].astype(jnp.int32)
    n_slots = n_tiles * tm
    pad_lo = jnp.concatenate([off + cnt, (n_used * tm).reshape(1)]).astype(jnp.int32)
    pad_hi = jnp.concatenate([off + tiles_e * tm, jnp.full((1,), n_slots, jnp.int32)]).astype(jnp.int32)
    fmap = _slot_map(jnp.concatenate([pos1, pos2]), pad_lo, pad_hi, n_slots)
    n_used_arr = n_used.reshape(1).astype(jnp.int32)

    y = _moe_experts(h2, fmap.reshape(n_tiles, 1, tm), tile_expert, n_used_arr, tile_half,
                     moe_w_gate[0], moe_w_up[0], moe_w_down[0], n_tiles, 2 * n + 2 * tm)
    out = _combine(y, x3.reshape(n, D), route, mod, final_gain, s)
    return out.reshape(b, s, D)
```
